```python
import math
import jax, jax.numpy as jnp
from jax import lax
import numpy as np

D_MODEL = 1024
BATCH = 16
SEQ = 256
DEPTH = 2
DEC_BATCH = 4
DEC_SEQ = 2048
PAST_LEN = 256

GRID_W = 64
MIX_WIDTH = D_MODEL
HALF = MIX_WIDTH // 2
POOL_WINDOWS = (2, 4, 8, 16)
POOL_GROUP = HALF // len(POOL_WINDOWS)
CONV_WIDTH = 3
HEAD_DIM = 64
N_HEADS_NA = HALF // HEAD_DIM
WIN_H = 8
WIN_W = 16
Q_COLS = 16
K_COLS = Q_COLS + WIN_W
CTX_Q_BLOCK = 128
SSM_GROUP = 16
N_SSM_GROUPS = HALF // SSM_GROUP
SSM_STATE = 64
D_FF = 4 * D_MODEL
N_EVEN = (DEPTH + 1) // 2
N_ODD = DEPTH // 2
EPS = 1e-6

kernel_name = "hybrid_dit_pool_conv_natten_s5_step"


def rmsnorm(x, g):
    xf = x.astype(jnp.float32)
    y = xf * lax.rsqrt(jnp.mean(xf * xf, axis=-1, keepdims=True) + EPS)
    return (y * g.astype(jnp.float32)).astype(x.dtype)


def adaln(cond, w, b):
    m = jax.nn.silu(cond) @ w + b
    return m.reshape(cond.shape[:-1] + (1, 6, D_MODEL))


def modulate(h, shift, scale):
    return h * (1 + scale) + shift


def sqrelu_mlp(h, w1, w2):
    return jnp.square(jax.nn.relu(h @ w1)) @ w2


def to_heads(z):
    b, t, _ = z.shape
    return z.reshape(b, t, N_HEADS_NA, HEAD_DIM).transpose(0, 2, 1, 3)


def from_heads(z):
    b, h, t, d = z.shape
    return z.transpose(0, 2, 1, 3).reshape(b, t, h * d)


def pool_mixer(u, w_grp, scale):
    b, t, _ = u.shape
    uf = u.astype(jnp.float32)
    cs = jnp.concatenate([jnp.zeros((b, 1, HALF), jnp.float32), jnp.cumsum(uf, axis=1)], axis=1)
    pos = jnp.arange(t)
    outs = []
    for gi, w in enumerate(POOL_WINDOWS):
        lo = jnp.maximum(pos - w // 2, 0)
        hi = jnp.minimum(pos + (w - w // 2), t)
        sl = slice(gi * POOL_GROUP, (gi + 1) * POOL_GROUP)
        cnt = (hi - lo).astype(jnp.float32)[None, :, None]
        outs.append((cs[:, hi, sl] - cs[:, lo, sl]) / cnt - uf[:, :, sl])
    pooled = jnp.stack(outs, axis=2).astype(u.dtype)
    mixed = jnp.einsum('btgc,gcd->btgd', pooled, w_grp)
    return mixed.reshape(b, t, HALF) * scale


def short_conv_mixer(b_gate, c_gate, v_in, conv_w):
    z = c_gate * v_in
    t = z.shape[1]
    pad = (CONV_WIDTH - 1) // 2
    zp = jnp.pad(z, ((0, 0), (pad, CONV_WIDTH - 1 - pad), (0, 0)))
    conv = sum(conv_w[j] * zp[:, j:j + t] for j in range(CONV_WIDTH))
    return b_gate * conv


def even_mixer(h, w_in, pool_w, pool_scale, conv_w, w_out):
    proj = h @ w_in
    a_in, b_gate, c_gate, v_in = jnp.split(proj, 4, axis=-1)
    ya = pool_mixer(a_in, pool_w, pool_scale)
    yb = short_conv_mixer(b_gate, c_gate, v_in, conv_w)
    return jnp.concatenate([ya, yb], axis=-1) @ w_out


def context_attention(q, k, v):
    b, h, l, dh = q.shape
    scale = HEAD_DIM ** -0.5
    qb = q.reshape(b, h, l // CTX_Q_BLOCK, CTX_Q_BLOCK, dh).transpose(2, 0, 1, 3, 4)

    def one_block(qi):
        s = jnp.einsum('bhqd,bhkd->bhqk', qi, k).astype(jnp.float32) * scale
        p = jax.nn.softmax(s, axis=-1).astype(v.dtype)
        return jnp.einsum('bhqk,bhkd->bhqd', p, v)

    o = lax.map(one_block, qb)
    return o.transpose(1, 2, 0, 3, 4).reshape(b, h, l, dh)


def neighbourhood_attention(q, k, v, k_ctx, v_ctx, rpb):
    b, h, t, dh = q.shape
    rows = t // GRID_W
    kh = min(WIN_H, rows)
    ncb = GRID_W // Q_COLS
    scale = HEAD_DIM ** -0.5
    r = jnp.arange(rows)
    r0 = jnp.clip(r - kh // 2, 0, rows - kh)
    key_rows = r0[:, None] + jnp.arange(kh)[None, :]
    cb = jnp.arange(ncb)
    s0 = jnp.clip(cb * Q_COLS - WIN_W // 2, 0, GRID_W - K_COLS)
    key_cols = s0[:, None] + jnp.arange(K_COLS)[None, :]
    qcol = cb[:, None] * Q_COLS + jnp.arange(Q_COLS)[None, :]
    c0 = jnp.clip(qcol - WIN_W // 2, 0, GRID_W - WIN_W)
    kc = key_cols[:, None, :]
    col_valid = (kc >= c0[:, :, None]) & (kc < c0[:, :, None] + WIN_W)
    col_off = jnp.clip(kc - qcol[:, :, None], -(WIN_W - 1), WIN_W - 1)
    row_off = key_rows - r[:, None]
    bias = rpb.astype(jnp.float32)[:, row_off[:, None, None, :, None] + (WIN_H - 1),
                                   col_off[None, :, :, None, :] + (WIN_W - 1)]
    bias = jnp.where(col_valid[None, None, :, :, None, :], bias, -jnp.inf)

    qg = q.reshape(b, h, rows, ncb, Q_COLS, dh)
    kg = k.reshape(b, h, rows, GRID_W, dh)
    vg = v.reshape(b, h, rows, GRID_W, dh)
    ridx = key_rows[:, None, :, None]
    cidx = key_cols[None, :, None, :]
    k_blk = kg[:, :, ridx, cidx]
    v_blk = vg[:, :, ridx, cidx]

    s_loc = jnp.einsum('bhrcqd,bhrcyxd->bhrcqyx', qg, k_blk).astype(jnp.float32) * scale + bias
    s_ctx = jnp.einsum('bhrcqd,bhld->bhrcql', qg, k_ctx).astype(jnp.float32) * scale
    n_loc = kh * K_COLS
    s = jnp.concatenate([s_loc.reshape(b, h, rows, ncb, Q_COLS, n_loc), s_ctx], axis=-1)
    p = jax.nn.softmax(s, axis=-1).astype(v.dtype)
    p_loc = p[..., :n_loc].reshape(b, h, rows, ncb, Q_COLS, kh, K_COLS)
    out = (jnp.einsum('bhrcqyx,bhrcyxd->bhrcqd', p_loc, v_blk)
           + jnp.einsum('bhrcql,bhld->bhrcqd', p[..., n_loc:], v_ctx))
    return out.reshape(b, h, t, dh)


def _linear_recurrence(e1, e2):
    a1, b1 = e1
    a2, b2 = e2
    return a2 * a1, a2 * b1 + b2


def s5_bidirectional(u, s_re, s_im, lam_re, lam_im, log_step, b_re, b_im, c_re, c_im, d, glu_w, glu_b):
    bsz, t, _ = u.shape
    f32 = jnp.float32
    uc = u.astype(f32).reshape(bsz, t, N_SSM_GROUPS, SSM_GROUP).astype(jnp.complex64)
    y = d.astype(f32) * u.astype(f32)
    fin_re, fin_im = [], []
    for direction in range(2):
        reverse = direction == 1
        lam = lax.complex(lam_re[direction].astype(f32), lam_im[direction].astype(f32))
        step = jnp.exp(log_step[direction].astype(f32))[:, None]
        lam_bar = jnp.exp(lam * step)
        b_mat = lax.complex(b_re[direction].astype(f32), b_im[direction].astype(f32))
        b_bar = ((lam_bar - 1.0) / lam)[..., None] * b_mat
        c_mat = lax.complex(c_re[direction].astype(f32), c_im[direction].astype(f32))
        s0 = lax.complex(s_re[:, direction].astype(f32), s_im[:, direction].astype(f32))
        bu = jnp.einsum('npg,btng->btnp', b_bar, uc)
        first, last = (t - 1, 0) if reverse else (0, t - 1)
        bu = bu.at[:, first].add(lam_bar * s0)
        a = jnp.broadcast_to(lam_bar, bu.shape)
        _, states = lax.associative_scan(_linear_recurrence, (a, bu), axis=1, reverse=reverse)
        y = y + jnp.einsum('ngp,btnp->btng', c_mat, states).real.reshape(bsz, t, HALF)
        fin = states[:, last]
        fin_re.append(fin.real)
        fin_im.append(fin.imag)
    y = jax.nn.gelu(y).astype(u.dtype)
    y = y * jax.nn.sigmoid(y @ glu_w + glu_b)
    return y, jnp.stack(fin_re, axis=1), jnp.stack(fin_im, axis=1)


def setup_inputs(seed: int = 0) -> dict:
    key = jax.random.key(seed)
    ks = iter(jax.random.split(key, 40))
    nrm = lambda shape, s=1.0: s * jax.random.normal(next(ks), shape, jnp.float32)
    n = jnp.arange(SSM_STATE, dtype=jnp.float32)
    lam_im = jnp.broadcast_to(math.pi * n, (N_ODD, 2, N_SSM_GROUPS, SSM_STATE))
    return {
        "x_prompt": nrm((BATCH, SEQ, D_MODEL)),
        "x_sample": nrm((DEC_BATCH, DEC_SEQ, D_MODEL)),
        "c": nrm((DEC_BATCH, D_MODEL)),
        "cache_k": nrm((DEC_BATCH, N_ODD, N_HEADS_NA, PAST_LEN, HEAD_DIM)),
        "cache_v": nrm((DEC_BATCH, N_ODD, N_HEADS_NA, PAST_LEN, HEAD_DIM)),
        "state_s5_re": nrm((DEC_BATCH, N_ODD, 2, N_SSM_GROUPS, SSM_STATE), 0.1),
        "state_s5_im": nrm((DEC_BATCH, N_ODD, 2, N_SSM_GROUPS, SSM_STATE), 0.1),
        "c_ctx": nrm((D_MODEL,)),
        "norm1_g": 1.0 + nrm((DEPTH, D_MODEL), 0.02),
        "norm2_g": 1.0 + nrm((DEPTH, D_MODEL), 0.02),
        "ada_w": nrm((DEPTH, D_MODEL, 6 * D_MODEL), 0.5 * D_MODEL ** -0.5),
        "ada_b": nrm((DEPTH, 6 * D_MODEL), 0.02),
        "mlp_w1": nrm((DEPTH, D_MODEL, D_FF), D_MODEL ** -0.5),
        "mlp_w2": nrm((DEPTH, D_FF, D_MODEL), D_FF ** -0.5),
        "ab_w_in": nrm((N_EVEN, D_MODEL, 4 * HALF), D_MODEL ** -0.5),
        "pool_w": nrm((N_EVEN, len(POOL_WINDOWS), POOL_GROUP, POOL_GROUP), POOL_GROUP ** -0.5),
        "pool_scale": 1.0 + nrm((N_EVEN, HALF), 0.05),
        "conv_w": nrm((N_EVEN, CONV_WIDTH, HALF), CONV_WIDTH ** -0.5),
        "ab_w_out": nrm((N_EVEN, MIX_WIDTH, D_MODEL), MIX_WIDTH ** -0.5),
        "cd_w_in": nrm((N_ODD, D_MODEL, 4 * HALF), D_MODEL ** -0.5),
        "na_rpb": nrm((N_ODD, N_HEADS_NA, 2 * WIN_H - 1, 2 * WIN_W - 1), 0.1),
        "ssm_lambda_re": -0.5 + nrm((N_ODD, 2, N_SSM_GROUPS, SSM_STATE), 0.01),
        "ssm_lambda_im": lam_im + nrm((N_ODD, 2, N_SSM_GROUPS, SSM_STATE), 0.01),
        "ssm_log_step": jax.random.uniform(next(ks), (N_ODD, 2, N_SSM_GROUPS), jnp.float32,
                                           math.log(1e-3), math.log(1e-1)),
        "ssm_b_re": nrm((N_ODD, 2, N_SSM_GROUPS, SSM_STATE, SSM_GROUP), (2 * SSM_GROUP) ** -0.5),
        "ssm_b_im": nrm((N_ODD, 2, N_SSM_GROUPS, SSM_STATE, SSM_GROUP), (2 * SSM_GROUP) ** -0.5),
        "ssm_c_re": nrm((N_ODD, 2, N_SSM_GROUPS, SSM_GROUP, SSM_STATE), (2 * SSM_STATE) ** -0.5),
        "ssm_c_im": nrm((N_ODD, 2, N_SSM_GROUPS, SSM_GROUP, SSM_STATE), (2 * SSM_STATE) ** -0.5),
        "ssm_d": nrm((N_ODD, HALF)),
        "glu_w": nrm((N_ODD, HALF, HALF), HALF ** -0.5),
        "glu_b": nrm((N_ODD, HALF), 0.02),
        "cd_w_out": nrm((N_ODD, MIX_WIDTH, D_MODEL), MIX_WIDTH ** -0.5),
        "final_g": 1.0 + nrm((D_MODEL,), 0.02),
    }


def reference(x_prompt, x_sample, c, cache_k, cache_v, state_s5_re, state_s5_im, c_ctx,
              norm1_g, norm2_g, ada_w, ada_b, mlp_w1, mlp_w2,
              ab_w_in, pool_w, pool_scale, conv_w, ab_w_out,
              cd_w_in, na_rpb, ssm_lambda_re, ssm_lambda_im, ssm_log_step,
              ssm_b_re, ssm_b_im, ssm_c_re, ssm_c_im, ssm_d, glu_w, glu_b, cd_w_out,
              final_g):
    xp, xs = x_prompt, x_sample
    new_k, new_v, new_re, new_im = [], [], [], []
    for layer in range(DEPTH):
        mod_p = adaln(c_ctx, ada_w[layer], ada_b[layer])
        mod_s = adaln(c, ada_w[layer], ada_b[layer])
        hp = modulate(rmsnorm(xp, norm1_g[layer]), mod_p[..., 0, :], mod_p[..., 1, :])
        hs = modulate(rmsnorm(xs, norm1_g[layer]), mod_s[..., 0, :], mod_s[..., 1, :])
        i = layer // 2
        if layer % 2 == 0:
            yp = even_mixer(hp, ab_w_in[i], pool_w[i], pool_scale[i], conv_w[i], ab_w_out[i])
            ys = even_mixer(hs, ab_w_in[i], pool_w[i], pool_scale[i], conv_w[i], ab_w_out[i])
        else:
            ssm_args = (ssm_lambda_re[i], ssm_lambda_im[i], ssm_log_step[i], ssm_b_re[i], ssm_b_im[i],
                        ssm_c_re[i], ssm_c_im[i], ssm_d[i], glu_w[i], glu_b[i])
            q_p, k_p, v_p, u_p = jnp.split(hp @ cd_w_in[i], 4, axis=-1)
            k_ph, v_ph = to_heads(k_p), to_heads(v_p)
            o_p = from_heads(context_attention(to_heads(q_p), k_ph, v_ph))
            zero_state = jnp.zeros((xp.shape[0], 2, N_SSM_GROUPS, SSM_STATE), jnp.float32)
            d_p, fin_re, fin_im = s5_bidirectional(u_p, zero_state, zero_state, *ssm_args)
            yp = jnp.concatenate([o_p, d_p], axis=-1) @ cd_w_out[i]
            new_k.append(k_ph)
            new_v.append(v_ph)
            new_re.append(fin_re)
            new_im.append(fin_im)
            q_s, k_s, v_s, u_s = jnp.split(hs @ cd_w_in[i], 4, axis=-1)
            o_s = from_heads(neighbourhood_attention(to_heads(q_s), to_heads(k_s), to_heads(v_s),
                                                     cache_k[:, i], cache_v[:, i], na_rpb[i]))
            d_s, _, _ = s5_bidirectional(u_s, state_s5_re[:, i], state_s5_im[:, i], *ssm_args)
            ys = jnp.concatenate([o_s, d_s], axis=-1) @ cd_w_out[i]
        xp = xp + mod_p[..., 2, :] * yp
        xs = xs + mod_s[..., 2, :] * ys
        hp = modulate(rmsnorm(xp, norm2_g[layer]), mod_p[..., 3, :], mod_p[..., 4, :])
        hs = modulate(rmsnorm(xs, norm2_g[layer]), mod_s[..., 3, :], mod_s[..., 4, :])
        xp = xp + mod_p[..., 5, :] * sqrelu_mlp(hp, mlp_w1[layer], mlp_w2[layer])
        xs = xs + mod_s[..., 5, :] * sqrelu_mlp(hs, mlp_w1[layer], mlp_w2[layer])
    y_prompt = rmsnorm(xp, final_g)
    y_sample = rmsnorm(xs, final_g)
    new_cache_k = jnp.stack(new_k, axis=1)
    new_cache_v = jnp.stack(new_v, axis=1)
    new_state_re = jnp.stack(new_re, axis=1)
    new_state_im = jnp.stack(new_im, axis=1)
    return (y_prompt, y_sample, new_cache_k, new_cache_v, new_state_re, new_state_im)
```

```python
import functools
import math

import jax
import jax.numpy as jnp
from jax import lax
from jax.experimental import pallas as pl
from jax.experimental.pallas import tpu as pltpu

F32 = jnp.float32
BF16 = jnp.bfloat16

D_MODEL = 1024
BATCH = 16
SEQ = 256
DEC_BATCH = 4
DEC_SEQ = 2048
PAST_LEN = 256
GRID_W = 64
GRID_ROWS = DEC_SEQ // GRID_W
HALF = 512
POOL_WINDOWS = (2, 4, 8, 16)
POOL_GROUP = 128
HEAD_DIM = 64
N_HEADS = 8
WIN_H = 8
WIN_W = 16
SSM_GROUP = 16
N_SSM_GROUPS = 32
SSM_STATE = 64
D_FF = 4096
EPS = 1e-6

N_PROMPT_TOK = BATCH * SEQ
N_SAMPLE_TOK = DEC_BATCH * DEC_SEQ
N_TOK = N_PROMPT_TOK + N_SAMPLE_TOK
N_COND = 8

S5_CHUNK = 16
S5_PAIRS = N_SSM_GROUPS // 2
S5_CH = 2 * SSM_GROUP
S5_P = 2 * SSM_STATE
S5_W = S5_CHUNK * S5_CH
S5_BP_SAMPLE = 8

VMEM_LIMIT = 56 * 1024 * 1024


def _cparams(sem):
    return pltpu.CompilerParams(dimension_semantics=sem, vmem_limit_bytes=VMEM_LIMIT)


def _cond_row(i, tm):
    npt = N_PROMPT_TOK // tm
    per = DEC_SEQ // tm
    return jnp.where(i < npt, 0, 1 + (i - npt) // per)


def _normmod(x, g, shift, scale):
    ms = jnp.mean(x * x, axis=-1, keepdims=True)
    return (x * lax.rsqrt(ms + EPS) * g) * (1.0 + scale) + shift


def _dot(a, b):
    return jnp.dot(a, b, preferred_element_type=F32)


def _dot_nt(a, b, precision=None):
    return lax.dot_general(a, b, (((1,), (1,)), ((), ())), preferred_element_type=F32, precision=precision)


def _adaln_body(c_ref, w_ref, b_ref, o_ref):
    c = c_ref[...]
    s = c * jax.nn.sigmoid(c)
    o_ref[...] = _dot(s.astype(BF16), w_ref[...].astype(BF16)) + b_ref[...]


def _adaln(cond, ada_w, ada_b):
    depth = ada_w.shape[0]
    nj = 6
    return pl.pallas_call(
        _adaln_body,
        grid=(depth, nj),
        in_specs=[
            pl.BlockSpec((N_COND, D_MODEL), lambda l, j: (0, 0)),
            pl.BlockSpec((None, D_MODEL, D_MODEL), lambda l, j: (l, 0, j)),
            pl.BlockSpec((None, 1, D_MODEL), lambda l, j: (l, 0, j)),
        ],
        out_specs=pl.BlockSpec((None, N_COND, D_MODEL), lambda l, j: (l, 0, j)),
        out_shape=jax.ShapeDtypeStruct((depth, N_COND, 6 * D_MODEL), F32),
        compiler_params=_cparams(("arbitrary", "arbitrary")),
        name="adaln",
    )(cond, ada_w, ada_b.reshape(depth, 1, 6 * D_MODEL))


TM_PROJ = 512


def _inproj_body(x_ref, m_ref, g_ref, w_ref, o_ref):
    m = m_ref[...]
    h = _normmod(x_ref[...], g_ref[...], m[0:1], m[1:2])
    o_ref[...] = _dot(h.astype(BF16), w_ref[...]).astype(BF16)


def _inproj(x, mods, layer, g, w):
    tm = TM_PROJ
    n_out = w.shape[1]
    return pl.pallas_call(
        _inproj_body,
        grid=(N_TOK // tm,),
        in_specs=[
            pl.BlockSpec((tm, D_MODEL), lambda i: (i, 0)),
            pl.BlockSpec((None, None, 6, D_MODEL), lambda i: (layer, _cond_row(i, tm), 0, 0)),
            pl.BlockSpec((1, D_MODEL), lambda i: (0, 0)),
            pl.BlockSpec((D_MODEL, n_out), lambda i: (0, 0)),
        ],
        out_specs=pl.BlockSpec((tm, n_out), lambda i: (i, 0)),
        out_shape=jax.ShapeDtypeStruct((N_TOK, n_out), BF16),
        compiler_params=_cparams(("parallel",)),
        name="inproj",
    )(x, mods, g.reshape(1, D_MODEL), w)


TM_MIX = 256
HALO = 16


def _even_mixer_body(p_ref, prev_ref, next_ref, pw_ref, ps_ref, cw_ref, y_ref, pad_ref):
    i = pl.program_id(0)
    n_prompt_tiles = N_PROMPT_TOK // TM_MIX
    tiles_per_seq = DEC_SEQ // TM_MIX
    is_prompt = i < n_prompt_tiles
    tile_in_seq = jnp.where(is_prompt, 0, (i - n_prompt_tiles) % tiles_per_seq)
    seq_len = jnp.where(is_prompt, SEQ, DEC_SEQ)
    has_prev = tile_in_seq > 0
    has_next = jnp.logical_and(jnp.logical_not(is_prompt), tile_in_seq < tiles_per_seq - 1)
    prev_keep = jnp.where(has_prev, 1.0, 0.0).astype(F32)
    next_keep = jnp.where(has_next, 1.0, 0.0).astype(F32)
    pos = tile_in_seq * TM_MIX + lax.broadcasted_iota(jnp.int32, (TM_MIX, 1), 0)
    lo_rows, hi_rows = HALO, HALO + TM_MIX

    pad_ref[0:HALO, :] = prev_ref[:, 0:HALF].astype(F32) * prev_keep
    pad_ref[lo_rows:hi_rows, :] = p_ref[:, 0:HALF].astype(F32)
    pad_ref[hi_rows:hi_rows + HALO, :] = next_ref[:, 0:HALF].astype(F32) * next_keep
    for gi, w in enumerate(POOL_WINDOWS):
        cs = slice(gi * POOL_GROUP, (gi + 1) * POOL_GROUP)
        s = None
        for o in range(-(w // 2), w - w // 2):
            t = pad_ref[lo_rows + o:hi_rows + o, cs]
            s = t if s is None else s + t
        lo = jnp.maximum(pos - w // 2, 0)
        hi = jnp.minimum(pos + (w - w // 2), seq_len)
        inv = 1.0 / (hi - lo).astype(F32)
        pooled = s * inv - pad_ref[lo_rows:hi_rows, cs]
        mixed = _dot(pooled.astype(BF16), pw_ref[gi]) * ps_ref[:, cs]
        y_ref[:, cs] = mixed.astype(BF16)

    c0, v0 = 2 * HALF, 3 * HALF
    pad_ref[0:HALO, :] = (prev_ref[:, c0:c0 + HALF].astype(F32) * prev_ref[:, v0:v0 + HALF].astype(F32)) * prev_keep
    pad_ref[lo_rows:hi_rows, :] = p_ref[:, c0:c0 + HALF].astype(F32) * p_ref[:, v0:v0 + HALF].astype(F32)
    pad_ref[hi_rows:hi_rows + HALO, :] = (next_ref[:, c0:c0 + HALF].astype(F32)
                                          * next_ref[:, v0:v0 + HALF].astype(F32)) * next_keep
    for ci in range(HALF // 128):
        cs = slice(ci * 128, (ci + 1) * 128)
        conv = (cw_ref[0:1, cs] * pad_ref[lo_rows - 1:hi_rows - 1, cs]
                + cw_ref[1:2, cs] * pad_ref[lo_rows:hi_rows, cs]
                + cw_ref[2:3, cs] * pad_ref[lo_rows + 1:hi_rows + 1, cs])
        bg = p_ref[:, HALF + ci * 128:HALF + (ci + 1) * 128].astype(F32)
        y_ref[:, HALF + ci * 128:HALF + (ci + 1) * 128] = (bg * conv).astype(BF16)


def _even_mixer(proj, pool_w, pool_scale, conv_w):
    n_tiles = N_TOK // TM_MIX
    per = TM_MIX // HALO
    n_halo_blocks = N_TOK // HALO
    return pl.pallas_call(
        _even_mixer_body,
        grid=(n_tiles,),
        in_specs=[
            pl.BlockSpec((TM_MIX, 4 * HALF), lambda i: (i, 0)),
            pl.BlockSpec((HALO, 4 * HALF), lambda i: (jnp.maximum(i * per - 1, 0), 0)),
            pl.BlockSpec((HALO, 4 * HALF), lambda i: (jnp.minimum((i + 1) * per, n_halo_blocks - 1), 0)),
            pl.BlockSpec((len(POOL_WINDOWS), POOL_GROUP, POOL_GROUP), lambda i: (0, 0, 0)),
            pl.BlockSpec((1, HALF), lambda i: (0, 0)),
            pl.BlockSpec((3, HALF), lambda i: (0, 0)),
        ],
        out_specs=pl.BlockSpec((TM_MIX, 2 * HALF), lambda i: (i, 0)),
        out_shape=jax.ShapeDtypeStruct((N_TOK, 2 * HALF), BF16),
        scratch_shapes=[pltpu.VMEM((TM_MIX + 2 * HALO, HALF), F32)],
        compiler_params=_cparams(("parallel",)),
        name="even_mixer",
    )(proj, proj, proj, pool_w.astype(BF16), pool_scale.reshape(1, HALF), conv_w)


TM_MLP = 1024
TF_MLP = 512


def _mlp_body(final, x_ref, ya_ref, yb_ref, wo_ref, m_ref, g_ref, fg_ref, w1_ref, w2_ref, o_ref,
              x1_ref, h_ref, acc_ref):
    j = pl.program_id(1)

    @pl.when(j == 0)
    def _():
        m = m_ref[...]
        mix = _dot(ya_ref[...], wo_ref[0:HALF, :]) + _dot(yb_ref[...], wo_ref[HALF:2 * HALF, :])
        x1 = x_ref[...] + m[2:3] * mix
        x1_ref[...] = x1
        h_ref[...] = _normmod(x1, g_ref[...], m[3:4], m[4:5]).astype(BF16)
        acc_ref[...] = jnp.zeros_like(acc_ref)

    t = _dot(h_ref[...], w1_ref[...])
    t = jnp.square(jnp.maximum(t, 0.0)).astype(BF16)
    acc_ref[...] += _dot(t, w2_ref[...])

    @pl.when(j == pl.num_programs(1) - 1)
    def _():
        out = x1_ref[...] + m_ref[5:6, :] * acc_ref[...]
        if final:
            ms = jnp.mean(out * out, axis=-1, keepdims=True)
            out = out * lax.rsqrt(ms + EPS) * fg_ref[...]
        o_ref[...] = out


def _mlp(x, ya, ya_col, yb, yb_col, w_out, mods, layer, g2, final_g, w1, w2, final):
    tm, tf = TM_MLP, TF_MLP
    return pl.pallas_call(
        functools.partial(_mlp_body, final),
        grid=(N_TOK // tm, D_FF // tf),
        in_specs=[
            pl.BlockSpec((tm, D_MODEL), lambda i, j: (i, 0)),
            pl.BlockSpec((tm, HALF), lambda i, j: (i, ya_col)),
            pl.BlockSpec((tm, HALF), lambda i, j: (i, yb_col)),
            pl.BlockSpec((D_MODEL, D_MODEL), lambda i, j: (0, 0)),
            pl.BlockSpec((None, None, 6, D_MODEL), lambda i, j: (layer, _cond_row(i, tm), 0, 0)),
            pl.BlockSpec((1, D_MODEL), lambda i, j: (0, 0)),
            pl.BlockSpec((1, D_MODEL), lambda i, j: (0, 0)),
            pl.BlockSpec((D_MODEL, tf), lambda i, j: (0, j)),
            pl.BlockSpec((tf, D_MODEL), lambda i, j: (j, 0)),
        ],
        out_specs=pl.BlockSpec((tm, D_MODEL), lambda i, j: (i, 0)),
        out_shape=jax.ShapeDtypeStruct((N_TOK, D_MODEL), F32),
        scratch_shapes=[
            pltpu.VMEM((tm, D_MODEL), F32),
            pltpu.VMEM((tm, D_MODEL), BF16),
            pltpu.VMEM((tm, D_MODEL), F32),
        ],
        compiler_params=_cparams(("parallel", "arbitrary")),
        name="mlp",
    )(x, ya, yb, w_out, mods, g2.reshape(1, D_MODEL), final_g.reshape(1, D_MODEL), w1, w2)


def _ctx_attn_body(q_ref, k_ref, v_ref, o_ref, ck_ref, cv_ref):
    scale = HEAD_DIM ** -0.5
    for h in range(N_HEADS):
        sl = slice(h * HEAD_DIM, (h + 1) * HEAD_DIM)
        q = q_ref[:, sl]
        k = k_ref[:, sl]
        v = v_ref[:, sl]
        ck_ref[h] = k.astype(F32)
        cv_ref[h] = v.astype(F32)
        s = _dot_nt(q, k) * scale
        m = jnp.max(s, axis=-1, keepdims=True)
        p = jnp.exp(s - m)
        l = jnp.sum(p, axis=-1, keepdims=True)
        o = _dot(p.astype(BF16), v) / l
        o_ref[:, sl] = o.astype(BF16)


def _ctx_attn(proj):
    return pl.pallas_call(
        _ctx_attn_body,
        grid=(BATCH,),
        in_specs=[
            pl.BlockSpec((SEQ, HALF), lambda b: (b, 0)),
            pl.BlockSpec((SEQ, HALF), lambda b: (b, 1)),
            pl.BlockSpec((SEQ, HALF), lambda b: (b, 2)),
        ],
        out_specs=[
            pl.BlockSpec((SEQ, HALF), lambda b: (b, 0)),
            pl.BlockSpec((None, None, N_HEADS, SEQ, HEAD_DIM), lambda b: (b, 0, 0, 0, 0)),
            pl.BlockSpec((None, None, N_HEADS, SEQ, HEAD_DIM), lambda b: (b, 0, 0, 0, 0)),
        ],
        out_shape=[
            jax.ShapeDtypeStruct((N_PROMPT_TOK, HALF), BF16),
            jax.ShapeDtypeStruct((BATCH, 1, N_HEADS, SEQ, HEAD_DIM), F32),
            jax.ShapeDtypeStruct((BATCH, 1, N_HEADS, SEQ, HEAD_DIM), F32),
        ],
        compiler_params=_cparams(("parallel",)),
        name="ctx_attn",
    )(proj, proj, proj)


NA_MASKED = -1e30


def _na_bias_table(rpb):
    qc = jnp.arange(GRID_W)[:, None]
    kc = jnp.arange(GRID_W)[None, :]
    c0 = jnp.clip(qc - WIN_W // 2, 0, GRID_W - WIN_W)
    valid = (kc >= c0) & (kc < c0 + WIN_W)
    coff = jnp.clip(kc - qc, -(WIN_W - 1), WIN_W - 1) + (WIN_W - 1)
    ridx = jnp.arange(WIN_H)[:, None] + jnp.arange(WIN_H)[None, :]
    t = rpb.astype(F32)[:, ridx[:, :, None, None], coff[None, None, :, :]]
    t = jnp.where(valid[None, None, None], t, NA_MASKED)
    return t.transpose(0, 1, 3, 2, 4).reshape(N_HEADS, WIN_H, GRID_W, WIN_H * GRID_W)


def _na_body(q_ref, k_ref, v_ref, ck_ref, cv_ref, bias_ref, o_ref):
    scale = HEAD_DIM ** -0.5
    n_loc = WIN_H * GRID_W
    for hh in range(2):
        sl = slice(hh * HEAD_DIM, (hh + 1) * HEAD_DIM)
        kc = ck_ref[hh].astype(BF16)
        vc = cv_ref[hh].astype(BF16)

        def row(r, carry, sl=sl, kc=kc, vc=vc, hh=hh):
            r0 = jnp.clip(r - WIN_H // 2, 0, GRID_ROWS - WIN_H)
            qs = pl.multiple_of(r * GRID_W, GRID_W)
            ks = pl.multiple_of(r0 * GRID_W, GRID_W)
            q = q_ref[pl.ds(qs, GRID_W), sl]
            kl = k_ref[pl.ds(ks, n_loc), sl]
            vl = v_ref[pl.ds(ks, n_loc), sl]
            s_loc = _dot_nt(q, kl) * scale + bias_ref[hh, r0 - r + (WIN_H - 1)]
            s_ctx = _dot_nt(q, kc) * scale
            m = jnp.maximum(jnp.max(s_loc, axis=-1, keepdims=True), jnp.max(s_ctx, axis=-1, keepdims=True))
            p_loc = jnp.exp(s_loc - m)
            p_ctx = jnp.exp(s_ctx - m)
            l = jnp.sum(p_loc, axis=-1, keepdims=True) + jnp.sum(p_ctx, axis=-1, keepdims=True)
            o = (_dot(p_loc.astype(BF16), vl) + _dot(p_ctx.astype(BF16), vc)) / l
            o_ref[pl.ds(qs, GRID_W), sl] = o.astype(BF16)
            return carry

        lax.fori_loop(0, GRID_ROWS, row, 0)


def _na_attn(proj, cache_k, cache_v, bias, layer_idx):
    row_blk0 = N_PROMPT_TOK // DEC_SEQ
    hp = N_HEADS // 2
    return pl.pallas_call(
        _na_body,
        grid=(DEC_BATCH, hp),
        in_specs=[
            pl.BlockSpec((DEC_SEQ, 128), lambda b, h: (row_blk0 + b, h)),
            pl.BlockSpec((DEC_SEQ, 128), lambda b, h: (row_blk0 + b, hp + h)),
            pl.BlockSpec((DEC_SEQ, 128), lambda b, h: (row_blk0 + b, 2 * hp + h)),
            pl.BlockSpec((None, None, 2, PAST_LEN, HEAD_DIM), lambda b, h: (b, layer_idx, h, 0, 0)),
            pl.BlockSpec((None, None, 2, PAST_LEN, HEAD_DIM), lambda b, h: (b, layer_idx, h, 0, 0)),
            pl.BlockSpec((2, WIN_H, GRID_W, WIN_H * GRID_W), lambda b, h: (h, 0, 0, 0)),
        ],
        out_specs=pl.BlockSpec((DEC_SEQ, 128), lambda b, h: (b, h)),
        out_shape=jax.ShapeDtypeStruct((N_SAMPLE_TOK, HALF), BF16),
        compiler_params=_cparams(("parallel", "arbitrary")),
        name="na_attn",
    )(proj, proj, proj, cache_k, cache_v, bias)


N_POW = 24


def _s5_prep_body(vec_ref, bt_ref, c_ref, wend_ref, ktoe_ref, woutT_ref, laml_ref, pw_ref, pct_ref):
    L = S5_CHUNK
    lane = lax.broadcasted_iota(jnp.int32, (S5_CH, S5_W), 1)
    laml_ref[...] = jnp.zeros_like(laml_ref)
    krows = []
    for d in range(2):
        v = vec_ref[d]
        lre, lim, step = v[0:1], v[1:2], v[2:3]
        a = lre * step
        b = lim * step
        ea = jnp.exp(a)
        nr = ea * jnp.cos(b) - 1.0
        ni = ea * jnp.sin(b)
        den = lre * lre + lim * lim
        qr = (nr * lre + ni * lim) / den
        qi = (ni * lre - nr * lim) / den
        bre, bim = bt_ref[d, 0], bt_ref[d, 1]
        bqr = bre * qr - bim * qi
        bqi = bre * qi + bim * qr
        cre, cim = c_ref[d, 0], c_ref[d, 1]

        e = lax.broadcasted_iota(jnp.int32, (N_POW, 1), 0).astype(F32)
        mag = jnp.exp(e * a)
        pw_ref[0] = mag * jnp.cos(e * b)
        pw_ref[1] = mag * jnp.sin(e * b)

        def power(k):
            return pw_ref[0, k:k + 1, :], pw_ref[1, k:k + 1, :]

        def c_block(k):
            pr, pi = power(k)
            return cre * pr - cim * pi, -(cre * pi + cim * pr)

        laml_ref[2 * d:2 * d + 1, :] = pw_ref[0, L:L + 1, :]
        laml_ref[2 * d + 1:2 * d + 2, :] = pw_ref[1, L:L + 1, :]

        for i in range(L):
            rows = slice(i * S5_CH, (i + 1) * S5_CH)
            pr, pi = power(L - 1 - i if d == 0 else i)
            wend_ref[rows, d * 256:d * 256 + 128] = (bqr * pr - bqi * pi).astype(BF16)
            wend_ref[rows, d * 256 + 128:d * 256 + 256] = (bqr * pi + bqi * pr).astype(BF16)
            mre, mim = c_block(i + 1 if d == 0 else L - i)
            woutT_ref[rows, d * 256:d * 256 + 128] = mre.astype(BF16)
            woutT_ref[rows, d * 256 + 128:d * 256 + 256] = mim.astype(BF16)
            mre, mim = c_block(i if d == 0 else L - 1 - i)
            pct_ref[rows, 0:128] = mre
            pct_ref[rows, 128:256] = mim

        bqcat = jnp.concatenate([bqr, bqi], axis=1)
        krows.append(_dot_nt(bqcat, pct_ref[...], precision=lax.Precision.HIGHEST))

    for i in range(L):
        shift_f = i * S5_CH
        shift_b = ((i + 1) * S5_CH) % S5_W
        blk_f = krows[0] if shift_f == 0 else pltpu.roll(krows[0], shift_f, 1)
        blk_b = krows[1] if shift_b == 0 else pltpu.roll(krows[1], shift_b, 1)
        blk = jnp.where(lane >= shift_f, blk_f, 0.0) + jnp.where(lane < (i + 1) * S5_CH, blk_b, 0.0)
        ktoe_ref[i * S5_CH:(i + 1) * S5_CH, :] = blk.astype(BF16)


def _s5_prep(lam_re, lam_im, log_step, b_re, b_im, c_re, c_im):
    def pair_vec(a):
        return a.astype(F32).reshape(2, S5_PAIRS, 1, S5_P).transpose(1, 0, 2, 3)

    step = jnp.broadcast_to(jnp.exp(log_step.astype(F32))[:, :, None], (2, N_SSM_GROUPS, SSM_STATE))
    vec = jnp.concatenate([pair_vec(lam_re), pair_vec(lam_im), pair_vec(step),
                           jnp.zeros((S5_PAIRS, 2, 5, S5_P), F32)], axis=2)

    def block_diag_ch_by_state(m):
        m = m.astype(F32).reshape(2, S5_PAIRS, 2, SSM_GROUP, SSM_STATE)
        eye = jnp.eye(2, dtype=F32)
        out = m[:, :, :, :, None, :] * eye[None, None, :, None, :, None]
        return out.reshape(2, S5_PAIRS, S5_CH, S5_P).transpose(1, 0, 2, 3)

    bt = jnp.stack([block_diag_ch_by_state(b_re.transpose(0, 1, 3, 2)),
                    block_diag_ch_by_state(b_im.transpose(0, 1, 3, 2))], axis=2)
    cc = jnp.stack([block_diag_ch_by_state(c_re), block_diag_ch_by_state(c_im)], axis=2)

    mat = jax.ShapeDtypeStruct((S5_PAIRS, S5_W, S5_W), BF16)
    mat_spec = pl.BlockSpec((None, S5_W, S5_W), lambda p: (p, 0, 0))
    return pl.pallas_call(
        _s5_prep_body,
        grid=(S5_PAIRS,),
        in_specs=[
            pl.BlockSpec((None, 2, 8, S5_P), lambda p: (p, 0, 0, 0)),
            pl.BlockSpec((None, 2, 2, S5_CH, S5_P), lambda p: (p, 0, 0, 0, 0)),
            pl.BlockSpec((None, 2, 2, S5_CH, S5_P), lambda p: (p, 0, 0, 0, 0)),
        ],
        out_specs=[mat_spec, mat_spec, mat_spec, pl.BlockSpec((None, 8, S5_P), lambda p: (p, 0, 0))],
        out_shape=[mat, mat, mat, jax.ShapeDtypeStruct((S5_PAIRS, 8, S5_P), F32)],
        scratch_shapes=[pltpu.VMEM((2, N_POW, S5_P), F32), pltpu.VMEM((S5_W, 2 * S5_P), F32)],
        compiler_params=_cparams(("parallel",)),
        name="s5_prep",
    )(vec, bt, cc)


def _s5_stream(u_ref, y_ref, n_chunks, bp, init, wend_ref, ktoe_ref, woutT_ref, laml_ref, z_ref, xs_ref):
    rows = n_chunks * bp
    u = u_ref[...]
    z_ref[0:rows, :] = _dot(u, wend_ref[...])
    fins = []
    for d in range(2):
        are = laml_ref[2 * d:2 * d + 1, :]
        aim = laml_ref[2 * d + 1:2 * d + 2, :]
        lr = slice(d * 256, d * 256 + 128)
        li = slice(d * 256 + 128, d * 256 + 256)

        def step(k, carry, d=d, are=are, aim=aim, lr=lr, li=li):
            xr, xi = carry
            c = k if d == 0 else n_chunks - 1 - k
            off = pl.multiple_of(c * bp, bp)
            xs_ref[pl.ds(off, bp), lr] = xr
            xs_ref[pl.ds(off, bp), li] = xi
            zr = z_ref[pl.ds(off, bp), lr]
            zi = z_ref[pl.ds(off, bp), li]
            return are * xr - aim * xi + zr, are * xi + aim * xr + zi

        fins.append(lax.fori_loop(0, n_chunks, step, init[d]))
    y = _dot(u, ktoe_ref[...]) + _dot_nt(xs_ref[0:rows, :].astype(BF16), woutT_ref[...])
    y_ref[...] = y
    return fins


def _s5_body(up_ref, us_ref, s0re_ref, s0im_ref, wend_ref, ktoe_ref, woutT_ref, laml_ref,
             yp_ref, ys_ref, finre_ref, finim_ref, z_ref, xs_ref):
    zero = jnp.zeros((BATCH, S5_P), F32)
    fins = _s5_stream(up_ref, yp_ref, SEQ // S5_CHUNK, BATCH, [(zero, zero), (zero, zero)],
                      wend_ref, ktoe_ref, woutT_ref, laml_ref, z_ref, xs_ref)
    for d in range(2):
        finre_ref[d] = fins[d][0]
        finim_ref[d] = fins[d][1]
    init = [(s0re_ref[d], s0im_ref[d]) for d in range(2)]
    _s5_stream(us_ref, ys_ref, DEC_SEQ // S5_CHUNK, S5_BP_SAMPLE, init,
               wend_ref, ktoe_ref, woutT_ref, laml_ref, z_ref, xs_ref)


def _s5_scan(u_p, u_s, s0_re, s0_im, wend, ktoe, woutT, laml):
    rows_p = (SEQ // S5_CHUNK) * BATCH
    rows_s = (DEC_SEQ // S5_CHUNK) * S5_BP_SAMPLE
    mat_spec = pl.BlockSpec((None, S5_W, S5_W), lambda p: (p, 0, 0))
    return pl.pallas_call(
        _s5_body,
        grid=(S5_PAIRS,),
        in_specs=[
            pl.BlockSpec((None, rows_p, S5_W), lambda p: (p, 0, 0)),
            pl.BlockSpec((None, rows_s, S5_W), lambda p: (p, 0, 0)),
            pl.BlockSpec((None, 2, S5_BP_SAMPLE, S5_P), lambda p: (p, 0, 0, 0)),
            pl.BlockSpec((None, 2, S5_BP_SAMPLE, S5_P), lambda p: (p, 0, 0, 0)),
            mat_spec, mat_spec, mat_spec,
            pl.BlockSpec((None, 8, S5_P), lambda p: (p, 0, 0)),
        ],
        out_specs=[
            pl.BlockSpec((None, rows_p, S5_W), lambda p: (p, 0, 0)),
            pl.BlockSpec((None, rows_s, S5_W), lambda p: (p, 0, 0)),
            pl.BlockSpec((None, 2, BATCH, S5_P), lambda p: (p, 0, 0, 0)),
            pl.BlockSpec((None, 2, BATCH, S5_P), lambda p: (p, 0, 0, 0)),
        ],
        out_shape=[
            jax.ShapeDtypeStruct((S5_PAIRS, rows_p, S5_W), F32),
            jax.ShapeDtypeStruct((S5_PAIRS, rows_s, S5_W), F32),
            jax.ShapeDtypeStruct((S5_PAIRS, 2, BATCH, S5_P), F32),
            jax.ShapeDtypeStruct((S5_PAIRS, 2, BATCH, S5_P), F32),
        ],
        scratch_shapes=[pltpu.VMEM((rows_s, S5_W), F32), pltpu.VMEM((rows_s, S5_W), F32)],
        compiler_params=_cparams(("parallel",)),
        name="s5_scan",
    )(u_p, u_s, s0_re, s0_im, wend, ktoe, woutT, laml)


def _to_s5_layout(u, b, t, bp):
    c = t // S5_CHUNK
    u = u.reshape(b, c, S5_CHUNK, S5_PAIRS, S5_CH).transpose(3, 1, 0, 2, 4)
    if bp != b:
        u = jnp.pad(u, ((0, 0), (0, 0), (0, bp - b), (0, 0), (0, 0)))
    return u.reshape(S5_PAIRS, c * bp, S5_W)


def _from_s5_layout(y, b, t, bp):
    c = t // S5_CHUNK
    y = y.reshape(S5_PAIRS, c, bp, S5_CHUNK, S5_CH)[:, :, :b]
    return y.transpose(2, 1, 3, 0, 4).reshape(b * t, HALF)


TM_GLU = 1024


def _glu_body(u_ref, ys_ref, d_ref, gw_ref, gb_ref, o_ref):
    y = d_ref[...] * u_ref[...].astype(F32) + ys_ref[...]
    cdf = 0.5 * (1.0 + jnp.tanh(math.sqrt(2.0 / math.pi) * (y + 0.044715 * (y * y * y))))
    y = y * cdf
    gate = jax.nn.sigmoid(_dot(y.astype(BF16), gw_ref[...]) + gb_ref[...])
    o_ref[...] = (y * gate).astype(BF16)


def _glu(proj, y_ssm, ssm_d, glu_w, glu_b):
    tm = TM_GLU
    return pl.pallas_call(
        _glu_body,
        grid=(N_TOK // tm,),
        in_specs=[
            pl.BlockSpec((tm, HALF), lambda i: (i, 3)),
            pl.BlockSpec((tm, HALF), lambda i: (i, 0)),
            pl.BlockSpec((1, HALF), lambda i: (0, 0)),
            pl.BlockSpec((HALF, HALF), lambda i: (0, 0)),
            pl.BlockSpec((1, HALF), lambda i: (0, 0)),
        ],
        out_specs=pl.BlockSpec((tm, HALF), lambda i: (i, 0)),
        out_shape=jax.ShapeDtypeStruct((N_TOK, HALF), BF16),
        compiler_params=_cparams(("parallel",)),
        name="s5_glu",
    )(proj, y_ssm, ssm_d.reshape(1, HALF), glu_w.astype(BF16), glu_b.reshape(1, HALF))


def kernel(x_prompt, x_sample, c, cache_k, cache_v, state_s5_re, state_s5_im, c_ctx, norm1_g, norm2_g, ada_w, ada_b, mlp_w1, mlp_w2, ab_w_in, pool_w, pool_scale, conv_w, ab_w_out, cd_w_in, na_rpb, ssm_lambda_re, ssm_lambda_im, ssm_log_step, ssm_b_re, ssm_b_im, ssm_c_re, ssm_c_im, ssm_d, glu_w, glu_b, cd_w_out, final_g):
    depth = ada_w.shape[0]
    x = jnp.concatenate([x_prompt.reshape(N_PROMPT_TOK, D_MODEL), x_sample.reshape(N_SAMPLE_TOK, D_MODEL)], axis=0)
    cond = jnp.concatenate([c_ctx[None, :], c, jnp.zeros((N_COND - 1 - DEC_BATCH, D_MODEL), F32)], axis=0)
    mods = _adaln(cond, ada_w, ada_b).reshape(depth, N_COND, 6, D_MODEL)

    new_k, new_v, new_re, new_im = [], [], [], []
    for layer in range(depth):
        i = layer // 2
        final = layer == depth - 1
        if layer % 2 == 0:
            proj = _inproj(x, mods, layer, norm1_g[layer], ab_w_in[i].astype(BF16))
            y = _even_mixer(proj, pool_w[i], pool_scale[i], conv_w[i])
            ya, ya_col, yb, yb_col = y, 0, y, 1
            w_out = ab_w_out[i]
        else:
            proj = _inproj(x, mods, layer, norm1_g[layer], cd_w_in[i].astype(BF16))
            o_p, ck, cv = _ctx_attn(proj)
            new_k.append(ck)
            new_v.append(cv)
            o_s = _na_attn(proj, cache_k, cache_v, _na_bias_table(na_rpb[i]), i)
            attn = jnp.concatenate([o_p, o_s], axis=0)

            wend, ktoe, woutT, laml = _s5_prep(ssm_lambda_re[i], ssm_lambda_im[i], ssm_log_step[i],
                                               ssm_b_re[i], ssm_b_im[i], ssm_c_re[i], ssm_c_im[i])
            u = proj[:, 3 * HALF:]
            u_p = _to_s5_layout(u[:N_PROMPT_TOK], BATCH, SEQ, BATCH)
            u_s = _to_s5_layout(u[N_PROMPT_TOK:], DEC_BATCH, DEC_SEQ, S5_BP_SAMPLE)

            def pair_state(s):
                s = s.astype(F32).reshape(DEC_BATCH, 2, S5_PAIRS, S5_P).transpose(2, 1, 0, 3)
                return jnp.pad(s, ((0, 0), (0, 0), (0, S5_BP_SAMPLE - DEC_BATCH), (0, 0)))

            y_p, y_s, fin_re, fin_im = _s5_scan(u_p, u_s, pair_state(state_s5_re[:, i]), pair_state(state_s5_im[:, i]),
                                                wend, ktoe, woutT, laml)
            y_ssm = jnp.concatenate([_from_s5_layout(y_p, BATCH, SEQ, BATCH),
                                     _from_s5_layout(y_s, DEC_BATCH, DEC_SEQ, S5_BP_SAMPLE)], axis=0)
            new_re.append(fin_re.transpose(2, 1, 0, 3).reshape(BATCH, 2, N_SSM_GROUPS, SSM_STATE))
            new_im.append(fin_im.transpose(2, 1, 0, 3).reshape(BATCH, 2, N_SSM_GROUPS, SSM_STATE))
            d_out = _glu(proj, y_ssm, ssm_d[i], glu_w[i], glu_b[i])
            ya, ya_col, yb, yb_col = attn, 0, d_out, 0
            w_out = cd_w_out[i]
        x = _mlp(x, ya, ya_col, yb, yb_col, w_out.astype(BF16), mods, layer, norm2_g[layer], final_g,
                 mlp_w1[layer].astype(BF16), mlp_w2[layer].astype(BF16), final)

    y_prompt = x[:N_PROMPT_TOK].reshape(BATCH, SEQ, D_MODEL)
    y_sample = x[N_PROMPT_TOK:].reshape(DEC_BATCH, DEC_SEQ, D_MODEL)
    return (y_prompt, y_sample, jnp.concatenate(new_k, axis=1), jnp.concatenate(new_v, axis=1),
            jnp.stack(new_re, axis=1), jnp.stack(new_im, axis=1))
```

```python
import functools
import math

import jax
import jax.numpy as jnp
from jax import lax
from jax.experimental import pallas as pl
from jax.experimental.pallas import tpu as pltpu

F32 = jnp.float32
BF16 = jnp.bfloat16

D_MODEL = 1024
BATCH = 16
SEQ = 256
DEC_BATCH = 4
DEC_SEQ = 2048
PAST_LEN = 256
GRID_W = 64
GRID_ROWS = DEC_SEQ // GRID_W
HALF = 512
POOL_WINDOWS = (2, 4, 8, 16)
POOL_GROUP = 128
HEAD_DIM = 64
N_HEADS = 8
WIN_H = 8
WIN_W = 16
SSM_GROUP = 16
N_SSM_GROUPS = 32
SSM_STATE = 64
D_FF = 4096
EPS = 1e-6

N_PROMPT_TOK = BATCH * SEQ
N_SAMPLE_TOK = DEC_BATCH * DEC_SEQ
N_TOK = N_PROMPT_TOK + N_SAMPLE_TOK
N_COND = 8

S5_CHUNK = 16
S5_PAIRS = N_SSM_GROUPS // 2
S5_CH = 2 * SSM_GROUP
S5_P = 2 * SSM_STATE
S5_W = S5_CHUNK * S5_CH
S5_BP_SAMPLE = 8

VMEM_LIMIT = 56 * 1024 * 1024


def _cparams(sem):
    return pltpu.CompilerParams(dimension_semantics=sem, vmem_limit_bytes=VMEM_LIMIT)


def _cond_row(i, tm):
    npt = N_PROMPT_TOK // tm
    per = DEC_SEQ // tm
    return jnp.where(i < npt, 0, 1 + (i - npt) // per)


def _normmod(x, g, shift, scale):
    ms = jnp.mean(x * x, axis=-1, keepdims=True)
    return (x * lax.rsqrt(ms + EPS) * g) * (1.0 + scale) + shift


def _dot(a, b):
    return jnp.dot(a, b, preferred_element_type=F32)


def _dot_nt(a, b, precision=None):
    return lax.dot_general(a, b, (((1,), (1,)), ((), ())), preferred_element_type=F32, precision=precision)


def _adaln_body(c_ref, w_ref, b_ref, o_ref):
    c = c_ref[...]
    s = c * jax.nn.sigmoid(c)
    o_ref[...] = _dot(s.astype(BF16), w_ref[...].astype(BF16)) + b_ref[...]


def _adaln(cond, ada_w, ada_b):
    depth = ada_w.shape[0]
    nj = 6
    return pl.pallas_call(
        _adaln_body,
        grid=(depth, nj),
        in_specs=[
            pl.BlockSpec((N_COND, D_MODEL), lambda l, j: (0, 0)),
            pl.BlockSpec((None, D_MODEL, D_MODEL), lambda l, j: (l, 0, j)),
            pl.BlockSpec((None, 1, D_MODEL), lambda l, j: (l, 0, j)),
        ],
        out_specs=pl.BlockSpec((None, N_COND, D_MODEL), lambda l, j: (l, 0, j)),
        out_shape=jax.ShapeDtypeStruct((depth, N_COND, 6 * D_MODEL), F32),
        compiler_params=_cparams(("arbitrary", "arbitrary")),
        name="adaln",
    )(cond, ada_w, ada_b.reshape(depth, 1, 6 * D_MODEL))


TM_PROJ = 512


def _inproj_body(x_ref, m_ref, g_ref, w_ref, o_ref):
    m = m_ref[...]
    h = _normmod(x_ref[...], g_ref[...], m[0:1], m[1:2])
    o_ref[...] = _dot(h.astype(BF16), w_ref[...]).astype(BF16)


def _inproj(x, mods, layer, g, w):
    tm = TM_PROJ
    n_out = w.shape[1]
    return pl.pallas_call(
        _inproj_body,
        grid=(N_TOK // tm,),
        in_specs=[
            pl.BlockSpec((tm, D_MODEL), lambda i: (i, 0)),
            pl.BlockSpec((None, None, 6, D_MODEL), lambda i: (layer, _cond_row(i, tm), 0, 0)),
            pl.BlockSpec((1, D_MODEL), lambda i: (0, 0)),
            pl.BlockSpec((D_MODEL, n_out), lambda i: (0, 0)),
        ],
        out_specs=pl.BlockSpec((tm, n_out), lambda i: (i, 0)),
        out_shape=jax.ShapeDtypeStruct((N_TOK, n_out), BF16),
        compiler_params=_cparams(("parallel",)),
        name="inproj",
    )(x, mods, g.reshape(1, D_MODEL), w)


TM_MIX = 256
HALO = 16


def _even_mixer_body(p_ref, prev_ref, next_ref, pw_ref, ps_ref, cw_ref, y_ref, pad_ref):
    i = pl.program_id(0)
    n_prompt_tiles = N_PROMPT_TOK // TM_MIX
    tiles_per_seq = DEC_SEQ // TM_MIX
    is_prompt = i < n_prompt_tiles
    tile_in_seq = jnp.where(is_prompt, 0, (i - n_prompt_tiles) % tiles_per_seq)
    seq_len = jnp.where(is_prompt, SEQ, DEC_SEQ)
    has_prev = tile_in_seq > 0
    has_next = jnp.logical_and(jnp.logical_not(is_prompt), tile_in_seq < tiles_per_seq - 1)
    prev_keep = jnp.where(has_prev, 1.0, 0.0).astype(F32)
    next_keep = jnp.where(has_next, 1.0, 0.0).astype(F32)
    pos = tile_in_seq * TM_MIX + lax.broadcasted_iota(jnp.int32, (TM_MIX, 1), 0)
    lo_rows, hi_rows = HALO, HALO + TM_MIX

    pad_ref[0:HALO, :] = prev_ref[:, 0:HALF].astype(F32) * prev_keep
    pad_ref[lo_rows:hi_rows, :] = p_ref[:, 0:HALF].astype(F32)
    pad_ref[hi_rows:hi_rows + HALO, :] = next_ref[:, 0:HALF].astype(F32) * next_keep
    for gi, w in enumerate(POOL_WINDOWS):
        cs = slice(gi * POOL_GROUP, (gi + 1) * POOL_GROUP)
        s = None
        for o in range(-(w // 2), w - w // 2):
            t = pad_ref[lo_rows + o:hi_rows + o, cs]
            s = t if s is None else s + t
        lo = jnp.maximum(pos - w // 2, 0)
        hi = jnp.minimum(pos + (w - w // 2), seq_len)
        inv = 1.0 / (hi - lo).astype(F32)
        pooled = s * inv - pad_ref[lo_rows:hi_rows, cs]
        mixed = _dot(pooled.astype(BF16), pw_ref[gi]) * ps_ref[:, cs]
        y_ref[:, cs] = mixed.astype(BF16)

    c0, v0 = 2 * HALF, 3 * HALF
    pad_ref[0:HALO, :] = (prev_ref[:, c0:c0 + HALF].astype(F32) * prev_ref[:, v0:v0 + HALF].astype(F32)) * prev_keep
    pad_ref[lo_rows:hi_rows, :] = p_ref[:, c0:c0 + HALF].astype(F32) * p_ref[:, v0:v0 + HALF].astype(F32)
    pad_ref[hi_rows:hi_rows + HALO, :] = (next_ref[:, c0:c0 + HALF].astype(F32)
                                          * next_ref[:, v0:v0 + HALF].astype(F32)) * next_keep
    for ci in range(HALF // 128):
        cs = slice(ci * 128, (ci + 1) * 128)
        conv = (cw_ref[0:1, cs] * pad_ref[lo_rows - 1:hi_rows - 1, cs]
                + cw_ref[1:2, cs] * pad_ref[lo_rows:hi_rows, cs]
                + cw_ref[2:3, cs] * pad_ref[lo_rows + 1:hi_rows + 1, cs])
        bg = p_ref[:, HALF + ci * 128:HALF + (ci + 1) * 128].astype(F32)
        y_ref[:, HALF + ci * 128:HALF + (ci + 1) * 128] = (bg * conv).astype(BF16)


def _even_mixer(proj, pool_w, pool_scale, conv_w):
    n_tiles = N_TOK // TM_MIX
    per = TM_MIX // HALO
    n_halo_blocks = N_TOK // HALO
    return pl.pallas_call(
        _even_mixer_body,
        grid=(n_tiles,),
        in_specs=[
            pl.BlockSpec((TM_MIX, 4 * HALF), lambda i: (i, 0)),
            pl.BlockSpec((HALO, 4 * HALF), lambda i: (jnp.maximum(i * per - 1, 0), 0)),
            pl.BlockSpec((HALO, 4 * HALF), lambda i: (jnp.minimum((i + 1) * per, n_halo_blocks - 1), 0)),
            pl.BlockSpec((len(POOL_WINDOWS), POOL_GROUP, POOL_GROUP), lambda i: (0, 0, 0)),
            pl.BlockSpec((1, HALF), lambda i: (0, 0)),
            pl.BlockSpec((3, HALF), lambda i: (0, 0)),
        ],
        out_specs=pl.BlockSpec((TM_MIX, 2 * HALF), lambda i: (i, 0)),
        out_shape=jax.ShapeDtypeStruct((N_TOK, 2 * HALF), BF16),
        scratch_shapes=[pltpu.VMEM((TM_MIX + 2 * HALO, HALF), F32)],
        compiler_params=_cparams(("parallel",)),
        name="even_mixer",
    )(proj, proj, proj, pool_w.astype(BF16), pool_scale.reshape(1, HALF), conv_w)


TM_MLP = 1024
TF_MLP = 512


def _mlp_body(final, x_ref, ya_ref, yb_ref, wo_ref, m_ref, g_ref, fg_ref, w1_ref, w2_ref, o_ref,
              x1_ref, h_ref, acc_ref):
    j = pl.program_id(1)

    @pl.when(j == 0)
    def _():
        m = m_ref[...]
        mix = _dot(ya_ref[...], wo_ref[0:HALF, :]) + _dot(yb_ref[...], wo_ref[HALF:2 * HALF, :])
        x1 = x_ref[...] + m[2:3] * mix
        x1_ref[...] = x1
        h_ref[...] = _normmod(x1, g_ref[...], m[3:4], m[4:5]).astype(BF16)
        acc_ref[...] = jnp.zeros_like(acc_ref)

    t = _dot(h_ref[...], w1_ref[...])
    t = jnp.square(jnp.maximum(t, 0.0)).astype(BF16)
    acc_ref[...] += _dot(t, w2_ref[...])

    @pl.when(j == pl.num_programs(1) - 1)
    def _():
        out = x1_ref[...] + m_ref[5:6, :] * acc_ref[...]
        if final:
            ms = jnp.mean(out * out, axis=-1, keepdims=True)
            out = out * lax.rsqrt(ms + EPS) * fg_ref[...]
        o_ref[...] = out


def _mlp(x, ya, ya_col, yb, yb_col, w_out, mods, layer, g2, final_g, w1, w2, final):
    tm, tf = TM_MLP, TF_MLP
    return pl.pallas_call(
        functools.partial(_mlp_body, final),
        grid=(N_TOK // tm, D_FF // tf),
        in_specs=[
            pl.BlockSpec((tm, D_MODEL), lambda i, j: (i, 0)),
            pl.BlockSpec((tm, HALF), lambda i, j: (i, ya_col)),
            pl.BlockSpec((tm, HALF), lambda i, j: (i, yb_col)),
            pl.BlockSpec((D_MODEL, D_MODEL), lambda i, j: (0, 0)),
            pl.BlockSpec((None, None, 6, D_MODEL), lambda i, j: (layer, _cond_row(i, tm), 0, 0)),
            pl.BlockSpec((1, D_MODEL), lambda i, j: (0, 0)),
            pl.BlockSpec((1, D_MODEL), lambda i, j: (0, 0)),
            pl.BlockSpec((D_MODEL, tf), lambda i, j: (0, j)),
            pl.BlockSpec((tf, D_MODEL), lambda i, j: (j, 0)),
        ],
        out_specs=pl.BlockSpec((tm, D_MODEL), lambda i, j: (i, 0)),
        out_shape=jax.ShapeDtypeStruct((N_TOK, D_MODEL), F32),
        scratch_shapes=[
            pltpu.VMEM((tm, D_MODEL), F32),
            pltpu.VMEM((tm, D_MODEL), BF16),
            pltpu.VMEM((tm, D_MODEL), F32),
        ],
        compiler_params=_cparams(("parallel", "arbitrary")),
        name="mlp",
    )(x, ya, yb, w_out, mods, g2.reshape(1, D_MODEL), final_g.reshape(1, D_MODEL), w1, w2)


def _ctx_attn_body(q_ref, k_ref, v_ref, o_ref, ck_ref, cv_ref):
    scale = HEAD_DIM ** -0.5
    for h in range(N_HEADS):
        sl = slice(h * HEAD_DIM, (h + 1) * HEAD_DIM)
        q = q_ref[:, sl]
        k = k_ref[:, sl]
        v = v_ref[:, sl]
        ck_ref[h] = k.astype(F32)
        cv_ref[h] = v.astype(F32)
        s = _dot_nt(q, k) * scale
        m = jnp.max(s, axis=-1, keepdims=True)
        p = jnp.exp(s - m)
        l = jnp.sum(p, axis=-1, keepdims=True)
        o = _dot(p.astype(BF16), v) / l
        o_ref[:, sl] = o.astype(BF16)


def _ctx_attn(proj):
    return pl.pallas_call(
        _ctx_attn_body,
        grid=(BATCH,),
        in_specs=[
            pl.BlockSpec((SEQ, HALF), lambda b: (b, 0)),
            pl.BlockSpec((SEQ, HALF), lambda b: (b, 1)),
            pl.BlockSpec((SEQ, HALF), lambda b: (b, 2)),
        ],
        out_specs=[
            pl.BlockSpec((SEQ, HALF), lambda b: (b, 0)),
            pl.BlockSpec((None, None, N_HEADS, SEQ, HEAD_DIM), lambda b: (b, 0, 0, 0, 0)),
            pl.BlockSpec((None, None, N_HEADS, SEQ, HEAD_DIM), lambda b: (b, 0, 0, 0, 0)),
        ],
        out_shape=[
            jax.ShapeDtypeStruct((N_PROMPT_TOK, HALF), BF16),
            jax.ShapeDtypeStruct((BATCH, 1, N_HEADS, SEQ, HEAD_DIM), F32),
            jax.ShapeDtypeStruct((BATCH, 1, N_HEADS, SEQ, HEAD_DIM), F32),
        ],
        compiler_params=_cparams(("parallel",)),
        name="ctx_attn",
    )(proj, proj, proj)


NA_MASKED = -1e30
NA_QROWS = 4
NA_KROWS = NA_QROWS + WIN_H
NA_NQ = NA_QROWS * GRID_W
NA_NK = NA_KROWS * GRID_W
NA_BLOCKS = GRID_ROWS // NA_QROWS
NA_TYPES = 3


def _na_col_table(rpb):
    qc = jnp.arange(GRID_W)[:, None]
    kc = jnp.arange(GRID_W)[None, :]
    c0 = jnp.clip(qc - WIN_W // 2, 0, GRID_W - WIN_W)
    valid = (kc >= c0) & (kc < c0 + WIN_W)
    coff = jnp.clip(kc - qc, -(WIN_W - 1), WIN_W - 1) + (WIN_W - 1)
    onehot = (coff.reshape(1, -1) == jnp.arange(2 * WIN_W - 1)[:, None]).astype(F32)
    t = jnp.dot(rpb.astype(F32).reshape(N_HEADS * (2 * WIN_H - 1), 2 * WIN_W - 1), onehot,
                precision=lax.Precision.HIGHEST)
    t = t.reshape(N_HEADS, 2 * WIN_H - 1, GRID_W, GRID_W)
    return jnp.where(valid[None, None], t, NA_MASKED)


def _na_window_start(rb):
    return jnp.clip(NA_QROWS * rb - WIN_H // 2, 0, GRID_ROWS - NA_KROWS)


def _na_body(q_ref, k_ref, v_ref, ck_ref, cv_ref, tab_ref, o_ref, bias_ref):
    @pl.when(pl.program_id(1) == 0)
    def _():
        masked = jnp.full((GRID_W, GRID_W), NA_MASKED, F32)
        for hh in range(2):
            for ty in range(NA_TYPES):
                q0 = (0, NA_QROWS, GRID_ROWS - NA_QROWS)[ty]
                ks = (0, 0, GRID_ROWS - NA_KROWS)[ty]
                for a in range(NA_QROWS):
                    r0 = min(max(q0 + a - WIN_H // 2, 0), GRID_ROWS - WIN_H)
                    for y in range(NA_KROWS):
                        kr = ks + y
                        ok = r0 <= kr < r0 + WIN_H
                        blk = tab_ref[hh, kr - (q0 + a) + WIN_H - 1] if ok else masked
                        bias_ref[hh, ty, a * GRID_W:(a + 1) * GRID_W, y * GRID_W:(y + 1) * GRID_W] = blk

    for hh in range(2):
        sl = slice(hh * HEAD_DIM, (hh + 1) * HEAD_DIM)
        kc = ck_ref[hh].astype(BF16)
        vc = cv_ref[hh].astype(BF16)

        def block(rb, ty, sl=sl, kc=kc, vc=vc, hh=hh):
            qs = pl.multiple_of(rb * NA_NQ, NA_NQ)
            ks = pl.multiple_of(_na_window_start(rb) * GRID_W, GRID_W)
            q = q_ref[pl.ds(qs, NA_NQ), sl] * jnp.asarray(HEAD_DIM ** -0.5, BF16)
            kl = k_ref[pl.ds(ks, NA_NK), sl]
            vl = v_ref[pl.ds(ks, NA_NK), sl]
            s_loc = _dot_nt(q, kl) + bias_ref[hh, ty]
            s_ctx = _dot_nt(q, kc)
            m = jnp.maximum(jnp.max(s_loc, axis=-1, keepdims=True), jnp.max(s_ctx, axis=-1, keepdims=True))
            p_loc = jnp.exp(s_loc - m)
            p_ctx = jnp.exp(s_ctx - m)
            l = jnp.sum(p_loc, axis=-1, keepdims=True) + jnp.sum(p_ctx, axis=-1, keepdims=True)
            o = (_dot(p_loc.astype(BF16), vl) + _dot(p_ctx.astype(BF16), vc)) / l
            o_ref[pl.ds(qs, NA_NQ), sl] = o.astype(BF16)

        block(0, 0)

        def interior(rb, carry, block=block):
            block(rb, 1)
            return carry

        lax.fori_loop(1, NA_BLOCKS - 1, interior, 0)
        block(NA_BLOCKS - 1, 2)


def _na_attn(proj, cache_k, cache_v, col_table, layer_idx):
    row_blk0 = N_PROMPT_TOK // DEC_SEQ
    hp = N_HEADS // 2
    return pl.pallas_call(
        _na_body,
        grid=(hp, DEC_BATCH),
        in_specs=[
            pl.BlockSpec((DEC_SEQ, 128), lambda h, b: (row_blk0 + b, h)),
            pl.BlockSpec((DEC_SEQ, 128), lambda h, b: (row_blk0 + b, hp + h)),
            pl.BlockSpec((DEC_SEQ, 128), lambda h, b: (row_blk0 + b, 2 * hp + h)),
            pl.BlockSpec((None, None, 2, PAST_LEN, HEAD_DIM), lambda h, b: (b, layer_idx, h, 0, 0)),
            pl.BlockSpec((None, None, 2, PAST_LEN, HEAD_DIM), lambda h, b: (b, layer_idx, h, 0, 0)),
            pl.BlockSpec((2, 2 * WIN_H - 1, GRID_W, GRID_W), lambda h, b: (h, 0, 0, 0)),
        ],
        out_specs=pl.BlockSpec((DEC_SEQ, 128), lambda h, b: (b, h)),
        out_shape=jax.ShapeDtypeStruct((N_SAMPLE_TOK, HALF), BF16),
        scratch_shapes=[pltpu.VMEM((2, NA_TYPES, NA_NQ, NA_NK), F32)],
        compiler_params=_cparams(("parallel", "arbitrary")),
        name="na_attn",
    )(proj, proj, proj, cache_k, cache_v, col_table)


N_POW = 24


def _s5_prep_body(vec_ref, bt_ref, c_ref, wend_ref, ktoe_ref, woutT_ref, laml_ref, pw_ref, pct_ref):
    L = S5_CHUNK
    lane = lax.broadcasted_iota(jnp.int32, (S5_CH, S5_W), 1)
    laml_ref[...] = jnp.zeros_like(laml_ref)
    krows = []
    for d in range(2):
        v = vec_ref[d]
        lre, lim, step = v[0:1], v[1:2], v[2:3]
        a = lre * step
        b = lim * step
        ea = jnp.exp(a)
        nr = ea * jnp.cos(b) - 1.0
        ni = ea * jnp.sin(b)
        den = lre * lre + lim * lim
        qr = (nr * lre + ni * lim) / den
        qi = (ni * lre - nr * lim) / den
        bre, bim = bt_ref[d, 0], bt_ref[d, 1]
        bqr = bre * qr - bim * qi
        bqi = bre * qi + bim * qr
        cre, cim = c_ref[d, 0], c_ref[d, 1]

        e = lax.broadcasted_iota(jnp.int32, (N_POW, 1), 0).astype(F32)
        mag = jnp.exp(e * a)
        pw_ref[0] = mag * jnp.cos(e * b)
        pw_ref[1] = mag * jnp.sin(e * b)

        def power(k):
            return pw_ref[0, k:k + 1, :], pw_ref[1, k:k + 1, :]

        def c_block(k):
            pr, pi = power(k)
            return cre * pr - cim * pi, -(cre * pi + cim * pr)

        laml_ref[2 * d:2 * d + 1, :] = pw_ref[0, L:L + 1, :]
        laml_ref[2 * d + 1:2 * d + 2, :] = pw_ref[1, L:L + 1, :]

        for i in range(L):
            rows = slice(i * S5_CH, (i + 1) * S5_CH)
            pr, pi = power(L - 1 - i if d == 0 else i)
            wend_ref[rows, d * 256:d * 256 + 128] = (bqr * pr - bqi * pi).astype(BF16)
            wend_ref[rows, d * 256 + 128:d * 256 + 256] = (bqr * pi + bqi * pr).astype(BF16)
            mre, mim = c_block(i + 1 if d == 0 else L - i)
            woutT_ref[rows, d * 256:d * 256 + 128] = mre.astype(BF16)
            woutT_ref[rows, d * 256 + 128:d * 256 + 256] = mim.astype(BF16)
            mre, mim = c_block(i if d == 0 else L - 1 - i)
            pct_ref[rows, 0:128] = mre
            pct_ref[rows, 128:256] = mim

        bqcat = jnp.concatenate([bqr, bqi], axis=1)
        krows.append(_dot_nt(bqcat, pct_ref[...], precision=lax.Precision.HIGHEST))

    for i in range(L):
        shift_f = i * S5_CH
        shift_b = ((i + 1) * S5_CH) % S5_W
        blk_f = krows[0] if shift_f == 0 else pltpu.roll(krows[0], shift_f, 1)
        blk_b = krows[1] if shift_b == 0 else pltpu.roll(krows[1], shift_b, 1)
        blk = jnp.where(lane >= shift_f, blk_f, 0.0) + jnp.where(lane < (i + 1) * S5_CH, blk_b, 0.0)
        ktoe_ref[i * S5_CH:(i + 1) * S5_CH, :] = blk.astype(BF16)


def _s5_prep(lam_re, lam_im, log_step, b_re, b_im, c_re, c_im):
    def pair_vec(a):
        return a.astype(F32).reshape(2, S5_PAIRS, 1, S5_P).transpose(1, 0, 2, 3)

    step = jnp.broadcast_to(jnp.exp(log_step.astype(F32))[:, :, None], (2, N_SSM_GROUPS, SSM_STATE))
    vec = jnp.concatenate([pair_vec(lam_re), pair_vec(lam_im), pair_vec(step),
                           jnp.zeros((S5_PAIRS, 2, 5, S5_P), F32)], axis=2)

    def block_diag_ch_by_state(m):
        m = m.astype(F32).reshape(2, S5_PAIRS, 2, SSM_GROUP, SSM_STATE)
        eye = jnp.eye(2, dtype=F32)
        out = m[:, :, :, :, None, :] * eye[None, None, :, None, :, None]
        return out.reshape(2, S5_PAIRS, S5_CH, S5_P).transpose(1, 0, 2, 3)

    bt = jnp.stack([block_diag_ch_by_state(b_re.transpose(0, 1, 3, 2)),
                    block_diag_ch_by_state(b_im.transpose(0, 1, 3, 2))], axis=2)
    cc = jnp.stack([block_diag_ch_by_state(c_re), block_diag_ch_by_state(c_im)], axis=2)

    mat = jax.ShapeDtypeStruct((S5_PAIRS, S5_W, S5_W), BF16)
    mat_spec = pl.BlockSpec((None, S5_W, S5_W), lambda p: (p, 0, 0))
    return pl.pallas_call(
        _s5_prep_body,
        grid=(S5_PAIRS,),
        in_specs=[
            pl.BlockSpec((None, 2, 8, S5_P), lambda p: (p, 0, 0, 0)),
            pl.BlockSpec((None, 2, 2, S5_CH, S5_P), lambda p: (p, 0, 0, 0, 0)),
            pl.BlockSpec((None, 2, 2, S5_CH, S5_P), lambda p: (p, 0, 0, 0, 0)),
        ],
        out_specs=[mat_spec, mat_spec, mat_spec, pl.BlockSpec((None, 8, S5_P), lambda p: (p, 0, 0))],
        out_shape=[mat, mat, mat, jax.ShapeDtypeStruct((S5_PAIRS, 8, S5_P), F32)],
        scratch_shapes=[pltpu.VMEM((2, N_POW, S5_P), F32), pltpu.VMEM((S5_W, 2 * S5_P), F32)],
        compiler_params=_cparams(("parallel",)),
        name="s5_prep",
    )(vec, bt, cc)


def _s5_stream(u_ref, y_ref, n_chunks, bp, init, wend_ref, ktoe_ref, woutT_ref, laml_ref, z_ref, xs_ref):
    rows = n_chunks * bp
    u = u_ref[...]
    z_ref[0:rows, :] = _dot(u, wend_ref[...])
    fins = []
    for d in range(2):
        are = laml_ref[2 * d:2 * d + 1, :]
        aim = laml_ref[2 * d + 1:2 * d + 2, :]
        lr = slice(d * 256, d * 256 + 128)
        li = slice(d * 256 + 128, d * 256 + 256)

        def step(k, carry, d=d, are=are, aim=aim, lr=lr, li=li):
            xr, xi = carry
            c = k if d == 0 else n_chunks - 1 - k
            off = pl.multiple_of(c * bp, bp)
            xs_ref[pl.ds(off, bp), lr] = xr
            xs_ref[pl.ds(off, bp), li] = xi
            zr = z_ref[pl.ds(off, bp), lr]
            zi = z_ref[pl.ds(off, bp), li]
            return are * xr - aim * xi + zr, are * xi + aim * xr + zi

        fins.append(lax.fori_loop(0, n_chunks, step, init[d]))
    y = _dot(u, ktoe_ref[...]) + _dot_nt(xs_ref[0:rows, :].astype(BF16), woutT_ref[...])
    y_ref[...] = y
    return fins


def _s5_body(up_ref, us_ref, s0re_ref, s0im_ref, wend_ref, ktoe_ref, woutT_ref, laml_ref,
             yp_ref, ys_ref, finre_ref, finim_ref, z_ref, xs_ref):
    zero = jnp.zeros((BATCH, S5_P), F32)
    fins = _s5_stream(up_ref, yp_ref, SEQ // S5_CHUNK, BATCH, [(zero, zero), (zero, zero)],
                      wend_ref, ktoe_ref, woutT_ref, laml_ref, z_ref, xs_ref)
    for d in range(2):
        finre_ref[d] = fins[d][0]
        finim_ref[d] = fins[d][1]
    init = [(s0re_ref[d], s0im_ref[d]) for d in range(2)]
    _s5_stream(us_ref, ys_ref, DEC_SEQ // S5_CHUNK, S5_BP_SAMPLE, init,
               wend_ref, ktoe_ref, woutT_ref, laml_ref, z_ref, xs_ref)


def _s5_scan(u_p, u_s, s0_re, s0_im, wend, ktoe, woutT, laml):
    rows_p = (SEQ // S5_CHUNK) * BATCH
    rows_s = (DEC_SEQ // S5_CHUNK) * S5_BP_SAMPLE
    mat_spec = pl.BlockSpec((None, S5_W, S5_W), lambda p: (p, 0, 0))
    return pl.pallas_call(
        _s5_body,
        grid=(S5_PAIRS,),
        in_specs=[
            pl.BlockSpec((None, rows_p, S5_W), lambda p: (p, 0, 0)),
            pl.BlockSpec((None, rows_s, S5_W), lambda p: (p, 0, 0)),
            pl.BlockSpec((None, 2, S5_BP_SAMPLE, S5_P), lambda p: (p, 0, 0, 0)),
            pl.BlockSpec((None, 2, S5_BP_SAMPLE, S5_P), lambda p: (p, 0, 0, 0)),
            mat_spec, mat_spec, mat_spec,
            pl.BlockSpec((None, 8, S5_P), lambda p: (p, 0, 0)),
        ],
        out_specs=[
            pl.BlockSpec((None, rows_p, S5_W), lambda p: (p, 0, 0)),
            pl.BlockSpec((None, rows_s, S5_W), lambda p: (p, 0, 0)),
            pl.BlockSpec((None, 2, BATCH, S5_P), lambda p: (p, 0, 0, 0)),
            pl.BlockSpec((None, 2, BATCH, S5_P), lambda p: (p, 0, 0, 0)),
        ],
        out_shape=[
            jax.ShapeDtypeStruct((S5_PAIRS, rows_p, S5_W), F32),
            jax.ShapeDtypeStruct((S5_PAIRS, rows_s, S5_W), F32),
            jax.ShapeDtypeStruct((S5_PAIRS, 2, BATCH, S5_P), F32),
            jax.ShapeDtypeStruct((S5_PAIRS, 2, BATCH, S5_P), F32),
        ],
        scratch_shapes=[pltpu.VMEM((rows_s, S5_W), F32), pltpu.VMEM((rows_s, S5_W), F32)],
        compiler_params=_cparams(("parallel",)),
        name="s5_scan",
    )(u_p, u_s, s0_re, s0_im, wend, ktoe, woutT, laml)


def _to_s5_layout(u, b, t, bp):
    c = t // S5_CHUNK
    u = u.reshape(b, c, S5_CHUNK, S5_PAIRS, S5_CH).transpose(3, 1, 0, 2, 4)
    if bp != b:
        u = jnp.pad(u, ((0, 0), (0, 0), (0, bp - b), (0, 0), (0, 0)))
    return u.reshape(S5_PAIRS, c * bp, S5_W)


def _from_s5_layout(y, b, t, bp):
    c = t // S5_CHUNK
    y = y.reshape(S5_PAIRS, c, bp, S5_CHUNK, S5_CH)[:, :, :b]
    return y.transpose(2, 1, 3, 0, 4).reshape(b * t, HALF)


TM_GLU = 1024


def _glu_body(u_ref, ys_ref, d_ref, gw_ref, gb_ref, o_ref):
    y = d_ref[...] * u_ref[...].astype(F32) + ys_ref[...]
    cdf = 0.5 * (1.0 + jnp.tanh(math.sqrt(2.0 / math.pi) * (y + 0.044715 * (y * y * y))))
    y = y * cdf
    gate = jax.nn.sigmoid(_dot(y.astype(BF16), gw_ref[...]) + gb_ref[...])
    o_ref[...] = (y * gate).astype(BF16)


def _glu(proj, y_ssm, ssm_d, glu_w, glu_b):
    tm = TM_GLU
    return pl.pallas_call(
        _glu_body,
        grid=(N_TOK // tm,),
        in_specs=[
            pl.BlockSpec((tm, HALF), lambda i: (i, 3)),
            pl.BlockSpec((tm, HALF), lambda i: (i, 0)),
            pl.BlockSpec((1, HALF), lambda i: (0, 0)),
            pl.BlockSpec((HALF, HALF), lambda i: (0, 0)),
            pl.BlockSpec((1, HALF), lambda i: (0, 0)),
        ],
        out_specs=pl.BlockSpec((tm, HALF), lambda i: (i, 0)),
        out_shape=jax.ShapeDtypeStruct((N_TOK, HALF), BF16),
        compiler_params=_cparams(("parallel",)),
        name="s5_glu",
    )(proj, y_ssm, ssm_d.reshape(1, HALF), glu_w.astype(BF16), glu_b.reshape(1, HALF))


def kernel(x_prompt, x_sample, c, cache_k, cache_v, state_s5_re, state_s5_im, c_ctx, norm1_g, norm2_g, ada_w, ada_b, mlp_w1, mlp_w2, ab_w_in, pool_w, pool_scale, conv_w, ab_w_out, cd_w_in, na_rpb, ssm_lambda_re, ssm_lambda_im, ssm_log_step, ssm_b_re, ssm_b_im, ssm_c_re, ssm_c_im, ssm_d, glu_w, glu_b, cd_w_out, final_g):
    depth = ada_w.shape[0]
    x = jnp.concatenate([x_prompt.reshape(N_PROMPT_TOK, D_MODEL), x_sample.reshape(N_SAMPLE_TOK, D_MODEL)], axis=0)
    cond = jnp.concatenate([c_ctx[None, :], c, jnp.zeros((N_COND - 1 - DEC_BATCH, D_MODEL), F32)], axis=0)
    mods = _adaln(cond, ada_w, ada_b).reshape(depth, N_COND, 6, D_MODEL)

    new_k, new_v, new_re, new_im = [], [], [], []
    for layer in range(depth):
        i = layer // 2
        final = layer == depth - 1
        if layer % 2 == 0:
            proj = _inproj(x, mods, layer, norm1_g[layer], ab_w_in[i].astype(BF16))
            y = _even_mixer(proj, pool_w[i], pool_scale[i], conv_w[i])
            ya, ya_col, yb, yb_col = y, 0, y, 1
            w_out = ab_w_out[i]
        else:
            proj = _inproj(x, mods, layer, norm1_g[layer], cd_w_in[i].astype(BF16))
            o_p, ck, cv = _ctx_attn(proj)
            new_k.append(ck)
            new_v.append(cv)
            o_s = _na_attn(proj, cache_k, cache_v, _na_col_table(na_rpb[i]), i)
            attn = jnp.concatenate([o_p, o_s], axis=0)

            wend, ktoe, woutT, laml = _s5_prep(ssm_lambda_re[i], ssm_lambda_im[i], ssm_log_step[i],
                                               ssm_b_re[i], ssm_b_im[i], ssm_c_re[i], ssm_c_im[i])
            u = proj[:, 3 * HALF:]
            u_p = _to_s5_layout(u[:N_PROMPT_TOK], BATCH, SEQ, BATCH)
            u_s = _to_s5_layout(u[N_PROMPT_TOK:], DEC_BATCH, DEC_SEQ, S5_BP_SAMPLE)

            def pair_state(s):
                s = s.astype(F32).reshape(DEC_BATCH, 2, S5_PAIRS, S5_P).transpose(2, 1, 0, 3)
                return jnp.pad(s, ((0, 0), (0, 0), (0, S5_BP_SAMPLE - DEC_BATCH), (0, 0)))

            y_p, y_s, fin_re, fin_im = _s5_scan(u_p, u_s, pair_state(state_s5_re[:, i]), pair_state(state_s5_im[:, i]),
                                                wend, ktoe, woutT, laml)
            y_ssm = jnp.concatenate([_from_s5_layout(y_p, BATCH, SEQ, BATCH),
                                     _from_s5_layout(y_s, DEC_BATCH, DEC_SEQ, S5_BP_SAMPLE)], axis=0)
            new_re.append(fin_re.transpose(2, 1, 0, 3).reshape(BATCH, 2, N_SSM_GROUPS, SSM_STATE))
            new_im.append(fin_im.transpose(2, 1, 0, 3).reshape(BATCH, 2, N_SSM_GROUPS, SSM_STATE))
            d_out = _glu(proj, y_ssm, ssm_d[i], glu_w[i], glu_b[i])
            ya, ya_col, yb, yb_col = attn, 0, d_out, 0
            w_out = cd_w_out[i]
        x = _mlp(x, ya, ya_col, yb, yb_col, w_out.astype(BF16), mods, layer, norm2_g[layer], final_g,
                 mlp_w1[layer].astype(BF16), mlp_w2[layer].astype(BF16), final)

    y_prompt = x[:N_PROMPT_TOK].reshape(BATCH, SEQ, D_MODEL)
    y_sample = x[N_PROMPT_TOK:].reshape(DEC_BATCH, DEC_SEQ, D_MODEL)
    return (y_prompt, y_sample, jnp.concatenate(new_k, axis=1), jnp.concatenate(new_v, axis=1),
            jnp.stack(new_re, axis=1), jnp.stack(new_im, axis=1))
```

```python
import functools
import math

import jax
import jax.numpy as jnp
from jax import lax
from jax.experimental import pallas as pl
from jax.experimental.pallas import tpu as pltpu

F32 = jnp.float32
BF16 = jnp.bfloat16

D_MODEL = 1024
BATCH = 16
SEQ = 256
DEC_BATCH = 4
DEC_SEQ = 2048
PAST_LEN = 256
GRID_W = 64
GRID_ROWS = DEC_SEQ // GRID_W
HALF = 512
POOL_WINDOWS = (2, 4, 8, 16)
POOL_GROUP = 128
HEAD_DIM = 64
N_HEADS = 8
WIN_H = 8
WIN_W = 16
SSM_GROUP = 16
N_SSM_GROUPS = 32
SSM_STATE = 64
D_FF = 4096
EPS = 1e-6

N_PROMPT_TOK = BATCH * SEQ
N_SAMPLE_TOK = DEC_BATCH * DEC_SEQ
N_TOK = N_PROMPT_TOK + N_SAMPLE_TOK
N_COND = 8

S5_CHUNK = 16
S5_PAIRS = N_SSM_GROUPS // 2
S5_CH = 2 * SSM_GROUP
S5_P = 2 * SSM_STATE
S5_W = S5_CHUNK * S5_CH
S5_BP_SAMPLE = 8

VMEM_LIMIT = 56 * 1024 * 1024


def _cparams(sem):
    return pltpu.CompilerParams(dimension_semantics=sem, vmem_limit_bytes=VMEM_LIMIT)


def _cond_row(i, tm):
    npt = N_PROMPT_TOK // tm
    per = DEC_SEQ // tm
    return jnp.where(i < npt, 0, 1 + (i - npt) // per)


def _normmod(x, g, shift, scale):
    ms = jnp.mean(x * x, axis=-1, keepdims=True)
    return (x * lax.rsqrt(ms + EPS) * g) * (1.0 + scale) + shift


def _dot(a, b):
    return jnp.dot(a, b, preferred_element_type=F32)


def _dot_nt(a, b, precision=None):
    return lax.dot_general(a, b, (((1,), (1,)), ((), ())), preferred_element_type=F32, precision=precision)


def _adaln_body(c_ref, w_ref, b_ref, o_ref):
    c = c_ref[...]
    s = c * jax.nn.sigmoid(c)
    o_ref[...] = _dot(s.astype(BF16), w_ref[...].astype(BF16)) + b_ref[...]


def _adaln(cond, ada_w, ada_b):
    depth = ada_w.shape[0]
    nj = 6
    return pl.pallas_call(
        _adaln_body,
        grid=(depth, nj),
        in_specs=[
            pl.BlockSpec((N_COND, D_MODEL), lambda l, j: (0, 0)),
            pl.BlockSpec((None, D_MODEL, D_MODEL), lambda l, j: (l, 0, j)),
            pl.BlockSpec((None, 1, D_MODEL), lambda l, j: (l, 0, j)),
        ],
        out_specs=pl.BlockSpec((None, N_COND, D_MODEL), lambda l, j: (l, 0, j)),
        out_shape=jax.ShapeDtypeStruct((depth, N_COND, 6 * D_MODEL), F32),
        compiler_params=_cparams(("arbitrary", "arbitrary")),
        name="adaln",
    )(cond, ada_w, ada_b.reshape(depth, 1, 6 * D_MODEL))


TM_PROJ = 512


def _inproj_body(x_ref, m_ref, g_ref, w_ref, o_ref):
    m = m_ref[...]
    h = _normmod(x_ref[...], g_ref[...], m[0:1], m[1:2])
    o_ref[...] = _dot(h.astype(BF16), w_ref[...]).astype(BF16)


def _inproj(x, mods, layer, g, w):
    tm = TM_PROJ
    n_out = w.shape[1]
    return pl.pallas_call(
        _inproj_body,
        grid=(N_TOK // tm,),
        in_specs=[
            pl.BlockSpec((tm, D_MODEL), lambda i: (i, 0)),
            pl.BlockSpec((None, None, 6, D_MODEL), lambda i: (layer, _cond_row(i, tm), 0, 0)),
            pl.BlockSpec((1, D_MODEL), lambda i: (0, 0)),
            pl.BlockSpec((D_MODEL, n_out), lambda i: (0, 0)),
        ],
        out_specs=pl.BlockSpec((tm, n_out), lambda i: (i, 0)),
        out_shape=jax.ShapeDtypeStruct((N_TOK, n_out), BF16),
        compiler_params=_cparams(("parallel",)),
        name="inproj",
    )(x, mods, g.reshape(1, D_MODEL), w)


N_CHUNK_ROWS = N_TOK // S5_CHUNK
LANE_TILES = S5_W // 128


def _inproj_odd_body(x_ref, m_ref, g_ref, w_ref, qkv_ref, tu_ref, u_ref):
    m = m_ref[...]
    h = _normmod(x_ref[...], g_ref[...], m[0:1], m[1:2])
    acc = _dot(h.astype(BF16), w_ref[...])
    qkv_ref[...] = acc[:, 0:3 * HALF].astype(BF16)
    for t in range(LANE_TILES):
        u_ref[t] = acc[:, 3 * HALF + t * 128:3 * HALF + (t + 1) * 128]
    rows = TM_PROJ // S5_CHUNK
    for j in range(S5_CHUNK):
        uj = jnp.concatenate([u_ref[t, pl.ds(j, rows, stride=S5_CHUNK), :] for t in range(LANE_TILES)], axis=1)
        if j:
            uj = pltpu.roll(uj, j * S5_CH, 1)
        for t in range(LANE_TILES):
            tu_ref[j, t] = uj[:, t * 128:(t + 1) * 128].astype(BF16)


def _inproj_odd(x, mods, layer, g, w):
    tm = TM_PROJ
    rows = tm // S5_CHUNK
    return pl.pallas_call(
        _inproj_odd_body,
        grid=(N_TOK // tm,),
        in_specs=[
            pl.BlockSpec((tm, D_MODEL), lambda i: (i, 0)),
            pl.BlockSpec((None, None, 6, D_MODEL), lambda i: (layer, _cond_row(i, tm), 0, 0)),
            pl.BlockSpec((1, D_MODEL), lambda i: (0, 0)),
            pl.BlockSpec((D_MODEL, 4 * HALF), lambda i: (0, 0)),
        ],
        out_specs=[
            pl.BlockSpec((tm, 3 * HALF), lambda i: (i, 0)),
            pl.BlockSpec((S5_CHUNK, LANE_TILES, rows, 128), lambda i: (0, 0, i, 0)),
        ],
        out_shape=[
            jax.ShapeDtypeStruct((N_TOK, 3 * HALF), BF16),
            jax.ShapeDtypeStruct((S5_CHUNK, LANE_TILES, N_CHUNK_ROWS, 128), BF16),
        ],
        scratch_shapes=[pltpu.VMEM((LANE_TILES, tm, 128), F32)],
        compiler_params=_cparams(("parallel",)),
        name="inproj_odd",
    )(x, mods, g.reshape(1, D_MODEL), w)


TM_MIX = 256
HALO = 16


def _even_mixer_body(p_ref, prev_ref, next_ref, pw_ref, ps_ref, cw_ref, y_ref, pad_ref):
    i = pl.program_id(0)
    n_prompt_tiles = N_PROMPT_TOK // TM_MIX
    tiles_per_seq = DEC_SEQ // TM_MIX
    is_prompt = i < n_prompt_tiles
    tile_in_seq = jnp.where(is_prompt, 0, (i - n_prompt_tiles) % tiles_per_seq)
    seq_len = jnp.where(is_prompt, SEQ, DEC_SEQ)
    has_prev = tile_in_seq > 0
    has_next = jnp.logical_and(jnp.logical_not(is_prompt), tile_in_seq < tiles_per_seq - 1)
    prev_keep = jnp.where(has_prev, 1.0, 0.0).astype(F32)
    next_keep = jnp.where(has_next, 1.0, 0.0).astype(F32)
    pos = tile_in_seq * TM_MIX + lax.broadcasted_iota(jnp.int32, (TM_MIX, 1), 0)
    lo_rows, hi_rows = HALO, HALO + TM_MIX

    pad_ref[0:HALO, :] = prev_ref[:, 0:HALF].astype(F32) * prev_keep
    pad_ref[lo_rows:hi_rows, :] = p_ref[:, 0:HALF].astype(F32)
    pad_ref[hi_rows:hi_rows + HALO, :] = next_ref[:, 0:HALF].astype(F32) * next_keep
    for gi, w in enumerate(POOL_WINDOWS):
        cs = slice(gi * POOL_GROUP, (gi + 1) * POOL_GROUP)
        s = None
        for o in range(-(w // 2), w - w // 2):
            t = pad_ref[lo_rows + o:hi_rows + o, cs]
            s = t if s is None else s + t
        lo = jnp.maximum(pos - w // 2, 0)
        hi = jnp.minimum(pos + (w - w // 2), seq_len)
        inv = 1.0 / (hi - lo).astype(F32)
        pooled = s * inv - pad_ref[lo_rows:hi_rows, cs]
        mixed = _dot(pooled.astype(BF16), pw_ref[gi]) * ps_ref[:, cs]
        y_ref[:, cs] = mixed.astype(BF16)

    c0, v0 = 2 * HALF, 3 * HALF
    pad_ref[0:HALO, :] = (prev_ref[:, c0:c0 + HALF].astype(F32) * prev_ref[:, v0:v0 + HALF].astype(F32)) * prev_keep
    pad_ref[lo_rows:hi_rows, :] = p_ref[:, c0:c0 + HALF].astype(F32) * p_ref[:, v0:v0 + HALF].astype(F32)
    pad_ref[hi_rows:hi_rows + HALO, :] = (next_ref[:, c0:c0 + HALF].astype(F32)
                                          * next_ref[:, v0:v0 + HALF].astype(F32)) * next_keep
    for ci in range(HALF // 128):
        cs = slice(ci * 128, (ci + 1) * 128)
        conv = (cw_ref[0:1, cs] * pad_ref[lo_rows - 1:hi_rows - 1, cs]
                + cw_ref[1:2, cs] * pad_ref[lo_rows:hi_rows, cs]
                + cw_ref[2:3, cs] * pad_ref[lo_rows + 1:hi_rows + 1, cs])
        bg = p_ref[:, HALF + ci * 128:HALF + (ci + 1) * 128].astype(F32)
        y_ref[:, HALF + ci * 128:HALF + (ci + 1) * 128] = (bg * conv).astype(BF16)


def _even_mixer(proj, pool_w, pool_scale, conv_w):
    n_tiles = N_TOK // TM_MIX
    per = TM_MIX // HALO
    n_halo_blocks = N_TOK // HALO
    return pl.pallas_call(
        _even_mixer_body,
        grid=(n_tiles,),
        in_specs=[
            pl.BlockSpec((TM_MIX, 4 * HALF), lambda i: (i, 0)),
            pl.BlockSpec((HALO, 4 * HALF), lambda i: (jnp.maximum(i * per - 1, 0), 0)),
            pl.BlockSpec((HALO, 4 * HALF), lambda i: (jnp.minimum((i + 1) * per, n_halo_blocks - 1), 0)),
            pl.BlockSpec((len(POOL_WINDOWS), POOL_GROUP, POOL_GROUP), lambda i: (0, 0, 0)),
            pl.BlockSpec((1, HALF), lambda i: (0, 0)),
            pl.BlockSpec((3, HALF), lambda i: (0, 0)),
        ],
        out_specs=pl.BlockSpec((TM_MIX, 2 * HALF), lambda i: (i, 0)),
        out_shape=jax.ShapeDtypeStruct((N_TOK, 2 * HALF), BF16),
        scratch_shapes=[pltpu.VMEM((TM_MIX + 2 * HALO, HALF), F32)],
        compiler_params=_cparams(("parallel",)),
        name="even_mixer",
    )(proj, proj, proj, pool_w.astype(BF16), pool_scale.reshape(1, HALF), conv_w)


TM_MLP = 1024
TF_MLP = 512


def _mlp_body(final, x_ref, ya_ref, yb_ref, wo_ref, m_ref, g_ref, fg_ref, w1_ref, w2_ref, o_ref,
              x1_ref, h_ref, acc_ref):
    j = pl.program_id(1)

    @pl.when(j == 0)
    def _():
        m = m_ref[...]
        mix = _dot(ya_ref[...], wo_ref[0:HALF, :]) + _dot(yb_ref[...], wo_ref[HALF:2 * HALF, :])
        x1 = x_ref[...] + m[2:3] * mix
        x1_ref[...] = x1
        h_ref[...] = _normmod(x1, g_ref[...], m[3:4], m[4:5]).astype(BF16)
        acc_ref[...] = jnp.zeros_like(acc_ref)

    t = _dot(h_ref[...], w1_ref[...])
    t = jnp.square(jnp.maximum(t, 0.0)).astype(BF16)
    acc_ref[...] += _dot(t, w2_ref[...])

    @pl.when(j == pl.num_programs(1) - 1)
    def _():
        out = x1_ref[...] + m_ref[5:6, :] * acc_ref[...]
        if final:
            ms = jnp.mean(out * out, axis=-1, keepdims=True)
            out = out * lax.rsqrt(ms + EPS) * fg_ref[...]
        o_ref[...] = out


def _mlp(x, ya, ya_col, yb, yb_col, w_out, mods, layer, g2, final_g, w1, w2, final):
    tm, tf = TM_MLP, TF_MLP
    return pl.pallas_call(
        functools.partial(_mlp_body, final),
        grid=(N_TOK // tm, D_FF // tf),
        in_specs=[
            pl.BlockSpec((tm, D_MODEL), lambda i, j: (i, 0)),
            pl.BlockSpec((tm, HALF), lambda i, j: (i, ya_col)),
            pl.BlockSpec((tm, HALF), lambda i, j: (i, yb_col)),
            pl.BlockSpec((D_MODEL, D_MODEL), lambda i, j: (0, 0)),
            pl.BlockSpec((None, None, 6, D_MODEL), lambda i, j: (layer, _cond_row(i, tm), 0, 0)),
            pl.BlockSpec((1, D_MODEL), lambda i, j: (0, 0)),
            pl.BlockSpec((1, D_MODEL), lambda i, j: (0, 0)),
            pl.BlockSpec((D_MODEL, tf), lambda i, j: (0, j)),
            pl.BlockSpec((tf, D_MODEL), lambda i, j: (j, 0)),
        ],
        out_specs=pl.BlockSpec((tm, D_MODEL), lambda i, j: (i, 0)),
        out_shape=jax.ShapeDtypeStruct((N_TOK, D_MODEL), F32),
        scratch_shapes=[
            pltpu.VMEM((tm, D_MODEL), F32),
            pltpu.VMEM((tm, D_MODEL), BF16),
            pltpu.VMEM((tm, D_MODEL), F32),
        ],
        compiler_params=_cparams(("parallel", "arbitrary")),
        name="mlp",
    )(x, ya, yb, w_out, mods, g2.reshape(1, D_MODEL), final_g.reshape(1, D_MODEL), w1, w2)


def _ctx_attn_body(q_ref, k_ref, v_ref, o_ref, ck_ref, cv_ref):
    scale = HEAD_DIM ** -0.5
    for h in range(N_HEADS):
        sl = slice(h * HEAD_DIM, (h + 1) * HEAD_DIM)
        q = q_ref[:, sl]
        k = k_ref[:, sl]
        v = v_ref[:, sl]
        ck_ref[h] = k.astype(F32)
        cv_ref[h] = v.astype(F32)
        s = _dot_nt(q, k) * scale
        m = jnp.max(s, axis=-1, keepdims=True)
        p = jnp.exp(s - m)
        l = jnp.sum(p, axis=-1, keepdims=True)
        o = _dot(p.astype(BF16), v) / l
        o_ref[:, sl] = o.astype(BF16)


def _ctx_attn(proj):
    return pl.pallas_call(
        _ctx_attn_body,
        grid=(BATCH,),
        in_specs=[
            pl.BlockSpec((SEQ, HALF), lambda b: (b, 0)),
            pl.BlockSpec((SEQ, HALF), lambda b: (b, 1)),
            pl.BlockSpec((SEQ, HALF), lambda b: (b, 2)),
        ],
        out_specs=[
            pl.BlockSpec((SEQ, HALF), lambda b: (b, 0)),
            pl.BlockSpec((None, None, N_HEADS, SEQ, HEAD_DIM), lambda b: (b, 0, 0, 0, 0)),
            pl.BlockSpec((None, None, N_HEADS, SEQ, HEAD_DIM), lambda b: (b, 0, 0, 0, 0)),
        ],
        out_shape=[
            jax.ShapeDtypeStruct((N_PROMPT_TOK, HALF), BF16),
            jax.ShapeDtypeStruct((BATCH, 1, N_HEADS, SEQ, HEAD_DIM), F32),
            jax.ShapeDtypeStruct((BATCH, 1, N_HEADS, SEQ, HEAD_DIM), F32),
        ],
        compiler_params=_cparams(("parallel",)),
        name="ctx_attn",
    )(proj, proj, proj)


NA_MASKED = -1e30
NA_QROWS = 4
NA_KROWS = NA_QROWS + WIN_H
NA_NQ = NA_QROWS * GRID_W
NA_NK = NA_KROWS * GRID_W
NA_BLOCKS = GRID_ROWS // NA_QROWS
NA_TYPES = 3


def _na_col_table(rpb):
    qc = jnp.arange(GRID_W)[:, None]
    kc = jnp.arange(GRID_W)[None, :]
    c0 = jnp.clip(qc - WIN_W // 2, 0, GRID_W - WIN_W)
    valid = (kc >= c0) & (kc < c0 + WIN_W)
    coff = jnp.clip(kc - qc, -(WIN_W - 1), WIN_W - 1) + (WIN_W - 1)
    onehot = (coff.reshape(1, -1) == jnp.arange(2 * WIN_W - 1)[:, None]).astype(F32)
    t = jnp.dot(rpb.astype(F32).reshape(N_HEADS * (2 * WIN_H - 1), 2 * WIN_W - 1), onehot,
                precision=lax.Precision.HIGHEST)
    t = t.reshape(N_HEADS, 2 * WIN_H - 1, GRID_W, GRID_W)
    return jnp.where(valid[None, None], t, NA_MASKED)


def _na_window_start(rb):
    return jnp.clip(NA_QROWS * rb - WIN_H // 2, 0, GRID_ROWS - NA_KROWS)


def _na_body(q_ref, k_ref, v_ref, ck_ref, cv_ref, tab_ref, o_ref, bias_ref):
    @pl.when(pl.program_id(1) == 0)
    def _():
        masked = jnp.full((GRID_W, GRID_W), NA_MASKED, F32)
        for hh in range(2):
            for ty in range(NA_TYPES):
                q0 = (0, NA_QROWS, GRID_ROWS - NA_QROWS)[ty]
                ks = (0, 0, GRID_ROWS - NA_KROWS)[ty]
                for a in range(NA_QROWS):
                    r0 = min(max(q0 + a - WIN_H // 2, 0), GRID_ROWS - WIN_H)
                    for y in range(NA_KROWS):
                        kr = ks + y
                        ok = r0 <= kr < r0 + WIN_H
                        blk = tab_ref[hh, kr - (q0 + a) + WIN_H - 1] if ok else masked
                        bias_ref[hh, ty, a * GRID_W:(a + 1) * GRID_W, y * GRID_W:(y + 1) * GRID_W] = blk

    for hh in range(2):
        sl = slice(hh * HEAD_DIM, (hh + 1) * HEAD_DIM)
        kc = ck_ref[hh].astype(BF16)
        vc = cv_ref[hh].astype(BF16)

        def block(rb, ty, sl=sl, kc=kc, vc=vc, hh=hh):
            qs = pl.multiple_of(rb * NA_NQ, NA_NQ)
            ks = pl.multiple_of(_na_window_start(rb) * GRID_W, GRID_W)
            q = q_ref[pl.ds(qs, NA_NQ), sl] * jnp.asarray(HEAD_DIM ** -0.5, BF16)
            kl = k_ref[pl.ds(ks, NA_NK), sl]
            vl = v_ref[pl.ds(ks, NA_NK), sl]
            s_loc = _dot_nt(q, kl) + bias_ref[hh, ty]
            s_ctx = _dot_nt(q, kc)
            m = jnp.maximum(jnp.max(s_loc, axis=-1, keepdims=True), jnp.max(s_ctx, axis=-1, keepdims=True))
            p_loc = jnp.exp(s_loc - m)
            p_ctx = jnp.exp(s_ctx - m)
            l = jnp.sum(p_loc, axis=-1, keepdims=True) + jnp.sum(p_ctx, axis=-1, keepdims=True)
            o = (_dot(p_loc.astype(BF16), vl) + _dot(p_ctx.astype(BF16), vc)) / l
            o_ref[pl.ds(qs, NA_NQ), sl] = o.astype(BF16)

        block(0, 0)

        def interior(rb, carry, block=block):
            block(rb, 1)
            return carry

        lax.fori_loop(1, NA_BLOCKS - 1, interior, 0)
        block(NA_BLOCKS - 1, 2)


def _na_attn(proj, cache_k, cache_v, col_table, layer_idx):
    row_blk0 = N_PROMPT_TOK // DEC_SEQ
    hp = N_HEADS // 2
    return pl.pallas_call(
        _na_body,
        grid=(hp, DEC_BATCH),
        in_specs=[
            pl.BlockSpec((DEC_SEQ, 128), lambda h, b: (row_blk0 + b, h)),
            pl.BlockSpec((DEC_SEQ, 128), lambda h, b: (row_blk0 + b, hp + h)),
            pl.BlockSpec((DEC_SEQ, 128), lambda h, b: (row_blk0 + b, 2 * hp + h)),
            pl.BlockSpec((None, None, 2, PAST_LEN, HEAD_DIM), lambda h, b: (b, layer_idx, h, 0, 0)),
            pl.BlockSpec((None, None, 2, PAST_LEN, HEAD_DIM), lambda h, b: (b, layer_idx, h, 0, 0)),
            pl.BlockSpec((2, 2 * WIN_H - 1, GRID_W, GRID_W), lambda h, b: (h, 0, 0, 0)),
        ],
        out_specs=pl.BlockSpec((DEC_SEQ, 128), lambda h, b: (b, h)),
        out_shape=jax.ShapeDtypeStruct((N_SAMPLE_TOK, HALF), BF16),
        scratch_shapes=[pltpu.VMEM((2, NA_TYPES, NA_NQ, NA_NK), F32)],
        compiler_params=_cparams(("parallel", "arbitrary")),
        name="na_attn",
    )(proj, proj, proj, cache_k, cache_v, col_table)


N_POW = 24


def _s5_prep_body(vec_ref, bt_ref, c_ref, wend_ref, ktoe_ref, woutT_ref, apow_ref, pw_ref, pct_ref):
    L = S5_CHUNK
    p = pl.program_id(0)
    lane_blk = lax.broadcasted_iota(jnp.int32, (S5_CH, S5_W), 1) // S5_CH
    src_blk = (lane_blk + L - p) % L
    krows = []
    for d in range(2):
        v = vec_ref[d]
        lre, lim, step = v[0:1], v[1:2], v[2:3]
        a = lre * step
        b = lim * step
        ea = jnp.exp(a)
        nr = ea * jnp.cos(b) - 1.0
        ni = ea * jnp.sin(b)
        den = lre * lre + lim * lim
        qr = (nr * lre + ni * lim) / den
        qi = (ni * lre - nr * lim) / den
        bre, bim = bt_ref[d, 0], bt_ref[d, 1]
        bqr = bre * qr - bim * qi
        bqi = bre * qi + bim * qr
        cre, cim = c_ref[d, 0], c_ref[d, 1]

        e = lax.broadcasted_iota(jnp.int32, (N_POW, 1), 0).astype(F32)
        mag = jnp.exp(e * a)
        pw_ref[0] = mag * jnp.cos(e * b)
        pw_ref[1] = mag * jnp.sin(e * b)
        mag = jnp.exp((L * e) * a)
        apow_ref[d, 0] = mag * jnp.cos((L * e) * b)
        apow_ref[d, 1] = mag * jnp.sin((L * e) * b)

        def power(k):
            return pw_ref[0, k:k + 1, :], pw_ref[1, k:k + 1, :]

        def c_block(k):
            pr, pi = power(k)
            return cre * pr - cim * pi, -(cre * pi + cim * pr)

        for i in range(L):
            rows = pl.ds(pl.multiple_of(((i + p) % L) * S5_CH, S5_CH), S5_CH)
            pr, pi = power(L - 1 - i if d == 0 else i)
            wend_ref[rows, d * 256:d * 256 + 128] = (bqr * pr - bqi * pi).astype(BF16)
            wend_ref[rows, d * 256 + 128:d * 256 + 256] = (bqr * pi + bqi * pr).astype(BF16)
            mre, mim = c_block(i + 1 if d == 0 else L - i)
            woutT_ref[rows, d * 256:d * 256 + 128] = mre.astype(BF16)
            woutT_ref[rows, d * 256 + 128:d * 256 + 256] = mim.astype(BF16)
            lag_rows = slice(i * S5_CH, (i + 1) * S5_CH)
            mre, mim = c_block(i if d == 0 else L - 1 - i)
            pct_ref[lag_rows, 0:128] = mre
            pct_ref[lag_rows, 128:256] = mim

        bqcat = jnp.concatenate([bqr, bqi], axis=1)
        krows.append(_dot_nt(bqcat, pct_ref[...], precision=lax.Precision.HIGHEST))

    for i in range(L):
        rows = pl.ds(pl.multiple_of(((i + p) % L) * S5_CH, S5_CH), S5_CH)
        blk_f = pltpu.roll(krows[0], ((i + p) % L) * S5_CH, 1)
        blk_b = pltpu.roll(krows[1], ((i + 1 + p) % L) * S5_CH, 1)
        blk = jnp.where(src_blk >= i, blk_f, 0.0) + jnp.where(src_blk <= i, blk_b, 0.0)
        ktoe_ref[rows, :] = blk.astype(BF16)


def _s5_prep(lam_re, lam_im, log_step, b_re, b_im, c_re, c_im):
    def pair_vec(a):
        return a.astype(F32).reshape(2, S5_PAIRS, 1, S5_P).transpose(1, 0, 2, 3)

    step = jnp.broadcast_to(jnp.exp(log_step.astype(F32))[:, :, None], (2, N_SSM_GROUPS, SSM_STATE))
    vec = jnp.concatenate([pair_vec(lam_re), pair_vec(lam_im), pair_vec(step),
                           jnp.zeros((S5_PAIRS, 2, 5, S5_P), F32)], axis=2)

    def block_diag_ch_by_state(m):
        m = m.astype(F32).reshape(2, S5_PAIRS, 2, SSM_GROUP, SSM_STATE)
        eye = jnp.eye(2, dtype=F32)
        out = m[:, :, :, :, None, :] * eye[None, None, :, None, :, None]
        return out.reshape(2, S5_PAIRS, S5_CH, S5_P).transpose(1, 0, 2, 3)

    bt = jnp.stack([block_diag_ch_by_state(b_re.transpose(0, 1, 3, 2)),
                    block_diag_ch_by_state(b_im.transpose(0, 1, 3, 2))], axis=2)
    cc = jnp.stack([block_diag_ch_by_state(c_re), block_diag_ch_by_state(c_im)], axis=2)

    mat = jax.ShapeDtypeStruct((S5_PAIRS, S5_W, S5_W), BF16)
    mat_spec = pl.BlockSpec((None, S5_W, S5_W), lambda p: (p, 0, 0))
    return pl.pallas_call(
        _s5_prep_body,
        grid=(S5_PAIRS,),
        in_specs=[
            pl.BlockSpec((None, 2, 8, S5_P), lambda p: (p, 0, 0, 0)),
            pl.BlockSpec((None, 2, 2, S5_CH, S5_P), lambda p: (p, 0, 0, 0, 0)),
            pl.BlockSpec((None, 2, 2, S5_CH, S5_P), lambda p: (p, 0, 0, 0, 0)),
        ],
        out_specs=[mat_spec, mat_spec, mat_spec,
                   pl.BlockSpec((None, 2, 2, N_POW, S5_P), lambda p: (p, 0, 0, 0, 0))],
        out_shape=[mat, mat, mat, jax.ShapeDtypeStruct((S5_PAIRS, 2, 2, N_POW, S5_P), F32)],
        scratch_shapes=[pltpu.VMEM((2, N_POW, S5_P), F32), pltpu.VMEM((S5_W, 2 * S5_P), F32)],
        compiler_params=_cparams(("parallel",)),
        name="s5_prep",
    )(vec, bt, cc)


S5_SEG_ROWS = 256
S5_N_SEG = N_CHUNK_ROWS // S5_SEG_ROWS
S5_SUB = S5_SEG_ROWS // S5_CHUNK
S5_SUB_PER_SEQ = (DEC_SEQ // S5_CHUNK) // S5_CHUNK


def _s5_body(tu_ref, s0_ref, wend_ref, ktoe_ref, woutT_ref, apow_ref, ty_ref, fin_ref, z_ref, xs_ref, g_ref):
    seg = pl.program_id(0)
    p = pl.program_id(1)
    L = S5_CHUNK
    lane_blk = lax.broadcasted_iota(jnp.int32, (S5_SEG_ROWS, 128), 1) // S5_CH

    tiles = []
    for t in range(LANE_TILES):
        tile = None
        for k in range(128 // S5_CH):
            j = (LANE_TILES * t + k + L - p) % L
            blk = tu_ref[j, t]
            tile = blk if tile is None else jnp.where(lane_blk == k, blk, tile)
        tiles.append(tile)
    s = jnp.concatenate(tiles, axis=1)
    z = _dot(s, wend_ref[...])
    for q in range(LANE_TILES):
        z_ref[q] = z[:, q * 128:(q + 1) * 128]

    ends = []
    for d in range(2):
        are, aim = apow_ref[d, 0, 1:2, :], apow_ref[d, 1, 1:2, :]
        xr = jnp.zeros((S5_SUB, S5_P), F32)
        xi = jnp.zeros((S5_SUB, S5_P), F32)
        for c in (range(L) if d == 0 else reversed(range(L))):
            rows = pl.ds(c, S5_SUB, stride=L)
            xs_ref[2 * d, rows, :] = xr
            xs_ref[2 * d + 1, rows, :] = xi
            zr = z_ref[2 * d, rows, :]
            zi = z_ref[2 * d + 1, rows, :]
            xr, xi = are * xr - aim * xi + zr, are * xi + aim * xr + zi
        ends.append((xr, xi))
        fin_ref[d, 0] = xr
        fin_ref[d, 1] = xi

    @pl.when(seg > 0)
    def _():
        for d in range(2):
            er, ei = ends[d]
            a16r, a16i = apow_ref[d, 0, L:L + 1, :], apow_ref[d, 1, L:L + 1, :]
            for bb in range(S5_SUB // S5_SUB_PER_SEQ):
                gr = s0_ref[bb, 2 * d:2 * d + 1, :]
                gi = s0_ref[bb, 2 * d + 1:2 * d + 2, :]
                order = range(S5_SUB_PER_SEQ) if d == 0 else reversed(range(S5_SUB_PER_SEQ))
                prev = None
                for sub in order:
                    row = bb * S5_SUB_PER_SEQ + sub
                    if prev is not None:
                        gr, gi = (a16r * gr - a16i * gi + er[prev:prev + 1, :],
                                  a16r * gi + a16i * gr + ei[prev:prev + 1, :])
                    g_ref[0, row:row + 1, :] = gr
                    g_ref[1, row:row + 1, :] = gi
                    prev = row
            gr, gi = g_ref[0], g_ref[1]
            for c in range(L):
                k = c if d == 0 else L - 1 - c
                pr, pi = apow_ref[d, 0, k:k + 1, :], apow_ref[d, 1, k:k + 1, :]
                rows = pl.ds(c, S5_SUB, stride=L)
                xs_ref[2 * d, rows, :] = xs_ref[2 * d, rows, :] + (pr * gr - pi * gi)
                xs_ref[2 * d + 1, rows, :] = xs_ref[2 * d + 1, rows, :] + (pr * gi + pi * gr)

    xs = jnp.concatenate([xs_ref[q].astype(BF16) for q in range(LANE_TILES)], axis=1)
    y = _dot(s, ktoe_ref[...]) + _dot_nt(xs, woutT_ref[...])

    @pl.when(p == 0)
    def _():
        ty_ref[...] = jnp.zeros_like(ty_ref)

    for t in range(LANE_TILES):
        yt = y[:, t * 128:(t + 1) * 128]
        for k in range(128 // S5_CH):
            j = (LANE_TILES * t + k + L - p) % L
            ty_ref[j, t] = jnp.where(lane_blk == k, yt, ty_ref[j, t])


def _s5_scan(tu, s0, wend, ktoe, woutT, apow):
    mat_spec = pl.BlockSpec((None, S5_W, S5_W), lambda g, p: (p, 0, 0))
    seqs = S5_SUB // S5_SUB_PER_SEQ
    return pl.pallas_call(
        _s5_body,
        grid=(S5_N_SEG, S5_PAIRS),
        in_specs=[
            pl.BlockSpec((S5_CHUNK, LANE_TILES, S5_SEG_ROWS, 128), lambda g, p: (0, 0, g, 0)),
            pl.BlockSpec((None, seqs, 4, S5_P), lambda g, p: (p, jnp.maximum(g - 1, 0), 0, 0)),
            mat_spec, mat_spec, mat_spec,
            pl.BlockSpec((None, 2, 2, N_POW, S5_P), lambda g, p: (p, 0, 0, 0, 0)),
        ],
        out_specs=[
            pl.BlockSpec((S5_CHUNK, LANE_TILES, S5_SEG_ROWS, 128), lambda g, p: (0, 0, g, 0)),
            pl.BlockSpec((None, 2, 2, S5_SUB, S5_P), lambda g, p: (p, 0, 0, g, 0)),
        ],
        out_shape=[
            jax.ShapeDtypeStruct((S5_CHUNK, LANE_TILES, N_CHUNK_ROWS, 128), F32),
            jax.ShapeDtypeStruct((S5_PAIRS, 2, 2, S5_N_SEG * S5_SUB, S5_P), F32),
        ],
        scratch_shapes=[pltpu.VMEM((LANE_TILES, S5_SEG_ROWS, 128), F32),
                        pltpu.VMEM((LANE_TILES, S5_SEG_ROWS, 128), F32),
                        pltpu.VMEM((2, S5_SUB, S5_P), F32)],
        compiler_params=_cparams(("parallel", "arbitrary")),
        name="s5_scan",
    )(tu, s0, wend, ktoe, woutT, apow)


TM_GLU = 1024


def _glu_body(tu_ref, ty_ref, d_ref, gw_ref, gb_ref, o_ref, u_ref, ys_ref):
    rows = TM_GLU // S5_CHUNK
    for j in range(S5_CHUNK):
        back = ((S5_CHUNK - j) % S5_CHUNK) * S5_CH
        uj = jnp.concatenate([tu_ref[j, t].astype(F32) for t in range(LANE_TILES)], axis=1)
        yj = jnp.concatenate([ty_ref[j, t] for t in range(LANE_TILES)], axis=1)
        if back:
            uj = pltpu.roll(uj, back, 1)
            yj = pltpu.roll(yj, back, 1)
        for t in range(LANE_TILES):
            u_ref[t, pl.ds(j, rows, stride=S5_CHUNK), :] = uj[:, t * 128:(t + 1) * 128]
            ys_ref[t, pl.ds(j, rows, stride=S5_CHUNK), :] = yj[:, t * 128:(t + 1) * 128]
    u = jnp.concatenate([u_ref[t] for t in range(LANE_TILES)], axis=1)
    y = d_ref[...] * u + jnp.concatenate([ys_ref[t] for t in range(LANE_TILES)], axis=1)
    cdf = 0.5 * (1.0 + jnp.tanh(math.sqrt(2.0 / math.pi) * (y + 0.044715 * (y * y * y))))
    y = y * cdf
    gate = jax.nn.sigmoid(_dot(y.astype(BF16), gw_ref[...]) + gb_ref[...])
    o_ref[...] = (y * gate).astype(BF16)


def _glu(tu, ty, ssm_d, glu_w, glu_b):
    tm = TM_GLU
    rows = tm // S5_CHUNK
    lay_spec = pl.BlockSpec((S5_CHUNK, LANE_TILES, rows, 128), lambda i: (0, 0, i, 0))
    return pl.pallas_call(
        _glu_body,
        grid=(N_TOK // tm,),
        in_specs=[
            lay_spec, lay_spec,
            pl.BlockSpec((1, HALF), lambda i: (0, 0)),
            pl.BlockSpec((HALF, HALF), lambda i: (0, 0)),
            pl.BlockSpec((1, HALF), lambda i: (0, 0)),
        ],
        out_specs=pl.BlockSpec((tm, HALF), lambda i: (i, 0)),
        out_shape=jax.ShapeDtypeStruct((N_TOK, HALF), BF16),
        scratch_shapes=[pltpu.VMEM((LANE_TILES, tm, 128), F32), pltpu.VMEM((LANE_TILES, tm, 128), F32)],
        compiler_params=_cparams(("parallel",)),
        name="s5_glu",
    )(tu, ty, ssm_d.reshape(1, HALF), glu_w.astype(BF16), glu_b.reshape(1, HALF))


def kernel(x_prompt, x_sample, c, cache_k, cache_v, state_s5_re, state_s5_im, c_ctx, norm1_g, norm2_g, ada_w, ada_b, mlp_w1, mlp_w2, ab_w_in, pool_w, pool_scale, conv_w, ab_w_out, cd_w_in, na_rpb, ssm_lambda_re, ssm_lambda_im, ssm_log_step, ssm_b_re, ssm_b_im, ssm_c_re, ssm_c_im, ssm_d, glu_w, glu_b, cd_w_out, final_g):
    depth = ada_w.shape[0]
    x = jnp.concatenate([x_prompt.reshape(N_PROMPT_TOK, D_MODEL), x_sample.reshape(N_SAMPLE_TOK, D_MODEL)], axis=0)
    cond = jnp.concatenate([c_ctx[None, :], c, jnp.zeros((N_COND - 1 - DEC_BATCH, D_MODEL), F32)], axis=0)
    mods = _adaln(cond, ada_w, ada_b).reshape(depth, N_COND, 6, D_MODEL)

    new_k, new_v, new_re, new_im = [], [], [], []
    for layer in range(depth):
        i = layer // 2
        final = layer == depth - 1
        if layer % 2 == 0:
            proj = _inproj(x, mods, layer, norm1_g[layer], ab_w_in[i].astype(BF16))
            y = _even_mixer(proj, pool_w[i], pool_scale[i], conv_w[i])
            ya, ya_col, yb, yb_col = y, 0, y, 1
            w_out = ab_w_out[i]
        else:
            qkv, tu = _inproj_odd(x, mods, layer, norm1_g[layer], cd_w_in[i].astype(BF16))
            o_p, ck, cv = _ctx_attn(qkv)
            new_k.append(ck)
            new_v.append(cv)
            o_s = _na_attn(qkv, cache_k, cache_v, _na_col_table(na_rpb[i]), i)
            attn = jnp.concatenate([o_p, o_s], axis=0)

            wend, ktoe, woutT, apow = _s5_prep(ssm_lambda_re[i], ssm_lambda_im[i], ssm_log_step[i],
                                               ssm_b_re[i], ssm_b_im[i], ssm_c_re[i], ssm_c_im[i])

            def pair_state(s):
                return s.astype(F32).reshape(DEC_BATCH, 2, S5_PAIRS, S5_P).transpose(2, 0, 1, 3)

            s0 = jnp.stack([pair_state(state_s5_re[:, i]), pair_state(state_s5_im[:, i])], axis=3)
            s0 = s0.reshape(S5_PAIRS, DEC_BATCH, 4, S5_P)
            ty, fin = _s5_scan(tu, s0, wend, ktoe, woutT, apow)
            fin = fin[:, :, :, :BATCH]
            new_re.append(fin[:, :, 0].transpose(2, 1, 0, 3).reshape(BATCH, 2, N_SSM_GROUPS, SSM_STATE))
            new_im.append(fin[:, :, 1].transpose(2, 1, 0, 3).reshape(BATCH, 2, N_SSM_GROUPS, SSM_STATE))
            d_out = _glu(tu, ty, ssm_d[i], glu_w[i], glu_b[i])
            ya, ya_col, yb, yb_col = attn, 0, d_out, 0
            w_out = cd_w_out[i]
        x = _mlp(x, ya, ya_col, yb, yb_col, w_out.astype(BF16), mods, layer, norm2_g[layer], final_g,
                 mlp_w1[layer].astype(BF16), mlp_w2[layer].astype(BF16), final)

    y_prompt = x[:N_PROMPT_TOK].reshape(BATCH, SEQ, D_MODEL)
    y_sample = x[N_PROMPT_TOK:].reshape(DEC_BATCH, DEC_SEQ, D_MODEL)
    return (y_prompt, y_sample, jnp.concatenate(new_k, axis=1), jnp.concatenate(new_v, axis=1),
            jnp.stack(new_re, axis=1), jnp.stack(new_im, axis=1))
```

```python
import functools
import math

import jax
import jax.numpy as jnp
from jax import lax
from jax.experimental import pallas as pl
from jax.experimental.pallas import tpu as pltpu

F32 = jnp.float32
BF16 = jnp.bfloat16

D_MODEL = 1024
BATCH = 16
SEQ = 256
DEC_BATCH = 4
DEC_SEQ = 2048
PAST_LEN = 256
GRID_W = 64
GRID_ROWS = DEC_SEQ // GRID_W
HALF = 512
POOL_WINDOWS = (2, 4, 8, 16)
POOL_GROUP = 128
HEAD_DIM = 64
N_HEADS = 8
WIN_H = 8
WIN_W = 16
SSM_GROUP = 16
N_SSM_GROUPS = 32
SSM_STATE = 64
D_FF = 4096
EPS = 1e-6

N_PROMPT_TOK = BATCH * SEQ
N_SAMPLE_TOK = DEC_BATCH * DEC_SEQ
N_TOK = N_PROMPT_TOK + N_SAMPLE_TOK
N_COND = 8

S5_CHUNK = 16
S5_PAIRS = N_SSM_GROUPS // 2
S5_CH = 2 * SSM_GROUP
S5_P = 2 * SSM_STATE
S5_W = S5_CHUNK * S5_CH
S5_BP_SAMPLE = 8

VMEM_LIMIT = 56 * 1024 * 1024


def _cparams(sem):
    return pltpu.CompilerParams(dimension_semantics=sem, vmem_limit_bytes=VMEM_LIMIT)


def _cond_row(i, tm):
    npt = N_PROMPT_TOK // tm
    per = DEC_SEQ // tm
    return jnp.where(i < npt, 0, 1 + (i - npt) // per)


def _normmod(x, g, shift, scale):
    ms = jnp.mean(x * x, axis=-1, keepdims=True)
    return (x * lax.rsqrt(ms + EPS) * g) * (1.0 + scale) + shift


def _dot(a, b):
    return jnp.dot(a, b, preferred_element_type=F32)


def _dot_nt(a, b, precision=None):
    return lax.dot_general(a, b, (((1,), (1,)), ((), ())), preferred_element_type=F32, precision=precision)


def _adaln_body(c_ref, w_ref, b_ref, o_ref):
    c = c_ref[...]
    s = c * jax.nn.sigmoid(c)
    o_ref[...] = _dot(s.astype(BF16), w_ref[...].astype(BF16)) + b_ref[...]


def _adaln(cond, ada_w, ada_b):
    depth = ada_w.shape[0]
    nj = 6
    return pl.pallas_call(
        _adaln_body,
        grid=(depth, nj),
        in_specs=[
            pl.BlockSpec((N_COND, D_MODEL), lambda l, j: (0, 0)),
            pl.BlockSpec((None, D_MODEL, D_MODEL), lambda l, j: (l, 0, j)),
            pl.BlockSpec((None, 1, D_MODEL), lambda l, j: (l, 0, j)),
        ],
        out_specs=pl.BlockSpec((None, N_COND, D_MODEL), lambda l, j: (l, 0, j)),
        out_shape=jax.ShapeDtypeStruct((depth, N_COND, 6 * D_MODEL), F32),
        compiler_params=_cparams(("arbitrary", "arbitrary")),
        name="adaln",
    )(cond, ada_w, ada_b.reshape(depth, 1, 6 * D_MODEL))


TM_PROJ = 512


def _stream_specs(xs, tm):
    if len(xs) == 1:
        return [pl.BlockSpec((tm, D_MODEL), lambda i, *_: (i, 0))]
    npt = N_PROMPT_TOK // tm
    return [pl.BlockSpec((tm, D_MODEL), lambda i, *_: (jnp.minimum(i, npt - 1), 0)),
            pl.BlockSpec((tm, D_MODEL), lambda i, *_: (jnp.maximum(i - npt, 0), 0))]


def _stream_tile(x_refs, tm):
    if len(x_refs) == 1:
        return x_refs[0][...]
    is_prompt = pl.program_id(0) < N_PROMPT_TOK // tm
    return jnp.where(is_prompt, x_refs[0][...], x_refs[1][...])


def _inproj_body(n_x, *refs):
    x_refs, (m_ref, g_ref, w_ref, o_ref) = refs[:n_x], refs[n_x:]
    m = m_ref[...]
    h = _normmod(_stream_tile(x_refs, TM_PROJ), g_ref[...], m[0:1], m[1:2])
    o_ref[...] = _dot(h.astype(BF16), w_ref[...]).astype(BF16)


def _inproj(xs, mods, layer, g, w):
    tm = TM_PROJ
    n_out = w.shape[1]
    return pl.pallas_call(
        functools.partial(_inproj_body, len(xs)),
        grid=(N_TOK // tm,),
        in_specs=_stream_specs(xs, tm) + [
            pl.BlockSpec((None, None, 6, D_MODEL), lambda i: (layer, _cond_row(i, tm), 0, 0)),
            pl.BlockSpec((1, D_MODEL), lambda i: (0, 0)),
            pl.BlockSpec((D_MODEL, n_out), lambda i: (0, 0)),
        ],
        out_specs=pl.BlockSpec((tm, n_out), lambda i: (i, 0)),
        out_shape=jax.ShapeDtypeStruct((N_TOK, n_out), BF16),
        compiler_params=_cparams(("parallel",)),
        name="inproj",
    )(*xs, mods, g.reshape(1, D_MODEL), w)


N_CHUNK_ROWS = N_TOK // S5_CHUNK
LANE_TILES = S5_W // 128


def _inproj_odd_body(x_ref, m_ref, g_ref, w_ref, qkv_ref, tu_ref, u_ref):
    m = m_ref[...]
    h = _normmod(x_ref[...], g_ref[...], m[0:1], m[1:2])
    acc = _dot(h.astype(BF16), w_ref[...])
    qkv_ref[...] = acc[:, 0:3 * HALF].astype(BF16)
    for t in range(LANE_TILES):
        u_ref[t] = acc[:, 3 * HALF + t * 128:3 * HALF + (t + 1) * 128]
    rows = TM_PROJ // S5_CHUNK
    for j in range(S5_CHUNK):
        uj = jnp.concatenate([u_ref[t, pl.ds(j, rows, stride=S5_CHUNK), :] for t in range(LANE_TILES)], axis=1)
        if j:
            uj = pltpu.roll(uj, j * S5_CH, 1)
        for t in range(LANE_TILES):
            tu_ref[j, t] = uj[:, t * 128:(t + 1) * 128].astype(BF16)


def _inproj_odd(x, mods, layer, g, w):
    tm = TM_PROJ
    rows = tm // S5_CHUNK
    return pl.pallas_call(
        _inproj_odd_body,
        grid=(N_TOK // tm,),
        in_specs=[
            pl.BlockSpec((tm, D_MODEL), lambda i: (i, 0)),
            pl.BlockSpec((None, None, 6, D_MODEL), lambda i: (layer, _cond_row(i, tm), 0, 0)),
            pl.BlockSpec((1, D_MODEL), lambda i: (0, 0)),
            pl.BlockSpec((D_MODEL, 4 * HALF), lambda i: (0, 0)),
        ],
        out_specs=[
            pl.BlockSpec((tm, 3 * HALF), lambda i: (i, 0)),
            pl.BlockSpec((S5_CHUNK, LANE_TILES, rows, 128), lambda i: (0, 0, i, 0)),
        ],
        out_shape=[
            jax.ShapeDtypeStruct((N_TOK, 3 * HALF), BF16),
            jax.ShapeDtypeStruct((S5_CHUNK, LANE_TILES, N_CHUNK_ROWS, 128), BF16),
        ],
        scratch_shapes=[pltpu.VMEM((LANE_TILES, tm, 128), F32)],
        compiler_params=_cparams(("parallel",)),
        name="inproj_odd",
    )(x, mods, g.reshape(1, D_MODEL), w)


TM_MIX = 256
HALO = 16


def _even_mixer_body(p_ref, prev_ref, next_ref, pw_ref, ps_ref, cw_ref, y_ref, pad_ref):
    i = pl.program_id(0)
    n_prompt_tiles = N_PROMPT_TOK // TM_MIX
    tiles_per_seq = DEC_SEQ // TM_MIX
    is_prompt = i < n_prompt_tiles
    tile_in_seq = jnp.where(is_prompt, 0, (i - n_prompt_tiles) % tiles_per_seq)
    seq_len = jnp.where(is_prompt, SEQ, DEC_SEQ)
    has_prev = tile_in_seq > 0
    has_next = jnp.logical_and(jnp.logical_not(is_prompt), tile_in_seq < tiles_per_seq - 1)
    prev_keep = jnp.where(has_prev, 1.0, 0.0).astype(F32)
    next_keep = jnp.where(has_next, 1.0, 0.0).astype(F32)
    pos = tile_in_seq * TM_MIX + lax.broadcasted_iota(jnp.int32, (TM_MIX, 1), 0)
    lo_rows, hi_rows = HALO, HALO + TM_MIX

    pad_ref[0:HALO, :] = prev_ref[:, 0:HALF].astype(F32) * prev_keep
    pad_ref[lo_rows:hi_rows, :] = p_ref[:, 0:HALF].astype(F32)
    pad_ref[hi_rows:hi_rows + HALO, :] = next_ref[:, 0:HALF].astype(F32) * next_keep
    for gi, w in enumerate(POOL_WINDOWS):
        cs = slice(gi * POOL_GROUP, (gi + 1) * POOL_GROUP)
        s = None
        for o in range(-(w // 2), w - w // 2):
            t = pad_ref[lo_rows + o:hi_rows + o, cs]
            s = t if s is None else s + t
        lo = jnp.maximum(pos - w // 2, 0)
        hi = jnp.minimum(pos + (w - w // 2), seq_len)
        inv = 1.0 / (hi - lo).astype(F32)
        pooled = s * inv - pad_ref[lo_rows:hi_rows, cs]
        mixed = _dot(pooled.astype(BF16), pw_ref[gi]) * ps_ref[:, cs]
        y_ref[:, cs] = mixed.astype(BF16)

    c0, v0 = 2 * HALF, 3 * HALF
    pad_ref[0:HALO, :] = (prev_ref[:, c0:c0 + HALF].astype(F32) * prev_ref[:, v0:v0 + HALF].astype(F32)) * prev_keep
    pad_ref[lo_rows:hi_rows, :] = p_ref[:, c0:c0 + HALF].astype(F32) * p_ref[:, v0:v0 + HALF].astype(F32)
    pad_ref[hi_rows:hi_rows + HALO, :] = (next_ref[:, c0:c0 + HALF].astype(F32)
                                          * next_ref[:, v0:v0 + HALF].astype(F32)) * next_keep
    for ci in range(HALF // 128):
        cs = slice(ci * 128, (ci + 1) * 128)
        conv = (cw_ref[0:1, cs] * pad_ref[lo_rows - 1:hi_rows - 1, cs]
                + cw_ref[1:2, cs] * pad_ref[lo_rows:hi_rows, cs]
                + cw_ref[2:3, cs] * pad_ref[lo_rows + 1:hi_rows + 1, cs])
        bg = p_ref[:, HALF + ci * 128:HALF + (ci + 1) * 128].astype(F32)
        y_ref[:, HALF + ci * 128:HALF + (ci + 1) * 128] = (bg * conv).astype(BF16)


def _even_mixer(proj, pool_w, pool_scale, conv_w):
    n_tiles = N_TOK // TM_MIX
    per = TM_MIX // HALO
    n_halo_blocks = N_TOK // HALO
    return pl.pallas_call(
        _even_mixer_body,
        grid=(n_tiles,),
        in_specs=[
            pl.BlockSpec((TM_MIX, 4 * HALF), lambda i: (i, 0)),
            pl.BlockSpec((HALO, 4 * HALF), lambda i: (jnp.maximum(i * per - 1, 0), 0)),
            pl.BlockSpec((HALO, 4 * HALF), lambda i: (jnp.minimum((i + 1) * per, n_halo_blocks - 1), 0)),
            pl.BlockSpec((len(POOL_WINDOWS), POOL_GROUP, POOL_GROUP), lambda i: (0, 0, 0)),
            pl.BlockSpec((1, HALF), lambda i: (0, 0)),
            pl.BlockSpec((3, HALF), lambda i: (0, 0)),
        ],
        out_specs=pl.BlockSpec((TM_MIX, 2 * HALF), lambda i: (i, 0)),
        out_shape=jax.ShapeDtypeStruct((N_TOK, 2 * HALF), BF16),
        scratch_shapes=[pltpu.VMEM((TM_MIX + 2 * HALO, HALF), F32)],
        compiler_params=_cparams(("parallel",)),
        name="even_mixer",
    )(proj, proj, proj, pool_w.astype(BF16), pool_scale.reshape(1, HALF), conv_w)


TM_MLP = 1024
TF_MLP = 512


def _mlp_body(n_x, final, *refs):
    x_refs = refs[:n_x]
    ya_ref, yb_ref, wo_ref, m_ref, g_ref, fg_ref, w1_ref, w2_ref = refs[n_x:n_x + 8]
    o_refs = refs[n_x + 8:-3]
    x1_ref, h_ref, acc_ref = refs[-3:]
    j = pl.program_id(1)

    @pl.when(j == 0)
    def _():
        m = m_ref[...]
        mix = _dot(ya_ref[...], wo_ref[0:HALF, :]) + _dot(yb_ref[...], wo_ref[HALF:2 * HALF, :])
        x1 = _stream_tile(x_refs, TM_MLP) + m[2:3] * mix
        x1_ref[...] = x1
        h_ref[...] = _normmod(x1, g_ref[...], m[3:4], m[4:5]).astype(BF16)
        acc_ref[...] = jnp.zeros_like(acc_ref)

    t = _dot(h_ref[...], w1_ref[...])
    t = jnp.square(jnp.maximum(t, 0.0)).astype(BF16)
    acc_ref[...] += _dot(t, w2_ref[...])

    @pl.when(j == pl.num_programs(1) - 1)
    def _():
        out = x1_ref[...] + m_ref[5:6, :] * acc_ref[...]
        if final:
            ms = jnp.mean(out * out, axis=-1, keepdims=True)
            out = out * lax.rsqrt(ms + EPS) * fg_ref[...]
        if len(o_refs) == 1:
            o_refs[0][...] = out
        else:
            is_prompt = pl.program_id(0) < N_PROMPT_TOK // TM_MLP

            @pl.when(is_prompt)
            def _():
                o_refs[0][...] = out

            @pl.when(jnp.logical_not(is_prompt))
            def _():
                o_refs[1][...] = out


def _mlp(xs, ya, ya_col, yb, yb_col, w_out, mods, layer, g2, final_g, w1, w2, final):
    tm, tf = TM_MLP, TF_MLP
    npt = N_PROMPT_TOK // tm
    if final:
        out_specs = [pl.BlockSpec((tm, D_MODEL), lambda i, j: (jnp.minimum(i, npt - 1), 0)),
                     pl.BlockSpec((tm, D_MODEL), lambda i, j: (jnp.maximum(i - npt, 0), 0))]
        out_shape = [jax.ShapeDtypeStruct((N_PROMPT_TOK, D_MODEL), F32),
                     jax.ShapeDtypeStruct((N_SAMPLE_TOK, D_MODEL), F32)]
    else:
        out_specs = pl.BlockSpec((tm, D_MODEL), lambda i, j: (i, 0))
        out_shape = jax.ShapeDtypeStruct((N_TOK, D_MODEL), F32)
    return pl.pallas_call(
        functools.partial(_mlp_body, len(xs), final),
        grid=(N_TOK // tm, D_FF // tf),
        in_specs=_stream_specs(xs, tm) + [
            pl.BlockSpec((tm, HALF), lambda i, j: (i, ya_col)),
            pl.BlockSpec((tm, HALF), lambda i, j: (i, yb_col)),
            pl.BlockSpec((D_MODEL, D_MODEL), lambda i, j: (0, 0)),
            pl.BlockSpec((None, None, 6, D_MODEL), lambda i, j: (layer, _cond_row(i, tm), 0, 0)),
            pl.BlockSpec((1, D_MODEL), lambda i, j: (0, 0)),
            pl.BlockSpec((1, D_MODEL), lambda i, j: (0, 0)),
            pl.BlockSpec((D_MODEL, tf), lambda i, j: (0, j)),
            pl.BlockSpec((tf, D_MODEL), lambda i, j: (j, 0)),
        ],
        out_specs=out_specs,
        out_shape=out_shape,
        scratch_shapes=[
            pltpu.VMEM((tm, D_MODEL), F32),
            pltpu.VMEM((tm, D_MODEL), BF16),
            pltpu.VMEM((tm, D_MODEL), F32),
        ],
        compiler_params=_cparams(("arbitrary", "arbitrary")),
        name="mlp",
    )(*xs, ya, yb, w_out, mods, g2.reshape(1, D_MODEL), final_g.reshape(1, D_MODEL), w1, w2)


def _ctx_attn_body(q_ref, k_ref, v_ref, o_ref, ck_ref, cv_ref):
    scale = HEAD_DIM ** -0.5
    for h in range(N_HEADS):
        sl = slice(h * HEAD_DIM, (h + 1) * HEAD_DIM)
        q = q_ref[:, sl]
        k = k_ref[:, sl]
        v = v_ref[:, sl]
        ck_ref[h] = k.astype(F32)
        cv_ref[h] = v.astype(F32)
        s = _dot_nt(q, k) * scale
        m = jnp.max(s, axis=-1, keepdims=True)
        p = jnp.exp(s - m)
        l = jnp.sum(p, axis=-1, keepdims=True)
        o = _dot(p.astype(BF16), v) / l
        o_ref[:, sl] = o.astype(BF16)


def _ctx_attn(proj):
    return pl.pallas_call(
        _ctx_attn_body,
        grid=(BATCH,),
        in_specs=[
            pl.BlockSpec((SEQ, HALF), lambda b: (b, 0)),
            pl.BlockSpec((SEQ, HALF), lambda b: (b, 1)),
            pl.BlockSpec((SEQ, HALF), lambda b: (b, 2)),
        ],
        out_specs=[
            pl.BlockSpec((SEQ, HALF), lambda b: (b, 0)),
            pl.BlockSpec((None, None, N_HEADS, SEQ, HEAD_DIM), lambda b: (b, 0, 0, 0, 0)),
            pl.BlockSpec((None, None, N_HEADS, SEQ, HEAD_DIM), lambda b: (b, 0, 0, 0, 0)),
        ],
        out_shape=[
            jax.ShapeDtypeStruct((N_PROMPT_TOK, HALF), BF16),
            jax.ShapeDtypeStruct((BATCH, 1, N_HEADS, SEQ, HEAD_DIM), F32),
            jax.ShapeDtypeStruct((BATCH, 1, N_HEADS, SEQ, HEAD_DIM), F32),
        ],
        compiler_params=_cparams(("parallel",)),
        name="ctx_attn",
    )(proj, proj, proj)


NA_MASKED = -1e30
NA_QROWS = 4
NA_KROWS = NA_QROWS + WIN_H
NA_NQ = NA_QROWS * GRID_W
NA_NK = NA_KROWS * GRID_W
NA_BLOCKS = GRID_ROWS // NA_QROWS
NA_TYPES = 3


def _na_col_table(rpb):
    qc = jnp.arange(GRID_W)[:, None]
    kc = jnp.arange(GRID_W)[None, :]
    c0 = jnp.clip(qc - WIN_W // 2, 0, GRID_W - WIN_W)
    valid = (kc >= c0) & (kc < c0 + WIN_W)
    coff = jnp.clip(kc - qc, -(WIN_W - 1), WIN_W - 1) + (WIN_W - 1)
    onehot = (coff.reshape(1, -1) == jnp.arange(2 * WIN_W - 1)[:, None]).astype(F32)
    t = jnp.dot(rpb.astype(F32).reshape(N_HEADS * (2 * WIN_H - 1), 2 * WIN_W - 1), onehot,
                precision=lax.Precision.HIGHEST)
    t = t.reshape(N_HEADS, 2 * WIN_H - 1, GRID_W, GRID_W)
    return jnp.where(valid[None, None], t, NA_MASKED)


def _na_window_start(rb):
    return jnp.clip(NA_QROWS * rb - WIN_H // 2, 0, GRID_ROWS - NA_KROWS)


def _na_body(q_ref, k_ref, v_ref, ck_ref, cv_ref, tab_ref, o_ref, bias_ref, kc_ref, vc_ref, va_ref):
    @pl.when(pl.program_id(1) == 0)
    def _():
        masked = jnp.full((GRID_W, GRID_W), NA_MASKED, F32)
        for hh in range(2):
            for ty in range(NA_TYPES):
                q0 = (0, NA_QROWS, GRID_ROWS - NA_QROWS)[ty]
                ks = (0, 0, GRID_ROWS - NA_KROWS)[ty]
                for a in range(NA_QROWS):
                    r0 = min(max(q0 + a - WIN_H // 2, 0), GRID_ROWS - WIN_H)
                    for y in range(NA_KROWS):
                        kr = ks + y
                        ok = r0 <= kr < r0 + WIN_H
                        blk = tab_ref[hh, kr - (q0 + a) + WIN_H - 1] if ok else masked
                        bias_ref[ty, hh * NA_NQ + a * GRID_W:hh * NA_NQ + (a + 1) * GRID_W,
                                 y * GRID_W:(y + 1) * GRID_W] = blk

    lane = lax.broadcasted_iota(jnp.int32, (1, 2 * HEAD_DIM), 1)
    own = [lane < HEAD_DIM, lane >= HEAD_DIM]
    kc_ref[...] = jnp.concatenate([ck_ref[0], ck_ref[1]], axis=1).astype(BF16)
    vc = jnp.concatenate([cv_ref[0], cv_ref[1]], axis=1)
    v = v_ref[...]
    for hh in range(2):
        vc_ref[hh] = jnp.where(own[hh], vc, 1.0).astype(BF16)
        va_ref[hh] = jnp.where(own[hh], v, jnp.ones_like(v))

    def block(rb, ty):
        qs = pl.multiple_of(rb * NA_NQ, NA_NQ)
        ks = pl.multiple_of(_na_window_start(rb) * GRID_W, GRID_W)
        q = q_ref[pl.ds(qs, NA_NQ), :] * jnp.asarray(HEAD_DIM ** -0.5, BF16)
        q2 = jnp.concatenate([jnp.where(own[0], q, jnp.zeros_like(q)),
                              jnp.where(own[1], q, jnp.zeros_like(q))], axis=0)
        s_loc = _dot_nt(q2, k_ref[pl.ds(ks, NA_NK), :]) + bias_ref[ty]
        s_ctx = _dot_nt(q2, kc_ref[...])
        m = jnp.maximum(jnp.max(s_loc, axis=-1, keepdims=True), jnp.max(s_ctx, axis=-1, keepdims=True))
        p_loc = jnp.exp(s_loc - m).astype(BF16)
        p_ctx = jnp.exp(s_ctx - m).astype(BF16)
        out = None
        for hh in range(2):
            rows = slice(hh * NA_NQ, (hh + 1) * NA_NQ)
            num = _dot(p_loc[rows], va_ref[hh, pl.ds(ks, NA_NK), :]) + _dot(p_ctx[rows], vc_ref[hh])
            den = jnp.where(own[hh], pltpu.roll(num, HEAD_DIM, 1), 1.0)
            o = num / den
            out = o if out is None else jnp.where(own[0], out, o)
        o_ref[pl.ds(qs, NA_NQ), :] = out.astype(BF16)

    block(0, 0)

    def interior(rb, carry):
        block(rb, 1)
        return carry

    lax.fori_loop(1, NA_BLOCKS - 1, interior, 0)
    block(NA_BLOCKS - 1, 2)


def _na_attn(proj, cache_k, cache_v, col_table, layer_idx):
    row_blk0 = N_PROMPT_TOK // DEC_SEQ
    hp = N_HEADS // 2
    return pl.pallas_call(
        _na_body,
        grid=(hp, DEC_BATCH),
        in_specs=[
            pl.BlockSpec((DEC_SEQ, 128), lambda h, b: (row_blk0 + b, h)),
            pl.BlockSpec((DEC_SEQ, 128), lambda h, b: (row_blk0 + b, hp + h)),
            pl.BlockSpec((DEC_SEQ, 128), lambda h, b: (row_blk0 + b, 2 * hp + h)),
            pl.BlockSpec((None, None, 2, PAST_LEN, HEAD_DIM), lambda h, b: (b, layer_idx, h, 0, 0)),
            pl.BlockSpec((None, None, 2, PAST_LEN, HEAD_DIM), lambda h, b: (b, layer_idx, h, 0, 0)),
            pl.BlockSpec((2, 2 * WIN_H - 1, GRID_W, GRID_W), lambda h, b: (h, 0, 0, 0)),
        ],
        out_specs=pl.BlockSpec((DEC_SEQ, 128), lambda h, b: (b, h)),
        out_shape=jax.ShapeDtypeStruct((N_SAMPLE_TOK, HALF), BF16),
        scratch_shapes=[pltpu.VMEM((NA_TYPES, 2 * NA_NQ, NA_NK), F32),
                        pltpu.VMEM((PAST_LEN, 2 * HEAD_DIM), BF16),
                        pltpu.VMEM((2, PAST_LEN, 2 * HEAD_DIM), BF16),
                        pltpu.VMEM((2, DEC_SEQ, 2 * HEAD_DIM), BF16)],
        compiler_params=_cparams(("parallel", "arbitrary")),
        name="na_attn",
    )(proj, proj, proj, cache_k, cache_v, col_table)


N_POW = 24


def _s5_prep_body(vec_ref, bt_ref, c_ref, wend_ref, ktoe_ref, woutT_ref, apow_ref, pw_ref, pct_ref):
    L = S5_CHUNK
    p = pl.program_id(0)
    lane_blk = lax.broadcasted_iota(jnp.int32, (S5_CH, S5_W), 1) // S5_CH
    src_blk = (lane_blk + L - p) % L
    krows = []
    for d in range(2):
        v = vec_ref[d]
        lre, lim, step = v[0:1], v[1:2], v[2:3]
        a = lre * step
        b = lim * step
        ea = jnp.exp(a)
        nr = ea * jnp.cos(b) - 1.0
        ni = ea * jnp.sin(b)
        den = lre * lre + lim * lim
        qr = (nr * lre + ni * lim) / den
        qi = (ni * lre - nr * lim) / den
        bre, bim = bt_ref[d, 0], bt_ref[d, 1]
        bqr = bre * qr - bim * qi
        bqi = bre * qi + bim * qr
        cre, cim = c_ref[d, 0], c_ref[d, 1]

        e = lax.broadcasted_iota(jnp.int32, (N_POW, 1), 0).astype(F32)
        mag = jnp.exp(e * a)
        pw_ref[0] = mag * jnp.cos(e * b)
        pw_ref[1] = mag * jnp.sin(e * b)
        mag = jnp.exp((L * e) * a)
        apow_ref[d, 0] = mag * jnp.cos((L * e) * b)
        apow_ref[d, 1] = mag * jnp.sin((L * e) * b)

        def power(k):
            return pw_ref[0, k:k + 1, :], pw_ref[1, k:k + 1, :]

        def c_block(k):
            pr, pi = power(k)
            return cre * pr - cim * pi, -(cre * pi + cim * pr)

        for i in range(L):
            rows = pl.ds(pl.multiple_of(((i + p) % L) * S5_CH, S5_CH), S5_CH)
            pr, pi = power(L - 1 - i if d == 0 else i)
            wend_ref[rows, d * 256:d * 256 + 128] = (bqr * pr - bqi * pi).astype(BF16)
            wend_ref[rows, d * 256 + 128:d * 256 + 256] = (bqr * pi + bqi * pr).astype(BF16)
            mre, mim = c_block(i + 1 if d == 0 else L - i)
            woutT_ref[rows, d * 256:d * 256 + 128] = mre.astype(BF16)
            woutT_ref[rows, d * 256 + 128:d * 256 + 256] = mim.astype(BF16)
            lag_rows = slice(i * S5_CH, (i + 1) * S5_CH)
            mre, mim = c_block(i if d == 0 else L - 1 - i)
            pct_ref[lag_rows, 0:128] = mre
            pct_ref[lag_rows, 128:256] = mim

        bqcat = jnp.concatenate([bqr, bqi], axis=1)
        krows.append(_dot_nt(bqcat, pct_ref[...], precision=lax.Precision.HIGHEST))

    for i in range(L):
        rows = pl.ds(pl.multiple_of(((i + p) % L) * S5_CH, S5_CH), S5_CH)
        blk_f = pltpu.roll(krows[0], ((i + p) % L) * S5_CH, 1)
        blk_b = pltpu.roll(krows[1], ((i + 1 + p) % L) * S5_CH, 1)
        blk = jnp.where(src_blk >= i, blk_f, 0.0) + jnp.where(src_blk <= i, blk_b, 0.0)
        ktoe_ref[rows, :] = blk.astype(BF16)


def _s5_prep(lam_re, lam_im, log_step, b_re, b_im, c_re, c_im):
    def pair_vec(a):
        return a.astype(F32).reshape(2, S5_PAIRS, 1, S5_P).transpose(1, 0, 2, 3)

    step = jnp.broadcast_to(jnp.exp(log_step.astype(F32))[:, :, None], (2, N_SSM_GROUPS, SSM_STATE))
    vec = jnp.concatenate([pair_vec(lam_re), pair_vec(lam_im), pair_vec(step),
                           jnp.zeros((S5_PAIRS, 2, 5, S5_P), F32)], axis=2)

    def block_diag_ch_by_state(m):
        m = m.astype(F32).reshape(2, S5_PAIRS, 2, SSM_GROUP, SSM_STATE)
        eye = jnp.eye(2, dtype=F32)
        out = m[:, :, :, :, None, :] * eye[None, None, :, None, :, None]
        return out.reshape(2, S5_PAIRS, S5_CH, S5_P).transpose(1, 0, 2, 3)

    bt = jnp.stack([block_diag_ch_by_state(b_re.transpose(0, 1, 3, 2)),
                    block_diag_ch_by_state(b_im.transpose(0, 1, 3, 2))], axis=2)
    cc = jnp.stack([block_diag_ch_by_state(c_re), block_diag_ch_by_state(c_im)], axis=2)

    mat = jax.ShapeDtypeStruct((S5_PAIRS, S5_W, S5_W), BF16)
    mat_spec = pl.BlockSpec((None, S5_W, S5_W), lambda p: (p, 0, 0))
    return pl.pallas_call(
        _s5_prep_body,
        grid=(S5_PAIRS,),
        in_specs=[
            pl.BlockSpec((None, 2, 8, S5_P), lambda p: (p, 0, 0, 0)),
            pl.BlockSpec((None, 2, 2, S5_CH, S5_P), lambda p: (p, 0, 0, 0, 0)),
            pl.BlockSpec((None, 2, 2, S5_CH, S5_P), lambda p: (p, 0, 0, 0, 0)),
        ],
        out_specs=[mat_spec, mat_spec, mat_spec,
                   pl.BlockSpec((None, 2, 2, N_POW, S5_P), lambda p: (p, 0, 0, 0, 0))],
        out_shape=[mat, mat, mat, jax.ShapeDtypeStruct((S5_PAIRS, 2, 2, N_POW, S5_P), F32)],
        scratch_shapes=[pltpu.VMEM((2, N_POW, S5_P), F32), pltpu.VMEM((S5_W, 2 * S5_P), F32)],
        compiler_params=_cparams(("parallel",)),
        name="s5_prep",
    )(vec, bt, cc)


S5_SEG_ROWS = 256
S5_N_SEG = N_CHUNK_ROWS // S5_SEG_ROWS
S5_SUB = S5_SEG_ROWS // S5_CHUNK
S5_SUB_PER_SEQ = (DEC_SEQ // S5_CHUNK) // S5_CHUNK


def _s5_body(tu_ref, s0_ref, wend_ref, ktoe_ref, woutT_ref, apow_ref, ty_ref, fin_ref, z_ref, xs_ref, g_ref):
    seg = pl.program_id(0)
    p = pl.program_id(1)
    L = S5_CHUNK
    lane_blk = lax.broadcasted_iota(jnp.int32, (S5_SEG_ROWS, 128), 1) // S5_CH

    tiles = []
    for t in range(LANE_TILES):
        tile = None
        for k in range(128 // S5_CH):
            j = (LANE_TILES * t + k + L - p) % L
            blk = tu_ref[j, t]
            tile = blk if tile is None else jnp.where(lane_blk == k, blk, tile)
        tiles.append(tile)
    s = jnp.concatenate(tiles, axis=1)
    z = _dot(s, wend_ref[...])
    for q in range(LANE_TILES):
        z_ref[q] = z[:, q * 128:(q + 1) * 128]

    ends = []
    for d in range(2):
        are, aim = apow_ref[d, 0, 1:2, :], apow_ref[d, 1, 1:2, :]
        xr = jnp.zeros((S5_SUB, S5_P), F32)
        xi = jnp.zeros((S5_SUB, S5_P), F32)
        for c in (range(L) if d == 0 else reversed(range(L))):
            rows = pl.ds(c, S5_SUB, stride=L)
            xs_ref[2 * d, rows, :] = xr
            xs_ref[2 * d + 1, rows, :] = xi
            zr = z_ref[2 * d, rows, :]
            zi = z_ref[2 * d + 1, rows, :]
            xr, xi = are * xr - aim * xi + zr, are * xi + aim * xr + zi
        ends.append((xr, xi))
        fin_ref[d, 0] = xr
        fin_ref[d, 1] = xi

    @pl.when(seg > 0)
    def _():
        for d in range(2):
            er, ei = ends[d]
            a16r, a16i = apow_ref[d, 0, L:L + 1, :], apow_ref[d, 1, L:L + 1, :]
            for bb in range(S5_SUB // S5_SUB_PER_SEQ):
                gr = s0_ref[bb, 2 * d:2 * d + 1, :]
                gi = s0_ref[bb, 2 * d + 1:2 * d + 2, :]
                order = range(S5_SUB_PER_SEQ) if d == 0 else reversed(range(S5_SUB_PER_SEQ))
                prev = None
                for sub in order:
                    row = bb * S5_SUB_PER_SEQ + sub
                    if prev is not None:
                        gr, gi = (a16r * gr - a16i * gi + er[prev:prev + 1, :],
                                  a16r * gi + a16i * gr + ei[prev:prev + 1, :])
                    g_ref[0, row:row + 1, :] = gr
                    g_ref[1, row:row + 1, :] = gi
                    prev = row
            gr, gi = g_ref[0], g_ref[1]
            for c in range(L):
                k = c if d == 0 else L - 1 - c
                pr, pi = apow_ref[d, 0, k:k + 1, :], apow_ref[d, 1, k:k + 1, :]
                rows = pl.ds(c, S5_SUB, stride=L)
                xs_ref[2 * d, rows, :] = xs_ref[2 * d, rows, :] + (pr * gr - pi * gi)
                xs_ref[2 * d + 1, rows, :] = xs_ref[2 * d + 1, rows, :] + (pr * gi + pi * gr)

    xs = jnp.concatenate([xs_ref[q].astype(BF16) for q in range(LANE_TILES)], axis=1)
    y = _dot(s, ktoe_ref[...]) + _dot_nt(xs, woutT_ref[...])

    @pl.when(p == 0)
    def _():
        ty_ref[...] = jnp.zeros_like(ty_ref)

    for t in range(LANE_TILES):
        yt = y[:, t * 128:(t + 1) * 128]
        for k in range(128 // S5_CH):
            j = (LANE_TILES * t + k + L - p) % L
            ty_ref[j, t] = jnp.where(lane_blk == k, yt, ty_ref[j, t])


def _s5_scan(tu, s0, wend, ktoe, woutT, apow):
    mat_spec = pl.BlockSpec((None, S5_W, S5_W), lambda g, p: (p, 0, 0))
    seqs = S5_SUB // S5_SUB_PER_SEQ
    return pl.pallas_call(
        _s5_body,
        grid=(S5_N_SEG, S5_PAIRS),
        in_specs=[
            pl.BlockSpec((S5_CHUNK, LANE_TILES, S5_SEG_ROWS, 128), lambda g, p: (0, 0, g, 0)),
            pl.BlockSpec((None, seqs, 4, S5_P), lambda g, p: (p, jnp.maximum(g - 1, 0), 0, 0)),
            mat_spec, mat_spec, mat_spec,
            pl.BlockSpec((None, 2, 2, N_POW, S5_P), lambda g, p: (p, 0, 0, 0, 0)),
        ],
        out_specs=[
            pl.BlockSpec((S5_CHUNK, LANE_TILES, S5_SEG_ROWS, 128), lambda g, p: (0, 0, g, 0)),
            pl.BlockSpec((None, 2, 2, S5_SUB, S5_P), lambda g, p: (p, 0, 0, g, 0)),
        ],
        out_shape=[
            jax.ShapeDtypeStruct((S5_CHUNK, LANE_TILES, N_CHUNK_ROWS, 128), F32),
            jax.ShapeDtypeStruct((S5_PAIRS, 2, 2, S5_N_SEG * S5_SUB, S5_P), F32),
        ],
        scratch_shapes=[pltpu.VMEM((LANE_TILES, S5_SEG_ROWS, 128), F32),
                        pltpu.VMEM((LANE_TILES, S5_SEG_ROWS, 128), F32),
                        pltpu.VMEM((2, S5_SUB, S5_P), F32)],
        compiler_params=_cparams(("parallel", "arbitrary")),
        name="s5_scan",
    )(tu, s0, wend, ktoe, woutT, apow)


TM_GLU = 1024


def _glu_body(tu_ref, ty_ref, d_ref, gw_ref, gb_ref, o_ref, u_ref, ys_ref):
    rows = TM_GLU // S5_CHUNK
    for j in range(S5_CHUNK):
        back = ((S5_CHUNK - j) % S5_CHUNK) * S5_CH
        uj = jnp.concatenate([tu_ref[j, t].astype(F32) for t in range(LANE_TILES)], axis=1)
        yj = jnp.concatenate([ty_ref[j, t] for t in range(LANE_TILES)], axis=1)
        if back:
            uj = pltpu.roll(uj, back, 1)
            yj = pltpu.roll(yj, back, 1)
        for t in range(LANE_TILES):
            u_ref[t, pl.ds(j, rows, stride=S5_CHUNK), :] = uj[:, t * 128:(t + 1) * 128]
            ys_ref[t, pl.ds(j, rows, stride=S5_CHUNK), :] = yj[:, t * 128:(t + 1) * 128]
    u = jnp.concatenate([u_ref[t] for t in range(LANE_TILES)], axis=1)
    y = d_ref[...] * u + jnp.concatenate([ys_ref[t] for t in range(LANE_TILES)], axis=1)
    cdf = 0.5 * (1.0 + jnp.tanh(math.sqrt(2.0 / math.pi) * (y + 0.044715 * (y * y * y))))
    y = y * cdf
    gate = jax.nn.sigmoid(_dot(y.astype(BF16), gw_ref[...]) + gb_ref[...])
    o_ref[...] = (y * gate).astype(BF16)


def _glu(tu, ty, ssm_d, glu_w, glu_b):
    tm = TM_GLU
    rows = tm // S5_CHUNK
    lay_spec = pl.BlockSpec((S5_CHUNK, LANE_TILES, rows, 128), lambda i: (0, 0, i, 0))
    return pl.pallas_call(
        _glu_body,
        grid=(N_TOK // tm,),
        in_specs=[
            lay_spec, lay_spec,
            pl.BlockSpec((1, HALF), lambda i: (0, 0)),
            pl.BlockSpec((HALF, HALF), lambda i: (0, 0)),
            pl.BlockSpec((1, HALF), lambda i: (0, 0)),
        ],
        out_specs=pl.BlockSpec((tm, HALF), lambda i: (i, 0)),
        out_shape=jax.ShapeDtypeStruct((N_TOK, HALF), BF16),
        scratch_shapes=[pltpu.VMEM((LANE_TILES, tm, 128), F32), pltpu.VMEM((LANE_TILES, tm, 128), F32)],
        compiler_params=_cparams(("parallel",)),
        name="s5_glu",
    )(tu, ty, ssm_d.reshape(1, HALF), glu_w.astype(BF16), glu_b.reshape(1, HALF))


def kernel(x_prompt, x_sample, c, cache_k, cache_v, state_s5_re, state_s5_im, c_ctx, norm1_g, norm2_g, ada_w, ada_b, mlp_w1, mlp_w2, ab_w_in, pool_w, pool_scale, conv_w, ab_w_out, cd_w_in, na_rpb, ssm_lambda_re, ssm_lambda_im, ssm_log_step, ssm_b_re, ssm_b_im, ssm_c_re, ssm_c_im, ssm_d, glu_w, glu_b, cd_w_out, final_g):
    depth = ada_w.shape[0]
    xs = (x_prompt.reshape(N_PROMPT_TOK, D_MODEL), x_sample.reshape(N_SAMPLE_TOK, D_MODEL))
    cond = jnp.concatenate([c_ctx[None, :], c, jnp.zeros((N_COND - 1 - DEC_BATCH, D_MODEL), F32)], axis=0)
    mods = _adaln(cond, ada_w, ada_b).reshape(depth, N_COND, 6, D_MODEL)

    new_k, new_v, new_re, new_im = [], [], [], []
    for layer in range(depth):
        i = layer // 2
        final = layer == depth - 1
        if layer % 2 == 0:
            proj = _inproj(xs, mods, layer, norm1_g[layer], ab_w_in[i].astype(BF16))
            y = _even_mixer(proj, pool_w[i], pool_scale[i], conv_w[i])
            ya, ya_col, yb, yb_col = y, 0, y, 1
            w_out = ab_w_out[i]
        else:
            qkv, tu = _inproj_odd(xs[0], mods, layer, norm1_g[layer], cd_w_in[i].astype(BF16))
            o_p, ck, cv = _ctx_attn(qkv)
            new_k.append(ck)
            new_v.append(cv)
            o_s = _na_attn(qkv, cache_k, cache_v, _na_col_table(na_rpb[i]), i)
            attn = jnp.concatenate([o_p, o_s], axis=0)

            wend, ktoe, woutT, apow = _s5_prep(ssm_lambda_re[i], ssm_lambda_im[i], ssm_log_step[i],
                                               ssm_b_re[i], ssm_b_im[i], ssm_c_re[i], ssm_c_im[i])

            def pair_state(s):
                return s.astype(F32).reshape(DEC_BATCH, 2, S5_PAIRS, S5_P).transpose(2, 0, 1, 3)

            s0 = jnp.stack([pair_state(state_s5_re[:, i]), pair_state(state_s5_im[:, i])], axis=3)
            s0 = s0.reshape(S5_PAIRS, DEC_BATCH, 4, S5_P)
            ty, fin = _s5_scan(tu, s0, wend, ktoe, woutT, apow)
            fin = fin[:, :, :, :BATCH]
            new_re.append(fin[:, :, 0].transpose(2, 1, 0, 3).reshape(BATCH, 2, N_SSM_GROUPS, SSM_STATE))
            new_im.append(fin[:, :, 1].transpose(2, 1, 0, 3).reshape(BATCH, 2, N_SSM_GROUPS, SSM_STATE))
            d_out = _glu(tu, ty, ssm_d[i], glu_w[i], glu_b[i])
            ya, ya_col, yb, yb_col = attn, 0, d_out, 0
            w_out = cd_w_out[i]
        out = _mlp(xs, ya, ya_col, yb, yb_col, w_out.astype(BF16), mods, layer, norm2_g[layer], final_g,
                   mlp_w1[layer].astype(BF16), mlp_w2[layer].astype(BF16), final)
        xs = tuple(out) if final else (out,)

    y_prompt = xs[0].reshape(BATCH, SEQ, D_MODEL)
    y_sample = xs[1].reshape(DEC_BATCH, DEC_SEQ, D_MODEL)
    return (y_prompt, y_sample, jnp.concatenate(new_k, axis=1), jnp.concatenate(new_v, axis=1),
            jnp.stack(new_re, axis=1), jnp.stack(new_im, axis=1))
```

```python
import functools
import math

import jax
import jax.numpy as jnp
from jax import lax
from jax.experimental import pallas as pl
from jax.experimental.pallas import tpu as pltpu

F32 = jnp.float32
BF16 = jnp.bfloat16

D_MODEL = 1024
BATCH = 16
SEQ = 256
DEC_BATCH = 4
DEC_SEQ = 2048
PAST_LEN = 256
GRID_W = 64
GRID_ROWS = DEC_SEQ // GRID_W
HALF = 512
POOL_WINDOWS = (2, 4, 8, 16)
POOL_GROUP = 128
HEAD_DIM = 64
N_HEADS = 8
WIN_H = 8
WIN_W = 16
SSM_GROUP = 16
N_SSM_GROUPS = 32
SSM_STATE = 64
D_FF = 4096
EPS = 1e-6

N_PROMPT_TOK = BATCH * SEQ
N_SAMPLE_TOK = DEC_BATCH * DEC_SEQ
N_TOK = N_PROMPT_TOK + N_SAMPLE_TOK
N_COND = 8

S5_CHUNK = 16
S5_PAIRS = N_SSM_GROUPS // 2
S5_CH = 2 * SSM_GROUP
S5_P = 2 * SSM_STATE
S5_W = S5_CHUNK * S5_CH
S5_BP_SAMPLE = 8

VMEM_LIMIT = 56 * 1024 * 1024


def _cparams(sem):
    return pltpu.CompilerParams(dimension_semantics=sem, vmem_limit_bytes=VMEM_LIMIT)


def _cond_row(i, tm):
    npt = N_PROMPT_TOK // tm
    per = DEC_SEQ // tm
    return jnp.where(i < npt, 0, 1 + (i - npt) // per)


def _normmod(x, g, shift, scale):
    ms = jnp.mean(x * x, axis=-1, keepdims=True)
    return (x * lax.rsqrt(ms + EPS) * g) * (1.0 + scale) + shift


def _dot(a, b):
    return jnp.dot(a, b, preferred_element_type=F32)


def _dot_nt(a, b, precision=None):
    return lax.dot_general(a, b, (((1,), (1,)), ((), ())), preferred_element_type=F32, precision=precision)


def _adaln_body(c_ref, w_ref, b_ref, o_ref):
    c = c_ref[...]
    s = c * jax.nn.sigmoid(c)
    o_ref[...] = _dot(s.astype(BF16), w_ref[...].astype(BF16)) + b_ref[...]


def _adaln(cond, ada_w, ada_b):
    depth = ada_w.shape[0]
    nj = 6
    return pl.pallas_call(
        _adaln_body,
        grid=(depth, nj),
        in_specs=[
            pl.BlockSpec((N_COND, D_MODEL), lambda l, j: (0, 0)),
            pl.BlockSpec((None, D_MODEL, D_MODEL), lambda l, j: (l, 0, j)),
            pl.BlockSpec((None, 1, D_MODEL), lambda l, j: (l, 0, j)),
        ],
        out_specs=pl.BlockSpec((None, N_COND, D_MODEL), lambda l, j: (l, 0, j)),
        out_shape=jax.ShapeDtypeStruct((depth, N_COND, 6 * D_MODEL), F32),
        compiler_params=_cparams(("arbitrary", "arbitrary")),
        name="adaln",
    )(cond, ada_w, ada_b.reshape(depth, 1, 6 * D_MODEL))


TM_PROJ = 512


def _stream_specs(xs, tm):
    if len(xs) == 1:
        return [pl.BlockSpec((tm, D_MODEL), lambda i, *_: (i, 0))]
    npt = N_PROMPT_TOK // tm
    return [pl.BlockSpec((tm, D_MODEL), lambda i, *_: (jnp.minimum(i, npt - 1), 0)),
            pl.BlockSpec((tm, D_MODEL), lambda i, *_: (jnp.maximum(i - npt, 0), 0))]


def _stream_tile(x_refs, tm):
    if len(x_refs) == 1:
        return x_refs[0][...]
    is_prompt = pl.program_id(0) < N_PROMPT_TOK // tm
    return jnp.where(is_prompt, x_refs[0][...], x_refs[1][...])


def _inproj_body(n_x, *refs):
    x_refs, (m_ref, g_ref, w_ref, o_ref) = refs[:n_x], refs[n_x:]
    m = m_ref[...]
    h = _normmod(_stream_tile(x_refs, TM_PROJ), g_ref[...], m[0:1], m[1:2])
    o_ref[...] = _dot(h.astype(BF16), w_ref[...]).astype(BF16)


def _inproj(xs, mods, layer, g, w):
    tm = TM_PROJ
    n_out = w.shape[1]
    return pl.pallas_call(
        functools.partial(_inproj_body, len(xs)),
        grid=(N_TOK // tm,),
        in_specs=_stream_specs(xs, tm) + [
            pl.BlockSpec((None, None, 6, D_MODEL), lambda i: (layer, _cond_row(i, tm), 0, 0)),
            pl.BlockSpec((1, D_MODEL), lambda i: (0, 0)),
            pl.BlockSpec((D_MODEL, n_out), lambda i: (0, 0)),
        ],
        out_specs=pl.BlockSpec((tm, n_out), lambda i: (i, 0)),
        out_shape=jax.ShapeDtypeStruct((N_TOK, n_out), BF16),
        compiler_params=_cparams(("parallel",)),
        name="inproj",
    )(*xs, mods, g.reshape(1, D_MODEL), w)


N_CHUNK_ROWS = N_TOK // S5_CHUNK
LANE_TILES = S5_W // 128


def _inproj_odd_body(x_ref, m_ref, g_ref, w_ref, qkv_ref, tu_ref, u_ref):
    m = m_ref[...]
    h = _normmod(x_ref[...], g_ref[...], m[0:1], m[1:2])
    acc = _dot(h.astype(BF16), w_ref[...])
    qkv_ref[...] = acc[:, 0:3 * HALF].astype(BF16)
    for t in range(LANE_TILES):
        u_ref[t] = acc[:, 3 * HALF + t * 128:3 * HALF + (t + 1) * 128]
    rows = TM_PROJ // S5_CHUNK
    for j in range(S5_CHUNK):
        uj = jnp.concatenate([u_ref[t, pl.ds(j, rows, stride=S5_CHUNK), :] for t in range(LANE_TILES)], axis=1)
        if j:
            uj = pltpu.roll(uj, j * S5_CH, 1)
        for t in range(LANE_TILES):
            tu_ref[j, t] = uj[:, t * 128:(t + 1) * 128].astype(BF16)


def _inproj_odd(x, mods, layer, g, w):
    tm = TM_PROJ
    rows = tm // S5_CHUNK
    return pl.pallas_call(
        _inproj_odd_body,
        grid=(N_TOK // tm,),
        in_specs=[
            pl.BlockSpec((tm, D_MODEL), lambda i: (i, 0)),
            pl.BlockSpec((None, None, 6, D_MODEL), lambda i: (layer, _cond_row(i, tm), 0, 0)),
            pl.BlockSpec((1, D_MODEL), lambda i: (0, 0)),
            pl.BlockSpec((D_MODEL, 4 * HALF), lambda i: (0, 0)),
        ],
        out_specs=[
            pl.BlockSpec((tm, 3 * HALF), lambda i: (i, 0)),
            pl.BlockSpec((S5_CHUNK, LANE_TILES, rows, 128), lambda i: (0, 0, i, 0)),
        ],
        out_shape=[
            jax.ShapeDtypeStruct((N_TOK, 3 * HALF), BF16),
            jax.ShapeDtypeStruct((S5_CHUNK, LANE_TILES, N_CHUNK_ROWS, 128), BF16),
        ],
        scratch_shapes=[pltpu.VMEM((LANE_TILES, tm, 128), F32)],
        compiler_params=_cparams(("parallel",)),
        name="inproj_odd",
    )(x, mods, g.reshape(1, D_MODEL), w)


TM_MIX = 256
HALO = 16


def _even_mixer_body(p_ref, prev_ref, next_ref, pw_ref, ps_ref, cw_ref, y_ref, pad_ref):
    i = pl.program_id(0)
    n_prompt_tiles = N_PROMPT_TOK // TM_MIX
    tiles_per_seq = DEC_SEQ // TM_MIX
    is_prompt = i < n_prompt_tiles
    tile_in_seq = jnp.where(is_prompt, 0, (i - n_prompt_tiles) % tiles_per_seq)
    seq_len = jnp.where(is_prompt, SEQ, DEC_SEQ)
    has_prev = tile_in_seq > 0
    has_next = jnp.logical_and(jnp.logical_not(is_prompt), tile_in_seq < tiles_per_seq - 1)
    prev_keep = jnp.where(has_prev, 1.0, 0.0).astype(F32)
    next_keep = jnp.where(has_next, 1.0, 0.0).astype(F32)
    pos = tile_in_seq * TM_MIX + lax.broadcasted_iota(jnp.int32, (TM_MIX, 1), 0)
    lo_rows, hi_rows = HALO, HALO + TM_MIX

    pad_ref[0:HALO, :] = prev_ref[:, 0:HALF].astype(F32) * prev_keep
    pad_ref[lo_rows:hi_rows, :] = p_ref[:, 0:HALF].astype(F32)
    pad_ref[hi_rows:hi_rows + HALO, :] = next_ref[:, 0:HALF].astype(F32) * next_keep
    for gi, w in enumerate(POOL_WINDOWS):
        cs = slice(gi * POOL_GROUP, (gi + 1) * POOL_GROUP)
        s = None
        for o in range(-(w // 2), w - w // 2):
            t = pad_ref[lo_rows + o:hi_rows + o, cs]
            s = t if s is None else s + t
        lo = jnp.maximum(pos - w // 2, 0)
        hi = jnp.minimum(pos + (w - w // 2), seq_len)
        inv = 1.0 / (hi - lo).astype(F32)
        pooled = s * inv - pad_ref[lo_rows:hi_rows, cs]
        mixed = _dot(pooled.astype(BF16), pw_ref[gi]) * ps_ref[:, cs]
        y_ref[:, cs] = mixed.astype(BF16)

    c0, v0 = 2 * HALF, 3 * HALF
    pad_ref[0:HALO, :] = (prev_ref[:, c0:c0 + HALF].astype(F32) * prev_ref[:, v0:v0 + HALF].astype(F32)) * prev_keep
    pad_ref[lo_rows:hi_rows, :] = p_ref[:, c0:c0 + HALF].astype(F32) * p_ref[:, v0:v0 + HALF].astype(F32)
    pad_ref[hi_rows:hi_rows + HALO, :] = (next_ref[:, c0:c0 + HALF].astype(F32)
                                          * next_ref[:, v0:v0 + HALF].astype(F32)) * next_keep
    for ci in range(HALF // 128):
        cs = slice(ci * 128, (ci + 1) * 128)
        conv = (cw_ref[0:1, cs] * pad_ref[lo_rows - 1:hi_rows - 1, cs]
                + cw_ref[1:2, cs] * pad_ref[lo_rows:hi_rows, cs]
                + cw_ref[2:3, cs] * pad_ref[lo_rows + 1:hi_rows + 1, cs])
        bg = p_ref[:, HALF + ci * 128:HALF + (ci + 1) * 128].astype(F32)
        y_ref[:, HALF + ci * 128:HALF + (ci + 1) * 128] = (bg * conv).astype(BF16)


def _even_mixer(proj, pool_w, pool_scale, conv_w):
    n_tiles = N_TOK // TM_MIX
    per = TM_MIX // HALO
    n_halo_blocks = N_TOK // HALO
    return pl.pallas_call(
        _even_mixer_body,
        grid=(n_tiles,),
        in_specs=[
            pl.BlockSpec((TM_MIX, 4 * HALF), lambda i: (i, 0)),
            pl.BlockSpec((HALO, 4 * HALF), lambda i: (jnp.maximum(i * per - 1, 0), 0)),
            pl.BlockSpec((HALO, 4 * HALF), lambda i: (jnp.minimum((i + 1) * per, n_halo_blocks - 1), 0)),
            pl.BlockSpec((len(POOL_WINDOWS), POOL_GROUP, POOL_GROUP), lambda i: (0, 0, 0)),
            pl.BlockSpec((1, HALF), lambda i: (0, 0)),
            pl.BlockSpec((3, HALF), lambda i: (0, 0)),
        ],
        out_specs=pl.BlockSpec((TM_MIX, 2 * HALF), lambda i: (i, 0)),
        out_shape=jax.ShapeDtypeStruct((N_TOK, 2 * HALF), BF16),
        scratch_shapes=[pltpu.VMEM((TM_MIX + 2 * HALO, HALF), F32)],
        compiler_params=_cparams(("parallel",)),
        name="even_mixer",
    )(proj, proj, proj, pool_w.astype(BF16), pool_scale.reshape(1, HALF), conv_w)


TM_MLP = 1024
TF_MLP = 512
RC_MLP = 256


def _stream_rows(x_refs, rows, tm):
    if len(x_refs) == 1:
        return x_refs[0][rows, :]
    is_prompt = pl.program_id(0) < N_PROMPT_TOK // tm
    return jnp.where(is_prompt, x_refs[0][rows, :], x_refs[1][rows, :])


def _mlp_body(n_x, final, *refs):
    x_refs = refs[:n_x]
    ya_ref, yb_ref, wo_ref, m_ref, g_ref, fg_ref, w1_ref, w2_ref, o_ref = refs[n_x:n_x + 9]
    x1_ref, h_ref, acc_ref = refs[-3:]
    j = pl.program_id(1)
    last = pl.num_programs(1) - 1
    chunks = [slice(r * RC_MLP, (r + 1) * RC_MLP) for r in range(TM_MLP // RC_MLP)]

    def ff(rows):
        t = _dot(h_ref[rows, :], w1_ref[...])
        return _dot(jnp.square(jnp.maximum(t, 0.0)).astype(BF16), w2_ref[...])

    @pl.when(j == 0)
    def _():
        m = m_ref[...]
        gain = g_ref[...] * (1.0 + m[4:5])
        for rows in chunks:
            mix = _dot(ya_ref[rows, :], wo_ref[0:HALF, :]) + _dot(yb_ref[rows, :], wo_ref[HALF:2 * HALF, :])
            x1 = _stream_rows(x_refs, rows, TM_MLP) + m[2:3] * mix
            x1_ref[rows, :] = x1
            ms = jnp.mean(x1 * x1, axis=-1, keepdims=True)
            h_ref[rows, :] = (x1 * lax.rsqrt(ms + EPS) * gain + m[3:4]).astype(BF16)
            acc_ref[rows, :] = ff(rows)

    @pl.when(jnp.logical_and(j > 0, j < last))
    def _():
        acc_ref[...] += ff(slice(None))

    @pl.when(j == last)
    def _():
        gate = m_ref[5:6, :]
        for rows in chunks:
            out = x1_ref[rows, :] + gate * (acc_ref[rows, :] + ff(rows))
            if final:
                ms = jnp.mean(out * out, axis=-1, keepdims=True)
                out = out * lax.rsqrt(ms + EPS) * fg_ref[...]
            o_ref[rows, :] = out


def _mlp_call(x_arrays, x_specs, ya, ya_spec, yb, yb_spec, n_tiles, tile0, w_out, mods, layer, g2, final_g, w1, w2,
              final):
    tm, tf = TM_MLP, TF_MLP
    return pl.pallas_call(
        functools.partial(_mlp_body, len(x_arrays), final),
        grid=(n_tiles, D_FF // tf),
        in_specs=list(x_specs) + [
            ya_spec, yb_spec,
            pl.BlockSpec((D_MODEL, D_MODEL), lambda i, j: (0, 0)),
            pl.BlockSpec((None, None, 6, D_MODEL), lambda i, j: (layer, _cond_row(i + tile0, tm), 0, 0)),
            pl.BlockSpec((1, D_MODEL), lambda i, j: (0, 0)),
            pl.BlockSpec((1, D_MODEL), lambda i, j: (0, 0)),
            pl.BlockSpec((D_MODEL, tf), lambda i, j: (0, j)),
            pl.BlockSpec((tf, D_MODEL), lambda i, j: (j, 0)),
        ],
        out_specs=pl.BlockSpec((tm, D_MODEL), lambda i, j: (i, 0)),
        out_shape=jax.ShapeDtypeStruct((n_tiles * tm, D_MODEL), F32),
        scratch_shapes=[
            pltpu.VMEM((tm, D_MODEL), F32),
            pltpu.VMEM((tm, D_MODEL), BF16),
            pltpu.VMEM((tm, D_MODEL), F32),
        ],
        compiler_params=_cparams(("parallel", "arbitrary")),
        name="mlp",
    )(*x_arrays, ya, yb, w_out, mods, g2.reshape(1, D_MODEL), final_g.reshape(1, D_MODEL), w1, w2)


def _mlp(xs, ya, ya_col, yb, yb_col, w_out, mods, layer, g2, final_g, w1, w2, final):
    tm = TM_MLP
    npt = N_PROMPT_TOK // tm
    weights = (w_out, mods, layer, g2, final_g, w1, w2, final)

    def rows_spec(width, off, col):
        return pl.BlockSpec((tm, width), lambda i, j: (i + off, col))

    if not final:
        if isinstance(ya, tuple):
            ya = jnp.concatenate(ya, axis=0)
        return (_mlp_call(xs, _stream_specs(xs, tm), ya, rows_spec(HALF, 0, ya_col), yb, rows_spec(HALF, 0, yb_col),
                          N_TOK // tm, 0, *weights),)
    outs = []
    for s, (tile0, n_tiles) in enumerate(((0, npt), (npt, N_TOK // tm - npt))):
        x, x_off = (xs[0], tile0) if len(xs) == 1 else (xs[s], 0)
        a, a_off = (ya[s], 0) if isinstance(ya, tuple) else (ya, tile0)
        outs.append(_mlp_call((x,), [rows_spec(D_MODEL, x_off, 0)], a, rows_spec(HALF, a_off, ya_col),
                              yb, rows_spec(HALF, tile0, yb_col), n_tiles, tile0, *weights))
    return tuple(outs)


def _ctx_attn_body(q_ref, k_ref, v_ref, o_ref, ck_ref, cv_ref):
    scale = HEAD_DIM ** -0.5
    for h in range(N_HEADS):
        sl = slice(h * HEAD_DIM, (h + 1) * HEAD_DIM)
        q = q_ref[:, sl]
        k = k_ref[:, sl]
        v = v_ref[:, sl]
        ck_ref[h] = k.astype(F32)
        cv_ref[h] = v.astype(F32)
        s = _dot_nt(q, k) * scale
        m = jnp.max(s, axis=-1, keepdims=True)
        p = jnp.exp(s - m)
        l = jnp.sum(p, axis=-1, keepdims=True)
        o = _dot(p.astype(BF16), v) / l
        o_ref[:, sl] = o.astype(BF16)


def _ctx_attn(proj):
    return pl.pallas_call(
        _ctx_attn_body,
        grid=(BATCH,),
        in_specs=[
            pl.BlockSpec((SEQ, HALF), lambda b: (b, 0)),
            pl.BlockSpec((SEQ, HALF), lambda b: (b, 1)),
            pl.BlockSpec((SEQ, HALF), lambda b: (b, 2)),
        ],
        out_specs=[
            pl.BlockSpec((SEQ, HALF), lambda b: (b, 0)),
            pl.BlockSpec((None, None, N_HEADS, SEQ, HEAD_DIM), lambda b: (b, 0, 0, 0, 0)),
            pl.BlockSpec((None, None, N_HEADS, SEQ, HEAD_DIM), lambda b: (b, 0, 0, 0, 0)),
        ],
        out_shape=[
            jax.ShapeDtypeStruct((N_PROMPT_TOK, HALF), BF16),
            jax.ShapeDtypeStruct((BATCH, 1, N_HEADS, SEQ, HEAD_DIM), F32),
            jax.ShapeDtypeStruct((BATCH, 1, N_HEADS, SEQ, HEAD_DIM), F32),
        ],
        compiler_params=_cparams(("parallel",)),
        name="ctx_attn",
    )(proj, proj, proj)


NA_MASKED = -1e30
NA_QROWS = 4
NA_KROWS = NA_QROWS + WIN_H
NA_NQ = NA_QROWS * GRID_W
NA_NK = NA_KROWS * GRID_W
NA_BLOCKS = GRID_ROWS // NA_QROWS
NA_TYPES = 3


def _na_col_table(rpb):
    qc = jnp.arange(GRID_W)[:, None]
    kc = jnp.arange(GRID_W)[None, :]
    c0 = jnp.clip(qc - WIN_W // 2, 0, GRID_W - WIN_W)
    valid = (kc >= c0) & (kc < c0 + WIN_W)
    coff = jnp.clip(kc - qc, -(WIN_W - 1), WIN_W - 1) + (WIN_W - 1)
    onehot = (coff.reshape(1, -1) == jnp.arange(2 * WIN_W - 1)[:, None]).astype(F32)
    t = jnp.dot(rpb.astype(F32).reshape(N_HEADS * (2 * WIN_H - 1), 2 * WIN_W - 1), onehot,
                precision=lax.Precision.HIGHEST)
    t = t.reshape(N_HEADS, 2 * WIN_H - 1, GRID_W, GRID_W)
    return jnp.where(valid[None, None], t, NA_MASKED)


def _na_window_start(rb):
    return jnp.clip(NA_QROWS * rb - WIN_H // 2, 0, GRID_ROWS - NA_KROWS)


def _na_body(q_ref, k_ref, v_ref, ck_ref, cv_ref, tab_ref, o_ref, bias_ref, kc_ref, vc_ref, va_ref):
    @pl.when(pl.program_id(1) == 0)
    def _():
        masked = jnp.full((GRID_W, GRID_W), NA_MASKED, F32)
        for hh in range(2):
            for ty in range(NA_TYPES):
                q0 = (0, NA_QROWS, GRID_ROWS - NA_QROWS)[ty]
                ks = (0, 0, GRID_ROWS - NA_KROWS)[ty]
                for a in range(NA_QROWS):
                    r0 = min(max(q0 + a - WIN_H // 2, 0), GRID_ROWS - WIN_H)
                    for y in range(NA_KROWS):
                        kr = ks + y
                        ok = r0 <= kr < r0 + WIN_H
                        blk = tab_ref[hh, kr - (q0 + a) + WIN_H - 1] if ok else masked
                        bias_ref[ty, hh * NA_NQ + a * GRID_W:hh * NA_NQ + (a + 1) * GRID_W,
                                 y * GRID_W:(y + 1) * GRID_W] = blk

    lane = lax.broadcasted_iota(jnp.int32, (1, 2 * HEAD_DIM), 1)
    own = [lane < HEAD_DIM, lane >= HEAD_DIM]
    kc_ref[...] = jnp.concatenate([ck_ref[0], ck_ref[1]], axis=1).astype(BF16)
    vc = jnp.concatenate([cv_ref[0], cv_ref[1]], axis=1)
    v = v_ref[...]
    for hh in range(2):
        vc_ref[hh] = jnp.where(own[hh], vc, 1.0).astype(BF16)
        va_ref[hh] = jnp.where(own[hh], v, jnp.ones_like(v))

    def block(rb, ty):
        qs = pl.multiple_of(rb * NA_NQ, NA_NQ)
        ks = pl.multiple_of(_na_window_start(rb) * GRID_W, GRID_W)
        q = q_ref[pl.ds(qs, NA_NQ), :] * jnp.asarray(HEAD_DIM ** -0.5, BF16)
        q2 = jnp.concatenate([jnp.where(own[0], q, jnp.zeros_like(q)),
                              jnp.where(own[1], q, jnp.zeros_like(q))], axis=0)
        s_loc = _dot_nt(q2, k_ref[pl.ds(ks, NA_NK), :]) + bias_ref[ty]
        s_ctx = _dot_nt(q2, kc_ref[...])
        m = jnp.maximum(jnp.max(s_loc, axis=-1, keepdims=True), jnp.max(s_ctx, axis=-1, keepdims=True))
        p_loc = jnp.exp(s_loc - m).astype(BF16)
        p_ctx = jnp.exp(s_ctx - m).astype(BF16)
        out = None
        for hh in range(2):
            rows = slice(hh * NA_NQ, (hh + 1) * NA_NQ)
            num = _dot(p_loc[rows], va_ref[hh, pl.ds(ks, NA_NK), :]) + _dot(p_ctx[rows], vc_ref[hh])
            den = jnp.where(own[hh], pltpu.roll(num, HEAD_DIM, 1), 1.0)
            o = num / den
            out = o if out is None else jnp.where(own[0], out, o)
        o_ref[pl.ds(qs, NA_NQ), :] = out.astype(BF16)

    block(0, 0)

    def interior(rb, carry):
        block(rb, 1)
        return carry

    lax.fori_loop(1, NA_BLOCKS - 1, interior, 0)
    block(NA_BLOCKS - 1, 2)


def _na_attn(proj, cache_k, cache_v, col_table, layer_idx):
    row_blk0 = N_PROMPT_TOK // DEC_SEQ
    hp = N_HEADS // 2
    return pl.pallas_call(
        _na_body,
        grid=(hp, DEC_BATCH),
        in_specs=[
            pl.BlockSpec((DEC_SEQ, 128), lambda h, b: (row_blk0 + b, h)),
            pl.BlockSpec((DEC_SEQ, 128), lambda h, b: (row_blk0 + b, hp + h)),
            pl.BlockSpec((DEC_SEQ, 128), lambda h, b: (row_blk0 + b, 2 * hp + h)),
            pl.BlockSpec((None, None, 2, PAST_LEN, HEAD_DIM), lambda h, b: (b, layer_idx, h, 0, 0)),
            pl.BlockSpec((None, None, 2, PAST_LEN, HEAD_DIM), lambda h, b: (b, layer_idx, h, 0, 0)),
            pl.BlockSpec((2, 2 * WIN_H - 1, GRID_W, GRID_W), lambda h, b: (h, 0, 0, 0)),
        ],
        out_specs=pl.BlockSpec((DEC_SEQ, 128), lambda h, b: (b, h)),
        out_shape=jax.ShapeDtypeStruct((N_SAMPLE_TOK, HALF), BF16),
        scratch_shapes=[pltpu.VMEM((NA_TYPES, 2 * NA_NQ, NA_NK), F32),
                        pltpu.VMEM((PAST_LEN, 2 * HEAD_DIM), BF16),
                        pltpu.VMEM((2, PAST_LEN, 2 * HEAD_DIM), BF16),
                        pltpu.VMEM((2, DEC_SEQ, 2 * HEAD_DIM), BF16)],
        compiler_params=_cparams(("parallel", "arbitrary")),
        name="na_attn",
    )(proj, proj, proj, cache_k, cache_v, col_table)


N_POW = 24


def _s5_prep_body(vec_ref, bt_ref, c_ref, wend_ref, ktoe_ref, woutT_ref, apow_ref, pw_ref, pct_ref):
    L = S5_CHUNK
    p = pl.program_id(0)
    lane_blk = lax.broadcasted_iota(jnp.int32, (S5_CH, S5_W), 1) // S5_CH
    src_blk = (lane_blk + L - p) % L
    krows = []
    for d in range(2):
        v = vec_ref[d]
        lre, lim, step = v[0:1], v[1:2], v[2:3]
        a = lre * step
        b = lim * step
        ea = jnp.exp(a)
        nr = ea * jnp.cos(b) - 1.0
        ni = ea * jnp.sin(b)
        den = lre * lre + lim * lim
        qr = (nr * lre + ni * lim) / den
        qi = (ni * lre - nr * lim) / den
        bre, bim = bt_ref[d, 0], bt_ref[d, 1]
        bqr = bre * qr - bim * qi
        bqi = bre * qi + bim * qr
        cre, cim = c_ref[d, 0], c_ref[d, 1]

        e = lax.broadcasted_iota(jnp.int32, (N_POW, 1), 0).astype(F32)
        mag = jnp.exp(e * a)
        pw_ref[0] = mag * jnp.cos(e * b)
        pw_ref[1] = mag * jnp.sin(e * b)
        mag = jnp.exp((L * e) * a)
        apow_ref[d, 0] = mag * jnp.cos((L * e) * b)
        apow_ref[d, 1] = mag * jnp.sin((L * e) * b)

        def power(k):
            return pw_ref[0, k:k + 1, :], pw_ref[1, k:k + 1, :]

        def c_block(k):
            pr, pi = power(k)
            return cre * pr - cim * pi, -(cre * pi + cim * pr)

        for i in range(L):
            rows = pl.ds(pl.multiple_of(((i + p) % L) * S5_CH, S5_CH), S5_CH)
            pr, pi = power(L - 1 - i if d == 0 else i)
            wend_ref[rows, d * 256:d * 256 + 128] = (bqr * pr - bqi * pi).astype(BF16)
            wend_ref[rows, d * 256 + 128:d * 256 + 256] = (bqr * pi + bqi * pr).astype(BF16)
            mre, mim = c_block(i + 1 if d == 0 else L - i)
            woutT_ref[rows, d * 256:d * 256 + 128] = mre.astype(BF16)
            woutT_ref[rows, d * 256 + 128:d * 256 + 256] = mim.astype(BF16)
            lag_rows = slice(i * S5_CH, (i + 1) * S5_CH)
            mre, mim = c_block(i if d == 0 else L - 1 - i)
            pct_ref[lag_rows, 0:128] = mre
            pct_ref[lag_rows, 128:256] = mim

        bqcat = jnp.concatenate([bqr, bqi], axis=1)
        krows.append(_dot_nt(bqcat, pct_ref[...], precision=lax.Precision.HIGHEST))

    for i in range(L):
        rows = pl.ds(pl.multiple_of(((i + p) % L) * S5_CH, S5_CH), S5_CH)
        blk_f = pltpu.roll(krows[0], ((i + p) % L) * S5_CH, 1)
        blk_b = pltpu.roll(krows[1], ((i + 1 + p) % L) * S5_CH, 1)
        blk = jnp.where(src_blk >= i, blk_f, 0.0) + jnp.where(src_blk <= i, blk_b, 0.0)
        ktoe_ref[rows, :] = blk.astype(BF16)


def _s5_prep(lam_re, lam_im, log_step, b_re, b_im, c_re, c_im):
    def pair_vec(a):
        return a.astype(F32).reshape(2, S5_PAIRS, 1, S5_P).transpose(1, 0, 2, 3)

    step = jnp.broadcast_to(jnp.exp(log_step.astype(F32))[:, :, None], (2, N_SSM_GROUPS, SSM_STATE))
    vec = jnp.concatenate([pair_vec(lam_re), pair_vec(lam_im), pair_vec(step),
                           jnp.zeros((S5_PAIRS, 2, 5, S5_P), F32)], axis=2)

    def block_diag_ch_by_state(m):
        m = m.astype(F32).reshape(2, S5_PAIRS, 2, SSM_GROUP, SSM_STATE)
        eye = jnp.eye(2, dtype=F32)
        out = m[:, :, :, :, None, :] * eye[None, None, :, None, :, None]
        return out.reshape(2, S5_PAIRS, S5_CH, S5_P).transpose(1, 0, 2, 3)

    bt = jnp.stack([block_diag_ch_by_state(b_re.transpose(0, 1, 3, 2)),
                    block_diag_ch_by_state(b_im.transpose(0, 1, 3, 2))], axis=2)
    cc = jnp.stack([block_diag_ch_by_state(c_re), block_diag_ch_by_state(c_im)], axis=2)

    mat = jax.ShapeDtypeStruct((S5_PAIRS, S5_W, S5_W), BF16)
    mat_spec = pl.BlockSpec((None, S5_W, S5_W), lambda p: (p, 0, 0))
    return pl.pallas_call(
        _s5_prep_body,
        grid=(S5_PAIRS,),
        in_specs=[
            pl.BlockSpec((None, 2, 8, S5_P), lambda p: (p, 0, 0, 0)),
            pl.BlockSpec((None, 2, 2, S5_CH, S5_P), lambda p: (p, 0, 0, 0, 0)),
            pl.BlockSpec((None, 2, 2, S5_CH, S5_P), lambda p: (p, 0, 0, 0, 0)),
        ],
        out_specs=[mat_spec, mat_spec, mat_spec,
                   pl.BlockSpec((None, 2, 2, N_POW, S5_P), lambda p: (p, 0, 0, 0, 0))],
        out_shape=[mat, mat, mat, jax.ShapeDtypeStruct((S5_PAIRS, 2, 2, N_POW, S5_P), F32)],
        scratch_shapes=[pltpu.VMEM((2, N_POW, S5_P), F32), pltpu.VMEM((S5_W, 2 * S5_P), F32)],
        compiler_params=_cparams(("parallel",)),
        name="s5_prep",
    )(vec, bt, cc)


S5_SEG_ROWS = 256
S5_N_SEG = N_CHUNK_ROWS // S5_SEG_ROWS
S5_SUB = S5_SEG_ROWS // S5_CHUNK
S5_SUB_PER_SEQ = (DEC_SEQ // S5_CHUNK) // S5_CHUNK


def _s5_body(tu_ref, s0_ref, wend_ref, ktoe_ref, woutT_ref, apow_ref, ty_ref, fin_ref, z_ref, xs_ref, g_ref):
    seg = pl.program_id(0)
    p = pl.program_id(1)
    L = S5_CHUNK
    lane_blk = lax.broadcasted_iota(jnp.int32, (S5_SEG_ROWS, 128), 1) // S5_CH

    tiles = []
    for t in range(LANE_TILES):
        tile = None
        for k in range(128 // S5_CH):
            j = (LANE_TILES * t + k + L - p) % L
            blk = tu_ref[j, t]
            tile = blk if tile is None else jnp.where(lane_blk == k, blk, tile)
        tiles.append(tile)
    s = jnp.concatenate(tiles, axis=1)
    z = _dot(s, wend_ref[...])
    for q in range(LANE_TILES):
        z_ref[q] = z[:, q * 128:(q + 1) * 128]

    ends = []
    for d in range(2):
        are, aim = apow_ref[d, 0, 1:2, :], apow_ref[d, 1, 1:2, :]
        xr = jnp.zeros((S5_SUB, S5_P), F32)
        xi = jnp.zeros((S5_SUB, S5_P), F32)
        for c in (range(L) if d == 0 else reversed(range(L))):
            rows = pl.ds(c, S5_SUB, stride=L)
            xs_ref[2 * d, rows, :] = xr
            xs_ref[2 * d + 1, rows, :] = xi
            zr = z_ref[2 * d, rows, :]
            zi = z_ref[2 * d + 1, rows, :]
            xr, xi = are * xr - aim * xi + zr, are * xi + aim * xr + zi
        ends.append((xr, xi))
        fin_ref[d, 0] = xr
        fin_ref[d, 1] = xi

    @pl.when(seg > 0)
    def _():
        for d in range(2):
            er, ei = ends[d]
            a16r, a16i = apow_ref[d, 0, L:L + 1, :], apow_ref[d, 1, L:L + 1, :]
            for bb in range(S5_SUB // S5_SUB_PER_SEQ):
                gr = s0_ref[bb, 2 * d:2 * d + 1, :]
                gi = s0_ref[bb, 2 * d + 1:2 * d + 2, :]
                order = range(S5_SUB_PER_SEQ) if d == 0 else reversed(range(S5_SUB_PER_SEQ))
                prev = None
                for sub in order:
                    row = bb * S5_SUB_PER_SEQ + sub
                    if prev is not None:
                        gr, gi = (a16r * gr - a16i * gi + er[prev:prev + 1, :],
                                  a16r * gi + a16i * gr + ei[prev:prev + 1, :])
                    g_ref[0, row:row + 1, :] = gr
                    g_ref[1, row:row + 1, :] = gi
                    prev = row
            gr, gi = g_ref[0], g_ref[1]
            for c in range(L):
                k = c if d == 0 else L - 1 - c
                pr, pi = apow_ref[d, 0, k:k + 1, :], apow_ref[d, 1, k:k + 1, :]
                rows = pl.ds(c, S5_SUB, stride=L)
                xs_ref[2 * d, rows, :] = xs_ref[2 * d, rows, :] + (pr * gr - pi * gi)
                xs_ref[2 * d + 1, rows, :] = xs_ref[2 * d + 1, rows, :] + (pr * gi + pi * gr)

    xs = jnp.concatenate([xs_ref[q].astype(BF16) for q in range(LANE_TILES)], axis=1)
    y = _dot(s, ktoe_ref[...]) + _dot_nt(xs, woutT_ref[...])

    @pl.when(p == 0)
    def _():
        ty_ref[...] = jnp.zeros_like(ty_ref)

    for t in range(LANE_TILES):
        yt = y[:, t * 128:(t + 1) * 128]
        for k in range(128 // S5_CH):
            j = (LANE_TILES * t + k + L - p) % L
            ty_ref[j, t] = jnp.where(lane_blk == k, yt, ty_ref[j, t])


def _s5_scan(tu, s0, wend, ktoe, woutT, apow):
    mat_spec = pl.BlockSpec((None, S5_W, S5_W), lambda g, p: (p, 0, 0))
    seqs = S5_SUB // S5_SUB_PER_SEQ
    return pl.pallas_call(
        _s5_body,
        grid=(S5_N_SEG, S5_PAIRS),
        in_specs=[
            pl.BlockSpec((S5_CHUNK, LANE_TILES, S5_SEG_ROWS, 128), lambda g, p: (0, 0, g, 0)),
            pl.BlockSpec((None, seqs, 4, S5_P), lambda g, p: (p, jnp.maximum(g - 1, 0), 0, 0)),
            mat_spec, mat_spec, mat_spec,
            pl.BlockSpec((None, 2, 2, N_POW, S5_P), lambda g, p: (p, 0, 0, 0, 0)),
        ],
        out_specs=[
            pl.BlockSpec((S5_CHUNK, LANE_TILES, S5_SEG_ROWS, 128), lambda g, p: (0, 0, g, 0)),
            pl.BlockSpec((None, 2, 2, S5_SUB, S5_P), lambda g, p: (p, 0, 0, g, 0)),
        ],
        out_shape=[
            jax.ShapeDtypeStruct((S5_CHUNK, LANE_TILES, N_CHUNK_ROWS, 128), F32),
            jax.ShapeDtypeStruct((S5_PAIRS, 2, 2, S5_N_SEG * S5_SUB, S5_P), F32),
        ],
        scratch_shapes=[pltpu.VMEM((LANE_TILES, S5_SEG_ROWS, 128), F32),
                        pltpu.VMEM((LANE_TILES, S5_SEG_ROWS, 128), F32),
                        pltpu.VMEM((2, S5_SUB, S5_P), F32)],
        compiler_params=_cparams(("parallel", "arbitrary")),
        name="s5_scan",
    )(tu, s0, wend, ktoe, woutT, apow)


TM_GLU = 1024


def _glu_body(tu_ref, ty_ref, d_ref, gw_ref, gb_ref, o_ref, u_ref, ys_ref):
    rows = TM_GLU // S5_CHUNK
    for j in range(S5_CHUNK):
        back = ((S5_CHUNK - j) % S5_CHUNK) * S5_CH
        uj = jnp.concatenate([tu_ref[j, t].astype(F32) for t in range(LANE_TILES)], axis=1)
        yj = jnp.concatenate([ty_ref[j, t] for t in range(LANE_TILES)], axis=1)
        if back:
            uj = pltpu.roll(uj, back, 1)
            yj = pltpu.roll(yj, back, 1)
        for t in range(LANE_TILES):
            u_ref[t, pl.ds(j, rows, stride=S5_CHUNK), :] = uj[:, t * 128:(t + 1) * 128]
            ys_ref[t, pl.ds(j, rows, stride=S5_CHUNK), :] = yj[:, t * 128:(t + 1) * 128]
    u = jnp.concatenate([u_ref[t] for t in range(LANE_TILES)], axis=1)
    y = d_ref[...] * u + jnp.concatenate([ys_ref[t] for t in range(LANE_TILES)], axis=1)
    cdf = 0.5 * (1.0 + jnp.tanh(math.sqrt(2.0 / math.pi) * (y + 0.044715 * (y * y * y))))
    y = y * cdf
    gate = jax.nn.sigmoid(_dot(y.astype(BF16), gw_ref[...]) + gb_ref[...])
    o_ref[...] = (y * gate).astype(BF16)


def _glu(tu, ty, ssm_d, glu_w, glu_b):
    tm = TM_GLU
    rows = tm // S5_CHUNK
    lay_spec = pl.BlockSpec((S5_CHUNK, LANE_TILES, rows, 128), lambda i: (0, 0, i, 0))
    return pl.pallas_call(
        _glu_body,
        grid=(N_TOK // tm,),
        in_specs=[
            lay_spec, lay_spec,
            pl.BlockSpec((1, HALF), lambda i: (0, 0)),
            pl.BlockSpec((HALF, HALF), lambda i: (0, 0)),
            pl.BlockSpec((1, HALF), lambda i: (0, 0)),
        ],
        out_specs=pl.BlockSpec((tm, HALF), lambda i: (i, 0)),
        out_shape=jax.ShapeDtypeStruct((N_TOK, HALF), BF16),
        scratch_shapes=[pltpu.VMEM((LANE_TILES, tm, 128), F32), pltpu.VMEM((LANE_TILES, tm, 128), F32)],
        compiler_params=_cparams(("parallel",)),
        name="s5_glu",
    )(tu, ty, ssm_d.reshape(1, HALF), glu_w.astype(BF16), glu_b.reshape(1, HALF))


def kernel(x_prompt, x_sample, c, cache_k, cache_v, state_s5_re, state_s5_im, c_ctx, norm1_g, norm2_g, ada_w, ada_b, mlp_w1, mlp_w2, ab_w_in, pool_w, pool_scale, conv_w, ab_w_out, cd_w_in, na_rpb, ssm_lambda_re, ssm_lambda_im, ssm_log_step, ssm_b_re, ssm_b_im, ssm_c_re, ssm_c_im, ssm_d, glu_w, glu_b, cd_w_out, final_g):
    depth = ada_w.shape[0]
    xs = (x_prompt.reshape(N_PROMPT_TOK, D_MODEL), x_sample.reshape(N_SAMPLE_TOK, D_MODEL))
    cond = jnp.concatenate([c_ctx[None, :], c, jnp.zeros((N_COND - 1 - DEC_BATCH, D_MODEL), F32)], axis=0)
    mods = _adaln(cond, ada_w, ada_b).reshape(depth, N_COND, 6, D_MODEL)

    new_k, new_v, new_re, new_im = [], [], [], []
    for layer in range(depth):
        i = layer // 2
        final = layer == depth - 1
        if layer % 2 == 0:
            proj = _inproj(xs, mods, layer, norm1_g[layer], ab_w_in[i].astype(BF16))
            y = _even_mixer(proj, pool_w[i], pool_scale[i], conv_w[i])
            ya, ya_col, yb, yb_col = y, 0, y, 1
            w_out = ab_w_out[i]
        else:
            qkv, tu = _inproj_odd(xs[0], mods, layer, norm1_g[layer], cd_w_in[i].astype(BF16))
            o_p, ck, cv = _ctx_attn(qkv)
            new_k.append(ck)
            new_v.append(cv)
            o_s = _na_attn(qkv, cache_k, cache_v, _na_col_table(na_rpb[i]), i)
            attn = (o_p, o_s)

            wend, ktoe, woutT, apow = _s5_prep(ssm_lambda_re[i], ssm_lambda_im[i], ssm_log_step[i],
                                               ssm_b_re[i], ssm_b_im[i], ssm_c_re[i], ssm_c_im[i])

            def pair_state(s):
                return s.astype(F32).reshape(DEC_BATCH, 2, S5_PAIRS, S5_P).transpose(2, 0, 1, 3)

            s0 = jnp.stack([pair_state(state_s5_re[:, i]), pair_state(state_s5_im[:, i])], axis=3)
            s0 = s0.reshape(S5_PAIRS, DEC_BATCH, 4, S5_P)
            ty, fin = _s5_scan(tu, s0, wend, ktoe, woutT, apow)
            fin = fin[:, :, :, :BATCH]
            new_re.append(fin[:, :, 0].transpose(2, 1, 0, 3).reshape(BATCH, 2, N_SSM_GROUPS, SSM_STATE))
            new_im.append(fin[:, :, 1].transpose(2, 1, 0, 3).reshape(BATCH, 2, N_SSM_GROUPS, SSM_STATE))
            d_out = _glu(tu, ty, ssm_d[i], glu_w[i], glu_b[i])
            ya, ya_col, yb, yb_col = attn, 0, d_out, 0
            w_out = cd_w_out[i]
        xs = _mlp(xs, ya, ya_col, yb, yb_col, w_out.astype(BF16), mods, layer, norm2_g[layer], final_g,
                  mlp_w1[layer].astype(BF16), mlp_w2[layer].astype(BF16), final)

    y_prompt = xs[0].reshape(BATCH, SEQ, D_MODEL)
    y_sample = xs[1].reshape(DEC_BATCH, DEC_SEQ, D_MODEL)
    return (y_prompt, y_sample, jnp.concatenate(new_k, axis=1), jnp.concatenate(new_v, axis=1),
            jnp.stack(new_re, axis=1), jnp.stack(new_im, axis=1))
```

```python
import functools
import math

import jax
import jax.numpy as jnp
from jax import lax
from jax.experimental import pallas as pl
from jax.experimental.pallas import tpu as pltpu

F32 = jnp.float32
BF16 = jnp.bfloat16

D_MODEL = 1024
BATCH = 16
SEQ = 256
DEC_BATCH = 4
DEC_SEQ = 2048
PAST_LEN = 256
GRID_W = 64
GRID_ROWS = DEC_SEQ // GRID_W
HALF = 512
POOL_WINDOWS = (2, 4, 8, 16)
POOL_GROUP = 128
HEAD_DIM = 64
N_HEADS = 8
WIN_H = 8
WIN_W = 16
SSM_GROUP = 16
N_SSM_GROUPS = 32
SSM_STATE = 64
D_FF = 4096
EPS = 1e-6

N_PROMPT_TOK = BATCH * SEQ
N_SAMPLE_TOK = DEC_BATCH * DEC_SEQ
N_TOK = N_PROMPT_TOK + N_SAMPLE_TOK
N_COND = 8

S5_CHUNK = 16
S5_PAIRS = N_SSM_GROUPS // 2
S5_CH = 2 * SSM_GROUP
S5_P = 2 * SSM_STATE
S5_W = S5_CHUNK * S5_CH
S5_BP_SAMPLE = 8

VMEM_LIMIT = 56 * 1024 * 1024


def _cparams(sem):
    return pltpu.CompilerParams(dimension_semantics=sem, vmem_limit_bytes=VMEM_LIMIT)


def _cond_row(i, tm):
    npt = N_PROMPT_TOK // tm
    per = DEC_SEQ // tm
    return jnp.where(i < npt, 0, 1 + (i - npt) // per)


def _normmod(x, g, shift, scale):
    ms = jnp.mean(x * x, axis=-1, keepdims=True)
    return (x * lax.rsqrt(ms + EPS) * g) * (1.0 + scale) + shift


def _dot(a, b):
    return jnp.dot(a, b, preferred_element_type=F32)


def _dot_nt(a, b, precision=None):
    return lax.dot_general(a, b, (((1,), (1,)), ((), ())), preferred_element_type=F32, precision=precision)


def _adaln_body(c_ref, w_ref, b_ref, o_ref):
    c = c_ref[...]
    s = c * jax.nn.sigmoid(c)
    o_ref[...] = _dot(s.astype(BF16), w_ref[...].astype(BF16)) + b_ref[...]


def _adaln(cond, ada_w, ada_b):
    depth = ada_w.shape[0]
    nj = 6
    return pl.pallas_call(
        _adaln_body,
        grid=(depth, nj),
        in_specs=[
            pl.BlockSpec((N_COND, D_MODEL), lambda l, j: (0, 0)),
            pl.BlockSpec((None, D_MODEL, D_MODEL), lambda l, j: (l, 0, j)),
            pl.BlockSpec((None, 1, D_MODEL), lambda l, j: (l, 0, j)),
        ],
        out_specs=pl.BlockSpec((None, N_COND, D_MODEL), lambda l, j: (l, 0, j)),
        out_shape=jax.ShapeDtypeStruct((depth, N_COND, 6 * D_MODEL), F32),
        compiler_params=_cparams(("arbitrary", "arbitrary")),
        name="adaln",
    )(cond, ada_w, ada_b.reshape(depth, 1, 6 * D_MODEL))


TM_PROJ = 512


def _stream_specs(xs, tm):
    if len(xs) == 1:
        return [pl.BlockSpec((tm, D_MODEL), lambda i, *_: (i, 0))]
    npt = N_PROMPT_TOK // tm
    return [pl.BlockSpec((tm, D_MODEL), lambda i, *_: (jnp.minimum(i, npt - 1), 0)),
            pl.BlockSpec((tm, D_MODEL), lambda i, *_: (jnp.maximum(i - npt, 0), 0))]


def _stream_tile(x_refs, tm):
    if len(x_refs) == 1:
        return x_refs[0][...]
    is_prompt = pl.program_id(0) < N_PROMPT_TOK // tm
    return jnp.where(is_prompt, x_refs[0][...], x_refs[1][...])


N_CHUNK_ROWS = N_TOK // S5_CHUNK
LANE_TILES = S5_W // 128


def _inproj_odd_body(x_ref, m_ref, g_ref, w_ref, qkv_ref, tu_ref, u_ref):
    m = m_ref[...]
    h = _normmod(x_ref[...], g_ref[...], m[0:1], m[1:2])
    acc = _dot(h.astype(BF16), w_ref[...])
    qkv_ref[...] = acc[:, 0:3 * HALF].astype(BF16)
    for t in range(LANE_TILES):
        u_ref[t] = acc[:, 3 * HALF + t * 128:3 * HALF + (t + 1) * 128]
    rows = TM_PROJ // S5_CHUNK
    for j in range(S5_CHUNK):
        uj = jnp.concatenate([u_ref[t, pl.ds(j, rows, stride=S5_CHUNK), :] for t in range(LANE_TILES)], axis=1)
        if j:
            uj = pltpu.roll(uj, j * S5_CH, 1)
        for t in range(LANE_TILES):
            tu_ref[j, t] = uj[:, t * 128:(t + 1) * 128].astype(BF16)


def _inproj_odd(x, mods, layer, g, w):
    tm = TM_PROJ
    rows = tm // S5_CHUNK
    return pl.pallas_call(
        _inproj_odd_body,
        grid=(N_TOK // tm,),
        in_specs=[
            pl.BlockSpec((tm, D_MODEL), lambda i: (i, 0)),
            pl.BlockSpec((None, None, 6, D_MODEL), lambda i: (layer, _cond_row(i, tm), 0, 0)),
            pl.BlockSpec((1, D_MODEL), lambda i: (0, 0)),
            pl.BlockSpec((D_MODEL, 4 * HALF), lambda i: (0, 0)),
        ],
        out_specs=[
            pl.BlockSpec((tm, 3 * HALF), lambda i: (i, 0)),
            pl.BlockSpec((S5_CHUNK, LANE_TILES, rows, 128), lambda i: (0, 0, i, 0)),
        ],
        out_shape=[
            jax.ShapeDtypeStruct((N_TOK, 3 * HALF), BF16),
            jax.ShapeDtypeStruct((S5_CHUNK, LANE_TILES, N_CHUNK_ROWS, 128), BF16),
        ],
        scratch_shapes=[pltpu.VMEM((LANE_TILES, tm, 128), F32)],
        compiler_params=_cparams(("parallel",)),
        name="inproj_odd",
    )(x, mods, g.reshape(1, D_MODEL), w)


TM_MIX = 256
HALO = 16
RING = 4
LAG = 2


def _even_body(n_x, *refs):
    x_refs = refs[:n_x]
    m_ref, g_ref, w_ref, pw_ref, ps_ref, cw_ref, y_ref, proj_ref, pad_ref, h_ref, stage_ref = refs[n_x:]
    i = pl.program_id(0)

    @pl.when(i == 0)
    def _():
        proj_ref[...] = jnp.zeros_like(proj_ref)

    t = jnp.maximum(i - LAG, 0)
    prev_slot, cur_slot, next_slot = (i + 1) % RING, (i + 2) % RING, (i + 3) % RING
    n_prompt_tiles = N_PROMPT_TOK // TM_MIX
    tiles_per_seq = DEC_SEQ // TM_MIX
    is_prompt = t < n_prompt_tiles
    tile_in_seq = jnp.where(is_prompt, 0, (t - n_prompt_tiles) % tiles_per_seq)
    seq_len = jnp.where(is_prompt, SEQ, DEC_SEQ)
    has_prev = tile_in_seq > 0
    has_next = jnp.logical_and(jnp.logical_not(is_prompt), tile_in_seq < tiles_per_seq - 1)
    prev_keep = jnp.where(has_prev, 1.0, 0.0).astype(F32)
    next_keep = jnp.where(has_next, 1.0, 0.0).astype(F32)
    pos = tile_in_seq * TM_MIX + lax.broadcasted_iota(jnp.int32, (TM_MIX, 1), 0)
    lo_rows, hi_rows = HALO, HALO + TM_MIX
    tail = slice(TM_MIX - HALO, TM_MIX)
    head = slice(0, HALO)

    m = m_ref[...]
    h_ref[...] = _normmod(_stream_tile(x_refs, TM_MIX), g_ref[...], m[0:1], m[1:2]).astype(BF16)
    n_pieces = len(POOL_WINDOWS) + HALF // 128
    pw = (4 * HALF) // n_pieces

    def project(piece):
        cols = slice(piece * pw, (piece + 1) * pw)
        stage_ref[:, cols] = _dot(h_ref[...], w_ref[:, cols])

    for gi, w in enumerate(POOL_WINDOWS):
        project(gi)
        cs = slice(gi * POOL_GROUP, (gi + 1) * POOL_GROUP)
        pad_ref[0:HALO, cs] = proj_ref[prev_slot, tail, cs] * prev_keep
        pad_ref[lo_rows:hi_rows, cs] = proj_ref[cur_slot, :, cs]
        pad_ref[hi_rows:hi_rows + HALO, cs] = proj_ref[next_slot, head, cs] * next_keep
        s = None
        for o in range(-(w // 2), w - w // 2):
            v = pad_ref[lo_rows + o:hi_rows + o, cs]
            s = v if s is None else s + v
        lo = jnp.maximum(pos - w // 2, 0)
        hi = jnp.minimum(pos + (w - w // 2), seq_len)
        inv = 1.0 / (hi - lo).astype(F32)
        pooled = s * inv - pad_ref[lo_rows:hi_rows, cs]
        mixed = _dot(pooled.astype(BF16), pw_ref[gi]) * ps_ref[:, cs]
        y_ref[:, cs] = mixed.astype(BF16)

    for ci in range(HALF // 128):
        project(len(POOL_WINDOWS) + ci)
        cs = slice(ci * 128, (ci + 1) * 128)
        bs = slice(HALF + ci * 128, HALF + (ci + 1) * 128)
        gs = slice(2 * HALF + ci * 128, 2 * HALF + (ci + 1) * 128)
        vs = slice(3 * HALF + ci * 128, 3 * HALF + (ci + 1) * 128)
        pad_ref[0:HALO, cs] = (proj_ref[prev_slot, tail, gs] * proj_ref[prev_slot, tail, vs]) * prev_keep
        pad_ref[lo_rows:hi_rows, cs] = proj_ref[cur_slot, :, gs] * proj_ref[cur_slot, :, vs]
        pad_ref[hi_rows:hi_rows + HALO, cs] = (proj_ref[next_slot, head, gs] * proj_ref[next_slot, head, vs]) * next_keep
        conv = (cw_ref[0:1, cs] * pad_ref[lo_rows - 1:hi_rows - 1, cs]
                + cw_ref[1:2, cs] * pad_ref[lo_rows:hi_rows, cs]
                + cw_ref[2:3, cs] * pad_ref[lo_rows + 1:hi_rows + 1, cs])
        y_ref[:, bs] = (proj_ref[cur_slot, :, bs] * conv).astype(BF16)

    proj_ref[i % RING] = stage_ref[...]


def _even_layer_mix(xs, mods, layer, g, w, pool_w, pool_scale, conv_w):
    tm = TM_MIX
    n_tiles = N_TOK // tm
    npt = N_PROMPT_TOK // tm
    last = n_tiles - 1
    if len(xs) == 1:
        x_specs = [pl.BlockSpec((tm, D_MODEL), lambda i: (jnp.minimum(i, last), 0))]
    else:
        x_specs = [pl.BlockSpec((tm, D_MODEL), lambda i: (jnp.minimum(i, npt - 1), 0)),
                   pl.BlockSpec((tm, D_MODEL), lambda i: (jnp.clip(i - npt, 0, last - npt), 0))]
    return pl.pallas_call(
        functools.partial(_even_body, len(xs)),
        grid=(n_tiles + LAG,),
        in_specs=x_specs + [
            pl.BlockSpec((None, None, 6, D_MODEL), lambda i: (layer, _cond_row(jnp.minimum(i, last), tm), 0, 0)),
            pl.BlockSpec((1, D_MODEL), lambda i: (0, 0)),
            pl.BlockSpec((D_MODEL, 4 * HALF), lambda i: (0, 0)),
            pl.BlockSpec((len(POOL_WINDOWS), POOL_GROUP, POOL_GROUP), lambda i: (0, 0, 0)),
            pl.BlockSpec((1, HALF), lambda i: (0, 0)),
            pl.BlockSpec((3, HALF), lambda i: (0, 0)),
        ],
        out_specs=pl.BlockSpec((tm, 2 * HALF), lambda i: (jnp.maximum(i - LAG, 0), 0)),
        out_shape=jax.ShapeDtypeStruct((N_TOK, 2 * HALF), BF16),
        scratch_shapes=[pltpu.VMEM((RING, tm, 4 * HALF), F32), pltpu.VMEM((tm + 2 * HALO, HALF), F32),
                        pltpu.VMEM((tm, D_MODEL), BF16), pltpu.VMEM((tm, 4 * HALF), F32)],
        compiler_params=_cparams(("arbitrary",)),
        name="even_layer_mix",
    )(*xs, mods, g.reshape(1, D_MODEL), w, pool_w.astype(BF16), pool_scale.reshape(1, HALF), conv_w)


TM_MLP = 1024
TF_MLP = 1024
TF_MLP_SPLIT = 512
RC_MLP = 256


def _stream_rows(x_refs, rows, tm):
    if len(x_refs) == 1:
        return x_refs[0][rows, :]
    is_prompt = pl.program_id(0) < N_PROMPT_TOK // tm
    return jnp.where(is_prompt, x_refs[0][rows, :], x_refs[1][rows, :])


def _mlp_body(n_x, final, *refs):
    x_refs = refs[:n_x]
    ya_ref, yb_ref, wo_ref, m_ref, g_ref, fg_ref, w1_ref, w2_ref, o_ref = refs[n_x:n_x + 9]
    x1_ref, h_ref, acc_ref = refs[-3:]
    j = pl.program_id(1)
    last = pl.num_programs(1) - 1
    chunks = [slice(r * RC_MLP, (r + 1) * RC_MLP) for r in range(TM_MLP // RC_MLP)]

    def ff(rows):
        t = _dot(h_ref[rows, :], w1_ref[...])
        return _dot(jnp.square(jnp.maximum(t, 0.0)).astype(BF16), w2_ref[...])

    @pl.when(j == 0)
    def _():
        m = m_ref[...]
        gain = g_ref[...] * (1.0 + m[4:5])
        for rows in chunks:
            mix = _dot(ya_ref[rows, :], wo_ref[0:HALF, :]) + _dot(yb_ref[rows, :], wo_ref[HALF:2 * HALF, :])
            x1 = _stream_rows(x_refs, rows, TM_MLP) + m[2:3] * mix
            x1_ref[rows, :] = x1
            ms = jnp.mean(x1 * x1, axis=-1, keepdims=True)
            h_ref[rows, :] = (x1 * lax.rsqrt(ms + EPS) * gain + m[3:4]).astype(BF16)
            acc_ref[rows, :] = ff(rows)

    @pl.when(jnp.logical_and(j > 0, j < last))
    def _():
        acc_ref[...] += ff(slice(None))

    @pl.when(j == last)
    def _():
        gate = m_ref[5:6, :]
        for rows in chunks:
            out = x1_ref[rows, :] + gate * (acc_ref[rows, :] + ff(rows))
            if final:
                ms = jnp.mean(out * out, axis=-1, keepdims=True)
                out = out * lax.rsqrt(ms + EPS) * fg_ref[...]
            o_ref[rows, :] = out


def _mlp_call(x_arrays, x_specs, ya, ya_spec, yb, yb_spec, n_tiles, tile0, w_out, mods, layer, g2, final_g, w1, w2,
              final):
    tm = TM_MLP
    tf = TF_MLP if len(x_arrays) == 1 else TF_MLP_SPLIT
    return pl.pallas_call(
        functools.partial(_mlp_body, len(x_arrays), final),
        grid=(n_tiles, D_FF // tf),
        in_specs=list(x_specs) + [
            ya_spec, yb_spec,
            pl.BlockSpec((D_MODEL, D_MODEL), lambda i, j: (0, 0)),
            pl.BlockSpec((None, None, 6, D_MODEL), lambda i, j: (layer, _cond_row(i + tile0, tm), 0, 0)),
            pl.BlockSpec((1, D_MODEL), lambda i, j: (0, 0)),
            pl.BlockSpec((1, D_MODEL), lambda i, j: (0, 0)),
            pl.BlockSpec((D_MODEL, tf), lambda i, j: (0, j)),
            pl.BlockSpec((tf, D_MODEL), lambda i, j: (j, 0)),
        ],
        out_specs=pl.BlockSpec((tm, D_MODEL), lambda i, j: (i, 0)),
        out_shape=jax.ShapeDtypeStruct((n_tiles * tm, D_MODEL), F32),
        scratch_shapes=[
            pltpu.VMEM((tm, D_MODEL), F32),
            pltpu.VMEM((tm, D_MODEL), BF16),
            pltpu.VMEM((tm, D_MODEL), F32),
        ],
        compiler_params=_cparams(("parallel", "arbitrary")),
        name="mlp",
    )(*x_arrays, ya, yb, w_out, mods, g2.reshape(1, D_MODEL), final_g.reshape(1, D_MODEL), w1, w2)


def _mlp(xs, ya, ya_col, yb, yb_col, w_out, mods, layer, g2, final_g, w1, w2, final):
    tm = TM_MLP
    npt = N_PROMPT_TOK // tm
    weights = (w_out, mods, layer, g2, final_g, w1, w2, final)

    def rows_spec(width, off, col):
        return pl.BlockSpec((tm, width), lambda i, j: (i + off, col))

    if not final:
        if isinstance(ya, tuple):
            ya = jnp.concatenate(ya, axis=0)
        return (_mlp_call(xs, _stream_specs(xs, tm), ya, rows_spec(HALF, 0, ya_col), yb, rows_spec(HALF, 0, yb_col),
                          N_TOK // tm, 0, *weights),)
    outs = []
    for s, (tile0, n_tiles) in enumerate(((0, npt), (npt, N_TOK // tm - npt))):
        x, x_off = (xs[0], tile0) if len(xs) == 1 else (xs[s], 0)
        a, a_off = (ya[s], 0) if isinstance(ya, tuple) else (ya, tile0)
        outs.append(_mlp_call((x,), [rows_spec(D_MODEL, x_off, 0)], a, rows_spec(HALF, a_off, ya_col),
                              yb, rows_spec(HALF, tile0, yb_col), n_tiles, tile0, *weights))
    return tuple(outs)


def _ctx_attn_body(q_ref, k_ref, v_ref, o_ref, ck_ref, cv_ref):
    lane = lax.broadcasted_iota(jnp.int32, (1, 2 * HEAD_DIM), 1)
    own = [lane < HEAD_DIM, lane >= HEAD_DIM]
    for h in range(N_HEADS):
        sl = slice(h * HEAD_DIM, (h + 1) * HEAD_DIM)
        ck_ref[h] = k_ref[:, sl].astype(F32)
        cv_ref[h] = v_ref[:, sl].astype(F32)
    for hp in range(N_HEADS // 2):
        sl = slice(hp * 2 * HEAD_DIM, (hp + 1) * 2 * HEAD_DIM)
        q = q_ref[:, sl] * jnp.asarray(HEAD_DIM ** -0.5, BF16)
        k = k_ref[:, sl]
        v = v_ref[:, sl]
        q2 = jnp.concatenate([jnp.where(own[0], q, jnp.zeros_like(q)),
                              jnp.where(own[1], q, jnp.zeros_like(q))], axis=0)
        s = _dot_nt(q2, k)
        m = jnp.max(s, axis=-1, keepdims=True)
        p = jnp.exp(s - m).astype(BF16)
        out = None
        for hh in range(2):
            num = _dot(p[hh * SEQ:(hh + 1) * SEQ], jnp.where(own[hh], v, jnp.ones_like(v)))
            den = jnp.where(own[hh], pltpu.roll(num, HEAD_DIM, 1), 1.0)
            o = num / den
            out = o if out is None else jnp.where(own[0], out, o)
        o_ref[:, sl] = out.astype(BF16)


def _ctx_attn(proj):
    return pl.pallas_call(
        _ctx_attn_body,
        grid=(BATCH,),
        in_specs=[
            pl.BlockSpec((SEQ, HALF), lambda b: (b, 0)),
            pl.BlockSpec((SEQ, HALF), lambda b: (b, 1)),
            pl.BlockSpec((SEQ, HALF), lambda b: (b, 2)),
        ],
        out_specs=[
            pl.BlockSpec((SEQ, HALF), lambda b: (b, 0)),
            pl.BlockSpec((None, None, N_HEADS, SEQ, HEAD_DIM), lambda b: (b, 0, 0, 0, 0)),
            pl.BlockSpec((None, None, N_HEADS, SEQ, HEAD_DIM), lambda b: (b, 0, 0, 0, 0)),
        ],
        out_shape=[
            jax.ShapeDtypeStruct((N_PROMPT_TOK, HALF), BF16),
            jax.ShapeDtypeStruct((BATCH, 1, N_HEADS, SEQ, HEAD_DIM), F32),
            jax.ShapeDtypeStruct((BATCH, 1, N_HEADS, SEQ, HEAD_DIM), F32),
        ],
        compiler_params=_cparams(("parallel",)),
        name="ctx_attn",
    )(proj, proj, proj)


NA_MASKED = -1e30
NA_QROWS = 4
NA_KROWS = NA_QROWS + WIN_H
NA_NQ = NA_QROWS * GRID_W
NA_NK = NA_KROWS * GRID_W
NA_BLOCKS = GRID_ROWS // NA_QROWS
NA_TYPES = 3


def _na_col_table(rpb):
    qc = jnp.arange(GRID_W)[:, None]
    kc = jnp.arange(GRID_W)[None, :]
    c0 = jnp.clip(qc - WIN_W // 2, 0, GRID_W - WIN_W)
    valid = (kc >= c0) & (kc < c0 + WIN_W)
    coff = jnp.clip(kc - qc, -(WIN_W - 1), WIN_W - 1) + (WIN_W - 1)
    onehot = (coff.reshape(1, -1) == jnp.arange(2 * WIN_W - 1)[:, None]).astype(F32)
    t = jnp.dot(rpb.astype(F32).reshape(N_HEADS * (2 * WIN_H - 1), 2 * WIN_W - 1), onehot,
                precision=lax.Precision.HIGHEST)
    t = t.reshape(N_HEADS, 2 * WIN_H - 1, GRID_W, GRID_W)
    return jnp.where(valid[None, None], t, NA_MASKED)


def _na_window_start(rb):
    return jnp.clip(NA_QROWS * rb - WIN_H // 2, 0, GRID_ROWS - NA_KROWS)


def _na_body(q_ref, k_ref, v_ref, ck_ref, cv_ref, tab_ref, o_ref, bias_ref, kc_ref, vc_ref, va_ref):
    @pl.when(pl.program_id(1) == 0)
    def _():
        masked = jnp.full((GRID_W, GRID_W), NA_MASKED, F32)
        for hh in range(2):
            for ty in range(NA_TYPES):
                q0 = (0, NA_QROWS, GRID_ROWS - NA_QROWS)[ty]
                ks = (0, 0, GRID_ROWS - NA_KROWS)[ty]
                for a in range(NA_QROWS):
                    r0 = min(max(q0 + a - WIN_H // 2, 0), GRID_ROWS - WIN_H)
                    for y in range(NA_KROWS):
                        kr = ks + y
                        ok = r0 <= kr < r0 + WIN_H
                        blk = tab_ref[hh, kr - (q0 + a) + WIN_H - 1] if ok else masked
                        bias_ref[ty, hh * NA_NQ + a * GRID_W:hh * NA_NQ + (a + 1) * GRID_W,
                                 y * GRID_W:(y + 1) * GRID_W] = blk

    lane = lax.broadcasted_iota(jnp.int32, (1, 2 * HEAD_DIM), 1)
    own = [lane < HEAD_DIM, lane >= HEAD_DIM]
    kc_ref[...] = jnp.concatenate([ck_ref[0], ck_ref[1]], axis=1).astype(BF16)
    vc = jnp.concatenate([cv_ref[0], cv_ref[1]], axis=1)
    v = v_ref[...]
    for hh in range(2):
        vc_ref[hh] = jnp.where(own[hh], vc, 1.0).astype(BF16)
        va_ref[hh] = jnp.where(own[hh], v, jnp.ones_like(v))

    def block(rb, ty):
        qs = pl.multiple_of(rb * NA_NQ, NA_NQ)
        ks = pl.multiple_of(_na_window_start(rb) * GRID_W, GRID_W)
        q = q_ref[pl.ds(qs, NA_NQ), :] * jnp.asarray(HEAD_DIM ** -0.5, BF16)
        q2 = jnp.concatenate([jnp.where(own[0], q, jnp.zeros_like(q)),
                              jnp.where(own[1], q, jnp.zeros_like(q))], axis=0)
        s_loc = _dot_nt(q2, k_ref[pl.ds(ks, NA_NK), :]) + bias_ref[ty]
        s_ctx = _dot_nt(q2, kc_ref[...])
        m = jnp.maximum(jnp.max(s_loc, axis=-1, keepdims=True), jnp.max(s_ctx, axis=-1, keepdims=True))
        p_loc = jnp.exp(s_loc - m).astype(BF16)
        p_ctx = jnp.exp(s_ctx - m).astype(BF16)
        out = None
        for hh in range(2):
            rows = slice(hh * NA_NQ, (hh + 1) * NA_NQ)
            num = _dot(p_loc[rows], va_ref[hh, pl.ds(ks, NA_NK), :]) + _dot(p_ctx[rows], vc_ref[hh])
            den = jnp.where(own[hh], pltpu.roll(num, HEAD_DIM, 1), 1.0)
            o = num / den
            out = o if out is None else jnp.where(own[0], out, o)
        o_ref[pl.ds(qs, NA_NQ), :] = out.astype(BF16)

    block(0, 0)

    def interior(rb, carry):
        block(rb, 1)
        return carry

    lax.fori_loop(1, NA_BLOCKS - 1, interior, 0)
    block(NA_BLOCKS - 1, 2)


def _na_attn(proj, cache_k, cache_v, col_table, layer_idx):
    row_blk0 = N_PROMPT_TOK // DEC_SEQ
    hp = N_HEADS // 2
    return pl.pallas_call(
        _na_body,
        grid=(hp, DEC_BATCH),
        in_specs=[
            pl.BlockSpec((DEC_SEQ, 128), lambda h, b: (row_blk0 + b, h)),
            pl.BlockSpec((DEC_SEQ, 128), lambda h, b: (row_blk0 + b, hp + h)),
            pl.BlockSpec((DEC_SEQ, 128), lambda h, b: (row_blk0 + b, 2 * hp + h)),
            pl.BlockSpec((None, None, 2, PAST_LEN, HEAD_DIM), lambda h, b: (b, layer_idx, h, 0, 0)),
            pl.BlockSpec((None, None, 2, PAST_LEN, HEAD_DIM), lambda h, b: (b, layer_idx, h, 0, 0)),
            pl.BlockSpec((2, 2 * WIN_H - 1, GRID_W, GRID_W), lambda h, b: (h, 0, 0, 0)),
        ],
        out_specs=pl.BlockSpec((DEC_SEQ, 128), lambda h, b: (b, h)),
        out_shape=jax.ShapeDtypeStruct((N_SAMPLE_TOK, HALF), BF16),
        scratch_shapes=[pltpu.VMEM((NA_TYPES, 2 * NA_NQ, NA_NK), F32),
                        pltpu.VMEM((PAST_LEN, 2 * HEAD_DIM), BF16),
                        pltpu.VMEM((2, PAST_LEN, 2 * HEAD_DIM), BF16),
                        pltpu.VMEM((2, DEC_SEQ, 2 * HEAD_DIM), BF16)],
        compiler_params=_cparams(("parallel", "arbitrary")),
        name="na_attn",
    )(proj, proj, proj, cache_k, cache_v, col_table)


N_POW = 24


def _s5_prep_body(vec_ref, bt_ref, c_ref, wend_ref, ktoe_ref, woutT_ref, apow_ref, pw_ref, pct_ref):
    L = S5_CHUNK
    p = pl.program_id(0)
    lane_blk = lax.broadcasted_iota(jnp.int32, (S5_CH, S5_W), 1) // S5_CH
    src_blk = (lane_blk + L - p) % L
    krows = []
    for d in range(2):
        v = vec_ref[d]
        lre, lim, step = v[0:1], v[1:2], v[2:3]
        a = lre * step
        b = lim * step
        ea = jnp.exp(a)
        nr = ea * jnp.cos(b) - 1.0
        ni = ea * jnp.sin(b)
        den = lre * lre + lim * lim
        qr = (nr * lre + ni * lim) / den
        qi = (ni * lre - nr * lim) / den
        bre, bim = bt_ref[d, 0], bt_ref[d, 1]
        bqr = bre * qr - bim * qi
        bqi = bre * qi + bim * qr
        cre, cim = c_ref[d, 0], c_ref[d, 1]

        e = lax.broadcasted_iota(jnp.int32, (N_POW, 1), 0).astype(F32)
        mag = jnp.exp(e * a)
        pw_ref[0] = mag * jnp.cos(e * b)
        pw_ref[1] = mag * jnp.sin(e * b)
        mag = jnp.exp((L * e) * a)
        apow_ref[d, 0] = mag * jnp.cos((L * e) * b)
        apow_ref[d, 1] = mag * jnp.sin((L * e) * b)

        def power(k):
            return pw_ref[0, k:k + 1, :], pw_ref[1, k:k + 1, :]

        def c_block(k):
            pr, pi = power(k)
            return cre * pr - cim * pi, -(cre * pi + cim * pr)

        for i in range(L):
            rows = pl.ds(pl.multiple_of(((i + p) % L) * S5_CH, S5_CH), S5_CH)
            pr, pi = power(L - 1 - i if d == 0 else i)
            wend_ref[rows, d * 256:d * 256 + 128] = (bqr * pr - bqi * pi).astype(BF16)
            wend_ref[rows, d * 256 + 128:d * 256 + 256] = (bqr * pi + bqi * pr).astype(BF16)
            mre, mim = c_block(i + 1 if d == 0 else L - i)
            woutT_ref[rows, d * 256:d * 256 + 128] = mre.astype(BF16)
            woutT_ref[rows, d * 256 + 128:d * 256 + 256] = mim.astype(BF16)
            lag_rows = slice(i * S5_CH, (i + 1) * S5_CH)
            mre, mim = c_block(i if d == 0 else L - 1 - i)
            pct_ref[lag_rows, 0:128] = mre
            pct_ref[lag_rows, 128:256] = mim

        bqcat = jnp.concatenate([bqr, bqi], axis=1)
        krows.append(_dot_nt(bqcat, pct_ref[...], precision=lax.Precision.HIGHEST))

    for i in range(L):
        rows = pl.ds(pl.multiple_of(((i + p) % L) * S5_CH, S5_CH), S5_CH)
        blk_f = pltpu.roll(krows[0], ((i + p) % L) * S5_CH, 1)
        blk_b = pltpu.roll(krows[1], ((i + 1 + p) % L) * S5_CH, 1)
        blk = jnp.where(src_blk >= i, blk_f, 0.0) + jnp.where(src_blk <= i, blk_b, 0.0)
        ktoe_ref[rows, :] = blk.astype(BF16)


def _s5_prep(lam_re, lam_im, log_step, b_re, b_im, c_re, c_im):
    def pair_vec(a):
        return a.astype(F32).reshape(2, S5_PAIRS, 1, S5_P).transpose(1, 0, 2, 3)

    step = jnp.broadcast_to(jnp.exp(log_step.astype(F32))[:, :, None], (2, N_SSM_GROUPS, SSM_STATE))
    vec = jnp.concatenate([pair_vec(lam_re), pair_vec(lam_im), pair_vec(step),
                           jnp.zeros((S5_PAIRS, 2, 5, S5_P), F32)], axis=2)

    def block_diag_ch_by_state(m):
        m = m.astype(F32).reshape(2, S5_PAIRS, 2, SSM_GROUP, SSM_STATE)
        eye = jnp.eye(2, dtype=F32)
        out = m[:, :, :, :, None, :] * eye[None, None, :, None, :, None]
        return out.reshape(2, S5_PAIRS, S5_CH, S5_P).transpose(1, 0, 2, 3)

    bt = jnp.stack([block_diag_ch_by_state(b_re.transpose(0, 1, 3, 2)),
                    block_diag_ch_by_state(b_im.transpose(0, 1, 3, 2))], axis=2)
    cc = jnp.stack([block_diag_ch_by_state(c_re), block_diag_ch_by_state(c_im)], axis=2)

    mat = jax.ShapeDtypeStruct((S5_PAIRS, S5_W, S5_W), BF16)
    mat_spec = pl.BlockSpec((None, S5_W, S5_W), lambda p: (p, 0, 0))
    return pl.pallas_call(
        _s5_prep_body,
        grid=(S5_PAIRS,),
        in_specs=[
            pl.BlockSpec((None, 2, 8, S5_P), lambda p: (p, 0, 0, 0)),
            pl.BlockSpec((None, 2, 2, S5_CH, S5_P), lambda p: (p, 0, 0, 0, 0)),
            pl.BlockSpec((None, 2, 2, S5_CH, S5_P), lambda p: (p, 0, 0, 0, 0)),
        ],
        out_specs=[mat_spec, mat_spec, mat_spec,
                   pl.BlockSpec((None, 2, 2, N_POW, S5_P), lambda p: (p, 0, 0, 0, 0))],
        out_shape=[mat, mat, mat, jax.ShapeDtypeStruct((S5_PAIRS, 2, 2, N_POW, S5_P), F32)],
        scratch_shapes=[pltpu.VMEM((2, N_POW, S5_P), F32), pltpu.VMEM((S5_W, 2 * S5_P), F32)],
        compiler_params=_cparams(("parallel",)),
        name="s5_prep",
    )(vec, bt, cc)


S5_SEG_ROWS = 256
S5_N_SEG = N_CHUNK_ROWS // S5_SEG_ROWS
S5_SUB = S5_SEG_ROWS // S5_CHUNK
S5_SUB_PER_SEQ = (DEC_SEQ // S5_CHUNK) // S5_CHUNK


def _s5_body(tu_ref, s0_ref, wend_ref, ktoe_ref, woutT_ref, apow_ref, ty_ref, fin_ref, z_ref, xs_ref, g_ref):
    seg = pl.program_id(0)
    p = pl.program_id(1)
    L = S5_CHUNK
    lane_blk = lax.broadcasted_iota(jnp.int32, (S5_SEG_ROWS, 128), 1) // S5_CH

    tiles = []
    for t in range(LANE_TILES):
        tile = None
        for k in range(128 // S5_CH):
            j = (LANE_TILES * t + k + L - p) % L
            blk = tu_ref[j, t]
            tile = blk if tile is None else jnp.where(lane_blk == k, blk, tile)
        tiles.append(tile)
    s = jnp.concatenate(tiles, axis=1)
    z = _dot(s, wend_ref[...])
    for q in range(LANE_TILES):
        z_ref[q] = z[:, q * 128:(q + 1) * 128]

    ends = []
    for d in range(2):
        are, aim = apow_ref[d, 0, 1:2, :], apow_ref[d, 1, 1:2, :]
        xr = jnp.zeros((S5_SUB, S5_P), F32)
        xi = jnp.zeros((S5_SUB, S5_P), F32)
        for c in (range(L) if d == 0 else reversed(range(L))):
            rows = pl.ds(c, S5_SUB, stride=L)
            xs_ref[2 * d, rows, :] = xr
            xs_ref[2 * d + 1, rows, :] = xi
            zr = z_ref[2 * d, rows, :]
            zi = z_ref[2 * d + 1, rows, :]
            xr, xi = are * xr - aim * xi + zr, are * xi + aim * xr + zi
        ends.append((xr, xi))
        fin_ref[d, 0] = xr
        fin_ref[d, 1] = xi

    @pl.when(seg > 0)
    def _():
        for d in range(2):
            er, ei = ends[d]
            a16r, a16i = apow_ref[d, 0, L:L + 1, :], apow_ref[d, 1, L:L + 1, :]
            for bb in range(S5_SUB // S5_SUB_PER_SEQ):
                gr = s0_ref[bb, 2 * d:2 * d + 1, :]
                gi = s0_ref[bb, 2 * d + 1:2 * d + 2, :]
                order = range(S5_SUB_PER_SEQ) if d == 0 else reversed(range(S5_SUB_PER_SEQ))
                prev = None
                for sub in order:
                    row = bb * S5_SUB_PER_SEQ + sub
                    if prev is not None:
                        gr, gi = (a16r * gr - a16i * gi + er[prev:prev + 1, :],
                                  a16r * gi + a16i * gr + ei[prev:prev + 1, :])
                    g_ref[0, row:row + 1, :] = gr
                    g_ref[1, row:row + 1, :] = gi
                    prev = row
            gr, gi = g_ref[0], g_ref[1]
            for c in range(L):
                k = c if d == 0 else L - 1 - c
                pr, pi = apow_ref[d, 0, k:k + 1, :], apow_ref[d, 1, k:k + 1, :]
                rows = pl.ds(c, S5_SUB, stride=L)
                xs_ref[2 * d, rows, :] = xs_ref[2 * d, rows, :] + (pr * gr - pi * gi)
                xs_ref[2 * d + 1, rows, :] = xs_ref[2 * d + 1, rows, :] + (pr * gi + pi * gr)

    xs = jnp.concatenate([xs_ref[q].astype(BF16) for q in range(LANE_TILES)], axis=1)
    y = _dot(s, ktoe_ref[...]) + _dot_nt(xs, woutT_ref[...])

    @pl.when(p == 0)
    def _():
        ty_ref[...] = jnp.zeros_like(ty_ref)

    for t in range(LANE_TILES):
        yt = y[:, t * 128:(t + 1) * 128]
        for k in range(128 // S5_CH):
            j = (LANE_TILES * t + k + L - p) % L
            ty_ref[j, t] = jnp.where(lane_blk == k, yt, ty_ref[j, t])


def _s5_scan(tu, s0, wend, ktoe, woutT, apow):
    mat_spec = pl.BlockSpec((None, S5_W, S5_W), lambda g, p: (p, 0, 0))
    seqs = S5_SUB // S5_SUB_PER_SEQ
    return pl.pallas_call(
        _s5_body,
        grid=(S5_N_SEG, S5_PAIRS),
        in_specs=[
            pl.BlockSpec((S5_CHUNK, LANE_TILES, S5_SEG_ROWS, 128), lambda g, p: (0, 0, g, 0)),
            pl.BlockSpec((None, seqs, 4, S5_P), lambda g, p: (p, jnp.maximum(g - 1, 0), 0, 0)),
            mat_spec, mat_spec, mat_spec,
            pl.BlockSpec((None, 2, 2, N_POW, S5_P), lambda g, p: (p, 0, 0, 0, 0)),
        ],
        out_specs=[
            pl.BlockSpec((S5_CHUNK, LANE_TILES, S5_SEG_ROWS, 128), lambda g, p: (0, 0, g, 0)),
            pl.BlockSpec((None, 2, 2, S5_SUB, S5_P), lambda g, p: (p, 0, 0, g, 0)),
        ],
        out_shape=[
            jax.ShapeDtypeStruct((S5_CHUNK, LANE_TILES, N_CHUNK_ROWS, 128), F32),
            jax.ShapeDtypeStruct((S5_PAIRS, 2, 2, S5_N_SEG * S5_SUB, S5_P), F32),
        ],
        scratch_shapes=[pltpu.VMEM((LANE_TILES, S5_SEG_ROWS, 128), F32),
                        pltpu.VMEM((LANE_TILES, S5_SEG_ROWS, 128), F32),
                        pltpu.VMEM((2, S5_SUB, S5_P), F32)],
        compiler_params=_cparams(("parallel", "arbitrary")),
        name="s5_scan",
    )(tu, s0, wend, ktoe, woutT, apow)


TM_GLU = 1024


def _glu_body(tu_ref, ty_ref, d_ref, gw_ref, gb_ref, o_ref, u_ref, ys_ref):
    rows = TM_GLU // S5_CHUNK
    for j in range(S5_CHUNK):
        back = ((S5_CHUNK - j) % S5_CHUNK) * S5_CH
        uj = jnp.concatenate([tu_ref[j, t].astype(F32) for t in range(LANE_TILES)], axis=1)
        yj = jnp.concatenate([ty_ref[j, t] for t in range(LANE_TILES)], axis=1)
        if back:
            uj = pltpu.roll(uj, back, 1)
            yj = pltpu.roll(yj, back, 1)
        for t in range(LANE_TILES):
            u_ref[t, pl.ds(j, rows, stride=S5_CHUNK), :] = uj[:, t * 128:(t + 1) * 128]
            ys_ref[t, pl.ds(j, rows, stride=S5_CHUNK), :] = yj[:, t * 128:(t + 1) * 128]
    u = jnp.concatenate([u_ref[t] for t in range(LANE_TILES)], axis=1)
    y = d_ref[...] * u + jnp.concatenate([ys_ref[t] for t in range(LANE_TILES)], axis=1)
    cdf = 0.5 * (1.0 + jnp.tanh(math.sqrt(2.0 / math.pi) * (y + 0.044715 * (y * y * y))))
    y = y * cdf
    gate = jax.nn.sigmoid(_dot(y.astype(BF16), gw_ref[...]) + gb_ref[...])
    o_ref[...] = (y * gate).astype(BF16)


def _glu(tu, ty, ssm_d, glu_w, glu_b):
    tm = TM_GLU
    rows = tm // S5_CHUNK
    lay_spec = pl.BlockSpec((S5_CHUNK, LANE_TILES, rows, 128), lambda i: (0, 0, i, 0))
    return pl.pallas_call(
        _glu_body,
        grid=(N_TOK // tm,),
        in_specs=[
            lay_spec, lay_spec,
            pl.BlockSpec((1, HALF), lambda i: (0, 0)),
            pl.BlockSpec((HALF, HALF), lambda i: (0, 0)),
            pl.BlockSpec((1, HALF), lambda i: (0, 0)),
        ],
        out_specs=pl.BlockSpec((tm, HALF), lambda i: (i, 0)),
        out_shape=jax.ShapeDtypeStruct((N_TOK, HALF), BF16),
        scratch_shapes=[pltpu.VMEM((LANE_TILES, tm, 128), F32), pltpu.VMEM((LANE_TILES, tm, 128), F32)],
        compiler_params=_cparams(("parallel",)),
        name="s5_glu",
    )(tu, ty, ssm_d.reshape(1, HALF), glu_w.astype(BF16), glu_b.reshape(1, HALF))


def kernel(x_prompt, x_sample, c, cache_k, cache_v, state_s5_re, state_s5_im, c_ctx, norm1_g, norm2_g, ada_w, ada_b, mlp_w1, mlp_w2, ab_w_in, pool_w, pool_scale, conv_w, ab_w_out, cd_w_in, na_rpb, ssm_lambda_re, ssm_lambda_im, ssm_log_step, ssm_b_re, ssm_b_im, ssm_c_re, ssm_c_im, ssm_d, glu_w, glu_b, cd_w_out, final_g):
    depth = ada_w.shape[0]
    xs = (x_prompt.reshape(N_PROMPT_TOK, D_MODEL), x_sample.reshape(N_SAMPLE_TOK, D_MODEL))
    cond = jnp.concatenate([c_ctx[None, :], c, jnp.zeros((N_COND - 1 - DEC_BATCH, D_MODEL), F32)], axis=0)
    mods = _adaln(cond, ada_w, ada_b).reshape(depth, N_COND, 6, D_MODEL)

    new_k, new_v, new_re, new_im = [], [], [], []
    for layer in range(depth):
        i = layer // 2
        final = layer == depth - 1
        if layer % 2 == 0:
            y = _even_layer_mix(xs, mods, layer, norm1_g[layer], ab_w_in[i].astype(BF16),
                                pool_w[i], pool_scale[i], conv_w[i])
            ya, ya_col, yb, yb_col = y, 0, y, 1
            w_out = ab_w_out[i]
        else:
            qkv, tu = _inproj_odd(xs[0], mods, layer, norm1_g[layer], cd_w_in[i].astype(BF16))
            o_p, ck, cv = _ctx_attn(qkv)
            new_k.append(ck)
            new_v.append(cv)
            o_s = _na_attn(qkv, cache_k, cache_v, _na_col_table(na_rpb[i]), i)
            attn = (o_p, o_s)

            wend, ktoe, woutT, apow = _s5_prep(ssm_lambda_re[i], ssm_lambda_im[i], ssm_log_step[i],
                                               ssm_b_re[i], ssm_b_im[i], ssm_c_re[i], ssm_c_im[i])

            def pair_state(s):
                return s.astype(F32).reshape(DEC_BATCH, 2, S5_PAIRS, S5_P).transpose(2, 0, 1, 3)

            s0 = jnp.stack([pair_state(state_s5_re[:, i]), pair_state(state_s5_im[:, i])], axis=3)
            s0 = s0.reshape(S5_PAIRS, DEC_BATCH, 4, S5_P)
            ty, fin = _s5_scan(tu, s0, wend, ktoe, woutT, apow)
            fin = fin[:, :, :, :BATCH]
            new_re.append(fin[:, :, 0].transpose(2, 1, 0, 3).reshape(BATCH, 2, N_SSM_GROUPS, SSM_STATE))
            new_im.append(fin[:, :, 1].transpose(2, 1, 0, 3).reshape(BATCH, 2, N_SSM_GROUPS, SSM_STATE))
            d_out = _glu(tu, ty, ssm_d[i], glu_w[i], glu_b[i])
            ya, ya_col, yb, yb_col = attn, 0, d_out, 0
            w_out = cd_w_out[i]
        xs = _mlp(xs, ya, ya_col, yb, yb_col, w_out.astype(BF16), mods, layer, norm2_g[layer], final_g,
                  mlp_w1[layer].astype(BF16), mlp_w2[layer].astype(BF16), final)

    y_prompt = xs[0].reshape(BATCH, SEQ, D_MODEL)
    y_sample = xs[1].reshape(DEC_BATCH, DEC_SEQ, D_MODEL)
    return (y_prompt, y_sample, jnp.concatenate(new_k, axis=1), jnp.concatenate(new_v, axis=1),
            jnp.stack(new_re, axis=1), jnp.stack(new_im, axis=1))
```

```python
import functools
import math

import jax
import jax.numpy as jnp
from jax import lax
from jax.experimental import pallas as pl
from jax.experimental.pallas import tpu as pltpu

F32 = jnp.float32
BF16 = jnp.bfloat16

D_MODEL = 1024
BATCH = 16
SEQ = 256
DEC_BATCH = 4
DEC_SEQ = 2048
PAST_LEN = 256
GRID_W = 64
GRID_ROWS = DEC_SEQ // GRID_W
HALF = 512
POOL_WINDOWS = (2, 4, 8, 16)
POOL_GROUP = 128
HEAD_DIM = 64
N_HEADS = 8
WIN_H = 8
WIN_W = 16
SSM_GROUP = 16
N_SSM_GROUPS = 32
SSM_STATE = 64
D_FF = 4096
EPS = 1e-6

N_PROMPT_TOK = BATCH * SEQ
N_SAMPLE_TOK = DEC_BATCH * DEC_SEQ
N_TOK = N_PROMPT_TOK + N_SAMPLE_TOK
N_COND = 8

S5_CHUNK = 16
S5_PAIRS = N_SSM_GROUPS // 2
S5_CH = 2 * SSM_GROUP
S5_P = 2 * SSM_STATE
S5_W = S5_CHUNK * S5_CH
S5_BP_SAMPLE = 8

VMEM_LIMIT = 56 * 1024 * 1024


def _cparams(sem):
    return pltpu.CompilerParams(dimension_semantics=sem, vmem_limit_bytes=VMEM_LIMIT)


def _cond_row(i, tm):
    npt = N_PROMPT_TOK // tm
    per = DEC_SEQ // tm
    return jnp.where(i < npt, 0, 1 + (i - npt) // per)


def _normmod(x, g, shift, scale):
    ms = jnp.mean(x * x, axis=-1, keepdims=True)
    return (x * lax.rsqrt(ms + EPS) * g) * (1.0 + scale) + shift


def _dot(a, b):
    return jnp.dot(a, b, preferred_element_type=F32)


def _dot_nt(a, b, precision=None):
    return lax.dot_general(a, b, (((1,), (1,)), ((), ())), preferred_element_type=F32, precision=precision)


def _adaln_body(c_ref, w_ref, b_ref, o_ref):
    c = c_ref[...]
    s = c * jax.nn.sigmoid(c)
    o_ref[...] = _dot(s.astype(BF16), w_ref[...].astype(BF16)) + b_ref[...]


def _adaln(cond, ada_w, ada_b):
    depth = ada_w.shape[0]
    nj = 6
    return pl.pallas_call(
        _adaln_body,
        grid=(depth, nj),
        in_specs=[
            pl.BlockSpec((N_COND, D_MODEL), lambda l, j: (0, 0)),
            pl.BlockSpec((None, D_MODEL, D_MODEL), lambda l, j: (l, 0, j)),
            pl.BlockSpec((None, 1, D_MODEL), lambda l, j: (l, 0, j)),
        ],
        out_specs=pl.BlockSpec((None, N_COND, D_MODEL), lambda l, j: (l, 0, j)),
        out_shape=jax.ShapeDtypeStruct((depth, N_COND, 6 * D_MODEL), F32),
        compiler_params=_cparams(("arbitrary", "arbitrary")),
        name="adaln",
    )(cond, ada_w, ada_b.reshape(depth, 1, 6 * D_MODEL))


TM_PROJ = 512


def _stream_tile(x_refs, tm):
    if len(x_refs) == 1:
        return x_refs[0][...]
    is_prompt = pl.program_id(0) < N_PROMPT_TOK // tm
    return jnp.where(is_prompt, x_refs[0][...], x_refs[1][...])


N_CHUNK_ROWS = N_TOK // S5_CHUNK
LANE_TILES = S5_W // 128


def _inproj_odd_body(x_ref, m_ref, g_ref, w_ref, qkv_ref, tu_ref, u_ref):
    m = m_ref[...]
    h = _normmod(x_ref[...], g_ref[...], m[0:1], m[1:2])
    acc = _dot(h.astype(BF16), w_ref[...])
    qkv_ref[...] = acc[:, 0:3 * HALF].astype(BF16)
    for t in range(LANE_TILES):
        u_ref[t] = acc[:, 3 * HALF + t * 128:3 * HALF + (t + 1) * 128]
    rows = TM_PROJ // S5_CHUNK
    for j in range(S5_CHUNK):
        uj = jnp.concatenate([u_ref[t, pl.ds(j, rows, stride=S5_CHUNK), :] for t in range(LANE_TILES)], axis=1)
        if j:
            uj = pltpu.roll(uj, j * S5_CH, 1)
        for t in range(LANE_TILES):
            tu_ref[j, t] = uj[:, t * 128:(t + 1) * 128].astype(BF16)


def _inproj_odd(x, mods, layer, g, w):
    tm = TM_PROJ
    rows = tm // S5_CHUNK
    return pl.pallas_call(
        _inproj_odd_body,
        grid=(N_TOK // tm,),
        in_specs=[
            pl.BlockSpec((tm, D_MODEL), lambda i: (i, 0)),
            pl.BlockSpec((None, None, 6, D_MODEL), lambda i: (layer, _cond_row(i, tm), 0, 0)),
            pl.BlockSpec((1, D_MODEL), lambda i: (0, 0)),
            pl.BlockSpec((D_MODEL, 4 * HALF), lambda i: (0, 0)),
        ],
        out_specs=[
            pl.BlockSpec((tm, 3 * HALF), lambda i: (i, 0)),
            pl.BlockSpec((S5_CHUNK, LANE_TILES, rows, 128), lambda i: (0, 0, i, 0)),
        ],
        out_shape=[
            jax.ShapeDtypeStruct((N_TOK, 3 * HALF), BF16),
            jax.ShapeDtypeStruct((S5_CHUNK, LANE_TILES, N_CHUNK_ROWS, 128), BF16),
        ],
        scratch_shapes=[pltpu.VMEM((LANE_TILES, tm, 128), F32)],
        compiler_params=_cparams(("parallel",)),
        name="inproj_odd",
    )(x, mods, g.reshape(1, D_MODEL), w)


TM_MIX = 256
HALO = 16
RING = 4
LAG = 2


def _even_body(n_x, *refs):
    x_refs = refs[:n_x]
    m_ref, g_ref, w_ref, pw_ref, ps_ref, cw_ref, y_ref, xcat_ref, proj_ref, pad_ref, h_ref, stage_ref = refs[n_x:]
    i = pl.program_id(0)

    @pl.when(i == 0)
    def _():
        proj_ref[...] = jnp.zeros_like(proj_ref)

    t = jnp.maximum(i - LAG, 0)
    prev_slot, cur_slot, next_slot = (i + 1) % RING, (i + 2) % RING, (i + 3) % RING
    n_prompt_tiles = N_PROMPT_TOK // TM_MIX
    tiles_per_seq = DEC_SEQ // TM_MIX
    is_prompt = t < n_prompt_tiles
    tile_in_seq = jnp.where(is_prompt, 0, (t - n_prompt_tiles) % tiles_per_seq)
    seq_len = jnp.where(is_prompt, SEQ, DEC_SEQ)
    has_prev = tile_in_seq > 0
    has_next = jnp.logical_and(jnp.logical_not(is_prompt), tile_in_seq < tiles_per_seq - 1)
    prev_keep = jnp.where(has_prev, 1.0, 0.0).astype(F32)
    next_keep = jnp.where(has_next, 1.0, 0.0).astype(F32)
    pos = tile_in_seq * TM_MIX + lax.broadcasted_iota(jnp.int32, (TM_MIX, 1), 0)
    lo_rows, hi_rows = HALO, HALO + TM_MIX
    tail = slice(TM_MIX - HALO, TM_MIX)
    head = slice(0, HALO)

    m = m_ref[...]
    x = _stream_tile(x_refs, TM_MIX)
    xcat_ref[...] = x
    h_ref[...] = _normmod(x, g_ref[...], m[0:1], m[1:2]).astype(BF16)
    n_pieces = len(POOL_WINDOWS) + HALF // 128
    pw = (4 * HALF) // n_pieces

    def project(piece):
        cols = slice(piece * pw, (piece + 1) * pw)
        stage_ref[:, cols] = _dot(h_ref[...], w_ref[:, cols])

    for gi, w in enumerate(POOL_WINDOWS):
        project(gi)
        cs = slice(gi * POOL_GROUP, (gi + 1) * POOL_GROUP)
        pad_ref[0:HALO, cs] = proj_ref[prev_slot, tail, cs] * prev_keep
        pad_ref[lo_rows:hi_rows, cs] = proj_ref[cur_slot, :, cs]
        pad_ref[hi_rows:hi_rows + HALO, cs] = proj_ref[next_slot, head, cs] * next_keep
        s = None
        for o in range(-(w // 2), w - w // 2):
            v = pad_ref[lo_rows + o:hi_rows + o, cs]
            s = v if s is None else s + v
        lo = jnp.maximum(pos - w // 2, 0)
        hi = jnp.minimum(pos + (w - w // 2), seq_len)
        inv = 1.0 / (hi - lo).astype(F32)
        pooled = s * inv - pad_ref[lo_rows:hi_rows, cs]
        mixed = _dot(pooled.astype(BF16), pw_ref[gi]) * ps_ref[:, cs]
        y_ref[:, cs] = mixed.astype(BF16)

    for ci in range(HALF // 128):
        project(len(POOL_WINDOWS) + ci)
        cs = slice(ci * 128, (ci + 1) * 128)
        bs = slice(HALF + ci * 128, HALF + (ci + 1) * 128)
        gs = slice(2 * HALF + ci * 128, 2 * HALF + (ci + 1) * 128)
        vs = slice(3 * HALF + ci * 128, 3 * HALF + (ci + 1) * 128)
        pad_ref[0:HALO, cs] = (proj_ref[prev_slot, tail, gs] * proj_ref[prev_slot, tail, vs]) * prev_keep
        pad_ref[lo_rows:hi_rows, cs] = proj_ref[cur_slot, :, gs] * proj_ref[cur_slot, :, vs]
        pad_ref[hi_rows:hi_rows + HALO, cs] = (proj_ref[next_slot, head, gs] * proj_ref[next_slot, head, vs]) * next_keep
        conv = (cw_ref[0:1, cs] * pad_ref[lo_rows - 1:hi_rows - 1, cs]
                + cw_ref[1:2, cs] * pad_ref[lo_rows:hi_rows, cs]
                + cw_ref[2:3, cs] * pad_ref[lo_rows + 1:hi_rows + 1, cs])
        y_ref[:, bs] = (proj_ref[cur_slot, :, bs] * conv).astype(BF16)

    proj_ref[i % RING] = stage_ref[...]


def _even_layer_mix(xs, mods, layer, g, w, pool_w, pool_scale, conv_w):
    tm = TM_MIX
    n_tiles = N_TOK // tm
    npt = N_PROMPT_TOK // tm
    last = n_tiles - 1
    if len(xs) == 1:
        x_specs = [pl.BlockSpec((tm, D_MODEL), lambda i: (jnp.minimum(i, last), 0))]
    else:
        x_specs = [pl.BlockSpec((tm, D_MODEL), lambda i: (jnp.minimum(i, npt - 1), 0)),
                   pl.BlockSpec((tm, D_MODEL), lambda i: (jnp.clip(i - npt, 0, last - npt), 0))]
    return pl.pallas_call(
        functools.partial(_even_body, len(xs)),
        grid=(n_tiles + LAG,),
        in_specs=x_specs + [
            pl.BlockSpec((None, None, 6, D_MODEL), lambda i: (layer, _cond_row(jnp.minimum(i, last), tm), 0, 0)),
            pl.BlockSpec((1, D_MODEL), lambda i: (0, 0)),
            pl.BlockSpec((D_MODEL, 4 * HALF), lambda i: (0, 0)),
            pl.BlockSpec((len(POOL_WINDOWS), POOL_GROUP, POOL_GROUP), lambda i: (0, 0, 0)),
            pl.BlockSpec((1, HALF), lambda i: (0, 0)),
            pl.BlockSpec((3, HALF), lambda i: (0, 0)),
        ],
        out_specs=[pl.BlockSpec((tm, 2 * HALF), lambda i: (jnp.maximum(i - LAG, 0), 0)),
                   pl.BlockSpec((tm, D_MODEL), lambda i: (jnp.minimum(i, last), 0))],
        out_shape=[jax.ShapeDtypeStruct((N_TOK, 2 * HALF), BF16), jax.ShapeDtypeStruct((N_TOK, D_MODEL), F32)],
        scratch_shapes=[pltpu.VMEM((RING, tm, 4 * HALF), F32), pltpu.VMEM((tm + 2 * HALO, HALF), F32),
                        pltpu.VMEM((tm, D_MODEL), BF16), pltpu.VMEM((tm, 4 * HALF), F32)],
        compiler_params=_cparams(("arbitrary",)),
        name="even_layer_mix",
    )(*xs, mods, g.reshape(1, D_MODEL), w, pool_w.astype(BF16), pool_scale.reshape(1, HALF), conv_w)


TM_MLP = 1024
RC_MLP = 256


def _mlp_body(final, x_ref, ya_ref, yb_ref, wo_ref, m_ref, g_ref, fg_ref, w1_ref, w2_ref, o_ref):
    m = m_ref[...]
    gain = g_ref[...] * (1.0 + m[4:5])
    for r in range(TM_MLP // RC_MLP):
        rows = slice(r * RC_MLP, (r + 1) * RC_MLP)
        mix = _dot(ya_ref[rows, :], wo_ref[0:HALF, :]) + _dot(yb_ref[rows, :], wo_ref[HALF:2 * HALF, :])
        x1 = x_ref[rows, :] + m[2:3] * mix
        ms = jnp.mean(x1 * x1, axis=-1, keepdims=True)
        h = (x1 * lax.rsqrt(ms + EPS) * gain + m[3:4]).astype(BF16)
        t = _dot(h, w1_ref[...])
        t = jnp.square(jnp.maximum(t, 0.0)).astype(BF16)
        out = x1 + m[5:6] * _dot(t, w2_ref[...])
        if final:
            ms = jnp.mean(out * out, axis=-1, keepdims=True)
            out = out * lax.rsqrt(ms + EPS) * fg_ref[...]
        o_ref[rows, :] = out


def _resident(shape):
    zeros = (0,) * len(shape)
    return pl.BlockSpec(shape, lambda i: zeros, pipeline_mode=pl.Buffered(1))


def _mlp_call(x, x_off, ya, ya_off, ya_col, yb, yb_off, yb_col, n_tiles, tile0, w_out, mods, layer, g2, final_g,
              w1, w2, final):
    tm = TM_MLP
    return pl.pallas_call(
        functools.partial(_mlp_body, final),
        grid=(n_tiles,),
        in_specs=[
            pl.BlockSpec((tm, D_MODEL), lambda i: (i + x_off, 0)),
            pl.BlockSpec((tm, HALF), lambda i: (i + ya_off, ya_col)),
            pl.BlockSpec((tm, HALF), lambda i: (i + yb_off, yb_col)),
            _resident((D_MODEL, D_MODEL)),
            pl.BlockSpec((None, None, 6, D_MODEL), lambda i: (layer, _cond_row(i + tile0, tm), 0, 0)),
            _resident((1, D_MODEL)),
            _resident((1, D_MODEL)),
            _resident((D_MODEL, D_FF)),
            _resident((D_FF, D_MODEL)),
        ],
        out_specs=pl.BlockSpec((tm, D_MODEL), lambda i: (i, 0)),
        out_shape=jax.ShapeDtypeStruct((n_tiles * tm, D_MODEL), F32),
        compiler_params=_cparams(("parallel",)),
        name="mlp",
    )(x, ya, yb, w_out, mods, g2.reshape(1, D_MODEL), final_g.reshape(1, D_MODEL), w1, w2)


def _mlp(x, ya, ya_col, yb, yb_col, w_out, mods, layer, g2, final_g, w1, w2, final):
    tm = TM_MLP
    npt = N_PROMPT_TOK // tm
    weights = (w_out, mods, layer, g2, final_g, w1, w2, final)
    if not final:
        if isinstance(ya, tuple):
            ya = jnp.concatenate(ya, axis=0)
        return (_mlp_call(x, 0, ya, 0, ya_col, yb, 0, yb_col, N_TOK // tm, 0, *weights),)
    outs = []
    for s, (tile0, n_tiles) in enumerate(((0, npt), (npt, N_TOK // tm - npt))):
        a, a_off = (ya[s], 0) if isinstance(ya, tuple) else (ya, tile0)
        outs.append(_mlp_call(x, tile0, a, a_off, ya_col, yb, tile0, yb_col, n_tiles, tile0, *weights))
    return tuple(outs)


def _ctx_attn_body(q_ref, k_ref, v_ref, o_ref, ck_ref, cv_ref):
    lane = lax.broadcasted_iota(jnp.int32, (1, 2 * HEAD_DIM), 1)
    own = [lane < HEAD_DIM, lane >= HEAD_DIM]
    for h in range(N_HEADS):
        sl = slice(h * HEAD_DIM, (h + 1) * HEAD_DIM)
        ck_ref[h] = k_ref[:, sl].astype(F32)
        cv_ref[h] = v_ref[:, sl].astype(F32)
    for hp in range(N_HEADS // 2):
        sl = slice(hp * 2 * HEAD_DIM, (hp + 1) * 2 * HEAD_DIM)
        q = q_ref[:, sl] * jnp.asarray(HEAD_DIM ** -0.5, BF16)
        k = k_ref[:, sl]
        v = v_ref[:, sl]
        q2 = jnp.concatenate([jnp.where(own[0], q, jnp.zeros_like(q)),
                              jnp.where(own[1], q, jnp.zeros_like(q))], axis=0)
        s = _dot_nt(q2, k)
        m = jnp.max(s, axis=-1, keepdims=True)
        p = jnp.exp(s - m).astype(BF16)
        out = None
        for hh in range(2):
            num = _dot(p[hh * SEQ:(hh + 1) * SEQ], jnp.where(own[hh], v, jnp.ones_like(v)))
            den = jnp.where(own[hh], pltpu.roll(num, HEAD_DIM, 1), 1.0)
            o = num / den
            out = o if out is None else jnp.where(own[0], out, o)
        o_ref[:, sl] = out.astype(BF16)


def _ctx_attn(proj):
    return pl.pallas_call(
        _ctx_attn_body,
        grid=(BATCH,),
        in_specs=[
            pl.BlockSpec((SEQ, HALF), lambda b: (b, 0)),
            pl.BlockSpec((SEQ, HALF), lambda b: (b, 1)),
            pl.BlockSpec((SEQ, HALF), lambda b: (b, 2)),
        ],
        out_specs=[
            pl.BlockSpec((SEQ, HALF), lambda b: (b, 0)),
            pl.BlockSpec((None, None, N_HEADS, SEQ, HEAD_DIM), lambda b: (b, 0, 0, 0, 0)),
            pl.BlockSpec((None, None, N_HEADS, SEQ, HEAD_DIM), lambda b: (b, 0, 0, 0, 0)),
        ],
        out_shape=[
            jax.ShapeDtypeStruct((N_PROMPT_TOK, HALF), BF16),
            jax.ShapeDtypeStruct((BATCH, 1, N_HEADS, SEQ, HEAD_DIM), F32),
            jax.ShapeDtypeStruct((BATCH, 1, N_HEADS, SEQ, HEAD_DIM), F32),
        ],
        compiler_params=_cparams(("parallel",)),
        name="ctx_attn",
    )(proj, proj, proj)


NA_MASKED = -1e30
NA_QROWS = 4
NA_KROWS = NA_QROWS + WIN_H
NA_NQ = NA_QROWS * GRID_W
NA_NK = NA_KROWS * GRID_W
NA_BLOCKS = GRID_ROWS // NA_QROWS
NA_TYPES = 3


def _na_col_table(rpb):
    qc = jnp.arange(GRID_W)[:, None]
    kc = jnp.arange(GRID_W)[None, :]
    c0 = jnp.clip(qc - WIN_W // 2, 0, GRID_W - WIN_W)
    valid = (kc >= c0) & (kc < c0 + WIN_W)
    coff = jnp.clip(kc - qc, -(WIN_W - 1), WIN_W - 1) + (WIN_W - 1)
    onehot = (coff.reshape(1, -1) == jnp.arange(2 * WIN_W - 1)[:, None]).astype(F32)
    t = jnp.dot(rpb.astype(F32).reshape(N_HEADS * (2 * WIN_H - 1), 2 * WIN_W - 1), onehot,
                precision=lax.Precision.HIGHEST)
    t = t.reshape(N_HEADS, 2 * WIN_H - 1, GRID_W, GRID_W)
    return jnp.where(valid[None, None], t, NA_MASKED)


def _na_window_start(rb):
    return jnp.clip(NA_QROWS * rb - WIN_H // 2, 0, GRID_ROWS - NA_KROWS)


def _na_body(q_ref, k_ref, v_ref, ck_ref, cv_ref, tab_ref, o_ref, bias_ref, kc_ref, vc_ref, va_ref):
    @pl.when(pl.program_id(1) == 0)
    def _():
        masked = jnp.full((GRID_W, GRID_W), NA_MASKED, F32)
        for hh in range(2):
            for ty in range(NA_TYPES):
                q0 = (0, NA_QROWS, GRID_ROWS - NA_QROWS)[ty]
                ks = (0, 0, GRID_ROWS - NA_KROWS)[ty]
                for a in range(NA_QROWS):
                    r0 = min(max(q0 + a - WIN_H // 2, 0), GRID_ROWS - WIN_H)
                    for y in range(NA_KROWS):
                        kr = ks + y
                        ok = r0 <= kr < r0 + WIN_H
                        blk = tab_ref[hh, kr - (q0 + a) + WIN_H - 1] if ok else masked
                        bias_ref[ty, hh * NA_NQ + a * GRID_W:hh * NA_NQ + (a + 1) * GRID_W,
                                 y * GRID_W:(y + 1) * GRID_W] = blk

    lane = lax.broadcasted_iota(jnp.int32, (1, 2 * HEAD_DIM), 1)
    own = [lane < HEAD_DIM, lane >= HEAD_DIM]
    kc_ref[...] = jnp.concatenate([ck_ref[0], ck_ref[1]], axis=1).astype(BF16)
    vc = jnp.concatenate([cv_ref[0], cv_ref[1]], axis=1)
    v = v_ref[...]
    for hh in range(2):
        vc_ref[hh] = jnp.where(own[hh], vc, 1.0).astype(BF16)
        va_ref[hh] = jnp.where(own[hh], v, jnp.ones_like(v))

    def block(rb, ty):
        qs = pl.multiple_of(rb * NA_NQ, NA_NQ)
        ks = pl.multiple_of(_na_window_start(rb) * GRID_W, GRID_W)
        q = q_ref[pl.ds(qs, NA_NQ), :] * jnp.asarray(HEAD_DIM ** -0.5, BF16)
        q2 = jnp.concatenate([jnp.where(own[0], q, jnp.zeros_like(q)),
                              jnp.where(own[1], q, jnp.zeros_like(q))], axis=0)
        s_loc = _dot_nt(q2, k_ref[pl.ds(ks, NA_NK), :]) + bias_ref[ty]
        s_ctx = _dot_nt(q2, kc_ref[...])
        m = jnp.maximum(jnp.max(s_loc, axis=-1, keepdims=True), jnp.max(s_ctx, axis=-1, keepdims=True))
        p_loc = jnp.exp(s_loc - m).astype(BF16)
        p_ctx = jnp.exp(s_ctx - m).astype(BF16)
        out = None
        for hh in range(2):
            rows = slice(hh * NA_NQ, (hh + 1) * NA_NQ)
            num = _dot(p_loc[rows], va_ref[hh, pl.ds(ks, NA_NK), :]) + _dot(p_ctx[rows], vc_ref[hh])
            den = jnp.where(own[hh], pltpu.roll(num, HEAD_DIM, 1), 1.0)
            o = num / den
            out = o if out is None else jnp.where(own[0], out, o)
        o_ref[pl.ds(qs, NA_NQ), :] = out.astype(BF16)

    block(0, 0)

    def interior(rb, carry):
        block(rb, 1)
        return carry

    lax.fori_loop(1, NA_BLOCKS - 1, interior, 0)
    block(NA_BLOCKS - 1, 2)


def _na_attn(proj, cache_k, cache_v, col_table, layer_idx):
    row_blk0 = N_PROMPT_TOK // DEC_SEQ
    hp = N_HEADS // 2
    return pl.pallas_call(
        _na_body,
        grid=(hp, DEC_BATCH),
        in_specs=[
            pl.BlockSpec((DEC_SEQ, 128), lambda h, b: (row_blk0 + b, h)),
            pl.BlockSpec((DEC_SEQ, 128), lambda h, b: (row_blk0 + b, hp + h)),
            pl.BlockSpec((DEC_SEQ, 128), lambda h, b: (row_blk0 + b, 2 * hp + h)),
            pl.BlockSpec((None, None, 2, PAST_LEN, HEAD_DIM), lambda h, b: (b, layer_idx, h, 0, 0)),
            pl.BlockSpec((None, None, 2, PAST_LEN, HEAD_DIM), lambda h, b: (b, layer_idx, h, 0, 0)),
            pl.BlockSpec((2, 2 * WIN_H - 1, GRID_W, GRID_W), lambda h, b: (h, 0, 0, 0)),
        ],
        out_specs=pl.BlockSpec((DEC_SEQ, 128), lambda h, b: (b, h)),
        out_shape=jax.ShapeDtypeStruct((N_SAMPLE_TOK, HALF), BF16),
        scratch_shapes=[pltpu.VMEM((NA_TYPES, 2 * NA_NQ, NA_NK), F32),
                        pltpu.VMEM((PAST_LEN, 2 * HEAD_DIM), BF16),
                        pltpu.VMEM((2, PAST_LEN, 2 * HEAD_DIM), BF16),
                        pltpu.VMEM((2, DEC_SEQ, 2 * HEAD_DIM), BF16)],
        compiler_params=_cparams(("parallel", "arbitrary")),
        name="na_attn",
    )(proj, proj, proj, cache_k, cache_v, col_table)


N_POW = 24


def _s5_prep_body(vec_ref, bt_ref, c_ref, wend_ref, ktoe_ref, woutT_ref, apow_ref, pw_ref, pct_ref):
    L = S5_CHUNK
    p = pl.program_id(0)
    lane_blk = lax.broadcasted_iota(jnp.int32, (S5_CH, S5_W), 1) // S5_CH
    src_blk = (lane_blk + L - p) % L
    krows = []
    for d in range(2):
        v = vec_ref[d]
        lre, lim, step = v[0:1], v[1:2], v[2:3]
        a = lre * step
        b = lim * step
        ea = jnp.exp(a)
        nr = ea * jnp.cos(b) - 1.0
        ni = ea * jnp.sin(b)
        den = lre * lre + lim * lim
        qr = (nr * lre + ni * lim) / den
        qi = (ni * lre - nr * lim) / den
        bre, bim = bt_ref[d, 0], bt_ref[d, 1]
        bqr = bre * qr - bim * qi
        bqi = bre * qi + bim * qr
        cre, cim = c_ref[d, 0], c_ref[d, 1]

        e = lax.broadcasted_iota(jnp.int32, (N_POW, 1), 0).astype(F32)
        mag = jnp.exp(e * a)
        pw_ref[0] = mag * jnp.cos(e * b)
        pw_ref[1] = mag * jnp.sin(e * b)
        mag = jnp.exp((L * e) * a)
        apow_ref[d, 0] = mag * jnp.cos((L * e) * b)
        apow_ref[d, 1] = mag * jnp.sin((L * e) * b)

        def power(k):
            return pw_ref[0, k:k + 1, :], pw_ref[1, k:k + 1, :]

        def c_block(k):
            pr, pi = power(k)
            return cre * pr - cim * pi, -(cre * pi + cim * pr)

        for i in range(L):
            rows = pl.ds(pl.multiple_of(((i + p) % L) * S5_CH, S5_CH), S5_CH)
            pr, pi = power(L - 1 - i if d == 0 else i)
            wend_ref[rows, d * 256:d * 256 + 128] = (bqr * pr - bqi * pi).astype(BF16)
            wend_ref[rows, d * 256 + 128:d * 256 + 256] = (bqr * pi + bqi * pr).astype(BF16)
            mre, mim = c_block(i + 1 if d == 0 else L - i)
            woutT_ref[rows, d * 256:d * 256 + 128] = mre.astype(BF16)
            woutT_ref[rows, d * 256 + 128:d * 256 + 256] = mim.astype(BF16)
            lag_rows = slice(i * S5_CH, (i + 1) * S5_CH)
            mre, mim = c_block(i if d == 0 else L - 1 - i)
            pct_ref[lag_rows, 0:128] = mre
            pct_ref[lag_rows, 128:256] = mim

        bqcat = jnp.concatenate([bqr, bqi], axis=1)
        krows.append(_dot_nt(bqcat, pct_ref[...], precision=lax.Precision.HIGHEST))

    for i in range(L):
        rows = pl.ds(pl.multiple_of(((i + p) % L) * S5_CH, S5_CH), S5_CH)
        blk_f = pltpu.roll(krows[0], ((i + p) % L) * S5_CH, 1)
        blk_b = pltpu.roll(krows[1], ((i + 1 + p) % L) * S5_CH, 1)
        blk = jnp.where(src_blk >= i, blk_f, 0.0) + jnp.where(src_blk <= i, blk_b, 0.0)
        ktoe_ref[rows, :] = blk.astype(BF16)


def _s5_prep(lam_re, lam_im, log_step, b_re, b_im, c_re, c_im):
    def pair_vec(a):
        return a.astype(F32).reshape(2, S5_PAIRS, 1, S5_P).transpose(1, 0, 2, 3)

    step = jnp.broadcast_to(jnp.exp(log_step.astype(F32))[:, :, None], (2, N_SSM_GROUPS, SSM_STATE))
    vec = jnp.concatenate([pair_vec(lam_re), pair_vec(lam_im), pair_vec(step),
                           jnp.zeros((S5_PAIRS, 2, 5, S5_P), F32)], axis=2)

    def block_diag_ch_by_state(m):
        m = m.astype(F32).reshape(2, S5_PAIRS, 2, SSM_GROUP, SSM_STATE)
        eye = jnp.eye(2, dtype=F32)
        out = m[:, :, :, :, None, :] * eye[None, None, :, None, :, None]
        return out.reshape(2, S5_PAIRS, S5_CH, S5_P).transpose(1, 0, 2, 3)

    bt = jnp.stack([block_diag_ch_by_state(b_re.transpose(0, 1, 3, 2)),
                    block_diag_ch_by_state(b_im.transpose(0, 1, 3, 2))], axis=2)
    cc = jnp.stack([block_diag_ch_by_state(c_re), block_diag_ch_by_state(c_im)], axis=2)

    mat = jax.ShapeDtypeStruct((S5_PAIRS, S5_W, S5_W), BF16)
    mat_spec = pl.BlockSpec((None, S5_W, S5_W), lambda p: (p, 0, 0))
    return pl.pallas_call(
        _s5_prep_body,
        grid=(S5_PAIRS,),
        in_specs=[
            pl.BlockSpec((None, 2, 8, S5_P), lambda p: (p, 0, 0, 0)),
            pl.BlockSpec((None, 2, 2, S5_CH, S5_P), lambda p: (p, 0, 0, 0, 0)),
            pl.BlockSpec((None, 2, 2, S5_CH, S5_P), lambda p: (p, 0, 0, 0, 0)),
        ],
        out_specs=[mat_spec, mat_spec, mat_spec,
                   pl.BlockSpec((None, 2, 2, N_POW, S5_P), lambda p: (p, 0, 0, 0, 0))],
        out_shape=[mat, mat, mat, jax.ShapeDtypeStruct((S5_PAIRS, 2, 2, N_POW, S5_P), F32)],
        scratch_shapes=[pltpu.VMEM((2, N_POW, S5_P), F32), pltpu.VMEM((S5_W, 2 * S5_P), F32)],
        compiler_params=_cparams(("parallel",)),
        name="s5_prep",
    )(vec, bt, cc)


S5_SEG_ROWS = 256
S5_N_SEG = N_CHUNK_ROWS // S5_SEG_ROWS
S5_SUB = S5_SEG_ROWS // S5_CHUNK
S5_SUB_PER_SEQ = (DEC_SEQ // S5_CHUNK) // S5_CHUNK


def _s5_body(tu_ref, s0_ref, wend_ref, ktoe_ref, woutT_ref, apow_ref, ty_ref, fin_ref, z_ref, xs_ref, g_ref):
    seg = pl.program_id(0)
    p = pl.program_id(1)
    L = S5_CHUNK
    lane_blk = lax.broadcasted_iota(jnp.int32, (S5_SEG_ROWS, 128), 1) // S5_CH

    tiles = []
    for t in range(LANE_TILES):
        tile = None
        for k in range(128 // S5_CH):
            j = (LANE_TILES * t + k + L - p) % L
            blk = tu_ref[j, t]
            tile = blk if tile is None else jnp.where(lane_blk == k, blk, tile)
        tiles.append(tile)
    s = jnp.concatenate(tiles, axis=1)
    z = _dot(s, wend_ref[...])
    for q in range(LANE_TILES):
        z_ref[q] = z[:, q * 128:(q + 1) * 128]

    ends = []
    for d in range(2):
        are, aim = apow_ref[d, 0, 1:2, :], apow_ref[d, 1, 1:2, :]
        xr = jnp.zeros((S5_SUB, S5_P), F32)
        xi = jnp.zeros((S5_SUB, S5_P), F32)
        for c in (range(L) if d == 0 else reversed(range(L))):
            rows = pl.ds(c, S5_SUB, stride=L)
            xs_ref[2 * d, rows, :] = xr
            xs_ref[2 * d + 1, rows, :] = xi
            zr = z_ref[2 * d, rows, :]
            zi = z_ref[2 * d + 1, rows, :]
            xr, xi = are * xr - aim * xi + zr, are * xi + aim * xr + zi
        ends.append((xr, xi))
        fin_ref[d, 0] = xr
        fin_ref[d, 1] = xi

    @pl.when(seg > 0)
    def _():
        for d in range(2):
            er, ei = ends[d]
            a16r, a16i = apow_ref[d, 0, L:L + 1, :], apow_ref[d, 1, L:L + 1, :]
            for bb in range(S5_SUB // S5_SUB_PER_SEQ):
                gr = s0_ref[bb, 2 * d:2 * d + 1, :]
                gi = s0_ref[bb, 2 * d + 1:2 * d + 2, :]
                order = range(S5_SUB_PER_SEQ) if d == 0 else reversed(range(S5_SUB_PER_SEQ))
                prev = None
                for sub in order:
                    row = bb * S5_SUB_PER_SEQ + sub
                    if prev is not None:
                        gr, gi = (a16r * gr - a16i * gi + er[prev:prev + 1, :],
                                  a16r * gi + a16i * gr + ei[prev:prev + 1, :])
                    g_ref[0, row:row + 1, :] = gr
                    g_ref[1, row:row + 1, :] = gi
                    prev = row
            gr, gi = g_ref[0], g_ref[1]
            for c in range(L):
                k = c if d == 0 else L - 1 - c
                pr, pi = apow_ref[d, 0, k:k + 1, :], apow_ref[d, 1, k:k + 1, :]
                rows = pl.ds(c, S5_SUB, stride=L)
                xs_ref[2 * d, rows, :] = xs_ref[2 * d, rows, :] + (pr * gr - pi * gi)
                xs_ref[2 * d + 1, rows, :] = xs_ref[2 * d + 1, rows, :] + (pr * gi + pi * gr)

    xs = jnp.concatenate([xs_ref[q].astype(BF16) for q in range(LANE_TILES)], axis=1)
    y = _dot(s, ktoe_ref[...]) + _dot_nt(xs, woutT_ref[...])

    @pl.when(p == 0)
    def _():
        ty_ref[...] = jnp.zeros_like(ty_ref)

    for t in range(LANE_TILES):
        yt = y[:, t * 128:(t + 1) * 128]
        for k in range(128 // S5_CH):
            j = (LANE_TILES * t + k + L - p) % L
            ty_ref[j, t] = jnp.where(lane_blk == k, yt, ty_ref[j, t])


def _s5_scan(tu, s0, wend, ktoe, woutT, apow):
    mat_spec = pl.BlockSpec((None, S5_W, S5_W), lambda g, p: (p, 0, 0))
    seqs = S5_SUB // S5_SUB_PER_SEQ
    return pl.pallas_call(
        _s5_body,
        grid=(S5_N_SEG, S5_PAIRS),
        in_specs=[
            pl.BlockSpec((S5_CHUNK, LANE_TILES, S5_SEG_ROWS, 128), lambda g, p: (0, 0, g, 0)),
            pl.BlockSpec((None, seqs, 4, S5_P), lambda g, p: (p, jnp.maximum(g - 1, 0), 0, 0)),
            mat_spec, mat_spec, mat_spec,
            pl.BlockSpec((None, 2, 2, N_POW, S5_P), lambda g, p: (p, 0, 0, 0, 0)),
        ],
        out_specs=[
            pl.BlockSpec((S5_CHUNK, LANE_TILES, S5_SEG_ROWS, 128), lambda g, p: (0, 0, g, 0)),
            pl.BlockSpec((None, 2, 2, S5_SUB, S5_P), lambda g, p: (p, 0, 0, g, 0)),
        ],
        out_shape=[
            jax.ShapeDtypeStruct((S5_CHUNK, LANE_TILES, N_CHUNK_ROWS, 128), F32),
            jax.ShapeDtypeStruct((S5_PAIRS, 2, 2, S5_N_SEG * S5_SUB, S5_P), F32),
        ],
        scratch_shapes=[pltpu.VMEM((LANE_TILES, S5_SEG_ROWS, 128), F32),
                        pltpu.VMEM((LANE_TILES, S5_SEG_ROWS, 128), F32),
                        pltpu.VMEM((2, S5_SUB, S5_P), F32)],
        compiler_params=_cparams(("parallel", "arbitrary")),
        name="s5_scan",
    )(tu, s0, wend, ktoe, woutT, apow)


TM_GLU = 1024


def _glu_body(tu_ref, ty_ref, d_ref, gw_ref, gb_ref, o_ref, u_ref, ys_ref):
    rows = TM_GLU // S5_CHUNK
    for j in range(S5_CHUNK):
        back = ((S5_CHUNK - j) % S5_CHUNK) * S5_CH
        uj = jnp.concatenate([tu_ref[j, t].astype(F32) for t in range(LANE_TILES)], axis=1)
        yj = jnp.concatenate([ty_ref[j, t] for t in range(LANE_TILES)], axis=1)
        if back:
            uj = pltpu.roll(uj, back, 1)
            yj = pltpu.roll(yj, back, 1)
        for t in range(LANE_TILES):
            u_ref[t, pl.ds(j, rows, stride=S5_CHUNK), :] = uj[:, t * 128:(t + 1) * 128]
            ys_ref[t, pl.ds(j, rows, stride=S5_CHUNK), :] = yj[:, t * 128:(t + 1) * 128]
    u = jnp.concatenate([u_ref[t] for t in range(LANE_TILES)], axis=1)
    y = d_ref[...] * u + jnp.concatenate([ys_ref[t] for t in range(LANE_TILES)], axis=1)
    cdf = 0.5 * (1.0 + jnp.tanh(math.sqrt(2.0 / math.pi) * (y + 0.044715 * (y * y * y))))
    y = y * cdf
    gate = jax.nn.sigmoid(_dot(y.astype(BF16), gw_ref[...]) + gb_ref[...])
    o_ref[...] = (y * gate).astype(BF16)


def _glu(tu, ty, ssm_d, glu_w, glu_b):
    tm = TM_GLU
    rows = tm // S5_CHUNK
    lay_spec = pl.BlockSpec((S5_CHUNK, LANE_TILES, rows, 128), lambda i: (0, 0, i, 0))
    return pl.pallas_call(
        _glu_body,
        grid=(N_TOK // tm,),
        in_specs=[
            lay_spec, lay_spec,
            pl.BlockSpec((1, HALF), lambda i: (0, 0)),
            pl.BlockSpec((HALF, HALF), lambda i: (0, 0)),
            pl.BlockSpec((1, HALF), lambda i: (0, 0)),
        ],
        out_specs=pl.BlockSpec((tm, HALF), lambda i: (i, 0)),
        out_shape=jax.ShapeDtypeStruct((N_TOK, HALF), BF16),
        scratch_shapes=[pltpu.VMEM((LANE_TILES, tm, 128), F32), pltpu.VMEM((LANE_TILES, tm, 128), F32)],
        compiler_params=_cparams(("parallel",)),
        name="s5_glu",
    )(tu, ty, ssm_d.reshape(1, HALF), glu_w.astype(BF16), glu_b.reshape(1, HALF))


def kernel(x_prompt, x_sample, c, cache_k, cache_v, state_s5_re, state_s5_im, c_ctx, norm1_g, norm2_g, ada_w, ada_b, mlp_w1, mlp_w2, ab_w_in, pool_w, pool_scale, conv_w, ab_w_out, cd_w_in, na_rpb, ssm_lambda_re, ssm_lambda_im, ssm_log_step, ssm_b_re, ssm_b_im, ssm_c_re, ssm_c_im, ssm_d, glu_w, glu_b, cd_w_out, final_g):
    depth = ada_w.shape[0]
    xs = (x_prompt.reshape(N_PROMPT_TOK, D_MODEL), x_sample.reshape(N_SAMPLE_TOK, D_MODEL))
    cond = jnp.concatenate([c_ctx[None, :], c, jnp.zeros((N_COND - 1 - DEC_BATCH, D_MODEL), F32)], axis=0)
    mods = _adaln(cond, ada_w, ada_b).reshape(depth, N_COND, 6, D_MODEL)

    new_k, new_v, new_re, new_im = [], [], [], []
    for layer in range(depth):
        i = layer // 2
        final = layer == depth - 1
        if layer % 2 == 0:
            y, x = _even_layer_mix(xs, mods, layer, norm1_g[layer], ab_w_in[i].astype(BF16),
                                   pool_w[i], pool_scale[i], conv_w[i])
            ya, ya_col, yb, yb_col = y, 0, y, 1
            w_out = ab_w_out[i]
        else:
            x = xs[0]
            qkv, tu = _inproj_odd(x, mods, layer, norm1_g[layer], cd_w_in[i].astype(BF16))
            o_p, ck, cv = _ctx_attn(qkv)
            new_k.append(ck)
            new_v.append(cv)
            o_s = _na_attn(qkv, cache_k, cache_v, _na_col_table(na_rpb[i]), i)
            attn = (o_p, o_s)

            wend, ktoe, woutT, apow = _s5_prep(ssm_lambda_re[i], ssm_lambda_im[i], ssm_log_step[i],
                                               ssm_b_re[i], ssm_b_im[i], ssm_c_re[i], ssm_c_im[i])

            def pair_state(s):
                return s.astype(F32).reshape(DEC_BATCH, 2, S5_PAIRS, S5_P).transpose(2, 0, 1, 3)

            s0 = jnp.stack([pair_state(state_s5_re[:, i]), pair_state(state_s5_im[:, i])], axis=3)
            s0 = s0.reshape(S5_PAIRS, DEC_BATCH, 4, S5_P)
            ty, fin = _s5_scan(tu, s0, wend, ktoe, woutT, apow)
            fin = fin[:, :, :, :BATCH]
            new_re.append(fin[:, :, 0].transpose(2, 1, 0, 3).reshape(BATCH, 2, N_SSM_GROUPS, SSM_STATE))
            new_im.append(fin[:, :, 1].transpose(2, 1, 0, 3).reshape(BATCH, 2, N_SSM_GROUPS, SSM_STATE))
            d_out = _glu(tu, ty, ssm_d[i], glu_w[i], glu_b[i])
            ya, ya_col, yb, yb_col = attn, 0, d_out, 0
            w_out = cd_w_out[i]
        xs = _mlp(x, ya, ya_col, yb, yb_col, w_out.astype(BF16), mods, layer, norm2_g[layer], final_g,
                  mlp_w1[layer].astype(BF16), mlp_w2[layer].astype(BF16), final)

    y_prompt = xs[0].reshape(BATCH, SEQ, D_MODEL)
    y_sample = xs[1].reshape(DEC_BATCH, DEC_SEQ, D_MODEL)
    return (y_prompt, y_sample, jnp.concatenate(new_k, axis=1), jnp.concatenate(new_v, axis=1),
            jnp.stack(new_re, axis=1), jnp.stack(new_im, axis=1))
```

```python
import functools
import math

import jax
import jax.numpy as jnp
from jax import lax
from jax.experimental import pallas as pl
from jax.experimental.pallas import tpu as pltpu

F32 = jnp.float32
BF16 = jnp.bfloat16

D_MODEL = 1024
BATCH = 16
SEQ = 256
DEC_BATCH = 4
DEC_SEQ = 2048
PAST_LEN = 256
GRID_W = 64
GRID_ROWS = DEC_SEQ // GRID_W
HALF = 512
POOL_WINDOWS = (2, 4, 8, 16)
POOL_GROUP = 128
HEAD_DIM = 64
N_HEADS = 8
WIN_H = 8
WIN_W = 16
SSM_GROUP = 16
N_SSM_GROUPS = 32
SSM_STATE = 64
D_FF = 4096
EPS = 1e-6

N_PROMPT_TOK = BATCH * SEQ
N_SAMPLE_TOK = DEC_BATCH * DEC_SEQ
N_TOK = N_PROMPT_TOK + N_SAMPLE_TOK
N_COND = 8

S5_CHUNK = 16
S5_PAIRS = N_SSM_GROUPS // 2
S5_CH = 2 * SSM_GROUP
S5_P = 2 * SSM_STATE
S5_W = S5_CHUNK * S5_CH
S5_BP_SAMPLE = 8

VMEM_LIMIT = 56 * 1024 * 1024


def _cparams(sem):
    return pltpu.CompilerParams(dimension_semantics=sem, vmem_limit_bytes=VMEM_LIMIT)


def _cond_row(i, tm):
    npt = N_PROMPT_TOK // tm
    per = DEC_SEQ // tm
    return jnp.where(i < npt, 0, 1 + (i - npt) // per)


def _normmod(x, g, shift, scale):
    ms = jnp.mean(x * x, axis=-1, keepdims=True)
    return (x * lax.rsqrt(ms + EPS) * g) * (1.0 + scale) + shift


def _dot(a, b):
    return jnp.dot(a, b, preferred_element_type=F32)


def _dot_nt(a, b, precision=None):
    return lax.dot_general(a, b, (((1,), (1,)), ((), ())), preferred_element_type=F32, precision=precision)


def _adaln_body(c_ref, w_ref, b_ref, o_ref):
    c = c_ref[...]
    s = c * jax.nn.sigmoid(c)
    o_ref[...] = _dot(s.astype(BF16), w_ref[...].astype(BF16)) + b_ref[...]


def _adaln(cond, ada_w, ada_b):
    depth = ada_w.shape[0]
    nj = 6
    return pl.pallas_call(
        _adaln_body,
        grid=(depth, nj),
        in_specs=[
            pl.BlockSpec((N_COND, D_MODEL), lambda l, j: (0, 0)),
            pl.BlockSpec((None, D_MODEL, D_MODEL), lambda l, j: (l, 0, j)),
            pl.BlockSpec((None, 1, D_MODEL), lambda l, j: (l, 0, j)),
        ],
        out_specs=pl.BlockSpec((None, N_COND, D_MODEL), lambda l, j: (l, 0, j)),
        out_shape=jax.ShapeDtypeStruct((depth, N_COND, 6 * D_MODEL), F32),
        compiler_params=_cparams(("arbitrary", "arbitrary")),
        name="adaln",
    )(cond, ada_w, ada_b.reshape(depth, 1, 6 * D_MODEL))


TM_PROJ = 512


def _stream_tile(x_refs, tm):
    if len(x_refs) == 1:
        return x_refs[0][...]
    is_prompt = pl.program_id(0) < N_PROMPT_TOK // tm
    return jnp.where(is_prompt, x_refs[0][...], x_refs[1][...])


N_CHUNK_ROWS = N_TOK // S5_CHUNK
LANE_TILES = S5_W // 128


def _inproj_odd_body(x_ref, m_ref, g_ref, w_ref, qkv_ref, tu_ref, u_ref):
    m = m_ref[...]
    h = _normmod(x_ref[...], g_ref[...], m[0:1], m[1:2])
    acc = _dot(h.astype(BF16), w_ref[...])
    qkv_ref[...] = acc[:, 0:3 * HALF].astype(BF16)
    for t in range(LANE_TILES):
        u_ref[t] = acc[:, 3 * HALF + t * 128:3 * HALF + (t + 1) * 128]
    rows = TM_PROJ // S5_CHUNK
    for j in range(S5_CHUNK):
        uj = jnp.concatenate([u_ref[t, pl.ds(j, rows, stride=S5_CHUNK), :] for t in range(LANE_TILES)], axis=1)
        if j:
            uj = pltpu.roll(uj, j * S5_CH, 1)
        for t in range(LANE_TILES):
            tu_ref[j, t] = uj[:, t * 128:(t + 1) * 128].astype(BF16)


def _inproj_odd(x, mods, layer, g, w):
    tm = TM_PROJ
    rows = tm // S5_CHUNK
    return pl.pallas_call(
        _inproj_odd_body,
        grid=(N_TOK // tm,),
        in_specs=[
            pl.BlockSpec((tm, D_MODEL), lambda i: (i, 0)),
            pl.BlockSpec((None, None, 6, D_MODEL), lambda i: (layer, _cond_row(i, tm), 0, 0)),
            pl.BlockSpec((1, D_MODEL), lambda i: (0, 0)),
            pl.BlockSpec((D_MODEL, 4 * HALF), lambda i: (0, 0)),
        ],
        out_specs=[
            pl.BlockSpec((tm, 3 * HALF), lambda i: (i, 0)),
            pl.BlockSpec((S5_CHUNK, LANE_TILES, rows, 128), lambda i: (0, 0, i, 0)),
        ],
        out_shape=[
            jax.ShapeDtypeStruct((N_TOK, 3 * HALF), BF16),
            jax.ShapeDtypeStruct((S5_CHUNK, LANE_TILES, N_CHUNK_ROWS, 128), BF16),
        ],
        scratch_shapes=[pltpu.VMEM((LANE_TILES, tm, 128), F32)],
        compiler_params=_cparams(("parallel",)),
        name="inproj_odd",
    )(x, mods, g.reshape(1, D_MODEL), w)


TM_MIX = 256
HALO = 16
RING = 3
LAG = 2


def _even_step(phase, x_refs, m_ref, g_ref, w_ref, pw_ref, ps_ref, cw_ref, y_ref, xcat_ref, proj_ref, tail_ref,
               pad_ref, h_ref):
    i = pl.program_id(0)
    write_slot, cur_slot, next_slot = phase, (phase + 1) % RING, (phase + 2) % RING

    m = m_ref[...]
    x = _stream_tile(x_refs, TM_MIX)
    xcat_ref[...] = x
    h_ref[...] = _normmod(x, g_ref[...], m[0:1], m[1:2]).astype(BF16)
    n_pieces = len(POOL_WINDOWS) + HALF // 128
    pw = (4 * HALF) // n_pieces

    def project(piece):
        cols = slice(piece * pw, (piece + 1) * pw)
        proj_ref[write_slot, :, cols] = _dot(h_ref[...], w_ref[:, cols])

    t = jnp.maximum(i - LAG, 0)
    n_prompt_tiles = N_PROMPT_TOK // TM_MIX
    tiles_per_seq = DEC_SEQ // TM_MIX
    is_prompt = t < n_prompt_tiles
    tile_in_seq = jnp.where(is_prompt, 0, (t - n_prompt_tiles) % tiles_per_seq)
    seq_len = jnp.where(is_prompt, SEQ, DEC_SEQ)
    has_prev = tile_in_seq > 0
    has_next = jnp.logical_and(jnp.logical_not(is_prompt), tile_in_seq < tiles_per_seq - 1)
    prev_keep = jnp.where(has_prev, 1.0, 0.0).astype(F32)
    next_keep = jnp.where(has_next, 1.0, 0.0).astype(F32)
    pos = tile_in_seq * TM_MIX + lax.broadcasted_iota(jnp.int32, (TM_MIX, 1), 0)
    lo_rows, hi_rows = HALO, HALO + TM_MIX
    head = slice(0, HALO)

    for gi, w in enumerate(POOL_WINDOWS):
        project(gi)
        cs = slice(gi * POOL_GROUP, (gi + 1) * POOL_GROUP)
        pad_ref[0:HALO, cs] = tail_ref[:, cs] * prev_keep
        pad_ref[lo_rows:hi_rows, cs] = proj_ref[cur_slot, :, cs]
        pad_ref[hi_rows:hi_rows + HALO, cs] = proj_ref[next_slot, head, cs] * next_keep
        s = None
        for o in range(-(w // 2), w - w // 2):
            v = pad_ref[lo_rows + o:hi_rows + o, cs]
            s = v if s is None else s + v
        lo = jnp.maximum(pos - w // 2, 0)
        hi = jnp.minimum(pos + (w - w // 2), seq_len)
        inv = 1.0 / (hi - lo).astype(F32)
        pooled = s * inv - pad_ref[lo_rows:hi_rows, cs]
        mixed = _dot(pooled.astype(BF16), pw_ref[gi]) * ps_ref[:, cs]
        y_ref[:, cs] = mixed.astype(BF16)

    for ci in range(HALF // 128):
        project(len(POOL_WINDOWS) + ci)
        cs = slice(ci * 128, (ci + 1) * 128)
        bs = slice(HALF + ci * 128, HALF + (ci + 1) * 128)
        gs = slice(2 * HALF + ci * 128, 2 * HALF + (ci + 1) * 128)
        vs = slice(3 * HALF + ci * 128, 3 * HALF + (ci + 1) * 128)
        pad_ref[0:HALO, cs] = (tail_ref[:, gs] * tail_ref[:, vs]) * prev_keep
        pad_ref[lo_rows:hi_rows, cs] = proj_ref[cur_slot, :, gs] * proj_ref[cur_slot, :, vs]
        pad_ref[hi_rows:hi_rows + HALO, cs] = (proj_ref[next_slot, head, gs] * proj_ref[next_slot, head, vs]) * next_keep
        conv = (cw_ref[0:1, cs] * pad_ref[lo_rows - 1:hi_rows - 1, cs]
                + cw_ref[1:2, cs] * pad_ref[lo_rows:hi_rows, cs]
                + cw_ref[2:3, cs] * pad_ref[lo_rows + 1:hi_rows + 1, cs])
        y_ref[:, bs] = (proj_ref[cur_slot, :, bs] * conv).astype(BF16)

    tail_ref[...] = proj_ref[cur_slot, TM_MIX - HALO:TM_MIX, :]


def _even_body(n_x, *refs):
    x_refs, rest = refs[:n_x], refs[n_x:]
    proj_ref, tail_ref = rest[-4], rest[-3]
    i = pl.program_id(0)

    @pl.when(i == 0)
    def _():
        proj_ref[...] = jnp.zeros_like(proj_ref)
        tail_ref[...] = jnp.zeros_like(tail_ref)

    for phase in range(RING):
        @pl.when(i % RING == phase)
        def _(phase=phase):
            _even_step(phase, x_refs, *rest)


def _even_layer_mix(xs, mods, layer, g, w, pool_w, pool_scale, conv_w):
    tm = TM_MIX
    n_tiles = N_TOK // tm
    npt = N_PROMPT_TOK // tm
    last = n_tiles - 1
    if len(xs) == 1:
        x_specs = [pl.BlockSpec((tm, D_MODEL), lambda i: (jnp.minimum(i, last), 0))]
    else:
        x_specs = [pl.BlockSpec((tm, D_MODEL), lambda i: (jnp.minimum(i, npt - 1), 0)),
                   pl.BlockSpec((tm, D_MODEL), lambda i: (jnp.clip(i - npt, 0, last - npt), 0))]
    return pl.pallas_call(
        functools.partial(_even_body, len(xs)),
        grid=(n_tiles + LAG,),
        in_specs=x_specs + [
            pl.BlockSpec((None, None, 6, D_MODEL), lambda i: (layer, _cond_row(jnp.minimum(i, last), tm), 0, 0)),
            pl.BlockSpec((1, D_MODEL), lambda i: (0, 0)),
            pl.BlockSpec((D_MODEL, 4 * HALF), lambda i: (0, 0)),
            pl.BlockSpec((len(POOL_WINDOWS), POOL_GROUP, POOL_GROUP), lambda i: (0, 0, 0)),
            pl.BlockSpec((1, HALF), lambda i: (0, 0)),
            pl.BlockSpec((3, HALF), lambda i: (0, 0)),
        ],
        out_specs=[pl.BlockSpec((tm, 2 * HALF), lambda i: (jnp.maximum(i - LAG, 0), 0)),
                   pl.BlockSpec((tm, D_MODEL), lambda i: (jnp.minimum(i, last), 0))],
        out_shape=[jax.ShapeDtypeStruct((N_TOK, 2 * HALF), BF16), jax.ShapeDtypeStruct((N_TOK, D_MODEL), F32)],
        scratch_shapes=[pltpu.VMEM((RING, tm, 4 * HALF), F32), pltpu.VMEM((HALO, 4 * HALF), F32),
                        pltpu.VMEM((tm + 2 * HALO, HALF), F32), pltpu.VMEM((tm, D_MODEL), BF16)],
        compiler_params=_cparams(("arbitrary",)),
        name="even_layer_mix",
    )(*xs, mods, g.reshape(1, D_MODEL), w, pool_w.astype(BF16), pool_scale.reshape(1, HALF), conv_w)


TM_MLP = 1024
RC_MLP = 256


def _mlp_body(final, x_ref, ya_ref, yb_ref, wo_ref, m_ref, g_ref, fg_ref, w1_ref, w2_ref, o_ref):
    m = m_ref[...]
    gain = g_ref[...] * (1.0 + m[4:5])
    for r in range(TM_MLP // RC_MLP):
        rows = slice(r * RC_MLP, (r + 1) * RC_MLP)
        mix = _dot(ya_ref[rows, :], wo_ref[0:HALF, :]) + _dot(yb_ref[rows, :], wo_ref[HALF:2 * HALF, :])
        x1 = x_ref[rows, :] + m[2:3] * mix
        ms = jnp.mean(x1 * x1, axis=-1, keepdims=True)
        h = (x1 * lax.rsqrt(ms + EPS) * gain + m[3:4]).astype(BF16)
        t = _dot(h, w1_ref[...])
        t = jnp.square(jnp.maximum(t, 0.0)).astype(BF16)
        out = x1 + m[5:6] * _dot(t, w2_ref[...])
        if final:
            ms = jnp.mean(out * out, axis=-1, keepdims=True)
            out = out * lax.rsqrt(ms + EPS) * fg_ref[...]
        o_ref[rows, :] = out


def _resident(shape):
    zeros = (0,) * len(shape)
    return pl.BlockSpec(shape, lambda i: zeros, pipeline_mode=pl.Buffered(1))


def _mlp_call(x, x_off, ya, ya_off, ya_col, yb, yb_off, yb_col, n_tiles, tile0, w_out, mods, layer, g2, final_g,
              w1, w2, final):
    tm = TM_MLP
    return pl.pallas_call(
        functools.partial(_mlp_body, final),
        grid=(n_tiles,),
        in_specs=[
            pl.BlockSpec((tm, D_MODEL), lambda i: (i + x_off, 0)),
            pl.BlockSpec((tm, HALF), lambda i: (i + ya_off, ya_col)),
            pl.BlockSpec((tm, HALF), lambda i: (i + yb_off, yb_col)),
            _resident((D_MODEL, D_MODEL)),
            pl.BlockSpec((None, None, 6, D_MODEL), lambda i: (layer, _cond_row(i + tile0, tm), 0, 0)),
            _resident((1, D_MODEL)),
            _resident((1, D_MODEL)),
            _resident((D_MODEL, D_FF)),
            _resident((D_FF, D_MODEL)),
        ],
        out_specs=pl.BlockSpec((tm, D_MODEL), lambda i: (i, 0)),
        out_shape=jax.ShapeDtypeStruct((n_tiles * tm, D_MODEL), F32),
        compiler_params=_cparams(("parallel",)),
        name="mlp",
    )(x, ya, yb, w_out, mods, g2.reshape(1, D_MODEL), final_g.reshape(1, D_MODEL), w1, w2)


def _mlp(x, ya, ya_col, yb, yb_col, w_out, mods, layer, g2, final_g, w1, w2, final):
    tm = TM_MLP
    npt = N_PROMPT_TOK // tm
    weights = (w_out, mods, layer, g2, final_g, w1, w2, final)
    if not final:
        if isinstance(ya, tuple):
            ya = jnp.concatenate(ya, axis=0)
        return (_mlp_call(x, 0, ya, 0, ya_col, yb, 0, yb_col, N_TOK // tm, 0, *weights),)
    outs = []
    for s, (tile0, n_tiles) in enumerate(((0, npt), (npt, N_TOK // tm - npt))):
        a, a_off = (ya[s], 0) if isinstance(ya, tuple) else (ya, tile0)
        outs.append(_mlp_call(x, tile0, a, a_off, ya_col, yb, tile0, yb_col, n_tiles, tile0, *weights))
    return tuple(outs)


def _ctx_attn_body(q_ref, k_ref, v_ref, o_ref, ck_ref, cv_ref):
    lane = lax.broadcasted_iota(jnp.int32, (1, 2 * HEAD_DIM), 1)
    own = [lane < HEAD_DIM, lane >= HEAD_DIM]
    for h in range(N_HEADS):
        sl = slice(h * HEAD_DIM, (h + 1) * HEAD_DIM)
        ck_ref[h] = k_ref[:, sl].astype(F32)
        cv_ref[h] = v_ref[:, sl].astype(F32)
    for hp in range(N_HEADS // 2):
        sl = slice(hp * 2 * HEAD_DIM, (hp + 1) * 2 * HEAD_DIM)
        q = q_ref[:, sl] * jnp.asarray(HEAD_DIM ** -0.5, BF16)
        k = k_ref[:, sl]
        v = v_ref[:, sl]
        q2 = jnp.concatenate([jnp.where(own[0], q, jnp.zeros_like(q)),
                              jnp.where(own[1], q, jnp.zeros_like(q))], axis=0)
        s = _dot_nt(q2, k)
        m = jnp.max(s, axis=-1, keepdims=True)
        p = jnp.exp(s - m).astype(BF16)
        out = None
        for hh in range(2):
            num = _dot(p[hh * SEQ:(hh + 1) * SEQ], jnp.where(own[hh], v, jnp.ones_like(v)))
            den = jnp.where(own[hh], pltpu.roll(num, HEAD_DIM, 1), 1.0)
            o = num / den
            out = o if out is None else jnp.where(own[0], out, o)
        o_ref[:, sl] = out.astype(BF16)


def _ctx_attn(proj):
    return pl.pallas_call(
        _ctx_attn_body,
        grid=(BATCH,),
        in_specs=[
            pl.BlockSpec((SEQ, HALF), lambda b: (b, 0)),
            pl.BlockSpec((SEQ, HALF), lambda b: (b, 1)),
            pl.BlockSpec((SEQ, HALF), lambda b: (b, 2)),
        ],
        out_specs=[
            pl.BlockSpec((SEQ, HALF), lambda b: (b, 0)),
            pl.BlockSpec((None, None, N_HEADS, SEQ, HEAD_DIM), lambda b: (b, 0, 0, 0, 0)),
            pl.BlockSpec((None, None, N_HEADS, SEQ, HEAD_DIM), lambda b: (b, 0, 0, 0, 0)),
        ],
        out_shape=[
            jax.ShapeDtypeStruct((N_PROMPT_TOK, HALF), BF16),
            jax.ShapeDtypeStruct((BATCH, 1, N_HEADS, SEQ, HEAD_DIM), F32),
            jax.ShapeDtypeStruct((BATCH, 1, N_HEADS, SEQ, HEAD_DIM), F32),
        ],
        compiler_params=_cparams(("parallel",)),
        name="ctx_attn",
    )(proj, proj, proj)


NA_MASKED = -1e30
NA_QROWS = 4
NA_KROWS = NA_QROWS + WIN_H
NA_NQ = NA_QROWS * GRID_W
NA_NK = NA_KROWS * GRID_W
NA_BLOCKS = GRID_ROWS // NA_QROWS
NA_TYPES = 3


def _na_col_table(rpb):
    qc = jnp.arange(GRID_W)[:, None]
    kc = jnp.arange(GRID_W)[None, :]
    c0 = jnp.clip(qc - WIN_W // 2, 0, GRID_W - WIN_W)
    valid = (kc >= c0) & (kc < c0 + WIN_W)
    coff = jnp.clip(kc - qc, -(WIN_W - 1), WIN_W - 1) + (WIN_W - 1)
    onehot = (coff.reshape(1, -1) == jnp.arange(2 * WIN_W - 1)[:, None]).astype(F32)
    t = jnp.dot(rpb.astype(F32).reshape(N_HEADS * (2 * WIN_H - 1), 2 * WIN_W - 1), onehot,
                precision=lax.Precision.HIGHEST)
    t = t.reshape(N_HEADS, 2 * WIN_H - 1, GRID_W, GRID_W)
    return jnp.where(valid[None, None], t, NA_MASKED)


def _na_window_start(rb):
    return jnp.clip(NA_QROWS * rb - WIN_H // 2, 0, GRID_ROWS - NA_KROWS)


def _na_body(q_ref, k_ref, v_ref, ck_ref, cv_ref, tab_ref, o_ref, bias_ref, kc_ref, vc_ref, va_ref):
    @pl.when(pl.program_id(1) == 0)
    def _():
        masked = jnp.full((GRID_W, GRID_W), NA_MASKED, F32)
        for hh in range(2):
            for ty in range(NA_TYPES):
                q0 = (0, NA_QROWS, GRID_ROWS - NA_QROWS)[ty]
                ks = (0, 0, GRID_ROWS - NA_KROWS)[ty]
                for a in range(NA_QROWS):
                    r0 = min(max(q0 + a - WIN_H // 2, 0), GRID_ROWS - WIN_H)
                    for y in range(NA_KROWS):
                        kr = ks + y
                        ok = r0 <= kr < r0 + WIN_H
                        blk = tab_ref[hh, kr - (q0 + a) + WIN_H - 1] if ok else masked
                        bias_ref[ty, hh * NA_NQ + a * GRID_W:hh * NA_NQ + (a + 1) * GRID_W,
                                 y * GRID_W:(y + 1) * GRID_W] = blk

    lane = lax.broadcasted_iota(jnp.int32, (1, 2 * HEAD_DIM), 1)
    own = [lane < HEAD_DIM, lane >= HEAD_DIM]
    kc_ref[...] = jnp.concatenate([ck_ref[0], ck_ref[1]], axis=1).astype(BF16)
    vc = jnp.concatenate([cv_ref[0], cv_ref[1]], axis=1)
    v = v_ref[...]
    for hh in range(2):
        vc_ref[hh] = jnp.where(own[hh], vc, 1.0).astype(BF16)
        va_ref[hh] = jnp.where(own[hh], v, jnp.ones_like(v))

    def block(rb, ty):
        qs = pl.multiple_of(rb * NA_NQ, NA_NQ)
        ks = pl.multiple_of(_na_window_start(rb) * GRID_W, GRID_W)
        q = q_ref[pl.ds(qs, NA_NQ), :] * jnp.asarray(HEAD_DIM ** -0.5, BF16)
        q2 = jnp.concatenate([jnp.where(own[0], q, jnp.zeros_like(q)),
                              jnp.where(own[1], q, jnp.zeros_like(q))], axis=0)
        s_loc = _dot_nt(q2, k_ref[pl.ds(ks, NA_NK), :]) + bias_ref[ty]
        s_ctx = _dot_nt(q2, kc_ref[...])
        m = jnp.maximum(jnp.max(s_loc, axis=-1, keepdims=True), jnp.max(s_ctx, axis=-1, keepdims=True))
        p_loc = jnp.exp(s_loc - m).astype(BF16)
        p_ctx = jnp.exp(s_ctx - m).astype(BF16)
        out = None
        for hh in range(2):
            rows = slice(hh * NA_NQ, (hh + 1) * NA_NQ)
            num = _dot(p_loc[rows], va_ref[hh, pl.ds(ks, NA_NK), :]) + _dot(p_ctx[rows], vc_ref[hh])
            den = jnp.where(own[hh], pltpu.roll(num, HEAD_DIM, 1), 1.0)
            o = num / den
            out = o if out is None else jnp.where(own[0], out, o)
        o_ref[pl.ds(qs, NA_NQ), :] = out.astype(BF16)

    block(0, 0)

    def interior(rb, carry):
        block(rb, 1)
        return carry

    lax.fori_loop(1, NA_BLOCKS - 1, interior, 0)
    block(NA_BLOCKS - 1, 2)


def _na_attn(proj, cache_k, cache_v, col_table, layer_idx):
    row_blk0 = N_PROMPT_TOK // DEC_SEQ
    hp = N_HEADS // 2
    return pl.pallas_call(
        _na_body,
        grid=(hp, DEC_BATCH),
        in_specs=[
            pl.BlockSpec((DEC_SEQ, 128), lambda h, b: (row_blk0 + b, h)),
            pl.BlockSpec((DEC_SEQ, 128), lambda h, b: (row_blk0 + b, hp + h)),
            pl.BlockSpec((DEC_SEQ, 128), lambda h, b: (row_blk0 + b, 2 * hp + h)),
            pl.BlockSpec((None, None, 2, PAST_LEN, HEAD_DIM), lambda h, b: (b, layer_idx, h, 0, 0)),
            pl.BlockSpec((None, None, 2, PAST_LEN, HEAD_DIM), lambda h, b: (b, layer_idx, h, 0, 0)),
            pl.BlockSpec((2, 2 * WIN_H - 1, GRID_W, GRID_W), lambda h, b: (h, 0, 0, 0)),
        ],
        out_specs=pl.BlockSpec((DEC_SEQ, 128), lambda h, b: (b, h)),
        out_shape=jax.ShapeDtypeStruct((N_SAMPLE_TOK, HALF), BF16),
        scratch_shapes=[pltpu.VMEM((NA_TYPES, 2 * NA_NQ, NA_NK), F32),
                        pltpu.VMEM((PAST_LEN, 2 * HEAD_DIM), BF16),
                        pltpu.VMEM((2, PAST_LEN, 2 * HEAD_DIM), BF16),
                        pltpu.VMEM((2, DEC_SEQ, 2 * HEAD_DIM), BF16)],
        compiler_params=_cparams(("parallel", "arbitrary")),
        name="na_attn",
    )(proj, proj, proj, cache_k, cache_v, col_table)


N_POW = 24


def _s5_prep_body(vec_ref, bt_ref, c_ref, wend_ref, ktoe_ref, woutT_ref, apow_ref, pw_ref, pct_ref):
    L = S5_CHUNK
    p = pl.program_id(0)
    lane_blk = lax.broadcasted_iota(jnp.int32, (S5_CH, S5_W), 1) // S5_CH
    src_blk = (lane_blk + L - p) % L
    krows = []
    for d in range(2):
        v = vec_ref[d]
        lre, lim, step = v[0:1], v[1:2], v[2:3]
        a = lre * step
        b = lim * step
        ea = jnp.exp(a)
        nr = ea * jnp.cos(b) - 1.0
        ni = ea * jnp.sin(b)
        den = lre * lre + lim * lim
        qr = (nr * lre + ni * lim) / den
        qi = (ni * lre - nr * lim) / den
        bre, bim = bt_ref[d, 0], bt_ref[d, 1]
        bqr = bre * qr - bim * qi
        bqi = bre * qi + bim * qr
        cre, cim = c_ref[d, 0], c_ref[d, 1]

        e = lax.broadcasted_iota(jnp.int32, (N_POW, 1), 0).astype(F32)
        mag = jnp.exp(e * a)
        pw_ref[0] = mag * jnp.cos(e * b)
        pw_ref[1] = mag * jnp.sin(e * b)
        mag = jnp.exp((L * e) * a)
        apow_ref[d, 0] = mag * jnp.cos((L * e) * b)
        apow_ref[d, 1] = mag * jnp.sin((L * e) * b)

        def power(k):
            return pw_ref[0, k:k + 1, :], pw_ref[1, k:k + 1, :]

        def c_block(k):
            pr, pi = power(k)
            return cre * pr - cim * pi, -(cre * pi + cim * pr)

        for i in range(L):
            rows = pl.ds(pl.multiple_of(((i + p) % L) * S5_CH, S5_CH), S5_CH)
            pr, pi = power(L - 1 - i if d == 0 else i)
            wend_ref[rows, d * 256:d * 256 + 128] = (bqr * pr - bqi * pi).astype(BF16)
            wend_ref[rows, d * 256 + 128:d * 256 + 256] = (bqr * pi + bqi * pr).astype(BF16)
            mre, mim = c_block(i + 1 if d == 0 else L - i)
            woutT_ref[rows, d * 256:d * 256 + 128] = mre.astype(BF16)
            woutT_ref[rows, d * 256 + 128:d * 256 + 256] = mim.astype(BF16)
            lag_rows = slice(i * S5_CH, (i + 1) * S5_CH)
            mre, mim = c_block(i if d == 0 else L - 1 - i)
            pct_ref[lag_rows, 0:128] = mre
            pct_ref[lag_rows, 128:256] = mim

        bqcat = jnp.concatenate([bqr, bqi], axis=1)
        krows.append(_dot_nt(bqcat, pct_ref[...], precision=lax.Precision.HIGHEST))

    for i in range(L):
        rows = pl.ds(pl.multiple_of(((i + p) % L) * S5_CH, S5_CH), S5_CH)
        blk_f = pltpu.roll(krows[0], ((i + p) % L) * S5_CH, 1)
        blk_b = pltpu.roll(krows[1], ((i + 1 + p) % L) * S5_CH, 1)
        blk = jnp.where(src_blk >= i, blk_f, 0.0) + jnp.where(src_blk <= i, blk_b, 0.0)
        ktoe_ref[rows, :] = blk.astype(BF16)


def _s5_prep(lam_re, lam_im, log_step, b_re, b_im, c_re, c_im):
    def pair_vec(a):
        return a.astype(F32).reshape(2, S5_PAIRS, 1, S5_P).transpose(1, 0, 2, 3)

    step = jnp.broadcast_to(jnp.exp(log_step.astype(F32))[:, :, None], (2, N_SSM_GROUPS, SSM_STATE))
    vec = jnp.concatenate([pair_vec(lam_re), pair_vec(lam_im), pair_vec(step),
                           jnp.zeros((S5_PAIRS, 2, 5, S5_P), F32)], axis=2)

    def block_diag_ch_by_state(m):
        m = m.astype(F32).reshape(2, S5_PAIRS, 2, SSM_GROUP, SSM_STATE)
        eye = jnp.eye(2, dtype=F32)
        out = m[:, :, :, :, None, :] * eye[None, None, :, None, :, None]
        return out.reshape(2, S5_PAIRS, S5_CH, S5_P).transpose(1, 0, 2, 3)

    bt = jnp.stack([block_diag_ch_by_state(b_re.transpose(0, 1, 3, 2)),
                    block_diag_ch_by_state(b_im.transpose(0, 1, 3, 2))], axis=2)
    cc = jnp.stack([block_diag_ch_by_state(c_re), block_diag_ch_by_state(c_im)], axis=2)

    mat = jax.ShapeDtypeStruct((S5_PAIRS, S5_W, S5_W), BF16)
    mat_spec = pl.BlockSpec((None, S5_W, S5_W), lambda p: (p, 0, 0))
    return pl.pallas_call(
        _s5_prep_body,
        grid=(S5_PAIRS,),
        in_specs=[
            pl.BlockSpec((None, 2, 8, S5_P), lambda p: (p, 0, 0, 0)),
            pl.BlockSpec((None, 2, 2, S5_CH, S5_P), lambda p: (p, 0, 0, 0, 0)),
            pl.BlockSpec((None, 2, 2, S5_CH, S5_P), lambda p: (p, 0, 0, 0, 0)),
        ],
        out_specs=[mat_spec, mat_spec, mat_spec,
                   pl.BlockSpec((None, 2, 2, N_POW, S5_P), lambda p: (p, 0, 0, 0, 0))],
        out_shape=[mat, mat, mat, jax.ShapeDtypeStruct((S5_PAIRS, 2, 2, N_POW, S5_P), F32)],
        scratch_shapes=[pltpu.VMEM((2, N_POW, S5_P), F32), pltpu.VMEM((S5_W, 2 * S5_P), F32)],
        compiler_params=_cparams(("parallel",)),
        name="s5_prep",
    )(vec, bt, cc)


S5_SEG_ROWS = 256
S5_N_SEG = N_CHUNK_ROWS // S5_SEG_ROWS
S5_SUB = S5_SEG_ROWS // S5_CHUNK
S5_SUB_PER_SEQ = (DEC_SEQ // S5_CHUNK) // S5_CHUNK


def _s5_body(tu_ref, s0_ref, wend_ref, ktoe_ref, woutT_ref, apow_ref, ty_ref, fin_ref, z_ref, xs_ref, g_ref):
    seg = pl.program_id(0)
    p = pl.program_id(1)
    L = S5_CHUNK
    lane_blk = lax.broadcasted_iota(jnp.int32, (S5_SEG_ROWS, 128), 1) // S5_CH

    tiles = []
    for t in range(LANE_TILES):
        tile = None
        for k in range(128 // S5_CH):
            j = (LANE_TILES * t + k + L - p) % L
            blk = tu_ref[j, t]
            tile = blk if tile is None else jnp.where(lane_blk == k, blk, tile)
        tiles.append(tile)
    s = jnp.concatenate(tiles, axis=1)
    z = _dot(s, wend_ref[...])
    for q in range(LANE_TILES):
        z_ref[q] = z[:, q * 128:(q + 1) * 128]

    ends = []
    for d in range(2):
        are, aim = apow_ref[d, 0, 1:2, :], apow_ref[d, 1, 1:2, :]
        xr = jnp.zeros((S5_SUB, S5_P), F32)
        xi = jnp.zeros((S5_SUB, S5_P), F32)
        for c in (range(L) if d == 0 else reversed(range(L))):
            rows = pl.ds(c, S5_SUB, stride=L)
            xs_ref[2 * d, rows, :] = xr
            xs_ref[2 * d + 1, rows, :] = xi
            zr = z_ref[2 * d, rows, :]
            zi = z_ref[2 * d + 1, rows, :]
            xr, xi = are * xr - aim * xi + zr, are * xi + aim * xr + zi
        ends.append((xr, xi))
        fin_ref[d, 0] = xr
        fin_ref[d, 1] = xi

    @pl.when(seg > 0)
    def _():
        for d in range(2):
            er, ei = ends[d]
            a16r, a16i = apow_ref[d, 0, L:L + 1, :], apow_ref[d, 1, L:L + 1, :]
            for bb in range(S5_SUB // S5_SUB_PER_SEQ):
                gr = s0_ref[bb, 2 * d:2 * d + 1, :]
                gi = s0_ref[bb, 2 * d + 1:2 * d + 2, :]
                order = range(S5_SUB_PER_SEQ) if d == 0 else reversed(range(S5_SUB_PER_SEQ))
                prev = None
                for sub in order:
                    row = bb * S5_SUB_PER_SEQ + sub
                    if prev is not None:
                        gr, gi = (a16r * gr - a16i * gi + er[prev:prev + 1, :],
                                  a16r * gi + a16i * gr + ei[prev:prev + 1, :])
                    g_ref[0, row:row + 1, :] = gr
                    g_ref[1, row:row + 1, :] = gi
                    prev = row
            gr, gi = g_ref[0], g_ref[1]
            for c in range(L):
                k = c if d == 0 else L - 1 - c
                pr, pi = apow_ref[d, 0, k:k + 1, :], apow_ref[d, 1, k:k + 1, :]
                rows = pl.ds(c, S5_SUB, stride=L)
                xs_ref[2 * d, rows, :] = xs_ref[2 * d, rows, :] + (pr * gr - pi * gi)
                xs_ref[2 * d + 1, rows, :] = xs_ref[2 * d + 1, rows, :] + (pr * gi + pi * gr)

    xs = jnp.concatenate([xs_ref[q].astype(BF16) for q in range(LANE_TILES)], axis=1)
    y = _dot(s, ktoe_ref[...]) + _dot_nt(xs, woutT_ref[...])

    @pl.when(p == 0)
    def _():
        ty_ref[...] = jnp.zeros_like(ty_ref)

    for t in range(LANE_TILES):
        yt = y[:, t * 128:(t + 1) * 128]
        for k in range(128 // S5_CH):
            j = (LANE_TILES * t + k + L - p) % L
            ty_ref[j, t] = jnp.where(lane_blk == k, yt, ty_ref[j, t])


def _s5_scan(tu, s0, wend, ktoe, woutT, apow):
    mat_spec = pl.BlockSpec((None, S5_W, S5_W), lambda g, p: (p, 0, 0))
    seqs = S5_SUB // S5_SUB_PER_SEQ
    return pl.pallas_call(
        _s5_body,
        grid=(S5_N_SEG, S5_PAIRS),
        in_specs=[
            pl.BlockSpec((S5_CHUNK, LANE_TILES, S5_SEG_ROWS, 128), lambda g, p: (0, 0, g, 0)),
            pl.BlockSpec((None, seqs, 4, S5_P), lambda g, p: (p, jnp.maximum(g - 1, 0), 0, 0)),
            mat_spec, mat_spec, mat_spec,
            pl.BlockSpec((None, 2, 2, N_POW, S5_P), lambda g, p: (p, 0, 0, 0, 0)),
        ],
        out_specs=[
            pl.BlockSpec((S5_CHUNK, LANE_TILES, S5_SEG_ROWS, 128), lambda g, p: (0, 0, g, 0)),
            pl.BlockSpec((None, 2, 2, S5_SUB, S5_P), lambda g, p: (p, 0, 0, g, 0)),
        ],
        out_shape=[
            jax.ShapeDtypeStruct((S5_CHUNK, LANE_TILES, N_CHUNK_ROWS, 128), F32),
            jax.ShapeDtypeStruct((S5_PAIRS, 2, 2, S5_N_SEG * S5_SUB, S5_P), F32),
        ],
        scratch_shapes=[pltpu.VMEM((LANE_TILES, S5_SEG_ROWS, 128), F32),
                        pltpu.VMEM((LANE_TILES, S5_SEG_ROWS, 128), F32),
                        pltpu.VMEM((2, S5_SUB, S5_P), F32)],
        compiler_params=_cparams(("parallel", "arbitrary")),
        name="s5_scan",
    )(tu, s0, wend, ktoe, woutT, apow)


TM_GLU = 1024


def _glu_body(tu_ref, ty_ref, d_ref, gw_ref, gb_ref, o_ref, u_ref, ys_ref):
    rows = TM_GLU // S5_CHUNK
    for j in range(S5_CHUNK):
        back = ((S5_CHUNK - j) % S5_CHUNK) * S5_CH
        uj = jnp.concatenate([tu_ref[j, t].astype(F32) for t in range(LANE_TILES)], axis=1)
        yj = jnp.concatenate([ty_ref[j, t] for t in range(LANE_TILES)], axis=1)
        if back:
            uj = pltpu.roll(uj, back, 1)
            yj = pltpu.roll(yj, back, 1)
        for t in range(LANE_TILES):
            u_ref[t, pl.ds(j, rows, stride=S5_CHUNK), :] = uj[:, t * 128:(t + 1) * 128]
            ys_ref[t, pl.ds(j, rows, stride=S5_CHUNK), :] = yj[:, t * 128:(t + 1) * 128]
    u = jnp.concatenate([u_ref[t] for t in range(LANE_TILES)], axis=1)
    y = d_ref[...] * u + jnp.concatenate([ys_ref[t] for t in range(LANE_TILES)], axis=1)
    cdf = 0.5 * (1.0 + jnp.tanh(math.sqrt(2.0 / math.pi) * (y + 0.044715 * (y * y * y))))
    y = y * cdf
    gate = jax.nn.sigmoid(_dot(y.astype(BF16), gw_ref[...]) + gb_ref[...])
    o_ref[...] = (y * gate).astype(BF16)


def _glu(tu, ty, ssm_d, glu_w, glu_b):
    tm = TM_GLU
    rows = tm // S5_CHUNK
    lay_spec = pl.BlockSpec((S5_CHUNK, LANE_TILES, rows, 128), lambda i: (0, 0, i, 0))
    return pl.pallas_call(
        _glu_body,
        grid=(N_TOK // tm,),
        in_specs=[
            lay_spec, lay_spec,
            pl.BlockSpec((1, HALF), lambda i: (0, 0)),
            pl.BlockSpec((HALF, HALF), lambda i: (0, 0)),
            pl.BlockSpec((1, HALF), lambda i: (0, 0)),
        ],
        out_specs=pl.BlockSpec((tm, HALF), lambda i: (i, 0)),
        out_shape=jax.ShapeDtypeStruct((N_TOK, HALF), BF16),
        scratch_shapes=[pltpu.VMEM((LANE_TILES, tm, 128), F32), pltpu.VMEM((LANE_TILES, tm, 128), F32)],
        compiler_params=_cparams(("parallel",)),
        name="s5_glu",
    )(tu, ty, ssm_d.reshape(1, HALF), glu_w.astype(BF16), glu_b.reshape(1, HALF))


def kernel(x_prompt, x_sample, c, cache_k, cache_v, state_s5_re, state_s5_im, c_ctx, norm1_g, norm2_g, ada_w, ada_b, mlp_w1, mlp_w2, ab_w_in, pool_w, pool_scale, conv_w, ab_w_out, cd_w_in, na_rpb, ssm_lambda_re, ssm_lambda_im, ssm_log_step, ssm_b_re, ssm_b_im, ssm_c_re, ssm_c_im, ssm_d, glu_w, glu_b, cd_w_out, final_g):
    depth = ada_w.shape[0]
    xs = (x_prompt.reshape(N_PROMPT_TOK, D_MODEL), x_sample.reshape(N_SAMPLE_TOK, D_MODEL))
    cond = jnp.concatenate([c_ctx[None, :], c, jnp.zeros((N_COND - 1 - DEC_BATCH, D_MODEL), F32)], axis=0)
    mods = _adaln(cond, ada_w, ada_b).reshape(depth, N_COND, 6, D_MODEL)

    new_k, new_v, new_re, new_im = [], [], [], []
    for layer in range(depth):
        i = layer // 2
        final = layer == depth - 1
        if layer % 2 == 0:
            y, x = _even_layer_mix(xs, mods, layer, norm1_g[layer], ab_w_in[i].astype(BF16),
                                   pool_w[i], pool_scale[i], conv_w[i])
            ya, ya_col, yb, yb_col = y, 0, y, 1
            w_out = ab_w_out[i]
        else:
            x = xs[0]
            qkv, tu = _inproj_odd(x, mods, layer, norm1_g[layer], cd_w_in[i].astype(BF16))
            o_p, ck, cv = _ctx_attn(qkv)
            new_k.append(ck)
            new_v.append(cv)
            o_s = _na_attn(qkv, cache_k, cache_v, _na_col_table(na_rpb[i]), i)
            attn = (o_p, o_s)

            wend, ktoe, woutT, apow = _s5_prep(ssm_lambda_re[i], ssm_lambda_im[i], ssm_log_step[i],
                                               ssm_b_re[i], ssm_b_im[i], ssm_c_re[i], ssm_c_im[i])

            def pair_state(s):
                return s.astype(F32).reshape(DEC_BATCH, 2, S5_PAIRS, S5_P).transpose(2, 0, 1, 3)

            s0 = jnp.stack([pair_state(state_s5_re[:, i]), pair_state(state_s5_im[:, i])], axis=3)
            s0 = s0.reshape(S5_PAIRS, DEC_BATCH, 4, S5_P)
            ty, fin = _s5_scan(tu, s0, wend, ktoe, woutT, apow)
            fin = fin[:, :, :, :BATCH]
            new_re.append(fin[:, :, 0].transpose(2, 1, 0, 3).reshape(BATCH, 2, N_SSM_GROUPS, SSM_STATE))
            new_im.append(fin[:, :, 1].transpose(2, 1, 0, 3).reshape(BATCH, 2, N_SSM_GROUPS, SSM_STATE))
            d_out = _glu(tu, ty, ssm_d[i], glu_w[i], glu_b[i])
            ya, ya_col, yb, yb_col = attn, 0, d_out, 0
            w_out = cd_w_out[i]
        xs = _mlp(x, ya, ya_col, yb, yb_col, w_out.astype(BF16), mods, layer, norm2_g[layer], final_g,
                  mlp_w1[layer].astype(BF16), mlp_w2[layer].astype(BF16), final)

    y_prompt = xs[0].reshape(BATCH, SEQ, D_MODEL)
    y_sample = xs[1].reshape(DEC_BATCH, DEC_SEQ, D_MODEL)
    return (y_prompt, y_sample, jnp.concatenate(new_k, axis=1), jnp.concatenate(new_v, axis=1),
            jnp.stack(new_re, axis=1), jnp.stack(new_im, axis=1))
```

```python
import functools
import math

import jax
import jax.numpy as jnp
from jax import lax
from jax.experimental import pallas as pl
from jax.experimental.pallas import tpu as pltpu

F32 = jnp.float32
BF16 = jnp.bfloat16

D_MODEL = 1024
BATCH = 16
SEQ = 256
DEC_BATCH = 4
DEC_SEQ = 2048
PAST_LEN = 256
GRID_W = 64
GRID_ROWS = DEC_SEQ // GRID_W
HALF = 512
POOL_WINDOWS = (2, 4, 8, 16)
POOL_GROUP = 128
HEAD_DIM = 64
N_HEADS = 8
WIN_H = 8
WIN_W = 16
SSM_GROUP = 16
N_SSM_GROUPS = 32
SSM_STATE = 64
D_FF = 4096
EPS = 1e-6

N_PROMPT_TOK = BATCH * SEQ
N_SAMPLE_TOK = DEC_BATCH * DEC_SEQ
N_TOK = N_PROMPT_TOK + N_SAMPLE_TOK
N_COND = 8

S5_CHUNK = 16
S5_PAIRS = N_SSM_GROUPS // 2
S5_CH = 2 * SSM_GROUP
S5_P = 2 * SSM_STATE
S5_W = S5_CHUNK * S5_CH
S5_BP_SAMPLE = 8

VMEM_LIMIT = 56 * 1024 * 1024


def _cparams(sem):
    return pltpu.CompilerParams(dimension_semantics=sem, vmem_limit_bytes=VMEM_LIMIT)


def _cond_row(i, tm):
    npt = N_PROMPT_TOK // tm
    per = DEC_SEQ // tm
    return jnp.where(i < npt, 0, 1 + (i - npt) // per)


def _normmod(x, g, shift, scale):
    ms = jnp.mean(x * x, axis=-1, keepdims=True)
    return (x * lax.rsqrt(ms + EPS) * g) * (1.0 + scale) + shift


def _dot(a, b):
    return jnp.dot(a, b, preferred_element_type=F32)


def _dot_nt(a, b, precision=None):
    return lax.dot_general(a, b, (((1,), (1,)), ((), ())), preferred_element_type=F32, precision=precision)


CAST_BLOCK_BYTES = 4 * 1024 * 1024


def _cast_body(w_ref, o_ref):
    o_ref[...] = w_ref[...].astype(BF16)


def _to_bf16(w_stack, idx):
    _, rows, cols = w_stack.shape
    tr = max(8, min(rows, CAST_BLOCK_BYTES // (4 * cols)))
    assert rows % tr == 0
    return pl.pallas_call(
        _cast_body,
        grid=(rows // tr,),
        in_specs=[pl.BlockSpec((None, tr, cols), lambda i: (idx, i, 0))],
        out_specs=pl.BlockSpec((tr, cols), lambda i: (i, 0)),
        out_shape=jax.ShapeDtypeStruct((rows, cols), BF16),
        compiler_params=_cparams(("parallel",)),
        name="to_bf16",
    )(w_stack)


def _adaln_body(c_ref, w_ref, b_ref, o_ref):
    c = c_ref[...]
    s = c * jax.nn.sigmoid(c)
    o_ref[...] = _dot(s.astype(BF16), w_ref[...].astype(BF16)) + b_ref[...]


def _adaln(cond, ada_w, ada_b):
    depth = ada_w.shape[0]
    nj = 6
    return pl.pallas_call(
        _adaln_body,
        grid=(depth, nj),
        in_specs=[
            pl.BlockSpec((N_COND, D_MODEL), lambda l, j: (0, 0)),
            pl.BlockSpec((None, D_MODEL, D_MODEL), lambda l, j: (l, 0, j)),
            pl.BlockSpec((None, 1, D_MODEL), lambda l, j: (l, 0, j)),
        ],
        out_specs=pl.BlockSpec((None, N_COND, D_MODEL), lambda l, j: (l, 0, j)),
        out_shape=jax.ShapeDtypeStruct((depth, N_COND, 6 * D_MODEL), F32),
        compiler_params=_cparams(("arbitrary", "arbitrary")),
        name="adaln",
    )(cond, ada_w, ada_b.reshape(depth, 1, 6 * D_MODEL))


TM_PROJ = 512


def _stream_tile(x_refs, tm):
    if len(x_refs) == 1:
        return x_refs[0][...]
    is_prompt = pl.program_id(0) < N_PROMPT_TOK // tm
    return jnp.where(is_prompt, x_refs[0][...], x_refs[1][...])


N_CHUNK_ROWS = N_TOK // S5_CHUNK
LANE_TILES = S5_W // 128


def _inproj_odd_body(x_ref, m_ref, g_ref, w_ref, qkv_ref, tu_ref, u_ref):
    m = m_ref[...]
    h = _normmod(x_ref[...], g_ref[...], m[0:1], m[1:2])
    acc = _dot(h.astype(BF16), w_ref[...])
    qkv_ref[...] = acc[:, 0:3 * HALF].astype(BF16)
    for t in range(LANE_TILES):
        u_ref[t] = acc[:, 3 * HALF + t * 128:3 * HALF + (t + 1) * 128]
    rows = TM_PROJ // S5_CHUNK
    for j in range(S5_CHUNK):
        uj = jnp.concatenate([u_ref[t, pl.ds(j, rows, stride=S5_CHUNK), :] for t in range(LANE_TILES)], axis=1)
        if j:
            uj = pltpu.roll(uj, j * S5_CH, 1)
        for t in range(LANE_TILES):
            tu_ref[j, t] = uj[:, t * 128:(t + 1) * 128].astype(BF16)


def _inproj_odd(x, mods, layer, g, w):
    tm = TM_PROJ
    rows = tm // S5_CHUNK
    return pl.pallas_call(
        _inproj_odd_body,
        grid=(N_TOK // tm,),
        in_specs=[
            pl.BlockSpec((tm, D_MODEL), lambda i: (i, 0)),
            pl.BlockSpec((None, None, 6, D_MODEL), lambda i: (layer, _cond_row(i, tm), 0, 0)),
            pl.BlockSpec((1, D_MODEL), lambda i: (0, 0)),
            pl.BlockSpec((D_MODEL, 4 * HALF), lambda i: (0, 0)),
        ],
        out_specs=[
            pl.BlockSpec((tm, 3 * HALF), lambda i: (i, 0)),
            pl.BlockSpec((S5_CHUNK, LANE_TILES, rows, 128), lambda i: (0, 0, i, 0)),
        ],
        out_shape=[
            jax.ShapeDtypeStruct((N_TOK, 3 * HALF), BF16),
            jax.ShapeDtypeStruct((S5_CHUNK, LANE_TILES, N_CHUNK_ROWS, 128), BF16),
        ],
        scratch_shapes=[pltpu.VMEM((LANE_TILES, tm, 128), F32)],
        compiler_params=_cparams(("parallel",)),
        name="inproj_odd",
    )(x, mods, g.reshape(1, D_MODEL), w)


TM_MIX = 256
HALO = 16
RING = 3
LAG = 2


def _even_step(phase, x_refs, m_ref, g_ref, w_ref, pw_ref, ps_ref, cw_ref, y_ref, xcat_ref, proj_ref, tail_ref,
               pad_ref, h_ref):
    i = pl.program_id(0)
    write_slot, cur_slot, next_slot = phase, (phase + 1) % RING, (phase + 2) % RING

    m = m_ref[...]
    x = _stream_tile(x_refs, TM_MIX)
    xcat_ref[...] = x
    h_ref[...] = _normmod(x, g_ref[...], m[0:1], m[1:2]).astype(BF16)
    n_pieces = len(POOL_WINDOWS) + HALF // 128
    pw = (4 * HALF) // n_pieces

    def project(piece):
        cols = slice(piece * pw, (piece + 1) * pw)
        proj_ref[write_slot, :, cols] = _dot(h_ref[...], w_ref[:, cols])

    t = jnp.maximum(i - LAG, 0)
    n_prompt_tiles = N_PROMPT_TOK // TM_MIX
    tiles_per_seq = DEC_SEQ // TM_MIX
    is_prompt = t < n_prompt_tiles
    tile_in_seq = jnp.where(is_prompt, 0, (t - n_prompt_tiles) % tiles_per_seq)
    seq_len = jnp.where(is_prompt, SEQ, DEC_SEQ)
    has_prev = tile_in_seq > 0
    has_next = jnp.logical_and(jnp.logical_not(is_prompt), tile_in_seq < tiles_per_seq - 1)
    prev_keep = jnp.where(has_prev, 1.0, 0.0).astype(F32)
    next_keep = jnp.where(has_next, 1.0, 0.0).astype(F32)
    pos = tile_in_seq * TM_MIX + lax.broadcasted_iota(jnp.int32, (TM_MIX, 1), 0)
    lo_rows, hi_rows = HALO, HALO + TM_MIX
    head = slice(0, HALO)

    for gi, w in enumerate(POOL_WINDOWS):
        project(gi)
        cs = slice(gi * POOL_GROUP, (gi + 1) * POOL_GROUP)
        pad_ref[0:HALO, cs] = tail_ref[:, cs] * prev_keep
        pad_ref[lo_rows:hi_rows, cs] = proj_ref[cur_slot, :, cs]
        pad_ref[hi_rows:hi_rows + HALO, cs] = proj_ref[next_slot, head, cs] * next_keep
        s = None
        for o in range(-(w // 2), w - w // 2):
            v = pad_ref[lo_rows + o:hi_rows + o, cs]
            s = v if s is None else s + v
        lo = jnp.maximum(pos - w // 2, 0)
        hi = jnp.minimum(pos + (w - w // 2), seq_len)
        inv = 1.0 / (hi - lo).astype(F32)
        pooled = s * inv - pad_ref[lo_rows:hi_rows, cs]
        mixed = _dot(pooled.astype(BF16), pw_ref[gi]) * ps_ref[:, cs]
        y_ref[:, cs] = mixed.astype(BF16)

    for ci in range(HALF // 128):
        project(len(POOL_WINDOWS) + ci)
        cs = slice(ci * 128, (ci + 1) * 128)
        bs = slice(HALF + ci * 128, HALF + (ci + 1) * 128)
        gs = slice(2 * HALF + ci * 128, 2 * HALF + (ci + 1) * 128)
        vs = slice(3 * HALF + ci * 128, 3 * HALF + (ci + 1) * 128)
        pad_ref[0:HALO, cs] = (tail_ref[:, gs] * tail_ref[:, vs]) * prev_keep
        pad_ref[lo_rows:hi_rows, cs] = proj_ref[cur_slot, :, gs] * proj_ref[cur_slot, :, vs]
        pad_ref[hi_rows:hi_rows + HALO, cs] = (proj_ref[next_slot, head, gs] * proj_ref[next_slot, head, vs]) * next_keep
        conv = (cw_ref[0:1, cs] * pad_ref[lo_rows - 1:hi_rows - 1, cs]
                + cw_ref[1:2, cs] * pad_ref[lo_rows:hi_rows, cs]
                + cw_ref[2:3, cs] * pad_ref[lo_rows + 1:hi_rows + 1, cs])
        y_ref[:, bs] = (proj_ref[cur_slot, :, bs] * conv).astype(BF16)

    tail_ref[...] = proj_ref[cur_slot, TM_MIX - HALO:TM_MIX, :]


def _even_body(n_x, *refs):
    x_refs, rest = refs[:n_x], refs[n_x:]
    proj_ref, tail_ref = rest[-4], rest[-3]
    i = pl.program_id(0)

    @pl.when(i == 0)
    def _():
        proj_ref[...] = jnp.zeros_like(proj_ref)
        tail_ref[...] = jnp.zeros_like(tail_ref)

    for phase in range(RING):
        @pl.when(i % RING == phase)
        def _(phase=phase):
            _even_step(phase, x_refs, *rest)


def _even_layer_mix(xs, mods, layer, g, w, pool_w, pool_scale, conv_w):
    tm = TM_MIX
    n_tiles = N_TOK // tm
    npt = N_PROMPT_TOK // tm
    last = n_tiles - 1
    if len(xs) == 1:
        x_specs = [pl.BlockSpec((tm, D_MODEL), lambda i: (jnp.minimum(i, last), 0))]
    else:
        x_specs = [pl.BlockSpec((tm, D_MODEL), lambda i: (jnp.minimum(i, npt - 1), 0)),
                   pl.BlockSpec((tm, D_MODEL), lambda i: (jnp.clip(i - npt, 0, last - npt), 0))]
    return pl.pallas_call(
        functools.partial(_even_body, len(xs)),
        grid=(n_tiles + LAG,),
        in_specs=x_specs + [
            pl.BlockSpec((None, None, 6, D_MODEL), lambda i: (layer, _cond_row(jnp.minimum(i, last), tm), 0, 0)),
            pl.BlockSpec((1, D_MODEL), lambda i: (0, 0)),
            pl.BlockSpec((D_MODEL, 4 * HALF), lambda i: (0, 0)),
            pl.BlockSpec((len(POOL_WINDOWS), POOL_GROUP, POOL_GROUP), lambda i: (0, 0, 0)),
            pl.BlockSpec((1, HALF), lambda i: (0, 0)),
            pl.BlockSpec((3, HALF), lambda i: (0, 0)),
        ],
        out_specs=[pl.BlockSpec((tm, 2 * HALF), lambda i: (jnp.maximum(i - LAG, 0), 0)),
                   pl.BlockSpec((tm, D_MODEL), lambda i: (jnp.minimum(i, last), 0))],
        out_shape=[jax.ShapeDtypeStruct((N_TOK, 2 * HALF), BF16), jax.ShapeDtypeStruct((N_TOK, D_MODEL), F32)],
        scratch_shapes=[pltpu.VMEM((RING, tm, 4 * HALF), F32), pltpu.VMEM((HALO, 4 * HALF), F32),
                        pltpu.VMEM((tm + 2 * HALO, HALF), F32), pltpu.VMEM((tm, D_MODEL), BF16)],
        compiler_params=_cparams(("arbitrary",)),
        name="even_layer_mix",
    )(*xs, mods, g.reshape(1, D_MODEL), w, pool_w.astype(BF16), pool_scale.reshape(1, HALF), conv_w)


TM_MLP = 1024
RC_MLP = 256


def _mlp_body(final, x_ref, ya_ref, yb_ref, wo_ref, m_ref, g_ref, fg_ref, w1_ref, w2_ref, o_ref):
    m = m_ref[...]
    gain = g_ref[...] * (1.0 + m[4:5])
    for r in range(TM_MLP // RC_MLP):
        rows = slice(r * RC_MLP, (r + 1) * RC_MLP)
        mix = _dot(ya_ref[rows, :], wo_ref[0:HALF, :]) + _dot(yb_ref[rows, :], wo_ref[HALF:2 * HALF, :])
        x1 = x_ref[rows, :] + m[2:3] * mix
        ms = jnp.mean(x1 * x1, axis=-1, keepdims=True)
        h = (x1 * lax.rsqrt(ms + EPS) * gain + m[3:4]).astype(BF16)
        t = _dot(h, w1_ref[...])
        t = jnp.square(jnp.maximum(t, 0.0)).astype(BF16)
        out = x1 + m[5:6] * _dot(t, w2_ref[...])
        if final:
            ms = jnp.mean(out * out, axis=-1, keepdims=True)
            out = out * lax.rsqrt(ms + EPS) * fg_ref[...]
        o_ref[rows, :] = out


def _resident(shape):
    zeros = (0,) * len(shape)
    return pl.BlockSpec(shape, lambda i: zeros, pipeline_mode=pl.Buffered(1))


def _mlp_call(x, x_off, ya, ya_off, ya_col, yb, yb_off, yb_col, n_tiles, tile0, w_out, mods, layer, g2, final_g,
              w1, w2, final):
    tm = TM_MLP
    return pl.pallas_call(
        functools.partial(_mlp_body, final),
        grid=(n_tiles,),
        in_specs=[
            pl.BlockSpec((tm, D_MODEL), lambda i: (i + x_off, 0)),
            pl.BlockSpec((tm, HALF), lambda i: (i + ya_off, ya_col)),
            pl.BlockSpec((tm, HALF), lambda i: (i + yb_off, yb_col)),
            _resident((D_MODEL, D_MODEL)),
            pl.BlockSpec((None, None, 6, D_MODEL), lambda i: (layer, _cond_row(i + tile0, tm), 0, 0)),
            _resident((1, D_MODEL)),
            _resident((1, D_MODEL)),
            _resident((D_MODEL, D_FF)),
            _resident((D_FF, D_MODEL)),
        ],
        out_specs=pl.BlockSpec((tm, D_MODEL), lambda i: (i, 0)),
        out_shape=jax.ShapeDtypeStruct((n_tiles * tm, D_MODEL), F32),
        compiler_params=_cparams(("parallel",)),
        name="mlp",
    )(x, ya, yb, w_out, mods, g2.reshape(1, D_MODEL), final_g.reshape(1, D_MODEL), w1, w2)


def _mlp(x, ya, ya_col, yb, yb_col, w_out, mods, layer, g2, final_g, w1, w2, final):
    tm = TM_MLP
    npt = N_PROMPT_TOK // tm
    weights = (w_out, mods, layer, g2, final_g, w1, w2, final)
    if not final:
        if isinstance(ya, tuple):
            ya = jnp.concatenate(ya, axis=0)
        return (_mlp_call(x, 0, ya, 0, ya_col, yb, 0, yb_col, N_TOK // tm, 0, *weights),)
    outs = []
    for s, (tile0, n_tiles) in enumerate(((0, npt), (npt, N_TOK // tm - npt))):
        a, a_off = (ya[s], 0) if isinstance(ya, tuple) else (ya, tile0)
        outs.append(_mlp_call(x, tile0, a, a_off, ya_col, yb, tile0, yb_col, n_tiles, tile0, *weights))
    return tuple(outs)


def _ctx_attn_body(q_ref, k_ref, v_ref, o_ref, ck_ref, cv_ref):
    lane = lax.broadcasted_iota(jnp.int32, (1, 2 * HEAD_DIM), 1)
    own = [lane < HEAD_DIM, lane >= HEAD_DIM]
    for h in range(N_HEADS):
        sl = slice(h * HEAD_DIM, (h + 1) * HEAD_DIM)
        ck_ref[h] = k_ref[:, sl].astype(F32)
        cv_ref[h] = v_ref[:, sl].astype(F32)
    for hp in range(N_HEADS // 2):
        sl = slice(hp * 2 * HEAD_DIM, (hp + 1) * 2 * HEAD_DIM)
        q = q_ref[:, sl] * jnp.asarray(HEAD_DIM ** -0.5, BF16)
        k = k_ref[:, sl]
        v = v_ref[:, sl]
        q2 = jnp.concatenate([jnp.where(own[0], q, jnp.zeros_like(q)),
                              jnp.where(own[1], q, jnp.zeros_like(q))], axis=0)
        s = _dot_nt(q2, k)
        m = jnp.max(s, axis=-1, keepdims=True)
        p = jnp.exp(s - m).astype(BF16)
        out = None
        for hh in range(2):
            num = _dot(p[hh * SEQ:(hh + 1) * SEQ], jnp.where(own[hh], v, jnp.ones_like(v)))
            den = jnp.where(own[hh], pltpu.roll(num, HEAD_DIM, 1), 1.0)
            o = num / den
            out = o if out is None else jnp.where(own[0], out, o)
        o_ref[:, sl] = out.astype(BF16)


def _ctx_attn(proj):
    return pl.pallas_call(
        _ctx_attn_body,
        grid=(BATCH,),
        in_specs=[
            pl.BlockSpec((SEQ, HALF), lambda b: (b, 0)),
            pl.BlockSpec((SEQ, HALF), lambda b: (b, 1)),
            pl.BlockSpec((SEQ, HALF), lambda b: (b, 2)),
        ],
        out_specs=[
            pl.BlockSpec((SEQ, HALF), lambda b: (b, 0)),
            pl.BlockSpec((None, None, N_HEADS, SEQ, HEAD_DIM), lambda b: (b, 0, 0, 0, 0)),
            pl.BlockSpec((None, None, N_HEADS, SEQ, HEAD_DIM), lambda b: (b, 0, 0, 0, 0)),
        ],
        out_shape=[
            jax.ShapeDtypeStruct((N_PROMPT_TOK, HALF), BF16),
            jax.ShapeDtypeStruct((BATCH, 1, N_HEADS, SEQ, HEAD_DIM), F32),
            jax.ShapeDtypeStruct((BATCH, 1, N_HEADS, SEQ, HEAD_DIM), F32),
        ],
        compiler_params=_cparams(("parallel",)),
        name="ctx_attn",
    )(proj, proj, proj)


NA_MASKED = -1e30
NA_QROWS = 4
NA_KROWS = NA_QROWS + WIN_H
NA_NQ = NA_QROWS * GRID_W
NA_NK = NA_KROWS * GRID_W
NA_BLOCKS = GRID_ROWS // NA_QROWS
NA_TYPES = 3


def _na_col_table(rpb):
    qc = jnp.arange(GRID_W)[:, None]
    kc = jnp.arange(GRID_W)[None, :]
    c0 = jnp.clip(qc - WIN_W // 2, 0, GRID_W - WIN_W)
    valid = (kc >= c0) & (kc < c0 + WIN_W)
    coff = jnp.clip(kc - qc, -(WIN_W - 1), WIN_W - 1) + (WIN_W - 1)
    onehot = (coff.reshape(1, -1) == jnp.arange(2 * WIN_W - 1)[:, None]).astype(F32)
    t = jnp.dot(rpb.astype(F32).reshape(N_HEADS * (2 * WIN_H - 1), 2 * WIN_W - 1), onehot,
                precision=lax.Precision.HIGHEST)
    t = t.reshape(N_HEADS, 2 * WIN_H - 1, GRID_W, GRID_W)
    return jnp.where(valid[None, None], t, NA_MASKED)


def _na_window_start(rb):
    return jnp.clip(NA_QROWS * rb - WIN_H // 2, 0, GRID_ROWS - NA_KROWS)


def _na_body(q_ref, k_ref, v_ref, ck_ref, cv_ref, tab_ref, o_ref, bias_ref, kc_ref, vc_ref, va_ref):
    @pl.when(pl.program_id(1) == 0)
    def _():
        masked = jnp.full((GRID_W, GRID_W), NA_MASKED, F32)
        for hh in range(2):
            for ty in range(NA_TYPES):
                q0 = (0, NA_QROWS, GRID_ROWS - NA_QROWS)[ty]
                ks = (0, 0, GRID_ROWS - NA_KROWS)[ty]
                for a in range(NA_QROWS):
                    r0 = min(max(q0 + a - WIN_H // 2, 0), GRID_ROWS - WIN_H)
                    for y in range(NA_KROWS):
                        kr = ks + y
                        ok = r0 <= kr < r0 + WIN_H
                        blk = tab_ref[hh, kr - (q0 + a) + WIN_H - 1] if ok else masked
                        bias_ref[ty, hh * NA_NQ + a * GRID_W:hh * NA_NQ + (a + 1) * GRID_W,
                                 y * GRID_W:(y + 1) * GRID_W] = blk

    lane = lax.broadcasted_iota(jnp.int32, (1, 2 * HEAD_DIM), 1)
    own = [lane < HEAD_DIM, lane >= HEAD_DIM]
    kc_ref[...] = jnp.concatenate([ck_ref[0], ck_ref[1]], axis=1).astype(BF16)
    vc = jnp.concatenate([cv_ref[0], cv_ref[1]], axis=1)
    v = v_ref[...]
    for hh in range(2):
        vc_ref[hh] = jnp.where(own[hh], vc, 1.0).astype(BF16)
        va_ref[hh] = jnp.where(own[hh], v, jnp.ones_like(v))

    def block(rb, ty):
        qs = pl.multiple_of(rb * NA_NQ, NA_NQ)
        ks = pl.multiple_of(_na_window_start(rb) * GRID_W, GRID_W)
        q = q_ref[pl.ds(qs, NA_NQ), :] * jnp.asarray(HEAD_DIM ** -0.5, BF16)
        q2 = jnp.concatenate([jnp.where(own[0], q, jnp.zeros_like(q)),
                              jnp.where(own[1], q, jnp.zeros_like(q))], axis=0)
        s_loc = _dot_nt(q2, k_ref[pl.ds(ks, NA_NK), :]) + bias_ref[ty]
        s_ctx = _dot_nt(q2, kc_ref[...])
        m = jnp.maximum(jnp.max(s_loc, axis=-1, keepdims=True), jnp.max(s_ctx, axis=-1, keepdims=True))
        p_loc = jnp.exp(s_loc - m).astype(BF16)
        p_ctx = jnp.exp(s_ctx - m).astype(BF16)
        out = None
        for hh in range(2):
            rows = slice(hh * NA_NQ, (hh + 1) * NA_NQ)
            num = _dot(p_loc[rows], va_ref[hh, pl.ds(ks, NA_NK), :]) + _dot(p_ctx[rows], vc_ref[hh])
            den = jnp.where(own[hh], pltpu.roll(num, HEAD_DIM, 1), 1.0)
            o = num / den
            out = o if out is None else jnp.where(own[0], out, o)
        o_ref[pl.ds(qs, NA_NQ), :] = out.astype(BF16)

    block(0, 0)

    def interior(rb, carry):
        block(rb, 1)
        return carry

    lax.fori_loop(1, NA_BLOCKS - 1, interior, 0, unroll=3)
    block(NA_BLOCKS - 1, 2)


def _na_attn(proj, cache_k, cache_v, col_table, layer_idx):
    row_blk0 = N_PROMPT_TOK // DEC_SEQ
    hp = N_HEADS // 2
    return pl.pallas_call(
        _na_body,
        grid=(hp, DEC_BATCH),
        in_specs=[
            pl.BlockSpec((DEC_SEQ, 128), lambda h, b: (row_blk0 + b, h)),
            pl.BlockSpec((DEC_SEQ, 128), lambda h, b: (row_blk0 + b, hp + h)),
            pl.BlockSpec((DEC_SEQ, 128), lambda h, b: (row_blk0 + b, 2 * hp + h)),
            pl.BlockSpec((None, None, 2, PAST_LEN, HEAD_DIM), lambda h, b: (b, layer_idx, h, 0, 0)),
            pl.BlockSpec((None, None, 2, PAST_LEN, HEAD_DIM), lambda h, b: (b, layer_idx, h, 0, 0)),
            pl.BlockSpec((2, 2 * WIN_H - 1, GRID_W, GRID_W), lambda h, b: (h, 0, 0, 0)),
        ],
        out_specs=pl.BlockSpec((DEC_SEQ, 128), lambda h, b: (b, h)),
        out_shape=jax.ShapeDtypeStruct((N_SAMPLE_TOK, HALF), BF16),
        scratch_shapes=[pltpu.VMEM((NA_TYPES, 2 * NA_NQ, NA_NK), F32),
                        pltpu.VMEM((PAST_LEN, 2 * HEAD_DIM), BF16),
                        pltpu.VMEM((2, PAST_LEN, 2 * HEAD_DIM), BF16),
                        pltpu.VMEM((2, DEC_SEQ, 2 * HEAD_DIM), BF16)],
        compiler_params=_cparams(("parallel", "arbitrary")),
        name="na_attn",
    )(proj, proj, proj, cache_k, cache_v, col_table)


N_POW = 24


def _s5_prep_body(vec_ref, bt_ref, c_ref, wend_ref, ktoe_ref, woutT_ref, apow_ref, pw_ref, pct_ref):
    L = S5_CHUNK
    p = pl.program_id(0)
    lane_blk = lax.broadcasted_iota(jnp.int32, (S5_CH, S5_W), 1) // S5_CH
    src_blk = (lane_blk + L - p) % L
    krows = []
    for d in range(2):
        v = vec_ref[d]
        lre, lim, step = v[0:1], v[1:2], v[2:3]
        a = lre * step
        b = lim * step
        ea = jnp.exp(a)
        nr = ea * jnp.cos(b) - 1.0
        ni = ea * jnp.sin(b)
        den = lre * lre + lim * lim
        qr = (nr * lre + ni * lim) / den
        qi = (ni * lre - nr * lim) / den
        bre, bim = bt_ref[d, 0], bt_ref[d, 1]
        bqr = bre * qr - bim * qi
        bqi = bre * qi + bim * qr
        cre, cim = c_ref[d, 0], c_ref[d, 1]

        e = lax.broadcasted_iota(jnp.int32, (N_POW, 1), 0).astype(F32)
        mag = jnp.exp(e * a)
        pw_ref[0] = mag * jnp.cos(e * b)
        pw_ref[1] = mag * jnp.sin(e * b)
        mag = jnp.exp((L * e) * a)
        apow_ref[d, 0] = mag * jnp.cos((L * e) * b)
        apow_ref[d, 1] = mag * jnp.sin((L * e) * b)

        def power(k):
            return pw_ref[0, k:k + 1, :], pw_ref[1, k:k + 1, :]

        def c_block(k):
            pr, pi = power(k)
            return cre * pr - cim * pi, -(cre * pi + cim * pr)

        for i in range(L):
            rows = pl.ds(pl.multiple_of(((i + p) % L) * S5_CH, S5_CH), S5_CH)
            pr, pi = power(L - 1 - i if d == 0 else i)
            wend_ref[rows, d * 256:d * 256 + 128] = (bqr * pr - bqi * pi).astype(BF16)
            wend_ref[rows, d * 256 + 128:d * 256 + 256] = (bqr * pi + bqi * pr).astype(BF16)
            mre, mim = c_block(i + 1 if d == 0 else L - i)
            woutT_ref[rows, d * 256:d * 256 + 128] = mre.astype(BF16)
            woutT_ref[rows, d * 256 + 128:d * 256 + 256] = mim.astype(BF16)
            lag_rows = slice(i * S5_CH, (i + 1) * S5_CH)
            mre, mim = c_block(i if d == 0 else L - 1 - i)
            pct_ref[lag_rows, 0:128] = mre
            pct_ref[lag_rows, 128:256] = mim

        bqcat = jnp.concatenate([bqr, bqi], axis=1)
        krows.append(_dot_nt(bqcat, pct_ref[...], precision=lax.Precision.HIGHEST))

    for i in range(L):
        rows = pl.ds(pl.multiple_of(((i + p) % L) * S5_CH, S5_CH), S5_CH)
        blk_f = pltpu.roll(krows[0], ((i + p) % L) * S5_CH, 1)
        blk_b = pltpu.roll(krows[1], ((i + 1 + p) % L) * S5_CH, 1)
        blk = jnp.where(src_blk >= i, blk_f, 0.0) + jnp.where(src_blk <= i, blk_b, 0.0)
        ktoe_ref[rows, :] = blk.astype(BF16)


def _s5_prep(lam_re, lam_im, log_step, b_re, b_im, c_re, c_im):
    def pair_vec(a):
        return a.astype(F32).reshape(2, S5_PAIRS, 1, S5_P).transpose(1, 0, 2, 3)

    step = jnp.broadcast_to(jnp.exp(log_step.astype(F32))[:, :, None], (2, N_SSM_GROUPS, SSM_STATE))
    vec = jnp.concatenate([pair_vec(lam_re), pair_vec(lam_im), pair_vec(step),
                           jnp.zeros((S5_PAIRS, 2, 5, S5_P), F32)], axis=2)

    def block_diag_ch_by_state(m):
        m = m.astype(F32).reshape(2, S5_PAIRS, 2, SSM_GROUP, SSM_STATE)
        eye = jnp.eye(2, dtype=F32)
        out = m[:, :, :, :, None, :] * eye[None, None, :, None, :, None]
        return out.reshape(2, S5_PAIRS, S5_CH, S5_P).transpose(1, 0, 2, 3)

    bt = jnp.stack([block_diag_ch_by_state(b_re.transpose(0, 1, 3, 2)),
                    block_diag_ch_by_state(b_im.transpose(0, 1, 3, 2))], axis=2)
    cc = jnp.stack([block_diag_ch_by_state(c_re), block_diag_ch_by_state(c_im)], axis=2)

    mat = jax.ShapeDtypeStruct((S5_PAIRS, S5_W, S5_W), BF16)
    mat_spec = pl.BlockSpec((None, S5_W, S5_W), lambda p: (p, 0, 0))
    return pl.pallas_call(
        _s5_prep_body,
        grid=(S5_PAIRS,),
        in_specs=[
            pl.BlockSpec((None, 2, 8, S5_P), lambda p: (p, 0, 0, 0)),
            pl.BlockSpec((None, 2, 2, S5_CH, S5_P), lambda p: (p, 0, 0, 0, 0)),
            pl.BlockSpec((None, 2, 2, S5_CH, S5_P), lambda p: (p, 0, 0, 0, 0)),
        ],
        out_specs=[mat_spec, mat_spec, mat_spec,
                   pl.BlockSpec((None, 2, 2, N_POW, S5_P), lambda p: (p, 0, 0, 0, 0))],
        out_shape=[mat, mat, mat, jax.ShapeDtypeStruct((S5_PAIRS, 2, 2, N_POW, S5_P), F32)],
        scratch_shapes=[pltpu.VMEM((2, N_POW, S5_P), F32), pltpu.VMEM((S5_W, 2 * S5_P), F32)],
        compiler_params=_cparams(("parallel",)),
        name="s5_prep",
    )(vec, bt, cc)


S5_SEG_ROWS = 256
S5_N_SEG = N_CHUNK_ROWS // S5_SEG_ROWS
S5_SUB = S5_SEG_ROWS // S5_CHUNK
S5_SUB_PER_SEQ = (DEC_SEQ // S5_CHUNK) // S5_CHUNK


def _s5_body(tu_ref, s0_ref, wend_ref, ktoe_ref, woutT_ref, apow_ref, ty_ref, fin_ref, z_ref, xs_ref, g_ref):
    seg = pl.program_id(0)
    p = pl.program_id(1)
    L = S5_CHUNK
    lane_blk = lax.broadcasted_iota(jnp.int32, (S5_SEG_ROWS, 128), 1) // S5_CH

    tiles = []
    for t in range(LANE_TILES):
        tile = None
        for k in range(128 // S5_CH):
            j = (LANE_TILES * t + k + L - p) % L
            blk = tu_ref[j, t]
            tile = blk if tile is None else jnp.where(lane_blk == k, blk, tile)
        tiles.append(tile)
    s = jnp.concatenate(tiles, axis=1)
    z = _dot(s, wend_ref[...])
    for q in range(LANE_TILES):
        z_ref[q] = z[:, q * 128:(q + 1) * 128]

    ends = []
    for d in range(2):
        are, aim = apow_ref[d, 0, 1:2, :], apow_ref[d, 1, 1:2, :]
        xr = jnp.zeros((S5_SUB, S5_P), F32)
        xi = jnp.zeros((S5_SUB, S5_P), F32)
        for c in (range(L) if d == 0 else reversed(range(L))):
            rows = pl.ds(c, S5_SUB, stride=L)
            xs_ref[2 * d, rows, :] = xr
            xs_ref[2 * d + 1, rows, :] = xi
            zr = z_ref[2 * d, rows, :]
            zi = z_ref[2 * d + 1, rows, :]
            xr, xi = are * xr - aim * xi + zr, are * xi + aim * xr + zi
        ends.append((xr, xi))
        fin_ref[d, 0] = xr
        fin_ref[d, 1] = xi

    @pl.when(seg > 0)
    def _():
        for d in range(2):
            er, ei = ends[d]
            a16r, a16i = apow_ref[d, 0, L:L + 1, :], apow_ref[d, 1, L:L + 1, :]
            for bb in range(S5_SUB // S5_SUB_PER_SEQ):
                gr = s0_ref[bb, 2 * d:2 * d + 1, :]
                gi = s0_ref[bb, 2 * d + 1:2 * d + 2, :]
                order = range(S5_SUB_PER_SEQ) if d == 0 else reversed(range(S5_SUB_PER_SEQ))
                prev = None
                for sub in order:
                    row = bb * S5_SUB_PER_SEQ + sub
                    if prev is not None:
                        gr, gi = (a16r * gr - a16i * gi + er[prev:prev + 1, :],
                                  a16r * gi + a16i * gr + ei[prev:prev + 1, :])
                    g_ref[0, row:row + 1, :] = gr
                    g_ref[1, row:row + 1, :] = gi
                    prev = row
            if d == 0:
                pr, pi = apow_ref[d, 0, 0:L, :], apow_ref[d, 1, 0:L, :]
            else:
                pr = jnp.concatenate([apow_ref[d, 0, k:k + 1, :] for k in reversed(range(L))], axis=0)
                pi = jnp.concatenate([apow_ref[d, 1, k:k + 1, :] for k in reversed(range(L))], axis=0)
            for sub in range(S5_SUB):
                rows = slice(sub * L, (sub + 1) * L)
                gr = g_ref[0, sub:sub + 1, :]
                gi = g_ref[1, sub:sub + 1, :]
                xs_ref[2 * d, rows, :] = xs_ref[2 * d, rows, :] + (pr * gr - pi * gi)
                xs_ref[2 * d + 1, rows, :] = xs_ref[2 * d + 1, rows, :] + (pr * gi + pi * gr)

    xs = jnp.concatenate([xs_ref[q].astype(BF16) for q in range(LANE_TILES)], axis=1)
    y = _dot(s, ktoe_ref[...]) + _dot_nt(xs, woutT_ref[...])

    @pl.when(p == 0)
    def _():
        ty_ref[...] = jnp.zeros_like(ty_ref)

    for t in range(LANE_TILES):
        yt = y[:, t * 128:(t + 1) * 128]
        for k in range(128 // S5_CH):
            j = (LANE_TILES * t + k + L - p) % L
            ty_ref[j, t] = jnp.where(lane_blk == k, yt, ty_ref[j, t])


def _s5_scan(tu, s0, wend, ktoe, woutT, apow):
    mat_spec = pl.BlockSpec((None, S5_W, S5_W), lambda g, p: (p, 0, 0))
    seqs = S5_SUB // S5_SUB_PER_SEQ
    return pl.pallas_call(
        _s5_body,
        grid=(S5_N_SEG, S5_PAIRS),
        in_specs=[
            pl.BlockSpec((S5_CHUNK, LANE_TILES, S5_SEG_ROWS, 128), lambda g, p: (0, 0, g, 0)),
            pl.BlockSpec((None, seqs, 4, S5_P), lambda g, p: (p, jnp.maximum(g - 1, 0), 0, 0)),
            mat_spec, mat_spec, mat_spec,
            pl.BlockSpec((None, 2, 2, N_POW, S5_P), lambda g, p: (p, 0, 0, 0, 0)),
        ],
        out_specs=[
            pl.BlockSpec((S5_CHUNK, LANE_TILES, S5_SEG_ROWS, 128), lambda g, p: (0, 0, g, 0)),
            pl.BlockSpec((None, 2, 2, S5_SUB, S5_P), lambda g, p: (p, 0, 0, g, 0)),
        ],
        out_shape=[
            jax.ShapeDtypeStruct((S5_CHUNK, LANE_TILES, N_CHUNK_ROWS, 128), F32),
            jax.ShapeDtypeStruct((S5_PAIRS, 2, 2, S5_N_SEG * S5_SUB, S5_P), F32),
        ],
        scratch_shapes=[pltpu.VMEM((LANE_TILES, S5_SEG_ROWS, 128), F32),
                        pltpu.VMEM((LANE_TILES, S5_SEG_ROWS, 128), F32),
                        pltpu.VMEM((2, S5_SUB, S5_P), F32)],
        compiler_params=_cparams(("parallel", "arbitrary")),
        name="s5_scan",
    )(tu, s0, wend, ktoe, woutT, apow)


TM_GLU = 1024


def _glu_body(tu_ref, ty_ref, d_ref, gw_ref, gb_ref, o_ref, u_ref, ys_ref):
    rows = TM_GLU // S5_CHUNK
    for j in range(S5_CHUNK):
        back = ((S5_CHUNK - j) % S5_CHUNK) * S5_CH
        uj = jnp.concatenate([tu_ref[j, t].astype(F32) for t in range(LANE_TILES)], axis=1)
        yj = jnp.concatenate([ty_ref[j, t] for t in range(LANE_TILES)], axis=1)
        if back:
            uj = pltpu.roll(uj, back, 1)
            yj = pltpu.roll(yj, back, 1)
        for t in range(LANE_TILES):
            u_ref[t, pl.ds(j, rows, stride=S5_CHUNK), :] = uj[:, t * 128:(t + 1) * 128]
            ys_ref[t, pl.ds(j, rows, stride=S5_CHUNK), :] = yj[:, t * 128:(t + 1) * 128]
    u = jnp.concatenate([u_ref[t] for t in range(LANE_TILES)], axis=1)
    y = d_ref[...] * u + jnp.concatenate([ys_ref[t] for t in range(LANE_TILES)], axis=1)
    cdf = 0.5 * (1.0 + jnp.tanh(math.sqrt(2.0 / math.pi) * (y + 0.044715 * (y * y * y))))
    y = y * cdf
    gate = jax.nn.sigmoid(_dot(y.astype(BF16), gw_ref[...]) + gb_ref[...])
    o_ref[...] = (y * gate).astype(BF16)


def _glu(tu, ty, ssm_d, glu_w, glu_b):
    tm = TM_GLU
    rows = tm // S5_CHUNK
    lay_spec = pl.BlockSpec((S5_CHUNK, LANE_TILES, rows, 128), lambda i: (0, 0, i, 0))
    return pl.pallas_call(
        _glu_body,
        grid=(N_TOK // tm,),
        in_specs=[
            lay_spec, lay_spec,
            pl.BlockSpec((1, HALF), lambda i: (0, 0)),
            pl.BlockSpec((HALF, HALF), lambda i: (0, 0)),
            pl.BlockSpec((1, HALF), lambda i: (0, 0)),
        ],
        out_specs=pl.BlockSpec((tm, HALF), lambda i: (i, 0)),
        out_shape=jax.ShapeDtypeStruct((N_TOK, HALF), BF16),
        scratch_shapes=[pltpu.VMEM((LANE_TILES, tm, 128), F32), pltpu.VMEM((LANE_TILES, tm, 128), F32)],
        compiler_params=_cparams(("parallel",)),
        name="s5_glu",
    )(tu, ty, ssm_d.reshape(1, HALF), glu_w.astype(BF16), glu_b.reshape(1, HALF))


def kernel(x_prompt, x_sample, c, cache_k, cache_v, state_s5_re, state_s5_im, c_ctx, norm1_g, norm2_g, ada_w, ada_b, mlp_w1, mlp_w2, ab_w_in, pool_w, pool_scale, conv_w, ab_w_out, cd_w_in, na_rpb, ssm_lambda_re, ssm_lambda_im, ssm_log_step, ssm_b_re, ssm_b_im, ssm_c_re, ssm_c_im, ssm_d, glu_w, glu_b, cd_w_out, final_g):
    depth = ada_w.shape[0]
    xs = (x_prompt.reshape(N_PROMPT_TOK, D_MODEL), x_sample.reshape(N_SAMPLE_TOK, D_MODEL))
    cond = jnp.concatenate([c_ctx[None, :], c, jnp.zeros((N_COND - 1 - DEC_BATCH, D_MODEL), F32)], axis=0)
    mods = _adaln(cond, ada_w, ada_b).reshape(depth, N_COND, 6, D_MODEL)

    new_k, new_v, new_re, new_im = [], [], [], []
    for layer in range(depth):
        i = layer // 2
        final = layer == depth - 1
        if layer % 2 == 0:
            y, x = _even_layer_mix(xs, mods, layer, norm1_g[layer], _to_bf16(ab_w_in, i),
                                   pool_w[i], pool_scale[i], conv_w[i])
            ya, ya_col, yb, yb_col = y, 0, y, 1
            w_out = _to_bf16(ab_w_out, i)
        else:
            x = xs[0]
            qkv, tu = _inproj_odd(x, mods, layer, norm1_g[layer], _to_bf16(cd_w_in, i))
            o_p, ck, cv = _ctx_attn(qkv)
            new_k.append(ck)
            new_v.append(cv)
            o_s = _na_attn(qkv, cache_k, cache_v, _na_col_table(na_rpb[i]), i)
            attn = (o_p, o_s)

            wend, ktoe, woutT, apow = _s5_prep(ssm_lambda_re[i], ssm_lambda_im[i], ssm_log_step[i],
                                               ssm_b_re[i], ssm_b_im[i], ssm_c_re[i], ssm_c_im[i])

            def pair_state(s):
                return s.astype(F32).reshape(DEC_BATCH, 2, S5_PAIRS, S5_P).transpose(2, 0, 1, 3)

            s0 = jnp.stack([pair_state(state_s5_re[:, i]), pair_state(state_s5_im[:, i])], axis=3)
            s0 = s0.reshape(S5_PAIRS, DEC_BATCH, 4, S5_P)
            ty, fin = _s5_scan(tu, s0, wend, ktoe, woutT, apow)
            fin = fin[:, :, :, :BATCH]
            new_re.append(fin[:, :, 0].transpose(2, 1, 0, 3).reshape(BATCH, 2, N_SSM_GROUPS, SSM_STATE))
            new_im.append(fin[:, :, 1].transpose(2, 1, 0, 3).reshape(BATCH, 2, N_SSM_GROUPS, SSM_STATE))
            d_out = _glu(tu, ty, ssm_d[i], glu_w[i], glu_b[i])
            ya, ya_col, yb, yb_col = attn, 0, d_out, 0
            w_out = _to_bf16(cd_w_out, i)
        xs = _mlp(x, ya, ya_col, yb, yb_col, w_out, mods, layer, norm2_g[layer], final_g,
                  _to_bf16(mlp_w1, layer), _to_bf16(mlp_w2, layer), final)

    y_prompt = xs[0].reshape(BATCH, SEQ, D_MODEL)
    y_sample = xs[1].reshape(DEC_BATCH, DEC_SEQ, D_MODEL)
    return (y_prompt, y_sample, jnp.concatenate(new_k, axis=1), jnp.concatenate(new_v, axis=1),
            jnp.stack(new_re, axis=1), jnp.stack(new_im, axis=1))
```

```python
import functools
import math

import jax
import jax.numpy as jnp
from jax import lax
from jax.experimental import pallas as pl
from jax.experimental.pallas import tpu as pltpu

F32 = jnp.float32
BF16 = jnp.bfloat16

D_MODEL = 1024
BATCH = 16
SEQ = 256
DEC_BATCH = 4
DEC_SEQ = 2048
PAST_LEN = 256
GRID_W = 64
GRID_ROWS = DEC_SEQ // GRID_W
HALF = 512
POOL_WINDOWS = (2, 4, 8, 16)
POOL_GROUP = 128
HEAD_DIM = 64
N_HEADS = 8
WIN_H = 8
WIN_W = 16
SSM_GROUP = 16
N_SSM_GROUPS = 32
SSM_STATE = 64
D_FF = 4096
EPS = 1e-6

N_PROMPT_TOK = BATCH * SEQ
N_SAMPLE_TOK = DEC_BATCH * DEC_SEQ
N_TOK = N_PROMPT_TOK + N_SAMPLE_TOK
N_COND = 8

S5_CHUNK = 16
S5_PAIRS = N_SSM_GROUPS // 2
S5_CH = 2 * SSM_GROUP
S5_P = 2 * SSM_STATE
S5_W = S5_CHUNK * S5_CH
S5_BP_SAMPLE = 8

VMEM_LIMIT = 56 * 1024 * 1024


def _cparams(sem):
    return pltpu.CompilerParams(dimension_semantics=sem, vmem_limit_bytes=VMEM_LIMIT)


def _cond_row(i, tm):
    npt = N_PROMPT_TOK // tm
    per = DEC_SEQ // tm
    return jnp.where(i < npt, 0, 1 + (i - npt) // per)


def _normmod(x, g, shift, scale):
    ms = jnp.mean(x * x, axis=-1, keepdims=True)
    return (x * lax.rsqrt(ms + EPS) * g) * (1.0 + scale) + shift


def _dot(a, b):
    return jnp.dot(a, b, preferred_element_type=F32)


def _dot_nt(a, b, precision=None):
    return lax.dot_general(a, b, (((1,), (1,)), ((), ())), preferred_element_type=F32, precision=precision)


def _to_bf16(w_stack, idx):
    return w_stack[idx].astype(BF16)


def _adaln_body(c_ref, w_ref, b_ref, o_ref):
    c = c_ref[...]
    s = c * jax.nn.sigmoid(c)
    o_ref[...] = _dot(s.astype(BF16), w_ref[...].astype(BF16)) + b_ref[...]


def _adaln(cond, ada_w, ada_b):
    depth = ada_w.shape[0]
    nj = 6
    return pl.pallas_call(
        _adaln_body,
        grid=(depth, nj),
        in_specs=[
            pl.BlockSpec((N_COND, D_MODEL), lambda l, j: (0, 0)),
            pl.BlockSpec((None, D_MODEL, D_MODEL), lambda l, j: (l, 0, j)),
            pl.BlockSpec((None, 1, D_MODEL), lambda l, j: (l, 0, j)),
        ],
        out_specs=pl.BlockSpec((None, N_COND, D_MODEL), lambda l, j: (l, 0, j)),
        out_shape=jax.ShapeDtypeStruct((depth, N_COND, 6 * D_MODEL), F32),
        compiler_params=_cparams(("arbitrary", "arbitrary")),
        name="adaln",
    )(cond, ada_w, ada_b.reshape(depth, 1, 6 * D_MODEL))


TM_PROJ = 1024


def _stream_tile(x_refs, tm):
    if len(x_refs) == 1:
        return x_refs[0][...]
    is_prompt = pl.program_id(0) < N_PROMPT_TOK // tm
    return jnp.where(is_prompt, x_refs[0][...], x_refs[1][...])


N_CHUNK_ROWS = N_TOK // S5_CHUNK
LANE_TILES = S5_W // 128


def _inproj_odd_body(x_ref, m_ref, g_ref, w_ref, qkv_ref, tu_ref, u_ref):
    m = m_ref[...]
    h = _normmod(x_ref[...], g_ref[...], m[0:1], m[1:2])
    acc = _dot(h.astype(BF16), w_ref[...])
    qkv_ref[...] = acc[:, 0:3 * HALF].astype(BF16)
    for t in range(LANE_TILES):
        u_ref[t] = acc[:, 3 * HALF + t * 128:3 * HALF + (t + 1) * 128]
    rows = TM_PROJ // S5_CHUNK
    for j in range(S5_CHUNK):
        uj = jnp.concatenate([u_ref[t, pl.ds(j, rows, stride=S5_CHUNK), :] for t in range(LANE_TILES)], axis=1)
        if j:
            uj = pltpu.roll(uj, j * S5_CH, 1)
        for t in range(LANE_TILES):
            tu_ref[j, t] = uj[:, t * 128:(t + 1) * 128].astype(BF16)


def _inproj_odd(x, mods, layer, g, w):
    tm = TM_PROJ
    rows = tm // S5_CHUNK
    return pl.pallas_call(
        _inproj_odd_body,
        grid=(N_TOK // tm,),
        in_specs=[
            pl.BlockSpec((tm, D_MODEL), lambda i: (i, 0)),
            pl.BlockSpec((None, None, 6, D_MODEL), lambda i: (layer, _cond_row(i, tm), 0, 0)),
            pl.BlockSpec((1, D_MODEL), lambda i: (0, 0)),
            pl.BlockSpec((D_MODEL, 4 * HALF), lambda i: (0, 0)),
        ],
        out_specs=[
            pl.BlockSpec((tm, 3 * HALF), lambda i: (i, 0)),
            pl.BlockSpec((S5_CHUNK, LANE_TILES, rows, 128), lambda i: (0, 0, i, 0)),
        ],
        out_shape=[
            jax.ShapeDtypeStruct((N_TOK, 3 * HALF), BF16),
            jax.ShapeDtypeStruct((S5_CHUNK, LANE_TILES, N_CHUNK_ROWS, 128), BF16),
        ],
        scratch_shapes=[pltpu.VMEM((LANE_TILES, tm, 128), F32)],
        compiler_params=_cparams(("parallel",)),
        name="inproj_odd",
    )(x, mods, g.reshape(1, D_MODEL), w)


TM_MIX = 256
HALO = 16
RING = 3
LAG = 2
PAD_ROWS = TM_MIX + 2 * HALO + 8


def _even_step(phase, x_refs, m_ref, g_ref, w_ref, pw_ref, ps_ref, cw_ref, y_ref, xcat_ref, proj_ref, tail_ref,
               pad_ref, h_ref, t1_ref, t2_ref):
    i = pl.program_id(0)
    write_slot, cur_slot, next_slot = phase, (phase + 1) % RING, (phase + 2) % RING

    m = m_ref[...]
    x = _stream_tile(x_refs, TM_MIX)
    xcat_ref[...] = x
    h_ref[...] = _normmod(x, g_ref[...], m[0:1], m[1:2]).astype(BF16)
    n_pieces = len(POOL_WINDOWS) + HALF // 128
    pw = (4 * HALF) // n_pieces

    def project(piece):
        cols = slice(piece * pw, (piece + 1) * pw)
        proj_ref[write_slot, :, cols] = _dot(h_ref[...], w_ref[:, cols])

    t = jnp.maximum(i - LAG, 0)
    n_prompt_tiles = N_PROMPT_TOK // TM_MIX
    tiles_per_seq = DEC_SEQ // TM_MIX
    is_prompt = t < n_prompt_tiles
    tile_in_seq = jnp.where(is_prompt, 0, (t - n_prompt_tiles) % tiles_per_seq)
    seq_len = jnp.where(is_prompt, SEQ, DEC_SEQ)
    has_prev = tile_in_seq > 0
    has_next = jnp.logical_and(jnp.logical_not(is_prompt), tile_in_seq < tiles_per_seq - 1)
    prev_keep = jnp.where(has_prev, 1.0, 0.0).astype(F32)
    next_keep = jnp.where(has_next, 1.0, 0.0).astype(F32)
    pos = tile_in_seq * TM_MIX + lax.broadcasted_iota(jnp.int32, (TM_MIX, 1), 0)
    lo_rows, hi_rows = HALO, HALO + TM_MIX
    head = slice(0, HALO)

    for gi, w in enumerate(POOL_WINDOWS):
        project(gi)
        cs = slice(gi * POOL_GROUP, (gi + 1) * POOL_GROUP)
        pad_ref[0:HALO, cs] = tail_ref[:, cs] * prev_keep
        pad_ref[lo_rows:hi_rows, cs] = proj_ref[cur_slot, :, cs]
        pad_ref[hi_rows:hi_rows + HALO, cs] = proj_ref[next_slot, head, cs] * next_keep
        src, n_rows, span = pad_ref, PAD_ROWS, 1
        stage = 0
        while 2 * span < w:
            dst = (t1_ref, t2_ref)[stage % 2]
            n_rows -= 8
            dst[0:n_rows, cs] = src[0:n_rows, cs] + src[span:span + n_rows, cs]
            src, span, stage = dst, 2 * span, stage + 1
        s = src[lo_rows - span:hi_rows - span, cs] + src[lo_rows:hi_rows, cs]
        lo = jnp.maximum(pos - w // 2, 0)
        hi = jnp.minimum(pos + (w - w // 2), seq_len)
        inv = 1.0 / (hi - lo).astype(F32)
        pooled = s * inv - pad_ref[lo_rows:hi_rows, cs]
        mixed = _dot(pooled.astype(BF16), pw_ref[gi]) * ps_ref[:, cs]
        y_ref[:, cs] = mixed.astype(BF16)

    for ci in range(HALF // 128):
        project(len(POOL_WINDOWS) + ci)
        cs = slice(ci * 128, (ci + 1) * 128)
        bs = slice(HALF + ci * 128, HALF + (ci + 1) * 128)
        gs = slice(2 * HALF + ci * 128, 2 * HALF + (ci + 1) * 128)
        vs = slice(3 * HALF + ci * 128, 3 * HALF + (ci + 1) * 128)
        pad_ref[0:HALO, cs] = (tail_ref[:, gs] * tail_ref[:, vs]) * prev_keep
        pad_ref[lo_rows:hi_rows, cs] = proj_ref[cur_slot, :, gs] * proj_ref[cur_slot, :, vs]
        pad_ref[hi_rows:hi_rows + HALO, cs] = (proj_ref[next_slot, head, gs] * proj_ref[next_slot, head, vs]) * next_keep
        conv = (cw_ref[0:1, cs] * pad_ref[lo_rows - 1:hi_rows - 1, cs]
                + cw_ref[1:2, cs] * pad_ref[lo_rows:hi_rows, cs]
                + cw_ref[2:3, cs] * pad_ref[lo_rows + 1:hi_rows + 1, cs])
        y_ref[:, bs] = (proj_ref[cur_slot, :, bs] * conv).astype(BF16)

    tail_ref[...] = proj_ref[cur_slot, TM_MIX - HALO:TM_MIX, :]


def _even_body(n_x, *refs):
    x_refs, rest = refs[:n_x], refs[n_x:]
    proj_ref, tail_ref, pad_ref = rest[-6], rest[-5], rest[-4]
    i = pl.program_id(0)

    @pl.when(i == 0)
    def _():
        proj_ref[...] = jnp.zeros_like(proj_ref)
        tail_ref[...] = jnp.zeros_like(tail_ref)
        pad_ref[...] = jnp.zeros_like(pad_ref)

    for phase in range(RING):
        @pl.when(i % RING == phase)
        def _(phase=phase):
            _even_step(phase, x_refs, *rest)


def _even_layer_mix(xs, mods, layer, g, w, pool_w, pool_scale, conv_w):
    tm = TM_MIX
    n_tiles = N_TOK // tm
    npt = N_PROMPT_TOK // tm
    last = n_tiles - 1
    if len(xs) == 1:
        x_specs = [pl.BlockSpec((tm, D_MODEL), lambda i: (jnp.minimum(i, last), 0))]
    else:
        x_specs = [pl.BlockSpec((tm, D_MODEL), lambda i: (jnp.minimum(i, npt - 1), 0)),
                   pl.BlockSpec((tm, D_MODEL), lambda i: (jnp.clip(i - npt, 0, last - npt), 0))]
    return pl.pallas_call(
        functools.partial(_even_body, len(xs)),
        grid=(n_tiles + LAG,),
        in_specs=x_specs + [
            pl.BlockSpec((None, None, 6, D_MODEL), lambda i: (layer, _cond_row(jnp.minimum(i, last), tm), 0, 0)),
            pl.BlockSpec((1, D_MODEL), lambda i: (0, 0)),
            pl.BlockSpec((D_MODEL, 4 * HALF), lambda i: (0, 0)),
            pl.BlockSpec((len(POOL_WINDOWS), POOL_GROUP, POOL_GROUP), lambda i: (0, 0, 0)),
            pl.BlockSpec((1, HALF), lambda i: (0, 0)),
            pl.BlockSpec((3, HALF), lambda i: (0, 0)),
        ],
        out_specs=[pl.BlockSpec((tm, 2 * HALF), lambda i: (jnp.maximum(i - LAG, 0), 0)),
                   pl.BlockSpec((tm, D_MODEL), lambda i: (jnp.minimum(i, last), 0))],
        out_shape=[jax.ShapeDtypeStruct((N_TOK, 2 * HALF), BF16), jax.ShapeDtypeStruct((N_TOK, D_MODEL), F32)],
        scratch_shapes=[pltpu.VMEM((RING, tm, 4 * HALF), F32), pltpu.VMEM((HALO, 4 * HALF), F32),
                        pltpu.VMEM((PAD_ROWS, HALF), F32), pltpu.VMEM((tm, D_MODEL), BF16),
                        pltpu.VMEM((PAD_ROWS - 8, HALF), F32), pltpu.VMEM((PAD_ROWS - 8, HALF), F32)],
        compiler_params=_cparams(("arbitrary",)),
        name="even_layer_mix",
    )(*xs, mods, g.reshape(1, D_MODEL), w, pool_w.astype(BF16), pool_scale.reshape(1, HALF), conv_w)


TM_MLP = 1024
RC_MLP = 256


def _mlp_body(final, x_ref, ya_ref, yb_ref, wo_ref, m_ref, g_ref, fg_ref, w1_ref, w2_ref, o_ref):
    m = m_ref[...]
    gain = g_ref[...] * (1.0 + m[4:5])
    for r in range(TM_MLP // RC_MLP):
        rows = slice(r * RC_MLP, (r + 1) * RC_MLP)
        mix = _dot(ya_ref[rows, :], wo_ref[0:HALF, :]) + _dot(yb_ref[rows, :], wo_ref[HALF:2 * HALF, :])
        x1 = x_ref[rows, :] + m[2:3] * mix
        ms = jnp.mean(x1 * x1, axis=-1, keepdims=True)
        h = (x1 * lax.rsqrt(ms + EPS) * gain + m[3:4]).astype(BF16)
        t = _dot(h, w1_ref[...])
        t = jnp.square(jnp.maximum(t, 0.0)).astype(BF16)
        out = x1 + m[5:6] * _dot(t, w2_ref[...])
        if final:
            ms = jnp.mean(out * out, axis=-1, keepdims=True)
            out = out * lax.rsqrt(ms + EPS) * fg_ref[...]
        o_ref[rows, :] = out


def _resident(shape):
    zeros = (0,) * len(shape)
    return pl.BlockSpec(shape, lambda i: zeros, pipeline_mode=pl.Buffered(1))


def _mlp_call(x, x_off, ya, ya_off, ya_col, yb, yb_off, yb_col, n_tiles, tile0, w_out, mods, layer, g2, final_g,
              w1, w2, final):
    tm = TM_MLP
    return pl.pallas_call(
        functools.partial(_mlp_body, final),
        grid=(n_tiles,),
        in_specs=[
            pl.BlockSpec((tm, D_MODEL), lambda i: (i + x_off, 0)),
            pl.BlockSpec((tm, HALF), lambda i: (i + ya_off, ya_col)),
            pl.BlockSpec((tm, HALF), lambda i: (i + yb_off, yb_col)),
            _resident((D_MODEL, D_MODEL)),
            pl.BlockSpec((None, None, 6, D_MODEL), lambda i: (layer, _cond_row(i + tile0, tm), 0, 0)),
            _resident((1, D_MODEL)),
            _resident((1, D_MODEL)),
            _resident((D_MODEL, D_FF)),
            _resident((D_FF, D_MODEL)),
        ],
        out_specs=pl.BlockSpec((tm, D_MODEL), lambda i: (i, 0)),
        out_shape=jax.ShapeDtypeStruct((n_tiles * tm, D_MODEL), F32),
        compiler_params=_cparams(("parallel",)),
        name="mlp",
    )(x, ya, yb, w_out, mods, g2.reshape(1, D_MODEL), final_g.reshape(1, D_MODEL), w1, w2)


def _mlp(x, ya, ya_col, yb, yb_col, w_out, mods, layer, g2, final_g, w1, w2, final):
    tm = TM_MLP
    npt = N_PROMPT_TOK // tm
    weights = (w_out, mods, layer, g2, final_g, w1, w2, final)
    if not final:
        if isinstance(ya, tuple):
            ya = jnp.concatenate(ya, axis=0)
        return (_mlp_call(x, 0, ya, 0, ya_col, yb, 0, yb_col, N_TOK // tm, 0, *weights),)
    outs = []
    for s, (tile0, n_tiles) in enumerate(((0, npt), (npt, N_TOK // tm - npt))):
        a, a_off = (ya[s], 0) if isinstance(ya, tuple) else (ya, tile0)
        outs.append(_mlp_call(x, tile0, a, a_off, ya_col, yb, tile0, yb_col, n_tiles, tile0, *weights))
    return tuple(outs)


def _ctx_attn_body(q_ref, k_ref, v_ref, o_ref, ck_ref, cv_ref):
    lane = lax.broadcasted_iota(jnp.int32, (1, 2 * HEAD_DIM), 1)
    own = [lane < HEAD_DIM, lane >= HEAD_DIM]
    for h in range(N_HEADS):
        sl = slice(h * HEAD_DIM, (h + 1) * HEAD_DIM)
        ck_ref[h] = k_ref[:, sl].astype(F32)
        cv_ref[h] = v_ref[:, sl].astype(F32)
    for hp in range(N_HEADS // 2):
        sl = slice(hp * 2 * HEAD_DIM, (hp + 1) * 2 * HEAD_DIM)
        q = q_ref[:, sl] * jnp.asarray(HEAD_DIM ** -0.5, BF16)
        k = k_ref[:, sl]
        v = v_ref[:, sl]
        q2 = jnp.concatenate([jnp.where(own[0], q, jnp.zeros_like(q)),
                              jnp.where(own[1], q, jnp.zeros_like(q))], axis=0)
        s = _dot_nt(q2, k)
        m = jnp.max(s, axis=-1, keepdims=True)
        p = jnp.exp(s - m).astype(BF16)
        out = None
        for hh in range(2):
            num = _dot(p[hh * SEQ:(hh + 1) * SEQ], jnp.where(own[hh], v, jnp.ones_like(v)))
            den = jnp.where(own[hh], pltpu.roll(num, HEAD_DIM, 1), 1.0)
            o = num / den
            out = o if out is None else jnp.where(own[0], out, o)
        o_ref[:, sl] = out.astype(BF16)


def _ctx_attn(proj):
    return pl.pallas_call(
        _ctx_attn_body,
        grid=(BATCH,),
        in_specs=[
            pl.BlockSpec((SEQ, HALF), lambda b: (b, 0)),
            pl.BlockSpec((SEQ, HALF), lambda b: (b, 1)),
            pl.BlockSpec((SEQ, HALF), lambda b: (b, 2)),
        ],
        out_specs=[
            pl.BlockSpec((SEQ, HALF), lambda b: (b, 0)),
            pl.BlockSpec((None, None, N_HEADS, SEQ, HEAD_DIM), lambda b: (b, 0, 0, 0, 0)),
            pl.BlockSpec((None, None, N_HEADS, SEQ, HEAD_DIM), lambda b: (b, 0, 0, 0, 0)),
        ],
        out_shape=[
            jax.ShapeDtypeStruct((N_PROMPT_TOK, HALF), BF16),
            jax.ShapeDtypeStruct((BATCH, 1, N_HEADS, SEQ, HEAD_DIM), F32),
            jax.ShapeDtypeStruct((BATCH, 1, N_HEADS, SEQ, HEAD_DIM), F32),
        ],
        compiler_params=_cparams(("parallel",)),
        name="ctx_attn",
    )(proj, proj, proj)


NA_MASKED = -1e30
NA_QROWS = 4
NA_KROWS = NA_QROWS + WIN_H
NA_NQ = NA_QROWS * GRID_W
NA_NK = NA_KROWS * GRID_W
NA_BLOCKS = GRID_ROWS // NA_QROWS
NA_TYPES = 3


def _na_col_table(rpb):
    qc = jnp.arange(GRID_W)[:, None]
    kc = jnp.arange(GRID_W)[None, :]
    c0 = jnp.clip(qc - WIN_W // 2, 0, GRID_W - WIN_W)
    valid = (kc >= c0) & (kc < c0 + WIN_W)
    coff = jnp.clip(kc - qc, -(WIN_W - 1), WIN_W - 1) + (WIN_W - 1)
    onehot = (coff.reshape(1, -1) == jnp.arange(2 * WIN_W - 1)[:, None]).astype(F32)
    t = jnp.dot(rpb.astype(F32).reshape(N_HEADS * (2 * WIN_H - 1), 2 * WIN_W - 1), onehot,
                precision=lax.Precision.HIGHEST)
    t = t.reshape(N_HEADS, 2 * WIN_H - 1, GRID_W, GRID_W)
    return jnp.where(valid[None, None], t, NA_MASKED)


def _na_window_start(rb):
    return jnp.clip(NA_QROWS * rb - WIN_H // 2, 0, GRID_ROWS - NA_KROWS)


def _na_body(q_ref, k_ref, v_ref, ck_ref, cv_ref, tab_ref, o_ref, bias_ref, kc_ref, vc_ref, va_ref):
    @pl.when(pl.program_id(1) == 0)
    def _():
        masked = jnp.full((GRID_W, GRID_W), NA_MASKED, F32)
        for hh in range(2):
            for ty in range(NA_TYPES):
                q0 = (0, NA_QROWS, GRID_ROWS - NA_QROWS)[ty]
                ks = (0, 0, GRID_ROWS - NA_KROWS)[ty]
                for a in range(NA_QROWS):
                    r0 = min(max(q0 + a - WIN_H // 2, 0), GRID_ROWS - WIN_H)
                    for y in range(NA_KROWS):
                        kr = ks + y
                        ok = r0 <= kr < r0 + WIN_H
                        blk = tab_ref[hh, kr - (q0 + a) + WIN_H - 1] if ok else masked
                        bias_ref[ty, hh * NA_NQ + a * GRID_W:hh * NA_NQ + (a + 1) * GRID_W,
                                 y * GRID_W:(y + 1) * GRID_W] = blk

    lane = lax.broadcasted_iota(jnp.int32, (1, 2 * HEAD_DIM), 1)
    own = [lane < HEAD_DIM, lane >= HEAD_DIM]
    kc_ref[...] = jnp.concatenate([ck_ref[0], ck_ref[1]], axis=1).astype(BF16)
    vc = jnp.concatenate([cv_ref[0], cv_ref[1]], axis=1)
    v = v_ref[...]
    for hh in range(2):
        vc_ref[hh] = jnp.where(own[hh], vc, 1.0).astype(BF16)
        va_ref[hh] = jnp.where(own[hh], v, jnp.ones_like(v))

    def block(rb, ty):
        qs = pl.multiple_of(rb * NA_NQ, NA_NQ)
        ks = pl.multiple_of(_na_window_start(rb) * GRID_W, GRID_W)
        q = q_ref[pl.ds(qs, NA_NQ), :] * jnp.asarray(HEAD_DIM ** -0.5, BF16)
        q2 = jnp.concatenate([jnp.where(own[0], q, jnp.zeros_like(q)),
                              jnp.where(own[1], q, jnp.zeros_like(q))], axis=0)
        s_loc = _dot_nt(q2, k_ref[pl.ds(ks, NA_NK), :]) + bias_ref[ty]
        s_ctx = _dot_nt(q2, kc_ref[...])
        m = jnp.maximum(jnp.max(s_loc, axis=-1, keepdims=True), jnp.max(s_ctx, axis=-1, keepdims=True))
        p_loc = jnp.exp(s_loc - m).astype(BF16)
        p_ctx = jnp.exp(s_ctx - m).astype(BF16)
        out = None
        for hh in range(2):
            rows = slice(hh * NA_NQ, (hh + 1) * NA_NQ)
            num = _dot(p_loc[rows], va_ref[hh, pl.ds(ks, NA_NK), :]) + _dot(p_ctx[rows], vc_ref[hh])
            den = jnp.where(own[hh], pltpu.roll(num, HEAD_DIM, 1), 1.0)
            o = num / den
            out = o if out is None else jnp.where(own[0], out, o)
        o_ref[pl.ds(qs, NA_NQ), :] = out.astype(BF16)

    block(0, 0)

    def interior(rb, carry):
        block(rb, 1)
        return carry

    lax.fori_loop(1, NA_BLOCKS - 1, interior, 0, unroll=3)
    block(NA_BLOCKS - 1, 2)


def _na_attn(proj, cache_k, cache_v, col_table, layer_idx):
    row_blk0 = N_PROMPT_TOK // DEC_SEQ
    hp = N_HEADS // 2
    return pl.pallas_call(
        _na_body,
        grid=(hp, DEC_BATCH),
        in_specs=[
            pl.BlockSpec((DEC_SEQ, 128), lambda h, b: (row_blk0 + b, h)),
            pl.BlockSpec((DEC_SEQ, 128), lambda h, b: (row_blk0 + b, hp + h)),
            pl.BlockSpec((DEC_SEQ, 128), lambda h, b: (row_blk0 + b, 2 * hp + h)),
            pl.BlockSpec((None, None, 2, PAST_LEN, HEAD_DIM), lambda h, b: (b, layer_idx, h, 0, 0)),
            pl.BlockSpec((None, None, 2, PAST_LEN, HEAD_DIM), lambda h, b: (b, layer_idx, h, 0, 0)),
            pl.BlockSpec((2, 2 * WIN_H - 1, GRID_W, GRID_W), lambda h, b: (h, 0, 0, 0)),
        ],
        out_specs=pl.BlockSpec((DEC_SEQ, 128), lambda h, b: (b, h)),
        out_shape=jax.ShapeDtypeStruct((N_SAMPLE_TOK, HALF), BF16),
        scratch_shapes=[pltpu.VMEM((NA_TYPES, 2 * NA_NQ, NA_NK), F32),
                        pltpu.VMEM((PAST_LEN, 2 * HEAD_DIM), BF16),
                        pltpu.VMEM((2, PAST_LEN, 2 * HEAD_DIM), BF16),
                        pltpu.VMEM((2, DEC_SEQ, 2 * HEAD_DIM), BF16)],
        compiler_params=_cparams(("parallel", "arbitrary")),
        name="na_attn",
    )(proj, proj, proj, cache_k, cache_v, col_table)


N_POW = 24


def _s5_prep_body(vec_ref, bt_ref, c_ref, wend_ref, ktoe_ref, woutT_ref, apow_ref, pw_ref, pct_ref):
    L = S5_CHUNK
    p = pl.program_id(0)
    lane_blk = lax.broadcasted_iota(jnp.int32, (S5_CH, S5_W), 1) // S5_CH
    src_blk = (lane_blk + L - p) % L
    krows = []
    for d in range(2):
        v = vec_ref[d]
        lre, lim, step = v[0:1], v[1:2], v[2:3]
        a = lre * step
        b = lim * step
        ea = jnp.exp(a)
        nr = ea * jnp.cos(b) - 1.0
        ni = ea * jnp.sin(b)
        den = lre * lre + lim * lim
        qr = (nr * lre + ni * lim) / den
        qi = (ni * lre - nr * lim) / den
        bre, bim = bt_ref[d, 0], bt_ref[d, 1]
        bqr = bre * qr - bim * qi
        bqi = bre * qi + bim * qr
        cre, cim = c_ref[d, 0], c_ref[d, 1]

        e = lax.broadcasted_iota(jnp.int32, (N_POW, 1), 0).astype(F32)
        mag = jnp.exp(e * a)
        pw_ref[0] = mag * jnp.cos(e * b)
        pw_ref[1] = mag * jnp.sin(e * b)
        mag = jnp.exp((L * e) * a)
        apow_ref[d, 0] = mag * jnp.cos((L * e) * b)
        apow_ref[d, 1] = mag * jnp.sin((L * e) * b)

        def power(k):
            return pw_ref[0, k:k + 1, :], pw_ref[1, k:k + 1, :]

        def c_block(k):
            pr, pi = power(k)
            return cre * pr - cim * pi, -(cre * pi + cim * pr)

        for i in range(L):
            rows = pl.ds(pl.multiple_of(((i + p) % L) * S5_CH, S5_CH), S5_CH)
            pr, pi = power(L - 1 - i if d == 0 else i)
            wend_ref[rows, d * 256:d * 256 + 128] = (bqr * pr - bqi * pi).astype(BF16)
            wend_ref[rows, d * 256 + 128:d * 256 + 256] = (bqr * pi + bqi * pr).astype(BF16)
            mre, mim = c_block(i + 1 if d == 0 else L - i)
            woutT_ref[rows, d * 256:d * 256 + 128] = mre.astype(BF16)
            woutT_ref[rows, d * 256 + 128:d * 256 + 256] = mim.astype(BF16)
            lag_rows = slice(i * S5_CH, (i + 1) * S5_CH)
            mre, mim = c_block(i if d == 0 else L - 1 - i)
            pct_ref[lag_rows, 0:128] = mre
            pct_ref[lag_rows, 128:256] = mim

        bqcat = jnp.concatenate([bqr, bqi], axis=1)
        krows.append(_dot_nt(bqcat, pct_ref[...], precision=lax.Precision.HIGHEST))

    for i in range(L):
        rows = pl.ds(pl.multiple_of(((i + p) % L) * S5_CH, S5_CH), S5_CH)
        blk_f = pltpu.roll(krows[0], ((i + p) % L) * S5_CH, 1)
        blk_b = pltpu.roll(krows[1], ((i + 1 + p) % L) * S5_CH, 1)
        blk = jnp.where(src_blk >= i, blk_f, 0.0) + jnp.where(src_blk <= i, blk_b, 0.0)
        ktoe_ref[rows, :] = blk.astype(BF16)


def _s5_prep(lam_re, lam_im, log_step, b_re, b_im, c_re, c_im):
    def pair_vec(a):
        return a.astype(F32).reshape(2, S5_PAIRS, 1, S5_P).transpose(1, 0, 2, 3)

    step = jnp.broadcast_to(jnp.exp(log_step.astype(F32))[:, :, None], (2, N_SSM_GROUPS, SSM_STATE))
    vec = jnp.concatenate([pair_vec(lam_re), pair_vec(lam_im), pair_vec(step),
                           jnp.zeros((S5_PAIRS, 2, 5, S5_P), F32)], axis=2)

    def block_diag_ch_by_state(m):
        m = m.astype(F32).reshape(2, S5_PAIRS, 2, SSM_GROUP, SSM_STATE)
        eye = jnp.eye(2, dtype=F32)
        out = m[:, :, :, :, None, :] * eye[None, None, :, None, :, None]
        return out.reshape(2, S5_PAIRS, S5_CH, S5_P).transpose(1, 0, 2, 3)

    bt = jnp.stack([block_diag_ch_by_state(b_re.transpose(0, 1, 3, 2)),
                    block_diag_ch_by_state(b_im.transpose(0, 1, 3, 2))], axis=2)
    cc = jnp.stack([block_diag_ch_by_state(c_re), block_diag_ch_by_state(c_im)], axis=2)

    mat = jax.ShapeDtypeStruct((S5_PAIRS, S5_W, S5_W), BF16)
    mat_spec = pl.BlockSpec((None, S5_W, S5_W), lambda p: (p, 0, 0))
    return pl.pallas_call(
        _s5_prep_body,
        grid=(S5_PAIRS,),
        in_specs=[
            pl.BlockSpec((None, 2, 8, S5_P), lambda p: (p, 0, 0, 0)),
            pl.BlockSpec((None, 2, 2, S5_CH, S5_P), lambda p: (p, 0, 0, 0, 0)),
            pl.BlockSpec((None, 2, 2, S5_CH, S5_P), lambda p: (p, 0, 0, 0, 0)),
        ],
        out_specs=[mat_spec, mat_spec, mat_spec,
                   pl.BlockSpec((None, 2, 2, N_POW, S5_P), lambda p: (p, 0, 0, 0, 0))],
        out_shape=[mat, mat, mat, jax.ShapeDtypeStruct((S5_PAIRS, 2, 2, N_POW, S5_P), F32)],
        scratch_shapes=[pltpu.VMEM((2, N_POW, S5_P), F32), pltpu.VMEM((S5_W, 2 * S5_P), F32)],
        compiler_params=_cparams(("parallel",)),
        name="s5_prep",
    )(vec, bt, cc)


S5_SEG_ROWS = 256
S5_N_SEG = N_CHUNK_ROWS // S5_SEG_ROWS
S5_SUB = S5_SEG_ROWS // S5_CHUNK
S5_SUB_PER_SEQ = (DEC_SEQ // S5_CHUNK) // S5_CHUNK


def _s5_body(tu_ref, s0_ref, wend_ref, ktoe_ref, woutT_ref, apow_ref, ty_ref, fin_ref, z_ref, xs_ref, g_ref):
    seg = pl.program_id(0)
    p = pl.program_id(1)
    L = S5_CHUNK
    lane_blk = lax.broadcasted_iota(jnp.int32, (S5_SEG_ROWS, 128), 1) // S5_CH

    tiles = []
    for t in range(LANE_TILES):
        tile = None
        for k in range(128 // S5_CH):
            j = (LANE_TILES * t + k + L - p) % L
            blk = tu_ref[j, t]
            tile = blk if tile is None else jnp.where(lane_blk == k, blk, tile)
        tiles.append(tile)
    s = jnp.concatenate(tiles, axis=1)
    z = _dot(s, wend_ref[...])
    for q in range(LANE_TILES):
        z_ref[q] = z[:, q * 128:(q + 1) * 128]

    ends = []
    for d in range(2):
        are, aim = apow_ref[d, 0, 1:2, :], apow_ref[d, 1, 1:2, :]
        xr = jnp.zeros((S5_SUB, S5_P), F32)
        xi = jnp.zeros((S5_SUB, S5_P), F32)
        for c in (range(L) if d == 0 else reversed(range(L))):
            rows = pl.ds(c, S5_SUB, stride=L)
            xs_ref[2 * d, rows, :] = xr
            xs_ref[2 * d + 1, rows, :] = xi
            zr = z_ref[2 * d, rows, :]
            zi = z_ref[2 * d + 1, rows, :]
            xr, xi = are * xr - aim * xi + zr, are * xi + aim * xr + zi
        ends.append((xr, xi))
        fin_ref[d, 0] = xr
        fin_ref[d, 1] = xi

    @pl.when(seg > 0)
    def _():
        for d in range(2):
            er, ei = ends[d]
            a16r, a16i = apow_ref[d, 0, L:L + 1, :], apow_ref[d, 1, L:L + 1, :]
            for bb in range(S5_SUB // S5_SUB_PER_SEQ):
                gr = s0_ref[bb, 2 * d:2 * d + 1, :]
                gi = s0_ref[bb, 2 * d + 1:2 * d + 2, :]
                order = range(S5_SUB_PER_SEQ) if d == 0 else reversed(range(S5_SUB_PER_SEQ))
                prev = None
                for sub in order:
                    row = bb * S5_SUB_PER_SEQ + sub
                    if prev is not None:
                        gr, gi = (a16r * gr - a16i * gi + er[prev:prev + 1, :],
                                  a16r * gi + a16i * gr + ei[prev:prev + 1, :])
                    g_ref[0, row:row + 1, :] = gr
                    g_ref[1, row:row + 1, :] = gi
                    prev = row
            if d == 0:
                pr, pi = apow_ref[d, 0, 0:L, :], apow_ref[d, 1, 0:L, :]
            else:
                pr = jnp.concatenate([apow_ref[d, 0, k:k + 1, :] for k in reversed(range(L))], axis=0)
                pi = jnp.concatenate([apow_ref[d, 1, k:k + 1, :] for k in reversed(range(L))], axis=0)
            for sub in range(S5_SUB):
                rows = slice(sub * L, (sub + 1) * L)
                gr = g_ref[0, sub:sub + 1, :]
                gi = g_ref[1, sub:sub + 1, :]
                xs_ref[2 * d, rows, :] = xs_ref[2 * d, rows, :] + (pr * gr - pi * gi)
                xs_ref[2 * d + 1, rows, :] = xs_ref[2 * d + 1, rows, :] + (pr * gi + pi * gr)

    xs = jnp.concatenate([xs_ref[q].astype(BF16) for q in range(LANE_TILES)], axis=1)
    y = _dot(s, ktoe_ref[...]) + _dot_nt(xs, woutT_ref[...])

    @pl.when(p == 0)
    def _():
        ty_ref[...] = jnp.zeros_like(ty_ref)

    for t in range(LANE_TILES):
        yt = y[:, t * 128:(t + 1) * 128]
        for k in range(128 // S5_CH):
            j = (LANE_TILES * t + k + L - p) % L
            ty_ref[j, t] = jnp.where(lane_blk == k, yt, ty_ref[j, t])


def _s5_scan(tu, s0, wend, ktoe, woutT, apow):
    mat_spec = pl.BlockSpec((None, S5_W, S5_W), lambda g, p: (p, 0, 0))
    seqs = S5_SUB // S5_SUB_PER_SEQ
    return pl.pallas_call(
        _s5_body,
        grid=(S5_N_SEG, S5_PAIRS),
        in_specs=[
            pl.BlockSpec((S5_CHUNK, LANE_TILES, S5_SEG_ROWS, 128), lambda g, p: (0, 0, g, 0)),
            pl.BlockSpec((None, seqs, 4, S5_P), lambda g, p: (p, jnp.maximum(g - 1, 0), 0, 0)),
            mat_spec, mat_spec, mat_spec,
            pl.BlockSpec((None, 2, 2, N_POW, S5_P), lambda g, p: (p, 0, 0, 0, 0)),
        ],
        out_specs=[
            pl.BlockSpec((S5_CHUNK, LANE_TILES, S5_SEG_ROWS, 128), lambda g, p: (0, 0, g, 0)),
            pl.BlockSpec((None, 2, 2, S5_SUB, S5_P), lambda g, p: (p, 0, 0, g, 0)),
        ],
        out_shape=[
            jax.ShapeDtypeStruct((S5_CHUNK, LANE_TILES, N_CHUNK_ROWS, 128), F32),
            jax.ShapeDtypeStruct((S5_PAIRS, 2, 2, S5_N_SEG * S5_SUB, S5_P), F32),
        ],
        scratch_shapes=[pltpu.VMEM((LANE_TILES, S5_SEG_ROWS, 128), F32),
                        pltpu.VMEM((LANE_TILES, S5_SEG_ROWS, 128), F32),
                        pltpu.VMEM((2, S5_SUB, S5_P), F32)],
        compiler_params=_cparams(("parallel", "arbitrary")),
        name="s5_scan",
    )(tu, s0, wend, ktoe, woutT, apow)


TM_GLU = 1024


def _glu_body(tu_ref, ty_ref, d_ref, gw_ref, gb_ref, o_ref, u_ref, ys_ref):
    rows = TM_GLU // S5_CHUNK
    for j in range(S5_CHUNK):
        back = ((S5_CHUNK - j) % S5_CHUNK) * S5_CH
        uj = jnp.concatenate([tu_ref[j, t].astype(F32) for t in range(LANE_TILES)], axis=1)
        yj = jnp.concatenate([ty_ref[j, t] for t in range(LANE_TILES)], axis=1)
        if back:
            uj = pltpu.roll(uj, back, 1)
            yj = pltpu.roll(yj, back, 1)
        for t in range(LANE_TILES):
            u_ref[t, pl.ds(j, rows, stride=S5_CHUNK), :] = uj[:, t * 128:(t + 1) * 128]
            ys_ref[t, pl.ds(j, rows, stride=S5_CHUNK), :] = yj[:, t * 128:(t + 1) * 128]
    u = jnp.concatenate([u_ref[t] for t in range(LANE_TILES)], axis=1)
    y = d_ref[...] * u + jnp.concatenate([ys_ref[t] for t in range(LANE_TILES)], axis=1)
    cdf = 0.5 * (1.0 + jnp.tanh(math.sqrt(2.0 / math.pi) * (y + 0.044715 * (y * y * y))))
    y = y * cdf
    gate = jax.nn.sigmoid(_dot(y.astype(BF16), gw_ref[...]) + gb_ref[...])
    o_ref[...] = (y * gate).astype(BF16)


def _glu(tu, ty, ssm_d, glu_w, glu_b):
    tm = TM_GLU
    rows = tm // S5_CHUNK
    lay_spec = pl.BlockSpec((S5_CHUNK, LANE_TILES, rows, 128), lambda i: (0, 0, i, 0))
    return pl.pallas_call(
        _glu_body,
        grid=(N_TOK // tm,),
        in_specs=[
            lay_spec, lay_spec,
            pl.BlockSpec((1, HALF), lambda i: (0, 0)),
            pl.BlockSpec((HALF, HALF), lambda i: (0, 0)),
            pl.BlockSpec((1, HALF), lambda i: (0, 0)),
        ],
        out_specs=pl.BlockSpec((tm, HALF), lambda i: (i, 0)),
        out_shape=jax.ShapeDtypeStruct((N_TOK, HALF), BF16),
        scratch_shapes=[pltpu.VMEM((LANE_TILES, tm, 128), F32), pltpu.VMEM((LANE_TILES, tm, 128), F32)],
        compiler_params=_cparams(("parallel",)),
        name="s5_glu",
    )(tu, ty, ssm_d.reshape(1, HALF), glu_w.astype(BF16), glu_b.reshape(1, HALF))


def kernel(x_prompt, x_sample, c, cache_k, cache_v, state_s5_re, state_s5_im, c_ctx, norm1_g, norm2_g, ada_w, ada_b, mlp_w1, mlp_w2, ab_w_in, pool_w, pool_scale, conv_w, ab_w_out, cd_w_in, na_rpb, ssm_lambda_re, ssm_lambda_im, ssm_log_step, ssm_b_re, ssm_b_im, ssm_c_re, ssm_c_im, ssm_d, glu_w, glu_b, cd_w_out, final_g):
    depth = ada_w.shape[0]
    xs = (x_prompt.reshape(N_PROMPT_TOK, D_MODEL), x_sample.reshape(N_SAMPLE_TOK, D_MODEL))
    cond = jnp.concatenate([c_ctx[None, :], c, jnp.zeros((N_COND - 1 - DEC_BATCH, D_MODEL), F32)], axis=0)
    mods = _adaln(cond, ada_w, ada_b).reshape(depth, N_COND, 6, D_MODEL)

    new_k, new_v, new_re, new_im = [], [], [], []
    for layer in range(depth):
        i = layer // 2
        final = layer == depth - 1
        if layer % 2 == 0:
            y, x = _even_layer_mix(xs, mods, layer, norm1_g[layer], _to_bf16(ab_w_in, i),
                                   pool_w[i], pool_scale[i], conv_w[i])
            ya, ya_col, yb, yb_col = y, 0, y, 1
            w_out = _to_bf16(ab_w_out, i)
        else:
            x = xs[0]
            qkv, tu = _inproj_odd(x, mods, layer, norm1_g[layer], _to_bf16(cd_w_in, i))
            o_p, ck, cv = _ctx_attn(qkv)
            new_k.append(ck)
            new_v.append(cv)
            o_s = _na_attn(qkv, cache_k, cache_v, _na_col_table(na_rpb[i]), i)
            attn = (o_p, o_s)

            wend, ktoe, woutT, apow = _s5_prep(ssm_lambda_re[i], ssm_lambda_im[i], ssm_log_step[i],
                                               ssm_b_re[i], ssm_b_im[i], ssm_c_re[i], ssm_c_im[i])

            def pair_state(s):
                return s.astype(F32).reshape(DEC_BATCH, 2, S5_PAIRS, S5_P).transpose(2, 0, 1, 3)

            s0 = jnp.stack([pair_state(state_s5_re[:, i]), pair_state(state_s5_im[:, i])], axis=3)
            s0 = s0.reshape(S5_PAIRS, DEC_BATCH, 4, S5_P)
            ty, fin = _s5_scan(tu, s0, wend, ktoe, woutT, apow)
            fin = fin[:, :, :, :BATCH]
            new_re.append(fin[:, :, 0].transpose(2, 1, 0, 3).reshape(BATCH, 2, N_SSM_GROUPS, SSM_STATE))
            new_im.append(fin[:, :, 1].transpose(2, 1, 0, 3).reshape(BATCH, 2, N_SSM_GROUPS, SSM_STATE))
            d_out = _glu(tu, ty, ssm_d[i], glu_w[i], glu_b[i])
            ya, ya_col, yb, yb_col = attn, 0, d_out, 0
            w_out = _to_bf16(cd_w_out, i)
        xs = _mlp(x, ya, ya_col, yb, yb_col, w_out, mods, layer, norm2_g[layer], final_g,
                  _to_bf16(mlp_w1, layer), _to_bf16(mlp_w2, layer), final)

    y_prompt = xs[0].reshape(BATCH, SEQ, D_MODEL)
    y_sample = xs[1].reshape(DEC_BATCH, DEC_SEQ, D_MODEL)
    return (y_prompt, y_sample, jnp.concatenate(new_k, axis=1), jnp.concatenate(new_v, axis=1),
            jnp.stack(new_re, axis=1), jnp.stack(new_im, axis=1))
```

```python
import functools
import math

import jax
import jax.numpy as jnp
from jax import lax
from jax.experimental import pallas as pl
from jax.experimental.pallas import tpu as pltpu

F32 = jnp.float32
BF16 = jnp.bfloat16

D_MODEL = 1024
BATCH = 16
SEQ = 256
DEC_BATCH = 4
DEC_SEQ = 2048
PAST_LEN = 256
GRID_W = 64
GRID_ROWS = DEC_SEQ // GRID_W
HALF = 512
POOL_WINDOWS = (2, 4, 8, 16)
POOL_GROUP = 128
HEAD_DIM = 64
N_HEADS = 8
WIN_H = 8
WIN_W = 16
SSM_GROUP = 16
N_SSM_GROUPS = 32
SSM_STATE = 64
D_FF = 4096
EPS = 1e-6

N_PROMPT_TOK = BATCH * SEQ
N_SAMPLE_TOK = DEC_BATCH * DEC_SEQ
N_TOK = N_PROMPT_TOK + N_SAMPLE_TOK
N_COND = 8

S5_CHUNK = 16
S5_PAIRS = N_SSM_GROUPS // 2
S5_CH = 2 * SSM_GROUP
S5_P = 2 * SSM_STATE
S5_W = S5_CHUNK * S5_CH
S5_BP_SAMPLE = 8

VMEM_LIMIT = 56 * 1024 * 1024


def _cparams(sem):
    return pltpu.CompilerParams(dimension_semantics=sem, vmem_limit_bytes=VMEM_LIMIT)


def _cond_row(i, tm):
    npt = N_PROMPT_TOK // tm
    per = DEC_SEQ // tm
    return jnp.where(i < npt, 0, 1 + (i - npt) // per)


def _normmod(x, g, shift, scale):
    ms = jnp.mean(x * x, axis=-1, keepdims=True)
    return (x * lax.rsqrt(ms + EPS) * g) * (1.0 + scale) + shift


def _dot(a, b):
    return jnp.dot(a, b, preferred_element_type=F32)


def _dot_nt(a, b, precision=None):
    return lax.dot_general(a, b, (((1,), (1,)), ((), ())), preferred_element_type=F32, precision=precision)


def _to_bf16(w_stack, idx):
    return w_stack[idx].astype(BF16)


def _adaln_body(c_ref, w_ref, b_ref, o_ref):
    c = c_ref[...]
    s = c * jax.nn.sigmoid(c)
    o_ref[...] = _dot(s.astype(BF16), w_ref[...].astype(BF16)) + b_ref[...]


def _adaln(cond, ada_w, ada_b):
    depth = ada_w.shape[0]
    nj = 6
    return pl.pallas_call(
        _adaln_body,
        grid=(depth, nj),
        in_specs=[
            pl.BlockSpec((N_COND, D_MODEL), lambda l, j: (0, 0)),
            pl.BlockSpec((None, D_MODEL, D_MODEL), lambda l, j: (l, 0, j)),
            pl.BlockSpec((None, 1, D_MODEL), lambda l, j: (l, 0, j)),
        ],
        out_specs=pl.BlockSpec((None, N_COND, D_MODEL), lambda l, j: (l, 0, j)),
        out_shape=jax.ShapeDtypeStruct((depth, N_COND, 6 * D_MODEL), F32),
        compiler_params=_cparams(("arbitrary", "arbitrary")),
        name="adaln",
    )(cond, ada_w, ada_b.reshape(depth, 1, 6 * D_MODEL))


TM_PROJ = 1024


def _stream_tile(x_refs, tm):
    if len(x_refs) == 1:
        return x_refs[0][...]
    is_prompt = pl.program_id(0) < N_PROMPT_TOK // tm
    return jnp.where(is_prompt, x_refs[0][...], x_refs[1][...])


N_CHUNK_ROWS = N_TOK // S5_CHUNK
LANE_TILES = S5_W // 128


def _inproj_odd_body(x_ref, m_ref, g_ref, w_ref, qkv_ref, tu_ref, u_ref):
    m = m_ref[...]
    h = _normmod(x_ref[...], g_ref[...], m[0:1], m[1:2])
    acc = _dot(h.astype(BF16), w_ref[...])
    qkv_ref[...] = acc[:, 0:3 * HALF].astype(BF16)
    for t in range(LANE_TILES):
        u_ref[t] = acc[:, 3 * HALF + t * 128:3 * HALF + (t + 1) * 128]
    rows = TM_PROJ // S5_CHUNK
    for j in range(S5_CHUNK):
        uj = jnp.concatenate([u_ref[t, pl.ds(j, rows, stride=S5_CHUNK), :] for t in range(LANE_TILES)], axis=1)
        if j:
            uj = pltpu.roll(uj, j * S5_CH, 1)
        for t in range(LANE_TILES):
            tu_ref[j, t] = uj[:, t * 128:(t + 1) * 128].astype(BF16)


def _inproj_odd(x, mods, layer, g, w):
    tm = TM_PROJ
    rows = tm // S5_CHUNK
    return pl.pallas_call(
        _inproj_odd_body,
        grid=(N_TOK // tm,),
        in_specs=[
            pl.BlockSpec((tm, D_MODEL), lambda i: (i, 0)),
            pl.BlockSpec((None, None, 6, D_MODEL), lambda i: (layer, _cond_row(i, tm), 0, 0)),
            pl.BlockSpec((1, D_MODEL), lambda i: (0, 0)),
            pl.BlockSpec((D_MODEL, 4 * HALF), lambda i: (0, 0)),
        ],
        out_specs=[
            pl.BlockSpec((tm, 3 * HALF), lambda i: (i, 0)),
            pl.BlockSpec((S5_CHUNK, LANE_TILES, rows, 128), lambda i: (0, 0, i, 0)),
        ],
        out_shape=[
            jax.ShapeDtypeStruct((N_TOK, 3 * HALF), BF16),
            jax.ShapeDtypeStruct((S5_CHUNK, LANE_TILES, N_CHUNK_ROWS, 128), BF16),
        ],
        scratch_shapes=[pltpu.VMEM((LANE_TILES, tm, 128), F32)],
        compiler_params=_cparams(("parallel",)),
        name="inproj_odd",
    )(x, mods, g.reshape(1, D_MODEL), w)


TM_MIX = 256
HALO = 16
RING = 3
LAG = 2
PAD_ROWS = TM_MIX + 2 * HALO + 8


def _even_step(phase, x_refs, m_ref, g_ref, w_ref, pw_ref, ps_ref, cw_ref, y_ref, xcat_ref, proj_ref, tail_ref,
               pad_ref, h_ref, t1_ref, t2_ref):
    i = pl.program_id(0)
    write_slot, cur_slot, next_slot = phase, (phase + 1) % RING, (phase + 2) % RING

    m = m_ref[...]
    x = _stream_tile(x_refs, TM_MIX)
    xcat_ref[...] = x
    h_ref[...] = _normmod(x, g_ref[...], m[0:1], m[1:2]).astype(BF16)
    n_pieces = len(POOL_WINDOWS) + HALF // 128
    pw = (4 * HALF) // n_pieces

    def project(piece):
        cols = slice(piece * pw, (piece + 1) * pw)
        proj_ref[write_slot, :, cols] = _dot(h_ref[...], w_ref[:, cols])

    t = jnp.maximum(i - LAG, 0)
    n_prompt_tiles = N_PROMPT_TOK // TM_MIX
    tiles_per_seq = DEC_SEQ // TM_MIX
    is_prompt = t < n_prompt_tiles
    tile_in_seq = jnp.where(is_prompt, 0, (t - n_prompt_tiles) % tiles_per_seq)
    seq_len = jnp.where(is_prompt, SEQ, DEC_SEQ)
    has_prev = tile_in_seq > 0
    has_next = jnp.logical_and(jnp.logical_not(is_prompt), tile_in_seq < tiles_per_seq - 1)
    prev_keep = jnp.where(has_prev, 1.0, 0.0).astype(F32)
    next_keep = jnp.where(has_next, 1.0, 0.0).astype(F32)
    pos = tile_in_seq * TM_MIX + lax.broadcasted_iota(jnp.int32, (TM_MIX, 1), 0)
    lo_rows, hi_rows = HALO, HALO + TM_MIX
    head = slice(0, HALO)

    for gi, w in enumerate(POOL_WINDOWS):
        project(gi)
        cs = slice(gi * POOL_GROUP, (gi + 1) * POOL_GROUP)
        pad_ref[0:HALO, cs] = tail_ref[:, cs] * prev_keep
        pad_ref[lo_rows:hi_rows, cs] = proj_ref[cur_slot, :, cs]
        pad_ref[hi_rows:hi_rows + HALO, cs] = proj_ref[next_slot, head, cs] * next_keep
        src, n_rows, span = pad_ref, PAD_ROWS, 1
        stage = 0
        while 2 * span < w:
            dst = (t1_ref, t2_ref)[stage % 2]
            n_rows -= 8
            dst[0:n_rows, cs] = src[0:n_rows, cs] + src[span:span + n_rows, cs]
            src, span, stage = dst, 2 * span, stage + 1
        s = src[lo_rows - span:hi_rows - span, cs] + src[lo_rows:hi_rows, cs]
        lo = jnp.maximum(pos - w // 2, 0)
        hi = jnp.minimum(pos + (w - w // 2), seq_len)
        inv = 1.0 / (hi - lo).astype(F32)
        pooled = s * inv - pad_ref[lo_rows:hi_rows, cs]
        mixed = _dot(pooled.astype(BF16), pw_ref[gi]) * ps_ref[:, cs]
        y_ref[:, cs] = mixed.astype(BF16)

    for ci in range(HALF // 128):
        project(len(POOL_WINDOWS) + ci)
        cs = slice(ci * 128, (ci + 1) * 128)
        bs = slice(HALF + ci * 128, HALF + (ci + 1) * 128)
        gs = slice(2 * HALF + ci * 128, 2 * HALF + (ci + 1) * 128)
        vs = slice(3 * HALF + ci * 128, 3 * HALF + (ci + 1) * 128)
        pad_ref[0:HALO, cs] = (tail_ref[:, gs] * tail_ref[:, vs]) * prev_keep
        pad_ref[lo_rows:hi_rows, cs] = proj_ref[cur_slot, :, gs] * proj_ref[cur_slot, :, vs]
        pad_ref[hi_rows:hi_rows + HALO, cs] = (proj_ref[next_slot, head, gs] * proj_ref[next_slot, head, vs]) * next_keep
        conv = (cw_ref[0:1, cs] * pad_ref[lo_rows - 1:hi_rows - 1, cs]
                + cw_ref[1:2, cs] * pad_ref[lo_rows:hi_rows, cs]
                + cw_ref[2:3, cs] * pad_ref[lo_rows + 1:hi_rows + 1, cs])
        y_ref[:, bs] = (proj_ref[cur_slot, :, bs] * conv).astype(BF16)

    tail_ref[...] = proj_ref[cur_slot, TM_MIX - HALO:TM_MIX, :]


def _even_body(n_x, *refs):
    x_refs, rest = refs[:n_x], refs[n_x:]
    proj_ref, tail_ref, pad_ref = rest[-6], rest[-5], rest[-4]
    i = pl.program_id(0)

    @pl.when(i == 0)
    def _():
        proj_ref[...] = jnp.zeros_like(proj_ref)
        tail_ref[...] = jnp.zeros_like(tail_ref)
        pad_ref[...] = jnp.zeros_like(pad_ref)

    for phase in range(RING):
        @pl.when(i % RING == phase)
        def _(phase=phase):
            _even_step(phase, x_refs, *rest)


def _even_layer_mix(xs, mods, layer, g, w, pool_w, pool_scale, conv_w):
    tm = TM_MIX
    n_tiles = N_TOK // tm
    npt = N_PROMPT_TOK // tm
    last = n_tiles - 1
    if len(xs) == 1:
        x_specs = [pl.BlockSpec((tm, D_MODEL), lambda i: (jnp.minimum(i, last), 0))]
    else:
        x_specs = [pl.BlockSpec((tm, D_MODEL), lambda i: (jnp.minimum(i, npt - 1), 0)),
                   pl.BlockSpec((tm, D_MODEL), lambda i: (jnp.clip(i - npt, 0, last - npt), 0))]
    return pl.pallas_call(
        functools.partial(_even_body, len(xs)),
        grid=(n_tiles + LAG,),
        in_specs=x_specs + [
            pl.BlockSpec((None, None, 6, D_MODEL), lambda i: (layer, _cond_row(jnp.minimum(i, last), tm), 0, 0)),
            pl.BlockSpec((1, D_MODEL), lambda i: (0, 0)),
            pl.BlockSpec((D_MODEL, 4 * HALF), lambda i: (0, 0)),
            pl.BlockSpec((len(POOL_WINDOWS), POOL_GROUP, POOL_GROUP), lambda i: (0, 0, 0)),
            pl.BlockSpec((1, HALF), lambda i: (0, 0)),
            pl.BlockSpec((3, HALF), lambda i: (0, 0)),
        ],
        out_specs=[pl.BlockSpec((tm, 2 * HALF), lambda i: (jnp.maximum(i - LAG, 0), 0)),
                   pl.BlockSpec((tm, D_MODEL), lambda i: (jnp.minimum(i, last), 0))],
        out_shape=[jax.ShapeDtypeStruct((N_TOK, 2 * HALF), BF16), jax.ShapeDtypeStruct((N_TOK, D_MODEL), F32)],
        scratch_shapes=[pltpu.VMEM((RING, tm, 4 * HALF), F32), pltpu.VMEM((HALO, 4 * HALF), F32),
                        pltpu.VMEM((PAD_ROWS, HALF), F32), pltpu.VMEM((tm, D_MODEL), BF16),
                        pltpu.VMEM((PAD_ROWS - 8, HALF), F32), pltpu.VMEM((PAD_ROWS - 8, HALF), F32)],
        compiler_params=_cparams(("arbitrary",)),
        name="even_layer_mix",
    )(*xs, mods, g.reshape(1, D_MODEL), w, pool_w.astype(BF16), pool_scale.reshape(1, HALF), conv_w)


TM_MLP = 1024
RC_MLP = 256
N_CAST = 16


def _mlp_tile(final, x_ref, ya_ref, yb_ref, m_ref, g_ref, fg_ref, o_ref, wo_ref, w1_ref, w2_ref):
    m = m_ref[...]
    gain = g_ref[...] * (1.0 + m[4:5])
    for r in range(TM_MLP // RC_MLP):
        rows = slice(r * RC_MLP, (r + 1) * RC_MLP)
        mix = _dot(ya_ref[rows, :], wo_ref[0:HALF, :]) + _dot(yb_ref[rows, :], wo_ref[HALF:2 * HALF, :])
        x1 = x_ref[rows, :] + m[2:3] * mix
        ms = jnp.mean(x1 * x1, axis=-1, keepdims=True)
        h = (x1 * lax.rsqrt(ms + EPS) * gain + m[3:4]).astype(BF16)
        t = _dot(h, w1_ref[...])
        t = jnp.square(jnp.maximum(t, 0.0)).astype(BF16)
        out = x1 + m[5:6] * _dot(t, w2_ref[...])
        if final:
            ms = jnp.mean(out * out, axis=-1, keepdims=True)
            out = out * lax.rsqrt(ms + EPS) * fg_ref[...]
        o_ref[rows, :] = out


def _mlp_resident_body(final, x_ref, ya_ref, yb_ref, wo_ref, m_ref, g_ref, fg_ref, w1_ref, w2_ref, o_ref):
    _mlp_tile(final, x_ref, ya_ref, yb_ref, m_ref, g_ref, fg_ref, o_ref, wo_ref, w1_ref, w2_ref)


def _mlp_cast_body(final, emit, x_ref, ya_ref, yb_ref, wo_f, m_ref, g_ref, fg_ref, w1_f, w2_f, o_ref, *rest):
    wo_ref, w1_ref, w2_ref = rest[-3:]
    emitted = rest[:-3]
    i = pl.program_id(0)

    @pl.when(i < N_CAST)
    def _():
        for k, (f_ref, b_ref) in enumerate(((wo_f, wo_ref), (w1_f, w1_ref), (w2_f, w2_ref))):
            n = f_ref.shape[0]
            slab = f_ref[...].astype(BF16)
            b_ref[pl.ds(pl.multiple_of(i * n, n), n), :] = slab
            if emit:
                emitted[k][...] = slab

    @pl.when(i >= N_CAST)
    def _():
        _mlp_tile(final, x_ref, ya_ref, yb_ref, m_ref, g_ref, fg_ref, o_ref, wo_ref, w1_ref, w2_ref)


def _resident(shape):
    zeros = (0,) * len(shape)
    return pl.BlockSpec(shape, lambda i: zeros, pipeline_mode=pl.Buffered(1))


def _mlp_call(x, x_off, ya, ya_off, ya_col, yb, yb_off, yb_col, n_tiles, tile0, mods, layer, g2, final_g, final,
              f32_weights=None, bf16_weights=None, emit=False):
    tm = TM_MLP
    lead = N_CAST if f32_weights is not None else 0

    def tile(i):
        return jnp.maximum(i - lead, 0)

    stream_specs = [
        pl.BlockSpec((tm, D_MODEL), lambda i: (tile(i) + x_off, 0)),
        pl.BlockSpec((tm, HALF), lambda i: (tile(i) + ya_off, ya_col)),
        pl.BlockSpec((tm, HALF), lambda i: (tile(i) + yb_off, yb_col)),
    ]
    mod_spec = pl.BlockSpec((None, None, 6, D_MODEL), lambda i: (layer, _cond_row(tile(i) + tile0, tm), 0, 0))
    out_spec = pl.BlockSpec((tm, D_MODEL), lambda i: (tile(i), 0))
    out_shape = jax.ShapeDtypeStruct((n_tiles * tm, D_MODEL), F32)
    vecs = (g2.reshape(1, D_MODEL), final_g.reshape(1, D_MODEL))
    if f32_weights is None:
        w_out, w1, w2 = bf16_weights
        return pl.pallas_call(
            functools.partial(_mlp_resident_body, final),
            grid=(n_tiles,),
            in_specs=stream_specs + [_resident((D_MODEL, D_MODEL)), mod_spec, _resident((1, D_MODEL)),
                                     _resident((1, D_MODEL)), _resident((D_MODEL, D_FF)), _resident((D_FF, D_MODEL))],
            out_specs=out_spec,
            out_shape=out_shape,
            compiler_params=_cparams(("parallel",)),
            name="mlp",
        )(x, ya, yb, w_out, mods, *vecs, w1, w2)

    wo_stack, wo_idx, w1_stack, w2_stack = f32_weights

    def slab(i):
        return jnp.minimum(i, N_CAST - 1)

    shapes = ((D_MODEL, D_MODEL), (D_MODEL, D_FF), (D_FF, D_MODEL))
    slab_rows = [rows // N_CAST for rows, _ in shapes]
    f32_specs = [pl.BlockSpec((None, n, cols), lambda i, idx=idx: (idx, slab(i), 0))
                 for n, (_, cols), idx in zip(slab_rows, shapes, (wo_idx, layer, layer))]
    out_specs, out_shapes = [out_spec], [out_shape]
    if emit:
        out_specs += [pl.BlockSpec((n, cols), lambda i: (slab(i), 0)) for n, (_, cols) in zip(slab_rows, shapes)]
        out_shapes += [jax.ShapeDtypeStruct(shp, BF16) for shp in shapes]
    return pl.pallas_call(
        functools.partial(_mlp_cast_body, final, emit),
        grid=(N_CAST + n_tiles,),
        in_specs=stream_specs + [f32_specs[0], mod_spec, _resident((1, D_MODEL)), _resident((1, D_MODEL)),
                                 f32_specs[1], f32_specs[2]],
        out_specs=out_specs,
        out_shape=out_shapes,
        scratch_shapes=[pltpu.VMEM(shp, BF16) for shp in shapes],
        compiler_params=_cparams(("arbitrary",)),
        name="mlp_cast",
    )(x, ya, yb, wo_stack, mods, *vecs, w1_stack, w2_stack)


def _mlp(x, ya, ya_col, yb, yb_col, w_out_stack, w_out_idx, mods, layer, g2, final_g, w1_stack, w2_stack, final):
    tm = TM_MLP
    npt = N_PROMPT_TOK // tm
    f32_weights = (w_out_stack, w_out_idx, w1_stack, w2_stack)
    common = (mods, layer, g2, final_g, final)
    if not final:
        if isinstance(ya, tuple):
            ya = jnp.concatenate(ya, axis=0)
        return tuple(_mlp_call(x, 0, ya, 0, ya_col, yb, 0, yb_col, N_TOK // tm, 0, *common, f32_weights=f32_weights))
    ya_p, ya_s = ((ya[0], 0), (ya[1], 0)) if isinstance(ya, tuple) else ((ya, 0), (ya, npt))
    y_p, *bf16_weights = _mlp_call(x, 0, ya_p[0], ya_p[1], ya_col, yb, 0, yb_col, npt, 0, *common,
                                   f32_weights=f32_weights, emit=True)
    y_s = _mlp_call(x, npt, ya_s[0], ya_s[1], ya_col, yb, npt, yb_col, N_TOK // tm - npt, npt, *common,
                    bf16_weights=bf16_weights)
    return y_p, y_s


def _ctx_attn_body(q_ref, k_ref, v_ref, o_ref, ck_ref, cv_ref):
    lane = lax.broadcasted_iota(jnp.int32, (1, 2 * HEAD_DIM), 1)
    own = [lane < HEAD_DIM, lane >= HEAD_DIM]
    for h in range(N_HEADS):
        sl = slice(h * HEAD_DIM, (h + 1) * HEAD_DIM)
        ck_ref[h] = k_ref[:, sl].astype(F32)
        cv_ref[h] = v_ref[:, sl].astype(F32)
    for hp in range(N_HEADS // 2):
        sl = slice(hp * 2 * HEAD_DIM, (hp + 1) * 2 * HEAD_DIM)
        q = q_ref[:, sl] * jnp.asarray(HEAD_DIM ** -0.5, BF16)
        k = k_ref[:, sl]
        v = v_ref[:, sl]
        q2 = jnp.concatenate([jnp.where(own[0], q, jnp.zeros_like(q)),
                              jnp.where(own[1], q, jnp.zeros_like(q))], axis=0)
        s = _dot_nt(q2, k)
        m = jnp.max(s, axis=-1, keepdims=True)
        p = jnp.exp(s - m).astype(BF16)
        out = None
        for hh in range(2):
            num = _dot(p[hh * SEQ:(hh + 1) * SEQ], jnp.where(own[hh], v, jnp.ones_like(v)))
            den = jnp.where(own[hh], pltpu.roll(num, HEAD_DIM, 1), 1.0)
            o = num / den
            out = o if out is None else jnp.where(own[0], out, o)
        o_ref[:, sl] = out.astype(BF16)


def _ctx_attn(proj):
    return pl.pallas_call(
        _ctx_attn_body,
        grid=(BATCH,),
        in_specs=[
            pl.BlockSpec((SEQ, HALF), lambda b: (b, 0)),
            pl.BlockSpec((SEQ, HALF), lambda b: (b, 1)),
            pl.BlockSpec((SEQ, HALF), lambda b: (b, 2)),
        ],
        out_specs=[
            pl.BlockSpec((SEQ, HALF), lambda b: (b, 0)),
            pl.BlockSpec((None, None, N_HEADS, SEQ, HEAD_DIM), lambda b: (b, 0, 0, 0, 0)),
            pl.BlockSpec((None, None, N_HEADS, SEQ, HEAD_DIM), lambda b: (b, 0, 0, 0, 0)),
        ],
        out_shape=[
            jax.ShapeDtypeStruct((N_PROMPT_TOK, HALF), BF16),
            jax.ShapeDtypeStruct((BATCH, 1, N_HEADS, SEQ, HEAD_DIM), F32),
            jax.ShapeDtypeStruct((BATCH, 1, N_HEADS, SEQ, HEAD_DIM), F32),
        ],
        compiler_params=_cparams(("parallel",)),
        name="ctx_attn",
    )(proj, proj, proj)


NA_MASKED = -1e30
NA_QROWS = 4
NA_KROWS = NA_QROWS + WIN_H
NA_NQ = NA_QROWS * GRID_W
NA_NK = NA_KROWS * GRID_W
NA_BLOCKS = GRID_ROWS // NA_QROWS
NA_TYPES = 3


def _na_col_table(rpb):
    qc = jnp.arange(GRID_W)[:, None]
    kc = jnp.arange(GRID_W)[None, :]
    c0 = jnp.clip(qc - WIN_W // 2, 0, GRID_W - WIN_W)
    valid = (kc >= c0) & (kc < c0 + WIN_W)
    coff = jnp.clip(kc - qc, -(WIN_W - 1), WIN_W - 1) + (WIN_W - 1)
    onehot = (coff.reshape(1, -1) == jnp.arange(2 * WIN_W - 1)[:, None]).astype(F32)
    t = jnp.dot(rpb.astype(F32).reshape(N_HEADS * (2 * WIN_H - 1), 2 * WIN_W - 1), onehot,
                precision=lax.Precision.HIGHEST)
    t = t.reshape(N_HEADS, 2 * WIN_H - 1, GRID_W, GRID_W)
    return jnp.where(valid[None, None], t, NA_MASKED)


def _na_window_start(rb):
    return jnp.clip(NA_QROWS * rb - WIN_H // 2, 0, GRID_ROWS - NA_KROWS)


def _na_body(q_ref, k_ref, v_ref, ck_ref, cv_ref, tab_ref, o_ref, bias_ref, kc_ref, vc_ref, va_ref):
    @pl.when(pl.program_id(1) == 0)
    def _():
        masked = jnp.full((GRID_W, GRID_W), NA_MASKED, F32)
        for hh in range(2):
            for ty in range(NA_TYPES):
                q0 = (0, NA_QROWS, GRID_ROWS - NA_QROWS)[ty]
                ks = (0, 0, GRID_ROWS - NA_KROWS)[ty]
                for a in range(NA_QROWS):
                    r0 = min(max(q0 + a - WIN_H // 2, 0), GRID_ROWS - WIN_H)
                    for y in range(NA_KROWS):
                        kr = ks + y
                        ok = r0 <= kr < r0 + WIN_H
                        blk = tab_ref[hh, kr - (q0 + a) + WIN_H - 1] if ok else masked
                        bias_ref[ty, hh * NA_NQ + a * GRID_W:hh * NA_NQ + (a + 1) * GRID_W,
                                 y * GRID_W:(y + 1) * GRID_W] = blk

    lane = lax.broadcasted_iota(jnp.int32, (1, 2 * HEAD_DIM), 1)
    own = [lane < HEAD_DIM, lane >= HEAD_DIM]
    kc_ref[...] = jnp.concatenate([ck_ref[0], ck_ref[1]], axis=1).astype(BF16)
    vc = jnp.concatenate([cv_ref[0], cv_ref[1]], axis=1)
    v = v_ref[...]
    for hh in range(2):
        vc_ref[hh] = jnp.where(own[hh], vc, 1.0).astype(BF16)
        va_ref[hh] = jnp.where(own[hh], v, jnp.ones_like(v))

    def block(rb, ty):
        qs = pl.multiple_of(rb * NA_NQ, NA_NQ)
        ks = pl.multiple_of(_na_window_start(rb) * GRID_W, GRID_W)
        q = q_ref[pl.ds(qs, NA_NQ), :] * jnp.asarray(HEAD_DIM ** -0.5, BF16)
        q2 = jnp.concatenate([jnp.where(own[0], q, jnp.zeros_like(q)),
                              jnp.where(own[1], q, jnp.zeros_like(q))], axis=0)
        s_loc = _dot_nt(q2, k_ref[pl.ds(ks, NA_NK), :]) + bias_ref[ty]
        s_ctx = _dot_nt(q2, kc_ref[...])
        m = jnp.maximum(jnp.max(s_loc, axis=-1, keepdims=True), jnp.max(s_ctx, axis=-1, keepdims=True))
        p_loc = jnp.exp(s_loc - m).astype(BF16)
        p_ctx = jnp.exp(s_ctx - m).astype(BF16)
        out = None
        for hh in range(2):
            rows = slice(hh * NA_NQ, (hh + 1) * NA_NQ)
            num = _dot(p_loc[rows], va_ref[hh, pl.ds(ks, NA_NK), :]) + _dot(p_ctx[rows], vc_ref[hh])
            den = jnp.where(own[hh], pltpu.roll(num, HEAD_DIM, 1), 1.0)
            o = num / den
            out = o if out is None else jnp.where(own[0], out, o)
        o_ref[pl.ds(qs, NA_NQ), :] = out.astype(BF16)

    block(0, 0)

    def interior(rb, carry):
        block(rb, 1)
        return carry

    lax.fori_loop(1, NA_BLOCKS - 1, interior, 0, unroll=3)
    block(NA_BLOCKS - 1, 2)


def _na_attn(proj, cache_k, cache_v, col_table, layer_idx):
    row_blk0 = N_PROMPT_TOK // DEC_SEQ
    hp = N_HEADS // 2
    return pl.pallas_call(
        _na_body,
        grid=(hp, DEC_BATCH),
        in_specs=[
            pl.BlockSpec((DEC_SEQ, 128), lambda h, b: (row_blk0 + b, h)),
            pl.BlockSpec((DEC_SEQ, 128), lambda h, b: (row_blk0 + b, hp + h)),
            pl.BlockSpec((DEC_SEQ, 128), lambda h, b: (row_blk0 + b, 2 * hp + h)),
            pl.BlockSpec((None, None, 2, PAST_LEN, HEAD_DIM), lambda h, b: (b, layer_idx, h, 0, 0)),
            pl.BlockSpec((None, None, 2, PAST_LEN, HEAD_DIM), lambda h, b: (b, layer_idx, h, 0, 0)),
            pl.BlockSpec((2, 2 * WIN_H - 1, GRID_W, GRID_W), lambda h, b: (h, 0, 0, 0)),
        ],
        out_specs=pl.BlockSpec((DEC_SEQ, 128), lambda h, b: (b, h)),
        out_shape=jax.ShapeDtypeStruct((N_SAMPLE_TOK, HALF), BF16),
        scratch_shapes=[pltpu.VMEM((NA_TYPES, 2 * NA_NQ, NA_NK), F32),
                        pltpu.VMEM((PAST_LEN, 2 * HEAD_DIM), BF16),
                        pltpu.VMEM((2, PAST_LEN, 2 * HEAD_DIM), BF16),
                        pltpu.VMEM((2, DEC_SEQ, 2 * HEAD_DIM), BF16)],
        compiler_params=_cparams(("parallel", "arbitrary")),
        name="na_attn",
    )(proj, proj, proj, cache_k, cache_v, col_table)


N_POW = 24


def _s5_prep_body(vec_ref, bt_ref, c_ref, wend_ref, ktoe_ref, woutT_ref, apow_ref, pw_ref, pct_ref):
    L = S5_CHUNK
    p = pl.program_id(0)
    lane_blk = lax.broadcasted_iota(jnp.int32, (S5_CH, S5_W), 1) // S5_CH
    src_blk = (lane_blk + L - p) % L
    krows = []
    for d in range(2):
        v = vec_ref[d]
        lre, lim, step = v[0:1], v[1:2], v[2:3]
        a = lre * step
        b = lim * step
        ea = jnp.exp(a)
        nr = ea * jnp.cos(b) - 1.0
        ni = ea * jnp.sin(b)
        den = lre * lre + lim * lim
        qr = (nr * lre + ni * lim) / den
        qi = (ni * lre - nr * lim) / den
        bre, bim = bt_ref[d, 0], bt_ref[d, 1]
        bqr = bre * qr - bim * qi
        bqi = bre * qi + bim * qr
        cre, cim = c_ref[d, 0], c_ref[d, 1]

        e = lax.broadcasted_iota(jnp.int32, (N_POW, 1), 0).astype(F32)
        mag = jnp.exp(e * a)
        pw_ref[0] = mag * jnp.cos(e * b)
        pw_ref[1] = mag * jnp.sin(e * b)
        mag = jnp.exp((L * e) * a)
        apow_ref[d, 0] = mag * jnp.cos((L * e) * b)
        apow_ref[d, 1] = mag * jnp.sin((L * e) * b)

        def power(k):
            return pw_ref[0, k:k + 1, :], pw_ref[1, k:k + 1, :]

        def c_block(k):
            pr, pi = power(k)
            return cre * pr - cim * pi, -(cre * pi + cim * pr)

        for i in range(L):
            rows = pl.ds(pl.multiple_of(((i + p) % L) * S5_CH, S5_CH), S5_CH)
            pr, pi = power(L - 1 - i if d == 0 else i)
            wend_ref[rows, d * 256:d * 256 + 128] = (bqr * pr - bqi * pi).astype(BF16)
            wend_ref[rows, d * 256 + 128:d * 256 + 256] = (bqr * pi + bqi * pr).astype(BF16)
            mre, mim = c_block(i + 1 if d == 0 else L - i)
            woutT_ref[rows, d * 256:d * 256 + 128] = mre.astype(BF16)
            woutT_ref[rows, d * 256 + 128:d * 256 + 256] = mim.astype(BF16)
            lag_rows = slice(i * S5_CH, (i + 1) * S5_CH)
            mre, mim = c_block(i if d == 0 else L - 1 - i)
            pct_ref[lag_rows, 0:128] = mre
            pct_ref[lag_rows, 128:256] = mim

        bqcat = jnp.concatenate([bqr, bqi], axis=1)
        krows.append(_dot_nt(bqcat, pct_ref[...], precision=lax.Precision.HIGHEST))

    base_f = pltpu.roll(krows[0], p * S5_CH, 1)
    base_b = pltpu.roll(krows[1], p * S5_CH, 1)
    for i in range(L):
        rows = pl.ds(pl.multiple_of(((i + p) % L) * S5_CH, S5_CH), S5_CH)
        blk_f = base_f if i == 0 else pltpu.roll(base_f, i * S5_CH, 1)
        blk_b = base_b if (i + 1) % L == 0 else pltpu.roll(base_b, ((i + 1) % L) * S5_CH, 1)
        blk = jnp.where(src_blk >= i, blk_f, 0.0) + jnp.where(src_blk <= i, blk_b, 0.0)
        ktoe_ref[rows, :] = blk.astype(BF16)


def _s5_prep(lam_re, lam_im, log_step, b_re, b_im, c_re, c_im):
    def pair_vec(a):
        return a.astype(F32).reshape(2, S5_PAIRS, 1, S5_P).transpose(1, 0, 2, 3)

    step = jnp.broadcast_to(jnp.exp(log_step.astype(F32))[:, :, None], (2, N_SSM_GROUPS, SSM_STATE))
    vec = jnp.concatenate([pair_vec(lam_re), pair_vec(lam_im), pair_vec(step),
                           jnp.zeros((S5_PAIRS, 2, 5, S5_P), F32)], axis=2)

    def block_diag_ch_by_state(m):
        m = m.astype(F32).reshape(2, S5_PAIRS, 2, SSM_GROUP, SSM_STATE)
        eye = jnp.eye(2, dtype=F32)
        out = m[:, :, :, :, None, :] * eye[None, None, :, None, :, None]
        return out.reshape(2, S5_PAIRS, S5_CH, S5_P).transpose(1, 0, 2, 3)

    bt = jnp.stack([block_diag_ch_by_state(b_re.transpose(0, 1, 3, 2)),
                    block_diag_ch_by_state(b_im.transpose(0, 1, 3, 2))], axis=2)
    cc = jnp.stack([block_diag_ch_by_state(c_re), block_diag_ch_by_state(c_im)], axis=2)

    mat = jax.ShapeDtypeStruct((S5_PAIRS, S5_W, S5_W), BF16)
    mat_spec = pl.BlockSpec((None, S5_W, S5_W), lambda p: (p, 0, 0))
    return pl.pallas_call(
        _s5_prep_body,
        grid=(S5_PAIRS,),
        in_specs=[
            pl.BlockSpec((None, 2, 8, S5_P), lambda p: (p, 0, 0, 0)),
            pl.BlockSpec((None, 2, 2, S5_CH, S5_P), lambda p: (p, 0, 0, 0, 0)),
            pl.BlockSpec((None, 2, 2, S5_CH, S5_P), lambda p: (p, 0, 0, 0, 0)),
        ],
        out_specs=[mat_spec, mat_spec, mat_spec,
                   pl.BlockSpec((None, 2, 2, N_POW, S5_P), lambda p: (p, 0, 0, 0, 0))],
        out_shape=[mat, mat, mat, jax.ShapeDtypeStruct((S5_PAIRS, 2, 2, N_POW, S5_P), F32)],
        scratch_shapes=[pltpu.VMEM((2, N_POW, S5_P), F32), pltpu.VMEM((S5_W, 2 * S5_P), F32)],
        compiler_params=_cparams(("parallel",)),
        name="s5_prep",
    )(vec, bt, cc)


S5_SEG_ROWS = 256
S5_N_SEG = N_CHUNK_ROWS // S5_SEG_ROWS
S5_SUB = S5_SEG_ROWS // S5_CHUNK
S5_SUB_PER_SEQ = (DEC_SEQ // S5_CHUNK) // S5_CHUNK


def _s5_body(tu_ref, s0_ref, wend_ref, ktoe_ref, woutT_ref, apow_ref, ty_ref, fin_ref, z_ref, xs_ref, g_ref):
    seg = pl.program_id(0)
    p = pl.program_id(1)
    L = S5_CHUNK
    lane_blk = lax.broadcasted_iota(jnp.int32, (S5_SEG_ROWS, 128), 1) // S5_CH

    tiles = []
    for t in range(LANE_TILES):
        tile = None
        for k in range(128 // S5_CH):
            j = (LANE_TILES * t + k + L - p) % L
            blk = tu_ref[j, t]
            tile = blk if tile is None else jnp.where(lane_blk == k, blk, tile)
        tiles.append(tile)
    s = jnp.concatenate(tiles, axis=1)
    z = _dot(s, wend_ref[...])
    for q in range(LANE_TILES):
        z_ref[q] = z[:, q * 128:(q + 1) * 128]

    ends = []
    for d in range(2):
        are, aim = apow_ref[d, 0, 1:2, :], apow_ref[d, 1, 1:2, :]
        xr = jnp.zeros((S5_SUB, S5_P), F32)
        xi = jnp.zeros((S5_SUB, S5_P), F32)
        for c in (range(L) if d == 0 else reversed(range(L))):
            rows = pl.ds(c, S5_SUB, stride=L)
            xs_ref[2 * d, rows, :] = xr
            xs_ref[2 * d + 1, rows, :] = xi
            zr = z_ref[2 * d, rows, :]
            zi = z_ref[2 * d + 1, rows, :]
            xr, xi = are * xr - aim * xi + zr, are * xi + aim * xr + zi
        ends.append((xr, xi))
        fin_ref[d, 0] = xr
        fin_ref[d, 1] = xi

    @pl.when(seg > 0)
    def _():
        for d in range(2):
            er, ei = ends[d]
            a16r, a16i = apow_ref[d, 0, L:L + 1, :], apow_ref[d, 1, L:L + 1, :]
            for bb in range(S5_SUB // S5_SUB_PER_SEQ):
                gr = s0_ref[bb, 2 * d:2 * d + 1, :]
                gi = s0_ref[bb, 2 * d + 1:2 * d + 2, :]
                order = range(S5_SUB_PER_SEQ) if d == 0 else reversed(range(S5_SUB_PER_SEQ))
                prev = None
                for sub in order:
                    row = bb * S5_SUB_PER_SEQ + sub
                    if prev is not None:
                        gr, gi = (a16r * gr - a16i * gi + er[prev:prev + 1, :],
                                  a16r * gi + a16i * gr + ei[prev:prev + 1, :])
                    g_ref[0, row:row + 1, :] = gr
                    g_ref[1, row:row + 1, :] = gi
                    prev = row
            if d == 0:
                pr, pi = apow_ref[d, 0, 0:L, :], apow_ref[d, 1, 0:L, :]
            else:
                pr = jnp.concatenate([apow_ref[d, 0, k:k + 1, :] for k in reversed(range(L))], axis=0)
                pi = jnp.concatenate([apow_ref[d, 1, k:k + 1, :] for k in reversed(range(L))], axis=0)
            for sub in range(S5_SUB):
                rows = slice(sub * L, (sub + 1) * L)
                gr = g_ref[0, sub:sub + 1, :]
                gi = g_ref[1, sub:sub + 1, :]
                xs_ref[2 * d, rows, :] = xs_ref[2 * d, rows, :] + (pr * gr - pi * gi)
                xs_ref[2 * d + 1, rows, :] = xs_ref[2 * d + 1, rows, :] + (pr * gi + pi * gr)

    xs = jnp.concatenate([xs_ref[q].astype(BF16) for q in range(LANE_TILES)], axis=1)
    y = _dot(s, ktoe_ref[...]) + _dot_nt(xs, woutT_ref[...])

    @pl.when(p == 0)
    def _():
        ty_ref[...] = jnp.zeros_like(ty_ref)

    for t in range(LANE_TILES):
        yt = y[:, t * 128:(t + 1) * 128]
        for k in range(128 // S5_CH):
            j = (LANE_TILES * t + k + L - p) % L
            ty_ref[j, t] = jnp.where(lane_blk == k, yt, ty_ref[j, t])


def _s5_scan(tu, s0, wend, ktoe, woutT, apow):
    mat_spec = pl.BlockSpec((None, S5_W, S5_W), lambda g, p: (p, 0, 0))
    seqs = S5_SUB // S5_SUB_PER_SEQ
    return pl.pallas_call(
        _s5_body,
        grid=(S5_N_SEG, S5_PAIRS),
        in_specs=[
            pl.BlockSpec((S5_CHUNK, LANE_TILES, S5_SEG_ROWS, 128), lambda g, p: (0, 0, g, 0)),
            pl.BlockSpec((None, seqs, 4, S5_P), lambda g, p: (p, jnp.maximum(g - 1, 0), 0, 0)),
            mat_spec, mat_spec, mat_spec,
            pl.BlockSpec((None, 2, 2, N_POW, S5_P), lambda g, p: (p, 0, 0, 0, 0)),
        ],
        out_specs=[
            pl.BlockSpec((S5_CHUNK, LANE_TILES, S5_SEG_ROWS, 128), lambda g, p: (0, 0, g, 0)),
            pl.BlockSpec((None, 2, 2, S5_SUB, S5_P), lambda g, p: (p, 0, 0, g, 0)),
        ],
        out_shape=[
            jax.ShapeDtypeStruct((S5_CHUNK, LANE_TILES, N_CHUNK_ROWS, 128), F32),
            jax.ShapeDtypeStruct((S5_PAIRS, 2, 2, S5_N_SEG * S5_SUB, S5_P), F32),
        ],
        scratch_shapes=[pltpu.VMEM((LANE_TILES, S5_SEG_ROWS, 128), F32),
                        pltpu.VMEM((LANE_TILES, S5_SEG_ROWS, 128), F32),
                        pltpu.VMEM((2, S5_SUB, S5_P), F32)],
        compiler_params=_cparams(("parallel", "arbitrary")),
        name="s5_scan",
    )(tu, s0, wend, ktoe, woutT, apow)


TM_GLU = 1024


def _glu_body(tu_ref, ty_ref, d_ref, gw_ref, gb_ref, o_ref, u_ref, ys_ref):
    rows = TM_GLU // S5_CHUNK
    for j in range(S5_CHUNK):
        back = ((S5_CHUNK - j) % S5_CHUNK) * S5_CH
        uj = jnp.concatenate([tu_ref[j, t].astype(F32) for t in range(LANE_TILES)], axis=1)
        yj = jnp.concatenate([ty_ref[j, t] for t in range(LANE_TILES)], axis=1)
        if back:
            uj = pltpu.roll(uj, back, 1)
            yj = pltpu.roll(yj, back, 1)
        for t in range(LANE_TILES):
            u_ref[t, pl.ds(j, rows, stride=S5_CHUNK), :] = uj[:, t * 128:(t + 1) * 128]
            ys_ref[t, pl.ds(j, rows, stride=S5_CHUNK), :] = yj[:, t * 128:(t + 1) * 128]
    u = jnp.concatenate([u_ref[t] for t in range(LANE_TILES)], axis=1)
    y = d_ref[...] * u + jnp.concatenate([ys_ref[t] for t in range(LANE_TILES)], axis=1)
    cdf = 0.5 * (1.0 + jnp.tanh(math.sqrt(2.0 / math.pi) * (y + 0.044715 * (y * y * y))))
    y = y * cdf
    gate = jax.nn.sigmoid(_dot(y.astype(BF16), gw_ref[...]) + gb_ref[...])
    o_ref[...] = (y * gate).astype(BF16)


def _glu(tu, ty, ssm_d, glu_w, glu_b):
    tm = TM_GLU
    rows = tm // S5_CHUNK
    lay_spec = pl.BlockSpec((S5_CHUNK, LANE_TILES, rows, 128), lambda i: (0, 0, i, 0))
    return pl.pallas_call(
        _glu_body,
        grid=(N_TOK // tm,),
        in_specs=[
            lay_spec, lay_spec,
            pl.BlockSpec((1, HALF), lambda i: (0, 0)),
            pl.BlockSpec((HALF, HALF), lambda i: (0, 0)),
            pl.BlockSpec((1, HALF), lambda i: (0, 0)),
        ],
        out_specs=pl.BlockSpec((tm, HALF), lambda i: (i, 0)),
        out_shape=jax.ShapeDtypeStruct((N_TOK, HALF), BF16),
        scratch_shapes=[pltpu.VMEM((LANE_TILES, tm, 128), F32), pltpu.VMEM((LANE_TILES, tm, 128), F32)],
        compiler_params=_cparams(("parallel",)),
        name="s5_glu",
    )(tu, ty, ssm_d.reshape(1, HALF), glu_w.astype(BF16), glu_b.reshape(1, HALF))


def kernel(x_prompt, x_sample, c, cache_k, cache_v, state_s5_re, state_s5_im, c_ctx, norm1_g, norm2_g, ada_w, ada_b, mlp_w1, mlp_w2, ab_w_in, pool_w, pool_scale, conv_w, ab_w_out, cd_w_in, na_rpb, ssm_lambda_re, ssm_lambda_im, ssm_log_step, ssm_b_re, ssm_b_im, ssm_c_re, ssm_c_im, ssm_d, glu_w, glu_b, cd_w_out, final_g):
    depth = ada_w.shape[0]
    xs = (x_prompt.reshape(N_PROMPT_TOK, D_MODEL), x_sample.reshape(N_SAMPLE_TOK, D_MODEL))
    cond = jnp.concatenate([c_ctx[None, :], c, jnp.zeros((N_COND - 1 - DEC_BATCH, D_MODEL), F32)], axis=0)
    mods = _adaln(cond, ada_w, ada_b).reshape(depth, N_COND, 6, D_MODEL)

    new_k, new_v, new_re, new_im = [], [], [], []
    for layer in range(depth):
        i = layer // 2
        final = layer == depth - 1
        if layer % 2 == 0:
            y, x = _even_layer_mix(xs, mods, layer, norm1_g[layer], _to_bf16(ab_w_in, i),
                                   pool_w[i], pool_scale[i], conv_w[i])
            ya, ya_col, yb, yb_col = y, 0, y, 1
            w_out_stack = ab_w_out
        else:
            x = xs[0]
            qkv, tu = _inproj_odd(x, mods, layer, norm1_g[layer], _to_bf16(cd_w_in, i))
            o_p, ck, cv = _ctx_attn(qkv)
            new_k.append(ck)
            new_v.append(cv)
            o_s = _na_attn(qkv, cache_k, cache_v, _na_col_table(na_rpb[i]), i)
            attn = (o_p, o_s)

            wend, ktoe, woutT, apow = _s5_prep(ssm_lambda_re[i], ssm_lambda_im[i], ssm_log_step[i],
                                               ssm_b_re[i], ssm_b_im[i], ssm_c_re[i], ssm_c_im[i])

            def pair_state(s):
                return s.astype(F32).reshape(DEC_BATCH, 2, S5_PAIRS, S5_P).transpose(2, 0, 1, 3)

            s0 = jnp.stack([pair_state(state_s5_re[:, i]), pair_state(state_s5_im[:, i])], axis=3)
            s0 = s0.reshape(S5_PAIRS, DEC_BATCH, 4, S5_P)
            ty, fin = _s5_scan(tu, s0, wend, ktoe, woutT, apow)
            fin = fin[:, :, :, :BATCH]
            new_re.append(fin[:, :, 0].transpose(2, 1, 0, 3).reshape(BATCH, 2, N_SSM_GROUPS, SSM_STATE))
            new_im.append(fin[:, :, 1].transpose(2, 1, 0, 3).reshape(BATCH, 2, N_SSM_GROUPS, SSM_STATE))
            d_out = _glu(tu, ty, ssm_d[i], glu_w[i], glu_b[i])
            ya, ya_col, yb, yb_col = attn, 0, d_out, 0
            w_out_stack = cd_w_out
        xs = _mlp(x, ya, ya_col, yb, yb_col, w_out_stack, i, mods, layer, norm2_g[layer], final_g,
                  mlp_w1, mlp_w2, final)

    y_prompt = xs[0].reshape(BATCH, SEQ, D_MODEL)
    y_sample = xs[1].reshape(DEC_BATCH, DEC_SEQ, D_MODEL)
    return (y_prompt, y_sample, jnp.concatenate(new_k, axis=1), jnp.concatenate(new_v, axis=1),
            jnp.stack(new_re, axis=1), jnp.stack(new_im, axis=1))
```

```python
import functools
import math

import jax
import jax.numpy as jnp
from jax import lax
from jax.experimental import pallas as pl
from jax.experimental.pallas import tpu as pltpu

F32 = jnp.float32
BF16 = jnp.bfloat16

D_MODEL = 1024
BATCH = 16
SEQ = 256
DEC_BATCH = 4
DEC_SEQ = 2048
PAST_LEN = 256
GRID_W = 64
GRID_ROWS = DEC_SEQ // GRID_W
HALF = 512
POOL_WINDOWS = (2, 4, 8, 16)
POOL_GROUP = 128
HEAD_DIM = 64
N_HEADS = 8
WIN_H = 8
WIN_W = 16
SSM_GROUP = 16
N_SSM_GROUPS = 32
SSM_STATE = 64
D_FF = 4096
EPS = 1e-6

N_PROMPT_TOK = BATCH * SEQ
N_SAMPLE_TOK = DEC_BATCH * DEC_SEQ
N_TOK = N_PROMPT_TOK + N_SAMPLE_TOK
N_COND = 8

S5_CHUNK = 16
S5_PAIRS = N_SSM_GROUPS // 2
S5_CH = 2 * SSM_GROUP
S5_P = 2 * SSM_STATE
S5_W = S5_CHUNK * S5_CH
S5_BP_SAMPLE = 8

VMEM_LIMIT = 56 * 1024 * 1024


def _cparams(sem):
    return pltpu.CompilerParams(dimension_semantics=sem, vmem_limit_bytes=VMEM_LIMIT)


def _cond_row(i, tm):
    npt = N_PROMPT_TOK // tm
    per = DEC_SEQ // tm
    return jnp.where(i < npt, 0, 1 + (i - npt) // per)


def _resident_layer(shape, idx):
    zeros = (0,) * len(shape)
    return pl.BlockSpec((None,) + tuple(shape), lambda i: (idx,) + zeros, pipeline_mode=pl.Buffered(1))


def _normmod(x, g, shift, scale):
    ms = jnp.mean(x * x, axis=-1, keepdims=True)
    return (x * lax.rsqrt(ms + EPS) * g) * (1.0 + scale) + shift


def _dot(a, b):
    return jnp.dot(a, b, preferred_element_type=F32)


def _dot_nt(a, b, precision=None):
    return lax.dot_general(a, b, (((1,), (1,)), ((), ())), preferred_element_type=F32, precision=precision)


def _adaln_body(c_ref, w_ref, b_ref, o_ref):
    c = c_ref[...]
    s = c * jax.nn.sigmoid(c)
    o_ref[...] = _dot(s.astype(BF16), w_ref[...].astype(BF16)) + b_ref[...]


def _adaln(cond, ada_w, ada_b):
    depth = ada_w.shape[0]
    nj = 6
    return pl.pallas_call(
        _adaln_body,
        grid=(depth, nj),
        in_specs=[
            pl.BlockSpec((N_COND, D_MODEL), lambda l, j: (0, 0)),
            pl.BlockSpec((None, D_MODEL, D_MODEL), lambda l, j: (l, 0, j)),
            pl.BlockSpec((None, 1, D_MODEL), lambda l, j: (l, 0, j)),
        ],
        out_specs=pl.BlockSpec((None, N_COND, D_MODEL), lambda l, j: (l, 0, j)),
        out_shape=jax.ShapeDtypeStruct((depth, N_COND, 6 * D_MODEL), F32),
        compiler_params=_cparams(("arbitrary", "arbitrary")),
        name="adaln",
    )(cond, ada_w, ada_b.reshape(depth, 1, 6 * D_MODEL))


TM_PROJ = 1024


def _stream_tile(x_refs, tm):
    if len(x_refs) == 1:
        return x_refs[0][...]
    is_prompt = pl.program_id(0) < N_PROMPT_TOK // tm
    return jnp.where(is_prompt, x_refs[0][...], x_refs[1][...])


N_CHUNK_ROWS = N_TOK // S5_CHUNK
LANE_TILES = S5_W // 128


def _inproj_odd_body(x_ref, m_ref, g_ref, w_ref, qkv_ref, tu_ref, u_ref, wbf_ref):
    @pl.when(pl.program_id(0) == 0)
    def _():
        wbf_ref[...] = w_ref[...].astype(BF16)

    m = m_ref[...]
    h = _normmod(x_ref[...], g_ref[...], m[0:1], m[1:2])
    acc = _dot(h.astype(BF16), wbf_ref[...])
    qkv_ref[...] = acc[:, 0:3 * HALF].astype(BF16)
    for t in range(LANE_TILES):
        u_ref[t] = acc[:, 3 * HALF + t * 128:3 * HALF + (t + 1) * 128]
    rows = TM_PROJ // S5_CHUNK
    for j in range(S5_CHUNK):
        uj = jnp.concatenate([u_ref[t, pl.ds(j, rows, stride=S5_CHUNK), :] for t in range(LANE_TILES)], axis=1)
        if j:
            uj = pltpu.roll(uj, j * S5_CH, 1)
        for t in range(LANE_TILES):
            tu_ref[j, t] = uj[:, t * 128:(t + 1) * 128].astype(BF16)


def _inproj_odd(x, mods, layer, g, w_stack, w_idx):
    tm = TM_PROJ
    rows = tm // S5_CHUNK
    return pl.pallas_call(
        _inproj_odd_body,
        grid=(N_TOK // tm,),
        in_specs=[
            pl.BlockSpec((tm, D_MODEL), lambda i: (i, 0)),
            pl.BlockSpec((None, None, 6, D_MODEL), lambda i: (layer, _cond_row(i, tm), 0, 0)),
            pl.BlockSpec((1, D_MODEL), lambda i: (0, 0)),
            _resident_layer((D_MODEL, 4 * HALF), w_idx),
        ],
        out_specs=[
            pl.BlockSpec((tm, 3 * HALF), lambda i: (i, 0)),
            pl.BlockSpec((S5_CHUNK, LANE_TILES, rows, 128), lambda i: (0, 0, i, 0)),
        ],
        out_shape=[
            jax.ShapeDtypeStruct((N_TOK, 3 * HALF), BF16),
            jax.ShapeDtypeStruct((S5_CHUNK, LANE_TILES, N_CHUNK_ROWS, 128), BF16),
        ],
        scratch_shapes=[pltpu.VMEM((LANE_TILES, tm, 128), F32), pltpu.VMEM((D_MODEL, 4 * HALF), BF16)],
        compiler_params=_cparams(("arbitrary",)),
        name="inproj_odd",
    )(x, mods, g.reshape(1, D_MODEL), w_stack)


TM_MIX = 256
HALO = 16
RING = 3
LAG = 2
PAD_ROWS = TM_MIX + 2 * HALO + 8


def _even_step(phase, x_refs, m_ref, g_ref, w_ref, pw_ref, ps_ref, cw_ref, y_ref, xcat_ref, proj_ref, tail_ref,
               h_ref):
    i = pl.program_id(0)
    write_slot, cur_slot, next_slot = phase, (phase + 1) % RING, (phase + 2) % RING

    m = m_ref[...]
    x = _stream_tile(x_refs, TM_MIX)
    xcat_ref[...] = x
    h_ref[...] = _normmod(x, g_ref[...], m[0:1], m[1:2]).astype(BF16)
    n_pieces = len(POOL_WINDOWS) + HALF // 128
    pw = (4 * HALF) // n_pieces

    def project(piece):
        cols = slice(piece * pw, (piece + 1) * pw)
        proj_ref[write_slot, :, cols] = _dot(h_ref[...], w_ref[:, cols])

    t = jnp.maximum(i - LAG, 0)
    n_prompt_tiles = N_PROMPT_TOK // TM_MIX
    tiles_per_seq = DEC_SEQ // TM_MIX
    is_prompt = t < n_prompt_tiles
    tile_in_seq = jnp.where(is_prompt, 0, (t - n_prompt_tiles) % tiles_per_seq)
    seq_len = jnp.where(is_prompt, SEQ, DEC_SEQ)
    has_prev = tile_in_seq > 0
    has_next = jnp.logical_and(jnp.logical_not(is_prompt), tile_in_seq < tiles_per_seq - 1)
    prev_keep = jnp.where(has_prev, 1.0, 0.0).astype(F32)
    next_keep = jnp.where(has_next, 1.0, 0.0).astype(F32)
    pos = tile_in_seq * TM_MIX + lax.broadcasted_iota(jnp.int32, (TM_MIX, 1), 0)
    lo_rows, hi_rows = HALO, HALO + TM_MIX
    head = slice(0, HALO)

    slack = jnp.zeros((PAD_ROWS - TM_MIX - 2 * HALO, POOL_GROUP), F32)
    for gi, w in enumerate(POOL_WINDOWS):
        project(gi)
        cs = slice(gi * POOL_GROUP, (gi + 1) * POOL_GROUP)
        cur = proj_ref[cur_slot, :, cs]
        f = jnp.concatenate([tail_ref[:, cs] * prev_keep, cur, proj_ref[next_slot, head, cs] * next_keep, slack],
                            axis=0)
        n_rows, span = PAD_ROWS, 1
        while 2 * span < w:
            n_rows -= 8
            f = f[0:n_rows] + f[span:span + n_rows]
            span *= 2
        s = f[lo_rows - span:hi_rows - span] + f[lo_rows:hi_rows]
        lo = jnp.maximum(pos - w // 2, 0)
        hi = jnp.minimum(pos + (w - w // 2), seq_len)
        inv = 1.0 / (hi - lo).astype(F32)
        pooled = s * inv - cur
        mixed = _dot(pooled.astype(BF16), pw_ref[gi]) * ps_ref[:, cs]
        y_ref[:, cs] = mixed.astype(BF16)

    for ci in range(HALF // 128):
        project(len(POOL_WINDOWS) + ci)
        cs = slice(ci * 128, (ci + 1) * 128)
        bs = slice(HALF + ci * 128, HALF + (ci + 1) * 128)
        gs = slice(2 * HALF + ci * 128, 2 * HALF + (ci + 1) * 128)
        vs = slice(3 * HALF + ci * 128, 3 * HALF + (ci + 1) * 128)
        z = jnp.concatenate([(tail_ref[:, gs] * tail_ref[:, vs]) * prev_keep,
                             proj_ref[cur_slot, :, gs] * proj_ref[cur_slot, :, vs],
                             (proj_ref[next_slot, head, gs] * proj_ref[next_slot, head, vs]) * next_keep], axis=0)
        conv = (cw_ref[0:1, cs] * z[lo_rows - 1:hi_rows - 1]
                + cw_ref[1:2, cs] * z[lo_rows:hi_rows]
                + cw_ref[2:3, cs] * z[lo_rows + 1:hi_rows + 1])
        y_ref[:, bs] = (proj_ref[cur_slot, :, bs] * conv).astype(BF16)

    tail_ref[...] = proj_ref[cur_slot, TM_MIX - HALO:TM_MIX, :]


def _even_body(n_x, *refs):
    x_refs, rest = refs[:n_x], refs[n_x:]
    m_ref, g_ref, w_ref = rest[:3]
    proj_ref, tail_ref, h_ref, wbf_ref = rest[-4:]
    i = pl.program_id(0)

    @pl.when(i == 0)
    def _():
        proj_ref[...] = jnp.zeros_like(proj_ref)
        tail_ref[...] = jnp.zeros_like(tail_ref)
        wbf_ref[...] = w_ref[...].astype(BF16)

    for phase in range(RING):
        @pl.when(i % RING == phase)
        def _(phase=phase):
            _even_step(phase, x_refs, m_ref, g_ref, wbf_ref, *rest[3:-1])


def _even_layer_mix(xs, mods, layer, g, w_stack, w_idx, pool_w, pool_scale, conv_w):
    tm = TM_MIX
    n_tiles = N_TOK // tm
    npt = N_PROMPT_TOK // tm
    last = n_tiles - 1
    if len(xs) == 1:
        x_specs = [pl.BlockSpec((tm, D_MODEL), lambda i: (jnp.minimum(i, last), 0))]
    else:
        x_specs = [pl.BlockSpec((tm, D_MODEL), lambda i: (jnp.minimum(i, npt - 1), 0)),
                   pl.BlockSpec((tm, D_MODEL), lambda i: (jnp.clip(i - npt, 0, last - npt), 0))]
    return pl.pallas_call(
        functools.partial(_even_body, len(xs)),
        grid=(n_tiles + LAG,),
        in_specs=x_specs + [
            pl.BlockSpec((None, None, 6, D_MODEL), lambda i: (layer, _cond_row(jnp.minimum(i, last), tm), 0, 0)),
            pl.BlockSpec((1, D_MODEL), lambda i: (0, 0)),
            _resident_layer((D_MODEL, 4 * HALF), w_idx),
            pl.BlockSpec((len(POOL_WINDOWS), POOL_GROUP, POOL_GROUP), lambda i: (0, 0, 0)),
            pl.BlockSpec((1, HALF), lambda i: (0, 0)),
            pl.BlockSpec((3, HALF), lambda i: (0, 0)),
        ],
        out_specs=[pl.BlockSpec((tm, 2 * HALF), lambda i: (jnp.maximum(i - LAG, 0), 0)),
                   pl.BlockSpec((tm, D_MODEL), lambda i: (jnp.minimum(i, last), 0))],
        out_shape=[jax.ShapeDtypeStruct((N_TOK, 2 * HALF), BF16), jax.ShapeDtypeStruct((N_TOK, D_MODEL), F32)],
        scratch_shapes=[pltpu.VMEM((RING, tm, 4 * HALF), F32), pltpu.VMEM((HALO, 4 * HALF), F32),
                        pltpu.VMEM((tm, D_MODEL), BF16), pltpu.VMEM((D_MODEL, 4 * HALF), BF16)],
        compiler_params=_cparams(("arbitrary",)),
        name="even_layer_mix",
    )(*xs, mods, g.reshape(1, D_MODEL), w_stack, pool_w.astype(BF16), pool_scale.reshape(1, HALF), conv_w)


TM_MLP = 1024
RC_MLP = 256
N_CAST = 16


def _mlp_tile(final, x_ref, ya_ref, yb_ref, m_ref, g_ref, fg_ref, o_ref, wo_ref, w1_ref, w2_ref):
    m = m_ref[...]
    gain = g_ref[...] * (1.0 + m[4:5])
    for r in range(TM_MLP // RC_MLP):
        rows = slice(r * RC_MLP, (r + 1) * RC_MLP)
        mix = _dot(ya_ref[rows, :], wo_ref[0:HALF, :]) + _dot(yb_ref[rows, :], wo_ref[HALF:2 * HALF, :])
        x1 = x_ref[rows, :] + m[2:3] * mix
        ms = jnp.mean(x1 * x1, axis=-1, keepdims=True)
        h = (x1 * lax.rsqrt(ms + EPS) * gain + m[3:4]).astype(BF16)
        t = _dot(h, w1_ref[...])
        t = jnp.square(jnp.maximum(t, 0.0)).astype(BF16)
        out = x1 + m[5:6] * _dot(t, w2_ref[...])
        if final:
            ms = jnp.mean(out * out, axis=-1, keepdims=True)
            out = out * lax.rsqrt(ms + EPS) * fg_ref[...]
        o_ref[rows, :] = out


def _mlp_resident_body(final, x_ref, ya_ref, yb_ref, wo_ref, m_ref, g_ref, fg_ref, w1_ref, w2_ref, o_ref):
    _mlp_tile(final, x_ref, ya_ref, yb_ref, m_ref, g_ref, fg_ref, o_ref, wo_ref, w1_ref, w2_ref)


def _mlp_cast_body(final, emit, x_ref, ya_ref, yb_ref, wo_f, m_ref, g_ref, fg_ref, w1_f, w2_f, o_ref, *rest):
    wo_ref, w1_ref, w2_ref = rest[-3:]
    emitted = rest[:-3]
    i = pl.program_id(0)

    @pl.when(i < N_CAST)
    def _():
        for k, (f_ref, b_ref) in enumerate(((wo_f, wo_ref), (w1_f, w1_ref), (w2_f, w2_ref))):
            n = f_ref.shape[0]
            slab = f_ref[...].astype(BF16)
            b_ref[pl.ds(pl.multiple_of(i * n, n), n), :] = slab
            if emit:
                emitted[k][...] = slab

    @pl.when(i >= N_CAST)
    def _():
        _mlp_tile(final, x_ref, ya_ref, yb_ref, m_ref, g_ref, fg_ref, o_ref, wo_ref, w1_ref, w2_ref)


def _resident(shape):
    zeros = (0,) * len(shape)
    return pl.BlockSpec(shape, lambda i: zeros, pipeline_mode=pl.Buffered(1))


def _mlp_call(x, x_off, ya, ya_off, ya_col, yb, yb_off, yb_col, n_tiles, tile0, mods, layer, g2, final_g, final,
              f32_weights=None, bf16_weights=None, emit=False):
    tm = TM_MLP
    lead = N_CAST if f32_weights is not None else 0

    def tile(i):
        return jnp.maximum(i - lead, 0)

    stream_specs = [
        pl.BlockSpec((tm, D_MODEL), lambda i: (tile(i) + x_off, 0)),
        pl.BlockSpec((tm, HALF), lambda i: (tile(i) + ya_off, ya_col)),
        pl.BlockSpec((tm, HALF), lambda i: (tile(i) + yb_off, yb_col)),
    ]
    mod_spec = pl.BlockSpec((None, None, 6, D_MODEL), lambda i: (layer, _cond_row(tile(i) + tile0, tm), 0, 0))
    out_spec = pl.BlockSpec((tm, D_MODEL), lambda i: (tile(i), 0))
    out_shape = jax.ShapeDtypeStruct((n_tiles * tm, D_MODEL), F32)
    vecs = (g2.reshape(1, D_MODEL), final_g.reshape(1, D_MODEL))
    if f32_weights is None:
        w_out, w1, w2 = bf16_weights
        return pl.pallas_call(
            functools.partial(_mlp_resident_body, final),
            grid=(n_tiles,),
            in_specs=stream_specs + [_resident((D_MODEL, D_MODEL)), mod_spec, _resident((1, D_MODEL)),
                                     _resident((1, D_MODEL)), _resident((D_MODEL, D_FF)), _resident((D_FF, D_MODEL))],
            out_specs=out_spec,
            out_shape=out_shape,
            compiler_params=_cparams(("parallel",)),
            name="mlp",
        )(x, ya, yb, w_out, mods, *vecs, w1, w2)

    wo_stack, wo_idx, w1_stack, w2_stack = f32_weights

    def slab(i):
        return jnp.minimum(i, N_CAST - 1)

    shapes = ((D_MODEL, D_MODEL), (D_MODEL, D_FF), (D_FF, D_MODEL))
    slab_rows = [rows // N_CAST for rows, _ in shapes]
    f32_specs = [pl.BlockSpec((None, n, cols), lambda i, idx=idx: (idx, slab(i), 0))
                 for n, (_, cols), idx in zip(slab_rows, shapes, (wo_idx, layer, layer))]
    out_specs, out_shapes = [out_spec], [out_shape]
    if emit:
        out_specs += [pl.BlockSpec((n, cols), lambda i: (slab(i), 0)) for n, (_, cols) in zip(slab_rows, shapes)]
        out_shapes += [jax.ShapeDtypeStruct(shp, BF16) for shp in shapes]
    return pl.pallas_call(
        functools.partial(_mlp_cast_body, final, emit),
        grid=(N_CAST + n_tiles,),
        in_specs=stream_specs + [f32_specs[0], mod_spec, _resident((1, D_MODEL)), _resident((1, D_MODEL)),
                                 f32_specs[1], f32_specs[2]],
        out_specs=out_specs,
        out_shape=out_shapes,
        scratch_shapes=[pltpu.VMEM(shp, BF16) for shp in shapes],
        compiler_params=_cparams(("arbitrary",)),
        name="mlp_cast",
    )(x, ya, yb, wo_stack, mods, *vecs, w1_stack, w2_stack)


def _mlp(x, ya, ya_col, yb, yb_col, w_out_stack, w_out_idx, mods, layer, g2, final_g, w1_stack, w2_stack, final):
    tm = TM_MLP
    npt = N_PROMPT_TOK // tm
    f32_weights = (w_out_stack, w_out_idx, w1_stack, w2_stack)
    common = (mods, layer, g2, final_g, final)
    if not final:
        if isinstance(ya, tuple):
            ya = jnp.concatenate(ya, axis=0)
        return tuple(_mlp_call(x, 0, ya, 0, ya_col, yb, 0, yb_col, N_TOK // tm, 0, *common, f32_weights=f32_weights))
    ya_p, ya_s = ((ya[0], 0), (ya[1], 0)) if isinstance(ya, tuple) else ((ya, 0), (ya, npt))
    y_p, *bf16_weights = _mlp_call(x, 0, ya_p[0], ya_p[1], ya_col, yb, 0, yb_col, npt, 0, *common,
                                   f32_weights=f32_weights, emit=True)
    y_s = _mlp_call(x, npt, ya_s[0], ya_s[1], ya_col, yb, npt, yb_col, N_TOK // tm - npt, npt, *common,
                    bf16_weights=bf16_weights)
    return y_p, y_s


def _ctx_attn_body(q_ref, k_ref, v_ref, o_ref, ck_ref, cv_ref):
    lane = lax.broadcasted_iota(jnp.int32, (1, 2 * HEAD_DIM), 1)
    own = [lane < HEAD_DIM, lane >= HEAD_DIM]
    for h in range(N_HEADS):
        sl = slice(h * HEAD_DIM, (h + 1) * HEAD_DIM)
        ck_ref[h] = k_ref[:, sl].astype(F32)
        cv_ref[h] = v_ref[:, sl].astype(F32)
    for hp in range(N_HEADS // 2):
        sl = slice(hp * 2 * HEAD_DIM, (hp + 1) * 2 * HEAD_DIM)
        q = q_ref[:, sl] * jnp.asarray(HEAD_DIM ** -0.5, BF16)
        k = k_ref[:, sl]
        v = v_ref[:, sl]
        q2 = jnp.concatenate([jnp.where(own[0], q, jnp.zeros_like(q)),
                              jnp.where(own[1], q, jnp.zeros_like(q))], axis=0)
        s = _dot_nt(q2, k)
        m = jnp.max(s, axis=-1, keepdims=True)
        p = jnp.exp(s - m).astype(BF16)
        out = None
        for hh in range(2):
            num = _dot(p[hh * SEQ:(hh + 1) * SEQ], jnp.where(own[hh], v, jnp.ones_like(v)))
            den = jnp.where(own[hh], pltpu.roll(num, HEAD_DIM, 1), 1.0)
            o = num / den
            out = o if out is None else jnp.where(own[0], out, o)
        o_ref[:, sl] = out.astype(BF16)


def _ctx_attn(proj):
    return pl.pallas_call(
        _ctx_attn_body,
        grid=(BATCH,),
        in_specs=[
            pl.BlockSpec((SEQ, HALF), lambda b: (b, 0)),
            pl.BlockSpec((SEQ, HALF), lambda b: (b, 1)),
            pl.BlockSpec((SEQ, HALF), lambda b: (b, 2)),
        ],
        out_specs=[
            pl.BlockSpec((SEQ, HALF), lambda b: (b, 0)),
            pl.BlockSpec((None, None, N_HEADS, SEQ, HEAD_DIM), lambda b: (b, 0, 0, 0, 0)),
            pl.BlockSpec((None, None, N_HEADS, SEQ, HEAD_DIM), lambda b: (b, 0, 0, 0, 0)),
        ],
        out_shape=[
            jax.ShapeDtypeStruct((N_PROMPT_TOK, HALF), BF16),
            jax.ShapeDtypeStruct((BATCH, 1, N_HEADS, SEQ, HEAD_DIM), F32),
            jax.ShapeDtypeStruct((BATCH, 1, N_HEADS, SEQ, HEAD_DIM), F32),
        ],
        compiler_params=_cparams(("parallel",)),
        name="ctx_attn",
    )(proj, proj, proj)


NA_MASKED = -1e30
NA_QROWS = 4
NA_KROWS = NA_QROWS + WIN_H
NA_NQ = NA_QROWS * GRID_W
NA_NK = NA_KROWS * GRID_W
NA_BLOCKS = GRID_ROWS // NA_QROWS
NA_TYPES = 3


def _na_col_table(rpb):
    qc = jnp.arange(GRID_W)[:, None]
    kc = jnp.arange(GRID_W)[None, :]
    c0 = jnp.clip(qc - WIN_W // 2, 0, GRID_W - WIN_W)
    valid = (kc >= c0) & (kc < c0 + WIN_W)
    coff = jnp.clip(kc - qc, -(WIN_W - 1), WIN_W - 1) + (WIN_W - 1)
    onehot = (coff.reshape(1, -1) == jnp.arange(2 * WIN_W - 1)[:, None]).astype(F32)
    t = jnp.dot(rpb.astype(F32).reshape(N_HEADS * (2 * WIN_H - 1), 2 * WIN_W - 1), onehot,
                precision=lax.Precision.HIGHEST)
    t = t.reshape(N_HEADS, 2 * WIN_H - 1, GRID_W, GRID_W)
    return jnp.where(valid[None, None], t, NA_MASKED)


def _na_window_start(rb):
    return jnp.clip(NA_QROWS * rb - WIN_H // 2, 0, GRID_ROWS - NA_KROWS)


def _na_body(q_ref, k_ref, v_ref, ck_ref, cv_ref, tab_ref, o_ref, bias_ref, kc_ref, vc_ref, va_ref):
    @pl.when(pl.program_id(1) == 0)
    def _():
        masked = jnp.full((GRID_W, GRID_W), NA_MASKED, F32)
        for hh in range(2):
            for ty in range(NA_TYPES):
                q0 = (0, NA_QROWS, GRID_ROWS - NA_QROWS)[ty]
                ks = (0, 0, GRID_ROWS - NA_KROWS)[ty]
                for a in range(NA_QROWS):
                    r0 = min(max(q0 + a - WIN_H // 2, 0), GRID_ROWS - WIN_H)
                    for y in range(NA_KROWS):
                        kr = ks + y
                        ok = r0 <= kr < r0 + WIN_H
                        blk = tab_ref[hh, kr - (q0 + a) + WIN_H - 1] if ok else masked
                        bias_ref[ty, hh * NA_NQ + a * GRID_W:hh * NA_NQ + (a + 1) * GRID_W,
                                 y * GRID_W:(y + 1) * GRID_W] = blk

    lane = lax.broadcasted_iota(jnp.int32, (1, 2 * HEAD_DIM), 1)
    own = [lane < HEAD_DIM, lane >= HEAD_DIM]
    kc_ref[...] = jnp.concatenate([ck_ref[0], ck_ref[1]], axis=1).astype(BF16)
    vc = jnp.concatenate([cv_ref[0], cv_ref[1]], axis=1)
    v = v_ref[...]
    for hh in range(2):
        vc_ref[hh] = jnp.where(own[hh], vc, 1.0).astype(BF16)
        va_ref[hh] = jnp.where(own[hh], v, jnp.ones_like(v))

    def block(rb, ty):
        qs = pl.multiple_of(rb * NA_NQ, NA_NQ)
        ks = pl.multiple_of(_na_window_start(rb) * GRID_W, GRID_W)
        q = q_ref[pl.ds(qs, NA_NQ), :] * jnp.asarray(HEAD_DIM ** -0.5, BF16)
        q2 = jnp.concatenate([jnp.where(own[0], q, jnp.zeros_like(q)),
                              jnp.where(own[1], q, jnp.zeros_like(q))], axis=0)
        s_loc = _dot_nt(q2, k_ref[pl.ds(ks, NA_NK), :]) + bias_ref[ty]
        s_ctx = _dot_nt(q2, kc_ref[...])
        m = jnp.maximum(jnp.max(s_loc, axis=-1, keepdims=True), jnp.max(s_ctx, axis=-1, keepdims=True))
        p_loc = jnp.exp(s_loc - m).astype(BF16)
        p_ctx = jnp.exp(s_ctx - m).astype(BF16)
        out = None
        for hh in range(2):
            rows = slice(hh * NA_NQ, (hh + 1) * NA_NQ)
            num = _dot(p_loc[rows], va_ref[hh, pl.ds(ks, NA_NK), :]) + _dot(p_ctx[rows], vc_ref[hh])
            den = jnp.where(own[hh], pltpu.roll(num, HEAD_DIM, 1), 1.0)
            o = num / den
            out = o if out is None else jnp.where(own[0], out, o)
        o_ref[pl.ds(qs, NA_NQ), :] = out.astype(BF16)

    block(0, 0)

    def interior(rb, carry):
        block(rb, 1)
        return carry

    lax.fori_loop(1, NA_BLOCKS - 1, interior, 0, unroll=3)
    block(NA_BLOCKS - 1, 2)


def _na_attn(proj, cache_k, cache_v, col_table, layer_idx):
    row_blk0 = N_PROMPT_TOK // DEC_SEQ
    hp = N_HEADS // 2
    return pl.pallas_call(
        _na_body,
        grid=(hp, DEC_BATCH),
        in_specs=[
            pl.BlockSpec((DEC_SEQ, 128), lambda h, b: (row_blk0 + b, h)),
            pl.BlockSpec((DEC_SEQ, 128), lambda h, b: (row_blk0 + b, hp + h)),
            pl.BlockSpec((DEC_SEQ, 128), lambda h, b: (row_blk0 + b, 2 * hp + h)),
            pl.BlockSpec((None, None, 2, PAST_LEN, HEAD_DIM), lambda h, b: (b, layer_idx, h, 0, 0)),
            pl.BlockSpec((None, None, 2, PAST_LEN, HEAD_DIM), lambda h, b: (b, layer_idx, h, 0, 0)),
            pl.BlockSpec((2, 2 * WIN_H - 1, GRID_W, GRID_W), lambda h, b: (h, 0, 0, 0)),
        ],
        out_specs=pl.BlockSpec((DEC_SEQ, 128), lambda h, b: (b, h)),
        out_shape=jax.ShapeDtypeStruct((N_SAMPLE_TOK, HALF), BF16),
        scratch_shapes=[pltpu.VMEM((NA_TYPES, 2 * NA_NQ, NA_NK), F32),
                        pltpu.VMEM((PAST_LEN, 2 * HEAD_DIM), BF16),
                        pltpu.VMEM((2, PAST_LEN, 2 * HEAD_DIM), BF16),
                        pltpu.VMEM((2, DEC_SEQ, 2 * HEAD_DIM), BF16)],
        compiler_params=_cparams(("parallel", "arbitrary")),
        name="na_attn",
    )(proj, proj, proj, cache_k, cache_v, col_table)


N_POW = 24


def _s5_prep_body(vec_ref, bt_ref, c_ref, wend_ref, ktoe_ref, woutT_ref, apow_ref, pw_ref, pct_ref):
    L = S5_CHUNK
    p = pl.program_id(0)
    lane_blk = lax.broadcasted_iota(jnp.int32, (S5_CH, S5_W), 1) // S5_CH
    src_blk = (lane_blk + L - p) % L
    krows = []
    for d in range(2):
        v = vec_ref[d]
        lre, lim, step = v[0:1], v[1:2], v[2:3]
        a = lre * step
        b = lim * step
        ea = jnp.exp(a)
        nr = ea * jnp.cos(b) - 1.0
        ni = ea * jnp.sin(b)
        den = lre * lre + lim * lim
        qr = (nr * lre + ni * lim) / den
        qi = (ni * lre - nr * lim) / den
        bre, bim = bt_ref[d, 0], bt_ref[d, 1]
        bqr = bre * qr - bim * qi
        bqi = bre * qi + bim * qr
        cre, cim = c_ref[d, 0], c_ref[d, 1]

        e = lax.broadcasted_iota(jnp.int32, (N_POW, 1), 0).astype(F32)
        mag = jnp.exp(e * a)
        pw_ref[0] = mag * jnp.cos(e * b)
        pw_ref[1] = mag * jnp.sin(e * b)
        mag = jnp.exp((L * e) * a)
        apow_ref[d, 0] = mag * jnp.cos((L * e) * b)
        apow_ref[d, 1] = mag * jnp.sin((L * e) * b)

        def power(k):
            return pw_ref[0, k:k + 1, :], pw_ref[1, k:k + 1, :]

        def c_block(k):
            pr, pi = power(k)
            return cre * pr - cim * pi, -(cre * pi + cim * pr)

        for i in range(L):
            rows = pl.ds(pl.multiple_of(((i + p) % L) * S5_CH, S5_CH), S5_CH)
            pr, pi = power(L - 1 - i if d == 0 else i)
            wend_ref[rows, d * 256:d * 256 + 128] = (bqr * pr - bqi * pi).astype(BF16)
            wend_ref[rows, d * 256 + 128:d * 256 + 256] = (bqr * pi + bqi * pr).astype(BF16)
            mre, mim = c_block(i + 1 if d == 0 else L - i)
            woutT_ref[rows, d * 256:d * 256 + 128] = mre.astype(BF16)
            woutT_ref[rows, d * 256 + 128:d * 256 + 256] = mim.astype(BF16)
            lag_rows = slice(i * S5_CH, (i + 1) * S5_CH)
            mre, mim = c_block(i if d == 0 else L - 1 - i)
            pct_ref[lag_rows, 0:128] = mre
            pct_ref[lag_rows, 128:256] = mim

        bqcat = jnp.concatenate([bqr, bqi], axis=1)
        krows.append(_dot_nt(bqcat, pct_ref[...], precision=lax.Precision.HIGHEST))

    base_f = pltpu.roll(krows[0], p * S5_CH, 1)
    base_b = pltpu.roll(krows[1], p * S5_CH, 1)
    for i in range(L):
        rows = pl.ds(pl.multiple_of(((i + p) % L) * S5_CH, S5_CH), S5_CH)
        blk_f = base_f if i == 0 else pltpu.roll(base_f, i * S5_CH, 1)
        blk_b = base_b if (i + 1) % L == 0 else pltpu.roll(base_b, ((i + 1) % L) * S5_CH, 1)
        blk = jnp.where(src_blk >= i, blk_f, 0.0) + jnp.where(src_blk <= i, blk_b, 0.0)
        ktoe_ref[rows, :] = blk.astype(BF16)


def _s5_prep(lam_re, lam_im, log_step, b_re, b_im, c_re, c_im):
    def pair_vec(a):
        return a.astype(F32).reshape(2, S5_PAIRS, 1, S5_P).transpose(1, 0, 2, 3)

    step = jnp.broadcast_to(jnp.exp(log_step.astype(F32))[:, :, None], (2, N_SSM_GROUPS, SSM_STATE))
    vec = jnp.concatenate([pair_vec(lam_re), pair_vec(lam_im), pair_vec(step),
                           jnp.zeros((S5_PAIRS, 2, 5, S5_P), F32)], axis=2)

    def block_diag_ch_by_state(m):
        m = m.astype(F32).reshape(2, S5_PAIRS, 2, SSM_GROUP, SSM_STATE)
        eye = jnp.eye(2, dtype=F32)
        out = m[:, :, :, :, None, :] * eye[None, None, :, None, :, None]
        return out.reshape(2, S5_PAIRS, S5_CH, S5_P).transpose(1, 0, 2, 3)

    bt = jnp.stack([block_diag_ch_by_state(b_re.transpose(0, 1, 3, 2)),
                    block_diag_ch_by_state(b_im.transpose(0, 1, 3, 2))], axis=2)
    cc = jnp.stack([block_diag_ch_by_state(c_re), block_diag_ch_by_state(c_im)], axis=2)

    mat = jax.ShapeDtypeStruct((S5_PAIRS, S5_W, S5_W), BF16)
    mat_spec = pl.BlockSpec((None, S5_W, S5_W), lambda p: (p, 0, 0))
    return pl.pallas_call(
        _s5_prep_body,
        grid=(S5_PAIRS,),
        in_specs=[
            pl.BlockSpec((None, 2, 8, S5_P), lambda p: (p, 0, 0, 0)),
            pl.BlockSpec((None, 2, 2, S5_CH, S5_P), lambda p: (p, 0, 0, 0, 0)),
            pl.BlockSpec((None, 2, 2, S5_CH, S5_P), lambda p: (p, 0, 0, 0, 0)),
        ],
        out_specs=[mat_spec, mat_spec, mat_spec,
                   pl.BlockSpec((None, 2, 2, N_POW, S5_P), lambda p: (p, 0, 0, 0, 0))],
        out_shape=[mat, mat, mat, jax.ShapeDtypeStruct((S5_PAIRS, 2, 2, N_POW, S5_P), F32)],
        scratch_shapes=[pltpu.VMEM((2, N_POW, S5_P), F32), pltpu.VMEM((S5_W, 2 * S5_P), F32)],
        compiler_params=_cparams(("parallel",)),
        name="s5_prep",
    )(vec, bt, cc)


S5_SEG_ROWS = 256
S5_N_SEG = N_CHUNK_ROWS // S5_SEG_ROWS
S5_SUB = S5_SEG_ROWS // S5_CHUNK
S5_SUB_PER_SEQ = (DEC_SEQ // S5_CHUNK) // S5_CHUNK


def _s5_body(tu_ref, s0_ref, wend_ref, ktoe_ref, woutT_ref, apow_ref, ty_ref, fin_ref, z_ref, xs_ref, g_ref):
    seg = pl.program_id(0)
    p = pl.program_id(1)
    L = S5_CHUNK
    lane_blk = lax.broadcasted_iota(jnp.int32, (S5_SEG_ROWS, 128), 1) // S5_CH

    tiles = []
    for t in range(LANE_TILES):
        tile = None
        for k in range(128 // S5_CH):
            j = (LANE_TILES * t + k + L - p) % L
            blk = tu_ref[j, t]
            tile = blk if tile is None else jnp.where(lane_blk == k, blk, tile)
        tiles.append(tile)
    s = jnp.concatenate(tiles, axis=1)
    z = _dot(s, wend_ref[...])
    for q in range(LANE_TILES):
        z_ref[q] = z[:, q * 128:(q + 1) * 128]

    ends = []
    for d in range(2):
        are, aim = apow_ref[d, 0, 1:2, :], apow_ref[d, 1, 1:2, :]
        xr = jnp.zeros((S5_SUB, S5_P), F32)
        xi = jnp.zeros((S5_SUB, S5_P), F32)
        for c in (range(L) if d == 0 else reversed(range(L))):
            rows = pl.ds(c, S5_SUB, stride=L)
            xs_ref[2 * d, rows, :] = xr
            xs_ref[2 * d + 1, rows, :] = xi
            zr = z_ref[2 * d, rows, :]
            zi = z_ref[2 * d + 1, rows, :]
            xr, xi = are * xr - aim * xi + zr, are * xi + aim * xr + zi
        ends.append((xr, xi))
        fin_ref[d, 0] = xr
        fin_ref[d, 1] = xi

    @pl.when(seg > 0)
    def _():
        for d in range(2):
            er, ei = ends[d]
            a16r, a16i = apow_ref[d, 0, L:L + 1, :], apow_ref[d, 1, L:L + 1, :]
            for bb in range(S5_SUB // S5_SUB_PER_SEQ):
                gr = s0_ref[bb, 2 * d:2 * d + 1, :]
                gi = s0_ref[bb, 2 * d + 1:2 * d + 2, :]
                order = range(S5_SUB_PER_SEQ) if d == 0 else reversed(range(S5_SUB_PER_SEQ))
                prev = None
                for sub in order:
                    row = bb * S5_SUB_PER_SEQ + sub
                    if prev is not None:
                        gr, gi = (a16r * gr - a16i * gi + er[prev:prev + 1, :],
                                  a16r * gi + a16i * gr + ei[prev:prev + 1, :])
                    g_ref[0, row:row + 1, :] = gr
                    g_ref[1, row:row + 1, :] = gi
                    prev = row
            if d == 0:
                pr, pi = apow_ref[d, 0, 0:L, :], apow_ref[d, 1, 0:L, :]
            else:
                pr = jnp.concatenate([apow_ref[d, 0, k:k + 1, :] for k in reversed(range(L))], axis=0)
                pi = jnp.concatenate([apow_ref[d, 1, k:k + 1, :] for k in reversed(range(L))], axis=0)
            for sub in range(S5_SUB):
                rows = slice(sub * L, (sub + 1) * L)
                gr = g_ref[0, sub:sub + 1, :]
                gi = g_ref[1, sub:sub + 1, :]
                xs_ref[2 * d, rows, :] = xs_ref[2 * d, rows, :] + (pr * gr - pi * gi)
                xs_ref[2 * d + 1, rows, :] = xs_ref[2 * d + 1, rows, :] + (pr * gi + pi * gr)

    xs = jnp.concatenate([xs_ref[q].astype(BF16) for q in range(LANE_TILES)], axis=1)
    y = _dot(s, ktoe_ref[...]) + _dot_nt(xs, woutT_ref[...])

    @pl.when(p == 0)
    def _():
        ty_ref[...] = jnp.zeros_like(ty_ref)

    for t in range(LANE_TILES):
        yt = y[:, t * 128:(t + 1) * 128]
        for k in range(128 // S5_CH):
            j = (LANE_TILES * t + k + L - p) % L
            ty_ref[j, t] = jnp.where(lane_blk == k, yt, ty_ref[j, t])


def _s5_scan(tu, s0, wend, ktoe, woutT, apow):
    mat_spec = pl.BlockSpec((None, S5_W, S5_W), lambda g, p: (p, 0, 0))
    seqs = S5_SUB // S5_SUB_PER_SEQ
    return pl.pallas_call(
        _s5_body,
        grid=(S5_N_SEG, S5_PAIRS),
        in_specs=[
            pl.BlockSpec((S5_CHUNK, LANE_TILES, S5_SEG_ROWS, 128), lambda g, p: (0, 0, g, 0)),
            pl.BlockSpec((None, seqs, 4, S5_P), lambda g, p: (p, jnp.maximum(g - 1, 0), 0, 0)),
            mat_spec, mat_spec, mat_spec,
            pl.BlockSpec((None, 2, 2, N_POW, S5_P), lambda g, p: (p, 0, 0, 0, 0)),
        ],
        out_specs=[
            pl.BlockSpec((S5_CHUNK, LANE_TILES, S5_SEG_ROWS, 128), lambda g, p: (0, 0, g, 0)),
            pl.BlockSpec((None, 2, 2, S5_SUB, S5_P), lambda g, p: (p, 0, 0, g, 0)),
        ],
        out_shape=[
            jax.ShapeDtypeStruct((S5_CHUNK, LANE_TILES, N_CHUNK_ROWS, 128), F32),
            jax.ShapeDtypeStruct((S5_PAIRS, 2, 2, S5_N_SEG * S5_SUB, S5_P), F32),
        ],
        scratch_shapes=[pltpu.VMEM((LANE_TILES, S5_SEG_ROWS, 128), F32),
                        pltpu.VMEM((LANE_TILES, S5_SEG_ROWS, 128), F32),
                        pltpu.VMEM((2, S5_SUB, S5_P), F32)],
        compiler_params=_cparams(("parallel", "arbitrary")),
        name="s5_scan",
    )(tu, s0, wend, ktoe, woutT, apow)


TM_GLU = 1024


def _glu_body(tu_ref, ty_ref, d_ref, gw_ref, gb_ref, o_ref, u_ref, ys_ref):
    rows = TM_GLU // S5_CHUNK
    for j in range(S5_CHUNK):
        back = ((S5_CHUNK - j) % S5_CHUNK) * S5_CH
        uj = jnp.concatenate([tu_ref[j, t].astype(F32) for t in range(LANE_TILES)], axis=1)
        yj = jnp.concatenate([ty_ref[j, t] for t in range(LANE_TILES)], axis=1)
        if back:
            uj = pltpu.roll(uj, back, 1)
            yj = pltpu.roll(yj, back, 1)
        for t in range(LANE_TILES):
            u_ref[t, pl.ds(j, rows, stride=S5_CHUNK), :] = uj[:, t * 128:(t + 1) * 128]
            ys_ref[t, pl.ds(j, rows, stride=S5_CHUNK), :] = yj[:, t * 128:(t + 1) * 128]
    u = jnp.concatenate([u_ref[t] for t in range(LANE_TILES)], axis=1)
    y = d_ref[...] * u + jnp.concatenate([ys_ref[t] for t in range(LANE_TILES)], axis=1)
    cdf = 0.5 * (1.0 + jnp.tanh(math.sqrt(2.0 / math.pi) * (y + 0.044715 * (y * y * y))))
    y = y * cdf
    gate = jax.nn.sigmoid(_dot(y.astype(BF16), gw_ref[...]) + gb_ref[...])
    o_ref[...] = (y * gate).astype(BF16)


def _glu(tu, ty, ssm_d, glu_w, glu_b):
    tm = TM_GLU
    rows = tm // S5_CHUNK
    lay_spec = pl.BlockSpec((S5_CHUNK, LANE_TILES, rows, 128), lambda i: (0, 0, i, 0))
    return pl.pallas_call(
        _glu_body,
        grid=(N_TOK // tm,),
        in_specs=[
            lay_spec, lay_spec,
            pl.BlockSpec((1, HALF), lambda i: (0, 0)),
            pl.BlockSpec((HALF, HALF), lambda i: (0, 0)),
            pl.BlockSpec((1, HALF), lambda i: (0, 0)),
        ],
        out_specs=pl.BlockSpec((tm, HALF), lambda i: (i, 0)),
        out_shape=jax.ShapeDtypeStruct((N_TOK, HALF), BF16),
        scratch_shapes=[pltpu.VMEM((LANE_TILES, tm, 128), F32), pltpu.VMEM((LANE_TILES, tm, 128), F32)],
        compiler_params=_cparams(("parallel",)),
        name="s5_glu",
    )(tu, ty, ssm_d.reshape(1, HALF), glu_w.astype(BF16), glu_b.reshape(1, HALF))


def kernel(x_prompt, x_sample, c, cache_k, cache_v, state_s5_re, state_s5_im, c_ctx, norm1_g, norm2_g, ada_w, ada_b, mlp_w1, mlp_w2, ab_w_in, pool_w, pool_scale, conv_w, ab_w_out, cd_w_in, na_rpb, ssm_lambda_re, ssm_lambda_im, ssm_log_step, ssm_b_re, ssm_b_im, ssm_c_re, ssm_c_im, ssm_d, glu_w, glu_b, cd_w_out, final_g):
    depth = ada_w.shape[0]
    xs = (x_prompt.reshape(N_PROMPT_TOK, D_MODEL), x_sample.reshape(N_SAMPLE_TOK, D_MODEL))
    cond = jnp.concatenate([c_ctx[None, :], c, jnp.zeros((N_COND - 1 - DEC_BATCH, D_MODEL), F32)], axis=0)
    mods = _adaln(cond, ada_w, ada_b).reshape(depth, N_COND, 6, D_MODEL)

    new_k, new_v, new_re, new_im = [], [], [], []
    for layer in range(depth):
        i = layer // 2
        final = layer == depth - 1
        if layer % 2 == 0:
            y, x = _even_layer_mix(xs, mods, layer, norm1_g[layer], ab_w_in, i,
                                   pool_w[i], pool_scale[i], conv_w[i])
            ya, ya_col, yb, yb_col = y, 0, y, 1
            w_out_stack = ab_w_out
        else:
            x = xs[0]
            qkv, tu = _inproj_odd(x, mods, layer, norm1_g[layer], cd_w_in, i)
            o_p, ck, cv = _ctx_attn(qkv)
            new_k.append(ck)
            new_v.append(cv)
            o_s = _na_attn(qkv, cache_k, cache_v, _na_col_table(na_rpb[i]), i)
            attn = (o_p, o_s)

            wend, ktoe, woutT, apow = _s5_prep(ssm_lambda_re[i], ssm_lambda_im[i], ssm_log_step[i],
                                               ssm_b_re[i], ssm_b_im[i], ssm_c_re[i], ssm_c_im[i])

            def pair_state(s):
                return s.astype(F32).reshape(DEC_BATCH, 2, S5_PAIRS, S5_P).transpose(2, 0, 1, 3)

            s0 = jnp.stack([pair_state(state_s5_re[:, i]), pair_state(state_s5_im[:, i])], axis=3)
            s0 = s0.reshape(S5_PAIRS, DEC_BATCH, 4, S5_P)
            ty, fin = _s5_scan(tu, s0, wend, ktoe, woutT, apow)
            fin = fin[:, :, :, :BATCH]
            new_re.append(fin[:, :, 0].transpose(2, 1, 0, 3).reshape(BATCH, 2, N_SSM_GROUPS, SSM_STATE))
            new_im.append(fin[:, :, 1].transpose(2, 1, 0, 3).reshape(BATCH, 2, N_SSM_GROUPS, SSM_STATE))
            d_out = _glu(tu, ty, ssm_d[i], glu_w[i], glu_b[i])
            ya, ya_col, yb, yb_col = attn, 0, d_out, 0
            w_out_stack = cd_w_out
        xs = _mlp(x, ya, ya_col, yb, yb_col, w_out_stack, i, mods, layer, norm2_g[layer], final_g,
                  mlp_w1, mlp_w2, final)

    y_prompt = xs[0].reshape(BATCH, SEQ, D_MODEL)
    y_sample = xs[1].reshape(DEC_BATCH, DEC_SEQ, D_MODEL)
    return (y_prompt, y_sample, jnp.concatenate(new_k, axis=1), jnp.concatenate(new_v, axis=1),
            jnp.stack(new_re, axis=1), jnp.stack(new_im, axis=1))
```

```python
import functools
import math

import jax
import jax.numpy as jnp
from jax import lax
from jax.experimental import pallas as pl
from jax.experimental.pallas import tpu as pltpu

F32 = jnp.float32
BF16 = jnp.bfloat16

D_MODEL = 1024
BATCH = 16
SEQ = 256
DEC_BATCH = 4
DEC_SEQ = 2048
PAST_LEN = 256
GRID_W = 64
GRID_ROWS = DEC_SEQ // GRID_W
HALF = 512
POOL_WINDOWS = (2, 4, 8, 16)
POOL_GROUP = 128
HEAD_DIM = 64
N_HEADS = 8
WIN_H = 8
WIN_W = 16
SSM_GROUP = 16
N_SSM_GROUPS = 32
SSM_STATE = 64
D_FF = 4096
EPS = 1e-6

N_PROMPT_TOK = BATCH * SEQ
N_SAMPLE_TOK = DEC_BATCH * DEC_SEQ
N_TOK = N_PROMPT_TOK + N_SAMPLE_TOK
N_COND = 8

S5_CHUNK = 16
S5_PAIRS = N_SSM_GROUPS // 2
S5_CH = 2 * SSM_GROUP
S5_P = 2 * SSM_STATE
S5_W = S5_CHUNK * S5_CH

VMEM_LIMIT = 56 * 1024 * 1024


def _cparams(sem):
    return pltpu.CompilerParams(dimension_semantics=sem, vmem_limit_bytes=VMEM_LIMIT)


def _cond_row(i, tm):
    npt = N_PROMPT_TOK // tm
    per = DEC_SEQ // tm
    return jnp.where(i < npt, 0, 1 + (i - npt) // per)


def _resident_layer(shape, idx):
    zeros = (0,) * len(shape)
    return pl.BlockSpec((None,) + tuple(shape), lambda i: (idx,) + zeros, pipeline_mode=pl.Buffered(1))


def _normmod(x, g, shift, scale):
    ms = jnp.mean(x * x, axis=-1, keepdims=True)
    return (x * lax.rsqrt(ms + EPS) * g) * (1.0 + scale) + shift


def _dot(a, b):
    return jnp.dot(a, b, preferred_element_type=F32)


def _dot_nt(a, b, precision=None):
    return lax.dot_general(a, b, (((1,), (1,)), ((), ())), preferred_element_type=F32, precision=precision)


def _adaln_body(c_ref, w_ref, b_ref, o_ref):
    c = c_ref[...]
    s = c * jax.nn.sigmoid(c)
    o_ref[...] = _dot(s.astype(BF16), w_ref[...].astype(BF16)) + b_ref[...]


def _adaln(cond, ada_w, ada_b):
    depth = ada_w.shape[0]
    nj = 6
    return pl.pallas_call(
        _adaln_body,
        grid=(depth, nj),
        in_specs=[
            pl.BlockSpec((N_COND, D_MODEL), lambda l, j: (0, 0)),
            pl.BlockSpec((None, D_MODEL, D_MODEL), lambda l, j: (l, 0, j)),
            pl.BlockSpec((None, 1, D_MODEL), lambda l, j: (l, 0, j)),
        ],
        out_specs=pl.BlockSpec((None, N_COND, D_MODEL), lambda l, j: (l, 0, j)),
        out_shape=jax.ShapeDtypeStruct((depth, N_COND, 6 * D_MODEL), F32),
        compiler_params=_cparams(("arbitrary", "arbitrary")),
        name="adaln",
    )(cond, ada_w, ada_b.reshape(depth, 1, 6 * D_MODEL))


TM_PROJ = 1024


def _stream_tile(x_refs, tm):
    if len(x_refs) == 1:
        return x_refs[0][...]
    is_prompt = pl.program_id(0) < N_PROMPT_TOK // tm
    return jnp.where(is_prompt, x_refs[0][...], x_refs[1][...])


N_CHUNK_ROWS = N_TOK // S5_CHUNK
LANE_TILES = S5_W // 128


def _inproj_odd_body(x_ref, m_ref, g_ref, w_ref, qkv_ref, tu_ref, u_ref, wbf_ref):
    @pl.when(pl.program_id(0) == 0)
    def _():
        wbf_ref[...] = w_ref[...].astype(BF16)

    m = m_ref[...]
    h = _normmod(x_ref[...], g_ref[...], m[0:1], m[1:2])
    acc = _dot(h.astype(BF16), wbf_ref[...])
    qkv_ref[...] = acc[:, 0:3 * HALF].astype(BF16)
    for t in range(LANE_TILES):
        u_ref[t] = acc[:, 3 * HALF + t * 128:3 * HALF + (t + 1) * 128]
    rows = TM_PROJ // S5_CHUNK
    for j in range(S5_CHUNK):
        uj = jnp.concatenate([u_ref[t, pl.ds(j, rows, stride=S5_CHUNK), :] for t in range(LANE_TILES)], axis=1)
        if j:
            uj = pltpu.roll(uj, j * S5_CH, 1)
        for t in range(LANE_TILES):
            tu_ref[j, t] = uj[:, t * 128:(t + 1) * 128].astype(BF16)


def _inproj_odd(x, mods, layer, g, w_stack, w_idx):
    tm = TM_PROJ
    rows = tm // S5_CHUNK
    return pl.pallas_call(
        _inproj_odd_body,
        grid=(N_TOK // tm,),
        in_specs=[
            pl.BlockSpec((tm, D_MODEL), lambda i: (i, 0)),
            pl.BlockSpec((None, None, 6, D_MODEL), lambda i: (layer, _cond_row(i, tm), 0, 0)),
            pl.BlockSpec((1, D_MODEL), lambda i: (0, 0)),
            _resident_layer((D_MODEL, 4 * HALF), w_idx),
        ],
        out_specs=[
            pl.BlockSpec((tm, 3 * HALF), lambda i: (i, 0)),
            pl.BlockSpec((S5_CHUNK, LANE_TILES, rows, 128), lambda i: (0, 0, i, 0)),
        ],
        out_shape=[
            jax.ShapeDtypeStruct((N_TOK, 3 * HALF), BF16),
            jax.ShapeDtypeStruct((S5_CHUNK, LANE_TILES, N_CHUNK_ROWS, 128), BF16),
        ],
        scratch_shapes=[pltpu.VMEM((LANE_TILES, tm, 128), F32), pltpu.VMEM((D_MODEL, 4 * HALF), BF16)],
        compiler_params=_cparams(("arbitrary",)),
        name="inproj_odd",
    )(x, mods, g.reshape(1, D_MODEL), w_stack)


TM_MIX = 256
HALO = 16
RING = 3
LAG = 2
PAD_ROWS = TM_MIX + 2 * HALO + 8


def _even_step(phase, x_refs, m_ref, g_ref, w_ref, pw_ref, ps_ref, cw_ref, y_ref, xcat_ref, proj_ref, tail_ref,
               h_ref):
    i = pl.program_id(0)
    write_slot, cur_slot, next_slot = phase, (phase + 1) % RING, (phase + 2) % RING

    m = m_ref[...]
    x = _stream_tile(x_refs, TM_MIX)
    xcat_ref[...] = x
    h_ref[...] = _normmod(x, g_ref[...], m[0:1], m[1:2]).astype(BF16)
    n_pieces = len(POOL_WINDOWS) + HALF // 128
    pw = (4 * HALF) // n_pieces

    def project(piece):
        cols = slice(piece * pw, (piece + 1) * pw)
        proj_ref[write_slot, :, cols] = _dot(h_ref[...], w_ref[:, cols])

    t = jnp.maximum(i - LAG, 0)
    n_prompt_tiles = N_PROMPT_TOK // TM_MIX
    tiles_per_seq = DEC_SEQ // TM_MIX
    is_prompt = t < n_prompt_tiles
    tile_in_seq = jnp.where(is_prompt, 0, (t - n_prompt_tiles) % tiles_per_seq)
    seq_len = jnp.where(is_prompt, SEQ, DEC_SEQ)
    has_prev = tile_in_seq > 0
    has_next = jnp.logical_and(jnp.logical_not(is_prompt), tile_in_seq < tiles_per_seq - 1)
    prev_keep = jnp.where(has_prev, 1.0, 0.0).astype(F32)
    next_keep = jnp.where(has_next, 1.0, 0.0).astype(F32)
    pos = tile_in_seq * TM_MIX + lax.broadcasted_iota(jnp.int32, (TM_MIX, 1), 0)
    lo_rows, hi_rows = HALO, HALO + TM_MIX
    head = slice(0, HALO)

    slack = jnp.zeros((PAD_ROWS - TM_MIX - 2 * HALO, POOL_GROUP), F32)
    for gi, w in enumerate(POOL_WINDOWS):
        project(gi)
        cs = slice(gi * POOL_GROUP, (gi + 1) * POOL_GROUP)
        cur = proj_ref[cur_slot, :, cs]
        f = jnp.concatenate([tail_ref[:, cs] * prev_keep, cur, proj_ref[next_slot, head, cs] * next_keep, slack],
                            axis=0)
        n_rows, span = PAD_ROWS, 1
        while 2 * span < w:
            n_rows -= 8
            f = f[0:n_rows] + f[span:span + n_rows]
            span *= 2
        s = f[lo_rows - span:hi_rows - span] + f[lo_rows:hi_rows]
        lo = jnp.maximum(pos - w // 2, 0)
        hi = jnp.minimum(pos + (w - w // 2), seq_len)
        inv = 1.0 / (hi - lo).astype(F32)
        pooled = s * inv - cur
        mixed = _dot(pooled.astype(BF16), pw_ref[gi]) * ps_ref[:, cs]
        y_ref[:, cs] = mixed.astype(BF16)

    for ci in range(HALF // 128):
        project(len(POOL_WINDOWS) + ci)
        cs = slice(ci * 128, (ci + 1) * 128)
        bs = slice(HALF + ci * 128, HALF + (ci + 1) * 128)
        gs = slice(2 * HALF + ci * 128, 2 * HALF + (ci + 1) * 128)
        vs = slice(3 * HALF + ci * 128, 3 * HALF + (ci + 1) * 128)
        z = jnp.concatenate([(tail_ref[:, gs] * tail_ref[:, vs]) * prev_keep,
                             proj_ref[cur_slot, :, gs] * proj_ref[cur_slot, :, vs],
                             (proj_ref[next_slot, head, gs] * proj_ref[next_slot, head, vs]) * next_keep], axis=0)
        conv = (cw_ref[0:1, cs] * z[lo_rows - 1:hi_rows - 1]
                + cw_ref[1:2, cs] * z[lo_rows:hi_rows]
                + cw_ref[2:3, cs] * z[lo_rows + 1:hi_rows + 1])
        y_ref[:, bs] = (proj_ref[cur_slot, :, bs] * conv).astype(BF16)

    tail_ref[...] = proj_ref[cur_slot, TM_MIX - HALO:TM_MIX, :]


def _even_body(n_x, *refs):
    x_refs, rest = refs[:n_x], refs[n_x:]
    m_ref, g_ref, w_ref = rest[:3]
    proj_ref, tail_ref, h_ref, wbf_ref = rest[-4:]
    i = pl.program_id(0)

    @pl.when(i == 0)
    def _():
        proj_ref[...] = jnp.zeros_like(proj_ref)
        tail_ref[...] = jnp.zeros_like(tail_ref)
        wbf_ref[...] = w_ref[...].astype(BF16)

    for phase in range(RING):
        @pl.when(i % RING == phase)
        def _(phase=phase):
            _even_step(phase, x_refs, m_ref, g_ref, wbf_ref, *rest[3:-1])


def _even_layer_mix(xs, mods, layer, g, w_stack, w_idx, pool_w, pool_scale, conv_w):
    tm = TM_MIX
    n_tiles = N_TOK // tm
    npt = N_PROMPT_TOK // tm
    last = n_tiles - 1
    if len(xs) == 1:
        x_specs = [pl.BlockSpec((tm, D_MODEL), lambda i: (jnp.minimum(i, last), 0))]
    else:
        x_specs = [pl.BlockSpec((tm, D_MODEL), lambda i: (jnp.minimum(i, npt - 1), 0)),
                   pl.BlockSpec((tm, D_MODEL), lambda i: (jnp.clip(i - npt, 0, last - npt), 0))]
    return pl.pallas_call(
        functools.partial(_even_body, len(xs)),
        grid=(n_tiles + LAG,),
        in_specs=x_specs + [
            pl.BlockSpec((None, None, 6, D_MODEL), lambda i: (layer, _cond_row(jnp.minimum(i, last), tm), 0, 0)),
            pl.BlockSpec((1, D_MODEL), lambda i: (0, 0)),
            _resident_layer((D_MODEL, 4 * HALF), w_idx),
            pl.BlockSpec((len(POOL_WINDOWS), POOL_GROUP, POOL_GROUP), lambda i: (0, 0, 0)),
            pl.BlockSpec((1, HALF), lambda i: (0, 0)),
            pl.BlockSpec((3, HALF), lambda i: (0, 0)),
        ],
        out_specs=[pl.BlockSpec((tm, 2 * HALF), lambda i: (jnp.maximum(i - LAG, 0), 0)),
                   pl.BlockSpec((tm, D_MODEL), lambda i: (jnp.minimum(i, last), 0))],
        out_shape=[jax.ShapeDtypeStruct((N_TOK, 2 * HALF), BF16), jax.ShapeDtypeStruct((N_TOK, D_MODEL), F32)],
        scratch_shapes=[pltpu.VMEM((RING, tm, 4 * HALF), F32), pltpu.VMEM((HALO, 4 * HALF), F32),
                        pltpu.VMEM((tm, D_MODEL), BF16), pltpu.VMEM((D_MODEL, 4 * HALF), BF16)],
        compiler_params=_cparams(("arbitrary",)),
        name="even_layer_mix",
    )(*xs, mods, g.reshape(1, D_MODEL), w_stack, pool_w.astype(BF16), pool_scale.reshape(1, HALF), conv_w)


TM_MLP = 1024
RC_MLP = 256
N_CAST = 16


def _mlp_tile(final, x_ref, ya_ref, yb_ref, m_ref, g_ref, fg_ref, o_ref, wo_ref, w1_ref, w2_ref):
    m = m_ref[...]
    gain = g_ref[...] * (1.0 + m[4:5])
    for r in range(TM_MLP // RC_MLP):
        rows = slice(r * RC_MLP, (r + 1) * RC_MLP)
        mix = _dot(ya_ref[rows, :], wo_ref[0:HALF, :]) + _dot(yb_ref[rows, :], wo_ref[HALF:2 * HALF, :])
        x1 = x_ref[rows, :] + m[2:3] * mix
        ms = jnp.mean(x1 * x1, axis=-1, keepdims=True)
        h = (x1 * lax.rsqrt(ms + EPS) * gain + m[3:4]).astype(BF16)
        t = _dot(h, w1_ref[...])
        t = jnp.square(jnp.maximum(t, 0.0)).astype(BF16)
        out = x1 + m[5:6] * _dot(t, w2_ref[...])
        if final:
            ms = jnp.mean(out * out, axis=-1, keepdims=True)
            out = out * lax.rsqrt(ms + EPS) * fg_ref[...]
        o_ref[rows, :] = out


def _mlp_resident_body(final, x_ref, ya_ref, yb_ref, wo_ref, m_ref, g_ref, fg_ref, w1_ref, w2_ref, o_ref):
    _mlp_tile(final, x_ref, ya_ref, yb_ref, m_ref, g_ref, fg_ref, o_ref, wo_ref, w1_ref, w2_ref)


def _mlp_cast_body(final, emit, x_ref, ya_ref, yb_ref, wo_f, m_ref, g_ref, fg_ref, w1_f, w2_f, o_ref, *rest):
    wo_ref, w1_ref, w2_ref = rest[-3:]
    emitted = rest[:-3]
    i = pl.program_id(0)

    @pl.when(i < N_CAST)
    def _():
        for k, (f_ref, b_ref) in enumerate(((wo_f, wo_ref), (w1_f, w1_ref), (w2_f, w2_ref))):
            n = f_ref.shape[0]
            slab = f_ref[...].astype(BF16)
            b_ref[pl.ds(pl.multiple_of(i * n, n), n), :] = slab
            if emit:
                emitted[k][...] = slab

    @pl.when(i >= N_CAST)
    def _():
        _mlp_tile(final, x_ref, ya_ref, yb_ref, m_ref, g_ref, fg_ref, o_ref, wo_ref, w1_ref, w2_ref)


def _resident(shape):
    zeros = (0,) * len(shape)
    return pl.BlockSpec(shape, lambda i: zeros, pipeline_mode=pl.Buffered(1))


def _mlp_call(x, x_off, ya, ya_off, ya_col, yb, yb_off, yb_col, n_tiles, tile0, mods, layer, g2, final_g, final,
              f32_weights=None, bf16_weights=None, emit=False):
    tm = TM_MLP
    lead = N_CAST if f32_weights is not None else 0

    def tile(i):
        return jnp.maximum(i - lead, 0)

    stream_specs = [
        pl.BlockSpec((tm, D_MODEL), lambda i: (tile(i) + x_off, 0)),
        pl.BlockSpec((tm, HALF), lambda i: (tile(i) + ya_off, ya_col)),
        pl.BlockSpec((tm, HALF), lambda i: (tile(i) + yb_off, yb_col)),
    ]
    mod_spec = pl.BlockSpec((None, None, 6, D_MODEL), lambda i: (layer, _cond_row(tile(i) + tile0, tm), 0, 0))
    out_spec = pl.BlockSpec((tm, D_MODEL), lambda i: (tile(i), 0))
    out_shape = jax.ShapeDtypeStruct((n_tiles * tm, D_MODEL), F32)
    vecs = (g2.reshape(1, D_MODEL), final_g.reshape(1, D_MODEL))
    if f32_weights is None:
        w_out, w1, w2 = bf16_weights
        return pl.pallas_call(
            functools.partial(_mlp_resident_body, final),
            grid=(n_tiles,),
            in_specs=stream_specs + [_resident((D_MODEL, D_MODEL)), mod_spec, _resident((1, D_MODEL)),
                                     _resident((1, D_MODEL)), _resident((D_MODEL, D_FF)), _resident((D_FF, D_MODEL))],
            out_specs=out_spec,
            out_shape=out_shape,
            compiler_params=_cparams(("parallel",)),
            name="mlp",
        )(x, ya, yb, w_out, mods, *vecs, w1, w2)

    wo_stack, wo_idx, w1_stack, w2_stack = f32_weights

    def slab(i):
        return jnp.minimum(i, N_CAST - 1)

    shapes = ((D_MODEL, D_MODEL), (D_MODEL, D_FF), (D_FF, D_MODEL))
    slab_rows = [rows // N_CAST for rows, _ in shapes]
    f32_specs = [pl.BlockSpec((None, n, cols), lambda i, idx=idx: (idx, slab(i), 0))
                 for n, (_, cols), idx in zip(slab_rows, shapes, (wo_idx, layer, layer))]
    out_specs, out_shapes = [out_spec], [out_shape]
    if emit:
        out_specs += [pl.BlockSpec((n, cols), lambda i: (slab(i), 0)) for n, (_, cols) in zip(slab_rows, shapes)]
        out_shapes += [jax.ShapeDtypeStruct(shp, BF16) for shp in shapes]
    return pl.pallas_call(
        functools.partial(_mlp_cast_body, final, emit),
        grid=(N_CAST + n_tiles,),
        in_specs=stream_specs + [f32_specs[0], mod_spec, _resident((1, D_MODEL)), _resident((1, D_MODEL)),
                                 f32_specs[1], f32_specs[2]],
        out_specs=out_specs,
        out_shape=out_shapes,
        scratch_shapes=[pltpu.VMEM(shp, BF16) for shp in shapes],
        compiler_params=_cparams(("arbitrary",)),
        name="mlp_cast",
    )(x, ya, yb, wo_stack, mods, *vecs, w1_stack, w2_stack)


def _mlp(x, ya, ya_col, yb, yb_col, w_out_stack, w_out_idx, mods, layer, g2, final_g, w1_stack, w2_stack, final):
    tm = TM_MLP
    npt = N_PROMPT_TOK // tm
    f32_weights = (w_out_stack, w_out_idx, w1_stack, w2_stack)
    common = (mods, layer, g2, final_g, final)
    if not final:
        if isinstance(ya, tuple):
            ya = jnp.concatenate(ya, axis=0)
        return tuple(_mlp_call(x, 0, ya, 0, ya_col, yb, 0, yb_col, N_TOK // tm, 0, *common, f32_weights=f32_weights))
    ya_p, ya_s = ((ya[0], 0), (ya[1], 0)) if isinstance(ya, tuple) else ((ya, 0), (ya, npt))
    y_p, *bf16_weights = _mlp_call(x, 0, ya_p[0], ya_p[1], ya_col, yb, 0, yb_col, npt, 0, *common,
                                   f32_weights=f32_weights, emit=True)
    y_s = _mlp_call(x, npt, ya_s[0], ya_s[1], ya_col, yb, npt, yb_col, N_TOK // tm - npt, npt, *common,
                    bf16_weights=bf16_weights)
    return y_p, y_s


def _ctx_attn_body(q_ref, k_ref, v_ref, o_ref, ck_ref, cv_ref):
    lane = lax.broadcasted_iota(jnp.int32, (1, 2 * HEAD_DIM), 1)
    own = [lane < HEAD_DIM, lane >= HEAD_DIM]
    for h in range(N_HEADS):
        sl = slice(h * HEAD_DIM, (h + 1) * HEAD_DIM)
        ck_ref[h] = k_ref[:, sl].astype(F32)
        cv_ref[h] = v_ref[:, sl].astype(F32)
    for hp in range(N_HEADS // 2):
        sl = slice(hp * 2 * HEAD_DIM, (hp + 1) * 2 * HEAD_DIM)
        q = q_ref[:, sl] * jnp.asarray(HEAD_DIM ** -0.5, BF16)
        k = k_ref[:, sl]
        v = v_ref[:, sl]
        q2 = jnp.concatenate([jnp.where(own[0], q, jnp.zeros_like(q)),
                              jnp.where(own[1], q, jnp.zeros_like(q))], axis=0)
        s = _dot_nt(q2, k)
        m = jnp.max(s, axis=-1, keepdims=True)
        p = jnp.exp(s - m).astype(BF16)
        out = None
        for hh in range(2):
            num = _dot(p[hh * SEQ:(hh + 1) * SEQ], jnp.where(own[hh], v, jnp.ones_like(v)))
            den = jnp.where(own[hh], pltpu.roll(num, HEAD_DIM, 1), 1.0)
            o = num / den
            out = o if out is None else jnp.where(own[0], out, o)
        o_ref[:, sl] = out.astype(BF16)


def _ctx_attn(proj):
    return pl.pallas_call(
        _ctx_attn_body,
        grid=(BATCH,),
        in_specs=[
            pl.BlockSpec((SEQ, HALF), lambda b: (b, 0)),
            pl.BlockSpec((SEQ, HALF), lambda b: (b, 1)),
            pl.BlockSpec((SEQ, HALF), lambda b: (b, 2)),
        ],
        out_specs=[
            pl.BlockSpec((SEQ, HALF), lambda b: (b, 0)),
            pl.BlockSpec((None, None, N_HEADS, SEQ, HEAD_DIM), lambda b: (b, 0, 0, 0, 0)),
            pl.BlockSpec((None, None, N_HEADS, SEQ, HEAD_DIM), lambda b: (b, 0, 0, 0, 0)),
        ],
        out_shape=[
            jax.ShapeDtypeStruct((N_PROMPT_TOK, HALF), BF16),
            jax.ShapeDtypeStruct((BATCH, 1, N_HEADS, SEQ, HEAD_DIM), F32),
            jax.ShapeDtypeStruct((BATCH, 1, N_HEADS, SEQ, HEAD_DIM), F32),
        ],
        compiler_params=_cparams(("parallel",)),
        name="ctx_attn",
    )(proj, proj, proj)


NA_MASKED = -1e30
NA_QROWS = 4
NA_KROWS = NA_QROWS + WIN_H
NA_NQ = NA_QROWS * GRID_W
NA_NK = NA_KROWS * GRID_W
NA_BLOCKS = GRID_ROWS // NA_QROWS
NA_TYPES = 3


def _na_col_table(rpb):
    qc = jnp.arange(GRID_W)[:, None]
    kc = jnp.arange(GRID_W)[None, :]
    c0 = jnp.clip(qc - WIN_W // 2, 0, GRID_W - WIN_W)
    valid = (kc >= c0) & (kc < c0 + WIN_W)
    coff = jnp.clip(kc - qc, -(WIN_W - 1), WIN_W - 1) + (WIN_W - 1)
    onehot = (coff.reshape(1, -1) == jnp.arange(2 * WIN_W - 1)[:, None]).astype(F32)
    t = jnp.dot(rpb.astype(F32).reshape(N_HEADS * (2 * WIN_H - 1), 2 * WIN_W - 1), onehot,
                precision=lax.Precision.HIGHEST)
    t = t.reshape(N_HEADS, 2 * WIN_H - 1, GRID_W, GRID_W)
    return jnp.where(valid[None, None], t, NA_MASKED)


def _na_window_start(rb):
    return min(max(NA_QROWS * rb - WIN_H // 2, 0), GRID_ROWS - NA_KROWS)


def _na_body(q_ref, k_ref, v_ref, ck_ref, cv_ref, tab_ref, o_ref, bias_ref, kc_ref, vc_ref, va_ref):
    @pl.when(pl.program_id(1) == 0)
    def _():
        masked = jnp.full((GRID_W, GRID_W), NA_MASKED, F32)
        for hh in range(2):
            for ty in range(NA_TYPES):
                q0 = (0, NA_QROWS, GRID_ROWS - NA_QROWS)[ty]
                ks = (0, 0, GRID_ROWS - NA_KROWS)[ty]
                for a in range(NA_QROWS):
                    r0 = min(max(q0 + a - WIN_H // 2, 0), GRID_ROWS - WIN_H)
                    for y in range(NA_KROWS):
                        kr = ks + y
                        ok = r0 <= kr < r0 + WIN_H
                        blk = tab_ref[hh, kr - (q0 + a) + WIN_H - 1] if ok else masked
                        bias_ref[ty, hh * NA_NQ + a * GRID_W:hh * NA_NQ + (a + 1) * GRID_W,
                                 y * GRID_W:(y + 1) * GRID_W] = blk

    lane = lax.broadcasted_iota(jnp.int32, (1, 2 * HEAD_DIM), 1)
    own = [lane < HEAD_DIM, lane >= HEAD_DIM]
    kc_ref[...] = jnp.concatenate([ck_ref[0], ck_ref[1]], axis=1).astype(BF16)
    vc = jnp.concatenate([cv_ref[0], cv_ref[1]], axis=1)
    v = v_ref[...]
    for hh in range(2):
        vc_ref[hh] = jnp.where(own[hh], vc, 1.0).astype(BF16)
        va_ref[hh] = jnp.where(own[hh], v, jnp.ones_like(v))

    def block(rb, ty):
        qs = rb * NA_NQ
        ks = _na_window_start(rb) * GRID_W
        q = q_ref[pl.ds(qs, NA_NQ), :] * jnp.asarray(HEAD_DIM ** -0.5, BF16)
        q2 = jnp.concatenate([jnp.where(own[0], q, jnp.zeros_like(q)),
                              jnp.where(own[1], q, jnp.zeros_like(q))], axis=0)
        s_loc = _dot_nt(q2, k_ref[pl.ds(ks, NA_NK), :]) + bias_ref[ty]
        s_ctx = _dot_nt(q2, kc_ref[...])
        m = jnp.maximum(jnp.max(s_loc, axis=-1, keepdims=True), jnp.max(s_ctx, axis=-1, keepdims=True))
        p_loc = jnp.exp(s_loc - m).astype(BF16)
        p_ctx = jnp.exp(s_ctx - m).astype(BF16)
        out = None
        for hh in range(2):
            rows = slice(hh * NA_NQ, (hh + 1) * NA_NQ)
            num = _dot(p_loc[rows], va_ref[hh, pl.ds(ks, NA_NK), :]) + _dot(p_ctx[rows], vc_ref[hh])
            den = jnp.where(own[hh], pltpu.roll(num, HEAD_DIM, 1), 1.0)
            o = num / den
            out = o if out is None else jnp.where(own[0], out, o)
        o_ref[pl.ds(qs, NA_NQ), :] = out.astype(BF16)

    for rb in range(NA_BLOCKS):
        block(rb, 0 if rb == 0 else (2 if rb == NA_BLOCKS - 1 else 1))


def _na_attn(proj, cache_k, cache_v, col_table, layer_idx):
    row_blk0 = N_PROMPT_TOK // DEC_SEQ
    hp = N_HEADS // 2
    return pl.pallas_call(
        _na_body,
        grid=(hp, DEC_BATCH),
        in_specs=[
            pl.BlockSpec((DEC_SEQ, 128), lambda h, b: (row_blk0 + b, h)),
            pl.BlockSpec((DEC_SEQ, 128), lambda h, b: (row_blk0 + b, hp + h)),
            pl.BlockSpec((DEC_SEQ, 128), lambda h, b: (row_blk0 + b, 2 * hp + h)),
            pl.BlockSpec((None, None, 2, PAST_LEN, HEAD_DIM), lambda h, b: (b, layer_idx, h, 0, 0)),
            pl.BlockSpec((None, None, 2, PAST_LEN, HEAD_DIM), lambda h, b: (b, layer_idx, h, 0, 0)),
            pl.BlockSpec((2, 2 * WIN_H - 1, GRID_W, GRID_W), lambda h, b: (h, 0, 0, 0)),
        ],
        out_specs=pl.BlockSpec((DEC_SEQ, 128), lambda h, b: (b, h)),
        out_shape=jax.ShapeDtypeStruct((N_SAMPLE_TOK, HALF), BF16),
        scratch_shapes=[pltpu.VMEM((NA_TYPES, 2 * NA_NQ, NA_NK), F32),
                        pltpu.VMEM((PAST_LEN, 2 * HEAD_DIM), BF16),
                        pltpu.VMEM((2, PAST_LEN, 2 * HEAD_DIM), BF16),
                        pltpu.VMEM((2, DEC_SEQ, 2 * HEAD_DIM), BF16)],
        compiler_params=_cparams(("parallel", "arbitrary")),
        name="na_attn",
    )(proj, proj, proj, cache_k, cache_v, col_table)


N_POW = 24


def _s5_prep_body(vec_ref, bt_ref, c_ref, wend_ref, ktoe_ref, woutT_ref, apow_ref, pw_ref, pct_ref):
    L = S5_CHUNK
    p = pl.program_id(0)
    lane_blk = lax.broadcasted_iota(jnp.int32, (S5_CH, S5_W), 1) // S5_CH
    src_blk = (lane_blk + L - p) % L
    krows = []
    for d in range(2):
        v = vec_ref[d]
        lre, lim, step = v[0:1], v[1:2], v[2:3]
        a = lre * step
        b = lim * step
        ea = jnp.exp(a)
        nr = ea * jnp.cos(b) - 1.0
        ni = ea * jnp.sin(b)
        den = lre * lre + lim * lim
        qr = (nr * lre + ni * lim) / den
        qi = (ni * lre - nr * lim) / den
        bre, bim = bt_ref[d, 0], bt_ref[d, 1]
        bqr = bre * qr - bim * qi
        bqi = bre * qi + bim * qr
        cre, cim = c_ref[d, 0], c_ref[d, 1]

        e = lax.broadcasted_iota(jnp.int32, (N_POW, 1), 0).astype(F32)
        mag = jnp.exp(e * a)
        pw_ref[0] = mag * jnp.cos(e * b)
        pw_ref[1] = mag * jnp.sin(e * b)
        mag = jnp.exp((L * e) * a)
        apow_ref[d, 0] = mag * jnp.cos((L * e) * b)
        apow_ref[d, 1] = mag * jnp.sin((L * e) * b)

        def power(k):
            return pw_ref[0, k:k + 1, :], pw_ref[1, k:k + 1, :]

        def c_block(k):
            pr, pi = power(k)
            return cre * pr - cim * pi, -(cre * pi + cim * pr)

        for i in range(L):
            rows = pl.ds(pl.multiple_of(((i + p) % L) * S5_CH, S5_CH), S5_CH)
            pr, pi = power(L - 1 - i if d == 0 else i)
            wend_ref[rows, d * 256:d * 256 + 128] = (bqr * pr - bqi * pi).astype(BF16)
            wend_ref[rows, d * 256 + 128:d * 256 + 256] = (bqr * pi + bqi * pr).astype(BF16)
            mre, mim = c_block(i + 1 if d == 0 else L - i)
            woutT_ref[rows, d * 256:d * 256 + 128] = mre.astype(BF16)
            woutT_ref[rows, d * 256 + 128:d * 256 + 256] = mim.astype(BF16)
            lag_rows = slice(i * S5_CH, (i + 1) * S5_CH)
            mre, mim = c_block(i if d == 0 else L - 1 - i)
            pct_ref[lag_rows, 0:128] = mre
            pct_ref[lag_rows, 128:256] = mim

        bqcat = jnp.concatenate([bqr, bqi], axis=1)
        krows.append(_dot_nt(bqcat, pct_ref[...], precision=lax.Precision.HIGHEST))

    base_f = pltpu.roll(krows[0], p * S5_CH, 1)
    base_b = pltpu.roll(krows[1], p * S5_CH, 1)
    for i in range(L):
        rows = pl.ds(pl.multiple_of(((i + p) % L) * S5_CH, S5_CH), S5_CH)
        blk_f = base_f if i == 0 else pltpu.roll(base_f, i * S5_CH, 1)
        blk_b = base_b if (i + 1) % L == 0 else pltpu.roll(base_b, ((i + 1) % L) * S5_CH, 1)
        blk = jnp.where(src_blk >= i, blk_f, 0.0) + jnp.where(src_blk <= i, blk_b, 0.0)
        ktoe_ref[rows, :] = blk.astype(BF16)


def _s5_prep(lam_re, lam_im, log_step, b_re, b_im, c_re, c_im):
    def pair_vec(a):
        return a.astype(F32).reshape(2, S5_PAIRS, 1, S5_P).transpose(1, 0, 2, 3)

    step = jnp.broadcast_to(jnp.exp(log_step.astype(F32))[:, :, None], (2, N_SSM_GROUPS, SSM_STATE))
    vec = jnp.concatenate([pair_vec(lam_re), pair_vec(lam_im), pair_vec(step),
                           jnp.zeros((S5_PAIRS, 2, 5, S5_P), F32)], axis=2)

    def block_diag_ch_by_state(m):
        m = m.astype(F32).reshape(2, S5_PAIRS, 2, SSM_GROUP, SSM_STATE)
        eye = jnp.eye(2, dtype=F32)
        out = m[:, :, :, :, None, :] * eye[None, None, :, None, :, None]
        return out.reshape(2, S5_PAIRS, S5_CH, S5_P).transpose(1, 0, 2, 3)

    bt = jnp.stack([block_diag_ch_by_state(b_re.transpose(0, 1, 3, 2)),
                    block_diag_ch_by_state(b_im.transpose(0, 1, 3, 2))], axis=2)
    cc = jnp.stack([block_diag_ch_by_state(c_re), block_diag_ch_by_state(c_im)], axis=2)

    mat = jax.ShapeDtypeStruct((S5_PAIRS, S5_W, S5_W), BF16)
    mat_spec = pl.BlockSpec((None, S5_W, S5_W), lambda p: (p, 0, 0))
    return pl.pallas_call(
        _s5_prep_body,
        grid=(S5_PAIRS,),
        in_specs=[
            pl.BlockSpec((None, 2, 8, S5_P), lambda p: (p, 0, 0, 0)),
            pl.BlockSpec((None, 2, 2, S5_CH, S5_P), lambda p: (p, 0, 0, 0, 0)),
            pl.BlockSpec((None, 2, 2, S5_CH, S5_P), lambda p: (p, 0, 0, 0, 0)),
        ],
        out_specs=[mat_spec, mat_spec, mat_spec,
                   pl.BlockSpec((None, 2, 2, N_POW, S5_P), lambda p: (p, 0, 0, 0, 0))],
        out_shape=[mat, mat, mat, jax.ShapeDtypeStruct((S5_PAIRS, 2, 2, N_POW, S5_P), F32)],
        scratch_shapes=[pltpu.VMEM((2, N_POW, S5_P), F32), pltpu.VMEM((S5_W, 2 * S5_P), F32)],
        compiler_params=_cparams(("parallel",)),
        name="s5_prep",
    )(vec, bt, cc)


S5_SEG_ROWS = 256
S5_N_SEG = N_CHUNK_ROWS // S5_SEG_ROWS
S5_SUB = S5_SEG_ROWS // S5_CHUNK
S5_SUB_PER_SEQ = (DEC_SEQ // S5_CHUNK) // S5_CHUNK


def _s5_body(tu_ref, s0_ref, wend_ref, ktoe_ref, woutT_ref, apow_ref, ty_ref, fin_ref, z_ref, xs_ref, g_ref):
    seg = pl.program_id(0)
    p = pl.program_id(1)
    L = S5_CHUNK
    lane_blk = lax.broadcasted_iota(jnp.int32, (S5_SEG_ROWS, 128), 1) // S5_CH

    tiles = []
    for t in range(LANE_TILES):
        tile = None
        for k in range(128 // S5_CH):
            j = (LANE_TILES * t + k + L - p) % L
            blk = tu_ref[j, t]
            tile = blk if tile is None else jnp.where(lane_blk == k, blk, tile)
        tiles.append(tile)
    s = jnp.concatenate(tiles, axis=1)
    z = _dot(s, wend_ref[...])
    for q in range(LANE_TILES):
        z_ref[q] = z[:, q * 128:(q + 1) * 128]

    ends = []
    for d in range(2):
        are, aim = apow_ref[d, 0, 1:2, :], apow_ref[d, 1, 1:2, :]
        xr = jnp.zeros((S5_SUB, S5_P), F32)
        xi = jnp.zeros((S5_SUB, S5_P), F32)
        for c in (range(L) if d == 0 else reversed(range(L))):
            rows = pl.ds(c, S5_SUB, stride=L)
            xs_ref[2 * d, rows, :] = xr
            xs_ref[2 * d + 1, rows, :] = xi
            zr = z_ref[2 * d, rows, :]
            zi = z_ref[2 * d + 1, rows, :]
            xr, xi = are * xr - aim * xi + zr, are * xi + aim * xr + zi
        ends.append((xr, xi))
        fin_ref[d, 0] = xr
        fin_ref[d, 1] = xi

    @pl.when(seg > 0)
    def _():
        for d in range(2):
            er, ei = ends[d]
            a16r, a16i = apow_ref[d, 0, L:L + 1, :], apow_ref[d, 1, L:L + 1, :]
            for bb in range(S5_SUB // S5_SUB_PER_SEQ):
                gr = s0_ref[bb, 2 * d:2 * d + 1, :]
                gi = s0_ref[bb, 2 * d + 1:2 * d + 2, :]
                order = range(S5_SUB_PER_SEQ) if d == 0 else reversed(range(S5_SUB_PER_SEQ))
                prev = None
                for sub in order:
                    row = bb * S5_SUB_PER_SEQ + sub
                    if prev is not None:
                        gr, gi = (a16r * gr - a16i * gi + er[prev:prev + 1, :],
                                  a16r * gi + a16i * gr + ei[prev:prev + 1, :])
                    g_ref[0, row:row + 1, :] = gr
                    g_ref[1, row:row + 1, :] = gi
                    prev = row
            if d == 0:
                pr, pi = apow_ref[d, 0, 0:L, :], apow_ref[d, 1, 0:L, :]
            else:
                pr = jnp.concatenate([apow_ref[d, 0, k:k + 1, :] for k in reversed(range(L))], axis=0)
                pi = jnp.concatenate([apow_ref[d, 1, k:k + 1, :] for k in reversed(range(L))], axis=0)
            for sub in range(S5_SUB):
                rows = slice(sub * L, (sub + 1) * L)
                gr = g_ref[0, sub:sub + 1, :]
                gi = g_ref[1, sub:sub + 1, :]
                xs_ref[2 * d, rows, :] = xs_ref[2 * d, rows, :] + (pr * gr - pi * gi)
                xs_ref[2 * d + 1, rows, :] = xs_ref[2 * d + 1, rows, :] + (pr * gi + pi * gr)

    xs = jnp.concatenate([xs_ref[q].astype(BF16) for q in range(LANE_TILES)], axis=1)
    y = _dot(s, ktoe_ref[...]) + _dot_nt(xs, woutT_ref[...])

    @pl.when(p == 0)
    def _():
        ty_ref[...] = jnp.zeros_like(ty_ref)

    for t in range(LANE_TILES):
        yt = y[:, t * 128:(t + 1) * 128]
        for k in range(128 // S5_CH):
            j = (LANE_TILES * t + k + L - p) % L
            ty_ref[j, t] = jnp.where(lane_blk == k, yt, ty_ref[j, t])


def _s5_scan(tu, s0, wend, ktoe, woutT, apow):
    mat_spec = pl.BlockSpec((None, S5_W, S5_W), lambda g, p: (p, 0, 0))
    seqs = S5_SUB // S5_SUB_PER_SEQ
    return pl.pallas_call(
        _s5_body,
        grid=(S5_N_SEG, S5_PAIRS),
        in_specs=[
            pl.BlockSpec((S5_CHUNK, LANE_TILES, S5_SEG_ROWS, 128), lambda g, p: (0, 0, g, 0)),
            pl.BlockSpec((None, seqs, 4, S5_P), lambda g, p: (p, jnp.maximum(g - 1, 0), 0, 0)),
            mat_spec, mat_spec, mat_spec,
            pl.BlockSpec((None, 2, 2, N_POW, S5_P), lambda g, p: (p, 0, 0, 0, 0)),
        ],
        out_specs=[
            pl.BlockSpec((S5_CHUNK, LANE_TILES, S5_SEG_ROWS, 128), lambda g, p: (0, 0, g, 0)),
            pl.BlockSpec((None, 2, 2, S5_SUB, S5_P), lambda g, p: (p, 0, 0, g, 0)),
        ],
        out_shape=[
            jax.ShapeDtypeStruct((S5_CHUNK, LANE_TILES, N_CHUNK_ROWS, 128), F32),
            jax.ShapeDtypeStruct((S5_PAIRS, 2, 2, S5_N_SEG * S5_SUB, S5_P), F32),
        ],
        scratch_shapes=[pltpu.VMEM((LANE_TILES, S5_SEG_ROWS, 128), F32),
                        pltpu.VMEM((LANE_TILES, S5_SEG_ROWS, 128), F32),
                        pltpu.VMEM((2, S5_SUB, S5_P), F32)],
        compiler_params=_cparams(("parallel", "arbitrary")),
        name="s5_scan",
    )(tu, s0, wend, ktoe, woutT, apow)


TM_GLU = 1024


def _glu_body(tu_ref, ty_ref, d_ref, gw_ref, gb_ref, o_ref, out_ref):
    rows = TM_GLU // S5_CHUNK
    d = d_ref[...]
    pre = []
    for j in range(S5_CHUNK):
        uj = jnp.concatenate([tu_ref[j, t].astype(F32) for t in range(LANE_TILES)], axis=1)
        yj = jnp.concatenate([ty_ref[j, t] for t in range(LANE_TILES)], axis=1)
        dj = pltpu.roll(d, j * S5_CH, 1) if j else d
        yj = dj * uj + yj
        back = ((S5_CHUNK - j) % S5_CHUNK) * S5_CH
        pre.append(pltpu.roll(yj, back, 1) if back else yj)
    y = jnp.concatenate(pre, axis=0)
    cdf = 0.5 * (1.0 + jnp.tanh(math.sqrt(2.0 / math.pi) * (y + 0.044715 * (y * y * y))))
    y = y * cdf
    gate = jax.nn.sigmoid(_dot(y.astype(BF16), gw_ref[...]) + gb_ref[...])
    out = y * gate
    for j in range(S5_CHUNK):
        oj = out[j * rows:(j + 1) * rows]
        for t in range(LANE_TILES):
            out_ref[t, pl.ds(j, rows, stride=S5_CHUNK), :] = oj[:, t * 128:(t + 1) * 128]
    o_ref[...] = jnp.concatenate([out_ref[t] for t in range(LANE_TILES)], axis=1).astype(BF16)


def _glu(tu, ty, ssm_d, glu_w, glu_b):
    tm = TM_GLU
    rows = tm // S5_CHUNK
    lay_spec = pl.BlockSpec((S5_CHUNK, LANE_TILES, rows, 128), lambda i: (0, 0, i, 0))
    return pl.pallas_call(
        _glu_body,
        grid=(N_TOK // tm,),
        in_specs=[
            lay_spec, lay_spec,
            pl.BlockSpec((1, HALF), lambda i: (0, 0)),
            pl.BlockSpec((HALF, HALF), lambda i: (0, 0)),
            pl.BlockSpec((1, HALF), lambda i: (0, 0)),
        ],
        out_specs=pl.BlockSpec((tm, HALF), lambda i: (i, 0)),
        out_shape=jax.ShapeDtypeStruct((N_TOK, HALF), BF16),
        scratch_shapes=[pltpu.VMEM((LANE_TILES, tm, 128), F32)],
        compiler_params=_cparams(("parallel",)),
        name="s5_glu",
    )(tu, ty, ssm_d.reshape(1, HALF), glu_w.astype(BF16), glu_b.reshape(1, HALF))


def kernel(x_prompt, x_sample, c, cache_k, cache_v, state_s5_re, state_s5_im, c_ctx, norm1_g, norm2_g, ada_w, ada_b, mlp_w1, mlp_w2, ab_w_in, pool_w, pool_scale, conv_w, ab_w_out, cd_w_in, na_rpb, ssm_lambda_re, ssm_lambda_im, ssm_log_step, ssm_b_re, ssm_b_im, ssm_c_re, ssm_c_im, ssm_d, glu_w, glu_b, cd_w_out, final_g):
    depth = ada_w.shape[0]
    xs = (x_prompt.reshape(N_PROMPT_TOK, D_MODEL), x_sample.reshape(N_SAMPLE_TOK, D_MODEL))
    cond = jnp.concatenate([c_ctx[None, :], c, jnp.zeros((N_COND - 1 - DEC_BATCH, D_MODEL), F32)], axis=0)
    mods = _adaln(cond, ada_w, ada_b).reshape(depth, N_COND, 6, D_MODEL)

    new_k, new_v, new_re, new_im = [], [], [], []
    for layer in range(depth):
        i = layer // 2
        final = layer == depth - 1
        if layer % 2 == 0:
            y, x = _even_layer_mix(xs, mods, layer, norm1_g[layer], ab_w_in, i,
                                   pool_w[i], pool_scale[i], conv_w[i])
            ya, ya_col, yb, yb_col = y, 0, y, 1
            w_out_stack = ab_w_out
        else:
            x = xs[0]
            qkv, tu = _inproj_odd(x, mods, layer, norm1_g[layer], cd_w_in, i)
            o_p, ck, cv = _ctx_attn(qkv)
            new_k.append(ck)
            new_v.append(cv)
            o_s = _na_attn(qkv, cache_k, cache_v, _na_col_table(na_rpb[i]), i)
            attn = (o_p, o_s)

            wend, ktoe, woutT, apow = _s5_prep(ssm_lambda_re[i], ssm_lambda_im[i], ssm_log_step[i],
                                               ssm_b_re[i], ssm_b_im[i], ssm_c_re[i], ssm_c_im[i])

            def pair_state(s):
                return s.astype(F32).reshape(DEC_BATCH, 2, S5_PAIRS, S5_P).transpose(2, 0, 1, 3)

            s0 = jnp.stack([pair_state(state_s5_re[:, i]), pair_state(state_s5_im[:, i])], axis=3)
            s0 = s0.reshape(S5_PAIRS, DEC_BATCH, 4, S5_P)
            ty, fin = _s5_scan(tu, s0, wend, ktoe, woutT, apow)
            fin = fin[:, :, :, :BATCH]
            new_re.append(fin[:, :, 0].transpose(2, 1, 0, 3).reshape(BATCH, 2, N_SSM_GROUPS, SSM_STATE))
            new_im.append(fin[:, :, 1].transpose(2, 1, 0, 3).reshape(BATCH, 2, N_SSM_GROUPS, SSM_STATE))
            d_out = _glu(tu, ty, ssm_d[i], glu_w[i], glu_b[i])
            ya, ya_col, yb, yb_col = attn, 0, d_out, 0
            w_out_stack = cd_w_out
        xs = _mlp(x, ya, ya_col, yb, yb_col, w_out_stack, i, mods, layer, norm2_g[layer], final_g,
                  mlp_w1, mlp_w2, final)

    y_prompt = xs[0].reshape(BATCH, SEQ, D_MODEL)
    y_sample = xs[1].reshape(DEC_BATCH, DEC_SEQ, D_MODEL)
    return (y_prompt, y_sample, jnp.concatenate(new_k, axis=1), jnp.concatenate(new_v, axis=1),
            jnp.stack(new_re, axis=1), jnp.stack(new_im, axis=1))
```

```python
import functools
import math

import jax
import jax.numpy as jnp
from jax import lax
from jax.experimental import pallas as pl
from jax.experimental.pallas import tpu as pltpu

F32 = jnp.float32
BF16 = jnp.bfloat16

D_MODEL = 1024
BATCH = 16
SEQ = 256
DEC_BATCH = 4
DEC_SEQ = 2048
PAST_LEN = 256
GRID_W = 64
GRID_ROWS = DEC_SEQ // GRID_W
HALF = 512
POOL_WINDOWS = (2, 4, 8, 16)
POOL_GROUP = 128
HEAD_DIM = 64
N_HEADS = 8
WIN_H = 8
WIN_W = 16
SSM_GROUP = 16
N_SSM_GROUPS = 32
SSM_STATE = 64
D_FF = 4096
EPS = 1e-6

N_PROMPT_TOK = BATCH * SEQ
N_SAMPLE_TOK = DEC_BATCH * DEC_SEQ
N_TOK = N_PROMPT_TOK + N_SAMPLE_TOK
N_COND = 8

S5_CHUNK = 16
S5_PAIRS = N_SSM_GROUPS // 2
S5_CH = 2 * SSM_GROUP
S5_P = 2 * SSM_STATE
S5_W = S5_CHUNK * S5_CH

VMEM_LIMIT = 56 * 1024 * 1024


def _cparams(sem):
    return pltpu.CompilerParams(dimension_semantics=sem, vmem_limit_bytes=VMEM_LIMIT)


def _cond_row(i, tm):
    npt = N_PROMPT_TOK // tm
    per = DEC_SEQ // tm
    return jnp.where(i < npt, 0, 1 + (i - npt) // per)


def _resident_layer(shape, idx):
    zeros = (0,) * len(shape)
    return pl.BlockSpec((None,) + tuple(shape), lambda i: (idx,) + zeros, pipeline_mode=pl.Buffered(1))


def _normmod(x, g, shift, scale):
    ms = jnp.mean(x * x, axis=-1, keepdims=True)
    return (x * lax.rsqrt(ms + EPS) * g) * (1.0 + scale) + shift


def _dot(a, b):
    return jnp.dot(a, b, preferred_element_type=F32)


def _dot_nt(a, b, precision=None):
    return lax.dot_general(a, b, (((1,), (1,)), ((), ())), preferred_element_type=F32, precision=precision)


def _adaln_body(c_ref, w_ref, b_ref, o_ref):
    c = c_ref[...]
    s = c * jax.nn.sigmoid(c)
    o_ref[...] = _dot(s.astype(BF16), w_ref[...].astype(BF16)) + b_ref[...]


def _adaln(cond, ada_w, ada_b):
    depth = ada_w.shape[0]
    nj = 6
    return pl.pallas_call(
        _adaln_body,
        grid=(depth, nj),
        in_specs=[
            pl.BlockSpec((N_COND, D_MODEL), lambda l, j: (0, 0)),
            pl.BlockSpec((None, D_MODEL, D_MODEL), lambda l, j: (l, 0, j)),
            pl.BlockSpec((None, 1, D_MODEL), lambda l, j: (l, 0, j)),
        ],
        out_specs=pl.BlockSpec((None, N_COND, D_MODEL), lambda l, j: (l, 0, j)),
        out_shape=jax.ShapeDtypeStruct((depth, N_COND, 6 * D_MODEL), F32),
        compiler_params=_cparams(("arbitrary", "arbitrary")),
        name="adaln",
    )(cond, ada_w, ada_b.reshape(depth, 1, 6 * D_MODEL))


TM_PROJ = 1024


def _stream_tile(x_refs, tm):
    if len(x_refs) == 1:
        return x_refs[0][...]
    is_prompt = pl.program_id(0) < N_PROMPT_TOK // tm
    return jnp.where(is_prompt, x_refs[0][...], x_refs[1][...])


N_CHUNK_ROWS = N_TOK // S5_CHUNK
LANE_TILES = S5_W // 128


def _inproj_odd_body(x_ref, m_ref, g_ref, w_ref, qkv_ref, tu_ref, u_ref, wbf_ref):
    @pl.when(pl.program_id(0) == 0)
    def _():
        wbf_ref[...] = w_ref[...].astype(BF16)

    m = m_ref[...]
    h = _normmod(x_ref[...], g_ref[...], m[0:1], m[1:2])
    acc = _dot(h.astype(BF16), wbf_ref[...])
    qkv_ref[...] = acc[:, 0:3 * HALF].astype(BF16)
    for t in range(LANE_TILES):
        u_ref[t] = acc[:, 3 * HALF + t * 128:3 * HALF + (t + 1) * 128]
    rows = TM_PROJ // S5_CHUNK
    for j in range(S5_CHUNK):
        uj = jnp.concatenate([u_ref[t, pl.ds(j, rows, stride=S5_CHUNK), :] for t in range(LANE_TILES)], axis=1)
        if j:
            uj = pltpu.roll(uj, j * S5_CH, 1)
        for t in range(LANE_TILES):
            tu_ref[j, t] = uj[:, t * 128:(t + 1) * 128].astype(BF16)


def _inproj_odd(x, mods, layer, g, w_stack, w_idx):
    tm = TM_PROJ
    rows = tm // S5_CHUNK
    return pl.pallas_call(
        _inproj_odd_body,
        grid=(N_TOK // tm,),
        in_specs=[
            pl.BlockSpec((tm, D_MODEL), lambda i: (i, 0)),
            pl.BlockSpec((None, None, 6, D_MODEL), lambda i: (layer, _cond_row(i, tm), 0, 0)),
            pl.BlockSpec((1, D_MODEL), lambda i: (0, 0)),
            _resident_layer((D_MODEL, 4 * HALF), w_idx),
        ],
        out_specs=[
            pl.BlockSpec((tm, 3 * HALF), lambda i: (i, 0)),
            pl.BlockSpec((S5_CHUNK, LANE_TILES, rows, 128), lambda i: (0, 0, i, 0)),
        ],
        out_shape=[
            jax.ShapeDtypeStruct((N_TOK, 3 * HALF), BF16),
            jax.ShapeDtypeStruct((S5_CHUNK, LANE_TILES, N_CHUNK_ROWS, 128), BF16),
        ],
        scratch_shapes=[pltpu.VMEM((LANE_TILES, tm, 128), F32), pltpu.VMEM((D_MODEL, 4 * HALF), BF16)],
        compiler_params=_cparams(("arbitrary",)),
        name="inproj_odd",
    )(x, mods, g.reshape(1, D_MODEL), w_stack)


TM_MIX = 256
HALO = 16
RING = 3
LAG = 2
PAD_ROWS = TM_MIX + 2 * HALO + 8


def _even_step(phase, x_refs, m_ref, g_ref, w_ref, pw_ref, ps_ref, cw_ref, y_ref, xcat_ref, proj_ref, tail_ref,
               h_ref):
    i = pl.program_id(0)
    write_slot, cur_slot, next_slot = phase, (phase + 1) % RING, (phase + 2) % RING

    m = m_ref[...]
    x = _stream_tile(x_refs, TM_MIX)
    xcat_ref[...] = x
    h_ref[...] = _normmod(x, g_ref[...], m[0:1], m[1:2]).astype(BF16)
    n_pieces = len(POOL_WINDOWS) + HALF // 128
    pw = (4 * HALF) // n_pieces

    def project(piece):
        cols = slice(piece * pw, (piece + 1) * pw)
        proj_ref[write_slot, :, cols] = _dot(h_ref[...], w_ref[:, cols])

    t = jnp.maximum(i - LAG, 0)
    n_prompt_tiles = N_PROMPT_TOK // TM_MIX
    tiles_per_seq = DEC_SEQ // TM_MIX
    is_prompt = t < n_prompt_tiles
    tile_in_seq = jnp.where(is_prompt, 0, (t - n_prompt_tiles) % tiles_per_seq)
    seq_len = jnp.where(is_prompt, SEQ, DEC_SEQ)
    has_prev = tile_in_seq > 0
    has_next = jnp.logical_and(jnp.logical_not(is_prompt), tile_in_seq < tiles_per_seq - 1)
    prev_keep = jnp.where(has_prev, 1.0, 0.0).astype(F32)
    next_keep = jnp.where(has_next, 1.0, 0.0).astype(F32)
    pos = tile_in_seq * TM_MIX + lax.broadcasted_iota(jnp.int32, (TM_MIX, 1), 0)
    lo_rows, hi_rows = HALO, HALO + TM_MIX
    head = slice(0, HALO)

    slack = jnp.zeros((PAD_ROWS - TM_MIX - 2 * HALO, POOL_GROUP), F32)
    for gi, w in enumerate(POOL_WINDOWS):
        project(gi)
        cs = slice(gi * POOL_GROUP, (gi + 1) * POOL_GROUP)
        cur = proj_ref[cur_slot, :, cs]
        f = jnp.concatenate([tail_ref[:, cs] * prev_keep, cur, proj_ref[next_slot, head, cs] * next_keep, slack],
                            axis=0)
        n_rows, span = PAD_ROWS, 1
        while 2 * span < w:
            n_rows -= 8
            f = f[0:n_rows] + f[span:span + n_rows]
            span *= 2
        s = f[lo_rows - span:hi_rows - span] + f[lo_rows:hi_rows]
        lo = jnp.maximum(pos - w // 2, 0)
        hi = jnp.minimum(pos + (w - w // 2), seq_len)
        inv = 1.0 / (hi - lo).astype(F32)
        pooled = s * inv - cur
        mixed = _dot(pooled.astype(BF16), pw_ref[gi]) * ps_ref[:, cs]
        y_ref[:, cs] = mixed.astype(BF16)

    for ci in range(HALF // 128):
        project(len(POOL_WINDOWS) + ci)
        cs = slice(ci * 128, (ci + 1) * 128)
        bs = slice(HALF + ci * 128, HALF + (ci + 1) * 128)
        gs = slice(2 * HALF + ci * 128, 2 * HALF + (ci + 1) * 128)
        vs = slice(3 * HALF + ci * 128, 3 * HALF + (ci + 1) * 128)
        z = jnp.concatenate([(tail_ref[:, gs] * tail_ref[:, vs]) * prev_keep,
                             proj_ref[cur_slot, :, gs] * proj_ref[cur_slot, :, vs],
                             (proj_ref[next_slot, head, gs] * proj_ref[next_slot, head, vs]) * next_keep], axis=0)
        conv = (cw_ref[0:1, cs] * z[lo_rows - 1:hi_rows - 1]
                + cw_ref[1:2, cs] * z[lo_rows:hi_rows]
                + cw_ref[2:3, cs] * z[lo_rows + 1:hi_rows + 1])
        y_ref[:, bs] = (proj_ref[cur_slot, :, bs] * conv).astype(BF16)

    tail_ref[...] = proj_ref[cur_slot, TM_MIX - HALO:TM_MIX, :]


def _even_body(n_x, *refs):
    x_refs, rest = refs[:n_x], refs[n_x:]
    m_ref, g_ref, w_ref = rest[:3]
    proj_ref, tail_ref, h_ref, wbf_ref = rest[-4:]
    i = pl.program_id(0)

    @pl.when(i == 0)
    def _():
        proj_ref[...] = jnp.zeros_like(proj_ref)
        tail_ref[...] = jnp.zeros_like(tail_ref)
        wbf_ref[...] = w_ref[...].astype(BF16)

    for phase in range(RING):
        @pl.when(i % RING == phase)
        def _(phase=phase):
            _even_step(phase, x_refs, m_ref, g_ref, wbf_ref, *rest[3:-1])


def _even_layer_mix(xs, mods, layer, g, w_stack, w_idx, pool_w, pool_scale, conv_w):
    tm = TM_MIX
    n_tiles = N_TOK // tm
    npt = N_PROMPT_TOK // tm
    last = n_tiles - 1
    if len(xs) == 1:
        x_specs = [pl.BlockSpec((tm, D_MODEL), lambda i: (jnp.minimum(i, last), 0))]
    else:
        x_specs = [pl.BlockSpec((tm, D_MODEL), lambda i: (jnp.minimum(i, npt - 1), 0)),
                   pl.BlockSpec((tm, D_MODEL), lambda i: (jnp.clip(i - npt, 0, last - npt), 0))]
    return pl.pallas_call(
        functools.partial(_even_body, len(xs)),
        grid=(n_tiles + LAG,),
        in_specs=x_specs + [
            pl.BlockSpec((None, None, 6, D_MODEL), lambda i: (layer, _cond_row(jnp.minimum(i, last), tm), 0, 0)),
            pl.BlockSpec((1, D_MODEL), lambda i: (0, 0)),
            _resident_layer((D_MODEL, 4 * HALF), w_idx),
            pl.BlockSpec((len(POOL_WINDOWS), POOL_GROUP, POOL_GROUP), lambda i: (0, 0, 0)),
            pl.BlockSpec((1, HALF), lambda i: (0, 0)),
            pl.BlockSpec((3, HALF), lambda i: (0, 0)),
        ],
        out_specs=[pl.BlockSpec((tm, 2 * HALF), lambda i: (jnp.maximum(i - LAG, 0), 0)),
                   pl.BlockSpec((tm, D_MODEL), lambda i: (jnp.minimum(i, last), 0))],
        out_shape=[jax.ShapeDtypeStruct((N_TOK, 2 * HALF), BF16), jax.ShapeDtypeStruct((N_TOK, D_MODEL), F32)],
        scratch_shapes=[pltpu.VMEM((RING, tm, 4 * HALF), F32), pltpu.VMEM((HALO, 4 * HALF), F32),
                        pltpu.VMEM((tm, D_MODEL), BF16), pltpu.VMEM((D_MODEL, 4 * HALF), BF16)],
        compiler_params=_cparams(("arbitrary",)),
        name="even_layer_mix",
    )(*xs, mods, g.reshape(1, D_MODEL), w_stack, pool_w.astype(BF16), pool_scale.reshape(1, HALF), conv_w)


TM_MLP = 1024
RC_MLP = 256
N_CAST = 16


def _mlp_tile(final, x_ref, ya_ref, yb_ref, m_ref, g_ref, fg_ref, o_ref, wo_ref, w1_ref, w2_ref):
    m = m_ref[...]
    gain = g_ref[...] * (1.0 + m[4:5])
    for r in range(TM_MLP // RC_MLP):
        rows = slice(r * RC_MLP, (r + 1) * RC_MLP)
        mix = _dot(ya_ref[rows, :], wo_ref[0:HALF, :]) + _dot(yb_ref[rows, :], wo_ref[HALF:2 * HALF, :])
        x1 = x_ref[rows, :] + m[2:3] * mix
        ms = jnp.mean(x1 * x1, axis=-1, keepdims=True)
        h = (x1 * lax.rsqrt(ms + EPS) * gain + m[3:4]).astype(BF16)
        t = _dot(h, w1_ref[...])
        t = jnp.square(jnp.maximum(t, 0.0)).astype(BF16)
        out = x1 + m[5:6] * _dot(t, w2_ref[...])
        if final:
            ms = jnp.mean(out * out, axis=-1, keepdims=True)
            out = out * lax.rsqrt(ms + EPS) * fg_ref[...]
        o_ref[rows, :] = out


def _mlp_resident_body(final, x_ref, ya_ref, yb_ref, wo_ref, m_ref, g_ref, fg_ref, w1_ref, w2_ref, o_ref):
    _mlp_tile(final, x_ref, ya_ref, yb_ref, m_ref, g_ref, fg_ref, o_ref, wo_ref, w1_ref, w2_ref)


def _mlp_cast_body(final, emit, x_ref, ya_ref, yb_ref, wo_f, m_ref, g_ref, fg_ref, w1_f, w2_f, o_ref, *rest):
    wo_ref, w1_ref, w2_ref = rest[-3:]
    emitted = rest[:-3]
    i = pl.program_id(0)

    @pl.when(i < N_CAST)
    def _():
        for k, (f_ref, b_ref) in enumerate(((wo_f, wo_ref), (w1_f, w1_ref), (w2_f, w2_ref))):
            n = f_ref.shape[0]
            slab = f_ref[...].astype(BF16)
            b_ref[pl.ds(pl.multiple_of(i * n, n), n), :] = slab
            if emit:
                emitted[k][...] = slab

    @pl.when(i >= N_CAST)
    def _():
        _mlp_tile(final, x_ref, ya_ref, yb_ref, m_ref, g_ref, fg_ref, o_ref, wo_ref, w1_ref, w2_ref)


def _resident(shape):
    zeros = (0,) * len(shape)
    return pl.BlockSpec(shape, lambda i: zeros, pipeline_mode=pl.Buffered(1))


def _mlp_call(x, x_off, ya, ya_off, ya_col, yb, yb_off, yb_col, n_tiles, tile0, mods, layer, g2, final_g, final,
              f32_weights=None, bf16_weights=None, emit=False):
    tm = TM_MLP
    lead = N_CAST if f32_weights is not None else 0

    def tile(i):
        return jnp.maximum(i - lead, 0)

    stream_specs = [
        pl.BlockSpec((tm, D_MODEL), lambda i: (tile(i) + x_off, 0)),
        pl.BlockSpec((tm, HALF), lambda i: (tile(i) + ya_off, ya_col)),
        pl.BlockSpec((tm, HALF), lambda i: (tile(i) + yb_off, yb_col)),
    ]
    mod_spec = pl.BlockSpec((None, None, 6, D_MODEL), lambda i: (layer, _cond_row(tile(i) + tile0, tm), 0, 0))
    out_spec = pl.BlockSpec((tm, D_MODEL), lambda i: (tile(i), 0))
    out_shape = jax.ShapeDtypeStruct((n_tiles * tm, D_MODEL), F32)
    vecs = (g2.reshape(1, D_MODEL), final_g.reshape(1, D_MODEL))
    if f32_weights is None:
        w_out, w1, w2 = bf16_weights
        return pl.pallas_call(
            functools.partial(_mlp_resident_body, final),
            grid=(n_tiles,),
            in_specs=stream_specs + [_resident((D_MODEL, D_MODEL)), mod_spec, _resident((1, D_MODEL)),
                                     _resident((1, D_MODEL)), _resident((D_MODEL, D_FF)), _resident((D_FF, D_MODEL))],
            out_specs=out_spec,
            out_shape=out_shape,
            compiler_params=_cparams(("parallel",)),
            name="mlp",
        )(x, ya, yb, w_out, mods, *vecs, w1, w2)

    wo_stack, wo_idx, w1_stack, w2_stack = f32_weights

    def slab(i):
        return jnp.minimum(i, N_CAST - 1)

    shapes = ((D_MODEL, D_MODEL), (D_MODEL, D_FF), (D_FF, D_MODEL))
    slab_rows = [rows // N_CAST for rows, _ in shapes]
    f32_specs = [pl.BlockSpec((None, n, cols), lambda i, idx=idx: (idx, slab(i), 0))
                 for n, (_, cols), idx in zip(slab_rows, shapes, (wo_idx, layer, layer))]
    out_specs, out_shapes = [out_spec], [out_shape]
    if emit:
        out_specs += [pl.BlockSpec((n, cols), lambda i: (slab(i), 0)) for n, (_, cols) in zip(slab_rows, shapes)]
        out_shapes += [jax.ShapeDtypeStruct(shp, BF16) for shp in shapes]
    return pl.pallas_call(
        functools.partial(_mlp_cast_body, final, emit),
        grid=(N_CAST + n_tiles,),
        in_specs=stream_specs + [f32_specs[0], mod_spec, _resident((1, D_MODEL)), _resident((1, D_MODEL)),
                                 f32_specs[1], f32_specs[2]],
        out_specs=out_specs,
        out_shape=out_shapes,
        scratch_shapes=[pltpu.VMEM(shp, BF16) for shp in shapes],
        compiler_params=_cparams(("arbitrary",)),
        name="mlp_cast",
    )(x, ya, yb, wo_stack, mods, *vecs, w1_stack, w2_stack)


def _mlp(x, ya, ya_col, yb, yb_col, w_out_stack, w_out_idx, mods, layer, g2, final_g, w1_stack, w2_stack, final):
    tm = TM_MLP
    npt = N_PROMPT_TOK // tm
    f32_weights = (w_out_stack, w_out_idx, w1_stack, w2_stack)
    common = (mods, layer, g2, final_g, final)
    if not final:
        if isinstance(ya, tuple):
            ya = jnp.concatenate(ya, axis=0)
        return tuple(_mlp_call(x, 0, ya, 0, ya_col, yb, 0, yb_col, N_TOK // tm, 0, *common, f32_weights=f32_weights))
    ya_p, ya_s = ((ya[0], 0), (ya[1], 0)) if isinstance(ya, tuple) else ((ya, 0), (ya, npt))
    y_p, *bf16_weights = _mlp_call(x, 0, ya_p[0], ya_p[1], ya_col, yb, 0, yb_col, npt, 0, *common,
                                   f32_weights=f32_weights, emit=True)
    y_s = _mlp_call(x, npt, ya_s[0], ya_s[1], ya_col, yb, npt, yb_col, N_TOK // tm - npt, npt, *common,
                    bf16_weights=bf16_weights)
    return y_p, y_s


def _ctx_attn_body(q_ref, k_ref, v_ref, o_ref, ck_ref, cv_ref):
    lane = lax.broadcasted_iota(jnp.int32, (1, 2 * HEAD_DIM), 1)
    own = [lane < HEAD_DIM, lane >= HEAD_DIM]
    for h in range(N_HEADS):
        sl = slice(h * HEAD_DIM, (h + 1) * HEAD_DIM)
        ck_ref[h] = k_ref[:, sl].astype(F32)
        cv_ref[h] = v_ref[:, sl].astype(F32)
    for hp in range(N_HEADS // 2):
        sl = slice(hp * 2 * HEAD_DIM, (hp + 1) * 2 * HEAD_DIM)
        q = q_ref[:, sl] * jnp.asarray(HEAD_DIM ** -0.5, BF16)
        k = k_ref[:, sl]
        v = v_ref[:, sl]
        q2 = jnp.concatenate([jnp.where(own[0], q, jnp.zeros_like(q)),
                              jnp.where(own[1], q, jnp.zeros_like(q))], axis=0)
        s = _dot_nt(q2, k)
        m = jnp.max(s, axis=-1, keepdims=True)
        p = jnp.exp(s - m).astype(BF16)
        out = None
        for hh in range(2):
            num = _dot(p[hh * SEQ:(hh + 1) * SEQ], jnp.where(own[hh], v, jnp.ones_like(v)))
            den = jnp.where(own[hh], pltpu.roll(num, HEAD_DIM, 1), 1.0)
            o = num / den
            out = o if out is None else jnp.where(own[0], out, o)
        o_ref[:, sl] = out.astype(BF16)


def _ctx_attn(proj):
    return pl.pallas_call(
        _ctx_attn_body,
        grid=(BATCH,),
        in_specs=[
            pl.BlockSpec((SEQ, HALF), lambda b: (b, 0)),
            pl.BlockSpec((SEQ, HALF), lambda b: (b, 1)),
            pl.BlockSpec((SEQ, HALF), lambda b: (b, 2)),
        ],
        out_specs=[
            pl.BlockSpec((SEQ, HALF), lambda b: (b, 0)),
            pl.BlockSpec((None, None, N_HEADS, SEQ, HEAD_DIM), lambda b: (b, 0, 0, 0, 0)),
            pl.BlockSpec((None, None, N_HEADS, SEQ, HEAD_DIM), lambda b: (b, 0, 0, 0, 0)),
        ],
        out_shape=[
            jax.ShapeDtypeStruct((N_PROMPT_TOK, HALF), BF16),
            jax.ShapeDtypeStruct((BATCH, 1, N_HEADS, SEQ, HEAD_DIM), F32),
            jax.ShapeDtypeStruct((BATCH, 1, N_HEADS, SEQ, HEAD_DIM), F32),
        ],
        compiler_params=_cparams(("parallel",)),
        name="ctx_attn",
    )(proj, proj, proj)


NA_MASKED = -math.inf
NA_QROWS = 4
NA_KROWS = NA_QROWS + WIN_H
NA_NQ = NA_QROWS * GRID_W
NA_NK = NA_KROWS * GRID_W
NA_BLOCKS = GRID_ROWS // NA_QROWS
NA_TYPES = 3


def _na_col_table(rpb):
    qc = jnp.arange(GRID_W)[:, None]
    kc = jnp.arange(GRID_W)[None, :]
    c0 = jnp.clip(qc - WIN_W // 2, 0, GRID_W - WIN_W)
    valid = (kc >= c0) & (kc < c0 + WIN_W)
    coff = jnp.clip(kc - qc, -(WIN_W - 1), WIN_W - 1) + (WIN_W - 1)
    onehot = (coff.reshape(1, -1) == jnp.arange(2 * WIN_W - 1)[:, None]).astype(F32)
    t = jnp.dot(rpb.astype(F32).reshape(N_HEADS * (2 * WIN_H - 1), 2 * WIN_W - 1), onehot,
                precision=lax.Precision.HIGHEST)
    t = t.reshape(N_HEADS, 2 * WIN_H - 1, GRID_W, GRID_W)
    return jnp.where(valid[None, None], t, NA_MASKED)


def _na_window_start(rb):
    return min(max(NA_QROWS * rb - WIN_H // 2, 0), GRID_ROWS - NA_KROWS)


def _na_body(q_ref, k_ref, v_ref, ck_ref, cv_ref, tab_ref, o_ref, bias_ref, kc_ref, vc_ref, va_ref):
    @pl.when(pl.program_id(1) == 0)
    def _():
        masked = jnp.full((GRID_W, GRID_W), NA_MASKED, F32)
        for hh in range(2):
            for ty in range(NA_TYPES):
                q0 = (0, NA_QROWS, GRID_ROWS - NA_QROWS)[ty]
                ks = (0, 0, GRID_ROWS - NA_KROWS)[ty]
                for a in range(NA_QROWS):
                    r0 = min(max(q0 + a - WIN_H // 2, 0), GRID_ROWS - WIN_H)
                    for y in range(NA_KROWS):
                        kr = ks + y
                        ok = r0 <= kr < r0 + WIN_H
                        blk = tab_ref[hh, kr - (q0 + a) + WIN_H - 1] if ok else masked
                        bias_ref[ty, hh * NA_NQ + a * GRID_W:hh * NA_NQ + (a + 1) * GRID_W,
                                 y * GRID_W:(y + 1) * GRID_W] = blk

    lane = lax.broadcasted_iota(jnp.int32, (1, 2 * HEAD_DIM), 1)
    own = [lane < HEAD_DIM, lane >= HEAD_DIM]
    kc_ref[...] = jnp.concatenate([ck_ref[0], ck_ref[1]], axis=1).astype(BF16)
    vc = jnp.concatenate([cv_ref[0], cv_ref[1]], axis=1)
    v = v_ref[...]
    for hh in range(2):
        vc_ref[hh] = jnp.where(own[hh], vc, 1.0).astype(BF16)
        va_ref[hh] = jnp.where(own[hh], v, jnp.ones_like(v))

    def block(rb, ty):
        qs = rb * NA_NQ
        ks = _na_window_start(rb) * GRID_W
        q = q_ref[pl.ds(qs, NA_NQ), :] * jnp.asarray(HEAD_DIM ** -0.5, BF16)
        q2 = jnp.concatenate([jnp.where(own[0], q, jnp.zeros_like(q)),
                              jnp.where(own[1], q, jnp.zeros_like(q))], axis=0)
        s_loc = _dot_nt(q2, k_ref[pl.ds(ks, NA_NK), :]) + bias_ref[ty]
        s_ctx = _dot_nt(q2, kc_ref[...])
        m = jnp.maximum(jnp.max(s_loc, axis=-1, keepdims=True), jnp.max(s_ctx, axis=-1, keepdims=True))
        p_loc = jnp.exp(s_loc - m).astype(BF16)
        p_ctx = jnp.exp(s_ctx - m).astype(BF16)
        out = None
        for hh in range(2):
            rows = slice(hh * NA_NQ, (hh + 1) * NA_NQ)
            num = _dot(p_loc[rows], va_ref[hh, pl.ds(ks, NA_NK), :]) + _dot(p_ctx[rows], vc_ref[hh])
            den = jnp.where(own[hh], pltpu.roll(num, HEAD_DIM, 1), 1.0)
            o = num / den
            out = o if out is None else jnp.where(own[0], out, o)
        o_ref[pl.ds(qs, NA_NQ), :] = out.astype(BF16)

    for rb in range(NA_BLOCKS):
        block(rb, 0 if rb == 0 else (2 if rb == NA_BLOCKS - 1 else 1))


def _na_attn(proj, cache_k, cache_v, col_table, layer_idx):
    row_blk0 = N_PROMPT_TOK // DEC_SEQ
    hp = N_HEADS // 2
    return pl.pallas_call(
        _na_body,
        grid=(hp, DEC_BATCH),
        in_specs=[
            pl.BlockSpec((DEC_SEQ, 128), lambda h, b: (row_blk0 + b, h)),
            pl.BlockSpec((DEC_SEQ, 128), lambda h, b: (row_blk0 + b, hp + h)),
            pl.BlockSpec((DEC_SEQ, 128), lambda h, b: (row_blk0 + b, 2 * hp + h)),
            pl.BlockSpec((None, None, 2, PAST_LEN, HEAD_DIM), lambda h, b: (b, layer_idx, h, 0, 0)),
            pl.BlockSpec((None, None, 2, PAST_LEN, HEAD_DIM), lambda h, b: (b, layer_idx, h, 0, 0)),
            pl.BlockSpec((2, 2 * WIN_H - 1, GRID_W, GRID_W), lambda h, b: (h, 0, 0, 0)),
        ],
        out_specs=pl.BlockSpec((DEC_SEQ, 128), lambda h, b: (b, h)),
        out_shape=jax.ShapeDtypeStruct((N_SAMPLE_TOK, HALF), BF16),
        scratch_shapes=[pltpu.VMEM((NA_TYPES, 2 * NA_NQ, NA_NK), F32),
                        pltpu.VMEM((PAST_LEN, 2 * HEAD_DIM), BF16),
                        pltpu.VMEM((2, PAST_LEN, 2 * HEAD_DIM), BF16),
                        pltpu.VMEM((2, DEC_SEQ, 2 * HEAD_DIM), BF16)],
        compiler_params=_cparams(("parallel", "arbitrary")),
        name="na_attn",
    )(proj, proj, proj, cache_k, cache_v, col_table)


N_POW = 24


def _s5_prep_body(vec_ref, bt_ref, c_ref, wend_ref, ktoe_ref, woutT_ref, apow_ref, pw_ref, pct_ref):
    L = S5_CHUNK
    p = pl.program_id(0)
    lane_blk = lax.broadcasted_iota(jnp.int32, (S5_CH, S5_W), 1) // S5_CH
    src_blk = (lane_blk + L - p) % L
    krows = []
    for d in range(2):
        v = vec_ref[d]
        lre, lim, step = v[0:1], v[1:2], v[2:3]
        a = lre * step
        b = lim * step
        ea = jnp.exp(a)
        nr = ea * jnp.cos(b) - 1.0
        ni = ea * jnp.sin(b)
        den = lre * lre + lim * lim
        qr = (nr * lre + ni * lim) / den
        qi = (ni * lre - nr * lim) / den
        bre, bim = bt_ref[d, 0], bt_ref[d, 1]
        bqr = bre * qr - bim * qi
        bqi = bre * qi + bim * qr
        cre, cim = c_ref[d, 0], c_ref[d, 1]

        e = lax.broadcasted_iota(jnp.int32, (N_POW, 1), 0).astype(F32)
        mag = jnp.exp(e * a)
        pw_ref[0] = mag * jnp.cos(e * b)
        pw_ref[1] = mag * jnp.sin(e * b)
        mag = jnp.exp((L * e) * a)
        apow_ref[d, 0] = mag * jnp.cos((L * e) * b)
        apow_ref[d, 1] = mag * jnp.sin((L * e) * b)

        def power(k):
            return pw_ref[0, k:k + 1, :], pw_ref[1, k:k + 1, :]

        def c_block(k):
            pr, pi = power(k)
            return cre * pr - cim * pi, -(cre * pi + cim * pr)

        for i in range(L):
            rows = pl.ds(pl.multiple_of(((i + p) % L) * S5_CH, S5_CH), S5_CH)
            pr, pi = power(L - 1 - i if d == 0 else i)
            wend_ref[rows, d * 256:d * 256 + 128] = (bqr * pr - bqi * pi).astype(BF16)
            wend_ref[rows, d * 256 + 128:d * 256 + 256] = (bqr * pi + bqi * pr).astype(BF16)
            mre, mim = c_block(i + 1 if d == 0 else L - i)
            woutT_ref[rows, d * 256:d * 256 + 128] = mre.astype(BF16)
            woutT_ref[rows, d * 256 + 128:d * 256 + 256] = mim.astype(BF16)
            lag_rows = slice(i * S5_CH, (i + 1) * S5_CH)
            mre, mim = c_block(i if d == 0 else L - 1 - i)
            pct_ref[lag_rows, 0:128] = mre
            pct_ref[lag_rows, 128:256] = mim

        bqcat = jnp.concatenate([bqr, bqi], axis=1)
        krows.append(_dot_nt(bqcat, pct_ref[...], precision=lax.Precision.HIGHEST))

    base_f = pltpu.roll(krows[0], p * S5_CH, 1)
    base_b = pltpu.roll(krows[1], p * S5_CH, 1)
    for i in range(L):
        rows = pl.ds(pl.multiple_of(((i + p) % L) * S5_CH, S5_CH), S5_CH)
        blk_f = base_f if i == 0 else pltpu.roll(base_f, i * S5_CH, 1)
        blk_b = base_b if (i + 1) % L == 0 else pltpu.roll(base_b, ((i + 1) % L) * S5_CH, 1)
        blk = jnp.where(src_blk >= i, blk_f, 0.0) + jnp.where(src_blk <= i, blk_b, 0.0)
        ktoe_ref[rows, :] = blk.astype(BF16)


def _s5_prep(lam_re, lam_im, log_step, b_re, b_im, c_re, c_im):
    def pair_vec(a):
        return a.astype(F32).reshape(2, S5_PAIRS, 1, S5_P).transpose(1, 0, 2, 3)

    step = jnp.broadcast_to(jnp.exp(log_step.astype(F32))[:, :, None], (2, N_SSM_GROUPS, SSM_STATE))
    vec = jnp.concatenate([pair_vec(lam_re), pair_vec(lam_im), pair_vec(step),
                           jnp.zeros((S5_PAIRS, 2, 5, S5_P), F32)], axis=2)

    def block_diag_ch_by_state(m):
        m = m.astype(F32).reshape(2, S5_PAIRS, 2, SSM_GROUP, SSM_STATE)
        eye = jnp.eye(2, dtype=F32)
        out = m[:, :, :, :, None, :] * eye[None, None, :, None, :, None]
        return out.reshape(2, S5_PAIRS, S5_CH, S5_P).transpose(1, 0, 2, 3)

    bt = jnp.stack([block_diag_ch_by_state(b_re.transpose(0, 1, 3, 2)),
                    block_diag_ch_by_state(b_im.transpose(0, 1, 3, 2))], axis=2)
    cc = jnp.stack([block_diag_ch_by_state(c_re), block_diag_ch_by_state(c_im)], axis=2)

    mat = jax.ShapeDtypeStruct((S5_PAIRS, S5_W, S5_W), BF16)
    mat_spec = pl.BlockSpec((None, S5_W, S5_W), lambda p: (p, 0, 0))
    return pl.pallas_call(
        _s5_prep_body,
        grid=(S5_PAIRS,),
        in_specs=[
            pl.BlockSpec((None, 2, 8, S5_P), lambda p: (p, 0, 0, 0)),
            pl.BlockSpec((None, 2, 2, S5_CH, S5_P), lambda p: (p, 0, 0, 0, 0)),
            pl.BlockSpec((None, 2, 2, S5_CH, S5_P), lambda p: (p, 0, 0, 0, 0)),
        ],
        out_specs=[mat_spec, mat_spec, mat_spec,
                   pl.BlockSpec((None, 2, 2, N_POW, S5_P), lambda p: (p, 0, 0, 0, 0))],
        out_shape=[mat, mat, mat, jax.ShapeDtypeStruct((S5_PAIRS, 2, 2, N_POW, S5_P), F32)],
        scratch_shapes=[pltpu.VMEM((2, N_POW, S5_P), F32), pltpu.VMEM((S5_W, 2 * S5_P), F32)],
        compiler_params=_cparams(("parallel",)),
        name="s5_prep",
    )(vec, bt, cc)


S5_SEG_ROWS = 256
S5_N_SEG = N_CHUNK_ROWS // S5_SEG_ROWS
S5_SUB = S5_SEG_ROWS // S5_CHUNK
S5_SUB_PER_SEQ = (DEC_SEQ // S5_CHUNK) // S5_CHUNK


def _s5_body(tu_ref, s0_ref, wend_ref, ktoe_ref, woutT_ref, apow_ref, ty_ref, fin_ref, z_ref, xs_ref, g_ref):
    seg = pl.program_id(0)
    p = pl.program_id(1)
    L = S5_CHUNK
    lane_blk = lax.broadcasted_iota(jnp.int32, (S5_SEG_ROWS, 128), 1) // S5_CH

    tiles = []
    for t in range(LANE_TILES):
        tile = None
        for k in range(128 // S5_CH):
            j = (LANE_TILES * t + k + L - p) % L
            blk = tu_ref[j, t]
            tile = blk if tile is None else jnp.where(lane_blk == k, blk, tile)
        tiles.append(tile)
    s = jnp.concatenate(tiles, axis=1)
    z = _dot(s, wend_ref[...])
    for q in range(LANE_TILES):
        z_ref[q] = z[:, q * 128:(q + 1) * 128]

    ends = []
    for d in range(2):
        are, aim = apow_ref[d, 0, 1:2, :], apow_ref[d, 1, 1:2, :]
        xr = jnp.zeros((S5_SUB, S5_P), F32)
        xi = jnp.zeros((S5_SUB, S5_P), F32)
        for c in (range(L) if d == 0 else reversed(range(L))):
            rows = pl.ds(c, S5_SUB, stride=L)
            xs_ref[2 * d, rows, :] = xr
            xs_ref[2 * d + 1, rows, :] = xi
            zr = z_ref[2 * d, rows, :]
            zi = z_ref[2 * d + 1, rows, :]
            xr, xi = are * xr - aim * xi + zr, are * xi + aim * xr + zi
        ends.append((xr, xi))
        fin_ref[d, 0] = xr
        fin_ref[d, 1] = xi

    @pl.when(seg > 0)
    def _():
        for d in range(2):
            er, ei = ends[d]
            a16r, a16i = apow_ref[d, 0, L:L + 1, :], apow_ref[d, 1, L:L + 1, :]
            for bb in range(S5_SUB // S5_SUB_PER_SEQ):
                gr = s0_ref[bb, 2 * d:2 * d + 1, :]
                gi = s0_ref[bb, 2 * d + 1:2 * d + 2, :]
                order = range(S5_SUB_PER_SEQ) if d == 0 else reversed(range(S5_SUB_PER_SEQ))
                prev = None
                for sub in order:
                    row = bb * S5_SUB_PER_SEQ + sub
                    if prev is not None:
                        gr, gi = (a16r * gr - a16i * gi + er[prev:prev + 1, :],
                                  a16r * gi + a16i * gr + ei[prev:prev + 1, :])
                    g_ref[0, row:row + 1, :] = gr
                    g_ref[1, row:row + 1, :] = gi
                    prev = row
            if d == 0:
                pr, pi = apow_ref[d, 0, 0:L, :], apow_ref[d, 1, 0:L, :]
            else:
                pr = jnp.concatenate([apow_ref[d, 0, k:k + 1, :] for k in reversed(range(L))], axis=0)
                pi = jnp.concatenate([apow_ref[d, 1, k:k + 1, :] for k in reversed(range(L))], axis=0)
            for sub in range(S5_SUB):
                rows = slice(sub * L, (sub + 1) * L)
                gr = g_ref[0, sub:sub + 1, :]
                gi = g_ref[1, sub:sub + 1, :]
                xs_ref[2 * d, rows, :] = xs_ref[2 * d, rows, :] + (pr * gr - pi * gi)
                xs_ref[2 * d + 1, rows, :] = xs_ref[2 * d + 1, rows, :] + (pr * gi + pi * gr)

    xs = jnp.concatenate([xs_ref[q].astype(BF16) for q in range(LANE_TILES)], axis=1)
    y = _dot(s, ktoe_ref[...]) + _dot_nt(xs, woutT_ref[...])

    @pl.when(p == 0)
    def _():
        ty_ref[...] = jnp.zeros_like(ty_ref)

    for t in range(LANE_TILES):
        yt = y[:, t * 128:(t + 1) * 128]
        for k in range(128 // S5_CH):
            j = (LANE_TILES * t + k + L - p) % L
            ty_ref[j, t] = jnp.where(lane_blk == k, yt, ty_ref[j, t])


def _s5_scan(tu, s0, wend, ktoe, woutT, apow):
    mat_spec = pl.BlockSpec((None, S5_W, S5_W), lambda g, p: (p, 0, 0))
    seqs = S5_SUB // S5_SUB_PER_SEQ
    return pl.pallas_call(
        _s5_body,
        grid=(S5_N_SEG, S5_PAIRS),
        in_specs=[
            pl.BlockSpec((S5_CHUNK, LANE_TILES, S5_SEG_ROWS, 128), lambda g, p: (0, 0, g, 0)),
            pl.BlockSpec((None, seqs, 4, S5_P), lambda g, p: (p, jnp.maximum(g - 1, 0), 0, 0)),
            mat_spec, mat_spec, mat_spec,
            pl.BlockSpec((None, 2, 2, N_POW, S5_P), lambda g, p: (p, 0, 0, 0, 0)),
        ],
        out_specs=[
            pl.BlockSpec((S5_CHUNK, LANE_TILES, S5_SEG_ROWS, 128), lambda g, p: (0, 0, g, 0)),
            pl.BlockSpec((None, 2, 2, S5_SUB, S5_P), lambda g, p: (p, 0, 0, g, 0)),
        ],
        out_shape=[
            jax.ShapeDtypeStruct((S5_CHUNK, LANE_TILES, N_CHUNK_ROWS, 128), F32),
            jax.ShapeDtypeStruct((S5_PAIRS, 2, 2, S5_N_SEG * S5_SUB, S5_P), F32),
        ],
        scratch_shapes=[pltpu.VMEM((LANE_TILES, S5_SEG_ROWS, 128), F32),
                        pltpu.VMEM((LANE_TILES, S5_SEG_ROWS, 128), F32),
                        pltpu.VMEM((2, S5_SUB, S5_P), F32)],
        compiler_params=_cparams(("parallel", "arbitrary")),
        name="s5_scan",
    )(tu, s0, wend, ktoe, woutT, apow)


TM_GLU = 2048


def _glu_body(tu_ref, ty_ref, d_ref, gw_ref, gb_ref, o_ref, out_ref):
    rows = TM_GLU // S5_CHUNK
    d = d_ref[...]
    pre = []
    for j in range(S5_CHUNK):
        uj = jnp.concatenate([tu_ref[j, t].astype(F32) for t in range(LANE_TILES)], axis=1)
        yj = jnp.concatenate([ty_ref[j, t] for t in range(LANE_TILES)], axis=1)
        dj = pltpu.roll(d, j * S5_CH, 1) if j else d
        yj = dj * uj + yj
        back = ((S5_CHUNK - j) % S5_CHUNK) * S5_CH
        pre.append(pltpu.roll(yj, back, 1) if back else yj)
    y = jnp.concatenate(pre, axis=0)
    cdf = 0.5 * (1.0 + jnp.tanh(math.sqrt(2.0 / math.pi) * (y + 0.044715 * (y * y * y))))
    y = y * cdf
    gate = jax.nn.sigmoid(_dot(y.astype(BF16), gw_ref[...]) + gb_ref[...])
    out = y * gate
    for j in range(S5_CHUNK):
        oj = out[j * rows:(j + 1) * rows]
        for t in range(LANE_TILES):
            out_ref[t, pl.ds(j, rows, stride=S5_CHUNK), :] = oj[:, t * 128:(t + 1) * 128]
    o_ref[...] = jnp.concatenate([out_ref[t] for t in range(LANE_TILES)], axis=1).astype(BF16)


def _glu(tu, ty, ssm_d, glu_w, glu_b):
    tm = TM_GLU
    rows = tm // S5_CHUNK
    lay_spec = pl.BlockSpec((S5_CHUNK, LANE_TILES, rows, 128), lambda i: (0, 0, i, 0))
    return pl.pallas_call(
        _glu_body,
        grid=(N_TOK // tm,),
        in_specs=[
            lay_spec, lay_spec,
            pl.BlockSpec((1, HALF), lambda i: (0, 0)),
            pl.BlockSpec((HALF, HALF), lambda i: (0, 0)),
            pl.BlockSpec((1, HALF), lambda i: (0, 0)),
        ],
        out_specs=pl.BlockSpec((tm, HALF), lambda i: (i, 0)),
        out_shape=jax.ShapeDtypeStruct((N_TOK, HALF), BF16),
        scratch_shapes=[pltpu.VMEM((LANE_TILES, tm, 128), F32)],
        compiler_params=_cparams(("parallel",)),
        name="s5_glu",
    )(tu, ty, ssm_d.reshape(1, HALF), glu_w.astype(BF16), glu_b.reshape(1, HALF))


def kernel(x_prompt, x_sample, c, cache_k, cache_v, state_s5_re, state_s5_im, c_ctx, norm1_g, norm2_g, ada_w, ada_b, mlp_w1, mlp_w2, ab_w_in, pool_w, pool_scale, conv_w, ab_w_out, cd_w_in, na_rpb, ssm_lambda_re, ssm_lambda_im, ssm_log_step, ssm_b_re, ssm_b_im, ssm_c_re, ssm_c_im, ssm_d, glu_w, glu_b, cd_w_out, final_g):
    depth = ada_w.shape[0]
    xs = (x_prompt.reshape(N_PROMPT_TOK, D_MODEL), x_sample.reshape(N_SAMPLE_TOK, D_MODEL))
    cond = jnp.concatenate([c_ctx[None, :], c, jnp.zeros((N_COND - 1 - DEC_BATCH, D_MODEL), F32)], axis=0)
    mods = _adaln(cond, ada_w, ada_b).reshape(depth, N_COND, 6, D_MODEL)

    new_k, new_v, new_re, new_im = [], [], [], []
    for layer in range(depth):
        i = layer // 2
        final = layer == depth - 1
        if layer % 2 == 0:
            y, x = _even_layer_mix(xs, mods, layer, norm1_g[layer], ab_w_in, i,
                                   pool_w[i], pool_scale[i], conv_w[i])
            ya, ya_col, yb, yb_col = y, 0, y, 1
            w_out_stack = ab_w_out
        else:
            x = xs[0]
            qkv, tu = _inproj_odd(x, mods, layer, norm1_g[layer], cd_w_in, i)
            o_p, ck, cv = _ctx_attn(qkv)
            new_k.append(ck)
            new_v.append(cv)
            o_s = _na_attn(qkv, cache_k, cache_v, _na_col_table(na_rpb[i]), i)
            attn = (o_p, o_s)

            wend, ktoe, woutT, apow = _s5_prep(ssm_lambda_re[i], ssm_lambda_im[i], ssm_log_step[i],
                                               ssm_b_re[i], ssm_b_im[i], ssm_c_re[i], ssm_c_im[i])

            def pair_state(s):
                return s.astype(F32).reshape(DEC_BATCH, 2, S5_PAIRS, S5_P).transpose(2, 0, 1, 3)

            s0 = jnp.stack([pair_state(state_s5_re[:, i]), pair_state(state_s5_im[:, i])], axis=3)
            s0 = s0.reshape(S5_PAIRS, DEC_BATCH, 4, S5_P)
            ty, fin = _s5_scan(tu, s0, wend, ktoe, woutT, apow)
            fin = fin[:, :, :, :BATCH]
            new_re.append(fin[:, :, 0].transpose(2, 1, 0, 3).reshape(BATCH, 2, N_SSM_GROUPS, SSM_STATE))
            new_im.append(fin[:, :, 1].transpose(2, 1, 0, 3).reshape(BATCH, 2, N_SSM_GROUPS, SSM_STATE))
            d_out = _glu(tu, ty, ssm_d[i], glu_w[i], glu_b[i])
            ya, ya_col, yb, yb_col = attn, 0, d_out, 0
            w_out_stack = cd_w_out
        xs = _mlp(x, ya, ya_col, yb, yb_col, w_out_stack, i, mods, layer, norm2_g[layer], final_g,
                  mlp_w1, mlp_w2, final)

    y_prompt = xs[0].reshape(BATCH, SEQ, D_MODEL)
    y_sample = xs[1].reshape(DEC_BATCH, DEC_SEQ, D_MODEL)
    return (y_prompt, y_sample, jnp.concatenate(new_k, axis=1), jnp.concatenate(new_v, axis=1),
            jnp.stack(new_re, axis=1), jnp.stack(new_im, axis=1))
```

```python
import functools
import math

import jax
import jax.numpy as jnp
from jax import lax
from jax.experimental import pallas as pl
from jax.experimental.pallas import tpu as pltpu

F32 = jnp.float32
BF16 = jnp.bfloat16

D_MODEL = 1024
BATCH = 16
SEQ = 256
DEC_BATCH = 4
DEC_SEQ = 2048
PAST_LEN = 256
GRID_W = 64
GRID_ROWS = DEC_SEQ // GRID_W
HALF = 512
POOL_WINDOWS = (2, 4, 8, 16)
POOL_GROUP = 128
HEAD_DIM = 64
N_HEADS = 8
WIN_H = 8
WIN_W = 16
SSM_GROUP = 16
N_SSM_GROUPS = 32
SSM_STATE = 64
D_FF = 4096
EPS = 1e-6

N_PROMPT_TOK = BATCH * SEQ
N_SAMPLE_TOK = DEC_BATCH * DEC_SEQ
N_TOK = N_PROMPT_TOK + N_SAMPLE_TOK
N_COND = 8

S5_CHUNK = 16
S5_PAIRS = N_SSM_GROUPS // 2
S5_CH = 2 * SSM_GROUP
S5_P = 2 * SSM_STATE
S5_W = S5_CHUNK * S5_CH

VMEM_LIMIT = 56 * 1024 * 1024


def _cparams(sem):
    return pltpu.CompilerParams(dimension_semantics=sem, vmem_limit_bytes=VMEM_LIMIT)


def _cond_row(i, tm):
    npt = N_PROMPT_TOK // tm
    per = DEC_SEQ // tm
    return jnp.where(i < npt, 0, 1 + (i - npt) // per)


def _resident_layer(shape, idx):
    zeros = (0,) * len(shape)
    return pl.BlockSpec((None,) + tuple(shape), lambda i: (idx,) + zeros, pipeline_mode=pl.Buffered(1))


def _normmod(x, g, shift, scale):
    ms = jnp.mean(x * x, axis=-1, keepdims=True)
    return (x * lax.rsqrt(ms + EPS) * g) * (1.0 + scale) + shift


def _dot(a, b):
    return jnp.dot(a, b, preferred_element_type=F32)


def _dot_nt(a, b, precision=None):
    return lax.dot_general(a, b, (((1,), (1,)), ((), ())), preferred_element_type=F32, precision=precision)


def _adaln_body(c_ref, w_ref, b_ref, o_ref):
    c = c_ref[...]
    s = c * jax.nn.sigmoid(c)
    o_ref[...] = _dot(s.astype(BF16), w_ref[...].astype(BF16)) + b_ref[...]


def _adaln(cond, ada_w, ada_b):
    depth = ada_w.shape[0]
    nj = 6
    return pl.pallas_call(
        _adaln_body,
        grid=(depth, nj),
        in_specs=[
            pl.BlockSpec((N_COND, D_MODEL), lambda l, j: (0, 0)),
            pl.BlockSpec((None, D_MODEL, D_MODEL), lambda l, j: (l, 0, j)),
            pl.BlockSpec((None, 1, D_MODEL), lambda l, j: (l, 0, j)),
        ],
        out_specs=pl.BlockSpec((None, N_COND, D_MODEL), lambda l, j: (l, 0, j)),
        out_shape=jax.ShapeDtypeStruct((depth, N_COND, 6 * D_MODEL), F32),
        compiler_params=_cparams(("arbitrary", "arbitrary")),
        name="adaln",
    )(cond, ada_w, ada_b.reshape(depth, 1, 6 * D_MODEL))


TM_PROJ = 1024


def _stream_tile(x_refs, tm):
    if len(x_refs) == 1:
        return x_refs[0][...]
    is_prompt = pl.program_id(0) < N_PROMPT_TOK // tm
    return jnp.where(is_prompt, x_refs[0][...], x_refs[1][...])


N_CHUNK_ROWS = N_TOK // S5_CHUNK
LANE_TILES = S5_W // 128


def _inproj_odd_body(x_ref, m_ref, g_ref, w_ref, qkv_ref, tu_ref, u_ref, wbf_ref):
    @pl.when(pl.program_id(0) == 0)
    def _():
        wbf_ref[...] = w_ref[...].astype(BF16)

    m = m_ref[...]
    h = _normmod(x_ref[...], g_ref[...], m[0:1], m[1:2])
    acc = _dot(h.astype(BF16), wbf_ref[...])
    qkv_ref[...] = acc[:, 0:3 * HALF].astype(BF16)
    for t in range(LANE_TILES):
        u_ref[t] = acc[:, 3 * HALF + t * 128:3 * HALF + (t + 1) * 128]
    rows = TM_PROJ // S5_CHUNK
    for j in range(S5_CHUNK):
        uj = jnp.concatenate([u_ref[t, pl.ds(j, rows, stride=S5_CHUNK), :] for t in range(LANE_TILES)], axis=1)
        if j:
            uj = pltpu.roll(uj, j * S5_CH, 1)
        for t in range(LANE_TILES):
            tu_ref[j, t] = uj[:, t * 128:(t + 1) * 128].astype(BF16)


def _inproj_odd(x, mods, layer, g, w_stack, w_idx):
    tm = TM_PROJ
    rows = tm // S5_CHUNK
    return pl.pallas_call(
        _inproj_odd_body,
        grid=(N_TOK // tm,),
        in_specs=[
            pl.BlockSpec((tm, D_MODEL), lambda i: (i, 0)),
            pl.BlockSpec((None, None, 6, D_MODEL), lambda i: (layer, _cond_row(i, tm), 0, 0)),
            pl.BlockSpec((1, D_MODEL), lambda i: (0, 0)),
            _resident_layer((D_MODEL, 4 * HALF), w_idx),
        ],
        out_specs=[
            pl.BlockSpec((tm, 3 * HALF), lambda i: (i, 0)),
            pl.BlockSpec((S5_CHUNK, LANE_TILES, rows, 128), lambda i: (0, 0, i, 0)),
        ],
        out_shape=[
            jax.ShapeDtypeStruct((N_TOK, 3 * HALF), BF16),
            jax.ShapeDtypeStruct((S5_CHUNK, LANE_TILES, N_CHUNK_ROWS, 128), BF16),
        ],
        scratch_shapes=[pltpu.VMEM((LANE_TILES, tm, 128), F32), pltpu.VMEM((D_MODEL, 4 * HALF), BF16)],
        compiler_params=_cparams(("arbitrary",)),
        name="inproj_odd",
    )(x, mods, g.reshape(1, D_MODEL), w_stack)


TM_MIX = 256
HALO = 16
RING = 3
LAG = 2
PAD_ROWS = TM_MIX + 2 * HALO + 8


def _even_step(phase, x_refs, m_ref, g_ref, w_ref, pw_ref, ps_ref, cw_ref, y_ref, xcat_ref, proj_ref, tail_ref,
               h_ref):
    i = pl.program_id(0)
    write_slot, cur_slot, next_slot = phase, (phase + 1) % RING, (phase + 2) % RING

    m = m_ref[...]
    x = _stream_tile(x_refs, TM_MIX)
    xcat_ref[...] = x
    h_ref[...] = _normmod(x, g_ref[...], m[0:1], m[1:2]).astype(BF16)
    n_pieces = len(POOL_WINDOWS) + HALF // 128
    pw = (4 * HALF) // n_pieces

    def project(piece):
        cols = slice(piece * pw, (piece + 1) * pw)
        proj_ref[write_slot, :, cols] = _dot(h_ref[...], w_ref[:, cols])

    t = jnp.maximum(i - LAG, 0)
    n_prompt_tiles = N_PROMPT_TOK // TM_MIX
    tiles_per_seq = DEC_SEQ // TM_MIX
    is_prompt = t < n_prompt_tiles
    tile_in_seq = jnp.where(is_prompt, 0, (t - n_prompt_tiles) % tiles_per_seq)
    seq_len = jnp.where(is_prompt, SEQ, DEC_SEQ)
    has_prev = tile_in_seq > 0
    has_next = jnp.logical_and(jnp.logical_not(is_prompt), tile_in_seq < tiles_per_seq - 1)
    prev_keep = jnp.where(has_prev, 1.0, 0.0).astype(F32)
    next_keep = jnp.where(has_next, 1.0, 0.0).astype(F32)
    pos = tile_in_seq * TM_MIX + lax.broadcasted_iota(jnp.int32, (TM_MIX, 1), 0)
    lo_rows, hi_rows = HALO, HALO + TM_MIX
    head = slice(0, HALO)

    slack = jnp.zeros((PAD_ROWS - TM_MIX - 2 * HALO, POOL_GROUP), F32)
    for gi, w in enumerate(POOL_WINDOWS):
        project(gi)
        cs = slice(gi * POOL_GROUP, (gi + 1) * POOL_GROUP)
        cur = proj_ref[cur_slot, :, cs]
        f = jnp.concatenate([tail_ref[:, cs] * prev_keep, cur, proj_ref[next_slot, head, cs] * next_keep, slack],
                            axis=0)
        n_rows, span = PAD_ROWS, 1
        while 2 * span < w:
            n_rows -= 8
            f = f[0:n_rows] + f[span:span + n_rows]
            span *= 2
        s = f[lo_rows - span:hi_rows - span] + f[lo_rows:hi_rows]
        lo = jnp.maximum(pos - w // 2, 0)
        hi = jnp.minimum(pos + (w - w // 2), seq_len)
        inv = 1.0 / (hi - lo).astype(F32)
        pooled = s * inv - cur
        mixed = _dot(pooled.astype(BF16), pw_ref[gi]) * ps_ref[:, cs]
        y_ref[:, cs] = mixed.astype(BF16)

    for ci in range(HALF // 128):
        project(len(POOL_WINDOWS) + ci)
        cs = slice(ci * 128, (ci + 1) * 128)
        bs = slice(HALF + ci * 128, HALF + (ci + 1) * 128)
        gs = slice(2 * HALF + ci * 128, 2 * HALF + (ci + 1) * 128)
        vs = slice(3 * HALF + ci * 128, 3 * HALF + (ci + 1) * 128)
        z = jnp.concatenate([(tail_ref[:, gs] * tail_ref[:, vs]) * prev_keep,
                             proj_ref[cur_slot, :, gs] * proj_ref[cur_slot, :, vs],
                             (proj_ref[next_slot, head, gs] * proj_ref[next_slot, head, vs]) * next_keep], axis=0)
        conv = (cw_ref[0:1, cs] * z[lo_rows - 1:hi_rows - 1]
                + cw_ref[1:2, cs] * z[lo_rows:hi_rows]
                + cw_ref[2:3, cs] * z[lo_rows + 1:hi_rows + 1])
        y_ref[:, bs] = (proj_ref[cur_slot, :, bs] * conv).astype(BF16)

    tail_ref[...] = proj_ref[cur_slot, TM_MIX - HALO:TM_MIX, :]


def _even_body(n_x, *refs):
    x_refs, rest = refs[:n_x], refs[n_x:]
    m_ref, g_ref, w_ref = rest[:3]
    proj_ref, tail_ref, h_ref, wbf_ref = rest[-4:]
    i = pl.program_id(0)

    @pl.when(i == 0)
    def _():
        proj_ref[...] = jnp.zeros_like(proj_ref)
        tail_ref[...] = jnp.zeros_like(tail_ref)
        wbf_ref[...] = w_ref[...].astype(BF16)

    for phase in range(RING):
        @pl.when(i % RING == phase)
        def _(phase=phase):
            _even_step(phase, x_refs, m_ref, g_ref, wbf_ref, *rest[3:-1])


def _even_layer_mix(xs, mods, layer, g, w_stack, w_idx, pool_w, pool_scale, conv_w):
    tm = TM_MIX
    n_tiles = N_TOK // tm
    npt = N_PROMPT_TOK // tm
    last = n_tiles - 1
    if len(xs) == 1:
        x_specs = [pl.BlockSpec((tm, D_MODEL), lambda i: (jnp.minimum(i, last), 0))]
    else:
        x_specs = [pl.BlockSpec((tm, D_MODEL), lambda i: (jnp.minimum(i, npt - 1), 0)),
                   pl.BlockSpec((tm, D_MODEL), lambda i: (jnp.clip(i - npt, 0, last - npt), 0))]
    return pl.pallas_call(
        functools.partial(_even_body, len(xs)),
        grid=(n_tiles + LAG,),
        in_specs=x_specs + [
            pl.BlockSpec((None, None, 6, D_MODEL), lambda i: (layer, _cond_row(jnp.minimum(i, last), tm), 0, 0)),
            pl.BlockSpec((1, D_MODEL), lambda i: (0, 0)),
            _resident_layer((D_MODEL, 4 * HALF), w_idx),
            pl.BlockSpec((len(POOL_WINDOWS), POOL_GROUP, POOL_GROUP), lambda i: (0, 0, 0)),
            pl.BlockSpec((1, HALF), lambda i: (0, 0)),
            pl.BlockSpec((3, HALF), lambda i: (0, 0)),
        ],
        out_specs=[pl.BlockSpec((tm, 2 * HALF), lambda i: (jnp.maximum(i - LAG, 0), 0)),
                   pl.BlockSpec((tm, D_MODEL), lambda i: (jnp.minimum(i, last), 0))],
        out_shape=[jax.ShapeDtypeStruct((N_TOK, 2 * HALF), BF16), jax.ShapeDtypeStruct((N_TOK, D_MODEL), F32)],
        scratch_shapes=[pltpu.VMEM((RING, tm, 4 * HALF), F32), pltpu.VMEM((HALO, 4 * HALF), F32),
                        pltpu.VMEM((tm, D_MODEL), BF16), pltpu.VMEM((D_MODEL, 4 * HALF), BF16)],
        compiler_params=_cparams(("arbitrary",)),
        name="even_layer_mix",
    )(*xs, mods, g.reshape(1, D_MODEL), w_stack, pool_w.astype(BF16), pool_scale.reshape(1, HALF), conv_w)


TM_MLP = 1024
RC_MLP = 256
N_CAST = 16


def _mlp_tile(final, x_ref, ya_ref, yb_ref, m_ref, g_ref, fg_ref, o_ref, wo_ref, w1_ref, w2_ref):
    m = m_ref[...]
    gain = g_ref[...] * (1.0 + m[4:5])
    for r in range(TM_MLP // RC_MLP):
        rows = slice(r * RC_MLP, (r + 1) * RC_MLP)
        mix = _dot(ya_ref[rows, :], wo_ref[0:HALF, :]) + _dot(yb_ref[rows, :], wo_ref[HALF:2 * HALF, :])
        x1 = x_ref[rows, :] + m[2:3] * mix
        ms = jnp.mean(x1 * x1, axis=-1, keepdims=True)
        h = (x1 * lax.rsqrt(ms + EPS) * gain + m[3:4]).astype(BF16)
        t = _dot(h, w1_ref[...])
        t = jnp.square(jnp.maximum(t, 0.0)).astype(BF16)
        out = x1 + m[5:6] * _dot(t, w2_ref[...])
        if final:
            ms = jnp.mean(out * out, axis=-1, keepdims=True)
            out = out * lax.rsqrt(ms + EPS) * fg_ref[...]
        o_ref[rows, :] = out


def _mlp_resident_body(final, x_ref, ya_ref, yb_ref, wo_ref, m_ref, g_ref, fg_ref, w1_ref, w2_ref, o_ref):
    _mlp_tile(final, x_ref, ya_ref, yb_ref, m_ref, g_ref, fg_ref, o_ref, wo_ref, w1_ref, w2_ref)


def _mlp_cast_body(final, emit, x_ref, ya_ref, yb_ref, wo_f, m_ref, g_ref, fg_ref, w1_f, w2_f, o_ref, *rest):
    wo_ref, w1_ref, w2_ref = rest[-3:]
    emitted = rest[:-3]
    i = pl.program_id(0)

    @pl.when(i < N_CAST)
    def _():
        for k, (f_ref, b_ref) in enumerate(((wo_f, wo_ref), (w1_f, w1_ref), (w2_f, w2_ref))):
            n = f_ref.shape[0]
            slab = f_ref[...].astype(BF16)
            b_ref[pl.ds(pl.multiple_of(i * n, n), n), :] = slab
            if emit:
                emitted[k][...] = slab

    @pl.when(i >= N_CAST)
    def _():
        _mlp_tile(final, x_ref, ya_ref, yb_ref, m_ref, g_ref, fg_ref, o_ref, wo_ref, w1_ref, w2_ref)


def _resident(shape):
    zeros = (0,) * len(shape)
    return pl.BlockSpec(shape, lambda i: zeros, pipeline_mode=pl.Buffered(1))


def _mlp_call(x, x_off, ya, ya_off, ya_col, yb, yb_off, yb_col, n_tiles, tile0, mods, layer, g2, final_g, final,
              f32_weights=None, bf16_weights=None, emit=False):
    tm = TM_MLP
    lead = N_CAST if f32_weights is not None else 0

    def tile(i):
        return jnp.maximum(i - lead, 0)

    stream_specs = [
        pl.BlockSpec((tm, D_MODEL), lambda i: (tile(i) + x_off, 0)),
        pl.BlockSpec((tm, HALF), lambda i: (tile(i) + ya_off, ya_col)),
        pl.BlockSpec((tm, HALF), lambda i: (tile(i) + yb_off, yb_col)),
    ]
    mod_spec = pl.BlockSpec((None, None, 6, D_MODEL), lambda i: (layer, _cond_row(tile(i) + tile0, tm), 0, 0))
    out_spec = pl.BlockSpec((tm, D_MODEL), lambda i: (tile(i), 0))
    out_shape = jax.ShapeDtypeStruct((n_tiles * tm, D_MODEL), F32)
    vecs = (g2.reshape(1, D_MODEL), final_g.reshape(1, D_MODEL))
    if f32_weights is None:
        w_out, w1, w2 = bf16_weights
        return pl.pallas_call(
            functools.partial(_mlp_resident_body, final),
            grid=(n_tiles,),
            in_specs=stream_specs + [_resident((D_MODEL, D_MODEL)), mod_spec, _resident((1, D_MODEL)),
                                     _resident((1, D_MODEL)), _resident((D_MODEL, D_FF)), _resident((D_FF, D_MODEL))],
            out_specs=out_spec,
            out_shape=out_shape,
            compiler_params=_cparams(("parallel",)),
            name="mlp",
        )(x, ya, yb, w_out, mods, *vecs, w1, w2)

    wo_stack, wo_idx, w1_stack, w2_stack = f32_weights

    def slab(i):
        return jnp.minimum(i, N_CAST - 1)

    shapes = ((D_MODEL, D_MODEL), (D_MODEL, D_FF), (D_FF, D_MODEL))
    slab_rows = [rows // N_CAST for rows, _ in shapes]
    f32_specs = [pl.BlockSpec((None, n, cols), lambda i, idx=idx: (idx, slab(i), 0))
                 for n, (_, cols), idx in zip(slab_rows, shapes, (wo_idx, layer, layer))]
    out_specs, out_shapes = [out_spec], [out_shape]
    if emit:
        out_specs += [pl.BlockSpec((n, cols), lambda i: (slab(i), 0)) for n, (_, cols) in zip(slab_rows, shapes)]
        out_shapes += [jax.ShapeDtypeStruct(shp, BF16) for shp in shapes]
    return pl.pallas_call(
        functools.partial(_mlp_cast_body, final, emit),
        grid=(N_CAST + n_tiles,),
        in_specs=stream_specs + [f32_specs[0], mod_spec, _resident((1, D_MODEL)), _resident((1, D_MODEL)),
                                 f32_specs[1], f32_specs[2]],
        out_specs=out_specs,
        out_shape=out_shapes,
        scratch_shapes=[pltpu.VMEM(shp, BF16) for shp in shapes],
        compiler_params=_cparams(("arbitrary",)),
        name="mlp_cast",
    )(x, ya, yb, wo_stack, mods, *vecs, w1_stack, w2_stack)


def _mlp(x, ya, ya_col, yb, yb_col, w_out_stack, w_out_idx, mods, layer, g2, final_g, w1_stack, w2_stack, final):
    tm = TM_MLP
    npt = N_PROMPT_TOK // tm
    f32_weights = (w_out_stack, w_out_idx, w1_stack, w2_stack)
    common = (mods, layer, g2, final_g, final)
    if not final:
        if isinstance(ya, tuple):
            ya = jnp.concatenate(ya, axis=0)
        return tuple(_mlp_call(x, 0, ya, 0, ya_col, yb, 0, yb_col, N_TOK // tm, 0, *common, f32_weights=f32_weights))
    ya_p, ya_s = ((ya[0], 0), (ya[1], 0)) if isinstance(ya, tuple) else ((ya, 0), (ya, npt))
    y_p, *bf16_weights = _mlp_call(x, 0, ya_p[0], ya_p[1], ya_col, yb, 0, yb_col, npt, 0, *common,
                                   f32_weights=f32_weights, emit=True)
    y_s = _mlp_call(x, npt, ya_s[0], ya_s[1], ya_col, yb, npt, yb_col, N_TOK // tm - npt, npt, *common,
                    bf16_weights=bf16_weights)
    return y_p, y_s


def _ctx_attn_body(q_ref, k_ref, v_ref, o_ref, ck_ref, cv_ref):
    lane = lax.broadcasted_iota(jnp.int32, (1, 2 * HEAD_DIM), 1)
    own = [lane < HEAD_DIM, lane >= HEAD_DIM]
    for h in range(N_HEADS):
        sl = slice(h * HEAD_DIM, (h + 1) * HEAD_DIM)
        ck_ref[h] = k_ref[:, sl].astype(F32)
        cv_ref[h] = v_ref[:, sl].astype(F32)
    for hp in range(N_HEADS // 2):
        sl = slice(hp * 2 * HEAD_DIM, (hp + 1) * 2 * HEAD_DIM)
        q = q_ref[:, sl] * jnp.asarray(HEAD_DIM ** -0.5, BF16)
        k = k_ref[:, sl]
        v = v_ref[:, sl]
        q2 = jnp.concatenate([jnp.where(own[0], q, jnp.zeros_like(q)),
                              jnp.where(own[1], q, jnp.zeros_like(q))], axis=0)
        s = _dot_nt(q2, k)
        m = jnp.max(s, axis=-1, keepdims=True)
        p = jnp.exp(s - m).astype(BF16)
        out = None
        for hh in range(2):
            num = _dot(p[hh * SEQ:(hh + 1) * SEQ], jnp.where(own[hh], v, jnp.ones_like(v)))
            den = jnp.where(own[hh], pltpu.roll(num, HEAD_DIM, 1), 1.0)
            o = num / den
            out = o if out is None else jnp.where(own[0], out, o)
        o_ref[:, sl] = out.astype(BF16)


def _ctx_attn(proj):
    return pl.pallas_call(
        _ctx_attn_body,
        grid=(BATCH,),
        in_specs=[
            pl.BlockSpec((SEQ, HALF), lambda b: (b, 0)),
            pl.BlockSpec((SEQ, HALF), lambda b: (b, 1)),
            pl.BlockSpec((SEQ, HALF), lambda b: (b, 2)),
        ],
        out_specs=[
            pl.BlockSpec((SEQ, HALF), lambda b: (b, 0)),
            pl.BlockSpec((None, None, N_HEADS, SEQ, HEAD_DIM), lambda b: (b, 0, 0, 0, 0)),
            pl.BlockSpec((None, None, N_HEADS, SEQ, HEAD_DIM), lambda b: (b, 0, 0, 0, 0)),
        ],
        out_shape=[
            jax.ShapeDtypeStruct((N_PROMPT_TOK, HALF), BF16),
            jax.ShapeDtypeStruct((BATCH, 1, N_HEADS, SEQ, HEAD_DIM), F32),
            jax.ShapeDtypeStruct((BATCH, 1, N_HEADS, SEQ, HEAD_DIM), F32),
        ],
        compiler_params=_cparams(("parallel",)),
        name="ctx_attn",
    )(proj, proj, proj)


NA_MASKED = -math.inf
NA_QROWS = 4
NA_KROWS = NA_QROWS + WIN_H
NA_NQ = NA_QROWS * GRID_W
NA_NK = NA_KROWS * GRID_W
NA_BLOCKS = GRID_ROWS // NA_QROWS
NA_TYPES = 3


def _na_col_table(rpb):
    qc = jnp.arange(GRID_W)[:, None]
    kc = jnp.arange(GRID_W)[None, :]
    c0 = jnp.clip(qc - WIN_W // 2, 0, GRID_W - WIN_W)
    valid = (kc >= c0) & (kc < c0 + WIN_W)
    coff = jnp.clip(kc - qc, -(WIN_W - 1), WIN_W - 1) + (WIN_W - 1)
    onehot = (coff.reshape(1, -1) == jnp.arange(2 * WIN_W - 1)[:, None]).astype(F32)
    t = jnp.dot(rpb.astype(F32).reshape(N_HEADS * (2 * WIN_H - 1), 2 * WIN_W - 1), onehot,
                precision=lax.Precision.HIGHEST)
    t = t.reshape(N_HEADS, 2 * WIN_H - 1, GRID_W, GRID_W)
    return jnp.where(valid[None, None], t, NA_MASKED)


def _na_window_start(rb):
    return min(max(NA_QROWS * rb - WIN_H // 2, 0), GRID_ROWS - NA_KROWS)


def _na_body(q_ref, k_ref, v_ref, ck_ref, cv_ref, tab_ref, o_ref, bias_ref, kc_ref, vc_ref, va_ref):
    @pl.when(pl.program_id(1) == 0)
    def _():
        masked = jnp.full((GRID_W, GRID_W), NA_MASKED, F32)
        for hh in range(2):
            for ty in range(NA_TYPES):
                q0 = (0, NA_QROWS, GRID_ROWS - NA_QROWS)[ty]
                ks = (0, 0, GRID_ROWS - NA_KROWS)[ty]
                for a in range(NA_QROWS):
                    r0 = min(max(q0 + a - WIN_H // 2, 0), GRID_ROWS - WIN_H)
                    for y in range(NA_KROWS):
                        kr = ks + y
                        ok = r0 <= kr < r0 + WIN_H
                        blk = tab_ref[hh, kr - (q0 + a) + WIN_H - 1] if ok else masked
                        bias_ref[ty, hh * NA_NQ + a * GRID_W:hh * NA_NQ + (a + 1) * GRID_W,
                                 y * GRID_W:(y + 1) * GRID_W] = blk

    lane = lax.broadcasted_iota(jnp.int32, (1, 2 * HEAD_DIM), 1)
    own = [lane < HEAD_DIM, lane >= HEAD_DIM]
    kc_ref[...] = jnp.concatenate([ck_ref[0], ck_ref[1]], axis=1).astype(BF16)
    vc = jnp.concatenate([cv_ref[0], cv_ref[1]], axis=1)
    v = v_ref[...]
    for hh in range(2):
        vc_ref[hh] = jnp.where(own[hh], vc, 1.0).astype(BF16)
        va_ref[hh] = jnp.where(own[hh], v, jnp.ones_like(v))

    def block(rb, ty):
        qs = rb * NA_NQ
        ks = _na_window_start(rb) * GRID_W
        q = q_ref[pl.ds(qs, NA_NQ), :] * jnp.asarray(HEAD_DIM ** -0.5, BF16)
        q2 = jnp.concatenate([jnp.where(own[0], q, jnp.zeros_like(q)),
                              jnp.where(own[1], q, jnp.zeros_like(q))], axis=0)
        s_loc = _dot_nt(q2, k_ref[pl.ds(ks, NA_NK), :]) + bias_ref[ty]
        s_ctx = _dot_nt(q2, kc_ref[...])
        m = jnp.maximum(jnp.max(s_loc, axis=-1, keepdims=True), jnp.max(s_ctx, axis=-1, keepdims=True))
        p_loc = jnp.exp(s_loc - m).astype(BF16)
        p_ctx = jnp.exp(s_ctx - m).astype(BF16)
        out = None
        for hh in range(2):
            rows = slice(hh * NA_NQ, (hh + 1) * NA_NQ)
            num = _dot(p_loc[rows], va_ref[hh, pl.ds(ks, NA_NK), :]) + _dot(p_ctx[rows], vc_ref[hh])
            den = jnp.where(own[hh], pltpu.roll(num, HEAD_DIM, 1), 1.0)
            o = num / den
            out = o if out is None else jnp.where(own[0], out, o)
        o_ref[pl.ds(qs, NA_NQ), :] = out.astype(BF16)

    for rb in range(NA_BLOCKS):
        block(rb, 0 if rb == 0 else (2 if rb == NA_BLOCKS - 1 else 1))


def _na_attn(proj, cache_k, cache_v, col_table, layer_idx):
    row_blk0 = N_PROMPT_TOK // DEC_SEQ
    hp = N_HEADS // 2
    return pl.pallas_call(
        _na_body,
        grid=(hp, DEC_BATCH),
        in_specs=[
            pl.BlockSpec((DEC_SEQ, 128), lambda h, b: (row_blk0 + b, h)),
            pl.BlockSpec((DEC_SEQ, 128), lambda h, b: (row_blk0 + b, hp + h)),
            pl.BlockSpec((DEC_SEQ, 128), lambda h, b: (row_blk0 + b, 2 * hp + h)),
            pl.BlockSpec((None, None, 2, PAST_LEN, HEAD_DIM), lambda h, b: (b, layer_idx, h, 0, 0)),
            pl.BlockSpec((None, None, 2, PAST_LEN, HEAD_DIM), lambda h, b: (b, layer_idx, h, 0, 0)),
            pl.BlockSpec((2, 2 * WIN_H - 1, GRID_W, GRID_W), lambda h, b: (h, 0, 0, 0)),
        ],
        out_specs=pl.BlockSpec((DEC_SEQ, 128), lambda h, b: (b, h)),
        out_shape=jax.ShapeDtypeStruct((N_SAMPLE_TOK, HALF), BF16),
        scratch_shapes=[pltpu.VMEM((NA_TYPES, 2 * NA_NQ, NA_NK), F32),
                        pltpu.VMEM((PAST_LEN, 2 * HEAD_DIM), BF16),
                        pltpu.VMEM((2, PAST_LEN, 2 * HEAD_DIM), BF16),
                        pltpu.VMEM((2, DEC_SEQ, 2 * HEAD_DIM), BF16)],
        compiler_params=_cparams(("parallel", "arbitrary")),
        name="na_attn",
    )(proj, proj, proj, cache_k, cache_v, col_table)


N_POW = 24


def _s5_prep_body(vec_ref, bt_ref, c_ref, wend_ref, ktoe_ref, woutT_ref, apow_ref, pw_ref, pct_ref):
    L = S5_CHUNK
    p = pl.program_id(0)
    lane_blk = lax.broadcasted_iota(jnp.int32, (S5_CH, S5_W), 1) // S5_CH
    src_blk = (lane_blk + L - p) % L
    krows = []
    for d in range(2):
        v = vec_ref[d]
        lre, lim, step = v[0:1], v[1:2], v[2:3]
        a = lre * step
        b = lim * step
        ea = jnp.exp(a)
        nr = ea * jnp.cos(b) - 1.0
        ni = ea * jnp.sin(b)
        den = lre * lre + lim * lim
        qr = (nr * lre + ni * lim) / den
        qi = (ni * lre - nr * lim) / den
        bre, bim = bt_ref[d, 0], bt_ref[d, 1]
        bqr = bre * qr - bim * qi
        bqi = bre * qi + bim * qr
        cre, cim = c_ref[d, 0], c_ref[d, 1]

        e = lax.broadcasted_iota(jnp.int32, (N_POW, 1), 0).astype(F32)
        mag = jnp.exp(e * a)
        pw_ref[0] = mag * jnp.cos(e * b)
        pw_ref[1] = mag * jnp.sin(e * b)
        mag = jnp.exp((L * e) * a)
        apow_ref[d, 0] = mag * jnp.cos((L * e) * b)
        apow_ref[d, 1] = mag * jnp.sin((L * e) * b)

        def power(k):
            return pw_ref[0, k:k + 1, :], pw_ref[1, k:k + 1, :]

        def c_block(k):
            pr, pi = power(k)
            return cre * pr - cim * pi, -(cre * pi + cim * pr)

        for i in range(L):
            rows = pl.ds(pl.multiple_of(((i + p) % L) * S5_CH, S5_CH), S5_CH)
            pr, pi = power(L - 1 - i if d == 0 else i)
            wend_ref[rows, d * 256:d * 256 + 128] = (bqr * pr - bqi * pi).astype(BF16)
            wend_ref[rows, d * 256 + 128:d * 256 + 256] = (bqr * pi + bqi * pr).astype(BF16)
            mre, mim = c_block(i + 1 if d == 0 else L - i)
            woutT_ref[rows, d * 256:d * 256 + 128] = mre.astype(BF16)
            woutT_ref[rows, d * 256 + 128:d * 256 + 256] = mim.astype(BF16)
            lag_rows = slice(i * S5_CH, (i + 1) * S5_CH)
            mre, mim = c_block(i if d == 0 else L - 1 - i)
            pct_ref[lag_rows, 0:128] = mre
            pct_ref[lag_rows, 128:256] = mim

        bqcat = jnp.concatenate([bqr, bqi], axis=1)
        krows.append(_dot_nt(bqcat, pct_ref[...], precision=lax.Precision.HIGHEST))

    base_f = pltpu.roll(krows[0], p * S5_CH, 1)
    base_b = pltpu.roll(krows[1], p * S5_CH, 1)
    for i in range(L):
        rows = pl.ds(pl.multiple_of(((i + p) % L) * S5_CH, S5_CH), S5_CH)
        blk_f = base_f if i == 0 else pltpu.roll(base_f, i * S5_CH, 1)
        blk_b = base_b if (i + 1) % L == 0 else pltpu.roll(base_b, ((i + 1) % L) * S5_CH, 1)
        blk = jnp.where(src_blk >= i, blk_f, 0.0) + jnp.where(src_blk <= i, blk_b, 0.0)
        ktoe_ref[rows, :] = blk.astype(BF16)


def _s5_prep(lam_re, lam_im, log_step, b_re, b_im, c_re, c_im):
    def pair_vec(a):
        return a.astype(F32).reshape(2, S5_PAIRS, 1, S5_P).transpose(1, 0, 2, 3)

    step = jnp.broadcast_to(jnp.exp(log_step.astype(F32))[:, :, None], (2, N_SSM_GROUPS, SSM_STATE))
    vec = jnp.concatenate([pair_vec(lam_re), pair_vec(lam_im), pair_vec(step),
                           jnp.zeros((S5_PAIRS, 2, 5, S5_P), F32)], axis=2)

    def block_diag_ch_by_state(m):
        m = m.astype(F32).reshape(2, S5_PAIRS, 2, SSM_GROUP, SSM_STATE)
        eye = jnp.eye(2, dtype=F32)
        out = m[:, :, :, :, None, :] * eye[None, None, :, None, :, None]
        return out.reshape(2, S5_PAIRS, S5_CH, S5_P).transpose(1, 0, 2, 3)

    bt = jnp.stack([block_diag_ch_by_state(b_re.transpose(0, 1, 3, 2)),
                    block_diag_ch_by_state(b_im.transpose(0, 1, 3, 2))], axis=2)
    cc = jnp.stack([block_diag_ch_by_state(c_re), block_diag_ch_by_state(c_im)], axis=2)

    mat = jax.ShapeDtypeStruct((S5_PAIRS, S5_W, S5_W), BF16)
    mat_spec = pl.BlockSpec((None, S5_W, S5_W), lambda p: (p, 0, 0))
    return pl.pallas_call(
        _s5_prep_body,
        grid=(S5_PAIRS,),
        in_specs=[
            pl.BlockSpec((None, 2, 8, S5_P), lambda p: (p, 0, 0, 0)),
            pl.BlockSpec((None, 2, 2, S5_CH, S5_P), lambda p: (p, 0, 0, 0, 0)),
            pl.BlockSpec((None, 2, 2, S5_CH, S5_P), lambda p: (p, 0, 0, 0, 0)),
        ],
        out_specs=[mat_spec, mat_spec, mat_spec,
                   pl.BlockSpec((None, 2, 2, N_POW, S5_P), lambda p: (p, 0, 0, 0, 0))],
        out_shape=[mat, mat, mat, jax.ShapeDtypeStruct((S5_PAIRS, 2, 2, N_POW, S5_P), F32)],
        scratch_shapes=[pltpu.VMEM((2, N_POW, S5_P), F32), pltpu.VMEM((S5_W, 2 * S5_P), F32)],
        compiler_params=_cparams(("parallel",)),
        name="s5_prep",
    )(vec, bt, cc)


S5_SEG_ROWS = 256
S5_N_SEG = N_CHUNK_ROWS // S5_SEG_ROWS
S5_SUB = S5_SEG_ROWS // S5_CHUNK
S5_SUB_PER_SEQ = (DEC_SEQ // S5_CHUNK) // S5_CHUNK


S5_PPS = 2


def _s5_body(tu_ref, s0_ref, wend_ref, ktoe_ref, woutT_ref, apow_ref, ty_ref, fin_ref, z_ref, xs_ref, g_ref):
    seg = pl.program_id(0)
    L = S5_CHUNK
    lane_blk = lax.broadcasted_iota(jnp.int32, (S5_SEG_ROWS, 128), 1) // S5_CH
    pairs = [S5_PPS * pl.program_id(1) + n for n in range(S5_PPS)]

    s_rows, ends = [], []
    for n, p in enumerate(pairs):
        tiles = []
        for t in range(LANE_TILES):
            tile = None
            for k in range(128 // S5_CH):
                j = (LANE_TILES * t + k + L - p) % L
                blk = tu_ref[j, t]
                tile = blk if tile is None else jnp.where(lane_blk == k, blk, tile)
            tiles.append(tile)
        s = jnp.concatenate(tiles, axis=1)
        s_rows.append(s)
        z = _dot(s, wend_ref[n])
        for q in range(LANE_TILES):
            z_ref[n, q] = z[:, q * 128:(q + 1) * 128]

        pair_ends = []
        for d in range(2):
            are, aim = apow_ref[n, d, 0, 1:2, :], apow_ref[n, d, 1, 1:2, :]
            xr = jnp.zeros((S5_SUB, S5_P), F32)
            xi = jnp.zeros((S5_SUB, S5_P), F32)
            for c in (range(L) if d == 0 else reversed(range(L))):
                rows = pl.ds(c, S5_SUB, stride=L)
                xs_ref[n, 2 * d, rows, :] = xr
                xs_ref[n, 2 * d + 1, rows, :] = xi
                zr = z_ref[n, 2 * d, rows, :]
                zi = z_ref[n, 2 * d + 1, rows, :]
                xr, xi = are * xr - aim * xi + zr, are * xi + aim * xr + zi
            pair_ends.append((xr, xi))
            fin_ref[n, d, 0] = xr
            fin_ref[n, d, 1] = xi
        ends.append(pair_ends)

    @pl.when(seg > 0)
    def _():
        for n in range(S5_PPS):
            for d in range(2):
                er, ei = ends[n][d]
                a16r, a16i = apow_ref[n, d, 0, L:L + 1, :], apow_ref[n, d, 1, L:L + 1, :]
                for bb in range(S5_SUB // S5_SUB_PER_SEQ):
                    gr = s0_ref[n, bb, 2 * d:2 * d + 1, :]
                    gi = s0_ref[n, bb, 2 * d + 1:2 * d + 2, :]
                    order = range(S5_SUB_PER_SEQ) if d == 0 else reversed(range(S5_SUB_PER_SEQ))
                    prev = None
                    for sub in order:
                        row = bb * S5_SUB_PER_SEQ + sub
                        if prev is not None:
                            gr, gi = (a16r * gr - a16i * gi + er[prev:prev + 1, :],
                                      a16r * gi + a16i * gr + ei[prev:prev + 1, :])
                        g_ref[n, 0, row:row + 1, :] = gr
                        g_ref[n, 1, row:row + 1, :] = gi
                        prev = row
                if d == 0:
                    pr, pi = apow_ref[n, d, 0, 0:L, :], apow_ref[n, d, 1, 0:L, :]
                else:
                    pr = jnp.concatenate([apow_ref[n, d, 0, k:k + 1, :] for k in reversed(range(L))], axis=0)
                    pi = jnp.concatenate([apow_ref[n, d, 1, k:k + 1, :] for k in reversed(range(L))], axis=0)
                for sub in range(S5_SUB):
                    rows = slice(sub * L, (sub + 1) * L)
                    gr = g_ref[n, 0, sub:sub + 1, :]
                    gi = g_ref[n, 1, sub:sub + 1, :]
                    xs_ref[n, 2 * d, rows, :] = xs_ref[n, 2 * d, rows, :] + (pr * gr - pi * gi)
                    xs_ref[n, 2 * d + 1, rows, :] = xs_ref[n, 2 * d + 1, rows, :] + (pr * gi + pi * gr)

    @pl.when(pl.program_id(1) == 0)
    def _():
        ty_ref[...] = jnp.zeros_like(ty_ref)

    for n, p in enumerate(pairs):
        xs = jnp.concatenate([xs_ref[n, q].astype(BF16) for q in range(LANE_TILES)], axis=1)
        y = _dot(s_rows[n], ktoe_ref[n]) + _dot_nt(xs, woutT_ref[n])
        for t in range(LANE_TILES):
            yt = y[:, t * 128:(t + 1) * 128]
            for k in range(128 // S5_CH):
                j = (LANE_TILES * t + k + L - p) % L
                ty_ref[j, t] = jnp.where(lane_blk == k, yt, ty_ref[j, t])


def _s5_scan(tu, s0, wend, ktoe, woutT, apow):
    mat_spec = pl.BlockSpec((S5_PPS, S5_W, S5_W), lambda g, p: (p, 0, 0))
    seqs = S5_SUB // S5_SUB_PER_SEQ
    return pl.pallas_call(
        _s5_body,
        grid=(S5_N_SEG, S5_PAIRS // S5_PPS),
        in_specs=[
            pl.BlockSpec((S5_CHUNK, LANE_TILES, S5_SEG_ROWS, 128), lambda g, p: (0, 0, g, 0)),
            pl.BlockSpec((S5_PPS, seqs, 4, S5_P), lambda g, p: (p, jnp.maximum(g - 1, 0), 0, 0)),
            mat_spec, mat_spec, mat_spec,
            pl.BlockSpec((S5_PPS, 2, 2, N_POW, S5_P), lambda g, p: (p, 0, 0, 0, 0)),
        ],
        out_specs=[
            pl.BlockSpec((S5_CHUNK, LANE_TILES, S5_SEG_ROWS, 128), lambda g, p: (0, 0, g, 0)),
            pl.BlockSpec((S5_PPS, 2, 2, S5_SUB, S5_P), lambda g, p: (p, 0, 0, g, 0)),
        ],
        out_shape=[
            jax.ShapeDtypeStruct((S5_CHUNK, LANE_TILES, N_CHUNK_ROWS, 128), F32),
            jax.ShapeDtypeStruct((S5_PAIRS, 2, 2, S5_N_SEG * S5_SUB, S5_P), F32),
        ],
        scratch_shapes=[pltpu.VMEM((S5_PPS, LANE_TILES, S5_SEG_ROWS, 128), F32),
                        pltpu.VMEM((S5_PPS, LANE_TILES, S5_SEG_ROWS, 128), F32),
                        pltpu.VMEM((S5_PPS, 2, S5_SUB, S5_P), F32)],
        compiler_params=_cparams(("parallel", "arbitrary")),
        name="s5_scan",
    )(tu, s0, wend, ktoe, woutT, apow)


TM_GLU = 2048


def _glu_body(tu_ref, ty_ref, d_ref, gw_ref, gb_ref, o_ref, out_ref):
    rows = TM_GLU // S5_CHUNK
    d = d_ref[...]
    pre = []
    for j in range(S5_CHUNK):
        uj = jnp.concatenate([tu_ref[j, t].astype(F32) for t in range(LANE_TILES)], axis=1)
        yj = jnp.concatenate([ty_ref[j, t] for t in range(LANE_TILES)], axis=1)
        dj = pltpu.roll(d, j * S5_CH, 1) if j else d
        yj = dj * uj + yj
        back = ((S5_CHUNK - j) % S5_CHUNK) * S5_CH
        pre.append(pltpu.roll(yj, back, 1) if back else yj)
    y = jnp.concatenate(pre, axis=0)
    cdf = 0.5 * (1.0 + jnp.tanh(math.sqrt(2.0 / math.pi) * (y + 0.044715 * (y * y * y))))
    y = y * cdf
    gate = jax.nn.sigmoid(_dot(y.astype(BF16), gw_ref[...]) + gb_ref[...])
    out = y * gate
    for j in range(S5_CHUNK):
        oj = out[j * rows:(j + 1) * rows]
        for t in range(LANE_TILES):
            out_ref[t, pl.ds(j, rows, stride=S5_CHUNK), :] = oj[:, t * 128:(t + 1) * 128]
    o_ref[...] = jnp.concatenate([out_ref[t] for t in range(LANE_TILES)], axis=1).astype(BF16)


def _glu(tu, ty, ssm_d, glu_w, glu_b):
    tm = TM_GLU
    rows = tm // S5_CHUNK
    lay_spec = pl.BlockSpec((S5_CHUNK, LANE_TILES, rows, 128), lambda i: (0, 0, i, 0))
    return pl.pallas_call(
        _glu_body,
        grid=(N_TOK // tm,),
        in_specs=[
            lay_spec, lay_spec,
            pl.BlockSpec((1, HALF), lambda i: (0, 0)),
            pl.BlockSpec((HALF, HALF), lambda i: (0, 0)),
            pl.BlockSpec((1, HALF), lambda i: (0, 0)),
        ],
        out_specs=pl.BlockSpec((tm, HALF), lambda i: (i, 0)),
        out_shape=jax.ShapeDtypeStruct((N_TOK, HALF), BF16),
        scratch_shapes=[pltpu.VMEM((LANE_TILES, tm, 128), F32)],
        compiler_params=_cparams(("parallel",)),
        name="s5_glu",
    )(tu, ty, ssm_d.reshape(1, HALF), glu_w.astype(BF16), glu_b.reshape(1, HALF))


def kernel(x_prompt, x_sample, c, cache_k, cache_v, state_s5_re, state_s5_im, c_ctx, norm1_g, norm2_g, ada_w, ada_b, mlp_w1, mlp_w2, ab_w_in, pool_w, pool_scale, conv_w, ab_w_out, cd_w_in, na_rpb, ssm_lambda_re, ssm_lambda_im, ssm_log_step, ssm_b_re, ssm_b_im, ssm_c_re, ssm_c_im, ssm_d, glu_w, glu_b, cd_w_out, final_g):
    depth = ada_w.shape[0]
    xs = (x_prompt.reshape(N_PROMPT_TOK, D_MODEL), x_sample.reshape(N_SAMPLE_TOK, D_MODEL))
    cond = jnp.concatenate([c_ctx[None, :], c, jnp.zeros((N_COND - 1 - DEC_BATCH, D_MODEL), F32)], axis=0)
    mods = _adaln(cond, ada_w, ada_b).reshape(depth, N_COND, 6, D_MODEL)

    new_k, new_v, new_re, new_im = [], [], [], []
    for layer in range(depth):
        i = layer // 2
        final = layer == depth - 1
        if layer % 2 == 0:
            y, x = _even_layer_mix(xs, mods, layer, norm1_g[layer], ab_w_in, i,
                                   pool_w[i], pool_scale[i], conv_w[i])
            ya, ya_col, yb, yb_col = y, 0, y, 1
            w_out_stack = ab_w_out
        else:
            x = xs[0]
            qkv, tu = _inproj_odd(x, mods, layer, norm1_g[layer], cd_w_in, i)
            o_p, ck, cv = _ctx_attn(qkv)
            new_k.append(ck)
            new_v.append(cv)
            o_s = _na_attn(qkv, cache_k, cache_v, _na_col_table(na_rpb[i]), i)
            attn = (o_p, o_s)

            wend, ktoe, woutT, apow = _s5_prep(ssm_lambda_re[i], ssm_lambda_im[i], ssm_log_step[i],
                                               ssm_b_re[i], ssm_b_im[i], ssm_c_re[i], ssm_c_im[i])

            def pair_state(s):
                return s.astype(F32).reshape(DEC_BATCH, 2, S5_PAIRS, S5_P).transpose(2, 0, 1, 3)

            s0 = jnp.stack([pair_state(state_s5_re[:, i]), pair_state(state_s5_im[:, i])], axis=3)
            s0 = s0.reshape(S5_PAIRS, DEC_BATCH, 4, S5_P)
            ty, fin = _s5_scan(tu, s0, wend, ktoe, woutT, apow)
            fin = fin[:, :, :, :BATCH]
            new_re.append(fin[:, :, 0].transpose(2, 1, 0, 3).reshape(BATCH, 2, N_SSM_GROUPS, SSM_STATE))
            new_im.append(fin[:, :, 1].transpose(2, 1, 0, 3).reshape(BATCH, 2, N_SSM_GROUPS, SSM_STATE))
            d_out = _glu(tu, ty, ssm_d[i], glu_w[i], glu_b[i])
            ya, ya_col, yb, yb_col = attn, 0, d_out, 0
            w_out_stack = cd_w_out
        xs = _mlp(x, ya, ya_col, yb, yb_col, w_out_stack, i, mods, layer, norm2_g[layer], final_g,
                  mlp_w1, mlp_w2, final)

    y_prompt = xs[0].reshape(BATCH, SEQ, D_MODEL)
    y_sample = xs[1].reshape(DEC_BATCH, DEC_SEQ, D_MODEL)
    return (y_prompt, y_sample, jnp.concatenate(new_k, axis=1), jnp.concatenate(new_v, axis=1),
            jnp.stack(new_re, axis=1), jnp.stack(new_im, axis=1))
```

```python
import functools
import math

import jax
import jax.numpy as jnp
from jax import lax
from jax.experimental import pallas as pl
from jax.experimental.pallas import tpu as pltpu

F32 = jnp.float32
BF16 = jnp.bfloat16

D_MODEL = 1024
BATCH = 16
SEQ = 256
DEC_BATCH = 4
DEC_SEQ = 2048
PAST_LEN = 256
GRID_W = 64
GRID_ROWS = DEC_SEQ // GRID_W
HALF = 512
POOL_WINDOWS = (2, 4, 8, 16)
POOL_GROUP = 128
HEAD_DIM = 64
N_HEADS = 8
WIN_H = 8
WIN_W = 16
SSM_GROUP = 16
N_SSM_GROUPS = 32
SSM_STATE = 64
D_FF = 4096
EPS = 1e-6

N_PROMPT_TOK = BATCH * SEQ
N_SAMPLE_TOK = DEC_BATCH * DEC_SEQ
N_TOK = N_PROMPT_TOK + N_SAMPLE_TOK
N_COND = 8

S5_CHUNK = 16
S5_PAIRS = N_SSM_GROUPS // 2
S5_CH = 2 * SSM_GROUP
S5_P = 2 * SSM_STATE
S5_W = S5_CHUNK * S5_CH

VMEM_LIMIT = 56 * 1024 * 1024


def _cparams(sem):
    return pltpu.CompilerParams(dimension_semantics=sem, vmem_limit_bytes=VMEM_LIMIT)


def _cond_row(i, tm):
    npt = N_PROMPT_TOK // tm
    per = DEC_SEQ // tm
    return jnp.where(i < npt, 0, 1 + (i - npt) // per)


def _resident_layer(shape, idx):
    zeros = (0,) * len(shape)
    return pl.BlockSpec((None,) + tuple(shape), lambda i: (idx,) + zeros, pipeline_mode=pl.Buffered(1))


def _normmod(x, g, shift, scale):
    ms = jnp.mean(x * x, axis=-1, keepdims=True)
    return (x * lax.rsqrt(ms + EPS) * g) * (1.0 + scale) + shift


def _dot(a, b):
    return jnp.dot(a, b, preferred_element_type=F32)


def _dot_nt(a, b, precision=None):
    return lax.dot_general(a, b, (((1,), (1,)), ((), ())), preferred_element_type=F32, precision=precision)


def _adaln_body(c_ref, w_ref, b_ref, o_ref):
    c = c_ref[...]
    s = c * jax.nn.sigmoid(c)
    o_ref[...] = _dot(s.astype(BF16), w_ref[...].astype(BF16)) + b_ref[...]


def _adaln(cond, ada_w, ada_b):
    depth = ada_w.shape[0]
    nj = 6
    return pl.pallas_call(
        _adaln_body,
        grid=(depth, nj),
        in_specs=[
            pl.BlockSpec((N_COND, D_MODEL), lambda l, j: (0, 0)),
            pl.BlockSpec((None, D_MODEL, D_MODEL), lambda l, j: (l, 0, j)),
            pl.BlockSpec((None, 1, D_MODEL), lambda l, j: (l, 0, j)),
        ],
        out_specs=pl.BlockSpec((None, N_COND, D_MODEL), lambda l, j: (l, 0, j)),
        out_shape=jax.ShapeDtypeStruct((depth, N_COND, 6 * D_MODEL), F32),
        compiler_params=_cparams(("arbitrary", "arbitrary")),
        name="adaln",
    )(cond, ada_w, ada_b.reshape(depth, 1, 6 * D_MODEL))


TM_PROJ = 1024


def _stream_tile(x_refs, tm):
    if len(x_refs) == 1:
        return x_refs[0][...]
    is_prompt = pl.program_id(0) < N_PROMPT_TOK // tm
    return jnp.where(is_prompt, x_refs[0][...], x_refs[1][...])


N_CHUNK_ROWS = N_TOK // S5_CHUNK
LANE_TILES = S5_W // 128


def _inproj_odd_body(x_ref, m_ref, g_ref, w_ref, qkv_ref, tu_ref, u_ref, wbf_ref):
    @pl.when(pl.program_id(0) == 0)
    def _():
        wbf_ref[...] = w_ref[...].astype(BF16)

    m = m_ref[...]
    h = _normmod(x_ref[...], g_ref[...], m[0:1], m[1:2])
    acc = _dot(h.astype(BF16), wbf_ref[...])
    qkv_ref[...] = acc[:, 0:3 * HALF].astype(BF16)
    for t in range(LANE_TILES):
        u_ref[t] = acc[:, 3 * HALF + t * 128:3 * HALF + (t + 1) * 128]
    rows = TM_PROJ // S5_CHUNK
    for j in range(S5_CHUNK):
        uj = jnp.concatenate([u_ref[t, pl.ds(j, rows, stride=S5_CHUNK), :] for t in range(LANE_TILES)], axis=1)
        if j:
            uj = pltpu.roll(uj, j * S5_CH, 1)
        for t in range(LANE_TILES):
            tu_ref[j, t] = uj[:, t * 128:(t + 1) * 128].astype(BF16)


def _inproj_odd(x, mods, layer, g, w_stack, w_idx):
    tm = TM_PROJ
    rows = tm // S5_CHUNK
    return pl.pallas_call(
        _inproj_odd_body,
        grid=(N_TOK // tm,),
        in_specs=[
            pl.BlockSpec((tm, D_MODEL), lambda i: (i, 0)),
            pl.BlockSpec((None, None, 6, D_MODEL), lambda i: (layer, _cond_row(i, tm), 0, 0)),
            pl.BlockSpec((1, D_MODEL), lambda i: (0, 0)),
            _resident_layer((D_MODEL, 4 * HALF), w_idx),
        ],
        out_specs=[
            pl.BlockSpec((tm, 3 * HALF), lambda i: (i, 0)),
            pl.BlockSpec((S5_CHUNK, LANE_TILES, rows, 128), lambda i: (0, 0, i, 0)),
        ],
        out_shape=[
            jax.ShapeDtypeStruct((N_TOK, 3 * HALF), BF16),
            jax.ShapeDtypeStruct((S5_CHUNK, LANE_TILES, N_CHUNK_ROWS, 128), BF16),
        ],
        scratch_shapes=[pltpu.VMEM((LANE_TILES, tm, 128), F32), pltpu.VMEM((D_MODEL, 4 * HALF), BF16)],
        compiler_params=_cparams(("arbitrary",)),
        name="inproj_odd",
    )(x, mods, g.reshape(1, D_MODEL), w_stack)


TM_MIX = 256
HALO = 16
RING = 3
LAG = 2
PAD_ROWS = TM_MIX + 2 * HALO + 8


def _even_step(phase, x_refs, m_ref, g_ref, w_ref, pw_ref, ps_ref, cw_ref, y_ref, xcat_ref, proj_ref, tail_ref,
               h_ref):
    i = pl.program_id(0)
    write_slot, cur_slot, next_slot = phase, (phase + 1) % RING, (phase + 2) % RING

    m = m_ref[...]
    x = _stream_tile(x_refs, TM_MIX)
    xcat_ref[...] = x
    h_ref[...] = _normmod(x, g_ref[...], m[0:1], m[1:2]).astype(BF16)
    n_pieces = len(POOL_WINDOWS) + HALF // 128
    pw = (4 * HALF) // n_pieces

    def project(piece):
        cols = slice(piece * pw, (piece + 1) * pw)
        proj_ref[write_slot, :, cols] = _dot(h_ref[...], w_ref[:, cols])

    t = jnp.maximum(i - LAG, 0)
    n_prompt_tiles = N_PROMPT_TOK // TM_MIX
    tiles_per_seq = DEC_SEQ // TM_MIX
    is_prompt = t < n_prompt_tiles
    tile_in_seq = jnp.where(is_prompt, 0, (t - n_prompt_tiles) % tiles_per_seq)
    seq_len = jnp.where(is_prompt, SEQ, DEC_SEQ)
    has_prev = tile_in_seq > 0
    has_next = jnp.logical_and(jnp.logical_not(is_prompt), tile_in_seq < tiles_per_seq - 1)
    prev_keep = jnp.where(has_prev, 1.0, 0.0).astype(F32)
    next_keep = jnp.where(has_next, 1.0, 0.0).astype(F32)
    pos = tile_in_seq * TM_MIX + lax.broadcasted_iota(jnp.int32, (TM_MIX, 1), 0)
    lo_rows, hi_rows = HALO, HALO + TM_MIX
    head = slice(0, HALO)

    slack = jnp.zeros((PAD_ROWS - TM_MIX - 2 * HALO, POOL_GROUP), F32)
    for gi, w in enumerate(POOL_WINDOWS):
        project(gi)
        cs = slice(gi * POOL_GROUP, (gi + 1) * POOL_GROUP)
        cur = proj_ref[cur_slot, :, cs]
        f = jnp.concatenate([tail_ref[:, cs] * prev_keep, cur, proj_ref[next_slot, head, cs] * next_keep, slack],
                            axis=0)
        n_rows, span = PAD_ROWS, 1
        while 2 * span < w:
            n_rows -= 8
            f = f[0:n_rows] + f[span:span + n_rows]
            span *= 2
        s = f[lo_rows - span:hi_rows - span] + f[lo_rows:hi_rows]
        lo = jnp.maximum(pos - w // 2, 0)
        hi = jnp.minimum(pos + (w - w // 2), seq_len)
        inv = 1.0 / (hi - lo).astype(F32)
        pooled = s * inv - cur
        mixed = _dot(pooled.astype(BF16), pw_ref[gi]) * ps_ref[:, cs]
        y_ref[:, cs] = mixed.astype(BF16)

    for ci in range(HALF // 128):
        project(len(POOL_WINDOWS) + ci)
        cs = slice(ci * 128, (ci + 1) * 128)
        bs = slice(HALF + ci * 128, HALF + (ci + 1) * 128)
        gs = slice(2 * HALF + ci * 128, 2 * HALF + (ci + 1) * 128)
        vs = slice(3 * HALF + ci * 128, 3 * HALF + (ci + 1) * 128)
        z = jnp.concatenate([(tail_ref[:, gs] * tail_ref[:, vs]) * prev_keep,
                             proj_ref[cur_slot, :, gs] * proj_ref[cur_slot, :, vs],
                             (proj_ref[next_slot, head, gs] * proj_ref[next_slot, head, vs]) * next_keep], axis=0)
        conv = (cw_ref[0:1, cs] * z[lo_rows - 1:hi_rows - 1]
                + cw_ref[1:2, cs] * z[lo_rows:hi_rows]
                + cw_ref[2:3, cs] * z[lo_rows + 1:hi_rows + 1])
        y_ref[:, bs] = (proj_ref[cur_slot, :, bs] * conv).astype(BF16)

    tail_ref[...] = proj_ref[cur_slot, TM_MIX - HALO:TM_MIX, :]


def _even_body(n_x, *refs):
    x_refs, rest = refs[:n_x], refs[n_x:]
    m_ref, g_ref, w_ref = rest[:3]
    proj_ref, tail_ref, h_ref, wbf_ref = rest[-4:]
    i = pl.program_id(0)

    @pl.when(i == 0)
    def _():
        proj_ref[...] = jnp.zeros_like(proj_ref)
        tail_ref[...] = jnp.zeros_like(tail_ref)
        wbf_ref[...] = w_ref[...].astype(BF16)

    for phase in range(RING):
        @pl.when(i % RING == phase)
        def _(phase=phase):
            _even_step(phase, x_refs, m_ref, g_ref, wbf_ref, *rest[3:-1])


def _even_layer_mix(xs, mods, layer, g, w_stack, w_idx, pool_w, pool_scale, conv_w):
    tm = TM_MIX
    n_tiles = N_TOK // tm
    npt = N_PROMPT_TOK // tm
    last = n_tiles - 1
    if len(xs) == 1:
        x_specs = [pl.BlockSpec((tm, D_MODEL), lambda i: (jnp.minimum(i, last), 0))]
    else:
        x_specs = [pl.BlockSpec((tm, D_MODEL), lambda i: (jnp.minimum(i, npt - 1), 0)),
                   pl.BlockSpec((tm, D_MODEL), lambda i: (jnp.clip(i - npt, 0, last - npt), 0))]
    return pl.pallas_call(
        functools.partial(_even_body, len(xs)),
        grid=(n_tiles + LAG,),
        in_specs=x_specs + [
            pl.BlockSpec((None, None, 6, D_MODEL), lambda i: (layer, _cond_row(jnp.minimum(i, last), tm), 0, 0)),
            pl.BlockSpec((1, D_MODEL), lambda i: (0, 0)),
            _resident_layer((D_MODEL, 4 * HALF), w_idx),
            pl.BlockSpec((len(POOL_WINDOWS), POOL_GROUP, POOL_GROUP), lambda i: (0, 0, 0)),
            pl.BlockSpec((1, HALF), lambda i: (0, 0)),
            pl.BlockSpec((3, HALF), lambda i: (0, 0)),
        ],
        out_specs=[pl.BlockSpec((tm, 2 * HALF), lambda i: (jnp.maximum(i - LAG, 0), 0)),
                   pl.BlockSpec((tm, D_MODEL), lambda i: (jnp.minimum(i, last), 0))],
        out_shape=[jax.ShapeDtypeStruct((N_TOK, 2 * HALF), BF16), jax.ShapeDtypeStruct((N_TOK, D_MODEL), F32)],
        scratch_shapes=[pltpu.VMEM((RING, tm, 4 * HALF), F32), pltpu.VMEM((HALO, 4 * HALF), F32),
                        pltpu.VMEM((tm, D_MODEL), BF16), pltpu.VMEM((D_MODEL, 4 * HALF), BF16)],
        compiler_params=_cparams(("arbitrary",)),
        name="even_layer_mix",
    )(*xs, mods, g.reshape(1, D_MODEL), w_stack, pool_w.astype(BF16), pool_scale.reshape(1, HALF), conv_w)


TM_MLP = 1024
RC_MLP = 256
N_CAST = 16


def _mlp_tile(final, x_ref, ya_ref, yb_ref, m_ref, g_ref, fg_ref, o_ref, wo_ref, w1_ref, w2_ref):
    m = m_ref[...]
    gain = g_ref[...] * (1.0 + m[4:5])
    for r in range(TM_MLP // RC_MLP):
        rows = slice(r * RC_MLP, (r + 1) * RC_MLP)
        mix = _dot(ya_ref[rows, :], wo_ref[0:HALF, :]) + _dot(yb_ref[rows, :], wo_ref[HALF:2 * HALF, :])
        x1 = x_ref[rows, :] + m[2:3] * mix
        ms = jnp.mean(x1 * x1, axis=-1, keepdims=True)
        h = (x1 * lax.rsqrt(ms + EPS) * gain + m[3:4]).astype(BF16)
        t = _dot(h, w1_ref[...])
        t = jnp.square(jnp.maximum(t, 0.0)).astype(BF16)
        out = x1 + m[5:6] * _dot(t, w2_ref[...])
        if final:
            ms = jnp.mean(out * out, axis=-1, keepdims=True)
            out = out * lax.rsqrt(ms + EPS) * fg_ref[...]
        o_ref[rows, :] = out


def _mlp_resident_body(final, x_ref, ya_ref, yb_ref, wo_ref, m_ref, g_ref, fg_ref, w1_ref, w2_ref, o_ref):
    _mlp_tile(final, x_ref, ya_ref, yb_ref, m_ref, g_ref, fg_ref, o_ref, wo_ref, w1_ref, w2_ref)


def _mlp_cast_body(final, emit, x_ref, ya_ref, yb_ref, wo_f, m_ref, g_ref, fg_ref, w1_f, w2_f, o_ref, *rest):
    wo_ref, w1_ref, w2_ref = rest[-3:]
    emitted = rest[:-3]
    i = pl.program_id(0)

    @pl.when(i < N_CAST)
    def _():
        for k, (f_ref, b_ref) in enumerate(((wo_f, wo_ref), (w1_f, w1_ref), (w2_f, w2_ref))):
            n = f_ref.shape[0]
            slab = f_ref[...].astype(BF16)
            b_ref[pl.ds(pl.multiple_of(i * n, n), n), :] = slab
            if emit:
                emitted[k][...] = slab

    @pl.when(i >= N_CAST)
    def _():
        _mlp_tile(final, x_ref, ya_ref, yb_ref, m_ref, g_ref, fg_ref, o_ref, wo_ref, w1_ref, w2_ref)


def _resident(shape):
    zeros = (0,) * len(shape)
    return pl.BlockSpec(shape, lambda i: zeros, pipeline_mode=pl.Buffered(1))


def _mlp_call(x, x_off, ya, ya_off, ya_col, yb, yb_off, yb_col, n_tiles, tile0, mods, layer, g2, final_g, final,
              f32_weights=None, bf16_weights=None, emit=False):
    tm = TM_MLP
    lead = N_CAST if f32_weights is not None else 0

    def tile(i):
        return jnp.maximum(i - lead, 0)

    stream_specs = [
        pl.BlockSpec((tm, D_MODEL), lambda i: (tile(i) + x_off, 0)),
        pl.BlockSpec((tm, HALF), lambda i: (tile(i) + ya_off, ya_col)),
        pl.BlockSpec((tm, HALF), lambda i: (tile(i) + yb_off, yb_col)),
    ]
    mod_spec = pl.BlockSpec((None, None, 6, D_MODEL), lambda i: (layer, _cond_row(tile(i) + tile0, tm), 0, 0))
    out_spec = pl.BlockSpec((tm, D_MODEL), lambda i: (tile(i), 0))
    out_shape = jax.ShapeDtypeStruct((n_tiles * tm, D_MODEL), F32)
    vecs = (g2.reshape(1, D_MODEL), final_g.reshape(1, D_MODEL))
    if f32_weights is None:
        w_out, w1, w2 = bf16_weights
        return pl.pallas_call(
            functools.partial(_mlp_resident_body, final),
            grid=(n_tiles,),
            in_specs=stream_specs + [_resident((D_MODEL, D_MODEL)), mod_spec, _resident((1, D_MODEL)),
                                     _resident((1, D_MODEL)), _resident((D_MODEL, D_FF)), _resident((D_FF, D_MODEL))],
            out_specs=out_spec,
            out_shape=out_shape,
            compiler_params=_cparams(("parallel",)),
            name="mlp",
        )(x, ya, yb, w_out, mods, *vecs, w1, w2)

    wo_stack, wo_idx, w1_stack, w2_stack = f32_weights

    def slab(i):
        return jnp.minimum(i, N_CAST - 1)

    shapes = ((D_MODEL, D_MODEL), (D_MODEL, D_FF), (D_FF, D_MODEL))
    slab_rows = [rows // N_CAST for rows, _ in shapes]
    f32_specs = [pl.BlockSpec((None, n, cols), lambda i, idx=idx: (idx, slab(i), 0))
                 for n, (_, cols), idx in zip(slab_rows, shapes, (wo_idx, layer, layer))]
    out_specs, out_shapes = [out_spec], [out_shape]
    if emit:
        out_specs += [pl.BlockSpec((n, cols), lambda i: (slab(i), 0)) for n, (_, cols) in zip(slab_rows, shapes)]
        out_shapes += [jax.ShapeDtypeStruct(shp, BF16) for shp in shapes]
    return pl.pallas_call(
        functools.partial(_mlp_cast_body, final, emit),
        grid=(N_CAST + n_tiles,),
        in_specs=stream_specs + [f32_specs[0], mod_spec, _resident((1, D_MODEL)), _resident((1, D_MODEL)),
                                 f32_specs[1], f32_specs[2]],
        out_specs=out_specs,
        out_shape=out_shapes,
        scratch_shapes=[pltpu.VMEM(shp, BF16) for shp in shapes],
        compiler_params=_cparams(("arbitrary",)),
        name="mlp_cast",
    )(x, ya, yb, wo_stack, mods, *vecs, w1_stack, w2_stack)


def _mlp(x, ya, ya_col, yb, yb_col, w_out_stack, w_out_idx, mods, layer, g2, final_g, w1_stack, w2_stack, final):
    tm = TM_MLP
    npt = N_PROMPT_TOK // tm
    f32_weights = (w_out_stack, w_out_idx, w1_stack, w2_stack)
    common = (mods, layer, g2, final_g, final)
    if not final:
        if isinstance(ya, tuple):
            ya = jnp.concatenate(ya, axis=0)
        return tuple(_mlp_call(x, 0, ya, 0, ya_col, yb, 0, yb_col, N_TOK // tm, 0, *common, f32_weights=f32_weights))
    ya_p, ya_s = ((ya[0], 0), (ya[1], 0)) if isinstance(ya, tuple) else ((ya, 0), (ya, npt))
    y_p, *bf16_weights = _mlp_call(x, 0, ya_p[0], ya_p[1], ya_col, yb, 0, yb_col, npt, 0, *common,
                                   f32_weights=f32_weights, emit=True)
    y_s = _mlp_call(x, npt, ya_s[0], ya_s[1], ya_col, yb, npt, yb_col, N_TOK // tm - npt, npt, *common,
                    bf16_weights=bf16_weights)
    return y_p, y_s


def _ctx_attn_body(q_ref, k_ref, v_ref, o_ref, ck_ref, cv_ref):
    lane = lax.broadcasted_iota(jnp.int32, (1, 2 * HEAD_DIM), 1)
    own = [lane < HEAD_DIM, lane >= HEAD_DIM]
    for h in range(N_HEADS):
        sl = slice(h * HEAD_DIM, (h + 1) * HEAD_DIM)
        ck_ref[h] = k_ref[:, sl].astype(F32)
        cv_ref[h] = v_ref[:, sl].astype(F32)
    for hp in range(N_HEADS // 2):
        sl = slice(hp * 2 * HEAD_DIM, (hp + 1) * 2 * HEAD_DIM)
        q = q_ref[:, sl] * jnp.asarray(HEAD_DIM ** -0.5, BF16)
        k = k_ref[:, sl]
        v = v_ref[:, sl]
        q2 = jnp.concatenate([jnp.where(own[0], q, jnp.zeros_like(q)),
                              jnp.where(own[1], q, jnp.zeros_like(q))], axis=0)
        s = _dot_nt(q2, k)
        m = jnp.max(s, axis=-1, keepdims=True)
        p = jnp.exp(s - m).astype(BF16)
        out = None
        for hh in range(2):
            num = _dot(p[hh * SEQ:(hh + 1) * SEQ], jnp.where(own[hh], v, jnp.ones_like(v)))
            den = jnp.where(own[hh], pltpu.roll(num, HEAD_DIM, 1), 1.0)
            o = num / den
            out = o if out is None else jnp.where(own[0], out, o)
        o_ref[:, sl] = out.astype(BF16)


def _ctx_attn(proj):
    return pl.pallas_call(
        _ctx_attn_body,
        grid=(BATCH,),
        in_specs=[
            pl.BlockSpec((SEQ, HALF), lambda b: (b, 0)),
            pl.BlockSpec((SEQ, HALF), lambda b: (b, 1)),
            pl.BlockSpec((SEQ, HALF), lambda b: (b, 2)),
        ],
        out_specs=[
            pl.BlockSpec((SEQ, HALF), lambda b: (b, 0)),
            pl.BlockSpec((None, None, N_HEADS, SEQ, HEAD_DIM), lambda b: (b, 0, 0, 0, 0)),
            pl.BlockSpec((None, None, N_HEADS, SEQ, HEAD_DIM), lambda b: (b, 0, 0, 0, 0)),
        ],
        out_shape=[
            jax.ShapeDtypeStruct((N_PROMPT_TOK, HALF), BF16),
            jax.ShapeDtypeStruct((BATCH, 1, N_HEADS, SEQ, HEAD_DIM), F32),
            jax.ShapeDtypeStruct((BATCH, 1, N_HEADS, SEQ, HEAD_DIM), F32),
        ],
        compiler_params=_cparams(("parallel",)),
        name="ctx_attn",
    )(proj, proj, proj)


NA_MASKED = -math.inf
NA_QROWS = 4
NA_KROWS = NA_QROWS + WIN_H
NA_NQ = NA_QROWS * GRID_W
NA_NK = NA_KROWS * GRID_W
NA_BLOCKS = GRID_ROWS // NA_QROWS
NA_TYPES = 3


def _na_col_table(rpb):
    qc = jnp.arange(GRID_W)[:, None]
    kc = jnp.arange(GRID_W)[None, :]
    c0 = jnp.clip(qc - WIN_W // 2, 0, GRID_W - WIN_W)
    valid = (kc >= c0) & (kc < c0 + WIN_W)
    coff = jnp.clip(kc - qc, -(WIN_W - 1), WIN_W - 1) + (WIN_W - 1)
    onehot = (coff.reshape(1, -1) == jnp.arange(2 * WIN_W - 1)[:, None]).astype(F32)
    t = jnp.dot(rpb.astype(F32).reshape(N_HEADS * (2 * WIN_H - 1), 2 * WIN_W - 1), onehot,
                precision=lax.Precision.HIGHEST)
    t = t.reshape(N_HEADS, 2 * WIN_H - 1, GRID_W, GRID_W)
    return jnp.where(valid[None, None], t, NA_MASKED)


def _na_window_start(rb):
    return min(max(NA_QROWS * rb - WIN_H // 2, 0), GRID_ROWS - NA_KROWS)


def _na_body(q_ref, k_ref, v_ref, ck_ref, cv_ref, tab_ref, o_ref, bias_ref, kc_ref, vc_ref, va_ref):
    @pl.when(pl.program_id(1) == 0)
    def _():
        masked = jnp.full((GRID_W, GRID_W), NA_MASKED, F32)
        for hh in range(2):
            for ty in range(NA_TYPES):
                q0 = (0, NA_QROWS, GRID_ROWS - NA_QROWS)[ty]
                ks = (0, 0, GRID_ROWS - NA_KROWS)[ty]
                for a in range(NA_QROWS):
                    r0 = min(max(q0 + a - WIN_H // 2, 0), GRID_ROWS - WIN_H)
                    for y in range(NA_KROWS):
                        kr = ks + y
                        ok = r0 <= kr < r0 + WIN_H
                        blk = tab_ref[hh, kr - (q0 + a) + WIN_H - 1] if ok else masked
                        bias_ref[ty, hh * NA_NQ + a * GRID_W:hh * NA_NQ + (a + 1) * GRID_W,
                                 y * GRID_W:(y + 1) * GRID_W] = blk

    lane = lax.broadcasted_iota(jnp.int32, (1, 2 * HEAD_DIM), 1)
    own = [lane < HEAD_DIM, lane >= HEAD_DIM]
    kc_ref[...] = jnp.concatenate([ck_ref[0], ck_ref[1]], axis=1).astype(BF16)
    vc = jnp.concatenate([cv_ref[0], cv_ref[1]], axis=1)
    v = v_ref[...]
    for hh in range(2):
        vc_ref[hh] = jnp.where(own[hh], vc, 1.0).astype(BF16)
        va_ref[hh] = jnp.where(own[hh], v, jnp.ones_like(v))

    def block(rb, ty):
        qs = rb * NA_NQ
        ks = _na_window_start(rb) * GRID_W
        q = q_ref[pl.ds(qs, NA_NQ), :] * jnp.asarray(HEAD_DIM ** -0.5, BF16)
        q2 = jnp.concatenate([jnp.where(own[0], q, jnp.zeros_like(q)),
                              jnp.where(own[1], q, jnp.zeros_like(q))], axis=0)
        s_loc = _dot_nt(q2, k_ref[pl.ds(ks, NA_NK), :]) + bias_ref[ty]
        s_ctx = _dot_nt(q2, kc_ref[...])
        m = jnp.maximum(jnp.max(s_loc, axis=-1, keepdims=True), jnp.max(s_ctx, axis=-1, keepdims=True))
        p_loc = jnp.exp(s_loc - m).astype(BF16)
        p_ctx = jnp.exp(s_ctx - m).astype(BF16)
        out = None
        for hh in range(2):
            rows = slice(hh * NA_NQ, (hh + 1) * NA_NQ)
            num = _dot(p_loc[rows], va_ref[hh, pl.ds(ks, NA_NK), :]) + _dot(p_ctx[rows], vc_ref[hh])
            den = jnp.where(own[hh], pltpu.roll(num, HEAD_DIM, 1), 1.0)
            o = num / den
            out = o if out is None else jnp.where(own[0], out, o)
        o_ref[pl.ds(qs, NA_NQ), :] = out.astype(BF16)

    for rb in range(NA_BLOCKS):
        block(rb, 0 if rb == 0 else (2 if rb == NA_BLOCKS - 1 else 1))


def _na_attn(proj, cache_k, cache_v, col_table, layer_idx):
    row_blk0 = N_PROMPT_TOK // DEC_SEQ
    hp = N_HEADS // 2
    return pl.pallas_call(
        _na_body,
        grid=(hp, DEC_BATCH),
        in_specs=[
            pl.BlockSpec((DEC_SEQ, 128), lambda h, b: (row_blk0 + b, h)),
            pl.BlockSpec((DEC_SEQ, 128), lambda h, b: (row_blk0 + b, hp + h)),
            pl.BlockSpec((DEC_SEQ, 128), lambda h, b: (row_blk0 + b, 2 * hp + h)),
            pl.BlockSpec((None, None, 2, PAST_LEN, HEAD_DIM), lambda h, b: (b, layer_idx, h, 0, 0)),
            pl.BlockSpec((None, None, 2, PAST_LEN, HEAD_DIM), lambda h, b: (b, layer_idx, h, 0, 0)),
            pl.BlockSpec((2, 2 * WIN_H - 1, GRID_W, GRID_W), lambda h, b: (h, 0, 0, 0)),
        ],
        out_specs=pl.BlockSpec((DEC_SEQ, 128), lambda h, b: (b, h)),
        out_shape=jax.ShapeDtypeStruct((N_SAMPLE_TOK, HALF), BF16),
        scratch_shapes=[pltpu.VMEM((NA_TYPES, 2 * NA_NQ, NA_NK), F32),
                        pltpu.VMEM((PAST_LEN, 2 * HEAD_DIM), BF16),
                        pltpu.VMEM((2, PAST_LEN, 2 * HEAD_DIM), BF16),
                        pltpu.VMEM((2, DEC_SEQ, 2 * HEAD_DIM), BF16)],
        compiler_params=_cparams(("parallel", "arbitrary")),
        name="na_attn",
    )(proj, proj, proj, cache_k, cache_v, col_table)


N_POW = 24


def _s5_prep_body(vec_ref, bt_ref, c_ref, wend_ref, ktoe_ref, woutT_ref, apow_ref, pw_ref, pct_ref):
    L = S5_CHUNK
    p = pl.program_id(0)
    lane_blk = lax.broadcasted_iota(jnp.int32, (S5_CH, S5_W), 1) // S5_CH
    src_blk = (lane_blk + L - p) % L
    krows = []
    for d in range(2):
        v = vec_ref[d]
        lre, lim, step = v[0:1], v[1:2], v[2:3]
        a = lre * step
        b = lim * step
        ea = jnp.exp(a)
        nr = ea * jnp.cos(b) - 1.0
        ni = ea * jnp.sin(b)
        den = lre * lre + lim * lim
        qr = (nr * lre + ni * lim) / den
        qi = (ni * lre - nr * lim) / den
        bre, bim = bt_ref[d, 0], bt_ref[d, 1]
        bqr = bre * qr - bim * qi
        bqi = bre * qi + bim * qr
        cre, cim = c_ref[d, 0], c_ref[d, 1]

        e = lax.broadcasted_iota(jnp.int32, (N_POW, 1), 0).astype(F32)
        mag = jnp.exp(e * a)
        pw_ref[0] = mag * jnp.cos(e * b)
        pw_ref[1] = mag * jnp.sin(e * b)
        mag = jnp.exp((L * e) * a)
        apow_ref[d, 0] = mag * jnp.cos((L * e) * b)
        apow_ref[d, 1] = mag * jnp.sin((L * e) * b)

        def power(k):
            return pw_ref[0, k:k + 1, :], pw_ref[1, k:k + 1, :]

        def c_block(k):
            pr, pi = power(k)
            return cre * pr - cim * pi, -(cre * pi + cim * pr)

        for i in range(L):
            rows = pl.ds(pl.multiple_of(((i + p) % L) * S5_CH, S5_CH), S5_CH)
            pr, pi = power(L - 1 - i if d == 0 else i)
            wend_ref[rows, d * 256:d * 256 + 128] = (bqr * pr - bqi * pi).astype(BF16)
            wend_ref[rows, d * 256 + 128:d * 256 + 256] = (bqr * pi + bqi * pr).astype(BF16)
            mre, mim = c_block(i + 1 if d == 0 else L - i)
            woutT_ref[rows, d * 256:d * 256 + 128] = mre.astype(BF16)
            woutT_ref[rows, d * 256 + 128:d * 256 + 256] = mim.astype(BF16)
            lag_rows = slice(i * S5_CH, (i + 1) * S5_CH)
            mre, mim = c_block(i if d == 0 else L - 1 - i)
            pct_ref[lag_rows, 0:128] = mre
            pct_ref[lag_rows, 128:256] = mim

        bqcat = jnp.concatenate([bqr, bqi], axis=1)
        krows.append(_dot_nt(bqcat, pct_ref[...], precision=lax.Precision.HIGHEST))

    base_f = pltpu.roll(krows[0], p * S5_CH, 1)
    base_b = pltpu.roll(krows[1], p * S5_CH, 1)
    for i in range(L):
        rows = pl.ds(pl.multiple_of(((i + p) % L) * S5_CH, S5_CH), S5_CH)
        blk_f = base_f if i == 0 else pltpu.roll(base_f, i * S5_CH, 1)
        blk_b = base_b if (i + 1) % L == 0 else pltpu.roll(base_b, ((i + 1) % L) * S5_CH, 1)
        blk = jnp.where(src_blk >= i, blk_f, 0.0) + jnp.where(src_blk <= i, blk_b, 0.0)
        ktoe_ref[rows, :] = blk.astype(BF16)


def _s5_prep(lam_re, lam_im, log_step, b_re, b_im, c_re, c_im):
    def pair_vec(a):
        return a.astype(F32).reshape(2, S5_PAIRS, 1, S5_P).transpose(1, 0, 2, 3)

    step = jnp.broadcast_to(jnp.exp(log_step.astype(F32))[:, :, None], (2, N_SSM_GROUPS, SSM_STATE))
    vec = jnp.concatenate([pair_vec(lam_re), pair_vec(lam_im), pair_vec(step),
                           jnp.zeros((S5_PAIRS, 2, 5, S5_P), F32)], axis=2)

    def block_diag_ch_by_state(m):
        m = m.astype(F32).reshape(2, S5_PAIRS, 2, SSM_GROUP, SSM_STATE)
        eye = jnp.eye(2, dtype=F32)
        out = m[:, :, :, :, None, :] * eye[None, None, :, None, :, None]
        return out.reshape(2, S5_PAIRS, S5_CH, S5_P).transpose(1, 0, 2, 3)

    bt = jnp.stack([block_diag_ch_by_state(b_re.transpose(0, 1, 3, 2)),
                    block_diag_ch_by_state(b_im.transpose(0, 1, 3, 2))], axis=2)
    cc = jnp.stack([block_diag_ch_by_state(c_re), block_diag_ch_by_state(c_im)], axis=2)

    mat = jax.ShapeDtypeStruct((S5_PAIRS, S5_W, S5_W), BF16)
    mat_spec = pl.BlockSpec((None, S5_W, S5_W), lambda p: (p, 0, 0))
    return pl.pallas_call(
        _s5_prep_body,
        grid=(S5_PAIRS,),
        in_specs=[
            pl.BlockSpec((None, 2, 8, S5_P), lambda p: (p, 0, 0, 0)),
            pl.BlockSpec((None, 2, 2, S5_CH, S5_P), lambda p: (p, 0, 0, 0, 0)),
            pl.BlockSpec((None, 2, 2, S5_CH, S5_P), lambda p: (p, 0, 0, 0, 0)),
        ],
        out_specs=[mat_spec, mat_spec, mat_spec,
                   pl.BlockSpec((None, 2, 2, N_POW, S5_P), lambda p: (p, 0, 0, 0, 0))],
        out_shape=[mat, mat, mat, jax.ShapeDtypeStruct((S5_PAIRS, 2, 2, N_POW, S5_P), F32)],
        scratch_shapes=[pltpu.VMEM((2, N_POW, S5_P), F32), pltpu.VMEM((S5_W, 2 * S5_P), F32)],
        compiler_params=_cparams(("parallel",)),
        name="s5_prep",
    )(vec, bt, cc)


S5_SEG_ROWS = 256
S5_N_SEG = N_CHUNK_ROWS // S5_SEG_ROWS
S5_SUB = S5_SEG_ROWS // S5_CHUNK
S5_SUB_PER_SEQ = (DEC_SEQ // S5_CHUNK) // S5_CHUNK


S5_PPS = 4


def _s5_body(tu_ref, s0_ref, wend_ref, ktoe_ref, woutT_ref, apow_ref, ty_ref, fin_ref, z_ref, xs_ref, g_ref):
    seg = pl.program_id(0)
    L = S5_CHUNK
    lane_blk = lax.broadcasted_iota(jnp.int32, (S5_SEG_ROWS, 128), 1) // S5_CH
    pairs = [S5_PPS * pl.program_id(1) + n for n in range(S5_PPS)]

    s_rows, ends = [], []
    for n, p in enumerate(pairs):
        tiles = []
        for t in range(LANE_TILES):
            tile = None
            for k in range(128 // S5_CH):
                j = (LANE_TILES * t + k + L - p) % L
                blk = tu_ref[j, t]
                tile = blk if tile is None else jnp.where(lane_blk == k, blk, tile)
            tiles.append(tile)
        s = jnp.concatenate(tiles, axis=1)
        s_rows.append(s)
        z = _dot(s, wend_ref[n])
        for q in range(LANE_TILES):
            z_ref[n, q] = z[:, q * 128:(q + 1) * 128]

        pair_ends = []
        for d in range(2):
            are, aim = apow_ref[n, d, 0, 1:2, :], apow_ref[n, d, 1, 1:2, :]
            xr = jnp.zeros((S5_SUB, S5_P), F32)
            xi = jnp.zeros((S5_SUB, S5_P), F32)
            for c in (range(L) if d == 0 else reversed(range(L))):
                rows = pl.ds(c, S5_SUB, stride=L)
                xs_ref[n, 2 * d, rows, :] = xr
                xs_ref[n, 2 * d + 1, rows, :] = xi
                zr = z_ref[n, 2 * d, rows, :]
                zi = z_ref[n, 2 * d + 1, rows, :]
                xr, xi = are * xr - aim * xi + zr, are * xi + aim * xr + zi
            pair_ends.append((xr, xi))
            fin_ref[n, d, 0] = xr
            fin_ref[n, d, 1] = xi
        ends.append(pair_ends)

    @pl.when(seg > 0)
    def _():
        for n in range(S5_PPS):
            for d in range(2):
                er, ei = ends[n][d]
                a16r, a16i = apow_ref[n, d, 0, L:L + 1, :], apow_ref[n, d, 1, L:L + 1, :]
                for bb in range(S5_SUB // S5_SUB_PER_SEQ):
                    gr = s0_ref[n, bb, 2 * d:2 * d + 1, :]
                    gi = s0_ref[n, bb, 2 * d + 1:2 * d + 2, :]
                    order = range(S5_SUB_PER_SEQ) if d == 0 else reversed(range(S5_SUB_PER_SEQ))
                    prev = None
                    for sub in order:
                        row = bb * S5_SUB_PER_SEQ + sub
                        if prev is not None:
                            gr, gi = (a16r * gr - a16i * gi + er[prev:prev + 1, :],
                                      a16r * gi + a16i * gr + ei[prev:prev + 1, :])
                        g_ref[n, 0, row:row + 1, :] = gr
                        g_ref[n, 1, row:row + 1, :] = gi
                        prev = row
                if d == 0:
                    pr, pi = apow_ref[n, d, 0, 0:L, :], apow_ref[n, d, 1, 0:L, :]
                else:
                    pr = jnp.concatenate([apow_ref[n, d, 0, k:k + 1, :] for k in reversed(range(L))], axis=0)
                    pi = jnp.concatenate([apow_ref[n, d, 1, k:k + 1, :] for k in reversed(range(L))], axis=0)
                for sub in range(S5_SUB):
                    rows = slice(sub * L, (sub + 1) * L)
                    gr = g_ref[n, 0, sub:sub + 1, :]
                    gi = g_ref[n, 1, sub:sub + 1, :]
                    xs_ref[n, 2 * d, rows, :] = xs_ref[n, 2 * d, rows, :] + (pr * gr - pi * gi)
                    xs_ref[n, 2 * d + 1, rows, :] = xs_ref[n, 2 * d + 1, rows, :] + (pr * gi + pi * gr)

    @pl.when(pl.program_id(1) == 0)
    def _():
        ty_ref[...] = jnp.zeros_like(ty_ref)

    for n, p in enumerate(pairs):
        xs = jnp.concatenate([xs_ref[n, q].astype(BF16) for q in range(LANE_TILES)], axis=1)
        y = _dot(s_rows[n], ktoe_ref[n]) + _dot_nt(xs, woutT_ref[n])
        for t in range(LANE_TILES):
            yt = y[:, t * 128:(t + 1) * 128]
            for k in range(128 // S5_CH):
                j = (LANE_TILES * t + k + L - p) % L
                ty_ref[j, t] = jnp.where(lane_blk == k, yt, ty_ref[j, t])


def _s5_scan(tu, s0, wend, ktoe, woutT, apow):
    mat_spec = pl.BlockSpec((S5_PPS, S5_W, S5_W), lambda g, p: (p, 0, 0))
    seqs = S5_SUB // S5_SUB_PER_SEQ
    return pl.pallas_call(
        _s5_body,
        grid=(S5_N_SEG, S5_PAIRS // S5_PPS),
        in_specs=[
            pl.BlockSpec((S5_CHUNK, LANE_TILES, S5_SEG_ROWS, 128), lambda g, p: (0, 0, g, 0)),
            pl.BlockSpec((S5_PPS, seqs, 4, S5_P), lambda g, p: (p, jnp.maximum(g - 1, 0), 0, 0)),
            mat_spec, mat_spec, mat_spec,
            pl.BlockSpec((S5_PPS, 2, 2, N_POW, S5_P), lambda g, p: (p, 0, 0, 0, 0)),
        ],
        out_specs=[
            pl.BlockSpec((S5_CHUNK, LANE_TILES, S5_SEG_ROWS, 128), lambda g, p: (0, 0, g, 0)),
            pl.BlockSpec((S5_PPS, 2, 2, S5_SUB, S5_P), lambda g, p: (p, 0, 0, g, 0)),
        ],
        out_shape=[
            jax.ShapeDtypeStruct((S5_CHUNK, LANE_TILES, N_CHUNK_ROWS, 128), F32),
            jax.ShapeDtypeStruct((S5_PAIRS, 2, 2, S5_N_SEG * S5_SUB, S5_P), F32),
        ],
        scratch_shapes=[pltpu.VMEM((S5_PPS, LANE_TILES, S5_SEG_ROWS, 128), F32),
                        pltpu.VMEM((S5_PPS, LANE_TILES, S5_SEG_ROWS, 128), F32),
                        pltpu.VMEM((S5_PPS, 2, S5_SUB, S5_P), F32)],
        compiler_params=_cparams(("parallel", "arbitrary")),
        name="s5_scan",
    )(tu, s0, wend, ktoe, woutT, apow)


TM_GLU = 2048


def _glu_body(tu_ref, ty_ref, d_ref, gw_ref, gb_ref, o_ref, out_ref):
    rows = TM_GLU // S5_CHUNK
    d = d_ref[...]
    pre = []
    for j in range(S5_CHUNK):
        uj = jnp.concatenate([tu_ref[j, t].astype(F32) for t in range(LANE_TILES)], axis=1)
        yj = jnp.concatenate([ty_ref[j, t] for t in range(LANE_TILES)], axis=1)
        dj = pltpu.roll(d, j * S5_CH, 1) if j else d
        yj = dj * uj + yj
        back = ((S5_CHUNK - j) % S5_CHUNK) * S5_CH
        pre.append(pltpu.roll(yj, back, 1) if back else yj)
    y = jnp.concatenate(pre, axis=0)
    cdf = 0.5 * (1.0 + jnp.tanh(math.sqrt(2.0 / math.pi) * (y + 0.044715 * (y * y * y))))
    y = y * cdf
    gate = jax.nn.sigmoid(_dot(y.astype(BF16), gw_ref[...]) + gb_ref[...])
    out = y * gate
    for j in range(S5_CHUNK):
        oj = out[j * rows:(j + 1) * rows]
        for t in range(LANE_TILES):
            out_ref[t, pl.ds(j, rows, stride=S5_CHUNK), :] = oj[:, t * 128:(t + 1) * 128]
    o_ref[...] = jnp.concatenate([out_ref[t] for t in range(LANE_TILES)], axis=1).astype(BF16)


def _glu(tu, ty, ssm_d, glu_w, glu_b):
    tm = TM_GLU
    rows = tm // S5_CHUNK
    lay_spec = pl.BlockSpec((S5_CHUNK, LANE_TILES, rows, 128), lambda i: (0, 0, i, 0))
    return pl.pallas_call(
        _glu_body,
        grid=(N_TOK // tm,),
        in_specs=[
            lay_spec, lay_spec,
            pl.BlockSpec((1, HALF), lambda i: (0, 0)),
            pl.BlockSpec((HALF, HALF), lambda i: (0, 0)),
            pl.BlockSpec((1, HALF), lambda i: (0, 0)),
        ],
        out_specs=pl.BlockSpec((tm, HALF), lambda i: (i, 0)),
        out_shape=jax.ShapeDtypeStruct((N_TOK, HALF), BF16),
        scratch_shapes=[pltpu.VMEM((LANE_TILES, tm, 128), F32)],
        compiler_params=_cparams(("parallel",)),
        name="s5_glu",
    )(tu, ty, ssm_d.reshape(1, HALF), glu_w.astype(BF16), glu_b.reshape(1, HALF))


def kernel(x_prompt, x_sample, c, cache_k, cache_v, state_s5_re, state_s5_im, c_ctx, norm1_g, norm2_g, ada_w, ada_b, mlp_w1, mlp_w2, ab_w_in, pool_w, pool_scale, conv_w, ab_w_out, cd_w_in, na_rpb, ssm_lambda_re, ssm_lambda_im, ssm_log_step, ssm_b_re, ssm_b_im, ssm_c_re, ssm_c_im, ssm_d, glu_w, glu_b, cd_w_out, final_g):
    depth = ada_w.shape[0]
    xs = (x_prompt.reshape(N_PROMPT_TOK, D_MODEL), x_sample.reshape(N_SAMPLE_TOK, D_MODEL))
    cond = jnp.concatenate([c_ctx[None, :], c, jnp.zeros((N_COND - 1 - DEC_BATCH, D_MODEL), F32)], axis=0)
    mods = _adaln(cond, ada_w, ada_b).reshape(depth, N_COND, 6, D_MODEL)

    new_k, new_v, new_re, new_im = [], [], [], []
    for layer in range(depth):
        i = layer // 2
        final = layer == depth - 1
        if layer % 2 == 0:
            y, x = _even_layer_mix(xs, mods, layer, norm1_g[layer], ab_w_in, i,
                                   pool_w[i], pool_scale[i], conv_w[i])
            ya, ya_col, yb, yb_col = y, 0, y, 1
            w_out_stack = ab_w_out
        else:
            x = xs[0]
            qkv, tu = _inproj_odd(x, mods, layer, norm1_g[layer], cd_w_in, i)
            o_p, ck, cv = _ctx_attn(qkv)
            new_k.append(ck)
            new_v.append(cv)
            o_s = _na_attn(qkv, cache_k, cache_v, _na_col_table(na_rpb[i]), i)
            attn = (o_p, o_s)

            wend, ktoe, woutT, apow = _s5_prep(ssm_lambda_re[i], ssm_lambda_im[i], ssm_log_step[i],
                                               ssm_b_re[i], ssm_b_im[i], ssm_c_re[i], ssm_c_im[i])

            def pair_state(s):
                return s.astype(F32).reshape(DEC_BATCH, 2, S5_PAIRS, S5_P).transpose(2, 0, 1, 3)

            s0 = jnp.stack([pair_state(state_s5_re[:, i]), pair_state(state_s5_im[:, i])], axis=3)
            s0 = s0.reshape(S5_PAIRS, DEC_BATCH, 4, S5_P)
            ty, fin = _s5_scan(tu, s0, wend, ktoe, woutT, apow)
            fin = fin[:, :, :, :BATCH]
            new_re.append(fin[:, :, 0].transpose(2, 1, 0, 3).reshape(BATCH, 2, N_SSM_GROUPS, SSM_STATE))
            new_im.append(fin[:, :, 1].transpose(2, 1, 0, 3).reshape(BATCH, 2, N_SSM_GROUPS, SSM_STATE))
            d_out = _glu(tu, ty, ssm_d[i], glu_w[i], glu_b[i])
            ya, ya_col, yb, yb_col = attn, 0, d_out, 0
            w_out_stack = cd_w_out
        xs = _mlp(x, ya, ya_col, yb, yb_col, w_out_stack, i, mods, layer, norm2_g[layer], final_g,
                  mlp_w1, mlp_w2, final)

    y_prompt = xs[0].reshape(BATCH, SEQ, D_MODEL)
    y_sample = xs[1].reshape(DEC_BATCH, DEC_SEQ, D_MODEL)
    return (y_prompt, y_sample, jnp.concatenate(new_k, axis=1), jnp.concatenate(new_v, axis=1),
            jnp.stack(new_re, axis=1), jnp.stack(new_im, axis=1))
```

```python
import functools
import math

import jax
import jax.numpy as jnp
from jax import lax
from jax.experimental import pallas as pl
from jax.experimental.pallas import tpu as pltpu

F32 = jnp.float32
BF16 = jnp.bfloat16

D_MODEL = 1024
BATCH = 16
SEQ = 256
DEC_BATCH = 4
DEC_SEQ = 2048
PAST_LEN = 256
GRID_W = 64
GRID_ROWS = DEC_SEQ // GRID_W
HALF = 512
POOL_WINDOWS = (2, 4, 8, 16)
POOL_GROUP = 128
HEAD_DIM = 64
N_HEADS = 8
WIN_H = 8
WIN_W = 16
SSM_GROUP = 16
N_SSM_GROUPS = 32
SSM_STATE = 64
D_FF = 4096
EPS = 1e-6

N_PROMPT_TOK = BATCH * SEQ
N_SAMPLE_TOK = DEC_BATCH * DEC_SEQ
N_TOK = N_PROMPT_TOK + N_SAMPLE_TOK
N_COND = 8

S5_CHUNK = 16
S5_PAIRS = N_SSM_GROUPS // 2
S5_CH = 2 * SSM_GROUP
S5_P = 2 * SSM_STATE
S5_W = S5_CHUNK * S5_CH

VMEM_LIMIT = 56 * 1024 * 1024


def _cparams(sem):
    return pltpu.CompilerParams(dimension_semantics=sem, vmem_limit_bytes=VMEM_LIMIT)


def _cond_row(i, tm):
    npt = N_PROMPT_TOK // tm
    per = DEC_SEQ // tm
    return jnp.where(i < npt, 0, 1 + (i - npt) // per)


def _resident_layer(shape, idx):
    zeros = (0,) * len(shape)
    return pl.BlockSpec((None,) + tuple(shape), lambda i: (idx,) + zeros, pipeline_mode=pl.Buffered(1))


def _normmod(x, g, shift, scale):
    ms = jnp.mean(x * x, axis=-1, keepdims=True)
    return (x * lax.rsqrt(ms + EPS) * g) * (1.0 + scale) + shift


def _dot(a, b):
    return jnp.dot(a, b, preferred_element_type=F32)


def _dot_nt(a, b, precision=None):
    return lax.dot_general(a, b, (((1,), (1,)), ((), ())), preferred_element_type=F32, precision=precision)


def _adaln_body(c_ref, w_ref, b_ref, o_ref):
    c = c_ref[...]
    s = c * jax.nn.sigmoid(c)
    o_ref[...] = _dot(s.astype(BF16), w_ref[...].astype(BF16)) + b_ref[...]


def _adaln(cond, ada_w, ada_b):
    depth = ada_w.shape[0]
    nj = 6
    return pl.pallas_call(
        _adaln_body,
        grid=(depth, nj),
        in_specs=[
            pl.BlockSpec((N_COND, D_MODEL), lambda l, j: (0, 0)),
            pl.BlockSpec((None, D_MODEL, D_MODEL), lambda l, j: (l, 0, j)),
            pl.BlockSpec((None, 1, D_MODEL), lambda l, j: (l, 0, j)),
        ],
        out_specs=pl.BlockSpec((None, N_COND, D_MODEL), lambda l, j: (l, 0, j)),
        out_shape=jax.ShapeDtypeStruct((depth, N_COND, 6 * D_MODEL), F32),
        compiler_params=_cparams(("arbitrary", "arbitrary")),
        name="adaln",
    )(cond, ada_w, ada_b.reshape(depth, 1, 6 * D_MODEL))


TM_PROJ = 1024


def _stream_tile(x_refs, tm):
    if len(x_refs) == 1:
        return x_refs[0][...]
    is_prompt = pl.program_id(0) < N_PROMPT_TOK // tm
    return jnp.where(is_prompt, x_refs[0][...], x_refs[1][...])


N_CHUNK_ROWS = N_TOK // S5_CHUNK
LANE_TILES = S5_W // 128


def _inproj_odd_body(x_ref, m_ref, g_ref, w_ref, qkv_ref, tu_ref, u_ref, wbf_ref):
    @pl.when(pl.program_id(0) == 0)
    def _():
        wbf_ref[...] = w_ref[...].astype(BF16)

    m = m_ref[...]
    h = _normmod(x_ref[...], g_ref[...], m[0:1], m[1:2])
    acc = _dot(h.astype(BF16), wbf_ref[...])
    qkv_ref[...] = acc[:, 0:3 * HALF].astype(BF16)
    for t in range(LANE_TILES):
        u_ref[t] = acc[:, 3 * HALF + t * 128:3 * HALF + (t + 1) * 128]
    rows = TM_PROJ // S5_CHUNK
    for j in range(S5_CHUNK):
        uj = jnp.concatenate([u_ref[t, pl.ds(j, rows, stride=S5_CHUNK), :] for t in range(LANE_TILES)], axis=1)
        if j:
            uj = pltpu.roll(uj, j * S5_CH, 1)
        for t in range(LANE_TILES):
            tu_ref[j, t] = uj[:, t * 128:(t + 1) * 128].astype(BF16)


def _inproj_odd(x, mods, layer, g, w_stack, w_idx):
    tm = TM_PROJ
    rows = tm // S5_CHUNK
    return pl.pallas_call(
        _inproj_odd_body,
        grid=(N_TOK // tm,),
        in_specs=[
            pl.BlockSpec((tm, D_MODEL), lambda i: (i, 0)),
            pl.BlockSpec((None, None, 6, D_MODEL), lambda i: (layer, _cond_row(i, tm), 0, 0)),
            pl.BlockSpec((1, D_MODEL), lambda i: (0, 0)),
            _resident_layer((D_MODEL, 4 * HALF), w_idx),
        ],
        out_specs=[
            pl.BlockSpec((tm, 3 * HALF), lambda i: (i, 0)),
            pl.BlockSpec((S5_CHUNK, LANE_TILES, rows, 128), lambda i: (0, 0, i, 0)),
        ],
        out_shape=[
            jax.ShapeDtypeStruct((N_TOK, 3 * HALF), BF16),
            jax.ShapeDtypeStruct((S5_CHUNK, LANE_TILES, N_CHUNK_ROWS, 128), BF16),
        ],
        scratch_shapes=[pltpu.VMEM((LANE_TILES, tm, 128), F32), pltpu.VMEM((D_MODEL, 4 * HALF), BF16)],
        compiler_params=_cparams(("arbitrary",)),
        name="inproj_odd",
    )(x, mods, g.reshape(1, D_MODEL), w_stack)


TM_MIX = 256
HALO = 16
RING = 3
LAG = 2
PAD_ROWS = TM_MIX + 2 * HALO + 8


def _even_step(phase, x_refs, m_ref, g_ref, w_ref, pw_ref, ps_ref, cw_ref, y_ref, xcat_ref, proj_ref, tail_ref,
               h_ref):
    i = pl.program_id(0)
    write_slot, cur_slot, next_slot = phase, (phase + 1) % RING, (phase + 2) % RING

    m = m_ref[...]
    x = _stream_tile(x_refs, TM_MIX)
    xcat_ref[...] = x
    h_ref[...] = _normmod(x, g_ref[...], m[0:1], m[1:2]).astype(BF16)
    n_pieces = len(POOL_WINDOWS) + HALF // 128
    pw = (4 * HALF) // n_pieces

    def project(piece):
        cols = slice(piece * pw, (piece + 1) * pw)
        proj_ref[write_slot, :, cols] = _dot(h_ref[...], w_ref[:, cols])

    t = jnp.maximum(i - LAG, 0)
    n_prompt_tiles = N_PROMPT_TOK // TM_MIX
    tiles_per_seq = DEC_SEQ // TM_MIX
    is_prompt = t < n_prompt_tiles
    tile_in_seq = jnp.where(is_prompt, 0, (t - n_prompt_tiles) % tiles_per_seq)
    seq_len = jnp.where(is_prompt, SEQ, DEC_SEQ)
    has_prev = tile_in_seq > 0
    has_next = jnp.logical_and(jnp.logical_not(is_prompt), tile_in_seq < tiles_per_seq - 1)
    prev_keep = jnp.where(has_prev, 1.0, 0.0).astype(F32)
    next_keep = jnp.where(has_next, 1.0, 0.0).astype(F32)
    pos = tile_in_seq * TM_MIX + lax.broadcasted_iota(jnp.int32, (TM_MIX, 1), 0)
    lo_rows, hi_rows = HALO, HALO + TM_MIX
    head = slice(0, HALO)

    slack = jnp.zeros((PAD_ROWS - TM_MIX - 2 * HALO, POOL_GROUP), F32)
    for gi, w in enumerate(POOL_WINDOWS):
        project(gi)
        cs = slice(gi * POOL_GROUP, (gi + 1) * POOL_GROUP)
        cur = proj_ref[cur_slot, :, cs]
        f = jnp.concatenate([tail_ref[:, cs] * prev_keep, cur, proj_ref[next_slot, head, cs] * next_keep, slack],
                            axis=0)
        n_rows, span = PAD_ROWS, 1
        while 2 * span < w:
            n_rows -= 8
            f = f[0:n_rows] + f[span:span + n_rows]
            span *= 2
        s = f[lo_rows - span:hi_rows - span] + f[lo_rows:hi_rows]
        lo = jnp.maximum(pos - w // 2, 0)
        hi = jnp.minimum(pos + (w - w // 2), seq_len)
        inv = 1.0 / (hi - lo).astype(F32)
        pooled = s * inv - cur
        mixed = _dot(pooled.astype(BF16), pw_ref[gi]) * ps_ref[:, cs]
        y_ref[:, cs] = mixed.astype(BF16)

    for ci in range(HALF // 128):
        project(len(POOL_WINDOWS) + ci)
        cs = slice(ci * 128, (ci + 1) * 128)
        bs = slice(HALF + ci * 128, HALF + (ci + 1) * 128)
        gs = slice(2 * HALF + ci * 128, 2 * HALF + (ci + 1) * 128)
        vs = slice(3 * HALF + ci * 128, 3 * HALF + (ci + 1) * 128)
        z = jnp.concatenate([(tail_ref[:, gs] * tail_ref[:, vs]) * prev_keep,
                             proj_ref[cur_slot, :, gs] * proj_ref[cur_slot, :, vs],
                             (proj_ref[next_slot, head, gs] * proj_ref[next_slot, head, vs]) * next_keep], axis=0)
        conv = (cw_ref[0:1, cs] * z[lo_rows - 1:hi_rows - 1]
                + cw_ref[1:2, cs] * z[lo_rows:hi_rows]
                + cw_ref[2:3, cs] * z[lo_rows + 1:hi_rows + 1])
        y_ref[:, bs] = (proj_ref[cur_slot, :, bs] * conv).astype(BF16)

    tail_ref[...] = proj_ref[cur_slot, TM_MIX - HALO:TM_MIX, :]


def _even_body(n_x, *refs):
    x_refs, rest = refs[:n_x], refs[n_x:]
    m_ref, g_ref, w_ref = rest[:3]
    proj_ref, tail_ref, h_ref, wbf_ref = rest[-4:]
    i = pl.program_id(0)

    @pl.when(i == 0)
    def _():
        proj_ref[...] = jnp.zeros_like(proj_ref)
        tail_ref[...] = jnp.zeros_like(tail_ref)
        wbf_ref[...] = w_ref[...].astype(BF16)

    for phase in range(RING):
        @pl.when(i % RING == phase)
        def _(phase=phase):
            _even_step(phase, x_refs, m_ref, g_ref, wbf_ref, *rest[3:-1])


def _even_layer_mix(xs, mods, layer, g, w_stack, w_idx, pool_w, pool_scale, conv_w):
    tm = TM_MIX
    n_tiles = N_TOK // tm
    npt = N_PROMPT_TOK // tm
    last = n_tiles - 1
    if len(xs) == 1:
        x_specs = [pl.BlockSpec((tm, D_MODEL), lambda i: (jnp.minimum(i, last), 0))]
    else:
        x_specs = [pl.BlockSpec((tm, D_MODEL), lambda i: (jnp.minimum(i, npt - 1), 0)),
                   pl.BlockSpec((tm, D_MODEL), lambda i: (jnp.clip(i - npt, 0, last - npt), 0))]
    return pl.pallas_call(
        functools.partial(_even_body, len(xs)),
        grid=(n_tiles + LAG,),
        in_specs=x_specs + [
            pl.BlockSpec((None, None, 6, D_MODEL), lambda i: (layer, _cond_row(jnp.minimum(i, last), tm), 0, 0)),
            pl.BlockSpec((1, D_MODEL), lambda i: (0, 0)),
            _resident_layer((D_MODEL, 4 * HALF), w_idx),
            pl.BlockSpec((len(POOL_WINDOWS), POOL_GROUP, POOL_GROUP), lambda i: (0, 0, 0)),
            pl.BlockSpec((1, HALF), lambda i: (0, 0)),
            pl.BlockSpec((3, HALF), lambda i: (0, 0)),
        ],
        out_specs=[pl.BlockSpec((tm, 2 * HALF), lambda i: (jnp.maximum(i - LAG, 0), 0)),
                   pl.BlockSpec((tm, D_MODEL), lambda i: (jnp.minimum(i, last), 0))],
        out_shape=[jax.ShapeDtypeStruct((N_TOK, 2 * HALF), BF16), jax.ShapeDtypeStruct((N_TOK, D_MODEL), F32)],
        scratch_shapes=[pltpu.VMEM((RING, tm, 4 * HALF), F32), pltpu.VMEM((HALO, 4 * HALF), F32),
                        pltpu.VMEM((tm, D_MODEL), BF16), pltpu.VMEM((D_MODEL, 4 * HALF), BF16)],
        compiler_params=_cparams(("arbitrary",)),
        name="even_layer_mix",
    )(*xs, mods, g.reshape(1, D_MODEL), w_stack, pool_w.astype(BF16), pool_scale.reshape(1, HALF), conv_w)


TM_MLP = 1024
RC_MLP = 256
N_CAST = 16


def _mlp_tile(final, x_ref, ya_ref, yb_ref, m_ref, g_ref, fg_ref, o_ref, wo_ref, w1_ref, w2_ref):
    m = m_ref[...]
    gain = g_ref[...] * (1.0 + m[4:5])
    for r in range(TM_MLP // RC_MLP):
        rows = slice(r * RC_MLP, (r + 1) * RC_MLP)
        mix = _dot(ya_ref[rows, :], wo_ref[0:HALF, :]) + _dot(yb_ref[rows, :], wo_ref[HALF:2 * HALF, :])
        x1 = x_ref[rows, :] + m[2:3] * mix
        ms = jnp.mean(x1 * x1, axis=-1, keepdims=True)
        h = (x1 * lax.rsqrt(ms + EPS) * gain + m[3:4]).astype(BF16)
        t = _dot(h, w1_ref[...])
        t = jnp.square(jnp.maximum(t, 0.0)).astype(BF16)
        out = x1 + m[5:6] * _dot(t, w2_ref[...])
        if final:
            ms = jnp.mean(out * out, axis=-1, keepdims=True)
            out = out * lax.rsqrt(ms + EPS) * fg_ref[...]
        o_ref[rows, :] = out


def _mlp_resident_body(final, x_ref, ya_ref, yb_ref, wo_ref, m_ref, g_ref, fg_ref, w1_ref, w2_ref, o_ref):
    _mlp_tile(final, x_ref, ya_ref, yb_ref, m_ref, g_ref, fg_ref, o_ref, wo_ref, w1_ref, w2_ref)


def _mlp_cast_body(final, emit, x_ref, ya_ref, yb_ref, wo_f, m_ref, g_ref, fg_ref, w1_f, w2_f, o_ref, *rest):
    wo_ref, w1_ref, w2_ref = rest[-3:]
    emitted = rest[:-3]
    i = pl.program_id(0)

    @pl.when(i < N_CAST)
    def _():
        for k, (f_ref, b_ref) in enumerate(((wo_f, wo_ref), (w1_f, w1_ref), (w2_f, w2_ref))):
            n = f_ref.shape[0]
            slab = f_ref[...].astype(BF16)
            b_ref[pl.ds(pl.multiple_of(i * n, n), n), :] = slab
            if emit:
                emitted[k][...] = slab

    @pl.when(i >= N_CAST)
    def _():
        _mlp_tile(final, x_ref, ya_ref, yb_ref, m_ref, g_ref, fg_ref, o_ref, wo_ref, w1_ref, w2_ref)


def _resident(shape):
    zeros = (0,) * len(shape)
    return pl.BlockSpec(shape, lambda i: zeros, pipeline_mode=pl.Buffered(1))


def _mlp_call(x, x_off, ya, ya_off, ya_col, yb, yb_off, yb_col, n_tiles, tile0, mods, layer, g2, final_g, final,
              f32_weights=None, bf16_weights=None, emit=False):
    tm = TM_MLP
    lead = N_CAST if f32_weights is not None else 0

    def tile(i):
        return jnp.maximum(i - lead, 0)

    stream_specs = [
        pl.BlockSpec((tm, D_MODEL), lambda i: (tile(i) + x_off, 0)),
        pl.BlockSpec((tm, HALF), lambda i: (tile(i) + ya_off, ya_col)),
        pl.BlockSpec((tm, HALF), lambda i: (tile(i) + yb_off, yb_col)),
    ]
    mod_spec = pl.BlockSpec((None, None, 6, D_MODEL), lambda i: (layer, _cond_row(tile(i) + tile0, tm), 0, 0))
    out_spec = pl.BlockSpec((tm, D_MODEL), lambda i: (tile(i), 0))
    out_shape = jax.ShapeDtypeStruct((n_tiles * tm, D_MODEL), F32)
    vecs = (g2.reshape(1, D_MODEL), final_g.reshape(1, D_MODEL))
    if f32_weights is None:
        w_out, w1, w2 = bf16_weights
        return pl.pallas_call(
            functools.partial(_mlp_resident_body, final),
            grid=(n_tiles,),
            in_specs=stream_specs + [_resident((D_MODEL, D_MODEL)), mod_spec, _resident((1, D_MODEL)),
                                     _resident((1, D_MODEL)), _resident((D_MODEL, D_FF)), _resident((D_FF, D_MODEL))],
            out_specs=out_spec,
            out_shape=out_shape,
            compiler_params=_cparams(("parallel",)),
            name="mlp",
        )(x, ya, yb, w_out, mods, *vecs, w1, w2)

    wo_stack, wo_idx, w1_stack, w2_stack = f32_weights

    def slab(i):
        return jnp.minimum(i, N_CAST - 1)

    shapes = ((D_MODEL, D_MODEL), (D_MODEL, D_FF), (D_FF, D_MODEL))
    slab_rows = [rows // N_CAST for rows, _ in shapes]
    f32_specs = [pl.BlockSpec((None, n, cols), lambda i, idx=idx: (idx, slab(i), 0))
                 for n, (_, cols), idx in zip(slab_rows, shapes, (wo_idx, layer, layer))]
    out_specs, out_shapes = [out_spec], [out_shape]
    if emit:
        out_specs += [pl.BlockSpec((n, cols), lambda i: (slab(i), 0)) for n, (_, cols) in zip(slab_rows, shapes)]
        out_shapes += [jax.ShapeDtypeStruct(shp, BF16) for shp in shapes]
    return pl.pallas_call(
        functools.partial(_mlp_cast_body, final, emit),
        grid=(N_CAST + n_tiles,),
        in_specs=stream_specs + [f32_specs[0], mod_spec, _resident((1, D_MODEL)), _resident((1, D_MODEL)),
                                 f32_specs[1], f32_specs[2]],
        out_specs=out_specs,
        out_shape=out_shapes,
        scratch_shapes=[pltpu.VMEM(shp, BF16) for shp in shapes],
        compiler_params=_cparams(("arbitrary",)),
        name="mlp_cast",
    )(x, ya, yb, wo_stack, mods, *vecs, w1_stack, w2_stack)


def _mlp(x, ya, ya_col, yb, yb_col, w_out_stack, w_out_idx, mods, layer, g2, final_g, w1_stack, w2_stack, final):
    tm = TM_MLP
    npt = N_PROMPT_TOK // tm
    f32_weights = (w_out_stack, w_out_idx, w1_stack, w2_stack)
    common = (mods, layer, g2, final_g, final)
    if not final:
        if isinstance(ya, tuple):
            ya = jnp.concatenate(ya, axis=0)
        return tuple(_mlp_call(x, 0, ya, 0, ya_col, yb, 0, yb_col, N_TOK // tm, 0, *common, f32_weights=f32_weights))
    ya_p, ya_s = ((ya[0], 0), (ya[1], 0)) if isinstance(ya, tuple) else ((ya, 0), (ya, npt))
    y_p, *bf16_weights = _mlp_call(x, 0, ya_p[0], ya_p[1], ya_col, yb, 0, yb_col, npt, 0, *common,
                                   f32_weights=f32_weights, emit=True)
    y_s = _mlp_call(x, npt, ya_s[0], ya_s[1], ya_col, yb, npt, yb_col, N_TOK // tm - npt, npt, *common,
                    bf16_weights=bf16_weights)
    return y_p, y_s


CTX_BPS = 2


def _ctx_attn_body(q_ref, k_ref, v_ref, o_ref, ck_ref, cv_ref):
    for n in range(CTX_BPS):
        rows = pl.ds(n * SEQ, SEQ)
        _ctx_attn_seq(q_ref.at[rows], k_ref.at[rows], v_ref.at[rows], o_ref.at[rows], ck_ref.at[n], cv_ref.at[n])


def _ctx_attn_seq(q_ref, k_ref, v_ref, o_ref, ck_ref, cv_ref):
    lane = lax.broadcasted_iota(jnp.int32, (1, 2 * HEAD_DIM), 1)
    own = [lane < HEAD_DIM, lane >= HEAD_DIM]
    for h in range(N_HEADS):
        sl = slice(h * HEAD_DIM, (h + 1) * HEAD_DIM)
        ck_ref[h] = k_ref[:, sl].astype(F32)
        cv_ref[h] = v_ref[:, sl].astype(F32)
    for hp in range(N_HEADS // 2):
        sl = slice(hp * 2 * HEAD_DIM, (hp + 1) * 2 * HEAD_DIM)
        q = q_ref[:, sl] * jnp.asarray(HEAD_DIM ** -0.5, BF16)
        k = k_ref[:, sl]
        v = v_ref[:, sl]
        q2 = jnp.concatenate([jnp.where(own[0], q, jnp.zeros_like(q)),
                              jnp.where(own[1], q, jnp.zeros_like(q))], axis=0)
        s = _dot_nt(q2, k)
        m = jnp.max(s, axis=-1, keepdims=True)
        p = jnp.exp(s - m).astype(BF16)
        out = None
        for hh in range(2):
            num = _dot(p[hh * SEQ:(hh + 1) * SEQ], jnp.where(own[hh], v, jnp.ones_like(v)))
            den = jnp.where(own[hh], pltpu.roll(num, HEAD_DIM, 1), 1.0)
            o = num / den
            out = o if out is None else jnp.where(own[0], out, o)
        o_ref[:, sl] = out.astype(BF16)


def _ctx_attn(proj):
    return pl.pallas_call(
        _ctx_attn_body,
        grid=(BATCH // CTX_BPS,),
        in_specs=[
            pl.BlockSpec((CTX_BPS * SEQ, HALF), lambda b: (b, 0)),
            pl.BlockSpec((CTX_BPS * SEQ, HALF), lambda b: (b, 1)),
            pl.BlockSpec((CTX_BPS * SEQ, HALF), lambda b: (b, 2)),
        ],
        out_specs=[
            pl.BlockSpec((CTX_BPS * SEQ, HALF), lambda b: (b, 0)),
            pl.BlockSpec((CTX_BPS, None, N_HEADS, SEQ, HEAD_DIM), lambda b: (b, 0, 0, 0, 0)),
            pl.BlockSpec((CTX_BPS, None, N_HEADS, SEQ, HEAD_DIM), lambda b: (b, 0, 0, 0, 0)),
        ],
        out_shape=[
            jax.ShapeDtypeStruct((N_PROMPT_TOK, HALF), BF16),
            jax.ShapeDtypeStruct((BATCH, 1, N_HEADS, SEQ, HEAD_DIM), F32),
            jax.ShapeDtypeStruct((BATCH, 1, N_HEADS, SEQ, HEAD_DIM), F32),
        ],
        compiler_params=_cparams(("parallel",)),
        name="ctx_attn",
    )(proj, proj, proj)


NA_MASKED = -math.inf
NA_QROWS = 4
NA_KROWS = NA_QROWS + WIN_H
NA_NQ = NA_QROWS * GRID_W
NA_NK = NA_KROWS * GRID_W
NA_BLOCKS = GRID_ROWS // NA_QROWS
NA_TYPES = 3


def _na_col_table(rpb):
    qc = jnp.arange(GRID_W)[:, None]
    kc = jnp.arange(GRID_W)[None, :]
    c0 = jnp.clip(qc - WIN_W // 2, 0, GRID_W - WIN_W)
    valid = (kc >= c0) & (kc < c0 + WIN_W)
    coff = jnp.clip(kc - qc, -(WIN_W - 1), WIN_W - 1) + (WIN_W - 1)
    onehot = (coff.reshape(1, -1) == jnp.arange(2 * WIN_W - 1)[:, None]).astype(F32)
    t = jnp.dot(rpb.astype(F32).reshape(N_HEADS * (2 * WIN_H - 1), 2 * WIN_W - 1), onehot,
                precision=lax.Precision.HIGHEST)
    t = t.reshape(N_HEADS, 2 * WIN_H - 1, GRID_W, GRID_W)
    return jnp.where(valid[None, None], t, NA_MASKED)


def _na_window_start(rb):
    return min(max(NA_QROWS * rb - WIN_H // 2, 0), GRID_ROWS - NA_KROWS)


def _na_body(q_ref, k_ref, v_ref, ck_ref, cv_ref, tab_ref, o_ref, bias_ref, kc_ref, vc_ref, va_ref):
    @pl.when(pl.program_id(1) == 0)
    def _():
        masked = jnp.full((GRID_W, GRID_W), NA_MASKED, F32)
        for hh in range(2):
            for ty in range(NA_TYPES):
                q0 = (0, NA_QROWS, GRID_ROWS - NA_QROWS)[ty]
                ks = (0, 0, GRID_ROWS - NA_KROWS)[ty]
                for a in range(NA_QROWS):
                    r0 = min(max(q0 + a - WIN_H // 2, 0), GRID_ROWS - WIN_H)
                    for y in range(NA_KROWS):
                        kr = ks + y
                        ok = r0 <= kr < r0 + WIN_H
                        blk = tab_ref[hh, kr - (q0 + a) + WIN_H - 1] if ok else masked
                        bias_ref[ty, hh * NA_NQ + a * GRID_W:hh * NA_NQ + (a + 1) * GRID_W,
                                 y * GRID_W:(y + 1) * GRID_W] = blk

    lane = lax.broadcasted_iota(jnp.int32, (1, 2 * HEAD_DIM), 1)
    own = [lane < HEAD_DIM, lane >= HEAD_DIM]
    kc_ref[...] = jnp.concatenate([ck_ref[0], ck_ref[1]], axis=1).astype(BF16)
    vc = jnp.concatenate([cv_ref[0], cv_ref[1]], axis=1)
    v = v_ref[...]
    for hh in range(2):
        vc_ref[hh] = jnp.where(own[hh], vc, 1.0).astype(BF16)
        va_ref[hh] = jnp.where(own[hh], v, jnp.ones_like(v))

    def block(rb, ty):
        qs = rb * NA_NQ
        ks = _na_window_start(rb) * GRID_W
        q = q_ref[pl.ds(qs, NA_NQ), :] * jnp.asarray(HEAD_DIM ** -0.5, BF16)
        q2 = jnp.concatenate([jnp.where(own[0], q, jnp.zeros_like(q)),
                              jnp.where(own[1], q, jnp.zeros_like(q))], axis=0)
        s_loc = _dot_nt(q2, k_ref[pl.ds(ks, NA_NK), :]) + bias_ref[ty]
        s_ctx = _dot_nt(q2, kc_ref[...])
        m = jnp.maximum(jnp.max(s_loc, axis=-1, keepdims=True), jnp.max(s_ctx, axis=-1, keepdims=True))
        p_loc = jnp.exp(s_loc - m).astype(BF16)
        p_ctx = jnp.exp(s_ctx - m).astype(BF16)
        out = None
        for hh in range(2):
            rows = slice(hh * NA_NQ, (hh + 1) * NA_NQ)
            num = _dot(p_loc[rows], va_ref[hh, pl.ds(ks, NA_NK), :]) + _dot(p_ctx[rows], vc_ref[hh])
            den = jnp.where(own[hh], pltpu.roll(num, HEAD_DIM, 1), 1.0)
            o = num / den
            out = o if out is None else jnp.where(own[0], out, o)
        o_ref[pl.ds(qs, NA_NQ), :] = out.astype(BF16)

    for rb in range(NA_BLOCKS):
        block(rb, 0 if rb == 0 else (2 if rb == NA_BLOCKS - 1 else 1))


def _na_attn(proj, cache_k, cache_v, col_table, layer_idx):
    row_blk0 = N_PROMPT_TOK // DEC_SEQ
    hp = N_HEADS // 2
    return pl.pallas_call(
        _na_body,
        grid=(hp, DEC_BATCH),
        in_specs=[
            pl.BlockSpec((DEC_SEQ, 128), lambda h, b: (row_blk0 + b, h)),
            pl.BlockSpec((DEC_SEQ, 128), lambda h, b: (row_blk0 + b, hp + h)),
            pl.BlockSpec((DEC_SEQ, 128), lambda h, b: (row_blk0 + b, 2 * hp + h)),
            pl.BlockSpec((None, None, 2, PAST_LEN, HEAD_DIM), lambda h, b: (b, layer_idx, h, 0, 0)),
            pl.BlockSpec((None, None, 2, PAST_LEN, HEAD_DIM), lambda h, b: (b, layer_idx, h, 0, 0)),
            pl.BlockSpec((2, 2 * WIN_H - 1, GRID_W, GRID_W), lambda h, b: (h, 0, 0, 0)),
        ],
        out_specs=pl.BlockSpec((DEC_SEQ, 128), lambda h, b: (b, h)),
        out_shape=jax.ShapeDtypeStruct((N_SAMPLE_TOK, HALF), BF16),
        scratch_shapes=[pltpu.VMEM((NA_TYPES, 2 * NA_NQ, NA_NK), F32),
                        pltpu.VMEM((PAST_LEN, 2 * HEAD_DIM), BF16),
                        pltpu.VMEM((2, PAST_LEN, 2 * HEAD_DIM), BF16),
                        pltpu.VMEM((2, DEC_SEQ, 2 * HEAD_DIM), BF16)],
        compiler_params=_cparams(("parallel", "arbitrary")),
        name="na_attn",
    )(proj, proj, proj, cache_k, cache_v, col_table)


N_POW = 24


PREP_PPS = 2


def _s5_prep_body(*refs):
    for n in range(PREP_PPS):
        _s5_prep_pair(PREP_PPS * pl.program_id(0) + n, *[r.at[n] for r in refs])


def _s5_prep_pair(p, vec_ref, bt_ref, c_ref, wend_ref, ktoe_ref, woutT_ref, apow_ref, pw_ref, pct_ref):
    L = S5_CHUNK
    lane_blk = lax.broadcasted_iota(jnp.int32, (S5_CH, S5_W), 1) // S5_CH
    src_blk = (lane_blk + L - p) % L
    krows = []
    for d in range(2):
        v = vec_ref[d]
        lre, lim, step = v[0:1], v[1:2], v[2:3]
        a = lre * step
        b = lim * step
        ea = jnp.exp(a)
        nr = ea * jnp.cos(b) - 1.0
        ni = ea * jnp.sin(b)
        den = lre * lre + lim * lim
        qr = (nr * lre + ni * lim) / den
        qi = (ni * lre - nr * lim) / den
        bre, bim = bt_ref[d, 0], bt_ref[d, 1]
        bqr = bre * qr - bim * qi
        bqi = bre * qi + bim * qr
        cre, cim = c_ref[d, 0], c_ref[d, 1]

        e = lax.broadcasted_iota(jnp.int32, (N_POW, 1), 0).astype(F32)
        mag = jnp.exp(e * a)
        pw_ref[0] = mag * jnp.cos(e * b)
        pw_ref[1] = mag * jnp.sin(e * b)
        mag = jnp.exp((L * e) * a)
        apow_ref[d, 0] = mag * jnp.cos((L * e) * b)
        apow_ref[d, 1] = mag * jnp.sin((L * e) * b)

        def power(k):
            return pw_ref[0, k:k + 1, :], pw_ref[1, k:k + 1, :]

        def c_block(k):
            pr, pi = power(k)
            return cre * pr - cim * pi, -(cre * pi + cim * pr)

        for i in range(L):
            rows = pl.ds(pl.multiple_of(((i + p) % L) * S5_CH, S5_CH), S5_CH)
            pr, pi = power(L - 1 - i if d == 0 else i)
            wend_ref[rows, d * 256:d * 256 + 128] = (bqr * pr - bqi * pi).astype(BF16)
            wend_ref[rows, d * 256 + 128:d * 256 + 256] = (bqr * pi + bqi * pr).astype(BF16)
            mre, mim = c_block(i + 1 if d == 0 else L - i)
            woutT_ref[rows, d * 256:d * 256 + 128] = mre.astype(BF16)
            woutT_ref[rows, d * 256 + 128:d * 256 + 256] = mim.astype(BF16)
            lag_rows = slice(i * S5_CH, (i + 1) * S5_CH)
            mre, mim = c_block(i if d == 0 else L - 1 - i)
            pct_ref[lag_rows, 0:128] = mre
            pct_ref[lag_rows, 128:256] = mim

        bqcat = jnp.concatenate([bqr, bqi], axis=1)
        krows.append(_dot_nt(bqcat, pct_ref[...], precision=lax.Precision.HIGHEST))

    base_f = pltpu.roll(krows[0], p * S5_CH, 1)
    base_b = pltpu.roll(krows[1], p * S5_CH, 1)
    for i in range(L):
        rows = pl.ds(pl.multiple_of(((i + p) % L) * S5_CH, S5_CH), S5_CH)
        blk_f = base_f if i == 0 else pltpu.roll(base_f, i * S5_CH, 1)
        blk_b = base_b if (i + 1) % L == 0 else pltpu.roll(base_b, ((i + 1) % L) * S5_CH, 1)
        blk = jnp.where(src_blk >= i, blk_f, 0.0) + jnp.where(src_blk <= i, blk_b, 0.0)
        ktoe_ref[rows, :] = blk.astype(BF16)


def _s5_prep(lam_re, lam_im, log_step, b_re, b_im, c_re, c_im):
    def pair_vec(a):
        return a.astype(F32).reshape(2, S5_PAIRS, 1, S5_P).transpose(1, 0, 2, 3)

    step = jnp.broadcast_to(jnp.exp(log_step.astype(F32))[:, :, None], (2, N_SSM_GROUPS, SSM_STATE))
    vec = jnp.concatenate([pair_vec(lam_re), pair_vec(lam_im), pair_vec(step),
                           jnp.zeros((S5_PAIRS, 2, 5, S5_P), F32)], axis=2)

    def block_diag_ch_by_state(m):
        m = m.astype(F32).reshape(2, S5_PAIRS, 2, SSM_GROUP, SSM_STATE)
        eye = jnp.eye(2, dtype=F32)
        out = m[:, :, :, :, None, :] * eye[None, None, :, None, :, None]
        return out.reshape(2, S5_PAIRS, S5_CH, S5_P).transpose(1, 0, 2, 3)

    bt = jnp.stack([block_diag_ch_by_state(b_re.transpose(0, 1, 3, 2)),
                    block_diag_ch_by_state(b_im.transpose(0, 1, 3, 2))], axis=2)
    cc = jnp.stack([block_diag_ch_by_state(c_re), block_diag_ch_by_state(c_im)], axis=2)

    mat = jax.ShapeDtypeStruct((S5_PAIRS, S5_W, S5_W), BF16)
    mat_spec = pl.BlockSpec((PREP_PPS, S5_W, S5_W), lambda p: (p, 0, 0))
    return pl.pallas_call(
        _s5_prep_body,
        grid=(S5_PAIRS // PREP_PPS,),
        in_specs=[
            pl.BlockSpec((PREP_PPS, 2, 8, S5_P), lambda p: (p, 0, 0, 0)),
            pl.BlockSpec((PREP_PPS, 2, 2, S5_CH, S5_P), lambda p: (p, 0, 0, 0, 0)),
            pl.BlockSpec((PREP_PPS, 2, 2, S5_CH, S5_P), lambda p: (p, 0, 0, 0, 0)),
        ],
        out_specs=[mat_spec, mat_spec, mat_spec,
                   pl.BlockSpec((PREP_PPS, 2, 2, N_POW, S5_P), lambda p: (p, 0, 0, 0, 0))],
        out_shape=[mat, mat, mat, jax.ShapeDtypeStruct((S5_PAIRS, 2, 2, N_POW, S5_P), F32)],
        scratch_shapes=[pltpu.VMEM((PREP_PPS, 2, N_POW, S5_P), F32), pltpu.VMEM((PREP_PPS, S5_W, 2 * S5_P), F32)],
        compiler_params=_cparams(("parallel",)),
        name="s5_prep",
    )(vec, bt, cc)


S5_SEG_ROWS = 256
S5_N_SEG = N_CHUNK_ROWS // S5_SEG_ROWS
S5_SUB = S5_SEG_ROWS // S5_CHUNK
S5_SUB_PER_SEQ = (DEC_SEQ // S5_CHUNK) // S5_CHUNK


S5_PPS = 4


def _s5_body(tu_ref, s0_ref, wend_ref, ktoe_ref, woutT_ref, apow_ref, ty_ref, fin_ref, z_ref, xs_ref, g_ref):
    seg = pl.program_id(0)
    L = S5_CHUNK
    lane_blk = lax.broadcasted_iota(jnp.int32, (S5_SEG_ROWS, 128), 1) // S5_CH
    pairs = [S5_PPS * pl.program_id(1) + n for n in range(S5_PPS)]

    s_rows, ends = [], []
    for n, p in enumerate(pairs):
        tiles = []
        for t in range(LANE_TILES):
            tile = None
            for k in range(128 // S5_CH):
                j = (LANE_TILES * t + k + L - p) % L
                blk = tu_ref[j, t]
                tile = blk if tile is None else jnp.where(lane_blk == k, blk, tile)
            tiles.append(tile)
        s = jnp.concatenate(tiles, axis=1)
        s_rows.append(s)
        z = _dot(s, wend_ref[n])
        for q in range(LANE_TILES):
            z_ref[n, q] = z[:, q * 128:(q + 1) * 128]

        pair_ends = []
        for d in range(2):
            are, aim = apow_ref[n, d, 0, 1:2, :], apow_ref[n, d, 1, 1:2, :]
            xr = jnp.zeros((S5_SUB, S5_P), F32)
            xi = jnp.zeros((S5_SUB, S5_P), F32)
            for c in (range(L) if d == 0 else reversed(range(L))):
                rows = pl.ds(c, S5_SUB, stride=L)
                xs_ref[n, 2 * d, rows, :] = xr
                xs_ref[n, 2 * d + 1, rows, :] = xi
                zr = z_ref[n, 2 * d, rows, :]
                zi = z_ref[n, 2 * d + 1, rows, :]
                xr, xi = are * xr - aim * xi + zr, are * xi + aim * xr + zi
            pair_ends.append((xr, xi))
            fin_ref[n, d, 0] = xr
            fin_ref[n, d, 1] = xi
        ends.append(pair_ends)

    @pl.when(seg > 0)
    def _():
        for n in range(S5_PPS):
            for d in range(2):
                er, ei = ends[n][d]
                a16r, a16i = apow_ref[n, d, 0, L:L + 1, :], apow_ref[n, d, 1, L:L + 1, :]
                for bb in range(S5_SUB // S5_SUB_PER_SEQ):
                    gr = s0_ref[n, bb, 2 * d:2 * d + 1, :]
                    gi = s0_ref[n, bb, 2 * d + 1:2 * d + 2, :]
                    order = range(S5_SUB_PER_SEQ) if d == 0 else reversed(range(S5_SUB_PER_SEQ))
                    prev = None
                    for sub in order:
                        row = bb * S5_SUB_PER_SEQ + sub
                        if prev is not None:
                            gr, gi = (a16r * gr - a16i * gi + er[prev:prev + 1, :],
                                      a16r * gi + a16i * gr + ei[prev:prev + 1, :])
                        g_ref[n, 0, row:row + 1, :] = gr
                        g_ref[n, 1, row:row + 1, :] = gi
                        prev = row
                if d == 0:
                    pr, pi = apow_ref[n, d, 0, 0:L, :], apow_ref[n, d, 1, 0:L, :]
                else:
                    pr = jnp.concatenate([apow_ref[n, d, 0, k:k + 1, :] for k in reversed(range(L))], axis=0)
                    pi = jnp.concatenate([apow_ref[n, d, 1, k:k + 1, :] for k in reversed(range(L))], axis=0)
                for sub in range(S5_SUB):
                    rows = slice(sub * L, (sub + 1) * L)
                    gr = g_ref[n, 0, sub:sub + 1, :]
                    gi = g_ref[n, 1, sub:sub + 1, :]
                    xs_ref[n, 2 * d, rows, :] = xs_ref[n, 2 * d, rows, :] + (pr * gr - pi * gi)
                    xs_ref[n, 2 * d + 1, rows, :] = xs_ref[n, 2 * d + 1, rows, :] + (pr * gi + pi * gr)

    @pl.when(pl.program_id(1) == 0)
    def _():
        ty_ref[...] = jnp.zeros_like(ty_ref)

    for n, p in enumerate(pairs):
        xs = jnp.concatenate([xs_ref[n, q].astype(BF16) for q in range(LANE_TILES)], axis=1)
        y = _dot(s_rows[n], ktoe_ref[n]) + _dot_nt(xs, woutT_ref[n])
        for t in range(LANE_TILES):
            yt = y[:, t * 128:(t + 1) * 128]
            for k in range(128 // S5_CH):
                j = (LANE_TILES * t + k + L - p) % L
                ty_ref[j, t] = jnp.where(lane_blk == k, yt, ty_ref[j, t])


def _s5_scan(tu, s0, wend, ktoe, woutT, apow):
    mat_spec = pl.BlockSpec((S5_PPS, S5_W, S5_W), lambda g, p: (p, 0, 0))
    seqs = S5_SUB // S5_SUB_PER_SEQ
    return pl.pallas_call(
        _s5_body,
        grid=(S5_N_SEG, S5_PAIRS // S5_PPS),
        in_specs=[
            pl.BlockSpec((S5_CHUNK, LANE_TILES, S5_SEG_ROWS, 128), lambda g, p: (0, 0, g, 0)),
            pl.BlockSpec((S5_PPS, seqs, 4, S5_P), lambda g, p: (p, jnp.maximum(g - 1, 0), 0, 0)),
            mat_spec, mat_spec, mat_spec,
            pl.BlockSpec((S5_PPS, 2, 2, N_POW, S5_P), lambda g, p: (p, 0, 0, 0, 0)),
        ],
        out_specs=[
            pl.BlockSpec((S5_CHUNK, LANE_TILES, S5_SEG_ROWS, 128), lambda g, p: (0, 0, g, 0)),
            pl.BlockSpec((S5_PPS, 2, 2, S5_SUB, S5_P), lambda g, p: (p, 0, 0, g, 0)),
        ],
        out_shape=[
            jax.ShapeDtypeStruct((S5_CHUNK, LANE_TILES, N_CHUNK_ROWS, 128), F32),
            jax.ShapeDtypeStruct((S5_PAIRS, 2, 2, S5_N_SEG * S5_SUB, S5_P), F32),
        ],
        scratch_shapes=[pltpu.VMEM((S5_PPS, LANE_TILES, S5_SEG_ROWS, 128), F32),
                        pltpu.VMEM((S5_PPS, LANE_TILES, S5_SEG_ROWS, 128), F32),
                        pltpu.VMEM((S5_PPS, 2, S5_SUB, S5_P), F32)],
        compiler_params=_cparams(("parallel", "arbitrary")),
        name="s5_scan",
    )(tu, s0, wend, ktoe, woutT, apow)


TM_GLU = 2048


def _glu_body(tu_ref, ty_ref, d_ref, gw_ref, gb_ref, o_ref, out_ref):
    rows = TM_GLU // S5_CHUNK
    d = d_ref[...]
    pre = []
    for j in range(S5_CHUNK):
        uj = jnp.concatenate([tu_ref[j, t].astype(F32) for t in range(LANE_TILES)], axis=1)
        yj = jnp.concatenate([ty_ref[j, t] for t in range(LANE_TILES)], axis=1)
        dj = pltpu.roll(d, j * S5_CH, 1) if j else d
        yj = dj * uj + yj
        back = ((S5_CHUNK - j) % S5_CHUNK) * S5_CH
        pre.append(pltpu.roll(yj, back, 1) if back else yj)
    y = jnp.concatenate(pre, axis=0)
    cdf = 0.5 * (1.0 + jnp.tanh(math.sqrt(2.0 / math.pi) * (y + 0.044715 * (y * y * y))))
    y = y * cdf
    gate = jax.nn.sigmoid(_dot(y.astype(BF16), gw_ref[...]) + gb_ref[...])
    out = y * gate
    for j in range(S5_CHUNK):
        oj = out[j * rows:(j + 1) * rows]
        for t in range(LANE_TILES):
            out_ref[t, pl.ds(j, rows, stride=S5_CHUNK), :] = oj[:, t * 128:(t + 1) * 128]
    o_ref[...] = jnp.concatenate([out_ref[t] for t in range(LANE_TILES)], axis=1).astype(BF16)


def _glu(tu, ty, ssm_d, glu_w, glu_b):
    tm = TM_GLU
    rows = tm // S5_CHUNK
    lay_spec = pl.BlockSpec((S5_CHUNK, LANE_TILES, rows, 128), lambda i: (0, 0, i, 0))
    return pl.pallas_call(
        _glu_body,
        grid=(N_TOK // tm,),
        in_specs=[
            lay_spec, lay_spec,
            pl.BlockSpec((1, HALF), lambda i: (0, 0)),
            pl.BlockSpec((HALF, HALF), lambda i: (0, 0)),
            pl.BlockSpec((1, HALF), lambda i: (0, 0)),
        ],
        out_specs=pl.BlockSpec((tm, HALF), lambda i: (i, 0)),
        out_shape=jax.ShapeDtypeStruct((N_TOK, HALF), BF16),
        scratch_shapes=[pltpu.VMEM((LANE_TILES, tm, 128), F32)],
        compiler_params=_cparams(("parallel",)),
        name="s5_glu",
    )(tu, ty, ssm_d.reshape(1, HALF), glu_w.astype(BF16), glu_b.reshape(1, HALF))


def kernel(x_prompt, x_sample, c, cache_k, cache_v, state_s5_re, state_s5_im, c_ctx, norm1_g, norm2_g, ada_w, ada_b, mlp_w1, mlp_w2, ab_w_in, pool_w, pool_scale, conv_w, ab_w_out, cd_w_in, na_rpb, ssm_lambda_re, ssm_lambda_im, ssm_log_step, ssm_b_re, ssm_b_im, ssm_c_re, ssm_c_im, ssm_d, glu_w, glu_b, cd_w_out, final_g):
    depth = ada_w.shape[0]
    xs = (x_prompt.reshape(N_PROMPT_TOK, D_MODEL), x_sample.reshape(N_SAMPLE_TOK, D_MODEL))
    cond = jnp.concatenate([c_ctx[None, :], c, jnp.zeros((N_COND - 1 - DEC_BATCH, D_MODEL), F32)], axis=0)
    mods = _adaln(cond, ada_w, ada_b).reshape(depth, N_COND, 6, D_MODEL)

    new_k, new_v, new_re, new_im = [], [], [], []
    for layer in range(depth):
        i = layer // 2
        final = layer == depth - 1
        if layer % 2 == 0:
            y, x = _even_layer_mix(xs, mods, layer, norm1_g[layer], ab_w_in, i,
                                   pool_w[i], pool_scale[i], conv_w[i])
            ya, ya_col, yb, yb_col = y, 0, y, 1
            w_out_stack = ab_w_out
        else:
            x = xs[0]
            qkv, tu = _inproj_odd(x, mods, layer, norm1_g[layer], cd_w_in, i)
            o_p, ck, cv = _ctx_attn(qkv)
            new_k.append(ck)
            new_v.append(cv)
            o_s = _na_attn(qkv, cache_k, cache_v, _na_col_table(na_rpb[i]), i)
            attn = (o_p, o_s)

            wend, ktoe, woutT, apow = _s5_prep(ssm_lambda_re[i], ssm_lambda_im[i], ssm_log_step[i],
                                               ssm_b_re[i], ssm_b_im[i], ssm_c_re[i], ssm_c_im[i])

            def pair_state(s):
                return s.astype(F32).reshape(DEC_BATCH, 2, S5_PAIRS, S5_P).transpose(2, 0, 1, 3)

            s0 = jnp.stack([pair_state(state_s5_re[:, i]), pair_state(state_s5_im[:, i])], axis=3)
            s0 = s0.reshape(S5_PAIRS, DEC_BATCH, 4, S5_P)
            ty, fin = _s5_scan(tu, s0, wend, ktoe, woutT, apow)
            fin = fin[:, :, :, :BATCH]
            new_re.append(fin[:, :, 0].transpose(2, 1, 0, 3).reshape(BATCH, 2, N_SSM_GROUPS, SSM_STATE))
            new_im.append(fin[:, :, 1].transpose(2, 1, 0, 3).reshape(BATCH, 2, N_SSM_GROUPS, SSM_STATE))
            d_out = _glu(tu, ty, ssm_d[i], glu_w[i], glu_b[i])
            ya, ya_col, yb, yb_col = attn, 0, d_out, 0
            w_out_stack = cd_w_out
        xs = _mlp(x, ya, ya_col, yb, yb_col, w_out_stack, i, mods, layer, norm2_g[layer], final_g,
                  mlp_w1, mlp_w2, final)

    y_prompt = xs[0].reshape(BATCH, SEQ, D_MODEL)
    y_sample = xs[1].reshape(DEC_BATCH, DEC_SEQ, D_MODEL)
    return (y_prompt, y_sample, jnp.concatenate(new_k, axis=1), jnp.concatenate(new_v, axis=1),
            jnp.stack(new_re, axis=1), jnp.stack(new_im, axis=1))
```

```python
import functools
import math

import jax
import jax.numpy as jnp
from jax import lax
from jax.experimental import pallas as pl
from jax.experimental.pallas import tpu as pltpu

F32 = jnp.float32
BF16 = jnp.bfloat16

D_MODEL = 1024
BATCH = 16
SEQ = 256
DEC_BATCH = 4
DEC_SEQ = 2048
PAST_LEN = 256
GRID_W = 64
GRID_ROWS = DEC_SEQ // GRID_W
HALF = 512
POOL_WINDOWS = (2, 4, 8, 16)
POOL_GROUP = 128
HEAD_DIM = 64
N_HEADS = 8
WIN_H = 8
WIN_W = 16
SSM_GROUP = 16
N_SSM_GROUPS = 32
SSM_STATE = 64
D_FF = 4096
EPS = 1e-6

N_PROMPT_TOK = BATCH * SEQ
N_SAMPLE_TOK = DEC_BATCH * DEC_SEQ
N_TOK = N_PROMPT_TOK + N_SAMPLE_TOK
N_COND = 8

S5_CHUNK = 16
S5_PAIRS = N_SSM_GROUPS // 2
S5_CH = 2 * SSM_GROUP
S5_P = 2 * SSM_STATE
S5_W = S5_CHUNK * S5_CH

VMEM_LIMIT = 56 * 1024 * 1024


def _cparams(sem):
    return pltpu.CompilerParams(dimension_semantics=sem, vmem_limit_bytes=VMEM_LIMIT)


def _cond_row(i, tm):
    npt = N_PROMPT_TOK // tm
    per = DEC_SEQ // tm
    return jnp.where(i < npt, 0, 1 + (i - npt) // per)


def _resident_layer(shape, idx):
    zeros = (0,) * len(shape)
    return pl.BlockSpec((None,) + tuple(shape), lambda i: (idx,) + zeros, pipeline_mode=pl.Buffered(1))


def _normmod(x, g, shift, scale):
    ms = jnp.mean(x * x, axis=-1, keepdims=True)
    return (x * lax.rsqrt(ms + EPS) * g) * (1.0 + scale) + shift


def _dot(a, b):
    return jnp.dot(a, b, preferred_element_type=F32)


def _dot_nt(a, b, precision=None):
    return lax.dot_general(a, b, (((1,), (1,)), ((), ())), preferred_element_type=F32, precision=precision)


def _adaln_body(c_ref, w_ref, b_ref, o_ref):
    c = c_ref[...]
    s = c * jax.nn.sigmoid(c)
    o_ref[...] = _dot(s.astype(BF16), w_ref[...].astype(BF16)) + b_ref[...]


def _adaln(cond, ada_w, ada_b):
    depth = ada_w.shape[0]
    nj = 6
    return pl.pallas_call(
        _adaln_body,
        grid=(depth, nj),
        in_specs=[
            pl.BlockSpec((N_COND, D_MODEL), lambda l, j: (0, 0)),
            pl.BlockSpec((None, D_MODEL, D_MODEL), lambda l, j: (l, 0, j)),
            pl.BlockSpec((None, 1, D_MODEL), lambda l, j: (l, 0, j)),
        ],
        out_specs=pl.BlockSpec((None, N_COND, D_MODEL), lambda l, j: (l, 0, j)),
        out_shape=jax.ShapeDtypeStruct((depth, N_COND, 6 * D_MODEL), F32),
        compiler_params=_cparams(("arbitrary", "arbitrary")),
        name="adaln",
    )(cond, ada_w, ada_b.reshape(depth, 1, 6 * D_MODEL))


TM_PROJ = 1024


def _stream_tile(x_refs, tm):
    if len(x_refs) == 1:
        return x_refs[0][...]
    is_prompt = pl.program_id(0) < N_PROMPT_TOK // tm
    return jnp.where(is_prompt, x_refs[0][...], x_refs[1][...])


N_CHUNK_ROWS = N_TOK // S5_CHUNK
LANE_TILES = S5_W // 128


def _inproj_odd_body(x_ref, m_ref, g_ref, w_ref, qkv_ref, tu_ref, u_ref, wbf_ref):
    @pl.when(pl.program_id(0) == 0)
    def _():
        wbf_ref[...] = w_ref[...].astype(BF16)

    m = m_ref[...]
    h = _normmod(x_ref[...], g_ref[...], m[0:1], m[1:2])
    acc = _dot(h.astype(BF16), wbf_ref[...])
    qkv_ref[...] = acc[:, 0:3 * HALF].astype(BF16)
    for t in range(LANE_TILES):
        u_ref[t] = acc[:, 3 * HALF + t * 128:3 * HALF + (t + 1) * 128]
    rows = TM_PROJ // S5_CHUNK
    for j in range(S5_CHUNK):
        uj = jnp.concatenate([u_ref[t, pl.ds(j, rows, stride=S5_CHUNK), :] for t in range(LANE_TILES)], axis=1)
        if j:
            uj = pltpu.roll(uj, j * S5_CH, 1)
        for t in range(LANE_TILES):
            tu_ref[j, t] = uj[:, t * 128:(t + 1) * 128].astype(BF16)


def _inproj_odd(x, mods, layer, g, w_stack, w_idx):
    tm = TM_PROJ
    rows = tm // S5_CHUNK
    return pl.pallas_call(
        _inproj_odd_body,
        grid=(N_TOK // tm,),
        in_specs=[
            pl.BlockSpec((tm, D_MODEL), lambda i: (i, 0)),
            pl.BlockSpec((None, None, 6, D_MODEL), lambda i: (layer, _cond_row(i, tm), 0, 0)),
            pl.BlockSpec((1, D_MODEL), lambda i: (0, 0)),
            _resident_layer((D_MODEL, 4 * HALF), w_idx),
        ],
        out_specs=[
            pl.BlockSpec((tm, 3 * HALF), lambda i: (i, 0)),
            pl.BlockSpec((S5_CHUNK, LANE_TILES, rows, 128), lambda i: (0, 0, i, 0)),
        ],
        out_shape=[
            jax.ShapeDtypeStruct((N_TOK, 3 * HALF), BF16),
            jax.ShapeDtypeStruct((S5_CHUNK, LANE_TILES, N_CHUNK_ROWS, 128), BF16),
        ],
        scratch_shapes=[pltpu.VMEM((LANE_TILES, tm, 128), F32), pltpu.VMEM((D_MODEL, 4 * HALF), BF16)],
        compiler_params=_cparams(("arbitrary",)),
        name="inproj_odd",
    )(x, mods, g.reshape(1, D_MODEL), w_stack)


TM_MIX = 256
EVEN_TPS = 2
HALO = 16
RING = 6
LAG = 2
PAD_ROWS = TM_MIX + 2 * HALO + 8


def _even_mix_tile(t, prev_rows, cur_slot, next_slot, out_rows, proj_ref, pw_ref, ps_ref, cw_ref, y_ref, pieces):
    n_prompt_tiles = N_PROMPT_TOK // TM_MIX
    tiles_per_seq = DEC_SEQ // TM_MIX
    is_prompt = t < n_prompt_tiles
    tile_in_seq = jnp.where(is_prompt, 0, (t - n_prompt_tiles) % tiles_per_seq)
    seq_len = jnp.where(is_prompt, SEQ, DEC_SEQ)
    has_prev = tile_in_seq > 0
    has_next = jnp.logical_and(jnp.logical_not(is_prompt), tile_in_seq < tiles_per_seq - 1)
    prev_keep = jnp.where(has_prev, 1.0, 0.0).astype(F32)
    next_keep = jnp.where(has_next, 1.0, 0.0).astype(F32)
    pos = tile_in_seq * TM_MIX + lax.broadcasted_iota(jnp.int32, (TM_MIX, 1), 0)
    lo_rows, hi_rows = HALO, HALO + TM_MIX
    head = slice(0, HALO)

    slack = jnp.zeros((PAD_ROWS - TM_MIX - 2 * HALO, POOL_GROUP), F32)
    for gi, w in enumerate(POOL_WINDOWS):
        pieces.pop(0)()
        cs = slice(gi * POOL_GROUP, (gi + 1) * POOL_GROUP)
        cur = proj_ref[cur_slot, :, cs]
        f = jnp.concatenate([prev_rows(cs) * prev_keep, cur, proj_ref[next_slot, head, cs] * next_keep, slack],
                            axis=0)
        n_rows, span = PAD_ROWS, 1
        while 2 * span < w:
            n_rows -= 8
            f = f[0:n_rows] + f[span:span + n_rows]
            span *= 2
        s = f[lo_rows - span:hi_rows - span] + f[lo_rows:hi_rows]
        lo = jnp.maximum(pos - w // 2, 0)
        hi = jnp.minimum(pos + (w - w // 2), seq_len)
        inv = 1.0 / (hi - lo).astype(F32)
        pooled = s * inv - cur
        mixed = _dot(pooled.astype(BF16), pw_ref[gi]) * ps_ref[:, cs]
        y_ref[out_rows, cs] = mixed.astype(BF16)

    for ci in range(HALF // 128):
        pieces.pop(0)()
        cs = slice(ci * 128, (ci + 1) * 128)
        bs = slice(HALF + ci * 128, HALF + (ci + 1) * 128)
        gs = slice(2 * HALF + ci * 128, 2 * HALF + (ci + 1) * 128)
        vs = slice(3 * HALF + ci * 128, 3 * HALF + (ci + 1) * 128)
        z = jnp.concatenate([(prev_rows(gs) * prev_rows(vs)) * prev_keep,
                             proj_ref[cur_slot, :, gs] * proj_ref[cur_slot, :, vs],
                             (proj_ref[next_slot, head, gs] * proj_ref[next_slot, head, vs]) * next_keep], axis=0)
        conv = (cw_ref[0:1, cs] * z[lo_rows - 1:hi_rows - 1]
                + cw_ref[1:2, cs] * z[lo_rows:hi_rows]
                + cw_ref[2:3, cs] * z[lo_rows + 1:hi_rows + 1])
        y_ref[out_rows, bs] = (proj_ref[cur_slot, :, bs] * conv).astype(BF16)


def _even_step(phase, x_refs, m_ref, g_ref, w_ref, pw_ref, ps_ref, cw_ref, y_ref, xcat_ref, proj_ref, tail_ref,
               h_ref):
    i = pl.program_id(0)
    write0 = EVEN_TPS * phase
    slot_a, slot_b, slot_c = (write0 + 2) % RING, (write0 + 3) % RING, (write0 + 4) % RING
    tail = slice(TM_MIX - HALO, TM_MIX)

    m = m_ref[...]
    x = _stream_tile(x_refs, EVEN_TPS * TM_MIX)
    xcat_ref[...] = x
    h_ref[...] = _normmod(x, g_ref[...], m[0:1], m[1:2]).astype(BF16)
    n_pieces = EVEN_TPS * (len(POOL_WINDOWS) + HALF // 128)
    pw = (4 * HALF) // n_pieces

    def project(piece):
        cols = slice(piece * pw, (piece + 1) * pw)
        d = _dot(h_ref[...], w_ref[:, cols])
        for n in range(EVEN_TPS):
            proj_ref[write0 + n, :, cols] = d[n * TM_MIX:(n + 1) * TM_MIX]

    pieces = [functools.partial(project, k) for k in range(n_pieces)]

    t0 = EVEN_TPS * jnp.maximum(i - LAG, 0)
    _even_mix_tile(t0, lambda cols: tail_ref[:, cols], slot_a, slot_b, slice(0, TM_MIX),
                   proj_ref, pw_ref, ps_ref, cw_ref, y_ref, pieces)
    _even_mix_tile(t0 + 1, lambda cols: proj_ref[slot_a, tail, cols], slot_b, slot_c, slice(TM_MIX, 2 * TM_MIX),
                   proj_ref, pw_ref, ps_ref, cw_ref, y_ref, pieces)

    tail_ref[...] = proj_ref[slot_b, tail, :]


def _even_body(n_x, *refs):
    x_refs, rest = refs[:n_x], refs[n_x:]
    m_ref, g_ref, w_ref = rest[:3]
    proj_ref, tail_ref, h_ref, wbf_ref = rest[-4:]
    i = pl.program_id(0)
    n_phases = RING // EVEN_TPS

    @pl.when(i == 0)
    def _():
        proj_ref[...] = jnp.zeros_like(proj_ref)
        tail_ref[...] = jnp.zeros_like(tail_ref)
        wbf_ref[...] = w_ref[...].astype(BF16)

    for phase in range(n_phases):
        @pl.when(i % n_phases == phase)
        def _(phase=phase):
            _even_step(phase, x_refs, m_ref, g_ref, wbf_ref, *rest[3:-1])


def _even_layer_mix(xs, mods, layer, g, w_stack, w_idx, pool_w, pool_scale, conv_w):
    tm = EVEN_TPS * TM_MIX
    n_steps = N_TOK // tm
    npt = N_PROMPT_TOK // tm
    last = n_steps - 1
    if len(xs) == 1:
        x_specs = [pl.BlockSpec((tm, D_MODEL), lambda i: (jnp.minimum(i, last), 0))]
    else:
        x_specs = [pl.BlockSpec((tm, D_MODEL), lambda i: (jnp.minimum(i, npt - 1), 0)),
                   pl.BlockSpec((tm, D_MODEL), lambda i: (jnp.clip(i - npt, 0, last - npt), 0))]
    return pl.pallas_call(
        functools.partial(_even_body, len(xs)),
        grid=(n_steps + LAG,),
        in_specs=x_specs + [
            pl.BlockSpec((None, None, 6, D_MODEL), lambda i: (layer, _cond_row(jnp.minimum(i, last), tm), 0, 0)),
            pl.BlockSpec((1, D_MODEL), lambda i: (0, 0)),
            _resident_layer((D_MODEL, 4 * HALF), w_idx),
            pl.BlockSpec((len(POOL_WINDOWS), POOL_GROUP, POOL_GROUP), lambda i: (0, 0, 0)),
            pl.BlockSpec((1, HALF), lambda i: (0, 0)),
            pl.BlockSpec((3, HALF), lambda i: (0, 0)),
        ],
        out_specs=[pl.BlockSpec((tm, 2 * HALF), lambda i: (jnp.maximum(i - LAG, 0), 0)),
                   pl.BlockSpec((tm, D_MODEL), lambda i: (jnp.minimum(i, last), 0))],
        out_shape=[jax.ShapeDtypeStruct((N_TOK, 2 * HALF), BF16), jax.ShapeDtypeStruct((N_TOK, D_MODEL), F32)],
        scratch_shapes=[pltpu.VMEM((RING, TM_MIX, 4 * HALF), F32), pltpu.VMEM((HALO, 4 * HALF), F32),
                        pltpu.VMEM((tm, D_MODEL), BF16), pltpu.VMEM((D_MODEL, 4 * HALF), BF16)],
        compiler_params=_cparams(("arbitrary",)),
        name="even_layer_mix",
    )(*xs, mods, g.reshape(1, D_MODEL), w_stack, pool_w.astype(BF16), pool_scale.reshape(1, HALF), conv_w)


TM_MLP = 1024
RC_MLP = 256
N_CAST = 16


def _mlp_tile(final, x_ref, ya_ref, yb_ref, m_ref, g_ref, fg_ref, o_ref, wo_ref, w1_ref, w2_ref):
    m = m_ref[...]
    gain = g_ref[...] * (1.0 + m[4:5])
    for r in range(TM_MLP // RC_MLP):
        rows = slice(r * RC_MLP, (r + 1) * RC_MLP)
        mix = _dot(ya_ref[rows, :], wo_ref[0:HALF, :]) + _dot(yb_ref[rows, :], wo_ref[HALF:2 * HALF, :])
        x1 = x_ref[rows, :] + m[2:3] * mix
        ms = jnp.mean(x1 * x1, axis=-1, keepdims=True)
        h = (x1 * lax.rsqrt(ms + EPS) * gain + m[3:4]).astype(BF16)
        t = _dot(h, w1_ref[...])
        t = jnp.square(jnp.maximum(t, 0.0)).astype(BF16)
        out = x1 + m[5:6] * _dot(t, w2_ref[...])
        if final:
            ms = jnp.mean(out * out, axis=-1, keepdims=True)
            out = out * lax.rsqrt(ms + EPS) * fg_ref[...]
        o_ref[rows, :] = out


def _mlp_resident_body(final, x_ref, ya_ref, yb_ref, wo_ref, m_ref, g_ref, fg_ref, w1_ref, w2_ref, o_ref):
    _mlp_tile(final, x_ref, ya_ref, yb_ref, m_ref, g_ref, fg_ref, o_ref, wo_ref, w1_ref, w2_ref)


def _mlp_cast_body(final, emit, x_ref, ya_ref, yb_ref, wo_f, m_ref, g_ref, fg_ref, w1_f, w2_f, o_ref, *rest):
    wo_ref, w1_ref, w2_ref = rest[-3:]
    emitted = rest[:-3]
    i = pl.program_id(0)

    @pl.when(i < N_CAST)
    def _():
        for k, (f_ref, b_ref) in enumerate(((wo_f, wo_ref), (w1_f, w1_ref), (w2_f, w2_ref))):
            n = f_ref.shape[0]
            slab = f_ref[...].astype(BF16)
            b_ref[pl.ds(pl.multiple_of(i * n, n), n), :] = slab
            if emit:
                emitted[k][...] = slab

    @pl.when(i >= N_CAST)
    def _():
        _mlp_tile(final, x_ref, ya_ref, yb_ref, m_ref, g_ref, fg_ref, o_ref, wo_ref, w1_ref, w2_ref)


def _resident(shape):
    zeros = (0,) * len(shape)
    return pl.BlockSpec(shape, lambda i: zeros, pipeline_mode=pl.Buffered(1))


def _mlp_call(x, x_off, ya, ya_off, ya_col, yb, yb_off, yb_col, n_tiles, tile0, mods, layer, g2, final_g, final,
              f32_weights=None, bf16_weights=None, emit=False):
    tm = TM_MLP
    lead = N_CAST if f32_weights is not None else 0

    def tile(i):
        return jnp.maximum(i - lead, 0)

    stream_specs = [
        pl.BlockSpec((tm, D_MODEL), lambda i: (tile(i) + x_off, 0)),
        pl.BlockSpec((tm, HALF), lambda i: (tile(i) + ya_off, ya_col)),
        pl.BlockSpec((tm, HALF), lambda i: (tile(i) + yb_off, yb_col)),
    ]
    mod_spec = pl.BlockSpec((None, None, 6, D_MODEL), lambda i: (layer, _cond_row(tile(i) + tile0, tm), 0, 0))
    out_spec = pl.BlockSpec((tm, D_MODEL), lambda i: (tile(i), 0))
    out_shape = jax.ShapeDtypeStruct((n_tiles * tm, D_MODEL), F32)
    vecs = (g2.reshape(1, D_MODEL), final_g.reshape(1, D_MODEL))
    if f32_weights is None:
        w_out, w1, w2 = bf16_weights
        return pl.pallas_call(
            functools.partial(_mlp_resident_body, final),
            grid=(n_tiles,),
            in_specs=stream_specs + [_resident((D_MODEL, D_MODEL)), mod_spec, _resident((1, D_MODEL)),
                                     _resident((1, D_MODEL)), _resident((D_MODEL, D_FF)), _resident((D_FF, D_MODEL))],
            out_specs=out_spec,
            out_shape=out_shape,
            compiler_params=_cparams(("parallel",)),
            name="mlp",
        )(x, ya, yb, w_out, mods, *vecs, w1, w2)

    wo_stack, wo_idx, w1_stack, w2_stack = f32_weights

    def slab(i):
        return jnp.minimum(i, N_CAST - 1)

    shapes = ((D_MODEL, D_MODEL), (D_MODEL, D_FF), (D_FF, D_MODEL))
    slab_rows = [rows // N_CAST for rows, _ in shapes]
    f32_specs = [pl.BlockSpec((None, n, cols), lambda i, idx=idx: (idx, slab(i), 0))
                 for n, (_, cols), idx in zip(slab_rows, shapes, (wo_idx, layer, layer))]
    out_specs, out_shapes = [out_spec], [out_shape]
    if emit:
        out_specs += [pl.BlockSpec((n, cols), lambda i: (slab(i), 0)) for n, (_, cols) in zip(slab_rows, shapes)]
        out_shapes += [jax.ShapeDtypeStruct(shp, BF16) for shp in shapes]
    return pl.pallas_call(
        functools.partial(_mlp_cast_body, final, emit),
        grid=(N_CAST + n_tiles,),
        in_specs=stream_specs + [f32_specs[0], mod_spec, _resident((1, D_MODEL)), _resident((1, D_MODEL)),
                                 f32_specs[1], f32_specs[2]],
        out_specs=out_specs,
        out_shape=out_shapes,
        scratch_shapes=[pltpu.VMEM(shp, BF16) for shp in shapes],
        compiler_params=_cparams(("arbitrary",)),
        name="mlp_cast",
    )(x, ya, yb, wo_stack, mods, *vecs, w1_stack, w2_stack)


def _mlp(x, ya, ya_col, yb, yb_col, w_out_stack, w_out_idx, mods, layer, g2, final_g, w1_stack, w2_stack, final):
    tm = TM_MLP
    npt = N_PROMPT_TOK // tm
    f32_weights = (w_out_stack, w_out_idx, w1_stack, w2_stack)
    common = (mods, layer, g2, final_g, final)
    if not final:
        if isinstance(ya, tuple):
            ya = jnp.concatenate(ya, axis=0)
        return tuple(_mlp_call(x, 0, ya, 0, ya_col, yb, 0, yb_col, N_TOK // tm, 0, *common, f32_weights=f32_weights))
    ya_p, ya_s = ((ya[0], 0), (ya[1], 0)) if isinstance(ya, tuple) else ((ya, 0), (ya, npt))
    y_p, *bf16_weights = _mlp_call(x, 0, ya_p[0], ya_p[1], ya_col, yb, 0, yb_col, npt, 0, *common,
                                   f32_weights=f32_weights, emit=True)
    y_s = _mlp_call(x, npt, ya_s[0], ya_s[1], ya_col, yb, npt, yb_col, N_TOK // tm - npt, npt, *common,
                    bf16_weights=bf16_weights)
    return y_p, y_s


def _ctx_attn_body(q_ref, k_ref, v_ref, o_ref, ck_ref, cv_ref):
    lane = lax.broadcasted_iota(jnp.int32, (1, 2 * HEAD_DIM), 1)
    own = [lane < HEAD_DIM, lane >= HEAD_DIM]
    for h in range(N_HEADS):
        sl = slice(h * HEAD_DIM, (h + 1) * HEAD_DIM)
        ck_ref[h] = k_ref[:, sl].astype(F32)
        cv_ref[h] = v_ref[:, sl].astype(F32)
    for hp in range(N_HEADS // 2):
        sl = slice(hp * 2 * HEAD_DIM, (hp + 1) * 2 * HEAD_DIM)
        q = q_ref[:, sl] * jnp.asarray(HEAD_DIM ** -0.5, BF16)
        k = k_ref[:, sl]
        v = v_ref[:, sl]
        q2 = jnp.concatenate([jnp.where(own[0], q, jnp.zeros_like(q)),
                              jnp.where(own[1], q, jnp.zeros_like(q))], axis=0)
        s = _dot_nt(q2, k)
        m = jnp.max(s, axis=-1, keepdims=True)
        p = jnp.exp(s - m).astype(BF16)
        out = None
        for hh in range(2):
            num = _dot(p[hh * SEQ:(hh + 1) * SEQ], jnp.where(own[hh], v, jnp.ones_like(v)))
            den = jnp.where(own[hh], pltpu.roll(num, HEAD_DIM, 1), 1.0)
            o = num / den
            out = o if out is None else jnp.where(own[0], out, o)
        o_ref[:, sl] = out.astype(BF16)


def _ctx_attn(proj):
    return pl.pallas_call(
        _ctx_attn_body,
        grid=(BATCH,),
        in_specs=[
            pl.BlockSpec((SEQ, HALF), lambda b: (b, 0)),
            pl.BlockSpec((SEQ, HALF), lambda b: (b, 1)),
            pl.BlockSpec((SEQ, HALF), lambda b: (b, 2)),
        ],
        out_specs=[
            pl.BlockSpec((SEQ, HALF), lambda b: (b, 0)),
            pl.BlockSpec((None, None, N_HEADS, SEQ, HEAD_DIM), lambda b: (b, 0, 0, 0, 0)),
            pl.BlockSpec((None, None, N_HEADS, SEQ, HEAD_DIM), lambda b: (b, 0, 0, 0, 0)),
        ],
        out_shape=[
            jax.ShapeDtypeStruct((N_PROMPT_TOK, HALF), BF16),
            jax.ShapeDtypeStruct((BATCH, 1, N_HEADS, SEQ, HEAD_DIM), F32),
            jax.ShapeDtypeStruct((BATCH, 1, N_HEADS, SEQ, HEAD_DIM), F32),
        ],
        compiler_params=_cparams(("parallel",)),
        name="ctx_attn",
    )(proj, proj, proj)


NA_MASKED = -math.inf
NA_QROWS = 4
NA_KROWS = NA_QROWS + WIN_H
NA_NQ = NA_QROWS * GRID_W
NA_NK = NA_KROWS * GRID_W
NA_BLOCKS = GRID_ROWS // NA_QROWS
NA_TYPES = 3


def _na_col_table(rpb):
    qc = jnp.arange(GRID_W)[:, None]
    kc = jnp.arange(GRID_W)[None, :]
    c0 = jnp.clip(qc - WIN_W // 2, 0, GRID_W - WIN_W)
    valid = (kc >= c0) & (kc < c0 + WIN_W)
    coff = jnp.clip(kc - qc, -(WIN_W - 1), WIN_W - 1) + (WIN_W - 1)
    onehot = (coff.reshape(1, -1) == jnp.arange(2 * WIN_W - 1)[:, None]).astype(F32)
    t = jnp.dot(rpb.astype(F32).reshape(N_HEADS * (2 * WIN_H - 1), 2 * WIN_W - 1), onehot,
                precision=lax.Precision.HIGHEST)
    t = t.reshape(N_HEADS, 2 * WIN_H - 1, GRID_W, GRID_W)
    return jnp.where(valid[None, None], t, NA_MASKED)


def _na_window_start(rb):
    return min(max(NA_QROWS * rb - WIN_H // 2, 0), GRID_ROWS - NA_KROWS)


def _na_body(q_ref, k_ref, v_ref, ck_ref, cv_ref, tab_ref, o_ref, bias_ref, kc_ref, vc_ref, va_ref):
    @pl.when(pl.program_id(1) == 0)
    def _():
        masked = jnp.full((GRID_W, GRID_W), NA_MASKED, F32)
        for hh in range(2):
            for ty in range(NA_TYPES):
                q0 = (0, NA_QROWS, GRID_ROWS - NA_QROWS)[ty]
                ks = (0, 0, GRID_ROWS - NA_KROWS)[ty]
                for a in range(NA_QROWS):
                    r0 = min(max(q0 + a - WIN_H // 2, 0), GRID_ROWS - WIN_H)
                    for y in range(NA_KROWS):
                        kr = ks + y
                        ok = r0 <= kr < r0 + WIN_H
                        blk = tab_ref[hh, kr - (q0 + a) + WIN_H - 1] if ok else masked
                        bias_ref[ty, hh * NA_NQ + a * GRID_W:hh * NA_NQ + (a + 1) * GRID_W,
                                 y * GRID_W:(y + 1) * GRID_W] = blk

    lane = lax.broadcasted_iota(jnp.int32, (1, 2 * HEAD_DIM), 1)
    own = [lane < HEAD_DIM, lane >= HEAD_DIM]
    kc_ref[...] = jnp.concatenate([ck_ref[0], ck_ref[1]], axis=1).astype(BF16)
    vc = jnp.concatenate([cv_ref[0], cv_ref[1]], axis=1)
    v = v_ref[...]
    for hh in range(2):
        vc_ref[hh] = jnp.where(own[hh], vc, 1.0).astype(BF16)
        va_ref[hh] = jnp.where(own[hh], v, jnp.ones_like(v))

    def block(rb, ty):
        qs = rb * NA_NQ
        ks = _na_window_start(rb) * GRID_W
        q = q_ref[pl.ds(qs, NA_NQ), :] * jnp.asarray(HEAD_DIM ** -0.5, BF16)
        q2 = jnp.concatenate([jnp.where(own[0], q, jnp.zeros_like(q)),
                              jnp.where(own[1], q, jnp.zeros_like(q))], axis=0)
        s_loc = _dot_nt(q2, k_ref[pl.ds(ks, NA_NK), :]) + bias_ref[ty]
        s_ctx = _dot_nt(q2, kc_ref[...])
        m = jnp.maximum(jnp.max(s_loc, axis=-1, keepdims=True), jnp.max(s_ctx, axis=-1, keepdims=True))
        p_loc = jnp.exp(s_loc - m).astype(BF16)
        p_ctx = jnp.exp(s_ctx - m).astype(BF16)
        out = None
        for hh in range(2):
            rows = slice(hh * NA_NQ, (hh + 1) * NA_NQ)
            num = _dot(p_loc[rows], va_ref[hh, pl.ds(ks, NA_NK), :]) + _dot(p_ctx[rows], vc_ref[hh])
            den = jnp.where(own[hh], pltpu.roll(num, HEAD_DIM, 1), 1.0)
            o = num / den
            out = o if out is None else jnp.where(own[0], out, o)
        o_ref[pl.ds(qs, NA_NQ), :] = out.astype(BF16)

    for rb in range(NA_BLOCKS):
        block(rb, 0 if rb == 0 else (2 if rb == NA_BLOCKS - 1 else 1))


def _na_attn(proj, cache_k, cache_v, col_table, layer_idx):
    row_blk0 = N_PROMPT_TOK // DEC_SEQ
    hp = N_HEADS // 2
    return pl.pallas_call(
        _na_body,
        grid=(hp, DEC_BATCH),
        in_specs=[
            pl.BlockSpec((DEC_SEQ, 128), lambda h, b: (row_blk0 + b, h)),
            pl.BlockSpec((DEC_SEQ, 128), lambda h, b: (row_blk0 + b, hp + h)),
            pl.BlockSpec((DEC_SEQ, 128), lambda h, b: (row_blk0 + b, 2 * hp + h)),
            pl.BlockSpec((None, None, 2, PAST_LEN, HEAD_DIM), lambda h, b: (b, layer_idx, h, 0, 0)),
            pl.BlockSpec((None, None, 2, PAST_LEN, HEAD_DIM), lambda h, b: (b, layer_idx, h, 0, 0)),
            pl.BlockSpec((2, 2 * WIN_H - 1, GRID_W, GRID_W), lambda h, b: (h, 0, 0, 0)),
        ],
        out_specs=pl.BlockSpec((DEC_SEQ, 128), lambda h, b: (b, h)),
        out_shape=jax.ShapeDtypeStruct((N_SAMPLE_TOK, HALF), BF16),
        scratch_shapes=[pltpu.VMEM((NA_TYPES, 2 * NA_NQ, NA_NK), F32),
                        pltpu.VMEM((PAST_LEN, 2 * HEAD_DIM), BF16),
                        pltpu.VMEM((2, PAST_LEN, 2 * HEAD_DIM), BF16),
                        pltpu.VMEM((2, DEC_SEQ, 2 * HEAD_DIM), BF16)],
        compiler_params=_cparams(("parallel", "arbitrary")),
        name="na_attn",
    )(proj, proj, proj, cache_k, cache_v, col_table)


N_POW = 24


PREP_PPS = 2


def _s5_prep_body(*refs):
    for n in range(PREP_PPS):
        _s5_prep_pair(PREP_PPS * pl.program_id(0) + n, *[r.at[n] for r in refs])


def _s5_prep_pair(p, vec_ref, bt_ref, c_ref, wend_ref, ktoe_ref, woutT_ref, apow_ref, pw_ref, pct_ref):
    L = S5_CHUNK
    lane_blk = lax.broadcasted_iota(jnp.int32, (S5_CH, S5_W), 1) // S5_CH
    src_blk = (lane_blk + L - p) % L
    krows = []
    for d in range(2):
        v = vec_ref[d]
        lre, lim, step = v[0:1], v[1:2], v[2:3]
        a = lre * step
        b = lim * step
        ea = jnp.exp(a)
        nr = ea * jnp.cos(b) - 1.0
        ni = ea * jnp.sin(b)
        den = lre * lre + lim * lim
        qr = (nr * lre + ni * lim) / den
        qi = (ni * lre - nr * lim) / den
        bre, bim = bt_ref[d, 0], bt_ref[d, 1]
        bqr = bre * qr - bim * qi
        bqi = bre * qi + bim * qr
        cre, cim = c_ref[d, 0], c_ref[d, 1]

        e = lax.broadcasted_iota(jnp.int32, (N_POW, 1), 0).astype(F32)
        mag = jnp.exp(e * a)
        pw_ref[0] = mag * jnp.cos(e * b)
        pw_ref[1] = mag * jnp.sin(e * b)
        mag = jnp.exp((L * e) * a)
        apow_ref[d, 0] = mag * jnp.cos((L * e) * b)
        apow_ref[d, 1] = mag * jnp.sin((L * e) * b)

        def power(k):
            return pw_ref[0, k:k + 1, :], pw_ref[1, k:k + 1, :]

        def c_block(k):
            pr, pi = power(k)
            return cre * pr - cim * pi, -(cre * pi + cim * pr)

        for i in range(L):
            rows = pl.ds(pl.multiple_of(((i + p) % L) * S5_CH, S5_CH), S5_CH)
            pr, pi = power(L - 1 - i if d == 0 else i)
            wend_ref[rows, d * 256:d * 256 + 128] = (bqr * pr - bqi * pi).astype(BF16)
            wend_ref[rows, d * 256 + 128:d * 256 + 256] = (bqr * pi + bqi * pr).astype(BF16)
            mre, mim = c_block(i + 1 if d == 0 else L - i)
            woutT_ref[rows, d * 256:d * 256 + 128] = mre.astype(BF16)
            woutT_ref[rows, d * 256 + 128:d * 256 + 256] = mim.astype(BF16)
            lag_rows = slice(i * S5_CH, (i + 1) * S5_CH)
            mre, mim = c_block(i if d == 0 else L - 1 - i)
            pct_ref[lag_rows, 0:128] = mre
            pct_ref[lag_rows, 128:256] = mim

        bqcat = jnp.concatenate([bqr, bqi], axis=1)
        krows.append(_dot_nt(bqcat, pct_ref[...], precision=lax.Precision.HIGHEST))

    base_f = pltpu.roll(krows[0], p * S5_CH, 1)
    base_b = pltpu.roll(krows[1], p * S5_CH, 1)
    for i in range(L):
        rows = pl.ds(pl.multiple_of(((i + p) % L) * S5_CH, S5_CH), S5_CH)
        blk_f = base_f if i == 0 else pltpu.roll(base_f, i * S5_CH, 1)
        blk_b = base_b if (i + 1) % L == 0 else pltpu.roll(base_b, ((i + 1) % L) * S5_CH, 1)
        blk = jnp.where(src_blk >= i, blk_f, 0.0) + jnp.where(src_blk <= i, blk_b, 0.0)
        ktoe_ref[rows, :] = blk.astype(BF16)


def _s5_prep(lam_re, lam_im, log_step, b_re, b_im, c_re, c_im):
    def pair_vec(a):
        return a.astype(F32).reshape(2, S5_PAIRS, 1, S5_P).transpose(1, 0, 2, 3)

    step = jnp.broadcast_to(jnp.exp(log_step.astype(F32))[:, :, None], (2, N_SSM_GROUPS, SSM_STATE))
    vec = jnp.concatenate([pair_vec(lam_re), pair_vec(lam_im), pair_vec(step),
                           jnp.zeros((S5_PAIRS, 2, 5, S5_P), F32)], axis=2)

    def block_diag_ch_by_state(m):
        m = m.astype(F32).reshape(2, S5_PAIRS, 2, SSM_GROUP, SSM_STATE)
        eye = jnp.eye(2, dtype=F32)
        out = m[:, :, :, :, None, :] * eye[None, None, :, None, :, None]
        return out.reshape(2, S5_PAIRS, S5_CH, S5_P).transpose(1, 0, 2, 3)

    bt = jnp.stack([block_diag_ch_by_state(b_re.transpose(0, 1, 3, 2)),
                    block_diag_ch_by_state(b_im.transpose(0, 1, 3, 2))], axis=2)
    cc = jnp.stack([block_diag_ch_by_state(c_re), block_diag_ch_by_state(c_im)], axis=2)

    mat = jax.ShapeDtypeStruct((S5_PAIRS, S5_W, S5_W), BF16)
    mat_spec = pl.BlockSpec((PREP_PPS, S5_W, S5_W), lambda p: (p, 0, 0))
    return pl.pallas_call(
        _s5_prep_body,
        grid=(S5_PAIRS // PREP_PPS,),
        in_specs=[
            pl.BlockSpec((PREP_PPS, 2, 8, S5_P), lambda p: (p, 0, 0, 0)),
            pl.BlockSpec((PREP_PPS, 2, 2, S5_CH, S5_P), lambda p: (p, 0, 0, 0, 0)),
            pl.BlockSpec((PREP_PPS, 2, 2, S5_CH, S5_P), lambda p: (p, 0, 0, 0, 0)),
        ],
        out_specs=[mat_spec, mat_spec, mat_spec,
                   pl.BlockSpec((PREP_PPS, 2, 2, N_POW, S5_P), lambda p: (p, 0, 0, 0, 0))],
        out_shape=[mat, mat, mat, jax.ShapeDtypeStruct((S5_PAIRS, 2, 2, N_POW, S5_P), F32)],
        scratch_shapes=[pltpu.VMEM((PREP_PPS, 2, N_POW, S5_P), F32), pltpu.VMEM((PREP_PPS, S5_W, 2 * S5_P), F32)],
        compiler_params=_cparams(("parallel",)),
        name="s5_prep",
    )(vec, bt, cc)


S5_SEG_ROWS = 256
S5_N_SEG = N_CHUNK_ROWS // S5_SEG_ROWS
S5_SUB = S5_SEG_ROWS // S5_CHUNK
S5_SUB_PER_SEQ = (DEC_SEQ // S5_CHUNK) // S5_CHUNK


S5_PPS = 4


def _s5_body(tu_ref, s0_ref, wend_ref, ktoe_ref, woutT_ref, apow_ref, ty_ref, fin_ref, z_ref, xs_ref, g_ref):
    seg = pl.program_id(0)
    L = S5_CHUNK
    lane_blk = lax.broadcasted_iota(jnp.int32, (S5_SEG_ROWS, 128), 1) // S5_CH
    pairs = [S5_PPS * pl.program_id(1) + n for n in range(S5_PPS)]

    s_rows, ends = [], []
    for n, p in enumerate(pairs):
        tiles = []
        for t in range(LANE_TILES):
            tile = None
            for k in range(128 // S5_CH):
                j = (LANE_TILES * t + k + L - p) % L
                blk = tu_ref[j, t]
                tile = blk if tile is None else jnp.where(lane_blk == k, blk, tile)
            tiles.append(tile)
        s = jnp.concatenate(tiles, axis=1)
        s_rows.append(s)
        z = _dot(s, wend_ref[n])
        for q in range(LANE_TILES):
            z_ref[n, q] = z[:, q * 128:(q + 1) * 128]

        pair_ends = []
        for d in range(2):
            are, aim = apow_ref[n, d, 0, 1:2, :], apow_ref[n, d, 1, 1:2, :]
            xr = jnp.zeros((S5_SUB, S5_P), F32)
            xi = jnp.zeros((S5_SUB, S5_P), F32)
            for c in (range(L) if d == 0 else reversed(range(L))):
                rows = pl.ds(c, S5_SUB, stride=L)
                xs_ref[n, 2 * d, rows, :] = xr
                xs_ref[n, 2 * d + 1, rows, :] = xi
                zr = z_ref[n, 2 * d, rows, :]
                zi = z_ref[n, 2 * d + 1, rows, :]
                xr, xi = are * xr - aim * xi + zr, are * xi + aim * xr + zi
            pair_ends.append((xr, xi))
            fin_ref[n, d, 0] = xr
            fin_ref[n, d, 1] = xi
        ends.append(pair_ends)

    @pl.when(seg > 0)
    def _():
        for n in range(S5_PPS):
            for d in range(2):
                er, ei = ends[n][d]
                a16r, a16i = apow_ref[n, d, 0, L:L + 1, :], apow_ref[n, d, 1, L:L + 1, :]
                for bb in range(S5_SUB // S5_SUB_PER_SEQ):
                    gr = s0_ref[n, bb, 2 * d:2 * d + 1, :]
                    gi = s0_ref[n, bb, 2 * d + 1:2 * d + 2, :]
                    order = range(S5_SUB_PER_SEQ) if d == 0 else reversed(range(S5_SUB_PER_SEQ))
                    prev = None
                    for sub in order:
                        row = bb * S5_SUB_PER_SEQ + sub
                        if prev is not None:
                            gr, gi = (a16r * gr - a16i * gi + er[prev:prev + 1, :],
                                      a16r * gi + a16i * gr + ei[prev:prev + 1, :])
                        g_ref[n, 0, row:row + 1, :] = gr
                        g_ref[n, 1, row:row + 1, :] = gi
                        prev = row
                if d == 0:
                    pr, pi = apow_ref[n, d, 0, 0:L, :], apow_ref[n, d, 1, 0:L, :]
                else:
                    pr = jnp.concatenate([apow_ref[n, d, 0, k:k + 1, :] for k in reversed(range(L))], axis=0)
                    pi = jnp.concatenate([apow_ref[n, d, 1, k:k + 1, :] for k in reversed(range(L))], axis=0)
                for sub in range(S5_SUB):
                    rows = slice(sub * L, (sub + 1) * L)
                    gr = g_ref[n, 0, sub:sub + 1, :]
                    gi = g_ref[n, 1, sub:sub + 1, :]
                    xs_ref[n, 2 * d, rows, :] = xs_ref[n, 2 * d, rows, :] + (pr * gr - pi * gi)
                    xs_ref[n, 2 * d + 1, rows, :] = xs_ref[n, 2 * d + 1, rows, :] + (pr * gi + pi * gr)

    @pl.when(pl.program_id(1) == 0)
    def _():
        ty_ref[...] = jnp.zeros_like(ty_ref)

    for n, p in enumerate(pairs):
        xs = jnp.concatenate([xs_ref[n, q].astype(BF16) for q in range(LANE_TILES)], axis=1)
        y = _dot(s_rows[n], ktoe_ref[n]) + _dot_nt(xs, woutT_ref[n])
        for t in range(LANE_TILES):
            yt = y[:, t * 128:(t + 1) * 128]
            for k in range(128 // S5_CH):
                j = (LANE_TILES * t + k + L - p) % L
                ty_ref[j, t] = jnp.where(lane_blk == k, yt, ty_ref[j, t])


def _s5_scan(tu, s0, wend, ktoe, woutT, apow):
    mat_spec = pl.BlockSpec((S5_PPS, S5_W, S5_W), lambda g, p: (p, 0, 0))
    seqs = S5_SUB // S5_SUB_PER_SEQ
    return pl.pallas_call(
        _s5_body,
        grid=(S5_N_SEG, S5_PAIRS // S5_PPS),
        in_specs=[
            pl.BlockSpec((S5_CHUNK, LANE_TILES, S5_SEG_ROWS, 128), lambda g, p: (0, 0, g, 0)),
            pl.BlockSpec((S5_PPS, seqs, 4, S5_P), lambda g, p: (p, jnp.maximum(g - 1, 0), 0, 0)),
            mat_spec, mat_spec, mat_spec,
            pl.BlockSpec((S5_PPS, 2, 2, N_POW, S5_P), lambda g, p: (p, 0, 0, 0, 0)),
        ],
        out_specs=[
            pl.BlockSpec((S5_CHUNK, LANE_TILES, S5_SEG_ROWS, 128), lambda g, p: (0, 0, g, 0)),
            pl.BlockSpec((S5_PPS, 2, 2, S5_SUB, S5_P), lambda g, p: (p, 0, 0, g, 0)),
        ],
        out_shape=[
            jax.ShapeDtypeStruct((S5_CHUNK, LANE_TILES, N_CHUNK_ROWS, 128), F32),
            jax.ShapeDtypeStruct((S5_PAIRS, 2, 2, S5_N_SEG * S5_SUB, S5_P), F32),
        ],
        scratch_shapes=[pltpu.VMEM((S5_PPS, LANE_TILES, S5_SEG_ROWS, 128), F32),
                        pltpu.VMEM((S5_PPS, LANE_TILES, S5_SEG_ROWS, 128), F32),
                        pltpu.VMEM((S5_PPS, 2, S5_SUB, S5_P), F32)],
        compiler_params=_cparams(("parallel", "arbitrary")),
        name="s5_scan",
    )(tu, s0, wend, ktoe, woutT, apow)


TM_GLU = 2048


def _glu_body(tu_ref, ty_ref, d_ref, gw_ref, gb_ref, o_ref, out_ref):
    rows = TM_GLU // S5_CHUNK
    d = d_ref[...]
    pre = []
    for j in range(S5_CHUNK):
        uj = jnp.concatenate([tu_ref[j, t].astype(F32) for t in range(LANE_TILES)], axis=1)
        yj = jnp.concatenate([ty_ref[j, t] for t in range(LANE_TILES)], axis=1)
        dj = pltpu.roll(d, j * S5_CH, 1) if j else d
        yj = dj * uj + yj
        back = ((S5_CHUNK - j) % S5_CHUNK) * S5_CH
        pre.append(pltpu.roll(yj, back, 1) if back else yj)
    y = jnp.concatenate(pre, axis=0)
    cdf = 0.5 * (1.0 + jnp.tanh(math.sqrt(2.0 / math.pi) * (y + 0.044715 * (y * y * y))))
    y = y * cdf
    gate = jax.nn.sigmoid(_dot(y.astype(BF16), gw_ref[...]) + gb_ref[...])
    out = y * gate
    for j in range(S5_CHUNK):
        oj = out[j * rows:(j + 1) * rows]
        for t in range(LANE_TILES):
            out_ref[t, pl.ds(j, rows, stride=S5_CHUNK), :] = oj[:, t * 128:(t + 1) * 128]
    o_ref[...] = jnp.concatenate([out_ref[t] for t in range(LANE_TILES)], axis=1).astype(BF16)


def _glu(tu, ty, ssm_d, glu_w, glu_b):
    tm = TM_GLU
    rows = tm // S5_CHUNK
    lay_spec = pl.BlockSpec((S5_CHUNK, LANE_TILES, rows, 128), lambda i: (0, 0, i, 0))
    return pl.pallas_call(
        _glu_body,
        grid=(N_TOK // tm,),
        in_specs=[
            lay_spec, lay_spec,
            pl.BlockSpec((1, HALF), lambda i: (0, 0)),
            pl.BlockSpec((HALF, HALF), lambda i: (0, 0)),
            pl.BlockSpec((1, HALF), lambda i: (0, 0)),
        ],
        out_specs=pl.BlockSpec((tm, HALF), lambda i: (i, 0)),
        out_shape=jax.ShapeDtypeStruct((N_TOK, HALF), BF16),
        scratch_shapes=[pltpu.VMEM((LANE_TILES, tm, 128), F32)],
        compiler_params=_cparams(("parallel",)),
        name="s5_glu",
    )(tu, ty, ssm_d.reshape(1, HALF), glu_w.astype(BF16), glu_b.reshape(1, HALF))


def kernel(x_prompt, x_sample, c, cache_k, cache_v, state_s5_re, state_s5_im, c_ctx, norm1_g, norm2_g, ada_w, ada_b, mlp_w1, mlp_w2, ab_w_in, pool_w, pool_scale, conv_w, ab_w_out, cd_w_in, na_rpb, ssm_lambda_re, ssm_lambda_im, ssm_log_step, ssm_b_re, ssm_b_im, ssm_c_re, ssm_c_im, ssm_d, glu_w, glu_b, cd_w_out, final_g):
    depth = ada_w.shape[0]
    xs = (x_prompt.reshape(N_PROMPT_TOK, D_MODEL), x_sample.reshape(N_SAMPLE_TOK, D_MODEL))
    cond = jnp.concatenate([c_ctx[None, :], c, jnp.zeros((N_COND - 1 - DEC_BATCH, D_MODEL), F32)], axis=0)
    mods = _adaln(cond, ada_w, ada_b).reshape(depth, N_COND, 6, D_MODEL)

    new_k, new_v, new_re, new_im = [], [], [], []
    for layer in range(depth):
        i = layer // 2
        final = layer == depth - 1
        if layer % 2 == 0:
            y, x = _even_layer_mix(xs, mods, layer, norm1_g[layer], ab_w_in, i,
                                   pool_w[i], pool_scale[i], conv_w[i])
            ya, ya_col, yb, yb_col = y, 0, y, 1
            w_out_stack = ab_w_out
        else:
            x = xs[0]
            qkv, tu = _inproj_odd(x, mods, layer, norm1_g[layer], cd_w_in, i)
            o_p, ck, cv = _ctx_attn(qkv)
            new_k.append(ck)
            new_v.append(cv)
            o_s = _na_attn(qkv, cache_k, cache_v, _na_col_table(na_rpb[i]), i)
            attn = (o_p, o_s)

            wend, ktoe, woutT, apow = _s5_prep(ssm_lambda_re[i], ssm_lambda_im[i], ssm_log_step[i],
                                               ssm_b_re[i], ssm_b_im[i], ssm_c_re[i], ssm_c_im[i])

            def pair_state(s):
                return s.astype(F32).reshape(DEC_BATCH, 2, S5_PAIRS, S5_P).transpose(2, 0, 1, 3)

            s0 = jnp.stack([pair_state(state_s5_re[:, i]), pair_state(state_s5_im[:, i])], axis=3)
            s0 = s0.reshape(S5_PAIRS, DEC_BATCH, 4, S5_P)
            ty, fin = _s5_scan(tu, s0, wend, ktoe, woutT, apow)
            fin = fin[:, :, :, :BATCH]
            new_re.append(fin[:, :, 0].transpose(2, 1, 0, 3).reshape(BATCH, 2, N_SSM_GROUPS, SSM_STATE))
            new_im.append(fin[:, :, 1].transpose(2, 1, 0, 3).reshape(BATCH, 2, N_SSM_GROUPS, SSM_STATE))
            d_out = _glu(tu, ty, ssm_d[i], glu_w[i], glu_b[i])
            ya, ya_col, yb, yb_col = attn, 0, d_out, 0
            w_out_stack = cd_w_out
        xs = _mlp(x, ya, ya_col, yb, yb_col, w_out_stack, i, mods, layer, norm2_g[layer], final_g,
                  mlp_w1, mlp_w2, final)

    y_prompt = xs[0].reshape(BATCH, SEQ, D_MODEL)
    y_sample = xs[1].reshape(DEC_BATCH, DEC_SEQ, D_MODEL)
    return (y_prompt, y_sample, jnp.concatenate(new_k, axis=1), jnp.concatenate(new_v, axis=1),
            jnp.stack(new_re, axis=1), jnp.stack(new_im, axis=1))
```

```python
import functools
import math

import jax
import jax.numpy as jnp
from jax import lax
from jax.experimental import pallas as pl
from jax.experimental.pallas import tpu as pltpu

F32 = jnp.float32
BF16 = jnp.bfloat16

D_MODEL = 1024
BATCH = 16
SEQ = 256
DEC_BATCH = 4
DEC_SEQ = 2048
PAST_LEN = 256
GRID_W = 64
GRID_ROWS = DEC_SEQ // GRID_W
HALF = 512
POOL_WINDOWS = (2, 4, 8, 16)
POOL_GROUP = 128
HEAD_DIM = 64
N_HEADS = 8
WIN_H = 8
WIN_W = 16
SSM_GROUP = 16
N_SSM_GROUPS = 32
SSM_STATE = 64
D_FF = 4096
EPS = 1e-6

N_PROMPT_TOK = BATCH * SEQ
N_SAMPLE_TOK = DEC_BATCH * DEC_SEQ
N_TOK = N_PROMPT_TOK + N_SAMPLE_TOK
N_COND = 8

S5_CHUNK = 16
S5_PAIRS = N_SSM_GROUPS // 2
S5_CH = 2 * SSM_GROUP
S5_P = 2 * SSM_STATE
S5_W = S5_CHUNK * S5_CH

VMEM_LIMIT = 56 * 1024 * 1024


def _cparams(sem):
    return pltpu.CompilerParams(dimension_semantics=sem, vmem_limit_bytes=VMEM_LIMIT)


def _cond_row(i, tm):
    npt = N_PROMPT_TOK // tm
    per = DEC_SEQ // tm
    return jnp.where(i < npt, 0, 1 + (i - npt) // per)


def _resident_layer(shape, idx):
    zeros = (0,) * len(shape)
    return pl.BlockSpec((None,) + tuple(shape), lambda i: (idx,) + zeros, pipeline_mode=pl.Buffered(1))


def _normmod(x, g, shift, scale):
    ms = jnp.mean(x * x, axis=-1, keepdims=True)
    return (x * lax.rsqrt(ms + EPS) * g) * (1.0 + scale) + shift


def _dot(a, b):
    return jnp.dot(a, b, preferred_element_type=F32)


def _dot_nt(a, b, precision=None):
    return lax.dot_general(a, b, (((1,), (1,)), ((), ())), preferred_element_type=F32, precision=precision)


def _adaln_body(c_ref, w_ref, b_ref, o_ref):
    c = c_ref[...]
    s = c * jax.nn.sigmoid(c)
    o_ref[...] = _dot(s.astype(BF16), w_ref[...].astype(BF16)) + b_ref[...]


def _adaln(cond, ada_w, ada_b):
    depth = ada_w.shape[0]
    nj = 6
    return pl.pallas_call(
        _adaln_body,
        grid=(depth, nj),
        in_specs=[
            pl.BlockSpec((N_COND, D_MODEL), lambda l, j: (0, 0)),
            pl.BlockSpec((None, D_MODEL, D_MODEL), lambda l, j: (l, 0, j)),
            pl.BlockSpec((None, 1, D_MODEL), lambda l, j: (l, 0, j)),
        ],
        out_specs=pl.BlockSpec((None, N_COND, D_MODEL), lambda l, j: (l, 0, j)),
        out_shape=jax.ShapeDtypeStruct((depth, N_COND, 6 * D_MODEL), F32),
        compiler_params=_cparams(("arbitrary", "arbitrary")),
        name="adaln",
    )(cond, ada_w, ada_b.reshape(depth, 1, 6 * D_MODEL))


TM_PROJ = 1024


def _stream_tile(x_refs, tm):
    if len(x_refs) == 1:
        return x_refs[0][...]
    is_prompt = pl.program_id(0) < N_PROMPT_TOK // tm
    return jnp.where(is_prompt, x_refs[0][...], x_refs[1][...])


N_CHUNK_ROWS = N_TOK // S5_CHUNK
LANE_TILES = S5_W // 128


def _inproj_odd_body(x_ref, m_ref, g_ref, w_ref, qkv_ref, tu_ref, u_ref, wbf_ref):
    @pl.when(pl.program_id(0) == 0)
    def _():
        wbf_ref[...] = w_ref[...].astype(BF16)

    m = m_ref[...]
    h = _normmod(x_ref[...], g_ref[...], m[0:1], m[1:2])
    acc = _dot(h.astype(BF16), wbf_ref[...])
    qkv_ref[...] = acc[:, 0:3 * HALF].astype(BF16)
    for t in range(LANE_TILES):
        u_ref[t] = acc[:, 3 * HALF + t * 128:3 * HALF + (t + 1) * 128]
    rows = TM_PROJ // S5_CHUNK
    for j in range(S5_CHUNK):
        uj = jnp.concatenate([u_ref[t, pl.ds(j, rows, stride=S5_CHUNK), :] for t in range(LANE_TILES)], axis=1)
        if j:
            uj = pltpu.roll(uj, j * S5_CH, 1)
        for t in range(LANE_TILES):
            tu_ref[j, t] = uj[:, t * 128:(t + 1) * 128].astype(BF16)


def _inproj_odd(x, mods, layer, g, w_stack, w_idx):
    tm = TM_PROJ
    rows = tm // S5_CHUNK
    return pl.pallas_call(
        _inproj_odd_body,
        grid=(N_TOK // tm,),
        in_specs=[
            pl.BlockSpec((tm, D_MODEL), lambda i: (i, 0)),
            pl.BlockSpec((None, None, 6, D_MODEL), lambda i: (layer, _cond_row(i, tm), 0, 0)),
            pl.BlockSpec((1, D_MODEL), lambda i: (0, 0)),
            _resident_layer((D_MODEL, 4 * HALF), w_idx),
        ],
        out_specs=[
            pl.BlockSpec((tm, 3 * HALF), lambda i: (i, 0)),
            pl.BlockSpec((S5_CHUNK, LANE_TILES, rows, 128), lambda i: (0, 0, i, 0)),
        ],
        out_shape=[
            jax.ShapeDtypeStruct((N_TOK, 3 * HALF), BF16),
            jax.ShapeDtypeStruct((S5_CHUNK, LANE_TILES, N_CHUNK_ROWS, 128), BF16),
        ],
        scratch_shapes=[pltpu.VMEM((LANE_TILES, tm, 128), F32), pltpu.VMEM((D_MODEL, 4 * HALF), BF16)],
        compiler_params=_cparams(("arbitrary",)),
        name="inproj_odd",
    )(x, mods, g.reshape(1, D_MODEL), w_stack)


TM_MIX = 256
HALO = 16
RING = 3
LAG = 2
PAD_ROWS = TM_MIX + 2 * HALO + 8


def _even_step(phase, x_refs, m_ref, g_ref, w_ref, pw_ref, ps_ref, cw_ref, y_ref, xcat_ref, proj_ref, tail_ref,
               h_ref):
    i = pl.program_id(0)
    write_slot, cur_slot, next_slot = phase, (phase + 1) % RING, (phase + 2) % RING

    m = m_ref[...]
    x = _stream_tile(x_refs, TM_MIX)
    xcat_ref[...] = x
    h_ref[...] = _normmod(x, g_ref[...], m[0:1], m[1:2]).astype(BF16)
    n_pieces = len(POOL_WINDOWS) + HALF // 128
    pw = (4 * HALF) // n_pieces

    def project(piece):
        cols = slice(piece * pw, (piece + 1) * pw)
        proj_ref[write_slot, :, cols] = _dot(h_ref[...], w_ref[:, cols])

    t = jnp.maximum(i - LAG, 0)
    n_prompt_tiles = N_PROMPT_TOK // TM_MIX
    tiles_per_seq = DEC_SEQ // TM_MIX
    is_prompt = t < n_prompt_tiles
    tile_in_seq = jnp.where(is_prompt, 0, (t - n_prompt_tiles) % tiles_per_seq)
    seq_len = jnp.where(is_prompt, SEQ, DEC_SEQ)
    has_prev = tile_in_seq > 0
    has_next = jnp.logical_and(jnp.logical_not(is_prompt), tile_in_seq < tiles_per_seq - 1)
    prev_keep = jnp.where(has_prev, 1.0, 0.0).astype(F32)
    next_keep = jnp.where(has_next, 1.0, 0.0).astype(F32)
    pos = tile_in_seq * TM_MIX + lax.broadcasted_iota(jnp.int32, (TM_MIX, 1), 0)
    lo_rows, hi_rows = HALO, HALO + TM_MIX
    head = slice(0, HALO)

    slack = jnp.zeros((PAD_ROWS - TM_MIX - 2 * HALO, POOL_GROUP), F32)
    for gi, w in enumerate(POOL_WINDOWS):
        project(gi)
        cs = slice(gi * POOL_GROUP, (gi + 1) * POOL_GROUP)
        cur = proj_ref[cur_slot, :, cs]
        f = jnp.concatenate([tail_ref[:, cs] * prev_keep, cur, proj_ref[next_slot, head, cs] * next_keep, slack],
                            axis=0)
        n_rows, span = PAD_ROWS, 1
        while 2 * span < w:
            n_rows -= 8
            f = f[0:n_rows] + f[span:span + n_rows]
            span *= 2
        s = f[lo_rows - span:hi_rows - span] + f[lo_rows:hi_rows]
        lo = jnp.maximum(pos - w // 2, 0)
        hi = jnp.minimum(pos + (w - w // 2), seq_len)
        inv = 1.0 / (hi - lo).astype(F32)
        pooled = s * inv - cur
        mixed = _dot(pooled.astype(BF16), pw_ref[gi]) * ps_ref[:, cs]
        y_ref[:, cs] = mixed.astype(BF16)

    for ci in range(HALF // 128):
        project(len(POOL_WINDOWS) + ci)
        cs = slice(ci * 128, (ci + 1) * 128)
        bs = slice(HALF + ci * 128, HALF + (ci + 1) * 128)
        gs = slice(2 * HALF + ci * 128, 2 * HALF + (ci + 1) * 128)
        vs = slice(3 * HALF + ci * 128, 3 * HALF + (ci + 1) * 128)
        z = jnp.concatenate([(tail_ref[:, gs] * tail_ref[:, vs]) * prev_keep,
                             proj_ref[cur_slot, :, gs] * proj_ref[cur_slot, :, vs],
                             (proj_ref[next_slot, head, gs] * proj_ref[next_slot, head, vs]) * next_keep], axis=0)
        conv = (cw_ref[0:1, cs] * z[lo_rows - 1:hi_rows - 1]
                + cw_ref[1:2, cs] * z[lo_rows:hi_rows]
                + cw_ref[2:3, cs] * z[lo_rows + 1:hi_rows + 1])
        y_ref[:, bs] = (proj_ref[cur_slot, :, bs] * conv).astype(BF16)

    tail_ref[...] = proj_ref[cur_slot, TM_MIX - HALO:TM_MIX, :]


def _even_body(n_x, *refs):
    x_refs, rest = refs[:n_x], refs[n_x:]
    m_ref, g_ref, w_ref = rest[:3]
    proj_ref, tail_ref, h_ref, wbf_ref = rest[-4:]
    i = pl.program_id(0)

    @pl.when(i == 0)
    def _():
        proj_ref[...] = jnp.zeros_like(proj_ref)
        tail_ref[...] = jnp.zeros_like(tail_ref)
        wbf_ref[...] = w_ref[...].astype(BF16)

    for phase in range(RING):
        @pl.when(i % RING == phase)
        def _(phase=phase):
            _even_step(phase, x_refs, m_ref, g_ref, wbf_ref, *rest[3:-1])


def _even_layer_mix(xs, mods, layer, g, w_stack, w_idx, pool_w, pool_scale, conv_w):
    tm = TM_MIX
    n_tiles = N_TOK // tm
    npt = N_PROMPT_TOK // tm
    last = n_tiles - 1
    if len(xs) == 1:
        x_specs = [pl.BlockSpec((tm, D_MODEL), lambda i: (jnp.minimum(i, last), 0))]
    else:
        x_specs = [pl.BlockSpec((tm, D_MODEL), lambda i: (jnp.minimum(i, npt - 1), 0)),
                   pl.BlockSpec((tm, D_MODEL), lambda i: (jnp.clip(i - npt, 0, last - npt), 0))]
    return pl.pallas_call(
        functools.partial(_even_body, len(xs)),
        grid=(n_tiles + LAG,),
        in_specs=x_specs + [
            pl.BlockSpec((None, None, 6, D_MODEL), lambda i: (layer, _cond_row(jnp.minimum(i, last), tm), 0, 0)),
            pl.BlockSpec((1, D_MODEL), lambda i: (0, 0)),
            _resident_layer((D_MODEL, 4 * HALF), w_idx),
            pl.BlockSpec((len(POOL_WINDOWS), POOL_GROUP, POOL_GROUP), lambda i: (0, 0, 0)),
            pl.BlockSpec((1, HALF), lambda i: (0, 0)),
            pl.BlockSpec((3, HALF), lambda i: (0, 0)),
        ],
        out_specs=[pl.BlockSpec((tm, 2 * HALF), lambda i: (jnp.maximum(i - LAG, 0), 0)),
                   pl.BlockSpec((tm, D_MODEL), lambda i: (jnp.minimum(i, last), 0))],
        out_shape=[jax.ShapeDtypeStruct((N_TOK, 2 * HALF), BF16), jax.ShapeDtypeStruct((N_TOK, D_MODEL), F32)],
        scratch_shapes=[pltpu.VMEM((RING, tm, 4 * HALF), F32), pltpu.VMEM((HALO, 4 * HALF), F32),
                        pltpu.VMEM((tm, D_MODEL), BF16), pltpu.VMEM((D_MODEL, 4 * HALF), BF16)],
        compiler_params=_cparams(("arbitrary",)),
        name="even_layer_mix",
    )(*xs, mods, g.reshape(1, D_MODEL), w_stack, pool_w.astype(BF16), pool_scale.reshape(1, HALF), conv_w)


TM_MLP = 1024
RC_MLP = 256
N_CAST = 16


def _mlp_tile(final, x_ref, ya_ref, yb_ref, m_ref, g_ref, fg_ref, o_ref, wo_ref, w1_ref, w2_ref):
    m = m_ref[...]
    gain = g_ref[...] * (1.0 + m[4:5])
    for r in range(TM_MLP // RC_MLP):
        rows = slice(r * RC_MLP, (r + 1) * RC_MLP)
        mix = _dot(ya_ref[rows, :], wo_ref[0:HALF, :]) + _dot(yb_ref[rows, :], wo_ref[HALF:2 * HALF, :])
        x1 = x_ref[rows, :] + m[2:3] * mix
        ms = jnp.mean(x1 * x1, axis=-1, keepdims=True)
        h = (x1 * lax.rsqrt(ms + EPS) * gain + m[3:4]).astype(BF16)
        t = _dot(h, w1_ref[...])
        t = jnp.square(jnp.maximum(t, 0.0)).astype(BF16)
        out = x1 + m[5:6] * _dot(t, w2_ref[...])
        if final:
            ms = jnp.mean(out * out, axis=-1, keepdims=True)
            out = out * lax.rsqrt(ms + EPS) * fg_ref[...]
        o_ref[rows, :] = out


def _mlp_resident_body(final, x_ref, ya_ref, yb_ref, wo_ref, m_ref, g_ref, fg_ref, w1_ref, w2_ref, o_ref):
    _mlp_tile(final, x_ref, ya_ref, yb_ref, m_ref, g_ref, fg_ref, o_ref, wo_ref, w1_ref, w2_ref)


def _mlp_cast_body(final, emit, x_ref, ya_ref, yb_ref, wo_f, m_ref, g_ref, fg_ref, w1_f, w2_f, o_ref, *rest):
    wo_ref, w1_ref, w2_ref = rest[-3:]
    emitted = rest[:-3]
    i = pl.program_id(0)

    @pl.when(i < N_CAST)
    def _():
        for k, (f_ref, b_ref) in enumerate(((wo_f, wo_ref), (w1_f, w1_ref), (w2_f, w2_ref))):
            n = f_ref.shape[0]
            slab = f_ref[...].astype(BF16)
            b_ref[pl.ds(pl.multiple_of(i * n, n), n), :] = slab
            if emit:
                emitted[k][...] = slab

    @pl.when(i >= N_CAST)
    def _():
        _mlp_tile(final, x_ref, ya_ref, yb_ref, m_ref, g_ref, fg_ref, o_ref, wo_ref, w1_ref, w2_ref)


def _resident(shape):
    zeros = (0,) * len(shape)
    return pl.BlockSpec(shape, lambda i: zeros, pipeline_mode=pl.Buffered(1))


def _mlp_call(x, x_off, ya, ya_off, ya_col, yb, yb_off, yb_col, n_tiles, tile0, mods, layer, g2, final_g, final,
              f32_weights=None, bf16_weights=None, emit=False):
    tm = TM_MLP
    lead = N_CAST if f32_weights is not None else 0

    def tile(i):
        return jnp.maximum(i - lead, 0)

    stream_specs = [
        pl.BlockSpec((tm, D_MODEL), lambda i: (tile(i) + x_off, 0)),
        pl.BlockSpec((tm, HALF), lambda i: (tile(i) + ya_off, ya_col)),
        pl.BlockSpec((tm, HALF), lambda i: (tile(i) + yb_off, yb_col)),
    ]
    mod_spec = pl.BlockSpec((None, None, 6, D_MODEL), lambda i: (layer, _cond_row(tile(i) + tile0, tm), 0, 0))
    out_spec = pl.BlockSpec((tm, D_MODEL), lambda i: (tile(i), 0))
    out_shape = jax.ShapeDtypeStruct((n_tiles * tm, D_MODEL), F32)
    vecs = (g2.reshape(1, D_MODEL), final_g.reshape(1, D_MODEL))
    if f32_weights is None:
        w_out, w1, w2 = bf16_weights
        return pl.pallas_call(
            functools.partial(_mlp_resident_body, final),
            grid=(n_tiles,),
            in_specs=stream_specs + [_resident((D_MODEL, D_MODEL)), mod_spec, _resident((1, D_MODEL)),
                                     _resident((1, D_MODEL)), _resident((D_MODEL, D_FF)), _resident((D_FF, D_MODEL))],
            out_specs=out_spec,
            out_shape=out_shape,
            compiler_params=_cparams(("parallel",)),
            name="mlp",
        )(x, ya, yb, w_out, mods, *vecs, w1, w2)

    wo_stack, wo_idx, w1_stack, w2_stack = f32_weights

    def slab(i):
        return jnp.minimum(i, N_CAST - 1)

    shapes = ((D_MODEL, D_MODEL), (D_MODEL, D_FF), (D_FF, D_MODEL))
    slab_rows = [rows // N_CAST for rows, _ in shapes]
    f32_specs = [pl.BlockSpec((None, n, cols), lambda i, idx=idx: (idx, slab(i), 0))
                 for n, (_, cols), idx in zip(slab_rows, shapes, (wo_idx, layer, layer))]
    out_specs, out_shapes = [out_spec], [out_shape]
    if emit:
        out_specs += [pl.BlockSpec((n, cols), lambda i: (slab(i), 0)) for n, (_, cols) in zip(slab_rows, shapes)]
        out_shapes += [jax.ShapeDtypeStruct(shp, BF16) for shp in shapes]
    return pl.pallas_call(
        functools.partial(_mlp_cast_body, final, emit),
        grid=(N_CAST + n_tiles,),
        in_specs=stream_specs + [f32_specs[0], mod_spec, _resident((1, D_MODEL)), _resident((1, D_MODEL)),
                                 f32_specs[1], f32_specs[2]],
        out_specs=out_specs,
        out_shape=out_shapes,
        scratch_shapes=[pltpu.VMEM(shp, BF16) for shp in shapes],
        compiler_params=_cparams(("arbitrary",)),
        name="mlp_cast",
    )(x, ya, yb, wo_stack, mods, *vecs, w1_stack, w2_stack)


def _mlp(x, ya, ya_col, yb, yb_col, w_out_stack, w_out_idx, mods, layer, g2, final_g, w1_stack, w2_stack, final):
    tm = TM_MLP
    npt = N_PROMPT_TOK // tm
    f32_weights = (w_out_stack, w_out_idx, w1_stack, w2_stack)
    common = (mods, layer, g2, final_g, final)
    if not final:
        if isinstance(ya, tuple):
            ya = jnp.concatenate(ya, axis=0)
        return tuple(_mlp_call(x, 0, ya, 0, ya_col, yb, 0, yb_col, N_TOK // tm, 0, *common, f32_weights=f32_weights))
    ya_p, ya_s = ((ya[0], 0), (ya[1], 0)) if isinstance(ya, tuple) else ((ya, 0), (ya, npt))
    y_p, *bf16_weights = _mlp_call(x, 0, ya_p[0], ya_p[1], ya_col, yb, 0, yb_col, npt, 0, *common,
                                   f32_weights=f32_weights, emit=True)
    y_s = _mlp_call(x, npt, ya_s[0], ya_s[1], ya_col, yb, npt, yb_col, N_TOK // tm - npt, npt, *common,
                    bf16_weights=bf16_weights)
    return y_p, y_s


def _ctx_attn_body(q_ref, k_ref, v_ref, o_ref, ck_ref, cv_ref):
    lane = lax.broadcasted_iota(jnp.int32, (1, 2 * HEAD_DIM), 1)
    own = [lane < HEAD_DIM, lane >= HEAD_DIM]
    for h in range(N_HEADS):
        sl = slice(h * HEAD_DIM, (h + 1) * HEAD_DIM)
        ck_ref[h] = k_ref[:, sl].astype(F32)
        cv_ref[h] = v_ref[:, sl].astype(F32)
    for hp in range(N_HEADS // 2):
        sl = slice(hp * 2 * HEAD_DIM, (hp + 1) * 2 * HEAD_DIM)
        q = q_ref[:, sl] * jnp.asarray(HEAD_DIM ** -0.5, BF16)
        k = k_ref[:, sl]
        v = v_ref[:, sl]
        q2 = jnp.concatenate([jnp.where(own[0], q, jnp.zeros_like(q)),
                              jnp.where(own[1], q, jnp.zeros_like(q))], axis=0)
        s = _dot_nt(q2, k)
        m = jnp.max(s, axis=-1, keepdims=True)
        p = jnp.exp(s - m).astype(BF16)
        out = None
        for hh in range(2):
            num = _dot(p[hh * SEQ:(hh + 1) * SEQ], jnp.where(own[hh], v, jnp.ones_like(v)))
            den = jnp.where(own[hh], pltpu.roll(num, HEAD_DIM, 1), 1.0)
            o = num / den
            out = o if out is None else jnp.where(own[0], out, o)
        o_ref[:, sl] = out.astype(BF16)


def _ctx_attn(proj):
    return pl.pallas_call(
        _ctx_attn_body,
        grid=(BATCH,),
        in_specs=[
            pl.BlockSpec((SEQ, HALF), lambda b: (b, 0)),
            pl.BlockSpec((SEQ, HALF), lambda b: (b, 1)),
            pl.BlockSpec((SEQ, HALF), lambda b: (b, 2)),
        ],
        out_specs=[
            pl.BlockSpec((SEQ, HALF), lambda b: (b, 0)),
            pl.BlockSpec((None, None, N_HEADS, SEQ, HEAD_DIM), lambda b: (b, 0, 0, 0, 0)),
            pl.BlockSpec((None, None, N_HEADS, SEQ, HEAD_DIM), lambda b: (b, 0, 0, 0, 0)),
        ],
        out_shape=[
            jax.ShapeDtypeStruct((N_PROMPT_TOK, HALF), BF16),
            jax.ShapeDtypeStruct((BATCH, 1, N_HEADS, SEQ, HEAD_DIM), F32),
            jax.ShapeDtypeStruct((BATCH, 1, N_HEADS, SEQ, HEAD_DIM), F32),
        ],
        compiler_params=_cparams(("parallel",)),
        name="ctx_attn",
    )(proj, proj, proj)


NA_MASKED = -math.inf
NA_QROWS = 4
NA_KROWS = NA_QROWS + WIN_H
NA_NQ = NA_QROWS * GRID_W
NA_NK = NA_KROWS * GRID_W
NA_BLOCKS = GRID_ROWS // NA_QROWS
NA_TYPES = 3


def _na_col_table(rpb):
    qc = jnp.arange(GRID_W)[:, None]
    kc = jnp.arange(GRID_W)[None, :]
    c0 = jnp.clip(qc - WIN_W // 2, 0, GRID_W - WIN_W)
    valid = (kc >= c0) & (kc < c0 + WIN_W)
    coff = jnp.clip(kc - qc, -(WIN_W - 1), WIN_W - 1) + (WIN_W - 1)
    onehot = (coff.reshape(1, -1) == jnp.arange(2 * WIN_W - 1)[:, None]).astype(F32)
    t = jnp.dot(rpb.astype(F32).reshape(N_HEADS * (2 * WIN_H - 1), 2 * WIN_W - 1), onehot,
                precision=lax.Precision.HIGHEST)
    t = t.reshape(N_HEADS, 2 * WIN_H - 1, GRID_W, GRID_W)
    return jnp.where(valid[None, None], t, NA_MASKED)


def _na_window_start(rb):
    return min(max(NA_QROWS * rb - WIN_H // 2, 0), GRID_ROWS - NA_KROWS)


def _na_body(q_ref, k_ref, v_ref, ck_ref, cv_ref, tab_ref, o_ref, bias_ref, kc_ref, vc_ref, va_ref):
    @pl.when(pl.program_id(1) == 0)
    def _():
        masked = jnp.full((GRID_W, GRID_W), NA_MASKED, F32)
        for hh in range(2):
            for ty in range(NA_TYPES):
                q0 = (0, NA_QROWS, GRID_ROWS - NA_QROWS)[ty]
                ks = (0, 0, GRID_ROWS - NA_KROWS)[ty]
                for a in range(NA_QROWS):
                    r0 = min(max(q0 + a - WIN_H // 2, 0), GRID_ROWS - WIN_H)
                    for y in range(NA_KROWS):
                        kr = ks + y
                        ok = r0 <= kr < r0 + WIN_H
                        blk = tab_ref[hh, kr - (q0 + a) + WIN_H - 1] if ok else masked
                        bias_ref[ty, hh * NA_NQ + a * GRID_W:hh * NA_NQ + (a + 1) * GRID_W,
                                 y * GRID_W:(y + 1) * GRID_W] = blk

    lane = lax.broadcasted_iota(jnp.int32, (1, 2 * HEAD_DIM), 1)
    own = [lane < HEAD_DIM, lane >= HEAD_DIM]
    kc_ref[...] = jnp.concatenate([ck_ref[0], ck_ref[1]], axis=1).astype(BF16)
    vc = jnp.concatenate([cv_ref[0], cv_ref[1]], axis=1)
    v = v_ref[...]
    for hh in range(2):
        vc_ref[hh] = jnp.where(own[hh], vc, 1.0).astype(BF16)
        va_ref[hh] = jnp.where(own[hh], v, jnp.ones_like(v))

    def block(rb, ty):
        qs = rb * NA_NQ
        ks = _na_window_start(rb) * GRID_W
        q = q_ref[pl.ds(qs, NA_NQ), :] * jnp.asarray(HEAD_DIM ** -0.5, BF16)
        q2 = jnp.concatenate([jnp.where(own[0], q, jnp.zeros_like(q)),
                              jnp.where(own[1], q, jnp.zeros_like(q))], axis=0)
        s_loc = _dot_nt(q2, k_ref[pl.ds(ks, NA_NK), :]) + bias_ref[ty]
        s_ctx = _dot_nt(q2, kc_ref[...])
        m = jnp.maximum(jnp.max(s_loc, axis=-1, keepdims=True), jnp.max(s_ctx, axis=-1, keepdims=True))
        p_loc = jnp.exp(s_loc - m).astype(BF16)
        p_ctx = jnp.exp(s_ctx - m).astype(BF16)
        out = None
        for hh in range(2):
            rows = slice(hh * NA_NQ, (hh + 1) * NA_NQ)
            num = _dot(p_loc[rows], va_ref[hh, pl.ds(ks, NA_NK), :]) + _dot(p_ctx[rows], vc_ref[hh])
            den = jnp.where(own[hh], pltpu.roll(num, HEAD_DIM, 1), 1.0)
            o = num / den
            out = o if out is None else jnp.where(own[0], out, o)
        o_ref[pl.ds(qs, NA_NQ), :] = out.astype(BF16)

    for rb in range(NA_BLOCKS):
        block(rb, 0 if rb == 0 else (2 if rb == NA_BLOCKS - 1 else 1))


def _na_attn(proj, cache_k, cache_v, col_table, layer_idx):
    row_blk0 = N_PROMPT_TOK // DEC_SEQ
    hp = N_HEADS // 2
    return pl.pallas_call(
        _na_body,
        grid=(hp, DEC_BATCH),
        in_specs=[
            pl.BlockSpec((DEC_SEQ, 128), lambda h, b: (row_blk0 + b, h)),
            pl.BlockSpec((DEC_SEQ, 128), lambda h, b: (row_blk0 + b, hp + h)),
            pl.BlockSpec((DEC_SEQ, 128), lambda h, b: (row_blk0 + b, 2 * hp + h)),
            pl.BlockSpec((None, None, 2, PAST_LEN, HEAD_DIM), lambda h, b: (b, layer_idx, h, 0, 0)),
            pl.BlockSpec((None, None, 2, PAST_LEN, HEAD_DIM), lambda h, b: (b, layer_idx, h, 0, 0)),
            pl.BlockSpec((2, 2 * WIN_H - 1, GRID_W, GRID_W), lambda h, b: (h, 0, 0, 0)),
        ],
        out_specs=pl.BlockSpec((DEC_SEQ, 128), lambda h, b: (b, h)),
        out_shape=jax.ShapeDtypeStruct((N_SAMPLE_TOK, HALF), BF16),
        scratch_shapes=[pltpu.VMEM((NA_TYPES, 2 * NA_NQ, NA_NK), F32),
                        pltpu.VMEM((PAST_LEN, 2 * HEAD_DIM), BF16),
                        pltpu.VMEM((2, PAST_LEN, 2 * HEAD_DIM), BF16),
                        pltpu.VMEM((2, DEC_SEQ, 2 * HEAD_DIM), BF16)],
        compiler_params=_cparams(("parallel", "arbitrary")),
        name="na_attn",
    )(proj, proj, proj, cache_k, cache_v, col_table)


N_POW = 24


PREP_PPS = 2


def _s5_prep_body(*refs):
    for n in range(PREP_PPS):
        _s5_prep_pair(PREP_PPS * pl.program_id(0) + n, *[r.at[n] for r in refs])


def _s5_prep_pair(p, vec_ref, bt_ref, c_ref, wend_ref, ktoe_ref, woutT_ref, apow_ref, pw_ref, pct_ref):
    L = S5_CHUNK
    lane_blk = lax.broadcasted_iota(jnp.int32, (S5_CH, S5_W), 1) // S5_CH
    src_blk = (lane_blk + L - p) % L
    krows = []
    for d in range(2):
        v = vec_ref[d]
        lre, lim, step = v[0:1], v[1:2], v[2:3]
        a = lre * step
        b = lim * step
        ea = jnp.exp(a)
        nr = ea * jnp.cos(b) - 1.0
        ni = ea * jnp.sin(b)
        den = lre * lre + lim * lim
        qr = (nr * lre + ni * lim) / den
        qi = (ni * lre - nr * lim) / den
        bre, bim = bt_ref[d, 0], bt_ref[d, 1]
        bqr = bre * qr - bim * qi
        bqi = bre * qi + bim * qr
        cre, cim = c_ref[d, 0], c_ref[d, 1]

        e = lax.broadcasted_iota(jnp.int32, (N_POW, 1), 0).astype(F32)
        mag = jnp.exp(e * a)
        pw_ref[0] = mag * jnp.cos(e * b)
        pw_ref[1] = mag * jnp.sin(e * b)
        mag = jnp.exp((L * e) * a)
        apow_ref[d, 0] = mag * jnp.cos((L * e) * b)
        apow_ref[d, 1] = mag * jnp.sin((L * e) * b)

        def power(k):
            return pw_ref[0, k:k + 1, :], pw_ref[1, k:k + 1, :]

        def c_block(k):
            pr, pi = power(k)
            return cre * pr - cim * pi, -(cre * pi + cim * pr)

        for i in range(L):
            rows = pl.ds(pl.multiple_of(((i + p) % L) * S5_CH, S5_CH), S5_CH)
            pr, pi = power(L - 1 - i if d == 0 else i)
            wend_ref[rows, d * 256:d * 256 + 128] = (bqr * pr - bqi * pi).astype(BF16)
            wend_ref[rows, d * 256 + 128:d * 256 + 256] = (bqr * pi + bqi * pr).astype(BF16)
            mre, mim = c_block(i + 1 if d == 0 else L - i)
            woutT_ref[rows, d * 256:d * 256 + 128] = mre.astype(BF16)
            woutT_ref[rows, d * 256 + 128:d * 256 + 256] = mim.astype(BF16)
            lag_rows = slice(i * S5_CH, (i + 1) * S5_CH)
            mre, mim = c_block(i if d == 0 else L - 1 - i)
            pct_ref[lag_rows, 0:128] = mre
            pct_ref[lag_rows, 128:256] = mim

        bqcat = jnp.concatenate([bqr, bqi], axis=1)
        krows.append(_dot_nt(bqcat, pct_ref[...], precision=lax.Precision.HIGHEST))

    base_f = pltpu.roll(krows[0], p * S5_CH, 1)
    base_b = pltpu.roll(krows[1], p * S5_CH, 1)
    for i in range(L):
        rows = pl.ds(pl.multiple_of(((i + p) % L) * S5_CH, S5_CH), S5_CH)
        blk_f = base_f if i == 0 else pltpu.roll(base_f, i * S5_CH, 1)
        blk_b = base_b if (i + 1) % L == 0 else pltpu.roll(base_b, ((i + 1) % L) * S5_CH, 1)
        blk = jnp.where(src_blk >= i, blk_f, 0.0) + jnp.where(src_blk <= i, blk_b, 0.0)
        ktoe_ref[rows, :] = blk.astype(BF16)


def _s5_prep(lam_re, lam_im, log_step, b_re, b_im, c_re, c_im):
    def pair_vec(a):
        return a.astype(F32).reshape(2, S5_PAIRS, 1, S5_P).transpose(1, 0, 2, 3)

    step = jnp.broadcast_to(jnp.exp(log_step.astype(F32))[:, :, None], (2, N_SSM_GROUPS, SSM_STATE))
    vec = jnp.concatenate([pair_vec(lam_re), pair_vec(lam_im), pair_vec(step),
                           jnp.zeros((S5_PAIRS, 2, 5, S5_P), F32)], axis=2)

    def block_diag_ch_by_state(m):
        m = m.astype(F32).reshape(2, S5_PAIRS, 2, SSM_GROUP, SSM_STATE)
        eye = jnp.eye(2, dtype=F32)
        out = m[:, :, :, :, None, :] * eye[None, None, :, None, :, None]
        return out.reshape(2, S5_PAIRS, S5_CH, S5_P).transpose(1, 0, 2, 3)

    bt = jnp.stack([block_diag_ch_by_state(b_re.transpose(0, 1, 3, 2)),
                    block_diag_ch_by_state(b_im.transpose(0, 1, 3, 2))], axis=2)
    cc = jnp.stack([block_diag_ch_by_state(c_re), block_diag_ch_by_state(c_im)], axis=2)

    mat = jax.ShapeDtypeStruct((S5_PAIRS, S5_W, S5_W), BF16)
    mat_spec = pl.BlockSpec((PREP_PPS, S5_W, S5_W), lambda p: (p, 0, 0))
    return pl.pallas_call(
        _s5_prep_body,
        grid=(S5_PAIRS // PREP_PPS,),
        in_specs=[
            pl.BlockSpec((PREP_PPS, 2, 8, S5_P), lambda p: (p, 0, 0, 0)),
            pl.BlockSpec((PREP_PPS, 2, 2, S5_CH, S5_P), lambda p: (p, 0, 0, 0, 0)),
            pl.BlockSpec((PREP_PPS, 2, 2, S5_CH, S5_P), lambda p: (p, 0, 0, 0, 0)),
        ],
        out_specs=[mat_spec, mat_spec, mat_spec,
                   pl.BlockSpec((PREP_PPS, 2, 2, N_POW, S5_P), lambda p: (p, 0, 0, 0, 0))],
        out_shape=[mat, mat, mat, jax.ShapeDtypeStruct((S5_PAIRS, 2, 2, N_POW, S5_P), F32)],
        scratch_shapes=[pltpu.VMEM((PREP_PPS, 2, N_POW, S5_P), F32), pltpu.VMEM((PREP_PPS, S5_W, 2 * S5_P), F32)],
        compiler_params=_cparams(("parallel",)),
        name="s5_prep",
    )(vec, bt, cc)


S5_SEG_ROWS = 256
S5_N_SEG = N_CHUNK_ROWS // S5_SEG_ROWS
S5_SUB = S5_SEG_ROWS // S5_CHUNK
S5_SUB_PER_SEQ = (DEC_SEQ // S5_CHUNK) // S5_CHUNK


S5_PPS = 4


def _s5_body(tu_ref, s0_ref, wend_ref, ktoe_ref, woutT_ref, apow_ref, ty_ref, fin_ref, z_ref, xs_ref, g_ref):
    seg = pl.program_id(0)
    L = S5_CHUNK
    lane_blk = lax.broadcasted_iota(jnp.int32, (S5_SEG_ROWS, 128), 1) // S5_CH
    pairs = [S5_PPS * pl.program_id(1) + n for n in range(S5_PPS)]

    s_rows, ends = [], []
    for n, p in enumerate(pairs):
        tiles = []
        for t in range(LANE_TILES):
            tile = None
            for k in range(128 // S5_CH):
                j = (LANE_TILES * t + k + L - p) % L
                blk = tu_ref[j, t]
                tile = blk if tile is None else jnp.where(lane_blk == k, blk, tile)
            tiles.append(tile)
        s = jnp.concatenate(tiles, axis=1)
        s_rows.append(s)
        z = _dot(s, wend_ref[n])
        for q in range(LANE_TILES):
            z_ref[n, q] = z[:, q * 128:(q + 1) * 128]

        pair_ends = []
        for d in range(2):
            are, aim = apow_ref[n, d, 0, 1:2, :], apow_ref[n, d, 1, 1:2, :]
            xr = jnp.zeros((S5_SUB, S5_P), F32)
            xi = jnp.zeros((S5_SUB, S5_P), F32)
            for c in (range(L) if d == 0 else reversed(range(L))):
                rows = pl.ds(c, S5_SUB, stride=L)
                xs_ref[n, 2 * d, rows, :] = xr
                xs_ref[n, 2 * d + 1, rows, :] = xi
                zr = z_ref[n, 2 * d, rows, :]
                zi = z_ref[n, 2 * d + 1, rows, :]
                xr, xi = are * xr - aim * xi + zr, are * xi + aim * xr + zi
            pair_ends.append((xr, xi))
            fin_ref[n, d, 0] = xr
            fin_ref[n, d, 1] = xi
        ends.append(pair_ends)

    @pl.when(seg > 0)
    def _():
        for n in range(S5_PPS):
            for d in range(2):
                er, ei = ends[n][d]
                a16r, a16i = apow_ref[n, d, 0, L:L + 1, :], apow_ref[n, d, 1, L:L + 1, :]
                for bb in range(S5_SUB // S5_SUB_PER_SEQ):
                    gr = s0_ref[n, bb, 2 * d:2 * d + 1, :]
                    gi = s0_ref[n, bb, 2 * d + 1:2 * d + 2, :]
                    order = range(S5_SUB_PER_SEQ) if d == 0 else reversed(range(S5_SUB_PER_SEQ))
                    prev = None
                    for sub in order:
                        row = bb * S5_SUB_PER_SEQ + sub
                        if prev is not None:
                            gr, gi = (a16r * gr - a16i * gi + er[prev:prev + 1, :],
                                      a16r * gi + a16i * gr + ei[prev:prev + 1, :])
                        g_ref[n, 0, row:row + 1, :] = gr
                        g_ref[n, 1, row:row + 1, :] = gi
                        prev = row
                if d == 0:
                    pr, pi = apow_ref[n, d, 0, 0:L, :], apow_ref[n, d, 1, 0:L, :]
                else:
                    pr = jnp.concatenate([apow_ref[n, d, 0, k:k + 1, :] for k in reversed(range(L))], axis=0)
                    pi = jnp.concatenate([apow_ref[n, d, 1, k:k + 1, :] for k in reversed(range(L))], axis=0)
                for sub in range(S5_SUB):
                    rows = slice(sub * L, (sub + 1) * L)
                    gr = g_ref[n, 0, sub:sub + 1, :]
                    gi = g_ref[n, 1, sub:sub + 1, :]
                    xs_ref[n, 2 * d, rows, :] = xs_ref[n, 2 * d, rows, :] + (pr * gr - pi * gi)
                    xs_ref[n, 2 * d + 1, rows, :] = xs_ref[n, 2 * d + 1, rows, :] + (pr * gi + pi * gr)

    @pl.when(pl.program_id(1) == 0)
    def _():
        ty_ref[...] = jnp.zeros_like(ty_ref)

    for n, p in enumerate(pairs):
        xs = jnp.concatenate([xs_ref[n, q].astype(BF16) for q in range(LANE_TILES)], axis=1)
        y = _dot(s_rows[n], ktoe_ref[n]) + _dot_nt(xs, woutT_ref[n])
        for t in range(LANE_TILES):
            yt = y[:, t * 128:(t + 1) * 128]
            for k in range(128 // S5_CH):
                j = (LANE_TILES * t + k + L - p) % L
                ty_ref[j, t] = jnp.where(lane_blk == k, yt, ty_ref[j, t])


def _s5_scan(tu, s0, wend, ktoe, woutT, apow):
    mat_spec = pl.BlockSpec((S5_PPS, S5_W, S5_W), lambda g, p: (p, 0, 0))
    seqs = S5_SUB // S5_SUB_PER_SEQ
    return pl.pallas_call(
        _s5_body,
        grid=(S5_N_SEG, S5_PAIRS // S5_PPS),
        in_specs=[
            pl.BlockSpec((S5_CHUNK, LANE_TILES, S5_SEG_ROWS, 128), lambda g, p: (0, 0, g, 0)),
            pl.BlockSpec((S5_PPS, seqs, 4, S5_P), lambda g, p: (p, jnp.maximum(g - 1, 0), 0, 0)),
            mat_spec, mat_spec, mat_spec,
            pl.BlockSpec((S5_PPS, 2, 2, N_POW, S5_P), lambda g, p: (p, 0, 0, 0, 0)),
        ],
        out_specs=[
            pl.BlockSpec((S5_CHUNK, LANE_TILES, S5_SEG_ROWS, 128), lambda g, p: (0, 0, g, 0)),
            pl.BlockSpec((S5_PPS, 2, 2, S5_SUB, S5_P), lambda g, p: (p, 0, 0, g, 0)),
        ],
        out_shape=[
            jax.ShapeDtypeStruct((S5_CHUNK, LANE_TILES, N_CHUNK_ROWS, 128), F32),
            jax.ShapeDtypeStruct((S5_PAIRS, 2, 2, S5_N_SEG * S5_SUB, S5_P), F32),
        ],
        scratch_shapes=[pltpu.VMEM((S5_PPS, LANE_TILES, S5_SEG_ROWS, 128), F32),
                        pltpu.VMEM((S5_PPS, LANE_TILES, S5_SEG_ROWS, 128), F32),
                        pltpu.VMEM((S5_PPS, 2, S5_SUB, S5_P), F32)],
        compiler_params=_cparams(("parallel", "arbitrary")),
        name="s5_scan",
    )(tu, s0, wend, ktoe, woutT, apow)


TM_GLU = 2048


def _glu_body(tu_ref, ty_ref, d_ref, gw_ref, gb_ref, o_ref, out_ref):
    rows = TM_GLU // S5_CHUNK
    d = d_ref[...]
    pre = []
    for j in range(S5_CHUNK):
        uj = jnp.concatenate([tu_ref[j, t].astype(F32) for t in range(LANE_TILES)], axis=1)
        yj = jnp.concatenate([ty_ref[j, t] for t in range(LANE_TILES)], axis=1)
        dj = pltpu.roll(d, j * S5_CH, 1) if j else d
        yj = dj * uj + yj
        back = ((S5_CHUNK - j) % S5_CHUNK) * S5_CH
        pre.append(pltpu.roll(yj, back, 1) if back else yj)
    y = jnp.concatenate(pre, axis=0)
    cdf = 0.5 * (1.0 + jnp.tanh(math.sqrt(2.0 / math.pi) * (y + 0.044715 * (y * y * y))))
    y = y * cdf
    gate = jax.nn.sigmoid(_dot(y.astype(BF16), gw_ref[...]) + gb_ref[...])
    out = y * gate
    for j in range(S5_CHUNK):
        oj = out[j * rows:(j + 1) * rows]
        for t in range(LANE_TILES):
            out_ref[t, pl.ds(j, rows, stride=S5_CHUNK), :] = oj[:, t * 128:(t + 1) * 128]
    o_ref[...] = jnp.concatenate([out_ref[t] for t in range(LANE_TILES)], axis=1).astype(BF16)


def _glu(tu, ty, ssm_d, glu_w, glu_b):
    tm = TM_GLU
    rows = tm // S5_CHUNK
    lay_spec = pl.BlockSpec((S5_CHUNK, LANE_TILES, rows, 128), lambda i: (0, 0, i, 0))
    return pl.pallas_call(
        _glu_body,
        grid=(N_TOK // tm,),
        in_specs=[
            lay_spec, lay_spec,
            pl.BlockSpec((1, HALF), lambda i: (0, 0)),
            pl.BlockSpec((HALF, HALF), lambda i: (0, 0)),
            pl.BlockSpec((1, HALF), lambda i: (0, 0)),
        ],
        out_specs=pl.BlockSpec((tm, HALF), lambda i: (i, 0)),
        out_shape=jax.ShapeDtypeStruct((N_TOK, HALF), BF16),
        scratch_shapes=[pltpu.VMEM((LANE_TILES, tm, 128), F32)],
        compiler_params=_cparams(("parallel",)),
        name="s5_glu",
    )(tu, ty, ssm_d.reshape(1, HALF), glu_w.astype(BF16), glu_b.reshape(1, HALF))


def kernel(x_prompt, x_sample, c, cache_k, cache_v, state_s5_re, state_s5_im, c_ctx, norm1_g, norm2_g, ada_w, ada_b, mlp_w1, mlp_w2, ab_w_in, pool_w, pool_scale, conv_w, ab_w_out, cd_w_in, na_rpb, ssm_lambda_re, ssm_lambda_im, ssm_log_step, ssm_b_re, ssm_b_im, ssm_c_re, ssm_c_im, ssm_d, glu_w, glu_b, cd_w_out, final_g):
    depth = ada_w.shape[0]
    xs = (x_prompt.reshape(N_PROMPT_TOK, D_MODEL), x_sample.reshape(N_SAMPLE_TOK, D_MODEL))
    cond = jnp.concatenate([c_ctx[None, :], c, jnp.zeros((N_COND - 1 - DEC_BATCH, D_MODEL), F32)], axis=0)
    mods = _adaln(cond, ada_w, ada_b).reshape(depth, N_COND, 6, D_MODEL)

    new_k, new_v, new_re, new_im = [], [], [], []
    for layer in range(depth):
        i = layer // 2
        final = layer == depth - 1
        if layer % 2 == 0:
            y, x = _even_layer_mix(xs, mods, layer, norm1_g[layer], ab_w_in, i,
                                   pool_w[i], pool_scale[i], conv_w[i])
            ya, ya_col, yb, yb_col = y, 0, y, 1
            w_out_stack = ab_w_out
        else:
            x = xs[0]
            qkv, tu = _inproj_odd(x, mods, layer, norm1_g[layer], cd_w_in, i)
            o_p, ck, cv = _ctx_attn(qkv)
            new_k.append(ck)
            new_v.append(cv)
            o_s = _na_attn(qkv, cache_k, cache_v, _na_col_table(na_rpb[i]), i)
            attn = (o_p, o_s)

            wend, ktoe, woutT, apow = _s5_prep(ssm_lambda_re[i], ssm_lambda_im[i], ssm_log_step[i],
                                               ssm_b_re[i], ssm_b_im[i], ssm_c_re[i], ssm_c_im[i])

            def pair_state(s):
                return s.astype(F32).reshape(DEC_BATCH, 2, S5_PAIRS, S5_P).transpose(2, 0, 1, 3)

            s0 = jnp.stack([pair_state(state_s5_re[:, i]), pair_state(state_s5_im[:, i])], axis=3)
            s0 = s0.reshape(S5_PAIRS, DEC_BATCH, 4, S5_P)
            ty, fin = _s5_scan(tu, s0, wend, ktoe, woutT, apow)
            fin = fin[:, :, :, :BATCH]
            new_re.append(fin[:, :, 0].transpose(2, 1, 0, 3).reshape(BATCH, 2, N_SSM_GROUPS, SSM_STATE))
            new_im.append(fin[:, :, 1].transpose(2, 1, 0, 3).reshape(BATCH, 2, N_SSM_GROUPS, SSM_STATE))
            d_out = _glu(tu, ty, ssm_d[i], glu_w[i], glu_b[i])
            ya, ya_col, yb, yb_col = attn, 0, d_out, 0
            w_out_stack = cd_w_out
        xs = _mlp(x, ya, ya_col, yb, yb_col, w_out_stack, i, mods, layer, norm2_g[layer], final_g,
                  mlp_w1, mlp_w2, final)

    y_prompt = xs[0].reshape(BATCH, SEQ, D_MODEL)
    y_sample = xs[1].reshape(DEC_BATCH, DEC_SEQ, D_MODEL)
    return (y_prompt, y_sample, jnp.concatenate(new_k, axis=1), jnp.concatenate(new_v, axis=1),
            jnp.stack(new_re, axis=1), jnp.stack(new_im, axis=1))
```

```python
import functools
import math

import jax
import jax.numpy as jnp
from jax import lax
from jax.experimental import pallas as pl
from jax.experimental.pallas import tpu as pltpu

F32 = jnp.float32
BF16 = jnp.bfloat16

D_MODEL = 1024
BATCH = 16
SEQ = 256
DEC_BATCH = 4
DEC_SEQ = 2048
PAST_LEN = 256
GRID_W = 64
GRID_ROWS = DEC_SEQ // GRID_W
HALF = 512
POOL_WINDOWS = (2, 4, 8, 16)
POOL_GROUP = 128
HEAD_DIM = 64
N_HEADS = 8
WIN_H = 8
WIN_W = 16
SSM_GROUP = 16
N_SSM_GROUPS = 32
SSM_STATE = 64
D_FF = 4096
EPS = 1e-6

N_PROMPT_TOK = BATCH * SEQ
N_SAMPLE_TOK = DEC_BATCH * DEC_SEQ
N_TOK = N_PROMPT_TOK + N_SAMPLE_TOK
N_COND = 8

S5_CHUNK = 16
S5_PAIRS = N_SSM_GROUPS // 2
S5_CH = 2 * SSM_GROUP
S5_P = 2 * SSM_STATE
S5_W = S5_CHUNK * S5_CH

VMEM_LIMIT = 56 * 1024 * 1024


def _cparams(sem):
    return pltpu.CompilerParams(dimension_semantics=sem, vmem_limit_bytes=VMEM_LIMIT)


def _cond_row(i, tm):
    npt = N_PROMPT_TOK // tm
    per = DEC_SEQ // tm
    return jnp.where(i < npt, 0, 1 + (i - npt) // per)


def _resident_layer(shape, idx):
    zeros = (0,) * len(shape)
    return pl.BlockSpec((None,) + tuple(shape), lambda i: (idx,) + zeros, pipeline_mode=pl.Buffered(1))


def _normmod(x, g, shift, scale):
    ms = jnp.mean(x * x, axis=-1, keepdims=True)
    return (x * lax.rsqrt(ms + EPS) * g) * (1.0 + scale) + shift


def _dot(a, b):
    return jnp.dot(a, b, preferred_element_type=F32)


def _dot_nt(a, b, precision=None):
    return lax.dot_general(a, b, (((1,), (1,)), ((), ())), preferred_element_type=F32, precision=precision)


def _adaln_body(c_ref, w_ref, b_ref, o_ref):
    c = c_ref[...]
    s = c * jax.nn.sigmoid(c)
    o_ref[...] = _dot(s.astype(BF16), w_ref[...].astype(BF16)) + b_ref[...]


def _adaln(cond, ada_w, ada_b):
    depth = ada_w.shape[0]
    nj = 6
    return pl.pallas_call(
        _adaln_body,
        grid=(depth, nj),
        in_specs=[
            pl.BlockSpec((N_COND, D_MODEL), lambda l, j: (0, 0)),
            pl.BlockSpec((None, D_MODEL, D_MODEL), lambda l, j: (l, 0, j)),
            pl.BlockSpec((None, 1, D_MODEL), lambda l, j: (l, 0, j)),
        ],
        out_specs=pl.BlockSpec((None, N_COND, D_MODEL), lambda l, j: (l, 0, j)),
        out_shape=jax.ShapeDtypeStruct((depth, N_COND, 6 * D_MODEL), F32),
        compiler_params=_cparams(("arbitrary", "arbitrary")),
        name="adaln",
    )(cond, ada_w, ada_b.reshape(depth, 1, 6 * D_MODEL))


TM_PROJ = 1024


def _stream_tile(x_refs, tm):
    if len(x_refs) == 1:
        return x_refs[0][...]
    is_prompt = pl.program_id(0) < N_PROMPT_TOK // tm
    return jnp.where(is_prompt, x_refs[0][...], x_refs[1][...])


N_CHUNK_ROWS = N_TOK // S5_CHUNK
LANE_TILES = S5_W // 128


def _inproj_odd_body(x_ref, m_ref, g_ref, w_ref, qkv_ref, tu_ref, u_ref, wbf_ref):
    @pl.when(pl.program_id(0) == 0)
    def _():
        wbf_ref[...] = w_ref[...].astype(BF16)

    m = m_ref[...]
    h = _normmod(x_ref[...], g_ref[...], m[0:1], m[1:2])
    acc = _dot(h.astype(BF16), wbf_ref[...])
    qkv_ref[...] = acc[:, 0:3 * HALF].astype(BF16)
    for t in range(LANE_TILES):
        u_ref[t] = acc[:, 3 * HALF + t * 128:3 * HALF + (t + 1) * 128]
    rows = TM_PROJ // S5_CHUNK
    for j in range(S5_CHUNK):
        uj = jnp.concatenate([u_ref[t, pl.ds(j, rows, stride=S5_CHUNK), :] for t in range(LANE_TILES)], axis=1)
        if j:
            uj = pltpu.roll(uj, j * S5_CH, 1)
        for t in range(LANE_TILES):
            tu_ref[j, t] = uj[:, t * 128:(t + 1) * 128].astype(BF16)


def _inproj_odd(x, mods, layer, g, w_stack, w_idx):
    tm = TM_PROJ
    rows = tm // S5_CHUNK
    return pl.pallas_call(
        _inproj_odd_body,
        grid=(N_TOK // tm,),
        in_specs=[
            pl.BlockSpec((tm, D_MODEL), lambda i: (i, 0)),
            pl.BlockSpec((None, None, 6, D_MODEL), lambda i: (layer, _cond_row(i, tm), 0, 0)),
            pl.BlockSpec((1, D_MODEL), lambda i: (0, 0)),
            _resident_layer((D_MODEL, 4 * HALF), w_idx),
        ],
        out_specs=[
            pl.BlockSpec((tm, 3 * HALF), lambda i: (i, 0)),
            pl.BlockSpec((S5_CHUNK, LANE_TILES, rows, 128), lambda i: (0, 0, i, 0)),
        ],
        out_shape=[
            jax.ShapeDtypeStruct((N_TOK, 3 * HALF), BF16),
            jax.ShapeDtypeStruct((S5_CHUNK, LANE_TILES, N_CHUNK_ROWS, 128), BF16),
        ],
        scratch_shapes=[pltpu.VMEM((LANE_TILES, tm, 128), F32), pltpu.VMEM((D_MODEL, 4 * HALF), BF16)],
        compiler_params=_cparams(("arbitrary",)),
        name="inproj_odd",
    )(x, mods, g.reshape(1, D_MODEL), w_stack)


TM_MIX = 256
HALO = 16
RING = 3
LAG = 2
PAD_ROWS = TM_MIX + 2 * HALO + 8


def _even_step(phase, x_refs, m_ref, g_ref, w_ref, pw_ref, ps_ref, cw_ref, y_ref, xcat_ref, proj_ref, tail_ref,
               h_ref):
    i = pl.program_id(0)
    write_slot, cur_slot, next_slot = phase, (phase + 1) % RING, (phase + 2) % RING

    m = m_ref[...]
    x = _stream_tile(x_refs, TM_MIX)
    xcat_ref[...] = x
    h_ref[...] = _normmod(x, g_ref[...], m[0:1], m[1:2]).astype(BF16)
    n_pieces = len(POOL_WINDOWS) + HALF // 128
    pw = (4 * HALF) // n_pieces

    def project(piece):
        cols = slice(piece * pw, (piece + 1) * pw)
        proj_ref[write_slot, :, cols] = _dot(h_ref[...], w_ref[:, cols])

    t = jnp.maximum(i - LAG, 0)
    n_prompt_tiles = N_PROMPT_TOK // TM_MIX
    tiles_per_seq = DEC_SEQ // TM_MIX
    is_prompt = t < n_prompt_tiles
    tile_in_seq = jnp.where(is_prompt, 0, (t - n_prompt_tiles) % tiles_per_seq)
    seq_len = jnp.where(is_prompt, SEQ, DEC_SEQ)
    has_prev = tile_in_seq > 0
    has_next = jnp.logical_and(jnp.logical_not(is_prompt), tile_in_seq < tiles_per_seq - 1)
    prev_keep = jnp.where(has_prev, 1.0, 0.0).astype(F32)
    next_keep = jnp.where(has_next, 1.0, 0.0).astype(F32)
    pos = tile_in_seq * TM_MIX + lax.broadcasted_iota(jnp.int32, (TM_MIX, 1), 0)
    lo_rows, hi_rows = HALO, HALO + TM_MIX
    head = slice(0, HALO)

    slack = jnp.zeros((PAD_ROWS - TM_MIX - 2 * HALO, POOL_GROUP), F32)
    for gi, w in enumerate(POOL_WINDOWS):
        project(gi)
        cs = slice(gi * POOL_GROUP, (gi + 1) * POOL_GROUP)
        cur = proj_ref[cur_slot, :, cs]
        f = jnp.concatenate([tail_ref[:, cs] * prev_keep, cur, proj_ref[next_slot, head, cs] * next_keep, slack],
                            axis=0)
        n_rows, span = PAD_ROWS, 1
        while 2 * span < w:
            n_rows -= 8
            f = f[0:n_rows] + f[span:span + n_rows]
            span *= 2
        s = f[lo_rows - span:hi_rows - span] + f[lo_rows:hi_rows]
        lo = jnp.maximum(pos - w // 2, 0)
        hi = jnp.minimum(pos + (w - w // 2), seq_len)
        inv = 1.0 / (hi - lo).astype(F32)
        pooled = s * inv - cur
        mixed = _dot(pooled.astype(BF16), pw_ref[gi]) * ps_ref[:, cs]
        y_ref[:, cs] = mixed.astype(BF16)

    for ci in range(HALF // 128):
        project(len(POOL_WINDOWS) + ci)
        cs = slice(ci * 128, (ci + 1) * 128)
        bs = slice(HALF + ci * 128, HALF + (ci + 1) * 128)
        gs = slice(2 * HALF + ci * 128, 2 * HALF + (ci + 1) * 128)
        vs = slice(3 * HALF + ci * 128, 3 * HALF + (ci + 1) * 128)
        z = jnp.concatenate([(tail_ref[:, gs] * tail_ref[:, vs]) * prev_keep,
                             proj_ref[cur_slot, :, gs] * proj_ref[cur_slot, :, vs],
                             (proj_ref[next_slot, head, gs] * proj_ref[next_slot, head, vs]) * next_keep], axis=0)
        conv = (cw_ref[0:1, cs] * z[lo_rows - 1:hi_rows - 1]
                + cw_ref[1:2, cs] * z[lo_rows:hi_rows]
                + cw_ref[2:3, cs] * z[lo_rows + 1:hi_rows + 1])
        y_ref[:, bs] = (proj_ref[cur_slot, :, bs] * conv).astype(BF16)

    tail_ref[...] = proj_ref[cur_slot, TM_MIX - HALO:TM_MIX, :]


def _even_body(n_x, *refs):
    x_refs, rest = refs[:n_x], refs[n_x:]
    m_ref, g_ref, w_ref = rest[:3]
    proj_ref, tail_ref, h_ref, wbf_ref = rest[-4:]
    i = pl.program_id(0)

    @pl.when(i == 0)
    def _():
        proj_ref[...] = jnp.zeros_like(proj_ref)
        tail_ref[...] = jnp.zeros_like(tail_ref)
        wbf_ref[...] = w_ref[...].astype(BF16)

    for phase in range(RING):
        @pl.when(i % RING == phase)
        def _(phase=phase):
            _even_step(phase, x_refs, m_ref, g_ref, wbf_ref, *rest[3:-1])


def _even_layer_mix(xs, mods, layer, g, w_stack, w_idx, pool_w, pool_scale, conv_w):
    tm = TM_MIX
    n_tiles = N_TOK // tm
    npt = N_PROMPT_TOK // tm
    last = n_tiles - 1
    if len(xs) == 1:
        x_specs = [pl.BlockSpec((tm, D_MODEL), lambda i: (jnp.minimum(i, last), 0))]
    else:
        x_specs = [pl.BlockSpec((tm, D_MODEL), lambda i: (jnp.minimum(i, npt - 1), 0)),
                   pl.BlockSpec((tm, D_MODEL), lambda i: (jnp.clip(i - npt, 0, last - npt), 0))]
    return pl.pallas_call(
        functools.partial(_even_body, len(xs)),
        grid=(n_tiles + LAG,),
        in_specs=x_specs + [
            pl.BlockSpec((None, None, 6, D_MODEL), lambda i: (layer, _cond_row(jnp.minimum(i, last), tm), 0, 0)),
            pl.BlockSpec((1, D_MODEL), lambda i: (0, 0)),
            _resident_layer((D_MODEL, 4 * HALF), w_idx),
            pl.BlockSpec((len(POOL_WINDOWS), POOL_GROUP, POOL_GROUP), lambda i: (0, 0, 0)),
            pl.BlockSpec((1, HALF), lambda i: (0, 0)),
            pl.BlockSpec((3, HALF), lambda i: (0, 0)),
        ],
        out_specs=[pl.BlockSpec((tm, 2 * HALF), lambda i: (jnp.maximum(i - LAG, 0), 0)),
                   pl.BlockSpec((tm, D_MODEL), lambda i: (jnp.minimum(i, last), 0))],
        out_shape=[jax.ShapeDtypeStruct((N_TOK, 2 * HALF), BF16), jax.ShapeDtypeStruct((N_TOK, D_MODEL), F32)],
        scratch_shapes=[pltpu.VMEM((RING, tm, 4 * HALF), F32), pltpu.VMEM((HALO, 4 * HALF), F32),
                        pltpu.VMEM((tm, D_MODEL), BF16), pltpu.VMEM((D_MODEL, 4 * HALF), BF16)],
        compiler_params=_cparams(("arbitrary",)),
        name="even_layer_mix",
    )(*xs, mods, g.reshape(1, D_MODEL), w_stack, pool_w.astype(BF16), pool_scale.reshape(1, HALF), conv_w)


TM_MLP = 1024
RC_MLP = 256
N_CAST = 16


def _mlp_tile(final, x_ref, ya_ref, yb_ref, m_ref, g_ref, fg_ref, o_ref, wo_ref, w1_ref, w2_ref):
    m = m_ref[...]
    gain = g_ref[...] * (1.0 + m[4:5])
    for r in range(TM_MLP // RC_MLP):
        rows = slice(r * RC_MLP, (r + 1) * RC_MLP)
        mix = _dot(ya_ref[rows, :], wo_ref[0:HALF, :]) + _dot(yb_ref[rows, :], wo_ref[HALF:2 * HALF, :])
        x1 = x_ref[rows, :] + m[2:3] * mix
        ms = jnp.mean(x1 * x1, axis=-1, keepdims=True)
        h = (x1 * lax.rsqrt(ms + EPS) * gain + m[3:4]).astype(BF16)
        t = _dot(h, w1_ref[...])
        t = jnp.square(jnp.maximum(t, 0.0)).astype(BF16)
        out = x1 + m[5:6] * _dot(t, w2_ref[...])
        if final:
            ms = jnp.mean(out * out, axis=-1, keepdims=True)
            out = out * lax.rsqrt(ms + EPS) * fg_ref[...]
        o_ref[rows, :] = out


def _mlp_resident_body(final, x_ref, ya_ref, yb_ref, wo_ref, m_ref, g_ref, fg_ref, w1_ref, w2_ref, o_ref):
    _mlp_tile(final, x_ref, ya_ref, yb_ref, m_ref, g_ref, fg_ref, o_ref, wo_ref, w1_ref, w2_ref)


def _mlp_cast_body(final, emit, x_ref, ya_ref, yb_ref, wo_f, m_ref, g_ref, fg_ref, w1_f, w2_f, o_ref, *rest):
    wo_ref, w1_ref, w2_ref = rest[-3:]
    emitted = rest[:-3]
    i = pl.program_id(0)

    @pl.when(i < N_CAST)
    def _():
        for k, (f_ref, b_ref) in enumerate(((wo_f, wo_ref), (w1_f, w1_ref), (w2_f, w2_ref))):
            n = f_ref.shape[0]
            slab = f_ref[...].astype(BF16)
            b_ref[pl.ds(pl.multiple_of(i * n, n), n), :] = slab
            if emit:
                emitted[k][...] = slab

    @pl.when(i >= N_CAST)
    def _():
        _mlp_tile(final, x_ref, ya_ref, yb_ref, m_ref, g_ref, fg_ref, o_ref, wo_ref, w1_ref, w2_ref)


def _resident(shape):
    zeros = (0,) * len(shape)
    return pl.BlockSpec(shape, lambda i: zeros, pipeline_mode=pl.Buffered(1))


def _mlp_call(x, x_off, ya, ya_off, ya_col, yb, yb_off, yb_col, n_tiles, tile0, mods, layer, g2, final_g, final,
              f32_weights=None, bf16_weights=None, emit=False):
    tm = TM_MLP
    lead = N_CAST if f32_weights is not None else 0

    def tile(i):
        return jnp.maximum(i - lead, 0)

    stream_specs = [
        pl.BlockSpec((tm, D_MODEL), lambda i: (tile(i) + x_off, 0)),
        pl.BlockSpec((tm, HALF), lambda i: (tile(i) + ya_off, ya_col)),
        pl.BlockSpec((tm, HALF), lambda i: (tile(i) + yb_off, yb_col)),
    ]
    mod_spec = pl.BlockSpec((None, None, 6, D_MODEL), lambda i: (layer, _cond_row(tile(i) + tile0, tm), 0, 0))
    out_spec = pl.BlockSpec((tm, D_MODEL), lambda i: (tile(i), 0))
    out_shape = jax.ShapeDtypeStruct((n_tiles * tm, D_MODEL), F32)
    vecs = (g2.reshape(1, D_MODEL), final_g.reshape(1, D_MODEL))
    if f32_weights is None:
        w_out, w1, w2 = bf16_weights
        return pl.pallas_call(
            functools.partial(_mlp_resident_body, final),
            grid=(n_tiles,),
            in_specs=stream_specs + [_resident((D_MODEL, D_MODEL)), mod_spec, _resident((1, D_MODEL)),
                                     _resident((1, D_MODEL)), _resident((D_MODEL, D_FF)), _resident((D_FF, D_MODEL))],
            out_specs=out_spec,
            out_shape=out_shape,
            compiler_params=_cparams(("parallel",)),
            name="mlp",
        )(x, ya, yb, w_out, mods, *vecs, w1, w2)

    wo_stack, wo_idx, w1_stack, w2_stack = f32_weights

    def slab(i):
        return jnp.minimum(i, N_CAST - 1)

    shapes = ((D_MODEL, D_MODEL), (D_MODEL, D_FF), (D_FF, D_MODEL))
    slab_rows = [rows // N_CAST for rows, _ in shapes]
    f32_specs = [pl.BlockSpec((None, n, cols), lambda i, idx=idx: (idx, slab(i), 0))
                 for n, (_, cols), idx in zip(slab_rows, shapes, (wo_idx, layer, layer))]
    out_specs, out_shapes = [out_spec], [out_shape]
    if emit:
        out_specs += [pl.BlockSpec((n, cols), lambda i: (slab(i), 0)) for n, (_, cols) in zip(slab_rows, shapes)]
        out_shapes += [jax.ShapeDtypeStruct(shp, BF16) for shp in shapes]
    return pl.pallas_call(
        functools.partial(_mlp_cast_body, final, emit),
        grid=(N_CAST + n_tiles,),
        in_specs=stream_specs + [f32_specs[0], mod_spec, _resident((1, D_MODEL)), _resident((1, D_MODEL)),
                                 f32_specs[1], f32_specs[2]],
        out_specs=out_specs,
        out_shape=out_shapes,
        scratch_shapes=[pltpu.VMEM(shp, BF16) for shp in shapes],
        compiler_params=_cparams(("arbitrary",)),
        name="mlp_cast",
    )(x, ya, yb, wo_stack, mods, *vecs, w1_stack, w2_stack)


def _mlp(x, ya, ya_col, yb, yb_col, w_out_stack, w_out_idx, mods, layer, g2, final_g, w1_stack, w2_stack, final):
    tm = TM_MLP
    npt = N_PROMPT_TOK // tm
    f32_weights = (w_out_stack, w_out_idx, w1_stack, w2_stack)
    common = (mods, layer, g2, final_g, final)
    if not final:
        if isinstance(ya, tuple):
            ya = jnp.concatenate(ya, axis=0)
        return tuple(_mlp_call(x, 0, ya, 0, ya_col, yb, 0, yb_col, N_TOK // tm, 0, *common, f32_weights=f32_weights))
    ya_p, ya_s = ((ya[0], 0), (ya[1], 0)) if isinstance(ya, tuple) else ((ya, 0), (ya, npt))
    y_p, *bf16_weights = _mlp_call(x, 0, ya_p[0], ya_p[1], ya_col, yb, 0, yb_col, npt, 0, *common,
                                   f32_weights=f32_weights, emit=True)
    y_s = _mlp_call(x, npt, ya_s[0], ya_s[1], ya_col, yb, npt, yb_col, N_TOK // tm - npt, npt, *common,
                    bf16_weights=bf16_weights)
    return y_p, y_s


def _ctx_attn_body(q_ref, k_ref, v_ref, o_ref, ck_ref, cv_ref):
    lane = lax.broadcasted_iota(jnp.int32, (1, 2 * HEAD_DIM), 1)
    own = [lane < HEAD_DIM, lane >= HEAD_DIM]
    for h in range(N_HEADS):
        sl = slice(h * HEAD_DIM, (h + 1) * HEAD_DIM)
        ck_ref[h] = k_ref[:, sl].astype(F32)
        cv_ref[h] = v_ref[:, sl].astype(F32)
    for hp in range(N_HEADS // 2):
        sl = slice(hp * 2 * HEAD_DIM, (hp + 1) * 2 * HEAD_DIM)
        q = q_ref[:, sl] * jnp.asarray(HEAD_DIM ** -0.5, BF16)
        k = k_ref[:, sl]
        v = v_ref[:, sl]
        q2 = jnp.concatenate([jnp.where(own[0], q, jnp.zeros_like(q)),
                              jnp.where(own[1], q, jnp.zeros_like(q))], axis=0)
        s = _dot_nt(q2, k)
        m = jnp.max(s, axis=-1, keepdims=True)
        p = jnp.exp(s - m).astype(BF16)
        out = None
        for hh in range(2):
            num = _dot(p[hh * SEQ:(hh + 1) * SEQ], jnp.where(own[hh], v, jnp.ones_like(v)))
            den = jnp.where(own[hh], pltpu.roll(num, HEAD_DIM, 1), 1.0)
            o = num / den
            out = o if out is None else jnp.where(own[0], out, o)
        o_ref[:, sl] = out.astype(BF16)


def _ctx_attn(proj):
    return pl.pallas_call(
        _ctx_attn_body,
        grid=(BATCH,),
        in_specs=[
            pl.BlockSpec((SEQ, HALF), lambda b: (b, 0)),
            pl.BlockSpec((SEQ, HALF), lambda b: (b, 1)),
            pl.BlockSpec((SEQ, HALF), lambda b: (b, 2)),
        ],
        out_specs=[
            pl.BlockSpec((SEQ, HALF), lambda b: (b, 0)),
            pl.BlockSpec((None, None, N_HEADS, SEQ, HEAD_DIM), lambda b: (b, 0, 0, 0, 0)),
            pl.BlockSpec((None, None, N_HEADS, SEQ, HEAD_DIM), lambda b: (b, 0, 0, 0, 0)),
        ],
        out_shape=[
            jax.ShapeDtypeStruct((N_PROMPT_TOK, HALF), BF16),
            jax.ShapeDtypeStruct((BATCH, 1, N_HEADS, SEQ, HEAD_DIM), F32),
            jax.ShapeDtypeStruct((BATCH, 1, N_HEADS, SEQ, HEAD_DIM), F32),
        ],
        compiler_params=_cparams(("parallel",)),
        name="ctx_attn",
    )(proj, proj, proj)


NA_MASKED = -math.inf
NA_QROWS = 4
NA_KROWS = NA_QROWS + WIN_H
NA_NQ = NA_QROWS * GRID_W
NA_NK = NA_KROWS * GRID_W
NA_BLOCKS = GRID_ROWS // NA_QROWS
NA_TYPES = 3


def _na_col_table(rpb):
    qc = jnp.arange(GRID_W)[:, None]
    kc = jnp.arange(GRID_W)[None, :]
    c0 = jnp.clip(qc - WIN_W // 2, 0, GRID_W - WIN_W)
    valid = (kc >= c0) & (kc < c0 + WIN_W)
    coff = jnp.clip(kc - qc, -(WIN_W - 1), WIN_W - 1) + (WIN_W - 1)
    onehot = (coff.reshape(1, -1) == jnp.arange(2 * WIN_W - 1)[:, None]).astype(F32)
    t = jnp.dot(rpb.astype(F32).reshape(N_HEADS * (2 * WIN_H - 1), 2 * WIN_W - 1), onehot,
                precision=lax.Precision.HIGHEST)
    t = t.reshape(N_HEADS, 2 * WIN_H - 1, GRID_W, GRID_W)
    return jnp.where(valid[None, None], t, NA_MASKED)


NA_TYPE_Q0 = (0, NA_QROWS, GRID_ROWS - NA_QROWS)
NA_TYPE_KROWS = (WIN_H, NA_KROWS, WIN_H)


def _na_window_start(rb, krows):
    return min(max(NA_QROWS * rb - WIN_H // 2, 0), GRID_ROWS - krows)


def _na_body(q_ref, k_ref, v_ref, ck_ref, cv_ref, tab_ref, o_ref, bias_ref, kc_ref, vc_ref, va_ref):
    @pl.when(pl.program_id(1) == 0)
    def _():
        masked = jnp.full((GRID_W, GRID_W), NA_MASKED, F32)
        for hh in range(2):
            for ty in range(NA_TYPES):
                q0 = NA_TYPE_Q0[ty]
                ks = _na_window_start(q0 // NA_QROWS, NA_TYPE_KROWS[ty])
                for a in range(NA_QROWS):
                    r0 = min(max(q0 + a - WIN_H // 2, 0), GRID_ROWS - WIN_H)
                    for y in range(NA_TYPE_KROWS[ty]):
                        kr = ks + y
                        ok = r0 <= kr < r0 + WIN_H
                        blk = tab_ref[hh, kr - (q0 + a) + WIN_H - 1] if ok else masked
                        bias_ref[ty, hh * NA_NQ + a * GRID_W:hh * NA_NQ + (a + 1) * GRID_W,
                                 y * GRID_W:(y + 1) * GRID_W] = blk

    lane = lax.broadcasted_iota(jnp.int32, (1, 2 * HEAD_DIM), 1)
    own = [lane < HEAD_DIM, lane >= HEAD_DIM]
    kc_ref[...] = jnp.concatenate([ck_ref[0], ck_ref[1]], axis=1).astype(BF16)
    vc = jnp.concatenate([cv_ref[0], cv_ref[1]], axis=1)
    v = v_ref[...]
    for hh in range(2):
        vc_ref[hh] = jnp.where(own[hh], vc, 1.0).astype(BF16)
        va_ref[hh] = jnp.where(own[hh], v, jnp.ones_like(v))

    def block(rb, ty):
        qs = rb * NA_NQ
        nk = NA_TYPE_KROWS[ty] * GRID_W
        ks = _na_window_start(rb, NA_TYPE_KROWS[ty]) * GRID_W
        q = q_ref[pl.ds(qs, NA_NQ), :] * jnp.asarray(HEAD_DIM ** -0.5, BF16)
        q2 = jnp.concatenate([jnp.where(own[0], q, jnp.zeros_like(q)),
                              jnp.where(own[1], q, jnp.zeros_like(q))], axis=0)
        s_loc = _dot_nt(q2, k_ref[pl.ds(ks, nk), :]) + bias_ref[ty, :, 0:nk]
        s_ctx = _dot_nt(q2, kc_ref[...])
        m = jnp.maximum(jnp.max(s_loc, axis=-1, keepdims=True), jnp.max(s_ctx, axis=-1, keepdims=True))
        p_loc = jnp.exp(s_loc - m).astype(BF16)
        p_ctx = jnp.exp(s_ctx - m).astype(BF16)
        out = None
        for hh in range(2):
            rows = slice(hh * NA_NQ, (hh + 1) * NA_NQ)
            num = _dot(p_loc[rows], va_ref[hh, pl.ds(ks, nk), :]) + _dot(p_ctx[rows], vc_ref[hh])
            den = jnp.where(own[hh], pltpu.roll(num, HEAD_DIM, 1), 1.0)
            o = num / den
            out = o if out is None else jnp.where(own[0], out, o)
        o_ref[pl.ds(qs, NA_NQ), :] = out.astype(BF16)

    for rb in range(NA_BLOCKS):
        block(rb, 0 if rb == 0 else (2 if rb == NA_BLOCKS - 1 else 1))


def _na_attn(proj, cache_k, cache_v, col_table, layer_idx):
    row_blk0 = N_PROMPT_TOK // DEC_SEQ
    hp = N_HEADS // 2
    return pl.pallas_call(
        _na_body,
        grid=(hp, DEC_BATCH),
        in_specs=[
            pl.BlockSpec((DEC_SEQ, 128), lambda h, b: (row_blk0 + b, h)),
            pl.BlockSpec((DEC_SEQ, 128), lambda h, b: (row_blk0 + b, hp + h)),
            pl.BlockSpec((DEC_SEQ, 128), lambda h, b: (row_blk0 + b, 2 * hp + h)),
            pl.BlockSpec((None, None, 2, PAST_LEN, HEAD_DIM), lambda h, b: (b, layer_idx, h, 0, 0)),
            pl.BlockSpec((None, None, 2, PAST_LEN, HEAD_DIM), lambda h, b: (b, layer_idx, h, 0, 0)),
            pl.BlockSpec((2, 2 * WIN_H - 1, GRID_W, GRID_W), lambda h, b: (h, 0, 0, 0)),
        ],
        out_specs=pl.BlockSpec((DEC_SEQ, 128), lambda h, b: (b, h)),
        out_shape=jax.ShapeDtypeStruct((N_SAMPLE_TOK, HALF), BF16),
        scratch_shapes=[pltpu.VMEM((NA_TYPES, 2 * NA_NQ, NA_NK), F32),
                        pltpu.VMEM((PAST_LEN, 2 * HEAD_DIM), BF16),
                        pltpu.VMEM((2, PAST_LEN, 2 * HEAD_DIM), BF16),
                        pltpu.VMEM((2, DEC_SEQ, 2 * HEAD_DIM), BF16)],
        compiler_params=_cparams(("parallel", "arbitrary")),
        name="na_attn",
    )(proj, proj, proj, cache_k, cache_v, col_table)


N_POW = 24


PREP_PPS = 2


def _s5_prep_body(*refs):
    for n in range(PREP_PPS):
        _s5_prep_pair(PREP_PPS * pl.program_id(0) + n, *[r.at[n] for r in refs])


def _s5_prep_pair(p, vec_ref, bt_ref, c_ref, wend_ref, ktoe_ref, woutT_ref, apow_ref, pw_ref, pct_ref):
    L = S5_CHUNK
    lane_blk = lax.broadcasted_iota(jnp.int32, (S5_CH, S5_W), 1) // S5_CH
    src_blk = (lane_blk + L - p) % L
    krows = []
    for d in range(2):
        v = vec_ref[d]
        lre, lim, step = v[0:1], v[1:2], v[2:3]
        a = lre * step
        b = lim * step
        ea = jnp.exp(a)
        nr = ea * jnp.cos(b) - 1.0
        ni = ea * jnp.sin(b)
        den = lre * lre + lim * lim
        qr = (nr * lre + ni * lim) / den
        qi = (ni * lre - nr * lim) / den
        bre, bim = bt_ref[d, 0], bt_ref[d, 1]
        bqr = bre * qr - bim * qi
        bqi = bre * qi + bim * qr
        cre, cim = c_ref[d, 0], c_ref[d, 1]

        e = lax.broadcasted_iota(jnp.int32, (N_POW, 1), 0).astype(F32)
        mag = jnp.exp(e * a)
        pw_ref[0] = mag * jnp.cos(e * b)
        pw_ref[1] = mag * jnp.sin(e * b)
        mag = jnp.exp((L * e) * a)
        apow_ref[d, 0] = mag * jnp.cos((L * e) * b)
        apow_ref[d, 1] = mag * jnp.sin((L * e) * b)

        def power(k):
            return pw_ref[0, k:k + 1, :], pw_ref[1, k:k + 1, :]

        def c_block(k):
            pr, pi = power(k)
            return cre * pr - cim * pi, -(cre * pi + cim * pr)

        for i in range(L):
            rows = pl.ds(pl.multiple_of(((i + p) % L) * S5_CH, S5_CH), S5_CH)
            pr, pi = power(L - 1 - i if d == 0 else i)
            wend_ref[rows, d * 256:d * 256 + 128] = (bqr * pr - bqi * pi).astype(BF16)
            wend_ref[rows, d * 256 + 128:d * 256 + 256] = (bqr * pi + bqi * pr).astype(BF16)
            mre, mim = c_block(i + 1 if d == 0 else L - i)
            woutT_ref[rows, d * 256:d * 256 + 128] = mre.astype(BF16)
            woutT_ref[rows, d * 256 + 128:d * 256 + 256] = mim.astype(BF16)
            lag_rows = slice(i * S5_CH, (i + 1) * S5_CH)
            mre, mim = c_block(i if d == 0 else L - 1 - i)
            pct_ref[lag_rows, 0:128] = mre
            pct_ref[lag_rows, 128:256] = mim

        bqcat = jnp.concatenate([bqr, bqi], axis=1)
        krows.append(_dot_nt(bqcat, pct_ref[...], precision=lax.Precision.HIGHEST))

    base_f = pltpu.roll(krows[0], p * S5_CH, 1)
    base_b = pltpu.roll(krows[1], p * S5_CH, 1)
    for i in range(L):
        rows = pl.ds(pl.multiple_of(((i + p) % L) * S5_CH, S5_CH), S5_CH)
        blk_f = base_f if i == 0 else pltpu.roll(base_f, i * S5_CH, 1)
        blk_b = base_b if (i + 1) % L == 0 else pltpu.roll(base_b, ((i + 1) % L) * S5_CH, 1)
        blk = jnp.where(src_blk >= i, blk_f, 0.0) + jnp.where(src_blk <= i, blk_b, 0.0)
        ktoe_ref[rows, :] = blk.astype(BF16)


def _s5_prep(lam_re, lam_im, log_step, b_re, b_im, c_re, c_im):
    def pair_vec(a):
        return a.astype(F32).reshape(2, S5_PAIRS, 1, S5_P).transpose(1, 0, 2, 3)

    step = jnp.broadcast_to(jnp.exp(log_step.astype(F32))[:, :, None], (2, N_SSM_GROUPS, SSM_STATE))
    vec = jnp.concatenate([pair_vec(lam_re), pair_vec(lam_im), pair_vec(step),
                           jnp.zeros((S5_PAIRS, 2, 5, S5_P), F32)], axis=2)

    def block_diag_ch_by_state(m):
        m = m.astype(F32).reshape(2, S5_PAIRS, 2, SSM_GROUP, SSM_STATE)
        eye = jnp.eye(2, dtype=F32)
        out = m[:, :, :, :, None, :] * eye[None, None, :, None, :, None]
        return out.reshape(2, S5_PAIRS, S5_CH, S5_P).transpose(1, 0, 2, 3)

    bt = jnp.stack([block_diag_ch_by_state(b_re.transpose(0, 1, 3, 2)),
                    block_diag_ch_by_state(b_im.transpose(0, 1, 3, 2))], axis=2)
    cc = jnp.stack([block_diag_ch_by_state(c_re), block_diag_ch_by_state(c_im)], axis=2)

    mat = jax.ShapeDtypeStruct((S5_PAIRS, S5_W, S5_W), BF16)
    mat_spec = pl.BlockSpec((PREP_PPS, S5_W, S5_W), lambda p: (p, 0, 0))
    return pl.pallas_call(
        _s5_prep_body,
        grid=(S5_PAIRS // PREP_PPS,),
        in_specs=[
            pl.BlockSpec((PREP_PPS, 2, 8, S5_P), lambda p: (p, 0, 0, 0)),
            pl.BlockSpec((PREP_PPS, 2, 2, S5_CH, S5_P), lambda p: (p, 0, 0, 0, 0)),
            pl.BlockSpec((PREP_PPS, 2, 2, S5_CH, S5_P), lambda p: (p, 0, 0, 0, 0)),
        ],
        out_specs=[mat_spec, mat_spec, mat_spec,
                   pl.BlockSpec((PREP_PPS, 2, 2, N_POW, S5_P), lambda p: (p, 0, 0, 0, 0))],
        out_shape=[mat, mat, mat, jax.ShapeDtypeStruct((S5_PAIRS, 2, 2, N_POW, S5_P), F32)],
        scratch_shapes=[pltpu.VMEM((PREP_PPS, 2, N_POW, S5_P), F32), pltpu.VMEM((PREP_PPS, S5_W, 2 * S5_P), F32)],
        compiler_params=_cparams(("parallel",)),
        name="s5_prep",
    )(vec, bt, cc)


S5_SEG_ROWS = 256
S5_N_SEG = N_CHUNK_ROWS // S5_SEG_ROWS
S5_SUB = S5_SEG_ROWS // S5_CHUNK
S5_SUB_PER_SEQ = (DEC_SEQ // S5_CHUNK) // S5_CHUNK


S5_PPS = 4


def _s5_body(tu_ref, s0_ref, wend_ref, ktoe_ref, woutT_ref, apow_ref, ty_ref, fin_ref, z_ref, xs_ref, g_ref):
    seg = pl.program_id(0)
    L = S5_CHUNK
    lane_blk = lax.broadcasted_iota(jnp.int32, (S5_SEG_ROWS, 128), 1) // S5_CH
    pairs = [S5_PPS * pl.program_id(1) + n for n in range(S5_PPS)]

    s_rows, ends = [], []
    for n, p in enumerate(pairs):
        tiles = []
        for t in range(LANE_TILES):
            tile = None
            for k in range(128 // S5_CH):
                j = (LANE_TILES * t + k + L - p) % L
                blk = tu_ref[j, t]
                tile = blk if tile is None else jnp.where(lane_blk == k, blk, tile)
            tiles.append(tile)
        s = jnp.concatenate(tiles, axis=1)
        s_rows.append(s)
        z = _dot(s, wend_ref[n])
        for q in range(LANE_TILES):
            z_ref[n, q] = z[:, q * 128:(q + 1) * 128]

        pair_ends = []
        for d in range(2):
            are, aim = apow_ref[n, d, 0, 1:2, :], apow_ref[n, d, 1, 1:2, :]
            xr = jnp.zeros((S5_SUB, S5_P), F32)
            xi = jnp.zeros((S5_SUB, S5_P), F32)
            for c in (range(L) if d == 0 else reversed(range(L))):
                rows = pl.ds(c, S5_SUB, stride=L)
                xs_ref[n, 2 * d, rows, :] = xr
                xs_ref[n, 2 * d + 1, rows, :] = xi
                zr = z_ref[n, 2 * d, rows, :]
                zi = z_ref[n, 2 * d + 1, rows, :]
                xr, xi = are * xr - aim * xi + zr, are * xi + aim * xr + zi
            pair_ends.append((xr, xi))
            fin_ref[n, d, 0] = xr
            fin_ref[n, d, 1] = xi
        ends.append(pair_ends)

    @pl.when(seg > 0)
    def _():
        for n in range(S5_PPS):
            for d in range(2):
                er, ei = ends[n][d]
                a16r, a16i = apow_ref[n, d, 0, L:L + 1, :], apow_ref[n, d, 1, L:L + 1, :]
                for bb in range(S5_SUB // S5_SUB_PER_SEQ):
                    gr = s0_ref[n, bb, 2 * d:2 * d + 1, :]
                    gi = s0_ref[n, bb, 2 * d + 1:2 * d + 2, :]
                    order = range(S5_SUB_PER_SEQ) if d == 0 else reversed(range(S5_SUB_PER_SEQ))
                    prev = None
                    for sub in order:
                        row = bb * S5_SUB_PER_SEQ + sub
                        if prev is not None:
                            gr, gi = (a16r * gr - a16i * gi + er[prev:prev + 1, :],
                                      a16r * gi + a16i * gr + ei[prev:prev + 1, :])
                        g_ref[n, 0, row:row + 1, :] = gr
                        g_ref[n, 1, row:row + 1, :] = gi
                        prev = row
                if d == 0:
                    pr, pi = apow_ref[n, d, 0, 0:L, :], apow_ref[n, d, 1, 0:L, :]
                else:
                    pr = jnp.concatenate([apow_ref[n, d, 0, k:k + 1, :] for k in reversed(range(L))], axis=0)
                    pi = jnp.concatenate([apow_ref[n, d, 1, k:k + 1, :] for k in reversed(range(L))], axis=0)
                for sub in range(S5_SUB):
                    rows = slice(sub * L, (sub + 1) * L)
                    gr = g_ref[n, 0, sub:sub + 1, :]
                    gi = g_ref[n, 1, sub:sub + 1, :]
                    xs_ref[n, 2 * d, rows, :] = xs_ref[n, 2 * d, rows, :] + (pr * gr - pi * gi)
                    xs_ref[n, 2 * d + 1, rows, :] = xs_ref[n, 2 * d + 1, rows, :] + (pr * gi + pi * gr)

    @pl.when(pl.program_id(1) == 0)
    def _():
        ty_ref[...] = jnp.zeros_like(ty_ref)

    for n, p in enumerate(pairs):
        xs = jnp.concatenate([xs_ref[n, q].astype(BF16) for q in range(LANE_TILES)], axis=1)
        y = _dot(s_rows[n], ktoe_ref[n]) + _dot_nt(xs, woutT_ref[n])
        for t in range(LANE_TILES):
            yt = y[:, t * 128:(t + 1) * 128]
            for k in range(128 // S5_CH):
                j = (LANE_TILES * t + k + L - p) % L
                ty_ref[j, t] = jnp.where(lane_blk == k, yt, ty_ref[j, t])


def _s5_scan(tu, s0, wend, ktoe, woutT, apow):
    mat_spec = pl.BlockSpec((S5_PPS, S5_W, S5_W), lambda g, p: (p, 0, 0))
    seqs = S5_SUB // S5_SUB_PER_SEQ
    return pl.pallas_call(
        _s5_body,
        grid=(S5_N_SEG, S5_PAIRS // S5_PPS),
        in_specs=[
            pl.BlockSpec((S5_CHUNK, LANE_TILES, S5_SEG_ROWS, 128), lambda g, p: (0, 0, g, 0)),
            pl.BlockSpec((S5_PPS, seqs, 4, S5_P), lambda g, p: (p, jnp.maximum(g - 1, 0), 0, 0)),
            mat_spec, mat_spec, mat_spec,
            pl.BlockSpec((S5_PPS, 2, 2, N_POW, S5_P), lambda g, p: (p, 0, 0, 0, 0)),
        ],
        out_specs=[
            pl.BlockSpec((S5_CHUNK, LANE_TILES, S5_SEG_ROWS, 128), lambda g, p: (0, 0, g, 0)),
            pl.BlockSpec((S5_PPS, 2, 2, S5_SUB, S5_P), lambda g, p: (p, 0, 0, g, 0)),
        ],
        out_shape=[
            jax.ShapeDtypeStruct((S5_CHUNK, LANE_TILES, N_CHUNK_ROWS, 128), F32),
            jax.ShapeDtypeStruct((S5_PAIRS, 2, 2, S5_N_SEG * S5_SUB, S5_P), F32),
        ],
        scratch_shapes=[pltpu.VMEM((S5_PPS, LANE_TILES, S5_SEG_ROWS, 128), F32),
                        pltpu.VMEM((S5_PPS, LANE_TILES, S5_SEG_ROWS, 128), F32),
                        pltpu.VMEM((S5_PPS, 2, S5_SUB, S5_P), F32)],
        compiler_params=_cparams(("parallel", "arbitrary")),
        name="s5_scan",
    )(tu, s0, wend, ktoe, woutT, apow)


TM_GLU = 2048


def _glu_body(tu_ref, ty_ref, d_ref, gw_ref, gb_ref, o_ref, out_ref):
    rows = TM_GLU // S5_CHUNK
    d = d_ref[...]
    pre = []
    for j in range(S5_CHUNK):
        uj = jnp.concatenate([tu_ref[j, t].astype(F32) for t in range(LANE_TILES)], axis=1)
        yj = jnp.concatenate([ty_ref[j, t] for t in range(LANE_TILES)], axis=1)
        dj = pltpu.roll(d, j * S5_CH, 1) if j else d
        yj = dj * uj + yj
        back = ((S5_CHUNK - j) % S5_CHUNK) * S5_CH
        pre.append(pltpu.roll(yj, back, 1) if back else yj)
    y = jnp.concatenate(pre, axis=0)
    cdf = 0.5 * (1.0 + jnp.tanh(math.sqrt(2.0 / math.pi) * (y + 0.044715 * (y * y * y))))
    y = y * cdf
    gate = jax.nn.sigmoid(_dot(y.astype(BF16), gw_ref[...]) + gb_ref[...])
    out = y * gate
    for j in range(S5_CHUNK):
        oj = out[j * rows:(j + 1) * rows]
        for t in range(LANE_TILES):
            out_ref[t, pl.ds(j, rows, stride=S5_CHUNK), :] = oj[:, t * 128:(t + 1) * 128]
    o_ref[...] = jnp.concatenate([out_ref[t] for t in range(LANE_TILES)], axis=1).astype(BF16)


def _glu(tu, ty, ssm_d, glu_w, glu_b):
    tm = TM_GLU
    rows = tm // S5_CHUNK
    lay_spec = pl.BlockSpec((S5_CHUNK, LANE_TILES, rows, 128), lambda i: (0, 0, i, 0))
    return pl.pallas_call(
        _glu_body,
        grid=(N_TOK // tm,),
        in_specs=[
            lay_spec, lay_spec,
            pl.BlockSpec((1, HALF), lambda i: (0, 0)),
            pl.BlockSpec((HALF, HALF), lambda i: (0, 0)),
            pl.BlockSpec((1, HALF), lambda i: (0, 0)),
        ],
        out_specs=pl.BlockSpec((tm, HALF), lambda i: (i, 0)),
        out_shape=jax.ShapeDtypeStruct((N_TOK, HALF), BF16),
        scratch_shapes=[pltpu.VMEM((LANE_TILES, tm, 128), F32)],
        compiler_params=_cparams(("parallel",)),
        name="s5_glu",
    )(tu, ty, ssm_d.reshape(1, HALF), glu_w.astype(BF16), glu_b.reshape(1, HALF))


def kernel(x_prompt, x_sample, c, cache_k, cache_v, state_s5_re, state_s5_im, c_ctx, norm1_g, norm2_g, ada_w, ada_b, mlp_w1, mlp_w2, ab_w_in, pool_w, pool_scale, conv_w, ab_w_out, cd_w_in, na_rpb, ssm_lambda_re, ssm_lambda_im, ssm_log_step, ssm_b_re, ssm_b_im, ssm_c_re, ssm_c_im, ssm_d, glu_w, glu_b, cd_w_out, final_g):
    depth = ada_w.shape[0]
    xs = (x_prompt.reshape(N_PROMPT_TOK, D_MODEL), x_sample.reshape(N_SAMPLE_TOK, D_MODEL))
    cond = jnp.concatenate([c_ctx[None, :], c, jnp.zeros((N_COND - 1 - DEC_BATCH, D_MODEL), F32)], axis=0)
    mods = _adaln(cond, ada_w, ada_b).reshape(depth, N_COND, 6, D_MODEL)

    new_k, new_v, new_re, new_im = [], [], [], []
    for layer in range(depth):
        i = layer // 2
        final = layer == depth - 1
        if layer % 2 == 0:
            y, x = _even_layer_mix(xs, mods, layer, norm1_g[layer], ab_w_in, i,
                                   pool_w[i], pool_scale[i], conv_w[i])
            ya, ya_col, yb, yb_col = y, 0, y, 1
            w_out_stack = ab_w_out
        else:
            x = xs[0]
            qkv, tu = _inproj_odd(x, mods, layer, norm1_g[layer], cd_w_in, i)
            o_p, ck, cv = _ctx_attn(qkv)
            new_k.append(ck)
            new_v.append(cv)
            o_s = _na_attn(qkv, cache_k, cache_v, _na_col_table(na_rpb[i]), i)
            attn = (o_p, o_s)

            wend, ktoe, woutT, apow = _s5_prep(ssm_lambda_re[i], ssm_lambda_im[i], ssm_log_step[i],
                                               ssm_b_re[i], ssm_b_im[i], ssm_c_re[i], ssm_c_im[i])

            def pair_state(s):
                return s.astype(F32).reshape(DEC_BATCH, 2, S5_PAIRS, S5_P).transpose(2, 0, 1, 3)

            s0 = jnp.stack([pair_state(state_s5_re[:, i]), pair_state(state_s5_im[:, i])], axis=3)
            s0 = s0.reshape(S5_PAIRS, DEC_BATCH, 4, S5_P)
            ty, fin = _s5_scan(tu, s0, wend, ktoe, woutT, apow)
            fin = fin[:, :, :, :BATCH]
            new_re.append(fin[:, :, 0].transpose(2, 1, 0, 3).reshape(BATCH, 2, N_SSM_GROUPS, SSM_STATE))
            new_im.append(fin[:, :, 1].transpose(2, 1, 0, 3).reshape(BATCH, 2, N_SSM_GROUPS, SSM_STATE))
            d_out = _glu(tu, ty, ssm_d[i], glu_w[i], glu_b[i])
            ya, ya_col, yb, yb_col = attn, 0, d_out, 0
            w_out_stack = cd_w_out
        xs = _mlp(x, ya, ya_col, yb, yb_col, w_out_stack, i, mods, layer, norm2_g[layer], final_g,
                  mlp_w1, mlp_w2, final)

    y_prompt = xs[0].reshape(BATCH, SEQ, D_MODEL)
    y_sample = xs[1].reshape(DEC_BATCH, DEC_SEQ, D_MODEL)
    return (y_prompt, y_sample, jnp.concatenate(new_k, axis=1), jnp.concatenate(new_v, axis=1),
            jnp.stack(new_re, axis=1), jnp.stack(new_im, axis=1))
```

```python
import functools
import math

import jax
import jax.numpy as jnp
from jax import lax
from jax.experimental import pallas as pl
from jax.experimental.pallas import tpu as pltpu

F32 = jnp.float32
BF16 = jnp.bfloat16

D_MODEL = 1024
BATCH = 16
SEQ = 256
DEC_BATCH = 4
DEC_SEQ = 2048
PAST_LEN = 256
GRID_W = 64
GRID_ROWS = DEC_SEQ // GRID_W
HALF = 512
POOL_WINDOWS = (2, 4, 8, 16)
POOL_GROUP = 128
HEAD_DIM = 64
N_HEADS = 8
WIN_H = 8
WIN_W = 16
SSM_GROUP = 16
N_SSM_GROUPS = 32
SSM_STATE = 64
D_FF = 4096
EPS = 1e-6

N_PROMPT_TOK = BATCH * SEQ
N_SAMPLE_TOK = DEC_BATCH * DEC_SEQ
N_TOK = N_PROMPT_TOK + N_SAMPLE_TOK
N_COND = 8

S5_CHUNK = 16
S5_PAIRS = N_SSM_GROUPS // 2
S5_CH = 2 * SSM_GROUP
S5_P = 2 * SSM_STATE
S5_W = S5_CHUNK * S5_CH

VMEM_LIMIT = 56 * 1024 * 1024


def _cparams(sem):
    return pltpu.CompilerParams(dimension_semantics=sem, vmem_limit_bytes=VMEM_LIMIT)


def _cond_row(i, tm):
    npt = N_PROMPT_TOK // tm
    per = DEC_SEQ // tm
    return jnp.where(i < npt, 0, 1 + (i - npt) // per)


def _resident_layer(shape, idx):
    zeros = (0,) * len(shape)
    return pl.BlockSpec((None,) + tuple(shape), lambda i: (idx,) + zeros, pipeline_mode=pl.Buffered(1))


def _normmod(x, g, shift, scale):
    ms = jnp.mean(x * x, axis=-1, keepdims=True)
    return (x * lax.rsqrt(ms + EPS) * g) * (1.0 + scale) + shift


def _dot(a, b):
    return jnp.dot(a, b, preferred_element_type=F32)


def _dot_nt(a, b, precision=None):
    return lax.dot_general(a, b, (((1,), (1,)), ((), ())), preferred_element_type=F32, precision=precision)


def _adaln_body(c_ref, w_ref, b_ref, o_ref):
    c = c_ref[...]
    s = c * jax.nn.sigmoid(c)
    o_ref[...] = _dot(s.astype(BF16), w_ref[...].astype(BF16)) + b_ref[...]


def _adaln(cond, ada_w, ada_b):
    depth = ada_w.shape[0]
    nj = 6
    return pl.pallas_call(
        _adaln_body,
        grid=(depth, nj),
        in_specs=[
            pl.BlockSpec((N_COND, D_MODEL), lambda l, j: (0, 0)),
            pl.BlockSpec((None, D_MODEL, D_MODEL), lambda l, j: (l, 0, j)),
            pl.BlockSpec((None, 1, D_MODEL), lambda l, j: (l, 0, j)),
        ],
        out_specs=pl.BlockSpec((None, N_COND, D_MODEL), lambda l, j: (l, 0, j)),
        out_shape=jax.ShapeDtypeStruct((depth, N_COND, 6 * D_MODEL), F32),
        compiler_params=_cparams(("arbitrary", "arbitrary")),
        name="adaln",
    )(cond, ada_w, ada_b.reshape(depth, 1, 6 * D_MODEL))


TM_PROJ = 1024


def _stream_tile(x_refs, tm):
    if len(x_refs) == 1:
        return x_refs[0][...]
    is_prompt = pl.program_id(0) < N_PROMPT_TOK // tm
    return jnp.where(is_prompt, x_refs[0][...], x_refs[1][...])


N_CHUNK_ROWS = N_TOK // S5_CHUNK
LANE_TILES = S5_W // 128


def _inproj_odd_body(x_ref, m_ref, g_ref, w_ref, qkv_ref, tu_ref, u_ref, wbf_ref):
    @pl.when(pl.program_id(0) == 0)
    def _():
        wbf_ref[...] = w_ref[...].astype(BF16)

    m = m_ref[...]
    h = _normmod(x_ref[...], g_ref[...], m[0:1], m[1:2])
    acc = _dot(h.astype(BF16), wbf_ref[...])
    qkv_ref[...] = acc[:, 0:3 * HALF].astype(BF16)
    for t in range(LANE_TILES):
        u_ref[t] = acc[:, 3 * HALF + t * 128:3 * HALF + (t + 1) * 128]
    rows = TM_PROJ // S5_CHUNK
    for j in range(S5_CHUNK):
        uj = jnp.concatenate([u_ref[t, pl.ds(j, rows, stride=S5_CHUNK), :] for t in range(LANE_TILES)], axis=1)
        if j:
            uj = pltpu.roll(uj, j * S5_CH, 1)
        for t in range(LANE_TILES):
            tu_ref[j, t] = uj[:, t * 128:(t + 1) * 128].astype(BF16)


def _inproj_odd(x, mods, layer, g, w_stack, w_idx):
    tm = TM_PROJ
    rows = tm // S5_CHUNK
    return pl.pallas_call(
        _inproj_odd_body,
        grid=(N_TOK // tm,),
        in_specs=[
            pl.BlockSpec((tm, D_MODEL), lambda i: (i, 0)),
            pl.BlockSpec((None, None, 6, D_MODEL), lambda i: (layer, _cond_row(i, tm), 0, 0)),
            pl.BlockSpec((1, D_MODEL), lambda i: (0, 0)),
            _resident_layer((D_MODEL, 4 * HALF), w_idx),
        ],
        out_specs=[
            pl.BlockSpec((tm, 3 * HALF), lambda i: (i, 0)),
            pl.BlockSpec((S5_CHUNK, LANE_TILES, rows, 128), lambda i: (0, 0, i, 0)),
        ],
        out_shape=[
            jax.ShapeDtypeStruct((N_TOK, 3 * HALF), BF16),
            jax.ShapeDtypeStruct((S5_CHUNK, LANE_TILES, N_CHUNK_ROWS, 128), BF16),
        ],
        scratch_shapes=[pltpu.VMEM((LANE_TILES, tm, 128), F32), pltpu.VMEM((D_MODEL, 4 * HALF), BF16)],
        compiler_params=_cparams(("arbitrary",)),
        name="inproj_odd",
    )(x, mods, g.reshape(1, D_MODEL), w_stack)


TM_MIX = 256
HALO = 16
RING = 3
LAG = 2
PAD_ROWS = TM_MIX + 2 * HALO + 8


def _even_step(phase, x_refs, m_ref, g_ref, w_ref, pw_ref, ps_ref, cw_ref, y_ref, xcat_ref, proj_ref, tail_ref,
               h_ref):
    i = pl.program_id(0)
    write_slot, cur_slot, next_slot = phase, (phase + 1) % RING, (phase + 2) % RING

    m = m_ref[...]
    x = _stream_tile(x_refs, TM_MIX)
    xcat_ref[...] = x
    h_ref[...] = _normmod(x, g_ref[...], m[0:1], m[1:2]).astype(BF16)
    n_pieces = len(POOL_WINDOWS) + HALF // 128
    pw = (4 * HALF) // n_pieces

    def project(piece):
        cols = slice(piece * pw, (piece + 1) * pw)
        proj_ref[write_slot, :, cols] = _dot(h_ref[...], w_ref[:, cols])

    t = jnp.maximum(i - LAG, 0)
    n_prompt_tiles = N_PROMPT_TOK // TM_MIX
    tiles_per_seq = DEC_SEQ // TM_MIX
    is_prompt = t < n_prompt_tiles
    tile_in_seq = jnp.where(is_prompt, 0, (t - n_prompt_tiles) % tiles_per_seq)
    seq_len = jnp.where(is_prompt, SEQ, DEC_SEQ)
    has_prev = tile_in_seq > 0
    has_next = jnp.logical_and(jnp.logical_not(is_prompt), tile_in_seq < tiles_per_seq - 1)
    prev_keep = jnp.where(has_prev, 1.0, 0.0).astype(F32)
    next_keep = jnp.where(has_next, 1.0, 0.0).astype(F32)
    pos = tile_in_seq * TM_MIX + lax.broadcasted_iota(jnp.int32, (TM_MIX, 1), 0)
    lo_rows, hi_rows = HALO, HALO + TM_MIX
    head = slice(0, HALO)

    slack = jnp.zeros((PAD_ROWS - TM_MIX - 2 * HALO, POOL_GROUP), F32)
    for gi, w in enumerate(POOL_WINDOWS):
        project(gi)
        cs = slice(gi * POOL_GROUP, (gi + 1) * POOL_GROUP)
        cur = proj_ref[cur_slot, :, cs]
        f = jnp.concatenate([tail_ref[:, cs] * prev_keep, cur, proj_ref[next_slot, head, cs] * next_keep, slack],
                            axis=0)
        n_rows, span = PAD_ROWS, 1
        while 2 * span < w:
            n_rows -= 8
            f = f[0:n_rows] + f[span:span + n_rows]
            span *= 2
        s = f[lo_rows - span:hi_rows - span] + f[lo_rows:hi_rows]
        lo = jnp.maximum(pos - w // 2, 0)
        hi = jnp.minimum(pos + (w - w // 2), seq_len)
        inv = 1.0 / (hi - lo).astype(F32)
        pooled = s * inv - cur
        mixed = _dot(pooled.astype(BF16), pw_ref[gi]) * ps_ref[:, cs]
        y_ref[:, cs] = mixed.astype(BF16)

    for ci in range(HALF // 128):
        project(len(POOL_WINDOWS) + ci)
        cs = slice(ci * 128, (ci + 1) * 128)
        bs = slice(HALF + ci * 128, HALF + (ci + 1) * 128)
        gs = slice(2 * HALF + ci * 128, 2 * HALF + (ci + 1) * 128)
        vs = slice(3 * HALF + ci * 128, 3 * HALF + (ci + 1) * 128)
        z = jnp.concatenate([(tail_ref[:, gs] * tail_ref[:, vs]) * prev_keep,
                             proj_ref[cur_slot, :, gs] * proj_ref[cur_slot, :, vs],
                             (proj_ref[next_slot, head, gs] * proj_ref[next_slot, head, vs]) * next_keep], axis=0)
        conv = (cw_ref[0:1, cs] * z[lo_rows - 1:hi_rows - 1]
                + cw_ref[1:2, cs] * z[lo_rows:hi_rows]
                + cw_ref[2:3, cs] * z[lo_rows + 1:hi_rows + 1])
        y_ref[:, bs] = (proj_ref[cur_slot, :, bs] * conv).astype(BF16)

    tail_ref[...] = proj_ref[cur_slot, TM_MIX - HALO:TM_MIX, :]


def _even_body(n_x, *refs):
    x_refs, rest = refs[:n_x], refs[n_x:]
    m_ref, g_ref, w_ref = rest[:3]
    proj_ref, tail_ref, h_ref, wbf_ref = rest[-4:]
    i = pl.program_id(0)

    @pl.when(i == 0)
    def _():
        proj_ref[...] = jnp.zeros_like(proj_ref)
        tail_ref[...] = jnp.zeros_like(tail_ref)
        wbf_ref[...] = w_ref[...].astype(BF16)

    for phase in range(RING):
        @pl.when(i % RING == phase)
        def _(phase=phase):
            _even_step(phase, x_refs, m_ref, g_ref, wbf_ref, *rest[3:-1])


def _even_layer_mix(xs, mods, layer, g, w_stack, w_idx, pool_w, pool_scale, conv_w):
    tm = TM_MIX
    n_tiles = N_TOK // tm
    npt = N_PROMPT_TOK // tm
    last = n_tiles - 1
    if len(xs) == 1:
        x_specs = [pl.BlockSpec((tm, D_MODEL), lambda i: (jnp.minimum(i, last), 0))]
    else:
        x_specs = [pl.BlockSpec((tm, D_MODEL), lambda i: (jnp.minimum(i, npt - 1), 0)),
                   pl.BlockSpec((tm, D_MODEL), lambda i: (jnp.clip(i - npt, 0, last - npt), 0))]
    return pl.pallas_call(
        functools.partial(_even_body, len(xs)),
        grid=(n_tiles + LAG,),
        in_specs=x_specs + [
            pl.BlockSpec((None, None, 6, D_MODEL), lambda i: (layer, _cond_row(jnp.minimum(i, last), tm), 0, 0)),
            pl.BlockSpec((1, D_MODEL), lambda i: (0, 0)),
            _resident_layer((D_MODEL, 4 * HALF), w_idx),
            pl.BlockSpec((len(POOL_WINDOWS), POOL_GROUP, POOL_GROUP), lambda i: (0, 0, 0)),
            pl.BlockSpec((1, HALF), lambda i: (0, 0)),
            pl.BlockSpec((3, HALF), lambda i: (0, 0)),
        ],
        out_specs=[pl.BlockSpec((tm, 2 * HALF), lambda i: (jnp.maximum(i - LAG, 0), 0)),
                   pl.BlockSpec((tm, D_MODEL), lambda i: (jnp.minimum(i, last), 0))],
        out_shape=[jax.ShapeDtypeStruct((N_TOK, 2 * HALF), BF16), jax.ShapeDtypeStruct((N_TOK, D_MODEL), F32)],
        scratch_shapes=[pltpu.VMEM((RING, tm, 4 * HALF), F32), pltpu.VMEM((HALO, 4 * HALF), F32),
                        pltpu.VMEM((tm, D_MODEL), BF16), pltpu.VMEM((D_MODEL, 4 * HALF), BF16)],
        compiler_params=_cparams(("arbitrary",)),
        name="even_layer_mix",
    )(*xs, mods, g.reshape(1, D_MODEL), w_stack, pool_w.astype(BF16), pool_scale.reshape(1, HALF), conv_w)


TM_MLP = 1024
RC_MLP = 256
N_CAST = 16


def _mlp_tile(final, x_ref, ya_ref, yb_ref, m_ref, g_ref, fg_ref, o_ref, wo_ref, w1_ref, w2_ref):
    m = m_ref[...]
    gain = g_ref[...] * (1.0 + m[4:5])
    for r in range(TM_MLP // RC_MLP):
        rows = slice(r * RC_MLP, (r + 1) * RC_MLP)
        mix = _dot(ya_ref[rows, :], wo_ref[0:HALF, :]) + _dot(yb_ref[rows, :], wo_ref[HALF:2 * HALF, :])
        x1 = x_ref[rows, :] + m[2:3] * mix
        ms = jnp.mean(x1 * x1, axis=-1, keepdims=True)
        h = (x1 * lax.rsqrt(ms + EPS) * gain + m[3:4]).astype(BF16)
        t = _dot(h, w1_ref[...])
        t = jnp.square(jnp.maximum(t, 0.0)).astype(BF16)
        out = x1 + m[5:6] * _dot(t, w2_ref[...])
        if final:
            ms = jnp.mean(out * out, axis=-1, keepdims=True)
            out = out * lax.rsqrt(ms + EPS) * fg_ref[...]
        o_ref[rows, :] = out


def _mlp_resident_body(final, x_ref, ya_ref, yb_ref, wo_ref, m_ref, g_ref, fg_ref, w1_ref, w2_ref, o_ref):
    _mlp_tile(final, x_ref, ya_ref, yb_ref, m_ref, g_ref, fg_ref, o_ref, wo_ref, w1_ref, w2_ref)


def _mlp_cast_body(final, emit, x_ref, ya_ref, yb_ref, wo_f, m_ref, g_ref, fg_ref, w1_f, w2_f, o_ref, *rest):
    wo_ref, w1_ref, w2_ref = rest[-3:]
    emitted = rest[:-3]
    i = pl.program_id(0)

    @pl.when(i < N_CAST)
    def _():
        for k, (f_ref, b_ref) in enumerate(((wo_f, wo_ref), (w1_f, w1_ref), (w2_f, w2_ref))):
            n = f_ref.shape[0]
            slab = f_ref[...].astype(BF16)
            b_ref[pl.ds(pl.multiple_of(i * n, n), n), :] = slab
            if emit:
                emitted[k][...] = slab

    @pl.when(i >= N_CAST)
    def _():
        _mlp_tile(final, x_ref, ya_ref, yb_ref, m_ref, g_ref, fg_ref, o_ref, wo_ref, w1_ref, w2_ref)


def _resident(shape):
    zeros = (0,) * len(shape)
    return pl.BlockSpec(shape, lambda i: zeros, pipeline_mode=pl.Buffered(1))


def _mlp_call(x, x_off, ya, ya_off, ya_col, yb, yb_off, yb_col, n_tiles, tile0, mods, layer, g2, final_g, final,
              f32_weights=None, bf16_weights=None, emit=False):
    tm = TM_MLP
    lead = N_CAST if f32_weights is not None else 0

    def tile(i):
        return jnp.maximum(i - lead, 0)

    stream_specs = [
        pl.BlockSpec((tm, D_MODEL), lambda i: (tile(i) + x_off, 0)),
        pl.BlockSpec((tm, HALF), lambda i: (tile(i) + ya_off, ya_col)),
        pl.BlockSpec((tm, HALF), lambda i: (tile(i) + yb_off, yb_col)),
    ]
    mod_spec = pl.BlockSpec((None, None, 6, D_MODEL), lambda i: (layer, _cond_row(tile(i) + tile0, tm), 0, 0))
    out_spec = pl.BlockSpec((tm, D_MODEL), lambda i: (tile(i), 0))
    out_shape = jax.ShapeDtypeStruct((n_tiles * tm, D_MODEL), F32)
    vecs = (g2.reshape(1, D_MODEL), final_g.reshape(1, D_MODEL))
    if f32_weights is None:
        w_out, w1, w2 = bf16_weights
        return pl.pallas_call(
            functools.partial(_mlp_resident_body, final),
            grid=(n_tiles,),
            in_specs=stream_specs + [_resident((D_MODEL, D_MODEL)), mod_spec, _resident((1, D_MODEL)),
                                     _resident((1, D_MODEL)), _resident((D_MODEL, D_FF)), _resident((D_FF, D_MODEL))],
            out_specs=out_spec,
            out_shape=out_shape,
            compiler_params=_cparams(("parallel",)),
            name="mlp",
        )(x, ya, yb, w_out, mods, *vecs, w1, w2)

    wo_stack, wo_idx, w1_stack, w2_stack = f32_weights

    def slab(i):
        return jnp.minimum(i, N_CAST - 1)

    shapes = ((D_MODEL, D_MODEL), (D_MODEL, D_FF), (D_FF, D_MODEL))
    slab_rows = [rows // N_CAST for rows, _ in shapes]
    f32_specs = [pl.BlockSpec((None, n, cols), lambda i, idx=idx: (idx, slab(i), 0))
                 for n, (_, cols), idx in zip(slab_rows, shapes, (wo_idx, layer, layer))]
    out_specs, out_shapes = [out_spec], [out_shape]
    if emit:
        out_specs += [pl.BlockSpec((n, cols), lambda i: (slab(i), 0)) for n, (_, cols) in zip(slab_rows, shapes)]
        out_shapes += [jax.ShapeDtypeStruct(shp, BF16) for shp in shapes]
    return pl.pallas_call(
        functools.partial(_mlp_cast_body, final, emit),
        grid=(N_CAST + n_tiles,),
        in_specs=stream_specs + [f32_specs[0], mod_spec, _resident((1, D_MODEL)), _resident((1, D_MODEL)),
                                 f32_specs[1], f32_specs[2]],
        out_specs=out_specs,
        out_shape=out_shapes,
        scratch_shapes=[pltpu.VMEM(shp, BF16) for shp in shapes],
        compiler_params=_cparams(("arbitrary",)),
        name="mlp_cast",
    )(x, ya, yb, wo_stack, mods, *vecs, w1_stack, w2_stack)


def _mlp(x, ya, ya_col, yb, yb_col, w_out_stack, w_out_idx, mods, layer, g2, final_g, w1_stack, w2_stack, final):
    tm = TM_MLP
    npt = N_PROMPT_TOK // tm
    f32_weights = (w_out_stack, w_out_idx, w1_stack, w2_stack)
    common = (mods, layer, g2, final_g, final)
    if not final:
        if isinstance(ya, tuple):
            ya = jnp.concatenate(ya, axis=0)
        return tuple(_mlp_call(x, 0, ya, 0, ya_col, yb, 0, yb_col, N_TOK // tm, 0, *common, f32_weights=f32_weights))
    ya_p, ya_s = ((ya[0], 0), (ya[1], 0)) if isinstance(ya, tuple) else ((ya, 0), (ya, npt))
    y_p, *bf16_weights = _mlp_call(x, 0, ya_p[0], ya_p[1], ya_col, yb, 0, yb_col, npt, 0, *common,
                                   f32_weights=f32_weights, emit=True)
    y_s = _mlp_call(x, npt, ya_s[0], ya_s[1], ya_col, yb, npt, yb_col, N_TOK // tm - npt, npt, *common,
                    bf16_weights=bf16_weights)
    return y_p, y_s


def _ctx_attn_body(q_ref, k_ref, v_ref, o_ref, ck_ref, cv_ref):
    lane = lax.broadcasted_iota(jnp.int32, (1, 2 * HEAD_DIM), 1)
    own = [lane < HEAD_DIM, lane >= HEAD_DIM]
    for h in range(N_HEADS):
        sl = slice(h * HEAD_DIM, (h + 1) * HEAD_DIM)
        ck_ref[h] = k_ref[:, sl].astype(F32)
        cv_ref[h] = v_ref[:, sl].astype(F32)
    for hp in range(N_HEADS // 2):
        sl = slice(hp * 2 * HEAD_DIM, (hp + 1) * 2 * HEAD_DIM)
        q = q_ref[:, sl] * jnp.asarray(HEAD_DIM ** -0.5, BF16)
        k = k_ref[:, sl]
        v = v_ref[:, sl]
        q2 = jnp.concatenate([jnp.where(own[0], q, jnp.zeros_like(q)),
                              jnp.where(own[1], q, jnp.zeros_like(q))], axis=0)
        s = _dot_nt(q2, k)
        m = jnp.max(s, axis=-1, keepdims=True)
        p = jnp.exp(s - m).astype(BF16)
        out = None
        for hh in range(2):
            num = _dot(p[hh * SEQ:(hh + 1) * SEQ], jnp.where(own[hh], v, jnp.ones_like(v)))
            den = jnp.where(own[hh], pltpu.roll(num, HEAD_DIM, 1), 1.0)
            o = num / den
            out = o if out is None else jnp.where(own[0], out, o)
        o_ref[:, sl] = out.astype(BF16)


def _ctx_attn(proj):
    return pl.pallas_call(
        _ctx_attn_body,
        grid=(BATCH,),
        in_specs=[
            pl.BlockSpec((SEQ, HALF), lambda b: (b, 0)),
            pl.BlockSpec((SEQ, HALF), lambda b: (b, 1)),
            pl.BlockSpec((SEQ, HALF), lambda b: (b, 2)),
        ],
        out_specs=[
            pl.BlockSpec((SEQ, HALF), lambda b: (b, 0)),
            pl.BlockSpec((None, None, N_HEADS, SEQ, HEAD_DIM), lambda b: (b, 0, 0, 0, 0)),
            pl.BlockSpec((None, None, N_HEADS, SEQ, HEAD_DIM), lambda b: (b, 0, 0, 0, 0)),
        ],
        out_shape=[
            jax.ShapeDtypeStruct((N_PROMPT_TOK, HALF), BF16),
            jax.ShapeDtypeStruct((BATCH, 1, N_HEADS, SEQ, HEAD_DIM), F32),
            jax.ShapeDtypeStruct((BATCH, 1, N_HEADS, SEQ, HEAD_DIM), F32),
        ],
        compiler_params=_cparams(("parallel",)),
        name="ctx_attn",
    )(proj, proj, proj)


NA_MASKED = -math.inf
NA_QROWS = 4
NA_KROWS = NA_QROWS + WIN_H
NA_NQ = NA_QROWS * GRID_W
NA_NK = NA_KROWS * GRID_W
NA_BLOCKS = GRID_ROWS // NA_QROWS
NA_TYPES = 3


def _na_col_table(rpb):
    qc = jnp.arange(GRID_W)[:, None]
    kc = jnp.arange(GRID_W)[None, :]
    c0 = jnp.clip(qc - WIN_W // 2, 0, GRID_W - WIN_W)
    valid = (kc >= c0) & (kc < c0 + WIN_W)
    coff = jnp.clip(kc - qc, -(WIN_W - 1), WIN_W - 1) + (WIN_W - 1)
    onehot = (coff.reshape(1, -1) == jnp.arange(2 * WIN_W - 1)[:, None]).astype(F32)
    t = jnp.dot(rpb.astype(F32).reshape(N_HEADS * (2 * WIN_H - 1), 2 * WIN_W - 1), onehot,
                precision=lax.Precision.HIGHEST)
    t = t.reshape(N_HEADS, 2 * WIN_H - 1, GRID_W, GRID_W)
    return jnp.where(valid[None, None], t, NA_MASKED)


NA_TYPE_Q0 = (0, NA_QROWS, GRID_ROWS - NA_QROWS)
NA_TYPE_KROWS = (WIN_H, NA_KROWS, WIN_H)


def _na_window_start(rb, krows):
    return min(max(NA_QROWS * rb - WIN_H // 2, 0), GRID_ROWS - krows)


def _na_body(q_ref, k_ref, v_ref, ck_ref, cv_ref, tab_ref, o_ref, bias_ref, kc_ref, vc_ref, va_ref):
    @pl.when(pl.program_id(1) == 0)
    def _():
        masked = jnp.full((GRID_W, GRID_W), NA_MASKED, F32)
        for hh in range(2):
            for ty in range(NA_TYPES):
                q0 = NA_TYPE_Q0[ty]
                ks = _na_window_start(q0 // NA_QROWS, NA_TYPE_KROWS[ty])
                for a in range(NA_QROWS):
                    r0 = min(max(q0 + a - WIN_H // 2, 0), GRID_ROWS - WIN_H)
                    for y in range(NA_TYPE_KROWS[ty]):
                        kr = ks + y
                        ok = r0 <= kr < r0 + WIN_H
                        blk = tab_ref[hh, kr - (q0 + a) + WIN_H - 1] if ok else masked
                        bias_ref[ty, hh * NA_NQ + a * GRID_W:hh * NA_NQ + (a + 1) * GRID_W,
                                 y * GRID_W:(y + 1) * GRID_W] = blk

    lane = lax.broadcasted_iota(jnp.int32, (1, 2 * HEAD_DIM), 1)
    own = [lane < HEAD_DIM, lane >= HEAD_DIM]
    kc_ref[...] = jnp.concatenate([ck_ref[0], ck_ref[1]], axis=1).astype(BF16)
    vc = jnp.concatenate([cv_ref[0], cv_ref[1]], axis=1)
    v = v_ref[...]
    for hh in range(2):
        vc_ref[hh] = jnp.where(own[hh], vc, 1.0).astype(BF16)
        va_ref[hh] = jnp.where(own[hh], v, jnp.ones_like(v))

    def attend(qs, nq, ks, nk, bias):
        q = q_ref[pl.ds(qs, nq), :] * jnp.asarray(HEAD_DIM ** -0.5, BF16)
        q2 = jnp.concatenate([jnp.where(own[0], q, jnp.zeros_like(q)),
                              jnp.where(own[1], q, jnp.zeros_like(q))], axis=0)
        s_loc = _dot_nt(q2, k_ref[pl.ds(ks, nk), :]) + bias
        s_ctx = _dot_nt(q2, kc_ref[...])
        m = jnp.maximum(jnp.max(s_loc, axis=-1, keepdims=True), jnp.max(s_ctx, axis=-1, keepdims=True))
        p_loc = jnp.exp(s_loc - m).astype(BF16)
        p_ctx = jnp.exp(s_ctx - m).astype(BF16)
        out = None
        for hh in range(2):
            rows = slice(hh * nq, (hh + 1) * nq)
            num = _dot(p_loc[rows], va_ref[hh, pl.ds(ks, nk), :]) + _dot(p_ctx[rows], vc_ref[hh])
            den = jnp.where(own[hh], pltpu.roll(num, HEAD_DIM, 1), 1.0)
            o = num / den
            out = o if out is None else jnp.where(own[0], out, o)
        o_ref[pl.ds(qs, nq), :] = out.astype(BF16)

    for rb in range(NA_BLOCKS):
        ty = 0 if rb == 0 else (2 if rb == NA_BLOCKS - 1 else 1)
        ws = _na_window_start(rb, NA_TYPE_KROWS[ty])
        if ty != 1:
            nk = NA_TYPE_KROWS[ty] * GRID_W
            attend(rb * NA_NQ, NA_NQ, ws * GRID_W, nk, bias_ref[ty, :, 0:nk])
            continue
        half = NA_NQ // 2
        nk = (NA_KROWS - 2) * GRID_W
        for g in range(2):
            lanes = slice(g * 2 * GRID_W, g * 2 * GRID_W + nk)
            bias = jnp.concatenate([bias_ref[ty, hh * NA_NQ + g * half:hh * NA_NQ + (g + 1) * half, lanes]
                                    for hh in range(2)], axis=0)
            attend(rb * NA_NQ + g * half, half, (ws + 2 * g) * GRID_W, nk, bias)


def _na_attn(proj, cache_k, cache_v, col_table, layer_idx):
    row_blk0 = N_PROMPT_TOK // DEC_SEQ
    hp = N_HEADS // 2
    return pl.pallas_call(
        _na_body,
        grid=(hp, DEC_BATCH),
        in_specs=[
            pl.BlockSpec((DEC_SEQ, 128), lambda h, b: (row_blk0 + b, h)),
            pl.BlockSpec((DEC_SEQ, 128), lambda h, b: (row_blk0 + b, hp + h)),
            pl.BlockSpec((DEC_SEQ, 128), lambda h, b: (row_blk0 + b, 2 * hp + h)),
            pl.BlockSpec((None, None, 2, PAST_LEN, HEAD_DIM), lambda h, b: (b, layer_idx, h, 0, 0)),
            pl.BlockSpec((None, None, 2, PAST_LEN, HEAD_DIM), lambda h, b: (b, layer_idx, h, 0, 0)),
            pl.BlockSpec((2, 2 * WIN_H - 1, GRID_W, GRID_W), lambda h, b: (h, 0, 0, 0)),
        ],
        out_specs=pl.BlockSpec((DEC_SEQ, 128), lambda h, b: (b, h)),
        out_shape=jax.ShapeDtypeStruct((N_SAMPLE_TOK, HALF), BF16),
        scratch_shapes=[pltpu.VMEM((NA_TYPES, 2 * NA_NQ, NA_NK), F32),
                        pltpu.VMEM((PAST_LEN, 2 * HEAD_DIM), BF16),
                        pltpu.VMEM((2, PAST_LEN, 2 * HEAD_DIM), BF16),
                        pltpu.VMEM((2, DEC_SEQ, 2 * HEAD_DIM), BF16)],
        compiler_params=_cparams(("parallel", "arbitrary")),
        name="na_attn",
    )(proj, proj, proj, cache_k, cache_v, col_table)


N_POW = 24


PREP_PPS = 2


def _s5_prep_body(*refs):
    for n in range(PREP_PPS):
        _s5_prep_pair(PREP_PPS * pl.program_id(0) + n, *[r.at[n] for r in refs])


def _s5_prep_pair(p, vec_ref, bt_ref, c_ref, wend_ref, ktoe_ref, woutT_ref, apow_ref, pw_ref, pct_ref):
    L = S5_CHUNK
    lane_blk = lax.broadcasted_iota(jnp.int32, (S5_CH, S5_W), 1) // S5_CH
    src_blk = (lane_blk + L - p) % L
    krows = []
    for d in range(2):
        v = vec_ref[d]
        lre, lim, step = v[0:1], v[1:2], v[2:3]
        a = lre * step
        b = lim * step
        ea = jnp.exp(a)
        nr = ea * jnp.cos(b) - 1.0
        ni = ea * jnp.sin(b)
        den = lre * lre + lim * lim
        qr = (nr * lre + ni * lim) / den
        qi = (ni * lre - nr * lim) / den
        bre, bim = bt_ref[d, 0], bt_ref[d, 1]
        bqr = bre * qr - bim * qi
        bqi = bre * qi + bim * qr
        cre, cim = c_ref[d, 0], c_ref[d, 1]

        e = lax.broadcasted_iota(jnp.int32, (N_POW, 1), 0).astype(F32)
        mag = jnp.exp(e * a)
        pw_ref[0] = mag * jnp.cos(e * b)
        pw_ref[1] = mag * jnp.sin(e * b)
        mag = jnp.exp((L * e) * a)
        apow_ref[d, 0] = mag * jnp.cos((L * e) * b)
        apow_ref[d, 1] = mag * jnp.sin((L * e) * b)

        def power(k):
            return pw_ref[0, k:k + 1, :], pw_ref[1, k:k + 1, :]

        def c_block(k):
            pr, pi = power(k)
            return cre * pr - cim * pi, -(cre * pi + cim * pr)

        for i in range(L):
            rows = pl.ds(pl.multiple_of(((i + p) % L) * S5_CH, S5_CH), S5_CH)
            pr, pi = power(L - 1 - i if d == 0 else i)
            wend_ref[rows, d * 256:d * 256 + 128] = (bqr * pr - bqi * pi).astype(BF16)
            wend_ref[rows, d * 256 + 128:d * 256 + 256] = (bqr * pi + bqi * pr).astype(BF16)
            mre, mim = c_block(i + 1 if d == 0 else L - i)
            woutT_ref[rows, d * 256:d * 256 + 128] = mre.astype(BF16)
            woutT_ref[rows, d * 256 + 128:d * 256 + 256] = mim.astype(BF16)
            lag_rows = slice(i * S5_CH, (i + 1) * S5_CH)
            mre, mim = c_block(i if d == 0 else L - 1 - i)
            pct_ref[lag_rows, 0:128] = mre
            pct_ref[lag_rows, 128:256] = mim

        bqcat = jnp.concatenate([bqr, bqi], axis=1)
        krows.append(_dot_nt(bqcat, pct_ref[...], precision=lax.Precision.HIGHEST))

    base_f = pltpu.roll(krows[0], p * S5_CH, 1)
    base_b = pltpu.roll(krows[1], p * S5_CH, 1)
    for i in range(L):
        rows = pl.ds(pl.multiple_of(((i + p) % L) * S5_CH, S5_CH), S5_CH)
        blk_f = base_f if i == 0 else pltpu.roll(base_f, i * S5_CH, 1)
        blk_b = base_b if (i + 1) % L == 0 else pltpu.roll(base_b, ((i + 1) % L) * S5_CH, 1)
        blk = jnp.where(src_blk >= i, blk_f, 0.0) + jnp.where(src_blk <= i, blk_b, 0.0)
        ktoe_ref[rows, :] = blk.astype(BF16)


def _s5_prep(lam_re, lam_im, log_step, b_re, b_im, c_re, c_im):
    def pair_vec(a):
        return a.astype(F32).reshape(2, S5_PAIRS, 1, S5_P).transpose(1, 0, 2, 3)

    step = jnp.broadcast_to(jnp.exp(log_step.astype(F32))[:, :, None], (2, N_SSM_GROUPS, SSM_STATE))
    vec = jnp.concatenate([pair_vec(lam_re), pair_vec(lam_im), pair_vec(step),
                           jnp.zeros((S5_PAIRS, 2, 5, S5_P), F32)], axis=2)

    def block_diag_ch_by_state(m):
        m = m.astype(F32).reshape(2, S5_PAIRS, 2, SSM_GROUP, SSM_STATE)
        eye = jnp.eye(2, dtype=F32)
        out = m[:, :, :, :, None, :] * eye[None, None, :, None, :, None]
        return out.reshape(2, S5_PAIRS, S5_CH, S5_P).transpose(1, 0, 2, 3)

    bt = jnp.stack([block_diag_ch_by_state(b_re.transpose(0, 1, 3, 2)),
                    block_diag_ch_by_state(b_im.transpose(0, 1, 3, 2))], axis=2)
    cc = jnp.stack([block_diag_ch_by_state(c_re), block_diag_ch_by_state(c_im)], axis=2)

    mat = jax.ShapeDtypeStruct((S5_PAIRS, S5_W, S5_W), BF16)
    mat_spec = pl.BlockSpec((PREP_PPS, S5_W, S5_W), lambda p: (p, 0, 0))
    return pl.pallas_call(
        _s5_prep_body,
        grid=(S5_PAIRS // PREP_PPS,),
        in_specs=[
            pl.BlockSpec((PREP_PPS, 2, 8, S5_P), lambda p: (p, 0, 0, 0)),
            pl.BlockSpec((PREP_PPS, 2, 2, S5_CH, S5_P), lambda p: (p, 0, 0, 0, 0)),
            pl.BlockSpec((PREP_PPS, 2, 2, S5_CH, S5_P), lambda p: (p, 0, 0, 0, 0)),
        ],
        out_specs=[mat_spec, mat_spec, mat_spec,
                   pl.BlockSpec((PREP_PPS, 2, 2, N_POW, S5_P), lambda p: (p, 0, 0, 0, 0))],
        out_shape=[mat, mat, mat, jax.ShapeDtypeStruct((S5_PAIRS, 2, 2, N_POW, S5_P), F32)],
        scratch_shapes=[pltpu.VMEM((PREP_PPS, 2, N_POW, S5_P), F32), pltpu.VMEM((PREP_PPS, S5_W, 2 * S5_P), F32)],
        compiler_params=_cparams(("parallel",)),
        name="s5_prep",
    )(vec, bt, cc)


S5_SEG_ROWS = 256
S5_N_SEG = N_CHUNK_ROWS // S5_SEG_ROWS
S5_SUB = S5_SEG_ROWS // S5_CHUNK
S5_SUB_PER_SEQ = (DEC_SEQ // S5_CHUNK) // S5_CHUNK


S5_PPS = 4


def _s5_body(tu_ref, s0_ref, wend_ref, ktoe_ref, woutT_ref, apow_ref, ty_ref, fin_ref, z_ref, xs_ref, g_ref):
    seg = pl.program_id(0)
    L = S5_CHUNK
    lane_blk = lax.broadcasted_iota(jnp.int32, (S5_SEG_ROWS, 128), 1) // S5_CH
    pairs = [S5_PPS * pl.program_id(1) + n for n in range(S5_PPS)]

    s_rows, ends = [], []
    for n, p in enumerate(pairs):
        tiles = []
        for t in range(LANE_TILES):
            tile = None
            for k in range(128 // S5_CH):
                j = (LANE_TILES * t + k + L - p) % L
                blk = tu_ref[j, t]
                tile = blk if tile is None else jnp.where(lane_blk == k, blk, tile)
            tiles.append(tile)
        s = jnp.concatenate(tiles, axis=1)
        s_rows.append(s)
        z = _dot(s, wend_ref[n])
        for q in range(LANE_TILES):
            z_ref[n, q] = z[:, q * 128:(q + 1) * 128]

        pair_ends = []
        for d in range(2):
            are, aim = apow_ref[n, d, 0, 1:2, :], apow_ref[n, d, 1, 1:2, :]
            xr = jnp.zeros((S5_SUB, S5_P), F32)
            xi = jnp.zeros((S5_SUB, S5_P), F32)
            for c in (range(L) if d == 0 else reversed(range(L))):
                rows = pl.ds(c, S5_SUB, stride=L)
                xs_ref[n, 2 * d, rows, :] = xr
                xs_ref[n, 2 * d + 1, rows, :] = xi
                zr = z_ref[n, 2 * d, rows, :]
                zi = z_ref[n, 2 * d + 1, rows, :]
                xr, xi = are * xr - aim * xi + zr, are * xi + aim * xr + zi
            pair_ends.append((xr, xi))
            fin_ref[n, d, 0] = xr
            fin_ref[n, d, 1] = xi
        ends.append(pair_ends)

    @pl.when(seg > 0)
    def _():
        for n in range(S5_PPS):
            for d in range(2):
                er, ei = ends[n][d]
                a16r, a16i = apow_ref[n, d, 0, L:L + 1, :], apow_ref[n, d, 1, L:L + 1, :]
                for bb in range(S5_SUB // S5_SUB_PER_SEQ):
                    gr = s0_ref[n, bb, 2 * d:2 * d + 1, :]
                    gi = s0_ref[n, bb, 2 * d + 1:2 * d + 2, :]
                    order = range(S5_SUB_PER_SEQ) if d == 0 else reversed(range(S5_SUB_PER_SEQ))
                    prev = None
                    for sub in order:
                        row = bb * S5_SUB_PER_SEQ + sub
                        if prev is not None:
                            gr, gi = (a16r * gr - a16i * gi + er[prev:prev + 1, :],
                                      a16r * gi + a16i * gr + ei[prev:prev + 1, :])
                        g_ref[n, 0, row:row + 1, :] = gr
                        g_ref[n, 1, row:row + 1, :] = gi
                        prev = row
                if d == 0:
                    pr, pi = apow_ref[n, d, 0, 0:L, :], apow_ref[n, d, 1, 0:L, :]
                else:
                    pr = jnp.concatenate([apow_ref[n, d, 0, k:k + 1, :] for k in reversed(range(L))], axis=0)
                    pi = jnp.concatenate([apow_ref[n, d, 1, k:k + 1, :] for k in reversed(range(L))], axis=0)
                for sub in range(S5_SUB):
                    rows = slice(sub * L, (sub + 1) * L)
                    gr = g_ref[n, 0, sub:sub + 1, :]
                    gi = g_ref[n, 1, sub:sub + 1, :]
                    xs_ref[n, 2 * d, rows, :] = xs_ref[n, 2 * d, rows, :] + (pr * gr - pi * gi)
                    xs_ref[n, 2 * d + 1, rows, :] = xs_ref[n, 2 * d + 1, rows, :] + (pr * gi + pi * gr)

    @pl.when(pl.program_id(1) == 0)
    def _():
        ty_ref[...] = jnp.zeros_like(ty_ref)

    for n, p in enumerate(pairs):
        xs = jnp.concatenate([xs_ref[n, q].astype(BF16) for q in range(LANE_TILES)], axis=1)
        y = _dot(s_rows[n], ktoe_ref[n]) + _dot_nt(xs, woutT_ref[n])
        for t in range(LANE_TILES):
            yt = y[:, t * 128:(t + 1) * 128]
            for k in range(128 // S5_CH):
                j = (LANE_TILES * t + k + L - p) % L
                ty_ref[j, t] = jnp.where(lane_blk == k, yt, ty_ref[j, t])


def _s5_scan(tu, s0, wend, ktoe, woutT, apow):
    mat_spec = pl.BlockSpec((S5_PPS, S5_W, S5_W), lambda g, p: (p, 0, 0))
    seqs = S5_SUB // S5_SUB_PER_SEQ
    return pl.pallas_call(
        _s5_body,
        grid=(S5_N_SEG, S5_PAIRS // S5_PPS),
        in_specs=[
            pl.BlockSpec((S5_CHUNK, LANE_TILES, S5_SEG_ROWS, 128), lambda g, p: (0, 0, g, 0)),
            pl.BlockSpec((S5_PPS, seqs, 4, S5_P), lambda g, p: (p, jnp.maximum(g - 1, 0), 0, 0)),
            mat_spec, mat_spec, mat_spec,
            pl.BlockSpec((S5_PPS, 2, 2, N_POW, S5_P), lambda g, p: (p, 0, 0, 0, 0)),
        ],
        out_specs=[
            pl.BlockSpec((S5_CHUNK, LANE_TILES, S5_SEG_ROWS, 128), lambda g, p: (0, 0, g, 0)),
            pl.BlockSpec((S5_PPS, 2, 2, S5_SUB, S5_P), lambda g, p: (p, 0, 0, g, 0)),
        ],
        out_shape=[
            jax.ShapeDtypeStruct((S5_CHUNK, LANE_TILES, N_CHUNK_ROWS, 128), F32),
            jax.ShapeDtypeStruct((S5_PAIRS, 2, 2, S5_N_SEG * S5_SUB, S5_P), F32),
        ],
        scratch_shapes=[pltpu.VMEM((S5_PPS, LANE_TILES, S5_SEG_ROWS, 128), F32),
                        pltpu.VMEM((S5_PPS, LANE_TILES, S5_SEG_ROWS, 128), F32),
                        pltpu.VMEM((S5_PPS, 2, S5_SUB, S5_P), F32)],
        compiler_params=_cparams(("parallel", "arbitrary")),
        name="s5_scan",
    )(tu, s0, wend, ktoe, woutT, apow)


TM_GLU = 2048


def _glu_body(tu_ref, ty_ref, d_ref, gw_ref, gb_ref, o_ref, out_ref):
    rows = TM_GLU // S5_CHUNK
    d = d_ref[...]
    pre = []
    for j in range(S5_CHUNK):
        uj = jnp.concatenate([tu_ref[j, t].astype(F32) for t in range(LANE_TILES)], axis=1)
        yj = jnp.concatenate([ty_ref[j, t] for t in range(LANE_TILES)], axis=1)
        dj = pltpu.roll(d, j * S5_CH, 1) if j else d
        yj = dj * uj + yj
        back = ((S5_CHUNK - j) % S5_CHUNK) * S5_CH
        pre.append(pltpu.roll(yj, back, 1) if back else yj)
    y = jnp.concatenate(pre, axis=0)
    cdf = 0.5 * (1.0 + jnp.tanh(math.sqrt(2.0 / math.pi) * (y + 0.044715 * (y * y * y))))
    y = y * cdf
    gate = jax.nn.sigmoid(_dot(y.astype(BF16), gw_ref[...]) + gb_ref[...])
    out = y * gate
    for j in range(S5_CHUNK):
        oj = out[j * rows:(j + 1) * rows]
        for t in range(LANE_TILES):
            out_ref[t, pl.ds(j, rows, stride=S5_CHUNK), :] = oj[:, t * 128:(t + 1) * 128]
    o_ref[...] = jnp.concatenate([out_ref[t] for t in range(LANE_TILES)], axis=1).astype(BF16)


def _glu(tu, ty, ssm_d, glu_w, glu_b):
    tm = TM_GLU
    rows = tm // S5_CHUNK
    lay_spec = pl.BlockSpec((S5_CHUNK, LANE_TILES, rows, 128), lambda i: (0, 0, i, 0))
    return pl.pallas_call(
        _glu_body,
        grid=(N_TOK // tm,),
        in_specs=[
            lay_spec, lay_spec,
            pl.BlockSpec((1, HALF), lambda i: (0, 0)),
            pl.BlockSpec((HALF, HALF), lambda i: (0, 0)),
            pl.BlockSpec((1, HALF), lambda i: (0, 0)),
        ],
        out_specs=pl.BlockSpec((tm, HALF), lambda i: (i, 0)),
        out_shape=jax.ShapeDtypeStruct((N_TOK, HALF), BF16),
        scratch_shapes=[pltpu.VMEM((LANE_TILES, tm, 128), F32)],
        compiler_params=_cparams(("parallel",)),
        name="s5_glu",
    )(tu, ty, ssm_d.reshape(1, HALF), glu_w.astype(BF16), glu_b.reshape(1, HALF))


def kernel(x_prompt, x_sample, c, cache_k, cache_v, state_s5_re, state_s5_im, c_ctx, norm1_g, norm2_g, ada_w, ada_b, mlp_w1, mlp_w2, ab_w_in, pool_w, pool_scale, conv_w, ab_w_out, cd_w_in, na_rpb, ssm_lambda_re, ssm_lambda_im, ssm_log_step, ssm_b_re, ssm_b_im, ssm_c_re, ssm_c_im, ssm_d, glu_w, glu_b, cd_w_out, final_g):
    depth = ada_w.shape[0]
    xs = (x_prompt.reshape(N_PROMPT_TOK, D_MODEL), x_sample.reshape(N_SAMPLE_TOK, D_MODEL))
    cond = jnp.concatenate([c_ctx[None, :], c, jnp.zeros((N_COND - 1 - DEC_BATCH, D_MODEL), F32)], axis=0)
    mods = _adaln(cond, ada_w, ada_b).reshape(depth, N_COND, 6, D_MODEL)

    new_k, new_v, new_re, new_im = [], [], [], []
    for layer in range(depth):
        i = layer // 2
        final = layer == depth - 1
        if layer % 2 == 0:
            y, x = _even_layer_mix(xs, mods, layer, norm1_g[layer], ab_w_in, i,
                                   pool_w[i], pool_scale[i], conv_w[i])
            ya, ya_col, yb, yb_col = y, 0, y, 1
            w_out_stack = ab_w_out
        else:
            x = xs[0]
            qkv, tu = _inproj_odd(x, mods, layer, norm1_g[layer], cd_w_in, i)
            o_p, ck, cv = _ctx_attn(qkv)
            new_k.append(ck)
            new_v.append(cv)
            o_s = _na_attn(qkv, cache_k, cache_v, _na_col_table(na_rpb[i]), i)
            attn = (o_p, o_s)

            wend, ktoe, woutT, apow = _s5_prep(ssm_lambda_re[i], ssm_lambda_im[i], ssm_log_step[i],
                                               ssm_b_re[i], ssm_b_im[i], ssm_c_re[i], ssm_c_im[i])

            def pair_state(s):
                return s.astype(F32).reshape(DEC_BATCH, 2, S5_PAIRS, S5_P).transpose(2, 0, 1, 3)

            s0 = jnp.stack([pair_state(state_s5_re[:, i]), pair_state(state_s5_im[:, i])], axis=3)
            s0 = s0.reshape(S5_PAIRS, DEC_BATCH, 4, S5_P)
            ty, fin = _s5_scan(tu, s0, wend, ktoe, woutT, apow)
            fin = fin[:, :, :, :BATCH]
            new_re.append(fin[:, :, 0].transpose(2, 1, 0, 3).reshape(BATCH, 2, N_SSM_GROUPS, SSM_STATE))
            new_im.append(fin[:, :, 1].transpose(2, 1, 0, 3).reshape(BATCH, 2, N_SSM_GROUPS, SSM_STATE))
            d_out = _glu(tu, ty, ssm_d[i], glu_w[i], glu_b[i])
            ya, ya_col, yb, yb_col = attn, 0, d_out, 0
            w_out_stack = cd_w_out
        xs = _mlp(x, ya, ya_col, yb, yb_col, w_out_stack, i, mods, layer, norm2_g[layer], final_g,
                  mlp_w1, mlp_w2, final)

    y_prompt = xs[0].reshape(BATCH, SEQ, D_MODEL)
    y_sample = xs[1].reshape(DEC_BATCH, DEC_SEQ, D_MODEL)
    return (y_prompt, y_sample, jnp.concatenate(new_k, axis=1), jnp.concatenate(new_v, axis=1),
            jnp.stack(new_re, axis=1), jnp.stack(new_im, axis=1))
```

```python
import functools
import math

import jax
import jax.numpy as jnp
from jax import lax
from jax.experimental import pallas as pl
from jax.experimental.pallas import tpu as pltpu

F32 = jnp.float32
BF16 = jnp.bfloat16

D_MODEL = 1024
BATCH = 16
SEQ = 256
DEC_BATCH = 4
DEC_SEQ = 2048
PAST_LEN = 256
GRID_W = 64
GRID_ROWS = DEC_SEQ // GRID_W
HALF = 512
POOL_WINDOWS = (2, 4, 8, 16)
POOL_GROUP = 128
HEAD_DIM = 64
N_HEADS = 8
WIN_H = 8
WIN_W = 16
SSM_GROUP = 16
N_SSM_GROUPS = 32
SSM_STATE = 64
D_FF = 4096
EPS = 1e-6

N_PROMPT_TOK = BATCH * SEQ
N_SAMPLE_TOK = DEC_BATCH * DEC_SEQ
N_TOK = N_PROMPT_TOK + N_SAMPLE_TOK
N_COND = 8

S5_CHUNK = 16
S5_PAIRS = N_SSM_GROUPS // 2
S5_CH = 2 * SSM_GROUP
S5_P = 2 * SSM_STATE
S5_W = S5_CHUNK * S5_CH

VMEM_LIMIT = 56 * 1024 * 1024


def _cparams(sem):
    return pltpu.CompilerParams(dimension_semantics=sem, vmem_limit_bytes=VMEM_LIMIT)


def _cond_row(i, tm):
    npt = N_PROMPT_TOK // tm
    per = DEC_SEQ // tm
    return jnp.where(i < npt, 0, 1 + (i - npt) // per)


def _resident_layer(shape, idx):
    zeros = (0,) * len(shape)
    return pl.BlockSpec((None,) + tuple(shape), lambda i: (idx,) + zeros, pipeline_mode=pl.Buffered(1))


def _normmod(x, g, shift, scale):
    ms = jnp.mean(x * x, axis=-1, keepdims=True)
    return (x * lax.rsqrt(ms + EPS) * g) * (1.0 + scale) + shift


def _dot(a, b):
    return jnp.dot(a, b, preferred_element_type=F32)


def _dot_nt(a, b, precision=None):
    return lax.dot_general(a, b, (((1,), (1,)), ((), ())), preferred_element_type=F32, precision=precision)


def _adaln_body(c_ref, w_ref, b_ref, o_ref):
    c = c_ref[...]
    s = c * jax.nn.sigmoid(c)
    o_ref[...] = _dot(s.astype(BF16), w_ref[...].astype(BF16)) + b_ref[...]


def _adaln(cond, ada_w, ada_b):
    depth = ada_w.shape[0]
    nj = 6
    return pl.pallas_call(
        _adaln_body,
        grid=(depth, nj),
        in_specs=[
            pl.BlockSpec((N_COND, D_MODEL), lambda l, j: (0, 0)),
            pl.BlockSpec((None, D_MODEL, D_MODEL), lambda l, j: (l, 0, j)),
            pl.BlockSpec((None, 1, D_MODEL), lambda l, j: (l, 0, j)),
        ],
        out_specs=pl.BlockSpec((None, N_COND, D_MODEL), lambda l, j: (l, 0, j)),
        out_shape=jax.ShapeDtypeStruct((depth, N_COND, 6 * D_MODEL), F32),
        compiler_params=_cparams(("arbitrary", "arbitrary")),
        name="adaln",
    )(cond, ada_w, ada_b.reshape(depth, 1, 6 * D_MODEL))


TM_PROJ = 1024


def _stream_tile(x_refs, tm):
    if len(x_refs) == 1:
        return x_refs[0][...]
    is_prompt = pl.program_id(0) < N_PROMPT_TOK // tm
    return jnp.where(is_prompt, x_refs[0][...], x_refs[1][...])


N_CHUNK_ROWS = N_TOK // S5_CHUNK
LANE_TILES = S5_W // 128


def _inproj_odd_body(x_ref, m_ref, g_ref, w_ref, qkv_ref, tu_ref, u_ref, wbf_ref):
    @pl.when(pl.program_id(0) == 0)
    def _():
        wbf_ref[...] = w_ref[...].astype(BF16)

    m = m_ref[...]
    h = _normmod(x_ref[...], g_ref[...], m[0:1], m[1:2])
    acc = _dot(h.astype(BF16), wbf_ref[...])
    qkv_ref[...] = acc[:, 0:3 * HALF].astype(BF16)
    for t in range(LANE_TILES):
        u_ref[t] = acc[:, 3 * HALF + t * 128:3 * HALF + (t + 1) * 128]
    rows = TM_PROJ // S5_CHUNK
    for j in range(S5_CHUNK):
        uj = jnp.concatenate([u_ref[t, pl.ds(j, rows, stride=S5_CHUNK), :] for t in range(LANE_TILES)], axis=1)
        if j:
            uj = pltpu.roll(uj, j * S5_CH, 1)
        for t in range(LANE_TILES):
            tu_ref[j, t] = uj[:, t * 128:(t + 1) * 128].astype(BF16)


def _inproj_odd(x, mods, layer, g, w_stack, w_idx):
    tm = TM_PROJ
    rows = tm // S5_CHUNK
    return pl.pallas_call(
        _inproj_odd_body,
        grid=(N_TOK // tm,),
        in_specs=[
            pl.BlockSpec((tm, D_MODEL), lambda i: (i, 0)),
            pl.BlockSpec((None, None, 6, D_MODEL), lambda i: (layer, _cond_row(i, tm), 0, 0)),
            pl.BlockSpec((1, D_MODEL), lambda i: (0, 0)),
            _resident_layer((D_MODEL, 4 * HALF), w_idx),
        ],
        out_specs=[
            pl.BlockSpec((tm, 3 * HALF), lambda i: (i, 0)),
            pl.BlockSpec((S5_CHUNK, LANE_TILES, rows, 128), lambda i: (0, 0, i, 0)),
        ],
        out_shape=[
            jax.ShapeDtypeStruct((N_TOK, 3 * HALF), BF16),
            jax.ShapeDtypeStruct((S5_CHUNK, LANE_TILES, N_CHUNK_ROWS, 128), BF16),
        ],
        scratch_shapes=[pltpu.VMEM((LANE_TILES, tm, 128), F32), pltpu.VMEM((D_MODEL, 4 * HALF), BF16)],
        compiler_params=_cparams(("arbitrary",)),
        name="inproj_odd",
    )(x, mods, g.reshape(1, D_MODEL), w_stack)


TM_MIX = 256
HALO = 16
RING = 3
LAG = 2
PAD_ROWS = TM_MIX + 2 * HALO + 8


def _even_step(phase, x_refs, m_ref, g_ref, w_ref, pw_ref, ps_ref, cw_ref, y_ref, xcat_ref, proj_ref, tail_ref,
               h_ref):
    i = pl.program_id(0)
    write_slot, cur_slot, next_slot = phase, (phase + 1) % RING, (phase + 2) % RING

    m = m_ref[...]
    x = _stream_tile(x_refs, TM_MIX)
    xcat_ref[...] = x
    h_ref[...] = _normmod(x, g_ref[...], m[0:1], m[1:2]).astype(BF16)
    n_pieces = len(POOL_WINDOWS) + HALF // 128
    pw = (4 * HALF) // n_pieces

    def project(piece):
        cols = slice(piece * pw, (piece + 1) * pw)
        proj_ref[write_slot, :, cols] = _dot(h_ref[...], w_ref[:, cols])

    t = jnp.maximum(i - LAG, 0)
    n_prompt_tiles = N_PROMPT_TOK // TM_MIX
    tiles_per_seq = DEC_SEQ // TM_MIX
    is_prompt = t < n_prompt_tiles
    tile_in_seq = jnp.where(is_prompt, 0, (t - n_prompt_tiles) % tiles_per_seq)
    seq_len = jnp.where(is_prompt, SEQ, DEC_SEQ)
    has_prev = tile_in_seq > 0
    has_next = jnp.logical_and(jnp.logical_not(is_prompt), tile_in_seq < tiles_per_seq - 1)
    prev_keep = jnp.where(has_prev, 1.0, 0.0).astype(F32)
    next_keep = jnp.where(has_next, 1.0, 0.0).astype(F32)
    pos = tile_in_seq * TM_MIX + lax.broadcasted_iota(jnp.int32, (TM_MIX, 1), 0)
    lo_rows, hi_rows = HALO, HALO + TM_MIX
    head = slice(0, HALO)

    slack = jnp.zeros((PAD_ROWS - TM_MIX - 2 * HALO, POOL_GROUP), F32)
    for gi, w in enumerate(POOL_WINDOWS):
        project(gi)
        cs = slice(gi * POOL_GROUP, (gi + 1) * POOL_GROUP)
        cur = proj_ref[cur_slot, :, cs]
        f = jnp.concatenate([tail_ref[:, cs] * prev_keep, cur, proj_ref[next_slot, head, cs] * next_keep, slack],
                            axis=0)
        n_rows, span = PAD_ROWS, 1
        while 2 * span < w:
            n_rows -= 8
            f = f[0:n_rows] + f[span:span + n_rows]
            span *= 2
        s = f[lo_rows - span:hi_rows - span] + f[lo_rows:hi_rows]
        lo = jnp.maximum(pos - w // 2, 0)
        hi = jnp.minimum(pos + (w - w // 2), seq_len)
        inv = 1.0 / (hi - lo).astype(F32)
        pooled = s * inv - cur
        mixed = _dot(pooled.astype(BF16), pw_ref[gi]) * ps_ref[:, cs]
        y_ref[:, cs] = mixed.astype(BF16)

    for ci in range(HALF // 128):
        project(len(POOL_WINDOWS) + ci)
        cs = slice(ci * 128, (ci + 1) * 128)
        bs = slice(HALF + ci * 128, HALF + (ci + 1) * 128)
        gs = slice(2 * HALF + ci * 128, 2 * HALF + (ci + 1) * 128)
        vs = slice(3 * HALF + ci * 128, 3 * HALF + (ci + 1) * 128)
        z = jnp.concatenate([(tail_ref[:, gs] * tail_ref[:, vs]) * prev_keep,
                             proj_ref[cur_slot, :, gs] * proj_ref[cur_slot, :, vs],
                             (proj_ref[next_slot, head, gs] * proj_ref[next_slot, head, vs]) * next_keep], axis=0)
        conv = (cw_ref[0:1, cs] * z[lo_rows - 1:hi_rows - 1]
                + cw_ref[1:2, cs] * z[lo_rows:hi_rows]
                + cw_ref[2:3, cs] * z[lo_rows + 1:hi_rows + 1])
        y_ref[:, bs] = (proj_ref[cur_slot, :, bs] * conv).astype(BF16)

    tail_ref[...] = proj_ref[cur_slot, TM_MIX - HALO:TM_MIX, :]


def _even_body(n_x, *refs):
    x_refs, rest = refs[:n_x], refs[n_x:]
    m_ref, g_ref, w_ref = rest[:3]
    proj_ref, tail_ref, h_ref, wbf_ref = rest[-4:]
    i = pl.program_id(0)

    @pl.when(i == 0)
    def _():
        proj_ref[...] = jnp.zeros_like(proj_ref)
        tail_ref[...] = jnp.zeros_like(tail_ref)
        wbf_ref[...] = w_ref[...].astype(BF16)

    for phase in range(RING):
        @pl.when(i % RING == phase)
        def _(phase=phase):
            _even_step(phase, x_refs, m_ref, g_ref, wbf_ref, *rest[3:-1])


def _even_layer_mix(xs, mods, layer, g, w_stack, w_idx, pool_w, pool_scale, conv_w):
    tm = TM_MIX
    n_tiles = N_TOK // tm
    npt = N_PROMPT_TOK // tm
    last = n_tiles - 1
    if len(xs) == 1:
        x_specs = [pl.BlockSpec((tm, D_MODEL), lambda i: (jnp.minimum(i, last), 0))]
    else:
        x_specs = [pl.BlockSpec((tm, D_MODEL), lambda i: (jnp.minimum(i, npt - 1), 0)),
                   pl.BlockSpec((tm, D_MODEL), lambda i: (jnp.clip(i - npt, 0, last - npt), 0))]
    return pl.pallas_call(
        functools.partial(_even_body, len(xs)),
        grid=(n_tiles + LAG,),
        in_specs=x_specs + [
            pl.BlockSpec((None, None, 6, D_MODEL), lambda i: (layer, _cond_row(jnp.minimum(i, last), tm), 0, 0)),
            pl.BlockSpec((1, D_MODEL), lambda i: (0, 0)),
            _resident_layer((D_MODEL, 4 * HALF), w_idx),
            pl.BlockSpec((len(POOL_WINDOWS), POOL_GROUP, POOL_GROUP), lambda i: (0, 0, 0)),
            pl.BlockSpec((1, HALF), lambda i: (0, 0)),
            pl.BlockSpec((3, HALF), lambda i: (0, 0)),
        ],
        out_specs=[pl.BlockSpec((tm, 2 * HALF), lambda i: (jnp.maximum(i - LAG, 0), 0)),
                   pl.BlockSpec((tm, D_MODEL), lambda i: (jnp.minimum(i, last), 0))],
        out_shape=[jax.ShapeDtypeStruct((N_TOK, 2 * HALF), BF16), jax.ShapeDtypeStruct((N_TOK, D_MODEL), F32)],
        scratch_shapes=[pltpu.VMEM((RING, tm, 4 * HALF), F32), pltpu.VMEM((HALO, 4 * HALF), F32),
                        pltpu.VMEM((tm, D_MODEL), BF16), pltpu.VMEM((D_MODEL, 4 * HALF), BF16)],
        compiler_params=_cparams(("arbitrary",)),
        name="even_layer_mix",
    )(*xs, mods, g.reshape(1, D_MODEL), w_stack, pool_w.astype(BF16), pool_scale.reshape(1, HALF), conv_w)


TM_MLP = 1024
RC_MLP = 512
N_CAST = 16


def _mlp_tile(final, x_ref, ya_ref, yb_ref, m_ref, g_ref, fg_ref, o_ref, wo_ref, w1_ref, w2_ref):
    m = m_ref[...]
    gain = g_ref[...] * (1.0 + m[4:5])
    for r in range(TM_MLP // RC_MLP):
        rows = slice(r * RC_MLP, (r + 1) * RC_MLP)
        mix = _dot(ya_ref[rows, :], wo_ref[0:HALF, :]) + _dot(yb_ref[rows, :], wo_ref[HALF:2 * HALF, :])
        x1 = x_ref[rows, :] + m[2:3] * mix
        ms = jnp.mean(x1 * x1, axis=-1, keepdims=True)
        h = (x1 * lax.rsqrt(ms + EPS) * gain + m[3:4]).astype(BF16)
        t = _dot(h, w1_ref[...])
        t = jnp.square(jnp.maximum(t, 0.0)).astype(BF16)
        out = x1 + m[5:6] * _dot(t, w2_ref[...])
        if final:
            ms = jnp.mean(out * out, axis=-1, keepdims=True)
            out = out * lax.rsqrt(ms + EPS) * fg_ref[...]
        o_ref[rows, :] = out


def _mlp_resident_body(final, x_ref, ya_ref, yb_ref, wo_ref, m_ref, g_ref, fg_ref, w1_ref, w2_ref, o_ref):
    _mlp_tile(final, x_ref, ya_ref, yb_ref, m_ref, g_ref, fg_ref, o_ref, wo_ref, w1_ref, w2_ref)


def _mlp_cast_body(final, emit, x_ref, ya_ref, yb_ref, wo_f, m_ref, g_ref, fg_ref, w1_f, w2_f, o_ref, *rest):
    wo_ref, w1_ref, w2_ref = rest[-3:]
    emitted = rest[:-3]
    i = pl.program_id(0)

    @pl.when(i < N_CAST)
    def _():
        for k, (f_ref, b_ref) in enumerate(((wo_f, wo_ref), (w1_f, w1_ref), (w2_f, w2_ref))):
            n = f_ref.shape[0]
            slab = f_ref[...].astype(BF16)
            b_ref[pl.ds(pl.multiple_of(i * n, n), n), :] = slab
            if emit:
                emitted[k][...] = slab

    @pl.when(i >= N_CAST)
    def _():
        _mlp_tile(final, x_ref, ya_ref, yb_ref, m_ref, g_ref, fg_ref, o_ref, wo_ref, w1_ref, w2_ref)


def _resident(shape):
    zeros = (0,) * len(shape)
    return pl.BlockSpec(shape, lambda i: zeros, pipeline_mode=pl.Buffered(1))


def _mlp_call(x, x_off, ya, ya_off, ya_col, yb, yb_off, yb_col, n_tiles, tile0, mods, layer, g2, final_g, final,
              f32_weights=None, bf16_weights=None, emit=False):
    tm = TM_MLP
    lead = N_CAST if f32_weights is not None else 0

    def tile(i):
        return jnp.maximum(i - lead, 0)

    stream_specs = [
        pl.BlockSpec((tm, D_MODEL), lambda i: (tile(i) + x_off, 0)),
        pl.BlockSpec((tm, HALF), lambda i: (tile(i) + ya_off, ya_col)),
        pl.BlockSpec((tm, HALF), lambda i: (tile(i) + yb_off, yb_col)),
    ]
    mod_spec = pl.BlockSpec((None, None, 6, D_MODEL), lambda i: (layer, _cond_row(tile(i) + tile0, tm), 0, 0))
    out_spec = pl.BlockSpec((tm, D_MODEL), lambda i: (tile(i), 0))
    out_shape = jax.ShapeDtypeStruct((n_tiles * tm, D_MODEL), F32)
    vecs = (g2.reshape(1, D_MODEL), final_g.reshape(1, D_MODEL))
    if f32_weights is None:
        w_out, w1, w2 = bf16_weights
        return pl.pallas_call(
            functools.partial(_mlp_resident_body, final),
            grid=(n_tiles,),
            in_specs=stream_specs + [_resident((D_MODEL, D_MODEL)), mod_spec, _resident((1, D_MODEL)),
                                     _resident((1, D_MODEL)), _resident((D_MODEL, D_FF)), _resident((D_FF, D_MODEL))],
            out_specs=out_spec,
            out_shape=out_shape,
            compiler_params=_cparams(("parallel",)),
            name="mlp",
        )(x, ya, yb, w_out, mods, *vecs, w1, w2)

    wo_stack, wo_idx, w1_stack, w2_stack = f32_weights

    def slab(i):
        return jnp.minimum(i, N_CAST - 1)

    shapes = ((D_MODEL, D_MODEL), (D_MODEL, D_FF), (D_FF, D_MODEL))
    slab_rows = [rows // N_CAST for rows, _ in shapes]
    f32_specs = [pl.BlockSpec((None, n, cols), lambda i, idx=idx: (idx, slab(i), 0))
                 for n, (_, cols), idx in zip(slab_rows, shapes, (wo_idx, layer, layer))]
    out_specs, out_shapes = [out_spec], [out_shape]
    if emit:
        out_specs += [pl.BlockSpec((n, cols), lambda i: (slab(i), 0)) for n, (_, cols) in zip(slab_rows, shapes)]
        out_shapes += [jax.ShapeDtypeStruct(shp, BF16) for shp in shapes]
    return pl.pallas_call(
        functools.partial(_mlp_cast_body, final, emit),
        grid=(N_CAST + n_tiles,),
        in_specs=stream_specs + [f32_specs[0], mod_spec, _resident((1, D_MODEL)), _resident((1, D_MODEL)),
                                 f32_specs[1], f32_specs[2]],
        out_specs=out_specs,
        out_shape=out_shapes,
        scratch_shapes=[pltpu.VMEM(shp, BF16) for shp in shapes],
        compiler_params=_cparams(("arbitrary",)),
        name="mlp_cast",
    )(x, ya, yb, wo_stack, mods, *vecs, w1_stack, w2_stack)


def _mlp(x, ya, ya_col, yb, yb_col, w_out_stack, w_out_idx, mods, layer, g2, final_g, w1_stack, w2_stack, final):
    tm = TM_MLP
    npt = N_PROMPT_TOK // tm
    f32_weights = (w_out_stack, w_out_idx, w1_stack, w2_stack)
    common = (mods, layer, g2, final_g, final)
    if not final:
        if isinstance(ya, tuple):
            ya = jnp.concatenate(ya, axis=0)
        return tuple(_mlp_call(x, 0, ya, 0, ya_col, yb, 0, yb_col, N_TOK // tm, 0, *common, f32_weights=f32_weights))
    ya_p, ya_s = ((ya[0], 0), (ya[1], 0)) if isinstance(ya, tuple) else ((ya, 0), (ya, npt))
    y_p, *bf16_weights = _mlp_call(x, 0, ya_p[0], ya_p[1], ya_col, yb, 0, yb_col, npt, 0, *common,
                                   f32_weights=f32_weights, emit=True)
    y_s = _mlp_call(x, npt, ya_s[0], ya_s[1], ya_col, yb, npt, yb_col, N_TOK // tm - npt, npt, *common,
                    bf16_weights=bf16_weights)
    return y_p, y_s


def _ctx_attn_body(q_ref, k_ref, v_ref, o_ref, ck_ref, cv_ref):
    lane = lax.broadcasted_iota(jnp.int32, (1, 2 * HEAD_DIM), 1)
    own = [lane < HEAD_DIM, lane >= HEAD_DIM]
    for h in range(N_HEADS):
        sl = slice(h * HEAD_DIM, (h + 1) * HEAD_DIM)
        ck_ref[h] = k_ref[:, sl].astype(F32)
        cv_ref[h] = v_ref[:, sl].astype(F32)
    for hp in range(N_HEADS // 2):
        sl = slice(hp * 2 * HEAD_DIM, (hp + 1) * 2 * HEAD_DIM)
        q = q_ref[:, sl] * jnp.asarray(HEAD_DIM ** -0.5, BF16)
        k = k_ref[:, sl]
        v = v_ref[:, sl]
        q2 = jnp.concatenate([jnp.where(own[0], q, jnp.zeros_like(q)),
                              jnp.where(own[1], q, jnp.zeros_like(q))], axis=0)
        s = _dot_nt(q2, k)
        m = jnp.max(s, axis=-1, keepdims=True)
        p = jnp.exp(s - m).astype(BF16)
        out = None
        for hh in range(2):
            num = _dot(p[hh * SEQ:(hh + 1) * SEQ], jnp.where(own[hh], v, jnp.ones_like(v)))
            den = jnp.where(own[hh], pltpu.roll(num, HEAD_DIM, 1), 1.0)
            o = num / den
            out = o if out is None else jnp.where(own[0], out, o)
        o_ref[:, sl] = out.astype(BF16)


def _ctx_attn(proj):
    return pl.pallas_call(
        _ctx_attn_body,
        grid=(BATCH,),
        in_specs=[
            pl.BlockSpec((SEQ, HALF), lambda b: (b, 0)),
            pl.BlockSpec((SEQ, HALF), lambda b: (b, 1)),
            pl.BlockSpec((SEQ, HALF), lambda b: (b, 2)),
        ],
        out_specs=[
            pl.BlockSpec((SEQ, HALF), lambda b: (b, 0)),
            pl.BlockSpec((None, None, N_HEADS, SEQ, HEAD_DIM), lambda b: (b, 0, 0, 0, 0)),
            pl.BlockSpec((None, None, N_HEADS, SEQ, HEAD_DIM), lambda b: (b, 0, 0, 0, 0)),
        ],
        out_shape=[
            jax.ShapeDtypeStruct((N_PROMPT_TOK, HALF), BF16),
            jax.ShapeDtypeStruct((BATCH, 1, N_HEADS, SEQ, HEAD_DIM), F32),
            jax.ShapeDtypeStruct((BATCH, 1, N_HEADS, SEQ, HEAD_DIM), F32),
        ],
        compiler_params=_cparams(("parallel",)),
        name="ctx_attn",
    )(proj, proj, proj)


NA_MASKED = -math.inf
NA_QROWS = 4
NA_KROWS = NA_QROWS + WIN_H
NA_NQ = NA_QROWS * GRID_W
NA_NK = NA_KROWS * GRID_W
NA_BLOCKS = GRID_ROWS // NA_QROWS
NA_TYPES = 3


def _na_col_table(rpb):
    qc = jnp.arange(GRID_W)[:, None]
    kc = jnp.arange(GRID_W)[None, :]
    c0 = jnp.clip(qc - WIN_W // 2, 0, GRID_W - WIN_W)
    valid = (kc >= c0) & (kc < c0 + WIN_W)
    coff = jnp.clip(kc - qc, -(WIN_W - 1), WIN_W - 1) + (WIN_W - 1)
    onehot = (coff.reshape(1, -1) == jnp.arange(2 * WIN_W - 1)[:, None]).astype(F32)
    t = jnp.dot(rpb.astype(F32).reshape(N_HEADS * (2 * WIN_H - 1), 2 * WIN_W - 1), onehot,
                precision=lax.Precision.HIGHEST)
    t = t.reshape(N_HEADS, 2 * WIN_H - 1, GRID_W, GRID_W)
    return jnp.where(valid[None, None], t, NA_MASKED)


NA_TYPE_Q0 = (0, NA_QROWS, GRID_ROWS - NA_QROWS)
NA_TYPE_KROWS = (WIN_H, NA_KROWS, WIN_H)


def _na_window_start(rb, krows):
    return min(max(NA_QROWS * rb - WIN_H // 2, 0), GRID_ROWS - krows)


def _na_body(q_ref, k_ref, v_ref, ck_ref, cv_ref, tab_ref, o_ref, bias_ref, kc_ref, vc_ref, va_ref):
    @pl.when(pl.program_id(1) == 0)
    def _():
        masked = jnp.full((GRID_W, GRID_W), NA_MASKED, F32)
        for hh in range(2):
            for ty in range(NA_TYPES):
                q0 = NA_TYPE_Q0[ty]
                ks = _na_window_start(q0 // NA_QROWS, NA_TYPE_KROWS[ty])
                for a in range(NA_QROWS):
                    r0 = min(max(q0 + a - WIN_H // 2, 0), GRID_ROWS - WIN_H)
                    for y in range(NA_TYPE_KROWS[ty]):
                        kr = ks + y
                        ok = r0 <= kr < r0 + WIN_H
                        blk = tab_ref[hh, kr - (q0 + a) + WIN_H - 1] if ok else masked
                        bias_ref[ty, hh * NA_NQ + a * GRID_W:hh * NA_NQ + (a + 1) * GRID_W,
                                 y * GRID_W:(y + 1) * GRID_W] = blk

    lane = lax.broadcasted_iota(jnp.int32, (1, 2 * HEAD_DIM), 1)
    own = [lane < HEAD_DIM, lane >= HEAD_DIM]
    kc_ref[...] = jnp.concatenate([ck_ref[0], ck_ref[1]], axis=1).astype(BF16)
    vc = jnp.concatenate([cv_ref[0], cv_ref[1]], axis=1)
    v = v_ref[...]
    for hh in range(2):
        vc_ref[hh] = jnp.where(own[hh], vc, 1.0).astype(BF16)
        va_ref[hh] = jnp.where(own[hh], v, jnp.ones_like(v))

    def block(rb, ty):
        qs = rb * NA_NQ
        nk = NA_TYPE_KROWS[ty] * GRID_W
        ks = _na_window_start(rb, NA_TYPE_KROWS[ty]) * GRID_W
        q = q_ref[pl.ds(qs, NA_NQ), :] * jnp.asarray(HEAD_DIM ** -0.5, BF16)
        q2 = jnp.concatenate([jnp.where(own[0], q, jnp.zeros_like(q)),
                              jnp.where(own[1], q, jnp.zeros_like(q))], axis=0)
        s_loc = _dot_nt(q2, k_ref[pl.ds(ks, nk), :]) + bias_ref[ty, :, 0:nk]
        s_ctx = _dot_nt(q2, kc_ref[...])
        m = jnp.maximum(jnp.max(s_loc, axis=-1, keepdims=True), jnp.max(s_ctx, axis=-1, keepdims=True))
        p_loc = jnp.exp(s_loc - m).astype(BF16)
        p_ctx = jnp.exp(s_ctx - m).astype(BF16)
        out = None
        for hh in range(2):
            rows = slice(hh * NA_NQ, (hh + 1) * NA_NQ)
            num = _dot(p_loc[rows], va_ref[hh, pl.ds(ks, nk), :]) + _dot(p_ctx[rows], vc_ref[hh])
            den = jnp.where(own[hh], pltpu.roll(num, HEAD_DIM, 1), 1.0)
            o = num / den
            out = o if out is None else jnp.where(own[0], out, o)
        o_ref[pl.ds(qs, NA_NQ), :] = out.astype(BF16)

    for rb in range(NA_BLOCKS):
        block(rb, 0 if rb == 0 else (2 if rb == NA_BLOCKS - 1 else 1))


def _na_attn(proj, cache_k, cache_v, col_table, layer_idx):
    row_blk0 = N_PROMPT_TOK // DEC_SEQ
    hp = N_HEADS // 2
    return pl.pallas_call(
        _na_body,
        grid=(hp, DEC_BATCH),
        in_specs=[
            pl.BlockSpec((DEC_SEQ, 128), lambda h, b: (row_blk0 + b, h)),
            pl.BlockSpec((DEC_SEQ, 128), lambda h, b: (row_blk0 + b, hp + h)),
            pl.BlockSpec((DEC_SEQ, 128), lambda h, b: (row_blk0 + b, 2 * hp + h)),
            pl.BlockSpec((None, None, 2, PAST_LEN, HEAD_DIM), lambda h, b: (b, layer_idx, h, 0, 0)),
            pl.BlockSpec((None, None, 2, PAST_LEN, HEAD_DIM), lambda h, b: (b, layer_idx, h, 0, 0)),
            pl.BlockSpec((2, 2 * WIN_H - 1, GRID_W, GRID_W), lambda h, b: (h, 0, 0, 0)),
        ],
        out_specs=pl.BlockSpec((DEC_SEQ, 128), lambda h, b: (b, h)),
        out_shape=jax.ShapeDtypeStruct((N_SAMPLE_TOK, HALF), BF16),
        scratch_shapes=[pltpu.VMEM((NA_TYPES, 2 * NA_NQ, NA_NK), F32),
                        pltpu.VMEM((PAST_LEN, 2 * HEAD_DIM), BF16),
                        pltpu.VMEM((2, PAST_LEN, 2 * HEAD_DIM), BF16),
                        pltpu.VMEM((2, DEC_SEQ, 2 * HEAD_DIM), BF16)],
        compiler_params=_cparams(("parallel", "arbitrary")),
        name="na_attn",
    )(proj, proj, proj, cache_k, cache_v, col_table)


N_POW = 24


PREP_PPS = 2


def _s5_prep_body(*refs):
    for n in range(PREP_PPS):
        _s5_prep_pair(PREP_PPS * pl.program_id(0) + n, *[r.at[n] for r in refs])


def _s5_prep_pair(p, vec_ref, bt_ref, c_ref, wend_ref, ktoe_ref, woutT_ref, apow_ref, pw_ref, pct_ref):
    L = S5_CHUNK
    lane_blk = lax.broadcasted_iota(jnp.int32, (S5_CH, S5_W), 1) // S5_CH
    src_blk = (lane_blk + L - p) % L
    krows = []
    for d in range(2):
        v = vec_ref[d]
        lre, lim, step = v[0:1], v[1:2], v[2:3]
        a = lre * step
        b = lim * step
        ea = jnp.exp(a)
        nr = ea * jnp.cos(b) - 1.0
        ni = ea * jnp.sin(b)
        den = lre * lre + lim * lim
        qr = (nr * lre + ni * lim) / den
        qi = (ni * lre - nr * lim) / den
        bre, bim = bt_ref[d, 0], bt_ref[d, 1]
        bqr = bre * qr - bim * qi
        bqi = bre * qi + bim * qr
        cre, cim = c_ref[d, 0], c_ref[d, 1]

        e = lax.broadcasted_iota(jnp.int32, (N_POW, 1), 0).astype(F32)
        mag = jnp.exp(e * a)
        pw_ref[0] = mag * jnp.cos(e * b)
        pw_ref[1] = mag * jnp.sin(e * b)
        mag = jnp.exp((L * e) * a)
        apow_ref[d, 0] = mag * jnp.cos((L * e) * b)
        apow_ref[d, 1] = mag * jnp.sin((L * e) * b)

        def power(k):
            return pw_ref[0, k:k + 1, :], pw_ref[1, k:k + 1, :]

        def c_block(k):
            pr, pi = power(k)
            return cre * pr - cim * pi, -(cre * pi + cim * pr)

        for i in range(L):
            rows = pl.ds(pl.multiple_of(((i + p) % L) * S5_CH, S5_CH), S5_CH)
            pr, pi = power(L - 1 - i if d == 0 else i)
            wend_ref[rows, d * 256:d * 256 + 128] = (bqr * pr - bqi * pi).astype(BF16)
            wend_ref[rows, d * 256 + 128:d * 256 + 256] = (bqr * pi + bqi * pr).astype(BF16)
            mre, mim = c_block(i + 1 if d == 0 else L - i)
            woutT_ref[rows, d * 256:d * 256 + 128] = mre.astype(BF16)
            woutT_ref[rows, d * 256 + 128:d * 256 + 256] = mim.astype(BF16)
            lag_rows = slice(i * S5_CH, (i + 1) * S5_CH)
            mre, mim = c_block(i if d == 0 else L - 1 - i)
            pct_ref[lag_rows, 0:128] = mre
            pct_ref[lag_rows, 128:256] = mim

        bqcat = jnp.concatenate([bqr, bqi], axis=1)
        krows.append(_dot_nt(bqcat, pct_ref[...], precision=lax.Precision.HIGHEST))

    base_f = pltpu.roll(krows[0], p * S5_CH, 1)
    base_b = pltpu.roll(krows[1], p * S5_CH, 1)
    for i in range(L):
        rows = pl.ds(pl.multiple_of(((i + p) % L) * S5_CH, S5_CH), S5_CH)
        blk_f = base_f if i == 0 else pltpu.roll(base_f, i * S5_CH, 1)
        blk_b = base_b if (i + 1) % L == 0 else pltpu.roll(base_b, ((i + 1) % L) * S5_CH, 1)
        blk = jnp.where(src_blk >= i, blk_f, 0.0) + jnp.where(src_blk <= i, blk_b, 0.0)
        ktoe_ref[rows, :] = blk.astype(BF16)


def _s5_prep(lam_re, lam_im, log_step, b_re, b_im, c_re, c_im):
    def pair_vec(a):
        return a.astype(F32).reshape(2, S5_PAIRS, 1, S5_P).transpose(1, 0, 2, 3)

    step = jnp.broadcast_to(jnp.exp(log_step.astype(F32))[:, :, None], (2, N_SSM_GROUPS, SSM_STATE))
    vec = jnp.concatenate([pair_vec(lam_re), pair_vec(lam_im), pair_vec(step),
                           jnp.zeros((S5_PAIRS, 2, 5, S5_P), F32)], axis=2)

    def block_diag_ch_by_state(m):
        m = m.astype(F32).reshape(2, S5_PAIRS, 2, SSM_GROUP, SSM_STATE)
        eye = jnp.eye(2, dtype=F32)
        out = m[:, :, :, :, None, :] * eye[None, None, :, None, :, None]
        return out.reshape(2, S5_PAIRS, S5_CH, S5_P).transpose(1, 0, 2, 3)

    bt = jnp.stack([block_diag_ch_by_state(b_re.transpose(0, 1, 3, 2)),
                    block_diag_ch_by_state(b_im.transpose(0, 1, 3, 2))], axis=2)
    cc = jnp.stack([block_diag_ch_by_state(c_re), block_diag_ch_by_state(c_im)], axis=2)

    mat = jax.ShapeDtypeStruct((S5_PAIRS, S5_W, S5_W), BF16)
    mat_spec = pl.BlockSpec((PREP_PPS, S5_W, S5_W), lambda p: (p, 0, 0))
    return pl.pallas_call(
        _s5_prep_body,
        grid=(S5_PAIRS // PREP_PPS,),
        in_specs=[
            pl.BlockSpec((PREP_PPS, 2, 8, S5_P), lambda p: (p, 0, 0, 0)),
            pl.BlockSpec((PREP_PPS, 2, 2, S5_CH, S5_P), lambda p: (p, 0, 0, 0, 0)),
            pl.BlockSpec((PREP_PPS, 2, 2, S5_CH, S5_P), lambda p: (p, 0, 0, 0, 0)),
        ],
        out_specs=[mat_spec, mat_spec, mat_spec,
                   pl.BlockSpec((PREP_PPS, 2, 2, N_POW, S5_P), lambda p: (p, 0, 0, 0, 0))],
        out_shape=[mat, mat, mat, jax.ShapeDtypeStruct((S5_PAIRS, 2, 2, N_POW, S5_P), F32)],
        scratch_shapes=[pltpu.VMEM((PREP_PPS, 2, N_POW, S5_P), F32), pltpu.VMEM((PREP_PPS, S5_W, 2 * S5_P), F32)],
        compiler_params=_cparams(("parallel",)),
        name="s5_prep",
    )(vec, bt, cc)


S5_SEG_ROWS = 256
S5_N_SEG = N_CHUNK_ROWS // S5_SEG_ROWS
S5_SUB = S5_SEG_ROWS // S5_CHUNK
S5_SUB_PER_SEQ = (DEC_SEQ // S5_CHUNK) // S5_CHUNK


S5_PPS = 4


def _s5_body(tu_ref, s0_ref, wend_ref, ktoe_ref, woutT_ref, apow_ref, ty_ref, fin_ref, z_ref, xs_ref, g_ref):
    seg = pl.program_id(0)
    L = S5_CHUNK
    lane_blk = lax.broadcasted_iota(jnp.int32, (S5_SEG_ROWS, 128), 1) // S5_CH
    pairs = [S5_PPS * pl.program_id(1) + n for n in range(S5_PPS)]

    s_rows, ends = [], []
    for n, p in enumerate(pairs):
        tiles = []
        for t in range(LANE_TILES):
            tile = None
            for k in range(128 // S5_CH):
                j = (LANE_TILES * t + k + L - p) % L
                blk = tu_ref[j, t]
                tile = blk if tile is None else jnp.where(lane_blk == k, blk, tile)
            tiles.append(tile)
        s = jnp.concatenate(tiles, axis=1)
        s_rows.append(s)
        z = _dot(s, wend_ref[n])
        for q in range(LANE_TILES):
            z_ref[n, q] = z[:, q * 128:(q + 1) * 128]

        pair_ends = []
        for d in range(2):
            are, aim = apow_ref[n, d, 0, 1:2, :], apow_ref[n, d, 1, 1:2, :]
            xr = jnp.zeros((S5_SUB, S5_P), F32)
            xi = jnp.zeros((S5_SUB, S5_P), F32)
            for c in (range(L) if d == 0 else reversed(range(L))):
                rows = pl.ds(c, S5_SUB, stride=L)
                xs_ref[n, 2 * d, rows, :] = xr
                xs_ref[n, 2 * d + 1, rows, :] = xi
                zr = z_ref[n, 2 * d, rows, :]
                zi = z_ref[n, 2 * d + 1, rows, :]
                xr, xi = are * xr - aim * xi + zr, are * xi + aim * xr + zi
            pair_ends.append((xr, xi))
            fin_ref[n, d, 0] = xr
            fin_ref[n, d, 1] = xi
        ends.append(pair_ends)

    @pl.when(seg > 0)
    def _():
        for n in range(S5_PPS):
            for d in range(2):
                er, ei = ends[n][d]
                a16r, a16i = apow_ref[n, d, 0, L:L + 1, :], apow_ref[n, d, 1, L:L + 1, :]
                for bb in range(S5_SUB // S5_SUB_PER_SEQ):
                    gr = s0_ref[n, bb, 2 * d:2 * d + 1, :]
                    gi = s0_ref[n, bb, 2 * d + 1:2 * d + 2, :]
                    order = range(S5_SUB_PER_SEQ) if d == 0 else reversed(range(S5_SUB_PER_SEQ))
                    prev = None
                    for sub in order:
                        row = bb * S5_SUB_PER_SEQ + sub
                        if prev is not None:
                            gr, gi = (a16r * gr - a16i * gi + er[prev:prev + 1, :],
                                      a16r * gi + a16i * gr + ei[prev:prev + 1, :])
                        g_ref[n, 0, row:row + 1, :] = gr
                        g_ref[n, 1, row:row + 1, :] = gi
                        prev = row
                if d == 0:
                    pr, pi = apow_ref[n, d, 0, 0:L, :], apow_ref[n, d, 1, 0:L, :]
                else:
                    pr = jnp.concatenate([apow_ref[n, d, 0, k:k + 1, :] for k in reversed(range(L))], axis=0)
                    pi = jnp.concatenate([apow_ref[n, d, 1, k:k + 1, :] for k in reversed(range(L))], axis=0)
                for sub in range(S5_SUB):
                    rows = slice(sub * L, (sub + 1) * L)
                    gr = g_ref[n, 0, sub:sub + 1, :]
                    gi = g_ref[n, 1, sub:sub + 1, :]
                    xs_ref[n, 2 * d, rows, :] = xs_ref[n, 2 * d, rows, :] + (pr * gr - pi * gi)
                    xs_ref[n, 2 * d + 1, rows, :] = xs_ref[n, 2 * d + 1, rows, :] + (pr * gi + pi * gr)

    @pl.when(pl.program_id(1) == 0)
    def _():
        ty_ref[...] = jnp.zeros_like(ty_ref)

    for n, p in enumerate(pairs):
        xs = jnp.concatenate([xs_ref[n, q].astype(BF16) for q in range(LANE_TILES)], axis=1)
        y = _dot(s_rows[n], ktoe_ref[n]) + _dot_nt(xs, woutT_ref[n])
        for t in range(LANE_TILES):
            yt = y[:, t * 128:(t + 1) * 128]
            for k in range(128 // S5_CH):
                j = (LANE_TILES * t + k + L - p) % L
                ty_ref[j, t] = jnp.where(lane_blk == k, yt, ty_ref[j, t])


def _s5_scan(tu, s0, wend, ktoe, woutT, apow):
    mat_spec = pl.BlockSpec((S5_PPS, S5_W, S5_W), lambda g, p: (p, 0, 0))
    seqs = S5_SUB // S5_SUB_PER_SEQ
    return pl.pallas_call(
        _s5_body,
        grid=(S5_N_SEG, S5_PAIRS // S5_PPS),
        in_specs=[
            pl.BlockSpec((S5_CHUNK, LANE_TILES, S5_SEG_ROWS, 128), lambda g, p: (0, 0, g, 0)),
            pl.BlockSpec((S5_PPS, seqs, 4, S5_P), lambda g, p: (p, jnp.maximum(g - 1, 0), 0, 0)),
            mat_spec, mat_spec, mat_spec,
            pl.BlockSpec((S5_PPS, 2, 2, N_POW, S5_P), lambda g, p: (p, 0, 0, 0, 0)),
        ],
        out_specs=[
            pl.BlockSpec((S5_CHUNK, LANE_TILES, S5_SEG_ROWS, 128), lambda g, p: (0, 0, g, 0)),
            pl.BlockSpec((S5_PPS, 2, 2, S5_SUB, S5_P), lambda g, p: (p, 0, 0, g, 0)),
        ],
        out_shape=[
            jax.ShapeDtypeStruct((S5_CHUNK, LANE_TILES, N_CHUNK_ROWS, 128), F32),
            jax.ShapeDtypeStruct((S5_PAIRS, 2, 2, S5_N_SEG * S5_SUB, S5_P), F32),
        ],
        scratch_shapes=[pltpu.VMEM((S5_PPS, LANE_TILES, S5_SEG_ROWS, 128), F32),
                        pltpu.VMEM((S5_PPS, LANE_TILES, S5_SEG_ROWS, 128), F32),
                        pltpu.VMEM((S5_PPS, 2, S5_SUB, S5_P), F32)],
        compiler_params=_cparams(("parallel", "arbitrary")),
        name="s5_scan",
    )(tu, s0, wend, ktoe, woutT, apow)


TM_GLU = 2048


def _glu_body(tu_ref, ty_ref, d_ref, gw_ref, gb_ref, o_ref, out_ref):
    rows = TM_GLU // S5_CHUNK
    d = d_ref[...]
    pre = []
    for j in range(S5_CHUNK):
        uj = jnp.concatenate([tu_ref[j, t].astype(F32) for t in range(LANE_TILES)], axis=1)
        yj = jnp.concatenate([ty_ref[j, t] for t in range(LANE_TILES)], axis=1)
        dj = pltpu.roll(d, j * S5_CH, 1) if j else d
        yj = dj * uj + yj
        back = ((S5_CHUNK - j) % S5_CHUNK) * S5_CH
        pre.append(pltpu.roll(yj, back, 1) if back else yj)
    y = jnp.concatenate(pre, axis=0)
    cdf = 0.5 * (1.0 + jnp.tanh(math.sqrt(2.0 / math.pi) * (y + 0.044715 * (y * y * y))))
    y = y * cdf
    gate = jax.nn.sigmoid(_dot(y.astype(BF16), gw_ref[...]) + gb_ref[...])
    out = y * gate
    for j in range(S5_CHUNK):
        oj = out[j * rows:(j + 1) * rows]
        for t in range(LANE_TILES):
            out_ref[t, pl.ds(j, rows, stride=S5_CHUNK), :] = oj[:, t * 128:(t + 1) * 128]
    o_ref[...] = jnp.concatenate([out_ref[t] for t in range(LANE_TILES)], axis=1).astype(BF16)


def _glu(tu, ty, ssm_d, glu_w, glu_b):
    tm = TM_GLU
    rows = tm // S5_CHUNK
    lay_spec = pl.BlockSpec((S5_CHUNK, LANE_TILES, rows, 128), lambda i: (0, 0, i, 0))
    return pl.pallas_call(
        _glu_body,
        grid=(N_TOK // tm,),
        in_specs=[
            lay_spec, lay_spec,
            pl.BlockSpec((1, HALF), lambda i: (0, 0)),
            pl.BlockSpec((HALF, HALF), lambda i: (0, 0)),
            pl.BlockSpec((1, HALF), lambda i: (0, 0)),
        ],
        out_specs=pl.BlockSpec((tm, HALF), lambda i: (i, 0)),
        out_shape=jax.ShapeDtypeStruct((N_TOK, HALF), BF16),
        scratch_shapes=[pltpu.VMEM((LANE_TILES, tm, 128), F32)],
        compiler_params=_cparams(("parallel",)),
        name="s5_glu",
    )(tu, ty, ssm_d.reshape(1, HALF), glu_w.astype(BF16), glu_b.reshape(1, HALF))


def kernel(x_prompt, x_sample, c, cache_k, cache_v, state_s5_re, state_s5_im, c_ctx, norm1_g, norm2_g, ada_w, ada_b, mlp_w1, mlp_w2, ab_w_in, pool_w, pool_scale, conv_w, ab_w_out, cd_w_in, na_rpb, ssm_lambda_re, ssm_lambda_im, ssm_log_step, ssm_b_re, ssm_b_im, ssm_c_re, ssm_c_im, ssm_d, glu_w, glu_b, cd_w_out, final_g):
    depth = ada_w.shape[0]
    xs = (x_prompt.reshape(N_PROMPT_TOK, D_MODEL), x_sample.reshape(N_SAMPLE_TOK, D_MODEL))
    cond = jnp.concatenate([c_ctx[None, :], c, jnp.zeros((N_COND - 1 - DEC_BATCH, D_MODEL), F32)], axis=0)
    mods = _adaln(cond, ada_w, ada_b).reshape(depth, N_COND, 6, D_MODEL)

    new_k, new_v, new_re, new_im = [], [], [], []
    for layer in range(depth):
        i = layer // 2
        final = layer == depth - 1
        if layer % 2 == 0:
            y, x = _even_layer_mix(xs, mods, layer, norm1_g[layer], ab_w_in, i,
                                   pool_w[i], pool_scale[i], conv_w[i])
            ya, ya_col, yb, yb_col = y, 0, y, 1
            w_out_stack = ab_w_out
        else:
            x = xs[0]
            qkv, tu = _inproj_odd(x, mods, layer, norm1_g[layer], cd_w_in, i)
            o_p, ck, cv = _ctx_attn(qkv)
            new_k.append(ck)
            new_v.append(cv)
            o_s = _na_attn(qkv, cache_k, cache_v, _na_col_table(na_rpb[i]), i)
            attn = (o_p, o_s)

            wend, ktoe, woutT, apow = _s5_prep(ssm_lambda_re[i], ssm_lambda_im[i], ssm_log_step[i],
                                               ssm_b_re[i], ssm_b_im[i], ssm_c_re[i], ssm_c_im[i])

            def pair_state(s):
                return s.astype(F32).reshape(DEC_BATCH, 2, S5_PAIRS, S5_P).transpose(2, 0, 1, 3)

            s0 = jnp.stack([pair_state(state_s5_re[:, i]), pair_state(state_s5_im[:, i])], axis=3)
            s0 = s0.reshape(S5_PAIRS, DEC_BATCH, 4, S5_P)
            ty, fin = _s5_scan(tu, s0, wend, ktoe, woutT, apow)
            fin = fin[:, :, :, :BATCH]
            new_re.append(fin[:, :, 0].transpose(2, 1, 0, 3).reshape(BATCH, 2, N_SSM_GROUPS, SSM_STATE))
            new_im.append(fin[:, :, 1].transpose(2, 1, 0, 3).reshape(BATCH, 2, N_SSM_GROUPS, SSM_STATE))
            d_out = _glu(tu, ty, ssm_d[i], glu_w[i], glu_b[i])
            ya, ya_col, yb, yb_col = attn, 0, d_out, 0
            w_out_stack = cd_w_out
        xs = _mlp(x, ya, ya_col, yb, yb_col, w_out_stack, i, mods, layer, norm2_g[layer], final_g,
                  mlp_w1, mlp_w2, final)

    y_prompt = xs[0].reshape(BATCH, SEQ, D_MODEL)
    y_sample = xs[1].reshape(DEC_BATCH, DEC_SEQ, D_MODEL)
    return (y_prompt, y_sample, jnp.concatenate(new_k, axis=1), jnp.concatenate(new_v, axis=1),
            jnp.stack(new_re, axis=1), jnp.stack(new_im, axis=1))
```
